```python
import jax, jax.numpy as jnp
from jax import lax
import numpy as np

D_MODEL = 1024
BATCH = 8
SEQ = 8192
DEPTH = 1

RNN_WIDTH = D_MODEL
RNN_BLOCKS = 8
RNN_BLOCK_W = RNN_WIDTH // RNN_BLOCKS
CONV_WIDTH = 4
LRU_C = 8.0
MLA_HEADS = 8
QK_NOPE = 128
QK_ROPE = 64
V_HEAD = D_MODEL // MLA_HEADS
Q_LORA = D_MODEL // 4
KV_LORA = D_MODEL // 4
ROPE_THETA = 10000.0
Q_BLOCK = 128
D_FF = 4 * D_MODEL
EPS = 1e-6

IN_SIZES = (RNN_WIDTH, RNN_WIDTH, Q_LORA, KV_LORA, QK_ROPE, D_MODEL, D_MODEL)
IN_TOTAL = sum(IN_SIZES)

kernel_name = "hybrid_rglru_mla_sqrelu"


def rmsnorm(x, g):
    xf = x.astype(jnp.float32)
    y = xf * lax.rsqrt(jnp.mean(xf * xf, axis=-1, keepdims=True) + EPS) * g.astype(jnp.float32)
    return y.astype(x.dtype)


def split_cols(z, sizes):
    offs = np.cumsum(sizes)[:-1].tolist()
    return jnp.split(z, offs, axis=-1)


def causal_depthwise_conv(x, w, b):
    c = x.shape[-1]
    y = lax.conv_general_dilated(
        x, w[:, None, :].astype(x.dtype), window_strides=(1,), padding=[(CONV_WIDTH - 1, 0)],
        dimension_numbers=("NWC", "WIO", "NWC"), feature_group_count=c)
    return y + b


def block_diag_linear(x, w, b):
    bsz, s, c = x.shape
    xb = x.reshape(bsz, s, RNN_BLOCKS, RNN_BLOCK_W)
    y = jnp.einsum("bsnc,ncd->bsnd", xb, w) + b
    return y.reshape(bsz, s, c)


def rg_lru(xa, wa, ba, wx, bx, lam):
    xf = xa.astype(jnp.float32)
    r = jax.nn.sigmoid(block_diag_linear(xa, wa, ba).astype(jnp.float32))
    i = jax.nn.sigmoid(block_diag_linear(xa, wx, bx).astype(jnp.float32))
    log_a = -LRU_C * r * jax.nn.softplus(-lam.astype(jnp.float32))
    a = jnp.exp(log_a)
    b = jnp.sqrt(-jnp.expm1(2.0 * log_a)) * (i * xf)

    def combine(e1, e2):
        a1, b1 = e1
        a2, b2 = e2
        return a1 * a2, a2 * b1 + b2

    _, h = lax.associative_scan(combine, (a, b), axis=1)
    return h


def rope_tables(seq):
    pos = jnp.arange(seq, dtype=jnp.float32)
    inv_freq = 1.0 / (ROPE_THETA ** (jnp.arange(0, QK_ROPE, 2, dtype=jnp.float32) / QK_ROPE))
    ang = pos[:, None] * inv_freq[None, :]
    cos = jnp.cos(ang)
    sin = jnp.sin(ang)
    return jnp.concatenate([cos, cos], -1), jnp.concatenate([sin, sin], -1)


def apply_rope(x, cos, sin):
    half = x.shape[-1] // 2
    rot = jnp.concatenate([-x[..., half:], x[..., :half]], axis=-1)
    return (x.astype(jnp.float32) * cos + rot.astype(jnp.float32) * sin).astype(x.dtype)


def mla(c_q, c_kv, k_rope, q_norm, w_uq, kv_norm, w_ukv):
    bsz, s, _ = c_q.shape
    q = (rmsnorm(c_q, q_norm) @ w_uq).reshape(bsz, s, MLA_HEADS, QK_NOPE + QK_ROPE)
    kv = (rmsnorm(c_kv, kv_norm) @ w_ukv).reshape(bsz, s, MLA_HEADS, QK_NOPE + V_HEAD)
    q_nope, q_rope = q[..., :QK_NOPE], q[..., QK_NOPE:]
    k_nope, v = kv[..., :QK_NOPE], kv[..., QK_NOPE:]
    cos, sin = rope_tables(s)
    q_rope = apply_rope(q_rope, cos[None, :, None, :], sin[None, :, None, :])
    k_rope = apply_rope(k_rope, cos[None], sin[None])
    scale = (QK_NOPE + QK_ROPE) ** -0.5
    nb = s // Q_BLOCK
    qn_b = q_nope.reshape(bsz, nb, Q_BLOCK, MLA_HEADS, QK_NOPE).transpose(1, 0, 2, 3, 4)
    qr_b = q_rope.reshape(bsz, nb, Q_BLOCK, MLA_HEADS, QK_ROPE).transpose(1, 0, 2, 3, 4)
    kpos = jnp.arange(s)
    qpos_b = kpos.reshape(nb, Q_BLOCK)
    neg = jnp.finfo(jnp.float32).min

    def one_block(args):
        qn, qr, qpos = args
        sc = jnp.einsum("bqhd,bkhd->bhqk", qn, k_nope, preferred_element_type=jnp.float32)
        sc = sc + jnp.einsum("bqhr,bkr->bhqk", qr, k_rope, preferred_element_type=jnp.float32)
        mask = qpos[:, None] >= kpos[None, :]
        sc = jnp.where(mask[None, None], sc * scale, neg)
        p = jax.nn.softmax(sc, axis=-1)
        return jnp.einsum("bhqk,bkhd->bqhd", p.astype(v.dtype), v)

    o = lax.map(one_block, (qn_b, qr_b, qpos_b))
    return o.transpose(1, 0, 2, 3, 4).reshape(bsz, s, MLA_HEADS * V_HEAD)


def _fwd_setup_inputs(seed: int = 0) -> dict:
    key = jax.random.key(seed)
    ks = jax.random.split(key, 20)
    f32 = jnp.float32
    nrm = lambda k, shape, fan: jax.random.normal(k, shape, f32) * fan ** -0.5
    gain = lambda k, n: 1.0 + 0.02 * jax.random.normal(k, (n,), f32)
    u = jax.random.uniform(ks[9], (RNN_WIDTH,), f32, 0.9, 0.999)
    a0 = u ** (1.0 / LRU_C)
    lru_lambda = jnp.log(a0) - jnp.log1p(-a0)
    return {
        "x": jax.random.normal(ks[0], (BATCH, SEQ, D_MODEL), f32),
        "norm_mix": gain(ks[1], D_MODEL),
        "w_in": nrm(ks[2], (D_MODEL, IN_TOTAL), D_MODEL),
        "conv_w": nrm(ks[3], (CONV_WIDTH, RNN_WIDTH), CONV_WIDTH),
        "conv_b": 0.01 * jax.random.normal(ks[4], (RNN_WIDTH,), f32),
        "lru_wa": nrm(ks[5], (RNN_BLOCKS, RNN_BLOCK_W, RNN_BLOCK_W), RNN_BLOCK_W),
        "lru_ba": 0.01 * jax.random.normal(ks[6], (RNN_BLOCKS, RNN_BLOCK_W), f32),
        "lru_wx": nrm(ks[7], (RNN_BLOCKS, RNN_BLOCK_W, RNN_BLOCK_W), RNN_BLOCK_W),
        "lru_bx": 0.01 * jax.random.normal(ks[8], (RNN_BLOCKS, RNN_BLOCK_W), f32),
        "lru_lambda": lru_lambda,
        "q_norm": gain(ks[10], Q_LORA),
        "w_uq": nrm(ks[11], (Q_LORA, MLA_HEADS * (QK_NOPE + QK_ROPE)), Q_LORA),
        "kv_norm": gain(ks[12], KV_LORA),
        "w_ukv": nrm(ks[13], (KV_LORA, MLA_HEADS * (QK_NOPE + V_HEAD)), KV_LORA),
        "w_out": nrm(ks[14], (D_MODEL, D_MODEL), D_MODEL),
        "norm_mlp": gain(ks[15], D_MODEL),
        "w_up": nrm(ks[16], (D_MODEL, D_FF), D_MODEL),
        "w_down": nrm(ks[17], (D_FF, D_MODEL), D_FF),
        "norm_final": gain(ks[18], D_MODEL),
    }


def _fwd_reference(x, norm_mix, w_in, conv_w, conv_b, lru_wa, lru_ba, lru_wx, lru_bx, lru_lambda,
              q_norm, w_uq, kv_norm, w_ukv, w_out, norm_mlp, w_up, w_down, norm_final):
    h = x
    for _ in range(DEPTH):
        z = rmsnorm(h, norm_mix) @ w_in
        rnn_x, rnn_gate, c_q, c_kv, k_rope, gate_a, gate_b = split_cols(z, IN_SIZES)
        xa = causal_depthwise_conv(rnn_x, conv_w, conv_b)
        hr = rg_lru(xa, lru_wa, lru_ba, lru_wx, lru_bx, lru_lambda)
        y_a = hr * jax.nn.gelu(rnn_gate.astype(jnp.float32))
        y_b = mla(c_q, c_kv, k_rope, q_norm, w_uq, kv_norm, w_ukv).astype(jnp.float32)
        merged = (jax.nn.sigmoid(gate_a.astype(jnp.float32)) * y_a
                  + jax.nn.sigmoid(gate_b.astype(jnp.float32)) * y_b).astype(h.dtype)
        h = h + merged @ w_out
        u = rmsnorm(h, norm_mlp) @ w_up
        h = h + jnp.square(jax.nn.relu(u)) @ w_down
    return rmsnorm(h, norm_final)


import jax as _jax
import jax.numpy as _jnp

TWIN_FORMAT = 'train_step'
FWD_PARAMS = ['x', 'norm_mix', 'w_in', 'conv_w', 'conv_b', 'lru_wa', 'lru_ba', 'lru_wx', 'lru_bx', 'lru_lambda', 'q_norm', 'w_uq', 'kv_norm', 'w_ukv', 'w_out', 'norm_mlp', 'w_up', 'w_down', 'norm_final']
TWIN_WEIGHTS = ['norm_mix', 'w_in', 'conv_w', 'conv_b', 'lru_wa', 'lru_ba', 'lru_wx', 'lru_bx', 'lru_lambda', 'q_norm', 'w_uq', 'kv_norm', 'w_ukv', 'w_out', 'norm_mlp', 'w_up', 'w_down', 'norm_final']
TWIN_DIFF_INPUT = 'x'
TWIN_INPUTS = ['x', 'norm_mix', 'w_in', 'conv_w', 'conv_b', 'lru_wa', 'lru_ba', 'lru_wx', 'lru_bx', 'lru_lambda', 'q_norm', 'w_uq', 'kv_norm', 'w_ukv', 'w_out', 'norm_mlp', 'w_up', 'w_down', 'norm_final', 'loss_target', 'm_norm_mix', 'm_w_in', 'm_conv_w', 'm_conv_b', 'm_lru_wa', 'm_lru_ba', 'm_lru_wx', 'm_lru_bx', 'm_lru_lambda', 'm_q_norm', 'm_w_uq', 'm_kv_norm', 'm_w_ukv', 'm_w_out', 'm_norm_mlp', 'm_w_up', 'm_w_down', 'm_norm_final', 'v_norm_mix', 'v_w_in', 'v_conv_w', 'v_conv_b', 'v_lru_wa', 'v_lru_ba', 'v_lru_wx', 'v_lru_bx', 'v_lru_lambda', 'v_q_norm', 'v_w_uq', 'v_kv_norm', 'v_w_ukv', 'v_w_out', 'v_norm_mlp', 'v_w_up', 'v_w_down', 'v_norm_final']
TWIN_OUTPUTS = ['loss', 'grad_x', 'grad_norm_mix', 'grad_w_in', 'grad_conv_w', 'grad_conv_b', 'grad_lru_wa', 'grad_lru_ba', 'grad_lru_wx', 'grad_lru_bx', 'grad_lru_lambda', 'grad_q_norm', 'grad_w_uq', 'grad_kv_norm', 'grad_w_ukv', 'grad_w_out', 'grad_norm_mlp', 'grad_w_up', 'grad_w_down', 'grad_norm_final', 'delta_norm_mix', 'delta_w_in', 'delta_conv_w', 'delta_conv_b', 'delta_lru_wa', 'delta_lru_ba', 'delta_lru_wx', 'delta_lru_bx', 'delta_lru_lambda', 'delta_q_norm', 'delta_w_uq', 'delta_kv_norm', 'delta_w_ukv', 'delta_w_out', 'delta_norm_mlp', 'delta_w_up', 'delta_w_down', 'delta_norm_final', 'new_m_norm_mix', 'new_m_w_in', 'new_m_conv_w', 'new_m_conv_b', 'new_m_lru_wa', 'new_m_lru_ba', 'new_m_lru_wx', 'new_m_lru_bx', 'new_m_lru_lambda', 'new_m_q_norm', 'new_m_w_uq', 'new_m_kv_norm', 'new_m_w_ukv', 'new_m_w_out', 'new_m_norm_mlp', 'new_m_w_up', 'new_m_w_down', 'new_m_norm_final', 'new_v_norm_mix', 'new_v_w_in', 'new_v_conv_w', 'new_v_conv_b', 'new_v_lru_wa', 'new_v_lru_ba', 'new_v_lru_wx', 'new_v_lru_bx', 'new_v_lru_lambda', 'new_v_q_norm', 'new_v_w_uq', 'new_v_kv_norm', 'new_v_w_ukv', 'new_v_w_out', 'new_v_norm_mlp', 'new_v_w_up', 'new_v_w_down', 'new_v_norm_final']
TWIN_LEAF_KINDS = {'loss': 'loss', 'grad_x': 'grad_x', 'grad_norm_mix': 'grad_w', 'grad_w_in': 'grad_w', 'grad_conv_w': 'grad_w', 'grad_conv_b': 'grad_w', 'grad_lru_wa': 'grad_w', 'grad_lru_ba': 'grad_w', 'grad_lru_wx': 'grad_w', 'grad_lru_bx': 'grad_w', 'grad_lru_lambda': 'grad_w', 'grad_q_norm': 'grad_w', 'grad_w_uq': 'grad_w', 'grad_kv_norm': 'grad_w', 'grad_w_ukv': 'grad_w', 'grad_w_out': 'grad_w', 'grad_norm_mlp': 'grad_w', 'grad_w_up': 'grad_w', 'grad_w_down': 'grad_w', 'grad_norm_final': 'grad_w', 'delta_norm_mix': 'delta_w', 'delta_w_in': 'delta_w', 'delta_conv_w': 'delta_w', 'delta_conv_b': 'delta_w', 'delta_lru_wa': 'delta_w', 'delta_lru_ba': 'delta_w', 'delta_lru_wx': 'delta_w', 'delta_lru_bx': 'delta_w', 'delta_lru_lambda': 'delta_w', 'delta_q_norm': 'delta_w', 'delta_w_uq': 'delta_w', 'delta_kv_norm': 'delta_w', 'delta_w_ukv': 'delta_w', 'delta_w_out': 'delta_w', 'delta_norm_mlp': 'delta_w', 'delta_w_up': 'delta_w', 'delta_w_down': 'delta_w', 'delta_norm_final': 'delta_w', 'new_m_norm_mix': 'new_m', 'new_m_w_in': 'new_m', 'new_m_conv_w': 'new_m', 'new_m_conv_b': 'new_m', 'new_m_lru_wa': 'new_m', 'new_m_lru_ba': 'new_m', 'new_m_lru_wx': 'new_m', 'new_m_lru_bx': 'new_m', 'new_m_lru_lambda': 'new_m', 'new_m_q_norm': 'new_m', 'new_m_w_uq': 'new_m', 'new_m_kv_norm': 'new_m', 'new_m_w_ukv': 'new_m', 'new_m_w_out': 'new_m', 'new_m_norm_mlp': 'new_m', 'new_m_w_up': 'new_m', 'new_m_w_down': 'new_m', 'new_m_norm_final': 'new_m', 'new_v_norm_mix': 'new_v', 'new_v_w_in': 'new_v', 'new_v_conv_w': 'new_v', 'new_v_conv_b': 'new_v', 'new_v_lru_wa': 'new_v', 'new_v_lru_ba': 'new_v', 'new_v_lru_wx': 'new_v', 'new_v_lru_bx': 'new_v', 'new_v_lru_lambda': 'new_v', 'new_v_q_norm': 'new_v', 'new_v_w_uq': 'new_v', 'new_v_kv_norm': 'new_v', 'new_v_w_ukv': 'new_v', 'new_v_w_out': 'new_v', 'new_v_norm_mlp': 'new_v', 'new_v_w_up': 'new_v', 'new_v_w_down': 'new_v', 'new_v_norm_final': 'new_v'}


def _forward(args):
    return _fwd_reference(*[args[k] for k in FWD_PARAMS])


def _output_shape():
    def fwd():
        inp = _fwd_setup_inputs(0)
        return _fwd_reference(*[inp[k] for k in FWD_PARAMS])
    out = _jax.eval_shape(fwd)
    return out.shape, out.dtype

N_MICROBATCH = 1
ADAM_LR = 0.001
ADAM_B1 = 0.9
ADAM_B2 = 0.999
ADAM_EPS = 1e-08
ADAM_WD = 0.01
ADAM_STEP = 10
PER_EXAMPLE_BATCH_AXIS = {'x': 0, 'loss_target': 0}
SHARED_INPUTS = []
_WEIGHT_DTYPES = {'norm_mix': _jnp.float32, 'w_in': _jnp.float32, 'conv_w': _jnp.float32, 'conv_b': _jnp.float32, 'lru_wa': _jnp.float32, 'lru_ba': _jnp.float32, 'lru_wx': _jnp.float32, 'lru_bx': _jnp.float32, 'lru_lambda': _jnp.float32, 'q_norm': _jnp.float32, 'w_uq': _jnp.float32, 'kv_norm': _jnp.float32, 'w_ukv': _jnp.float32, 'w_out': _jnp.float32, 'norm_mlp': _jnp.float32, 'w_up': _jnp.float32, 'w_down': _jnp.float32, 'norm_final': _jnp.float32}
MOMENT_SCALE = {'norm_mix': 9.201296e-02, 'w_in': 4.229192e-02, 'conv_w': 6.696471e-02, 'conv_b': 7.435650e-01, 'lru_wa': 2.090116e-02, 'lru_ba': 1.645004e-02, 'lru_wx': 3.695457e-02, 'lru_bx': 2.204000e-02, 'lru_lambda': 3.565221e-02, 'q_norm': 5.353270e-02, 'w_uq': 2.154766e-02, 'kv_norm': 7.631209e-02, 'w_ukv': 2.721955e-02, 'w_out': 5.902016e-02, 'norm_mlp': 2.222424e-01, 'w_up': 1.115997e-01, 'w_down': 2.357444e-01, 'norm_final': 6.464146e+01}


def _to_microbatches(a, axis):
    t = _jnp.moveaxis(a, axis, 0)
    t = t.reshape((N_MICROBATCH, t.shape[0] // N_MICROBATCH) + t.shape[1:])
    return _jnp.moveaxis(t, 1, axis + 1)


def setup_inputs(seed: int = 0) -> dict:
    inp = _fwd_setup_inputs(seed)
    key = _jax.random.fold_in(_jax.random.key(seed), 7919)
    shape, _ = _output_shape()
    out = dict(inp)
    out["loss_target"] = _jax.random.normal(_jax.random.fold_in(key, 0), shape, _jnp.float32)
    for i, name in enumerate(TWIN_WEIGHTS):
        w = inp[name].astype(_jnp.float32)
        if MOMENT_SCALE is None:
            s = _jnp.sqrt(_jnp.mean(_jnp.square(w)) + 1e-30)
        else:
            s = MOMENT_SCALE[name]
        km, kv = _jax.random.split(_jax.random.fold_in(key, i + 1))
        out[name] = w
        out["m_" + name] = s * _jax.random.normal(km, w.shape, _jnp.float32)
        out["v_" + name] = (s * s) * _jax.random.uniform(kv, w.shape, _jnp.float32, 0.5, 1.5)
    if N_MICROBATCH > 1:
        for name, axis in PER_EXAMPLE_BATCH_AXIS.items():
            out[name] = _to_microbatches(out[name], axis)
    return {'x': out['x'], 'norm_mix': out['norm_mix'], 'w_in': out['w_in'], 'conv_w': out['conv_w'], 'conv_b': out['conv_b'], 'lru_wa': out['lru_wa'], 'lru_ba': out['lru_ba'], 'lru_wx': out['lru_wx'], 'lru_bx': out['lru_bx'], 'lru_lambda': out['lru_lambda'], 'q_norm': out['q_norm'], 'w_uq': out['w_uq'], 'kv_norm': out['kv_norm'], 'w_ukv': out['w_ukv'], 'w_out': out['w_out'], 'norm_mlp': out['norm_mlp'], 'w_up': out['w_up'], 'w_down': out['w_down'], 'norm_final': out['norm_final'], 'loss_target': out['loss_target'], 'm_norm_mix': out['m_norm_mix'], 'm_w_in': out['m_w_in'], 'm_conv_w': out['m_conv_w'], 'm_conv_b': out['m_conv_b'], 'm_lru_wa': out['m_lru_wa'], 'm_lru_ba': out['m_lru_ba'], 'm_lru_wx': out['m_lru_wx'], 'm_lru_bx': out['m_lru_bx'], 'm_lru_lambda': out['m_lru_lambda'], 'm_q_norm': out['m_q_norm'], 'm_w_uq': out['m_w_uq'], 'm_kv_norm': out['m_kv_norm'], 'm_w_ukv': out['m_w_ukv'], 'm_w_out': out['m_w_out'], 'm_norm_mlp': out['m_norm_mlp'], 'm_w_up': out['m_w_up'], 'm_w_down': out['m_w_down'], 'm_norm_final': out['m_norm_final'], 'v_norm_mix': out['v_norm_mix'], 'v_w_in': out['v_w_in'], 'v_conv_w': out['v_conv_w'], 'v_conv_b': out['v_conv_b'], 'v_lru_wa': out['v_lru_wa'], 'v_lru_ba': out['v_lru_ba'], 'v_lru_wx': out['v_lru_wx'], 'v_lru_bx': out['v_lru_bx'], 'v_lru_lambda': out['v_lru_lambda'], 'v_q_norm': out['v_q_norm'], 'v_w_uq': out['v_w_uq'], 'v_kv_norm': out['v_kv_norm'], 'v_w_ukv': out['v_w_ukv'], 'v_w_out': out['v_w_out'], 'v_norm_mlp': out['v_norm_mlp'], 'v_w_up': out['v_w_up'], 'v_w_down': out['v_w_down'], 'v_norm_final': out['v_norm_final']}


def _loss(weights, diff, rest, loss_target):
    with _jax.named_scope("forward"):
        args = {**rest, TWIN_DIFF_INPUT: diff, **{k: w.astype(_WEIGHT_DTYPES[k]) for k, w in weights.items()}}
        y = _forward(args)
    with _jax.named_scope("loss_head"):
        err = _jnp.square(y.astype(_jnp.float32) - loss_target)
        return 0.5 * _jnp.sum(_jnp.mean(err, axis=-1)) if err.ndim else 0.5 * err


def _adamw(w, g, m, v):
    m = ADAM_B1 * m + (1.0 - ADAM_B1) * g
    v = ADAM_B2 * v + (1.0 - ADAM_B2) * _jnp.square(g)
    m_hat = m / (1.0 - ADAM_B1 ** ADAM_STEP)
    v_hat = v / (1.0 - ADAM_B2 ** ADAM_STEP)
    delta = -ADAM_LR * (m_hat / (_jnp.sqrt(v_hat) + ADAM_EPS) + ADAM_WD * w)
    return delta, m, v


def reference(x, norm_mix, w_in, conv_w, conv_b, lru_wa, lru_ba, lru_wx, lru_bx, lru_lambda, q_norm, w_uq, kv_norm, w_ukv, w_out, norm_mlp, w_up, w_down, norm_final, loss_target, m_norm_mix, m_w_in, m_conv_w, m_conv_b, m_lru_wa, m_lru_ba, m_lru_wx, m_lru_bx, m_lru_lambda, m_q_norm, m_w_uq, m_kv_norm, m_w_ukv, m_w_out, m_norm_mlp, m_w_up, m_w_down, m_norm_final, v_norm_mix, v_w_in, v_conv_w, v_conv_b, v_lru_wa, v_lru_ba, v_lru_wx, v_lru_bx, v_lru_lambda, v_q_norm, v_w_uq, v_kv_norm, v_w_ukv, v_w_out, v_norm_mlp, v_w_up, v_w_down, v_norm_final):
    given = dict(x=x, norm_mix=norm_mix, w_in=w_in, conv_w=conv_w, conv_b=conv_b, lru_wa=lru_wa, lru_ba=lru_ba, lru_wx=lru_wx, lru_bx=lru_bx, lru_lambda=lru_lambda, q_norm=q_norm, w_uq=w_uq, kv_norm=kv_norm, w_ukv=w_ukv, w_out=w_out, norm_mlp=norm_mlp, w_up=w_up, w_down=w_down, norm_final=norm_final, loss_target=loss_target, m_norm_mix=m_norm_mix, m_w_in=m_w_in, m_conv_w=m_conv_w, m_conv_b=m_conv_b, m_lru_wa=m_lru_wa, m_lru_ba=m_lru_ba, m_lru_wx=m_lru_wx, m_lru_bx=m_lru_bx, m_lru_lambda=m_lru_lambda, m_q_norm=m_q_norm, m_w_uq=m_w_uq, m_kv_norm=m_kv_norm, m_w_ukv=m_w_ukv, m_w_out=m_w_out, m_norm_mlp=m_norm_mlp, m_w_up=m_w_up, m_w_down=m_w_down, m_norm_final=m_norm_final, v_norm_mix=v_norm_mix, v_w_in=v_w_in, v_conv_w=v_conv_w, v_conv_b=v_conv_b, v_lru_wa=v_lru_wa, v_lru_ba=v_lru_ba, v_lru_wx=v_lru_wx, v_lru_bx=v_lru_bx, v_lru_lambda=v_lru_lambda, v_q_norm=v_q_norm, v_w_uq=v_w_uq, v_kv_norm=v_kv_norm, v_w_ukv=v_w_ukv, v_w_out=v_w_out, v_norm_mlp=v_norm_mlp, v_w_up=v_w_up, v_w_down=v_w_down, v_norm_final=v_norm_final)
    weights = {n: given[n] for n in TWIN_WEIGHTS}
    shared = {n: given[n] for n in SHARED_INPUTS}
    per_example = {n: given[n] for n in ['x']}
    grad_fn = _jax.value_and_grad(_loss, argnums=(0, 1))

    def one_microbatch(ex, loss_target):
        ex = dict(ex)
        diff = ex.pop(TWIN_DIFF_INPUT)
        return grad_fn(weights, diff, {**shared, **ex}, loss_target)

    if N_MICROBATCH == 1:
        loss, (grad_w, grad_x) = one_microbatch(per_example, given["loss_target"])
    else:
        def body(carry, xs):
            loss_sum, grad_sum = carry
            l_k, (gw_k, gx_k) = one_microbatch(xs[0], xs[1])
            with _jax.named_scope("update"):
                return (loss_sum + l_k, _jax.tree.map(_jnp.add, grad_sum, gw_k)), gx_k

        init = (_jnp.zeros((), _jnp.float32), _jax.tree.map(_jnp.zeros_like, weights))
        (loss, grad_w), grad_x = _jax.lax.scan(body, init, (per_example, given["loss_target"]))
    with _jax.named_scope("update"):
        delta_w, new_m, new_v = {}, {}, {}
        for n in TWIN_WEIGHTS:
            delta_w[n], new_m[n], new_v[n] = _adamw(weights[n], grad_w[n], given["m_" + n], given["v_" + n])
    return (loss, grad_x, *[grad_w[n] for n in TWIN_WEIGHTS], *[delta_w[n] for n in TWIN_WEIGHTS],
            *[new_m[n] for n in TWIN_WEIGHTS], *[new_v[n] for n in TWIN_WEIGHTS])
```

```python
import functools

import numpy as np
import jax
import jax.numpy as jnp
from jax import lax
from jax.experimental import pallas as pl
from jax.experimental.pallas import tpu as pltpu

F32 = jnp.float32
BF16 = jnp.bfloat16
MESH = pl.DeviceIdType.MESH

D_MODEL = 1024
N_DEV = 8
LANES = 128
RNN_BLOCKS = 8
RNN_BLOCK_W = 128
CONV_WIDTH = 4
LRU_C = 8.0
MLA_HEADS = 8
QK_NOPE = 128
QK_ROPE = 64
V_HEAD = 128
QK_PAD = 256
Q_LORA = 256
KV_LORA = 256
CKV_W = 640
ROPE_THETA = 10000.0
D_FF = 4096
EPS = 1e-6
ATTN_SCALE = (QK_NOPE + QK_ROPE) ** -0.5
NEG = -1e30

ADAM_LR = 0.001
ADAM_B1 = 0.9
ADAM_B2 = 0.999
ADAM_EPS = 1e-08
ADAM_WD = 0.01
ADAM_STEP = 10

VMEM_LIMIT = 56 * 1024 * 1024


def _params(sem=None):
    return pltpu.CompilerParams(dimension_semantics=sem, vmem_limit_bytes=VMEM_LIMIT)


def _sigmoid(v):
    return 1.0 / (1.0 + jnp.exp(-v))


def _neg_expm1(y):
    u = jnp.exp(y)
    lu = jnp.log(u)
    safe = jnp.where(lu == 0.0, 1.0, lu)
    return jnp.where(lu == 0.0, -y, (1.0 - u) * y / safe)


def _softplus(y):
    e = jnp.exp(-jnp.abs(y))
    u = 1.0 + e
    d = u - 1.0
    l1p = jnp.where(d == 0.0, e, jnp.log(u) * e / jnp.where(d == 0.0, 1.0, d))
    return jnp.maximum(y, 0.0) + l1p


_GELU_K = 0.7978845608028654
_GELU_C = 0.044715


def _gelu_and_grad(v):
    t = jnp.tanh(_GELU_K * (v + _GELU_C * v * v * v))
    g = 0.5 * v * (1.0 + t)
    dg = 0.5 * (1.0 + t) + 0.5 * v * (1.0 - t * t) * _GELU_K * (1.0 + 3.0 * _GELU_C * v * v)
    return g, dg


def _rms_fwd(v, g):
    rstd = lax.rsqrt(jnp.mean(v * v, axis=-1, keepdims=True) + EPS)
    return v * rstd * g, rstd


def _rms_bwd(dy, v, g):
    rstd = lax.rsqrt(jnp.mean(v * v, axis=-1, keepdims=True) + EPS)
    vh = v * rstd
    dvh = dy * g
    dv = rstd * (dvh - vh * jnp.mean(dvh * vh, axis=-1, keepdims=True))
    return dv, dy * vh


def _shift_down(v, s, fill, row):
    return jnp.where(row >= s, pltpu.roll(v, s, 0), fill)


def _shift_up(v, s, fill, row, n):
    return jnp.where(row < n - s, pltpu.roll(v, n - s, 0), fill)


def _rot_half(v, lane):
    n = v.shape[-1]
    l = lane & (LANES - 1)
    up = pltpu.roll(v, n - QK_ROPE // 2, 1)
    dn = pltpu.roll(v, QK_ROPE // 2, 1)
    return jnp.where(l < QK_ROPE // 2, -up, jnp.where(l < QK_ROPE, dn, 0.0))


def _mm(a, b, *, name, tm, tn, tk, outs, epilogue, extras=(), ta=False):
    if ta:
        K, M = a.shape
    else:
        M, K = a.shape
    K2, N = b.shape
    assert K == K2 and M % tm == 0 and N % tn == 0 and K % tk == 0, (name, a.shape, b.shape)
    n_i, n_j, n_k = M // tm, N // tn, K // tk
    n_ex, n_out = len(extras), len(outs)

    def body(*refs):
        a_ref, b_ref = refs[0], refs[1]
        ex_refs = refs[2:2 + n_ex]
        out_refs = refs[2 + n_ex:2 + n_ex + n_out]
        if ta:
            part = lax.dot_general(a_ref[...], b_ref[...], (((0,), (0,)), ((), ())), preferred_element_type=F32)
        else:
            part = jnp.dot(a_ref[...], b_ref[...], preferred_element_type=F32)

        def finish(acc):
            res = epilogue(acc, *[r[...] for r in ex_refs])
            for o_ref, r in zip(out_refs, res):
                o_ref[...] = r.astype(o_ref.dtype).reshape(o_ref.shape)

        if n_k == 1:
            finish(part)
        else:
            acc_ref = refs[-1]
            k = pl.program_id(2)

            @pl.when(k == 0)
            def _():
                acc_ref[...] = part

            @pl.when(k > 0)
            def _():
                acc_ref[...] += part

            @pl.when(k == n_k - 1)
            def _():
                finish(acc_ref[...])

    a_spec = pl.BlockSpec((tk, tm), lambda i, j, k: (k, i)) if ta else pl.BlockSpec((tm, tk), lambda i, j, k: (i, k))
    in_specs = [a_spec, pl.BlockSpec((tk, tn), lambda i, j, k: (k, j))]
    for ex in extras:
        kind = ex[0]
        if kind == "tile":
            in_specs.append(pl.BlockSpec((tm, tn), lambda i, j, k: (i, j)))
        elif kind == "tilecol":
            assert n_j == 1
            in_specs.append(pl.BlockSpec((tm, tn), functools.partial(lambda c, i, j, k: (i, c), ex[2])))
        else:
            in_specs.append(pl.BlockSpec((1, tn), lambda i, j, k: (0, j)))
    out_specs, out_shape = [], []
    for kind, dt in outs:
        if kind == "tile":
            out_specs.append(pl.BlockSpec((tm, tn), lambda i, j, k: (i, j)))
            out_shape.append(jax.ShapeDtypeStruct((M, N), dt))
        elif kind == "side":
            assert n_j == 1
            out_specs.append(pl.BlockSpec((tm, LANES), lambda i, j, k: (i, 0)))
            out_shape.append(jax.ShapeDtypeStruct((M, LANES), dt))
        else:
            out_specs.append(pl.BlockSpec((1, 1, tn), lambda i, j, k: (i, 0, j)))
            out_shape.append(jax.ShapeDtypeStruct((n_i, 1, N), dt))
    scratch = [pltpu.VMEM((tm, tn), F32)] if n_k > 1 else []
    return pl.pallas_call(
        body, name=name, grid=(n_i, n_j, n_k), in_specs=in_specs, out_specs=out_specs, out_shape=out_shape,
        scratch_shapes=scratch, compiler_params=_params(("parallel", "parallel", "arbitrary")),
    )(a, b, *[ex[1] for ex in extras])


def _rmsnorm_cast(x, g, *, ts, name):
    S, D = x.shape

    def body(x_ref, g_ref, o_ref):
        y, _ = _rms_fwd(x_ref[...], g_ref[...])
        o_ref[...] = y.astype(BF16)

    return pl.pallas_call(
        body, name=name, grid=(S // ts,),
        in_specs=[pl.BlockSpec((ts, D), lambda i: (i, 0)), pl.BlockSpec((1, D), lambda i: (0, 0))],
        out_specs=pl.BlockSpec((ts, D), lambda i: (i, 0)), out_shape=jax.ShapeDtypeStruct((S, D), BF16),
        compiler_params=_params(("parallel",)),
    )(x, g)


def _lru_gates(xa, wa, ba, wx, bx, lam):
    xab = xa.astype(BF16)
    r = _sigmoid(jnp.dot(xab, wa, preferred_element_type=F32) + ba)
    i = _sigmoid(jnp.dot(xab, wx, preferred_element_type=F32) + bx)
    sp = _softplus(-lam)
    log_a = (-LRU_C * r) * sp
    a = jnp.exp(log_a)
    mult = jnp.sqrt(_neg_expm1(2.0 * log_a))
    return r, i, sp, a, mult


def _lru_fwd(z_main, conv_w, conv_b, wa, ba, wx, bx, lam, *, tt):
    S = z_main.shape[0]
    n_t = S // tt
    W = RNN_BLOCK_W

    def body(x_ref, cw_ref, cb_ref, wa_ref, ba_ref, wx_ref, bx_ref, lam_ref, h_ref, ext, hc):
        t = pl.program_id(1)

        @pl.when(t == 0)
        def _():
            ext[0:8, :] = jnp.zeros((8, W), F32)
            hc[...] = jnp.zeros((8, W), F32)

        x = x_ref[...]
        ext[8:8 + tt, :] = x
        cw = cw_ref[...]
        xa = (cb_ref[...] + cw[3:4] * x + cw[2:3] * ext[7:7 + tt, :] + cw[1:2] * ext[6:6 + tt, :]
              + cw[0:1] * ext[5:5 + tt, :])
        ext[0:8, :] = x[tt - 8:tt, :]
        _r, i, _sp, a, mult = _lru_gates(xa, wa_ref[0], ba_ref[0], wx_ref[0], bx_ref[0], lam_ref[...])
        b = mult * (i * xa)
        row = lax.broadcasted_iota(jnp.int32, (tt, W), 0)
        A, B = a, b
        s = 1
        while s < tt:
            B = A * _shift_down(B, s, 0.0, row) + B
            A = A * _shift_down(A, s, 1.0, row)
            s *= 2
        h = A * hc[0:1, :] + B
        h_ref[...] = h
        hc[...] = jnp.broadcast_to(h[tt - 1:tt, :], (8, W))

    blk = lambda n, t: (t, n)
    vec = pl.BlockSpec((1, W), lambda n, t: (0, n))
    mat = pl.BlockSpec((1, W, W), lambda n, t: (n, 0, 0))
    bias = pl.BlockSpec((1, 1, W), lambda n, t: (n, 0, 0))
    return pl.pallas_call(
        body, name="lru_fwd", grid=(RNN_BLOCKS, n_t),
        in_specs=[pl.BlockSpec((tt, W), blk), pl.BlockSpec((CONV_WIDTH, W), lambda n, t: (0, n)), vec, mat, bias, mat,
                  bias, vec],
        out_specs=pl.BlockSpec((tt, W), blk), out_shape=jax.ShapeDtypeStruct((S, D_MODEL), F32),
        scratch_shapes=[pltpu.VMEM((tt + 8, W), F32), pltpu.VMEM((8, W), F32)],
        compiler_params=_params(("parallel", "arbitrary")),
    )(z_main, conv_w, conv_b, wa, ba, wx, bx, lam)


def _lru_bwd(z_main, h, dh, conv_w, conv_b, wa, wat, ba, wx, wxt, bx, lam, *, tt):
    S = z_main.shape[0]
    n_t = S // tt
    W = RNN_BLOCK_W
    t8 = tt // 8

    def body(x_ref, xp_ref, h_ref, hp_ref, dh_ref, cw_ref, cb_ref, wa_ref, wat_ref, ba_ref, wx_ref, wxt_ref, bx_ref,
             lam_ref, dx_ref, dwa_ref, dwx_ref, dba_ref, dbx_ref, dlam_ref, dcw_ref, dcb_ref, ext, dext, a_c, g_c):
        t = pl.program_id(1)
        tile = n_t - 1 - t

        @pl.when(t == 0)
        def _():
            a_c[...] = jnp.zeros((8, W), F32)
            g_c[...] = jnp.zeros((8, W), F32)
            dext[tt:tt + 8, :] = jnp.zeros((8, W), F32)
            dwa_ref[...] = jnp.zeros_like(dwa_ref)
            dwx_ref[...] = jnp.zeros_like(dwx_ref)
            dba_ref[...] = jnp.zeros_like(dba_ref)
            dbx_ref[...] = jnp.zeros_like(dbx_ref)
            dlam_ref[...] = jnp.zeros_like(dlam_ref)
            dcw_ref[...] = jnp.zeros_like(dcw_ref)
            dcb_ref[...] = jnp.zeros_like(dcb_ref)

        has_prev = (tile > 0).astype(F32)
        x = x_ref[...]
        ext[0:8, :] = xp_ref[...] * has_prev
        ext[8:8 + tt, :] = x
        xm1, xm2, xm3 = ext[7:7 + tt, :], ext[6:6 + tt, :], ext[5:5 + tt, :]
        cw = cw_ref[...]
        xa = cb_ref[...] + cw[3:4] * x + cw[2:3] * xm1 + cw[1:2] * xm2 + cw[0:1] * xm3
        lam = lam_ref[...]
        r, i, sp, a, mult = _lru_gates(xa, wa_ref[0], ba_ref[0], wx_ref[0], bx_ref[0], lam)
        gated = i * xa
        row = lax.broadcasted_iota(jnp.int32, (tt, W), 0)
        hcur = h_ref[...]
        h_prev = _shift_down(hcur, 1, hp_ref[7:8, :] * has_prev, row)
        C = _shift_up(a, 1, a_c[0:1, :], row, tt)
        G = dh_ref[...]
        s = 1
        while s < tt:
            G = G + C * _shift_up(G, s, 0.0, row, tt)
            C = C * _shift_up(C, s, 1.0, row, tt)
            s *= 2
        g = G + C * g_c[0:1, :]
        a_c[...] = jnp.broadcast_to(a[0:1, :], (8, W))
        g_c[...] = jnp.broadcast_to(g[0:1, :], (8, W))
        dlog_a = g * h_prev * a - g * gated * (a * a) / mult
        dgated = g * mult
        di = dgated * xa
        dxa = dgated * i
        dr = dlog_a * (-LRU_C * sp)
        dsp = jnp.sum(dlog_a * (-LRU_C * r), axis=0, keepdims=True)
        dlam_ref[0] += dsp * (-_sigmoid(-lam))
        dpr = dr * r * (1.0 - r)
        dpi = di * i * (1.0 - i)
        xab, dprb, dpib = xa.astype(BF16), dpr.astype(BF16), dpi.astype(BF16)
        tn_dims = (((0,), (0,)), ((), ()))
        dwa_ref[0] += lax.dot_general(xab, dprb, tn_dims, preferred_element_type=F32)
        dwx_ref[0] += lax.dot_general(xab, dpib, tn_dims, preferred_element_type=F32)
        dba_ref[0] += jnp.sum(dpr, axis=0, keepdims=True)
        dbx_ref[0] += jnp.sum(dpi, axis=0, keepdims=True)
        dxa = (dxa + jnp.dot(dprb, wat_ref[0], preferred_element_type=F32)
               + jnp.dot(dpib, wxt_ref[0], preferred_element_type=F32))
        dext[0:tt, :] = dxa
        dx = cw[3:4] * dxa + cw[2:3] * dext[1:1 + tt, :] + cw[1:2] * dext[2:2 + tt, :] + cw[0:1] * dext[3:3 + tt, :]
        dext[tt:tt + 8, :] = dxa[0:8, :]
        dx_ref[...] = dx.astype(BF16)
        dcw_ref[3:4, :] += jnp.sum(dxa * x, axis=0, keepdims=True)
        dcw_ref[2:3, :] += jnp.sum(dxa * xm1, axis=0, keepdims=True)
        dcw_ref[1:2, :] += jnp.sum(dxa * xm2, axis=0, keepdims=True)
        dcw_ref[0:1, :] += jnp.sum(dxa * xm3, axis=0, keepdims=True)
        dcb_ref[...] += jnp.sum(dxa, axis=0, keepdims=True)

    blk = lambda n, t: (n_t - 1 - t, n)
    prev = lambda n, t: (jnp.maximum((n_t - 1 - t) * t8 - 1, 0), n)
    vec = pl.BlockSpec((1, W), lambda n, t: (0, n))
    mat = pl.BlockSpec((1, W, W), lambda n, t: (n, 0, 0))
    bias = pl.BlockSpec((1, 1, W), lambda n, t: (n, 0, 0))
    cws = pl.BlockSpec((CONV_WIDTH, W), lambda n, t: (0, n))
    tile = pl.BlockSpec((tt, W), blk)
    prev8 = pl.BlockSpec((8, W), prev)
    return pl.pallas_call(
        body, name="lru_bwd", grid=(RNN_BLOCKS, n_t),
        in_specs=[tile, prev8, tile, prev8, tile, cws, vec, mat, mat, bias, mat, mat, bias, vec],
        out_specs=[tile, mat, mat, bias, bias, bias, cws, vec],
        out_shape=[jax.ShapeDtypeStruct((S, D_MODEL), BF16),
                   jax.ShapeDtypeStruct((RNN_BLOCKS, W, W), F32), jax.ShapeDtypeStruct((RNN_BLOCKS, W, W), F32),
                   jax.ShapeDtypeStruct((RNN_BLOCKS, 1, W), F32), jax.ShapeDtypeStruct((RNN_BLOCKS, 1, W), F32),
                   jax.ShapeDtypeStruct((RNN_BLOCKS, 1, W), F32),
                   jax.ShapeDtypeStruct((CONV_WIDTH, D_MODEL), F32), jax.ShapeDtypeStruct((1, D_MODEL), F32)],
        scratch_shapes=[pltpu.VMEM((tt + 8, W), F32), pltpu.VMEM((tt + 8, W), F32), pltpu.VMEM((8, W), F32),
                        pltpu.VMEM((8, W), F32)],
        compiler_params=_params(("parallel", "arbitrary")),
    )(z_main, z_main, h, h, dh, conv_w, conv_b, wa, wat, ba, wx, wxt, bx, lam)


def _mla_proj(z_ckv, q_norm, kv_norm, w_uq, w_ukv, cos, sin, *, ts):
    S = z_ckv.shape[0]
    H = MLA_HEADS

    def body(c_ref, qn_ref, kn_ref, wq_ref, wkv_ref, cos_ref, sin_ref, q_ref, k_ref, v_ref):
        c = c_ref[...]
        cqn, _ = _rms_fwd(c[:, 0:Q_LORA], qn_ref[...])
        ckn, _ = _rms_fwd(c[:, Q_LORA:Q_LORA + KV_LORA], kn_ref[...])
        q = jnp.dot(cqn.astype(BF16), wq_ref[...], preferred_element_type=F32) * ATTN_SCALE
        kv = jnp.dot(ckn.astype(BF16), wkv_ref[...], preferred_element_type=F32)
        cos1, sin1 = cos_ref[...], sin_ref[...]
        cos8 = jnp.concatenate([cos1] * H, axis=1)
        sin8 = jnp.concatenate([sin1] * H, axis=1)
        qr = q[:, H * QK_NOPE:]
        lane8 = lax.broadcasted_iota(jnp.int32, qr.shape, 1)
        qr = qr * cos8 + _rot_half(qr, lane8) * sin8
        kr = c[:, Q_LORA + KV_LORA:]
        lane1 = lax.broadcasted_iota(jnp.int32, kr.shape, 1)
        kr = (kr * cos1 + _rot_half(kr, lane1) * sin1).astype(BF16)
        for h in range(H):
            q_ref[h, :, 0:QK_NOPE] = q[:, h * QK_NOPE:(h + 1) * QK_NOPE].astype(BF16)
            q_ref[h, :, QK_NOPE:] = qr[:, h * LANES:(h + 1) * LANES].astype(BF16)
            k_ref[h, :, 0:QK_NOPE] = kv[:, h * 2 * LANES:h * 2 * LANES + LANES].astype(BF16)
            k_ref[h, :, QK_NOPE:] = kr
            v_ref[h] = kv[:, h * 2 * LANES + LANES:(h + 1) * 2 * LANES].astype(BF16)

    full = lambda shape: pl.BlockSpec(shape, lambda i: (0,) * len(shape))
    return pl.pallas_call(
        body, name="mla_proj", grid=(S // ts,),
        in_specs=[pl.BlockSpec((ts, CKV_W), lambda i: (i, 0)), full((1, Q_LORA)), full((1, KV_LORA)),
                  full(w_uq.shape), full(w_ukv.shape), pl.BlockSpec((ts, LANES), lambda i: (i, 0)),
                  pl.BlockSpec((ts, LANES), lambda i: (i, 0))],
        out_specs=[pl.BlockSpec((H, ts, QK_PAD), lambda i: (0, i, 0)), pl.BlockSpec((H, ts, QK_PAD), lambda i: (0, i, 0)),
                   pl.BlockSpec((H, ts, V_HEAD), lambda i: (0, i, 0))],
        out_shape=[jax.ShapeDtypeStruct((H, S, QK_PAD), BF16), jax.ShapeDtypeStruct((H, S, QK_PAD), BF16),
                   jax.ShapeDtypeStruct((H, S, V_HEAD), BF16)],
        compiler_params=_params(("parallel",)),
    )(z_ckv, q_norm, kv_norm, w_uq, w_ukv, cos, sin)


def _mla_proj_bwd(z_ckv, dq, dk, dv, q_norm, kv_norm, w_uqt, w_ukvt, cos, sin, *, ts):
    S = z_ckv.shape[0]
    H = MLA_HEADS

    def body(c_ref, dq_ref, dk_ref, dv_ref, qn_ref, kn_ref, wqt_ref, wkvt_ref, cos_ref, sin_ref,
             dz_ref, dwq_ref, dwkv_ref, dqn_ref, dkn_ref):
        @pl.when(pl.program_id(0) == 0)
        def _():
            dwq_ref[...] = jnp.zeros_like(dwq_ref)
            dwkv_ref[...] = jnp.zeros_like(dwkv_ref)
            dqn_ref[...] = jnp.zeros_like(dqn_ref)
            dkn_ref[...] = jnp.zeros_like(dkn_ref)

        c = c_ref[...]
        cq, ck = c[:, 0:Q_LORA], c[:, Q_LORA:Q_LORA + KV_LORA]
        qn, kn = qn_ref[...], kn_ref[...]
        cqn, _ = _rms_fwd(cq, qn)
        ckn, _ = _rms_fwd(ck, kn)
        cos1, sin1 = cos_ref[...], sin_ref[...]
        lane1 = lax.broadcasted_iota(jnp.int32, cos1.shape, 1)

        def unrope(g):
            return g * cos1 - _rot_half(g * sin1, lane1)

        dq_all = jnp.concatenate([dq_ref[h, :, 0:QK_NOPE] for h in range(H)]
                                 + [unrope(dq_ref[h, :, QK_NOPE:]) for h in range(H)], axis=1)
        dq_all = (dq_all * ATTN_SCALE).astype(BF16)
        dkv_all = jnp.concatenate([p for h in range(H) for p in (dk_ref[h, :, 0:QK_NOPE], dv_ref[h])],
                                  axis=1).astype(BF16)
        dkr = dk_ref[0, :, QK_NOPE:]
        for h in range(1, H):
            dkr = dkr + dk_ref[h, :, QK_NOPE:]
        dkr = unrope(dkr)
        tn_dims = (((0,), (0,)), ((), ()))
        dwq_ref[...] += lax.dot_general(cqn.astype(BF16), dq_all, tn_dims, preferred_element_type=F32)
        dwkv_ref[...] += lax.dot_general(ckn.astype(BF16), dkv_all, tn_dims, preferred_element_type=F32)
        dcqn = jnp.dot(dq_all, wqt_ref[...], preferred_element_type=F32)
        dckn = jnp.dot(dkv_all, wkvt_ref[...], preferred_element_type=F32)
        dcq, dqn_rows = _rms_bwd(dcqn, cq, qn)
        dck, dkn_rows = _rms_bwd(dckn, ck, kn)
        dqn_ref[...] += jnp.sum(dqn_rows, axis=0, keepdims=True)
        dkn_ref[...] += jnp.sum(dkn_rows, axis=0, keepdims=True)
        dz_ref[:, 0:Q_LORA] = dcq.astype(BF16)
        dz_ref[:, Q_LORA:Q_LORA + KV_LORA] = dck.astype(BF16)
        dz_ref[:, Q_LORA + KV_LORA:] = dkr.astype(BF16)

    full = lambda shape: pl.BlockSpec(shape, lambda i: (0,) * len(shape))
    return pl.pallas_call(
        body, name="mla_proj_bwd", grid=(S // ts,),
        in_specs=[pl.BlockSpec((ts, CKV_W), lambda i: (i, 0)), pl.BlockSpec((H, ts, QK_PAD), lambda i: (0, i, 0)),
                  pl.BlockSpec((H, ts, QK_PAD), lambda i: (0, i, 0)), pl.BlockSpec((H, ts, V_HEAD), lambda i: (0, i, 0)),
                  full((1, Q_LORA)), full((1, KV_LORA)), full(w_uqt.shape), full(w_ukvt.shape),
                  pl.BlockSpec((ts, LANES), lambda i: (i, 0)), pl.BlockSpec((ts, LANES), lambda i: (i, 0))],
        out_specs=[pl.BlockSpec((ts, CKV_W), lambda i: (i, 0)), full((Q_LORA, w_uqt.shape[0])),
                   full((KV_LORA, w_ukvt.shape[0])), full((1, Q_LORA)), full((1, KV_LORA))],
        out_shape=[jax.ShapeDtypeStruct((S, CKV_W), BF16), jax.ShapeDtypeStruct((Q_LORA, w_uqt.shape[0]), F32),
                   jax.ShapeDtypeStruct((KV_LORA, w_ukvt.shape[0]), F32), jax.ShapeDtypeStruct((1, Q_LORA), F32),
                   jax.ShapeDtypeStruct((1, KV_LORA), F32)],
        compiler_params=_params(("arbitrary",)),
    )(z_ckv, dq, dk, dv, q_norm, kv_norm, w_uqt, w_ukvt, cos, sin)


NT_DIMS = (((1,), (1,)), ((), ()))
TN_DIMS = (((0,), (0,)), ((), ()))


def _attn_fwd(q, k, v, *, t):
    H, S, _ = q.shape
    n = S // t
    pairs = [(i, j) for i in range(n) for j in range(i + 1)]
    qi = jnp.asarray(np.array([p[0] for p in pairs], np.int32))
    ki = jnp.asarray(np.array([p[1] for p in pairs], np.int32))

    def body(qi_ref, ki_ref, q_ref, k_ref, v_ref, o_ref, lse_ref, m_s, l_s, acc_s):
        p = pl.program_id(1)
        i, j = qi_ref[p], ki_ref[p]

        @pl.when(j == 0)
        def _():
            m_s[...] = jnp.full((t, 1), NEG, F32)
            l_s[...] = jnp.zeros((t, 1), F32)
            acc_s[...] = jnp.zeros((t, V_HEAD), F32)

        s = lax.dot_general(q_ref[0], k_ref[0], NT_DIMS, preferred_element_type=F32)
        row = lax.broadcasted_iota(jnp.int32, (t, t), 0) + i * t
        col = lax.broadcasted_iota(jnp.int32, (t, t), 1) + j * t
        s = jnp.where(row >= col, s, NEG)
        m_prev = m_s[...]
        m_new = jnp.maximum(m_prev, jnp.max(s, axis=1, keepdims=True))
        alpha = jnp.exp(m_prev - m_new)
        pr = jnp.exp(s - m_new)
        l_s[...] = alpha * l_s[...] + jnp.sum(pr, axis=1, keepdims=True)
        acc_s[...] = alpha * acc_s[...] + jnp.dot(pr.astype(BF16), v_ref[0], preferred_element_type=F32)
        m_s[...] = m_new

        @pl.when(j == i)
        def _():
            l = l_s[...]
            o_ref[...] = acc_s[...] / l
            lse_ref[0] = m_s[...] + jnp.log(l)

    grid_spec = pltpu.PrefetchScalarGridSpec(
        num_scalar_prefetch=2, grid=(H, len(pairs)),
        in_specs=[pl.BlockSpec((1, t, QK_PAD), lambda h, p, qi, ki: (h, qi[p], 0)),
                  pl.BlockSpec((1, t, QK_PAD), lambda h, p, qi, ki: (h, ki[p], 0)),
                  pl.BlockSpec((1, t, V_HEAD), lambda h, p, qi, ki: (h, ki[p], 0))],
        out_specs=[pl.BlockSpec((t, V_HEAD), lambda h, p, qi, ki: (qi[p], h)),
                   pl.BlockSpec((1, t, 1), lambda h, p, qi, ki: (h, qi[p], 0))],
        scratch_shapes=[pltpu.VMEM((t, 1), F32), pltpu.VMEM((t, 1), F32), pltpu.VMEM((t, V_HEAD), F32)],
    )
    return pl.pallas_call(
        body, name="attn_fwd", grid_spec=grid_spec,
        out_shape=[jax.ShapeDtypeStruct((S, H * V_HEAD), F32), jax.ShapeDtypeStruct((H, S, 1), F32)],
        compiler_params=_params(("parallel", "arbitrary")),
    )(qi, ki, q, k, v)


def _attn_bwd(q, k, v, do, lse_row, delta_row, *, t):
    H, S, _ = q.shape
    n = S // t
    pairs = [(i, j) for j in range(n) for i in range(j, n)]
    qi = jnp.asarray(np.array([p[0] for p in pairs], np.int32))
    ki = jnp.asarray(np.array([p[1] for p in pairs], np.int32))

    def body(qi_ref, ki_ref, q_ref, k_ref, v_ref, do_ref, lse_ref, dl_ref, dq_ref, dk_ref, dv_ref, dk_s, dv_s):
        p = pl.program_id(1)
        i, j = qi_ref[p], ki_ref[p]

        @pl.when(p == 0)
        def _():
            dq_ref[...] = jnp.zeros_like(dq_ref)

        @pl.when(i == j)
        def _():
            dk_s[...] = jnp.zeros_like(dk_s)
            dv_s[...] = jnp.zeros_like(dv_s)

        qb, kb, vb, dob = q_ref[0], k_ref[0], v_ref[0], do_ref[...]
        st = lax.dot_general(kb, qb, NT_DIMS, preferred_element_type=F32)
        krow = lax.broadcasted_iota(jnp.int32, (t, t), 0) + j * t
        qcol = lax.broadcasted_iota(jnp.int32, (t, t), 1) + i * t
        st = jnp.where(krow <= qcol, st, NEG)
        pt = jnp.exp(st - lse_ref[0])
        dv_s[...] += jnp.dot(pt.astype(BF16), dob, preferred_element_type=F32)
        dpt = lax.dot_general(vb, dob, NT_DIMS, preferred_element_type=F32)
        dst = (pt * (dpt - dl_ref[0])).astype(BF16)
        dk_s[...] += jnp.dot(dst, qb, preferred_element_type=F32)
        rows = pl.ds(pl.multiple_of(i * t, t), t)
        dq_ref[0, rows, :] += lax.dot_general(dst, kb, TN_DIMS, preferred_element_type=F32)

        @pl.when(i == n - 1)
        def _():
            dk_ref[0] = dk_s[...]
            dv_ref[0] = dv_s[...]

    grid_spec = pltpu.PrefetchScalarGridSpec(
        num_scalar_prefetch=2, grid=(H, len(pairs)),
        in_specs=[pl.BlockSpec((1, t, QK_PAD), lambda h, p, qi, ki: (h, qi[p], 0)),
                  pl.BlockSpec((1, t, QK_PAD), lambda h, p, qi, ki: (h, ki[p], 0)),
                  pl.BlockSpec((1, t, V_HEAD), lambda h, p, qi, ki: (h, ki[p], 0)),
                  pl.BlockSpec((t, V_HEAD), lambda h, p, qi, ki: (qi[p], h)),
                  pl.BlockSpec((1, 1, t), lambda h, p, qi, ki: (h, 0, qi[p])),
                  pl.BlockSpec((1, 1, t), lambda h, p, qi, ki: (h, 0, qi[p]))],
        out_specs=[pl.BlockSpec((1, S, QK_PAD), lambda h, p, qi, ki: (h, 0, 0)),
                   pl.BlockSpec((1, t, QK_PAD), lambda h, p, qi, ki: (h, ki[p], 0)),
                   pl.BlockSpec((1, t, V_HEAD), lambda h, p, qi, ki: (h, ki[p], 0))],
        scratch_shapes=[pltpu.VMEM((t, QK_PAD), F32), pltpu.VMEM((t, V_HEAD), F32)],
    )
    return pl.pallas_call(
        body, name="attn_bwd", grid_spec=grid_spec,
        out_shape=[jax.ShapeDtypeStruct((H, S, QK_PAD), F32), jax.ShapeDtypeStruct((H, S, QK_PAD), F32),
                   jax.ShapeDtypeStruct((H, S, V_HEAD), F32)],
        compiler_params=_params(("parallel", "arbitrary")),
    )(qi, ki, q, k, v, do, lse_row, delta_row)


def _merge_fwd(h, z_main, o, *, ts):
    S = h.shape[0]
    D = D_MODEL

    def body(h_ref, rg_ref, ga_ref, gb_ref, o_ref, m_ref):
        gl, _ = _gelu_and_grad(rg_ref[...])
        m = _sigmoid(ga_ref[...]) * (h_ref[...] * gl) + _sigmoid(gb_ref[...]) * o_ref[...]
        m_ref[...] = m.astype(BF16)

    col = lambda c: pl.BlockSpec((ts, D), lambda i: (i, c))
    return pl.pallas_call(
        body, name="merge_fwd", grid=(S // ts,),
        in_specs=[col(0), col(1), col(2), col(3), col(0)],
        out_specs=col(0), out_shape=jax.ShapeDtypeStruct((S, D), BF16),
        compiler_params=_params(("parallel",)),
    )(h, z_main, z_main, z_main, o)


def _my_place():
    return lax.axis_index("x"), lax.axis_index("y"), lax.axis_index("c")


def _all_gather(shard, *, name):
    R, C = shard.shape

    def body(x_ref, out_ref, send_sems, recv_sems, local_sem):
        x, y, c = _my_place()
        me, sibling = (x, y, c), (x, y, 1 - c)
        chips = [(1 - x, y), (x, 1 - y), (1 - x, 1 - y)]

        def slot(px, py, pc):
            return out_ref.at[4 * px + 2 * py + pc]

        def copy(k, block, to, src=None):
            return pltpu.make_async_remote_copy(
                src_ref=slot(*block) if src is None else src, dst_ref=slot(*block),
                send_sem=send_sems.at[k], recv_sem=recv_sems.at[k], device_id=to, device_id_type=MESH)

        mine = pltpu.make_async_copy(x_ref, slot(*me), local_sem)
        mine.start()
        first = [copy(0, me, sibling, src=x_ref)]
        first += [copy(1 + j, me, (*chip, c), src=x_ref) for j, chip in enumerate(chips)]
        for cp in first:
            cp.start()
        passed = [copy(4 + j, (*chip, c), sibling) for j, chip in enumerate(chips)]
        for j, chip in enumerate(chips):
            copy(1 + j, (*chip, c), me).wait_recv()
            passed[j].start()
        copy(0, sibling, me).wait_recv()
        for j, chip in enumerate(chips):
            copy(4 + j, (*chip, 1 - c), me).wait_recv()
        for cp in first + passed:
            cp.wait_send()
        mine.wait()

    return pl.pallas_call(
        body, name=name, out_shape=jax.ShapeDtypeStruct((N_DEV, R, C), shard.dtype),
        in_specs=[pl.BlockSpec(memory_space=pl.ANY)], out_specs=pl.BlockSpec(memory_space=pl.ANY),
        scratch_shapes=[pltpu.SemaphoreType.DMA((7,)), pltpu.SemaphoreType.DMA((7,)), pltpu.SemaphoreType.DMA],
    )(shard)


def _all_to_all(parts, *, name):
    _, R, C = parts.shape

    def body(p_ref, out_ref, send_sems, recv_sems, local_sem):
        x, y, c = _my_place()
        me = 4 * x + 2 * y + c
        mine = pltpu.make_async_copy(p_ref.at[me], out_ref.at[me], local_sem)
        mine.start()
        copies = []
        for k in range(1, N_DEV):
            px, py, pc = x ^ (k >> 2), y ^ ((k >> 1) & 1), c ^ (k & 1)
            copies.append(pltpu.make_async_remote_copy(
                src_ref=p_ref.at[4 * px + 2 * py + pc], dst_ref=out_ref.at[me],
                send_sem=send_sems.at[k - 1], recv_sem=recv_sems.at[k - 1], device_id=(px, py, pc),
                device_id_type=MESH))
        for cp in copies:
            cp.start()
        for cp in copies:
            cp.wait_recv()
        for cp in copies:
            cp.wait_send()
        mine.wait()

    return pl.pallas_call(
        body, name=name, out_shape=jax.ShapeDtypeStruct(parts.shape, parts.dtype),
        in_specs=[pl.BlockSpec(memory_space=pl.ANY)], out_specs=pl.BlockSpec(memory_space=pl.ANY),
        scratch_shapes=[pltpu.SemaphoreType.DMA((7,)), pltpu.SemaphoreType.DMA((7,)), pltpu.SemaphoreType.DMA],
    )(parts)


def _adamw(w, m, v, gparts, *, tr, name):
    R = w.shape[0]

    def body(w_ref, m_ref, v_ref, gp_ref, g_ref, d_ref, nm_ref, nv_ref):
        g = gp_ref[0]
        for p in range(1, N_DEV):
            g = g + gp_ref[p]
        wv = w_ref[...]
        m_new = ADAM_B1 * m_ref[...] + (1.0 - ADAM_B1) * g
        v_new = ADAM_B2 * v_ref[...] + (1.0 - ADAM_B2) * (g * g)
        m_hat = m_new / (1.0 - ADAM_B1 ** ADAM_STEP)
        v_hat = v_new / (1.0 - ADAM_B2 ** ADAM_STEP)
        g_ref[...] = g
        d_ref[...] = -ADAM_LR * (m_hat / (jnp.sqrt(v_hat) + ADAM_EPS) + ADAM_WD * wv)
        nm_ref[...] = m_new
        nv_ref[...] = v_new

    row = pl.BlockSpec((tr, LANES), lambda i: (i, 0))
    shp = jax.ShapeDtypeStruct((R, LANES), F32)
    return pl.pallas_call(
        body, name=name, grid=(R // tr,),
        in_specs=[row, row, row, pl.BlockSpec((N_DEV, tr, LANES), lambda i: (0, i, 0))],
        out_specs=[row, row, row, row], out_shape=[shp, shp, shp, shp],
        compiler_params=_params(("parallel",)),
    )(w, m, v, gparts)


def _rope_tables(s):
    pos = jnp.arange(s, dtype=F32)
    inv_freq = 1.0 / (ROPE_THETA ** (jnp.arange(0, QK_ROPE, 2, dtype=F32) / QK_ROPE))
    ang = pos[:, None] * inv_freq[None, :]
    cos, sin = jnp.cos(ang), jnp.sin(ang)
    zero = jnp.zeros((s, LANES - QK_ROPE), F32)
    return jnp.concatenate([cos, cos, zero], -1), jnp.concatenate([sin, sin, zero], -1)


def _pick(n, want):
    t = min(n, want)
    assert n % t == 0
    return t


def _local_step(x, target, wts, small):
    S = x.shape[0]
    H = MLA_HEADS
    ts = _pick(S, 512)
    tm = _pick(S, 512)
    tk_s = _pick(S, 512)
    row = lambda v: v.reshape(1, -1)
    w_in = wts["w_in"]
    w_main = jnp.concatenate([w_in[:, 0:2048], w_in[:, 2624:4672]], axis=1)
    w_ckv = jnp.concatenate([w_in[:, 2048:2624], jnp.zeros((D_MODEL, CKV_W - 576), BF16)], axis=1)
    w_uq3 = wts["w_uq"].reshape(Q_LORA, H, QK_NOPE + QK_ROPE)
    w_uq_p = jnp.concatenate(
        [w_uq3[:, :, :QK_NOPE].reshape(Q_LORA, H * QK_NOPE),
         jnp.pad(w_uq3[:, :, QK_NOPE:], ((0, 0), (0, 0), (0, LANES - QK_ROPE))).reshape(Q_LORA, H * LANES)], axis=1)
    w_ukv, w_out, w_up, w_down = wts["w_ukv"], wts["w_out"], wts["w_up"], wts["w_down"]
    cos, sin = _rope_tables(S)
    conv_w, conv_b = small["conv_w"], row(small["conv_b"])
    wa, wx = small["lru_wa"].astype(BF16), small["lru_wx"].astype(BF16)
    wat, wxt = jnp.swapaxes(wa, 1, 2), jnp.swapaxes(wx, 1, 2)
    ba, bx = small["lru_ba"].reshape(RNN_BLOCKS, 1, RNN_BLOCK_W), small["lru_bx"].reshape(RNN_BLOCKS, 1, RNN_BLOCK_W)
    lam = row(small["lru_lambda"])
    q_norm, kv_norm = row(small["q_norm"]), row(small["kv_norm"])
    norm_mix, norm_mlp, norm_final = row(small["norm_mix"]), row(small["norm_mlp"]), row(small["norm_final"])

    xn = _rmsnorm_cast(x, norm_mix, ts=ts, name="norm_mix")
    ident = lambda acc: (acc,)
    (z_main,) = _mm(xn, w_main, name="z_main", tm=tm, tn=1024, tk=1024, outs=[("tile", F32)], epilogue=ident)
    (z_ckv,) = _mm(xn, w_ckv, name="z_ckv", tm=tm, tn=CKV_W, tk=1024, outs=[("tile", F32)], epilogue=ident)
    tt = _pick(S, 256)
    h = _lru_fwd(z_main, conv_w, conv_b, wa, ba, wx, bx, lam, tt=tt)
    q, k, v = _mla_proj(z_ckv, q_norm, kv_norm, w_uq_p, w_ukv, cos, sin, ts=_pick(S, 256))
    ta = _pick(S, 512)
    o, lse = _attn_fwd(q, k, v, t=ta)
    merged = _merge_fwd(h, z_main, o, ts=_pick(S, 256))

    def ep_h1(acc, xv, g):
        h1 = acc + xv
        n2, _ = _rms_fwd(h1, g)
        return h1, n2

    h1, n2 = _mm(merged, w_out, name="h1", tm=tm, tn=1024, tk=1024, outs=[("tile", F32), ("tile", BF16)],
                 epilogue=ep_h1, extras=[("tile", x), ("row", norm_mlp)])

    def ep_up(acc):
        r = jnp.maximum(acc, 0.0)
        return r * r, r

    act, relu = _mm(n2, w_up, name="up", tm=tm, tn=1024, tk=1024, outs=[("tile", BF16), ("tile", BF16)],
                    epilogue=ep_up)

    def ep_loss(acc, h1v, tgt, g):
        h2 = acc + h1v
        y, _ = _rms_fwd(h2, g)
        err = y - tgt
        loss_rows = 0.5 * jnp.mean(err * err, axis=-1, keepdims=True)
        dy = err * (1.0 / D_MODEL)
        dh2, dg_rows = _rms_bwd(dy, h2, g)
        lsum = jnp.sum(loss_rows, axis=0, keepdims=True)
        return dh2, dh2, jnp.sum(dg_rows, axis=0, keepdims=True), jnp.broadcast_to(lsum, (1, D_MODEL))

    dh2, dh2b, dnf_p, loss_p = _mm(
        act, w_down, name="down_loss", tm=tm, tn=1024, tk=1024,
        outs=[("tile", F32), ("tile", BF16), ("rowpart", F32), ("rowpart", F32)], epilogue=ep_loss,
        extras=[("tile", h1), ("tile", target), ("row", norm_final)])
    loss_part = jnp.sum(loss_p[:, 0, 0])
    d_norm_final = jnp.sum(dnf_p, axis=(0, 1))

    w_down_t, w_up_t, w_out_t = w_down.T, w_up.T, w_out.T

    def ep_du(acc, r):
        return (acc * (2.0 * r.astype(F32)),)

    (du,) = _mm(dh2b, w_down_t, name="d_act", tm=tm, tn=1024, tk=1024, outs=[("tile", BF16)], epilogue=ep_du,
                extras=[("tile", relu)])

    def ep_dh1(acc, h1v, dh2v, g):
        dv, dg_rows = _rms_bwd(acc, h1v, g)
        dh1 = dh2v + dv
        return dh1, dh1, jnp.sum(dg_rows, axis=0, keepdims=True)

    dh1, dh1b, dnm_p = _mm(du, w_up_t, name="d_n2", tm=tm, tn=1024, tk=1024,
                           outs=[("tile", F32), ("tile", BF16), ("rowpart", F32)], epilogue=ep_dh1,
                           extras=[("tile", h1), ("tile", dh2), ("row", norm_mlp)])
    d_norm_mlp = jnp.sum(dnm_p, axis=(0, 1))
    tn_mm = functools.partial(_mm, ta=True, tk=tk_s, outs=[("tile", F32)], epilogue=ident)
    (d_w_down,) = tn_mm(act, dh2b, name="dw_down", tm=1024, tn=1024)
    (d_w_up,) = tn_mm(n2, du, name="dw_up", tm=1024, tn=1024)
    (d_w_out,) = tn_mm(merged, dh1b, name="dw_out", tm=1024, tn=1024)

    tmm = _pick(S, 256)

    def ep_dmerge(dm, hv, rg, ga, gb, ov):
        gl, dgl = _gelu_and_grad(rg)
        sa, sb = _sigmoid(ga), _sigmoid(gb)
        ya = hv * gl
        dya = dm * sa
        do = dm * sb
        dga = dm * ya * sa * (1.0 - sa)
        dgb = dm * ov * sb * (1.0 - sb)
        dh = dya * gl
        drg = dya * hv * dgl
        dov = do * ov
        lane = lax.broadcasted_iota(jnp.int32, (dm.shape[0], LANES), 1)
        delta = jnp.zeros((dm.shape[0], LANES), F32)
        for hh in range(H):
            dsum = jnp.sum(dov[:, hh * V_HEAD:(hh + 1) * V_HEAD], axis=1, keepdims=True)
            delta = jnp.where(lane == hh, dsum, delta)
        return dh, drg, dga, dgb, do, delta

    dh_lru, d_rg, d_ga, d_gb, do, delta_w = _mm(
        dh1b, w_out_t, name="d_merge", tm=tmm, tn=1024, tk=1024,
        outs=[("tile", F32), ("tile", BF16), ("tile", BF16), ("tile", BF16), ("tile", BF16), ("side", F32)],
        epilogue=ep_dmerge,
        extras=[("tile", h), ("tilecol", z_main, 1), ("tilecol", z_main, 2), ("tilecol", z_main, 3), ("tile", o)])
    delta_row = delta_w[:, :H].T.reshape(H, 1, S)
    lse_row = lse.reshape(H, 1, S)

    dq, dk, dv = _attn_bwd(q, k, v, do, lse_row, delta_row, t=ta)
    dz_ckv, d_w_uq_p, d_w_ukv, d_q_norm, d_kv_norm = _mla_proj_bwd(
        z_ckv, dq, dk, dv, q_norm, kv_norm, w_uq_p.T, w_ukv.T, cos, sin, ts=_pick(S, 256))
    d_w_uq = jnp.concatenate(
        [d_w_uq_p[:, :H * QK_NOPE].reshape(Q_LORA, H, QK_NOPE),
         d_w_uq_p[:, H * QK_NOPE:].reshape(Q_LORA, H, LANES)[:, :, :QK_ROPE]], axis=2).reshape(Q_LORA, -1)

    d_rx, d_wa, d_wx, d_ba, d_bx, d_lam, d_conv_w, d_conv_b = _lru_bwd(
        z_main, h, dh_lru, conv_w, conv_b, wa, wat, ba, wx, wxt, bx, lam, tt=tt)

    dz_main = jnp.concatenate([d_rx, d_rg, d_ga, d_gb], axis=1)
    (dxn_ckv,) = _mm(dz_ckv, w_ckv.T, name="dxn_ckv", tm=tm, tn=1024, tk=CKV_W, outs=[("tile", F32)], epilogue=ident)

    def ep_dx(acc, part, xv, dh1v, g):
        dv, dg_rows = _rms_bwd(acc + part, xv, g)
        return dh1v + dv, jnp.sum(dg_rows, axis=0, keepdims=True)

    grad_x, dnx_p = _mm(dz_main, w_main.T, name="dx", tm=tm, tn=1024, tk=1024,
                        outs=[("tile", F32), ("rowpart", F32)], epilogue=ep_dx,
                        extras=[("tile", dxn_ckv), ("tile", x), ("tile", dh1), ("row", norm_mix)])
    d_norm_mix = jnp.sum(dnx_p, axis=(0, 1))
    (d_w_main,) = tn_mm(xn, dz_main, name="dw_main", tm=1024, tn=1024)
    (d_w_ckv,) = tn_mm(xn, dz_ckv, name="dw_ckv", tm=1024, tn=CKV_W)
    d_w_in = jnp.concatenate([d_w_main[:, 0:2048], d_w_ckv[:, 0:576], d_w_main[:, 2048:4096]], axis=1)

    big = {"w_in": d_w_in, "w_uq": d_w_uq, "w_ukv": d_w_ukv, "w_out": d_w_out, "w_up": d_w_up, "w_down": d_w_down}
    sm = {"norm_mix": d_norm_mix, "conv_w": d_conv_w, "conv_b": d_conv_b.reshape(-1), "lru_wa": d_wa,
          "lru_ba": d_ba.reshape(RNN_BLOCKS, RNN_BLOCK_W), "lru_wx": d_wx, "lru_bx": d_bx.reshape(RNN_BLOCKS, RNN_BLOCK_W),
          "lru_lambda": d_lam.reshape(-1), "q_norm": d_q_norm.reshape(-1), "kv_norm": d_kv_norm.reshape(-1),
          "norm_mlp": d_norm_mlp, "norm_final": d_norm_final}
    return loss_part, grad_x, big, sm


BIG = ("w_in", "w_uq", "w_ukv", "w_out", "w_up", "w_down")
BIG_COL_SHARDED = {"w_in": True, "w_uq": True, "w_ukv": True, "w_out": False, "w_up": True, "w_down": False}
SMALL = ("norm_mix", "conv_b", "lru_wa", "lru_ba", "lru_wx", "lru_bx", "lru_lambda", "q_norm", "kv_norm", "norm_mlp",
         "norm_final")
WEIGHTS = ("norm_mix", "w_in", "conv_w", "conv_b", "lru_wa", "lru_ba", "lru_wx", "lru_bx", "lru_lambda", "q_norm", "w_uq",
           "kv_norm", "w_ukv", "w_out", "norm_mlp", "w_up", "w_down", "norm_final")
ADAM_ROWS = 512


def _rows(a):
    return a.reshape(-1, LANES)


def _pad_rows(a, mult):
    r = a.shape[-2]
    pad = (-r) % mult
    if pad == 0:
        return a
    cfg = [(0, 0)] * (a.ndim - 2) + [(0, pad), (0, 0)]
    return jnp.pad(a, cfg)


def _full_from_shards(g, shard_shape, col_sharded):
    r, c = shard_shape
    g = g.reshape(N_DEV, r, c)
    if col_sharded:
        return jnp.transpose(g, (1, 0, 2)).reshape(r, N_DEV * c)
    return g.reshape(N_DEV * r, c)


def _shards_from_full(full, shard_shape, col_sharded):
    r, c = shard_shape
    if col_sharded:
        parts = jnp.transpose(full.reshape(r, N_DEV, c), (1, 0, 2))
    else:
        parts = full.reshape(N_DEV, r, c)
    return parts.reshape(N_DEV, -1, LANES)


def kernel(x, norm_mix, w_in, conv_w, conv_b, lru_wa, lru_ba, lru_wx, lru_bx, lru_lambda, q_norm, w_uq, kv_norm, w_ukv, w_out, norm_mlp, w_up, w_down, norm_final, loss_target, m_norm_mix, m_w_in, m_conv_w, m_conv_b, m_lru_wa, m_lru_ba, m_lru_wx, m_lru_bx, m_lru_lambda, m_q_norm, m_w_uq, m_kv_norm, m_w_ukv, m_w_out, m_norm_mlp, m_w_up, m_w_down, m_norm_final, v_norm_mix, v_w_in, v_conv_w, v_conv_b, v_lru_wa, v_lru_ba, v_lru_wx, v_lru_bx, v_lru_lambda, v_q_norm, v_w_uq, v_kv_norm, v_w_ukv, v_w_out, v_norm_mlp, v_w_up, v_w_down, v_norm_final):
    W = dict(norm_mix=norm_mix, w_in=w_in, conv_w=conv_w, conv_b=conv_b, lru_wa=lru_wa, lru_ba=lru_ba, lru_wx=lru_wx,
             lru_bx=lru_bx, lru_lambda=lru_lambda, q_norm=q_norm, w_uq=w_uq, kv_norm=kv_norm, w_ukv=w_ukv, w_out=w_out,
             norm_mlp=norm_mlp, w_up=w_up, w_down=w_down, norm_final=norm_final)
    M = dict(norm_mix=m_norm_mix, w_in=m_w_in, conv_w=m_conv_w, conv_b=m_conv_b, lru_wa=m_lru_wa, lru_ba=m_lru_ba,
             lru_wx=m_lru_wx, lru_bx=m_lru_bx, lru_lambda=m_lru_lambda, q_norm=m_q_norm, w_uq=m_w_uq, kv_norm=m_kv_norm,
             w_ukv=m_w_ukv, w_out=m_w_out, norm_mlp=m_norm_mlp, w_up=m_w_up, w_down=m_w_down, norm_final=m_norm_final)
    V = dict(norm_mix=v_norm_mix, w_in=v_w_in, conv_w=v_conv_w, conv_b=v_conv_b, lru_wa=v_lru_wa, lru_ba=v_lru_ba,
             lru_wx=v_lru_wx, lru_bx=v_lru_bx, lru_lambda=v_lru_lambda, q_norm=v_q_norm, w_uq=v_w_uq, kv_norm=v_kv_norm,
             w_ukv=v_w_ukv, w_out=v_w_out, norm_mlp=v_norm_mlp, w_up=v_w_up, w_down=v_w_down, norm_final=v_norm_final)
    conv_bits = lax.bitcast_convert_type(conv_w, BF16).reshape(-1, LANES)
    pack = jnp.concatenate([_rows(W[n].astype(BF16)) for n in BIG] + [_pad_rows(conv_bits, 16)], axis=0)
    gathered = _all_gather(pack, name="gather_weights")
    wts, off = {}, 0
    for n in BIG:
        r = W[n].size // LANES
        wts[n] = _full_from_shards(gathered[:, off:off + r], W[n].shape, BIG_COL_SHARDED[n])
        off += r
    conv_full = lax.bitcast_convert_type(gathered[:, off:off + 8].reshape(N_DEV, CONV_WIDTH, LANES, 2), F32)
    conv_full = jnp.transpose(conv_full, (1, 0, 2)).reshape(CONV_WIDTH, D_MODEL)
    small = {n: W[n] for n in SMALL}
    small["conv_w"] = conv_full

    loss_part, grad_x, g_big, g_small = _local_step(x[0], loss_target[0], wts, small)

    conv_parts = _pad_rows(jnp.transpose(g_small["conv_w"].reshape(CONV_WIDTH, N_DEV, LANES), (1, 0, 2)), 8)
    parts = jnp.concatenate([_shards_from_full(g_big[n], W[n].shape, BIG_COL_SHARDED[n]) for n in BIG] + [conv_parts],
                            axis=1)
    parts = _pad_rows(parts, ADAM_ROWS)
    received = _all_to_all(parts, name="exchange_grads")
    sharded = list(BIG) + ["conv_w"]

    def pack_sharded(D):
        return _pad_rows(jnp.concatenate([_rows(D[n]) for n in BIG] + [_pad_rows(D["conv_w"], 8)], axis=0), ADAM_ROWS)

    g_s, d_s, nm_s, nv_s = _adamw(pack_sharded(W), pack_sharded(M), pack_sharded(V), received, tr=ADAM_ROWS,
                                  name="adamw_sharded")

    loss_row = jnp.zeros((1, LANES), F32).at[0, 0].set(loss_part)
    small_pack = _pad_rows(jnp.concatenate([_rows(g_small[n]) for n in SMALL] + [loss_row], axis=0), 8)
    small_all = _all_gather(small_pack, name="gather_small")

    def pack_small(D):
        return _pad_rows(jnp.concatenate([_rows(D[n]) for n in SMALL] + [jnp.zeros((1, LANES), F32)], axis=0), 8)

    g_r, d_r, nm_r, nv_r = _adamw(pack_small(W), pack_small(M), pack_small(V), small_all, tr=small_pack.shape[0],
                                  name="adamw_small")

    def unpack(ps, pr):
        out, off = {}, 0
        for n in sharded:
            r = 8 if n == "conv_w" else W[n].size // LANES
            out[n] = ps[off:off + r][:W[n].size // LANES].reshape(W[n].shape)
            off += r
        off = 0
        for n in SMALL:
            r = W[n].size // LANES
            out[n] = pr[off:off + r].reshape(W[n].shape)
            off += r
        return out

    G, Dl, NM, NV = unpack(g_s, g_r), unpack(d_s, d_r), unpack(nm_s, nm_r), unpack(nv_s, nv_r)
    loss = g_r[sum(W[n].size // LANES for n in SMALL), 0]
    return (loss, grad_x[None], *[G[n] for n in WEIGHTS], *[Dl[n] for n in WEIGHTS], *[NM[n] for n in WEIGHTS],
            *[NV[n] for n in WEIGHTS])
```

```python
import functools

import numpy as np
import jax
import jax.numpy as jnp
from jax import lax
from jax.experimental import pallas as pl
from jax.experimental.pallas import tpu as pltpu

F32 = jnp.float32
BF16 = jnp.bfloat16
MESH = pl.DeviceIdType.MESH

D_MODEL = 1024
N_DEV = 8
LANES = 128
RNN_BLOCKS = 8
RNN_BLOCK_W = 128
CONV_WIDTH = 4
LRU_C = 8.0
MLA_HEADS = 8
QK_NOPE = 128
QK_ROPE = 64
V_HEAD = 128
QK_PAD = 256
Q_LORA = 256
KV_LORA = 256
CKV_W = 640
ROPE_THETA = 10000.0
D_FF = 4096
EPS = 1e-6
ATTN_SCALE = (QK_NOPE + QK_ROPE) ** -0.5
LOG2E = 1.4426950408889634
LN2 = 0.6931471805599453
NEG = -1e30

ADAM_LR = 0.001
ADAM_B1 = 0.9
ADAM_B2 = 0.999
ADAM_EPS = 1e-08
ADAM_WD = 0.01
ADAM_STEP = 10

VMEM_LIMIT = 56 * 1024 * 1024


def _params(sem=None):
    return pltpu.CompilerParams(dimension_semantics=sem, vmem_limit_bytes=VMEM_LIMIT)


def _sigmoid(v):
    return 1.0 / (1.0 + jnp.exp(-v))


def _neg_expm1(y):
    u = jnp.exp(y)
    lu = jnp.log(u)
    safe = jnp.where(lu == 0.0, 1.0, lu)
    return jnp.where(lu == 0.0, -y, (1.0 - u) * y / safe)


def _softplus(y):
    e = jnp.exp(-jnp.abs(y))
    u = 1.0 + e
    d = u - 1.0
    l1p = jnp.where(d == 0.0, e, jnp.log(u) * e / jnp.where(d == 0.0, 1.0, d))
    return jnp.maximum(y, 0.0) + l1p


_GELU_K = 0.7978845608028654
_GELU_C = 0.044715


def _gelu_and_grad(v):
    t = jnp.tanh(_GELU_K * (v + _GELU_C * v * v * v))
    g = 0.5 * v * (1.0 + t)
    dg = 0.5 * (1.0 + t) + 0.5 * v * (1.0 - t * t) * _GELU_K * (1.0 + 3.0 * _GELU_C * v * v)
    return g, dg


def _rms_fwd(v, g):
    rstd = lax.rsqrt(jnp.mean(v * v, axis=-1, keepdims=True) + EPS)
    return v * rstd * g, rstd


def _rms_bwd(dy, v, g):
    rstd = lax.rsqrt(jnp.mean(v * v, axis=-1, keepdims=True) + EPS)
    vh = v * rstd
    dvh = dy * g
    dv = rstd * (dvh - vh * jnp.mean(dvh * vh, axis=-1, keepdims=True))
    return dv, dy * vh


def _shift_down(v, s, fill, row):
    return jnp.where(row >= s, pltpu.roll(v, s, 0), fill)


def _shift_up(v, s, fill, row, n):
    return jnp.where(row < n - s, pltpu.roll(v, n - s, 0), fill)


def _rot_half(v, lane):
    n = v.shape[-1]
    l = lane & (LANES - 1)
    up = pltpu.roll(v, n - QK_ROPE // 2, 1)
    dn = pltpu.roll(v, QK_ROPE // 2, 1)
    return jnp.where(l < QK_ROPE // 2, -up, jnp.where(l < QK_ROPE, dn, 0.0))


def _mm(a, b, *, name, tm, tn, tk, outs, epilogue, extras=(), ta=False):
    if ta:
        K, M = a.shape
    else:
        M, K = a.shape
    K2, N = b.shape
    assert K == K2 and M % tm == 0 and N % tn == 0 and K % tk == 0, (name, a.shape, b.shape)
    n_i, n_j, n_k = M // tm, N // tn, K // tk
    n_ex, n_out = len(extras), len(outs)

    def body(*refs):
        a_ref, b_ref = refs[0], refs[1]
        ex_refs = refs[2:2 + n_ex]
        out_refs = refs[2 + n_ex:2 + n_ex + n_out]
        if ta:
            part = lax.dot_general(a_ref[...], b_ref[...], (((0,), (0,)), ((), ())), preferred_element_type=F32)
        else:
            part = jnp.dot(a_ref[...], b_ref[...], preferred_element_type=F32)

        def finish(acc):
            res = epilogue(acc, *[r[...] for r in ex_refs])
            for o_ref, r in zip(out_refs, res):
                o_ref[...] = r.astype(o_ref.dtype).reshape(o_ref.shape)

        if n_k == 1:
            finish(part)
        else:
            acc_ref = refs[-1]
            k = pl.program_id(2)

            @pl.when(k == 0)
            def _():
                acc_ref[...] = part

            @pl.when(k > 0)
            def _():
                acc_ref[...] += part

            @pl.when(k == n_k - 1)
            def _():
                finish(acc_ref[...])

    a_spec = pl.BlockSpec((tk, tm), lambda j, i, k: (k, i)) if ta else pl.BlockSpec((tm, tk), lambda j, i, k: (i, k))
    b_once = dict(pipeline_mode=pl.Buffered(1)) if (n_j == 1 and n_k == 1) else {}
    in_specs = [a_spec, pl.BlockSpec((tk, tn), lambda j, i, k: (k, j), **b_once)]
    for ex in extras:
        kind = ex[0]
        if kind == "tile":
            in_specs.append(pl.BlockSpec((tm, tn), lambda j, i, k: (i, j)))
        elif kind == "tilecol":
            assert n_j == 1
            in_specs.append(pl.BlockSpec((tm, tn), functools.partial(lambda c, j, i, k: (i, c), ex[2])))
        else:
            in_specs.append(pl.BlockSpec((1, tn), lambda j, i, k: (0, j)))
    out_specs, out_shape = [], []
    for kind, dt in outs:
        if kind == "tile":
            out_specs.append(pl.BlockSpec((tm, tn), lambda j, i, k: (i, j)))
            out_shape.append(jax.ShapeDtypeStruct((M, N), dt))
        elif kind == "side":
            assert n_j == 1
            out_specs.append(pl.BlockSpec((tm, LANES), lambda j, i, k: (i, 0)))
            out_shape.append(jax.ShapeDtypeStruct((M, LANES), dt))
        else:
            out_specs.append(pl.BlockSpec((1, 1, tn), lambda j, i, k: (i, 0, j)))
            out_shape.append(jax.ShapeDtypeStruct((n_i, 1, N), dt))
    scratch = [pltpu.VMEM((tm, tn), F32)] if n_k > 1 else []
    return pl.pallas_call(
        body, name=name, grid=(n_j, n_i, n_k), in_specs=in_specs, out_specs=out_specs, out_shape=out_shape,
        scratch_shapes=scratch, compiler_params=_params(("parallel", "parallel", "arbitrary")),
    )(a, b, *[ex[1] for ex in extras])


def _rmsnorm_cast(x, g, *, ts, name):
    S, D = x.shape

    def body(x_ref, g_ref, o_ref):
        y, _ = _rms_fwd(x_ref[...], g_ref[...])
        o_ref[...] = y.astype(BF16)

    return pl.pallas_call(
        body, name=name, grid=(S // ts,),
        in_specs=[pl.BlockSpec((ts, D), lambda i: (i, 0)), pl.BlockSpec((1, D), lambda i: (0, 0))],
        out_specs=pl.BlockSpec((ts, D), lambda i: (i, 0)), out_shape=jax.ShapeDtypeStruct((S, D), BF16),
        compiler_params=_params(("parallel",)),
    )(x, g)


def _lru_gates(xa, wa, ba, wx, bx, lam):
    xab = xa.astype(BF16)
    r = _sigmoid(jnp.dot(xab, wa, preferred_element_type=F32) + ba)
    i = _sigmoid(jnp.dot(xab, wx, preferred_element_type=F32) + bx)
    sp = _softplus(-lam)
    log_a = (-LRU_C * r) * sp
    a = jnp.exp(log_a)
    mult = jnp.sqrt(_neg_expm1(2.0 * log_a))
    return r, i, sp, a, mult


def _lru_fwd(z_main, conv_w, conv_b, wa, ba, wx, bx, lam, *, tt):
    S = z_main.shape[0]
    n_t = S // tt
    W = RNN_BLOCK_W

    def body(x_ref, cw_ref, cb_ref, wa_ref, ba_ref, wx_ref, bx_ref, lam_ref, h_ref, ext, hc):
        t = pl.program_id(1)

        @pl.when(t == 0)
        def _():
            ext[0:8, :] = jnp.zeros((8, W), F32)
            hc[...] = jnp.zeros((8, W), F32)

        x = x_ref[...]
        ext[8:8 + tt, :] = x
        cw = cw_ref[...]
        xa = (cb_ref[...] + cw[3:4] * x + cw[2:3] * ext[7:7 + tt, :] + cw[1:2] * ext[6:6 + tt, :]
              + cw[0:1] * ext[5:5 + tt, :])
        ext[0:8, :] = x[tt - 8:tt, :]
        _r, i, _sp, a, mult = _lru_gates(xa, wa_ref[0], ba_ref[0], wx_ref[0], bx_ref[0], lam_ref[...])
        b = mult * (i * xa)
        row = lax.broadcasted_iota(jnp.int32, (tt, W), 0)
        A, B = a, b
        s = 1
        while s < tt:
            B = A * _shift_down(B, s, 0.0, row) + B
            A = A * _shift_down(A, s, 1.0, row)
            s *= 2
        h = A * hc[0:1, :] + B
        h_ref[...] = h
        hc[...] = jnp.broadcast_to(h[tt - 1:tt, :], (8, W))

    blk = lambda n, t: (t, n)
    vec = pl.BlockSpec((1, W), lambda n, t: (0, n))
    mat = pl.BlockSpec((1, W, W), lambda n, t: (n, 0, 0))
    bias = pl.BlockSpec((1, 1, W), lambda n, t: (n, 0, 0))
    return pl.pallas_call(
        body, name="lru_fwd", grid=(RNN_BLOCKS, n_t),
        in_specs=[pl.BlockSpec((tt, W), blk), pl.BlockSpec((CONV_WIDTH, W), lambda n, t: (0, n)), vec, mat, bias, mat,
                  bias, vec],
        out_specs=pl.BlockSpec((tt, W), blk), out_shape=jax.ShapeDtypeStruct((S, D_MODEL), F32),
        scratch_shapes=[pltpu.VMEM((tt + 8, W), F32), pltpu.VMEM((8, W), F32)],
        compiler_params=_params(("parallel", "arbitrary")),
    )(z_main, conv_w, conv_b, wa, ba, wx, bx, lam)


def _lru_bwd(z_main, h, dh, conv_w, conv_b, wa, wat, ba, wx, wxt, bx, lam, *, tt):
    S = z_main.shape[0]
    n_t = S // tt
    W = RNN_BLOCK_W
    t8 = tt // 8

    def body(x_ref, xp_ref, h_ref, hp_ref, dh_ref, cw_ref, cb_ref, wa_ref, wat_ref, ba_ref, wx_ref, wxt_ref, bx_ref,
             lam_ref, dx_ref, dwa_ref, dwx_ref, dba_ref, dbx_ref, dlam_ref, dcw_ref, dcb_ref, ext, dext, a_c, g_c):
        t = pl.program_id(1)
        tile = n_t - 1 - t

        @pl.when(t == 0)
        def _():
            a_c[...] = jnp.zeros((8, W), F32)
            g_c[...] = jnp.zeros((8, W), F32)
            dext[tt:tt + 8, :] = jnp.zeros((8, W), F32)
            dwa_ref[...] = jnp.zeros_like(dwa_ref)
            dwx_ref[...] = jnp.zeros_like(dwx_ref)
            dba_ref[...] = jnp.zeros_like(dba_ref)
            dbx_ref[...] = jnp.zeros_like(dbx_ref)
            dlam_ref[...] = jnp.zeros_like(dlam_ref)
            dcw_ref[...] = jnp.zeros_like(dcw_ref)
            dcb_ref[...] = jnp.zeros_like(dcb_ref)

        has_prev = (tile > 0).astype(F32)
        x = x_ref[...]
        ext[0:8, :] = xp_ref[...] * has_prev
        ext[8:8 + tt, :] = x
        xm1, xm2, xm3 = ext[7:7 + tt, :], ext[6:6 + tt, :], ext[5:5 + tt, :]
        cw = cw_ref[...]
        xa = cb_ref[...] + cw[3:4] * x + cw[2:3] * xm1 + cw[1:2] * xm2 + cw[0:1] * xm3
        lam = lam_ref[...]
        r, i, sp, a, mult = _lru_gates(xa, wa_ref[0], ba_ref[0], wx_ref[0], bx_ref[0], lam)
        gated = i * xa
        row = lax.broadcasted_iota(jnp.int32, (tt, W), 0)
        hcur = h_ref[...]
        h_prev = _shift_down(hcur, 1, hp_ref[7:8, :] * has_prev, row)
        C = _shift_up(a, 1, a_c[0:1, :], row, tt)
        G = dh_ref[...]
        s = 1
        while s < tt:
            G = G + C * _shift_up(G, s, 0.0, row, tt)
            C = C * _shift_up(C, s, 1.0, row, tt)
            s *= 2
        g = G + C * g_c[0:1, :]
        a_c[...] = jnp.broadcast_to(a[0:1, :], (8, W))
        g_c[...] = jnp.broadcast_to(g[0:1, :], (8, W))
        dlog_a = g * h_prev * a - g * gated * (a * a) / mult
        dgated = g * mult
        di = dgated * xa
        dxa = dgated * i
        dr = dlog_a * (-LRU_C * sp)
        dsp = jnp.sum(dlog_a * (-LRU_C * r), axis=0, keepdims=True)
        dlam_ref[0] += dsp * (-_sigmoid(-lam))
        dpr = dr * r * (1.0 - r)
        dpi = di * i * (1.0 - i)
        xab, dprb, dpib = xa.astype(BF16), dpr.astype(BF16), dpi.astype(BF16)
        tn_dims = (((0,), (0,)), ((), ()))
        dwa_ref[0] += lax.dot_general(xab, dprb, tn_dims, preferred_element_type=F32)
        dwx_ref[0] += lax.dot_general(xab, dpib, tn_dims, preferred_element_type=F32)
        dba_ref[0] += jnp.sum(dpr, axis=0, keepdims=True)
        dbx_ref[0] += jnp.sum(dpi, axis=0, keepdims=True)
        dxa = (dxa + jnp.dot(dprb, wat_ref[0], preferred_element_type=F32)
               + jnp.dot(dpib, wxt_ref[0], preferred_element_type=F32))
        dext[0:tt, :] = dxa
        dx = cw[3:4] * dxa + cw[2:3] * dext[1:1 + tt, :] + cw[1:2] * dext[2:2 + tt, :] + cw[0:1] * dext[3:3 + tt, :]
        dext[tt:tt + 8, :] = dxa[0:8, :]
        dx_ref[...] = dx.astype(BF16)
        dcw_ref[3:4, :] += jnp.sum(dxa * x, axis=0, keepdims=True)
        dcw_ref[2:3, :] += jnp.sum(dxa * xm1, axis=0, keepdims=True)
        dcw_ref[1:2, :] += jnp.sum(dxa * xm2, axis=0, keepdims=True)
        dcw_ref[0:1, :] += jnp.sum(dxa * xm3, axis=0, keepdims=True)
        dcb_ref[...] += jnp.sum(dxa, axis=0, keepdims=True)

    blk = lambda n, t: (n_t - 1 - t, n)
    prev = lambda n, t: (jnp.maximum((n_t - 1 - t) * t8 - 1, 0), n)
    vec = pl.BlockSpec((1, W), lambda n, t: (0, n))
    mat = pl.BlockSpec((1, W, W), lambda n, t: (n, 0, 0))
    bias = pl.BlockSpec((1, 1, W), lambda n, t: (n, 0, 0))
    cws = pl.BlockSpec((CONV_WIDTH, W), lambda n, t: (0, n))
    tile = pl.BlockSpec((tt, W), blk)
    prev8 = pl.BlockSpec((8, W), prev)
    return pl.pallas_call(
        body, name="lru_bwd", grid=(RNN_BLOCKS, n_t),
        in_specs=[tile, prev8, tile, prev8, tile, cws, vec, mat, mat, bias, mat, mat, bias, vec],
        out_specs=[tile, mat, mat, bias, bias, bias, cws, vec],
        out_shape=[jax.ShapeDtypeStruct((S, D_MODEL), BF16),
                   jax.ShapeDtypeStruct((RNN_BLOCKS, W, W), F32), jax.ShapeDtypeStruct((RNN_BLOCKS, W, W), F32),
                   jax.ShapeDtypeStruct((RNN_BLOCKS, 1, W), F32), jax.ShapeDtypeStruct((RNN_BLOCKS, 1, W), F32),
                   jax.ShapeDtypeStruct((RNN_BLOCKS, 1, W), F32),
                   jax.ShapeDtypeStruct((CONV_WIDTH, D_MODEL), F32), jax.ShapeDtypeStruct((1, D_MODEL), F32)],
        scratch_shapes=[pltpu.VMEM((tt + 8, W), F32), pltpu.VMEM((tt + 8, W), F32), pltpu.VMEM((8, W), F32),
                        pltpu.VMEM((8, W), F32)],
        compiler_params=_params(("parallel", "arbitrary")),
    )(z_main, z_main, h, h, dh, conv_w, conv_b, wa, wat, ba, wx, wxt, bx, lam)


def _mla_proj(z_ckv, q_norm, kv_norm, w_uq, w_ukv, cos, sin, *, ts):
    S = z_ckv.shape[0]
    H = MLA_HEADS

    def body(c_ref, qn_ref, kn_ref, wq_ref, wkv_ref, cos_ref, sin_ref, q_ref, k_ref, v_ref):
        c = c_ref[...]
        cqn, _ = _rms_fwd(c[:, 0:Q_LORA], qn_ref[...])
        ckn, _ = _rms_fwd(c[:, Q_LORA:Q_LORA + KV_LORA], kn_ref[...])
        q = jnp.dot(cqn.astype(BF16), wq_ref[...], preferred_element_type=F32) * (ATTN_SCALE * LOG2E)
        kv = jnp.dot(ckn.astype(BF16), wkv_ref[...], preferred_element_type=F32)
        cos1, sin1 = cos_ref[...], sin_ref[...]
        cos8 = jnp.concatenate([cos1] * H, axis=1)
        sin8 = jnp.concatenate([sin1] * H, axis=1)
        qr = q[:, H * QK_NOPE:]
        lane8 = lax.broadcasted_iota(jnp.int32, qr.shape, 1)
        qr = qr * cos8 + _rot_half(qr, lane8) * sin8
        kr = c[:, Q_LORA + KV_LORA:]
        lane1 = lax.broadcasted_iota(jnp.int32, kr.shape, 1)
        kr = (kr * cos1 + _rot_half(kr, lane1) * sin1).astype(BF16)
        for h in range(H):
            q_ref[h, :, 0:QK_NOPE] = q[:, h * QK_NOPE:(h + 1) * QK_NOPE].astype(BF16)
            q_ref[h, :, QK_NOPE:] = qr[:, h * LANES:(h + 1) * LANES].astype(BF16)
            k_ref[h, :, 0:QK_NOPE] = kv[:, h * 2 * LANES:h * 2 * LANES + LANES].astype(BF16)
            k_ref[h, :, QK_NOPE:] = kr
            v_ref[h] = kv[:, h * 2 * LANES + LANES:(h + 1) * 2 * LANES].astype(BF16)

    full = lambda shape: pl.BlockSpec(shape, lambda i: (0,) * len(shape))
    return pl.pallas_call(
        body, name="mla_proj", grid=(S // ts,),
        in_specs=[pl.BlockSpec((ts, CKV_W), lambda i: (i, 0)), full((1, Q_LORA)), full((1, KV_LORA)),
                  full(w_uq.shape), full(w_ukv.shape), pl.BlockSpec((ts, LANES), lambda i: (i, 0)),
                  pl.BlockSpec((ts, LANES), lambda i: (i, 0))],
        out_specs=[pl.BlockSpec((H, ts, QK_PAD), lambda i: (0, i, 0)), pl.BlockSpec((H, ts, QK_PAD), lambda i: (0, i, 0)),
                   pl.BlockSpec((H, ts, V_HEAD), lambda i: (0, i, 0))],
        out_shape=[jax.ShapeDtypeStruct((H, S, QK_PAD), BF16), jax.ShapeDtypeStruct((H, S, QK_PAD), BF16),
                   jax.ShapeDtypeStruct((H, S, V_HEAD), BF16)],
        compiler_params=_params(("parallel",)),
    )(z_ckv, q_norm, kv_norm, w_uq, w_ukv, cos, sin)


def _mla_proj_bwd(z_ckv, dq, dk, dv, q_norm, kv_norm, w_uqt, w_ukvt, cos, sin, *, ts):
    S = z_ckv.shape[0]
    H = MLA_HEADS

    def body(c_ref, dq_ref, dk_ref, dv_ref, qn_ref, kn_ref, wqt_ref, wkvt_ref, cos_ref, sin_ref,
             dz_ref, dwq_ref, dwkv_ref, dqn_ref, dkn_ref):
        @pl.when(pl.program_id(0) == 0)
        def _():
            dwq_ref[...] = jnp.zeros_like(dwq_ref)
            dwkv_ref[...] = jnp.zeros_like(dwkv_ref)
            dqn_ref[...] = jnp.zeros_like(dqn_ref)
            dkn_ref[...] = jnp.zeros_like(dkn_ref)

        c = c_ref[...]
        cq, ck = c[:, 0:Q_LORA], c[:, Q_LORA:Q_LORA + KV_LORA]
        qn, kn = qn_ref[...], kn_ref[...]
        cqn, _ = _rms_fwd(cq, qn)
        ckn, _ = _rms_fwd(ck, kn)
        cos1, sin1 = cos_ref[...], sin_ref[...]
        lane1 = lax.broadcasted_iota(jnp.int32, cos1.shape, 1)

        def unrope(g):
            return g * cos1 - _rot_half(g * sin1, lane1)

        dq_all = jnp.concatenate([dq_ref[h, :, 0:QK_NOPE] for h in range(H)]
                                 + [unrope(dq_ref[h, :, QK_NOPE:]) for h in range(H)], axis=1)
        dq_all = (dq_all * ATTN_SCALE).astype(BF16)
        dkv_all = jnp.concatenate([p for h in range(H) for p in (dk_ref[h, :, 0:QK_NOPE], dv_ref[h])],
                                  axis=1).astype(BF16)
        dkr = dk_ref[0, :, QK_NOPE:]
        for h in range(1, H):
            dkr = dkr + dk_ref[h, :, QK_NOPE:]
        dkr = unrope(dkr)
        tn_dims = (((0,), (0,)), ((), ()))
        dwq_ref[...] += lax.dot_general(cqn.astype(BF16), dq_all, tn_dims, preferred_element_type=F32)
        dwkv_ref[...] += lax.dot_general(ckn.astype(BF16), dkv_all, tn_dims, preferred_element_type=F32)
        dcqn = jnp.dot(dq_all, wqt_ref[...], preferred_element_type=F32)
        dckn = jnp.dot(dkv_all, wkvt_ref[...], preferred_element_type=F32)
        dcq, dqn_rows = _rms_bwd(dcqn, cq, qn)
        dck, dkn_rows = _rms_bwd(dckn, ck, kn)
        dqn_ref[...] += jnp.sum(dqn_rows, axis=0, keepdims=True)
        dkn_ref[...] += jnp.sum(dkn_rows, axis=0, keepdims=True)
        dz_ref[:, 0:Q_LORA] = dcq.astype(BF16)
        dz_ref[:, Q_LORA:Q_LORA + KV_LORA] = dck.astype(BF16)
        dz_ref[:, Q_LORA + KV_LORA:] = dkr.astype(BF16)

    full = lambda shape: pl.BlockSpec(shape, lambda i: (0,) * len(shape))
    return pl.pallas_call(
        body, name="mla_proj_bwd", grid=(S // ts,),
        in_specs=[pl.BlockSpec((ts, CKV_W), lambda i: (i, 0)), pl.BlockSpec((H, ts, QK_PAD), lambda i: (0, i, 0)),
                  pl.BlockSpec((H, ts, QK_PAD), lambda i: (0, i, 0)), pl.BlockSpec((H, ts, V_HEAD), lambda i: (0, i, 0)),
                  full((1, Q_LORA)), full((1, KV_LORA)), full(w_uqt.shape), full(w_ukvt.shape),
                  pl.BlockSpec((ts, LANES), lambda i: (i, 0)), pl.BlockSpec((ts, LANES), lambda i: (i, 0))],
        out_specs=[pl.BlockSpec((ts, CKV_W), lambda i: (i, 0)), full((Q_LORA, w_uqt.shape[0])),
                   full((KV_LORA, w_ukvt.shape[0])), full((1, Q_LORA)), full((1, KV_LORA))],
        out_shape=[jax.ShapeDtypeStruct((S, CKV_W), BF16), jax.ShapeDtypeStruct((Q_LORA, w_uqt.shape[0]), F32),
                   jax.ShapeDtypeStruct((KV_LORA, w_ukvt.shape[0]), F32), jax.ShapeDtypeStruct((1, Q_LORA), F32),
                   jax.ShapeDtypeStruct((1, KV_LORA), F32)],
        compiler_params=_params(("arbitrary",)),
    )(z_ckv, dq, dk, dv, q_norm, kv_norm, w_uqt, w_ukvt, cos, sin)


NT_DIMS = (((1,), (1,)), ((), ()))
TN_DIMS = (((0,), (0,)), ((), ()))


def _attn_fwd(q, k, v, *, t, hb):
    H, S, _ = q.shape
    n = S // t
    nc = t // LANES
    pairs = [(i, j) for i in range(n) for j in range(i + 1)]
    qi = jnp.asarray(np.array([p[0] for p in pairs], np.int32))
    ki = jnp.asarray(np.array([p[1] for p in pairs], np.int32))

    def body(qi_ref, ki_ref, q_ref, k_ref, v_ref, o_ref, lse_ref, m_s, l_s, acc_s):
        p = pl.program_id(1)
        i, j = qi_ref[p], ki_ref[p]

        @pl.when(j == 0)
        def _():
            m_s[...] = jnp.full(m_s.shape, NEG, F32)
            l_s[...] = jnp.zeros(l_s.shape, F32)
            acc_s[...] = jnp.zeros(acc_s.shape, F32)

        def step(masked):
            for hh in range(hb):
                s = lax.dot_general(q_ref[hh], k_ref[hh], NT_DIMS, preferred_element_type=F32)
                if masked:
                    row = lax.broadcasted_iota(jnp.int32, (t, t), 0)
                    col = lax.broadcasted_iota(jnp.int32, (t, t), 1)
                    s = jnp.where(row >= col, s, NEG)
                mc = s[:, 0:LANES]
                for c in range(1, nc):
                    mc = jnp.maximum(mc, s[:, c * LANES:(c + 1) * LANES])
                m_prev = m_s[hh]
                m_new = jnp.maximum(m_prev, jnp.max(mc, axis=1, keepdims=True))
                alpha = jnp.exp2(m_prev - m_new)
                pr = jnp.exp2(s - jnp.concatenate([m_new] * nc, axis=1))
                ls = pr[:, 0:LANES]
                for c in range(1, nc):
                    ls = ls + pr[:, c * LANES:(c + 1) * LANES]
                l_s[hh] = alpha * l_s[hh] + ls
                acc_s[hh] = alpha * acc_s[hh] + jnp.dot(pr.astype(BF16), v_ref[hh], preferred_element_type=F32)
                m_s[hh] = m_new

        @pl.when(j < i)
        def _():
            step(False)

        @pl.when(j == i)
        def _():
            step(True)
            for hh in range(hb):
                l = jnp.sum(l_s[hh], axis=1, keepdims=True)
                o_ref[:, hh * V_HEAD:(hh + 1) * V_HEAD] = acc_s[hh] / l
                lse_ref[hh] = m_s[hh][:, 0:1] + jnp.log2(l)

    grid_spec = pltpu.PrefetchScalarGridSpec(
        num_scalar_prefetch=2, grid=(H // hb, len(pairs)),
        in_specs=[pl.BlockSpec((hb, t, QK_PAD), lambda h, p, qi, ki: (h, qi[p], 0)),
                  pl.BlockSpec((hb, t, QK_PAD), lambda h, p, qi, ki: (h, ki[p], 0)),
                  pl.BlockSpec((hb, t, V_HEAD), lambda h, p, qi, ki: (h, ki[p], 0))],
        out_specs=[pl.BlockSpec((t, hb * V_HEAD), lambda h, p, qi, ki: (qi[p], h)),
                   pl.BlockSpec((hb, t, 1), lambda h, p, qi, ki: (h, qi[p], 0))],
        scratch_shapes=[pltpu.VMEM((hb, t, LANES), F32), pltpu.VMEM((hb, t, LANES), F32),
                        pltpu.VMEM((hb, t, V_HEAD), F32)],
    )
    return pl.pallas_call(
        body, name="attn_fwd", grid_spec=grid_spec,
        out_shape=[jax.ShapeDtypeStruct((S, H * V_HEAD), F32), jax.ShapeDtypeStruct((H, S, 1), F32)],
        compiler_params=_params(("parallel", "arbitrary")),
    )(qi, ki, q, k, v)


def _attn_bwd(q, k, v, do, lse_row, delta_row, *, t):
    H, S, _ = q.shape
    n = S // t
    pairs = [(i, j) for j in range(n) for i in range(j, n)]
    qi = jnp.asarray(np.array([p[0] for p in pairs], np.int32))
    ki = jnp.asarray(np.array([p[1] for p in pairs], np.int32))

    def body(qi_ref, ki_ref, q_ref, k_ref, v_ref, do_ref, lse_ref, dl_ref, dq_ref, dk_ref, dv_ref, dk_s, dv_s):
        p = pl.program_id(1)
        i, j = qi_ref[p], ki_ref[p]

        @pl.when(p == 0)
        def _():
            dq_ref[...] = jnp.zeros_like(dq_ref)

        def step(masked):
            qb, kb, vb, dob = q_ref[0], k_ref[0], v_ref[0], do_ref[...]
            st = lax.dot_general(kb, qb, NT_DIMS, preferred_element_type=F32)
            if masked:
                krow = lax.broadcasted_iota(jnp.int32, (t, t), 0)
                qcol = lax.broadcasted_iota(jnp.int32, (t, t), 1)
                st = jnp.where(krow <= qcol, st, NEG)
            pt = jnp.exp2(st - lse_ref[0])
            dvp = jnp.dot(pt.astype(BF16), dob, preferred_element_type=F32)
            dpt = lax.dot_general(vb, dob, NT_DIMS, preferred_element_type=F32)
            dst = (pt * (dpt - dl_ref[0])).astype(BF16)
            dkp = jnp.dot(dst, qb, preferred_element_type=F32)
            rows = pl.ds(pl.multiple_of(i * t, t), t)
            dq_ref[0, rows, :] += lax.dot_general(dst, kb, TN_DIMS, preferred_element_type=F32)
            return dkp, dvp

        @pl.when(i == j)
        def _():
            dkp, dvp = step(True)
            dk_s[...] = dkp
            dv_s[...] = dvp

        @pl.when(i != j)
        def _():
            dkp, dvp = step(False)
            dk_s[...] += dkp
            dv_s[...] += dvp

        @pl.when(i == n - 1)
        def _():
            dk_ref[0] = dk_s[...] * LN2
            dv_ref[0] = dv_s[...]

    grid_spec = pltpu.PrefetchScalarGridSpec(
        num_scalar_prefetch=2, grid=(H, len(pairs)),
        in_specs=[pl.BlockSpec((1, t, QK_PAD), lambda h, p, qi, ki: (h, qi[p], 0)),
                  pl.BlockSpec((1, t, QK_PAD), lambda h, p, qi, ki: (h, ki[p], 0)),
                  pl.BlockSpec((1, t, V_HEAD), lambda h, p, qi, ki: (h, ki[p], 0)),
                  pl.BlockSpec((t, V_HEAD), lambda h, p, qi, ki: (qi[p], h)),
                  pl.BlockSpec((1, 1, t), lambda h, p, qi, ki: (h, 0, qi[p])),
                  pl.BlockSpec((1, 1, t), lambda h, p, qi, ki: (h, 0, qi[p]))],
        out_specs=[pl.BlockSpec((1, S, QK_PAD), lambda h, p, qi, ki: (h, 0, 0)),
                   pl.BlockSpec((1, t, QK_PAD), lambda h, p, qi, ki: (h, ki[p], 0)),
                   pl.BlockSpec((1, t, V_HEAD), lambda h, p, qi, ki: (h, ki[p], 0))],
        scratch_shapes=[pltpu.VMEM((t, QK_PAD), F32), pltpu.VMEM((t, V_HEAD), F32)],
    )
    return pl.pallas_call(
        body, name="attn_bwd", grid_spec=grid_spec,
        out_shape=[jax.ShapeDtypeStruct((H, S, QK_PAD), F32), jax.ShapeDtypeStruct((H, S, QK_PAD), F32),
                   jax.ShapeDtypeStruct((H, S, V_HEAD), F32)],
        compiler_params=_params(("parallel", "arbitrary")),
    )(qi, ki, q, k, v, do, lse_row, delta_row)


def _merge_fwd(h, z_main, o, *, ts):
    S = h.shape[0]
    D = D_MODEL

    def body(h_ref, rg_ref, ga_ref, gb_ref, o_ref, m_ref):
        gl, _ = _gelu_and_grad(rg_ref[...])
        m = _sigmoid(ga_ref[...]) * (h_ref[...] * gl) + _sigmoid(gb_ref[...]) * o_ref[...]
        m_ref[...] = m.astype(BF16)

    col = lambda c: pl.BlockSpec((ts, D), lambda i: (i, c))
    return pl.pallas_call(
        body, name="merge_fwd", grid=(S // ts,),
        in_specs=[col(0), col(1), col(2), col(3), col(0)],
        out_specs=col(0), out_shape=jax.ShapeDtypeStruct((S, D), BF16),
        compiler_params=_params(("parallel",)),
    )(h, z_main, z_main, z_main, o)


def _my_place():
    return lax.axis_index("x"), lax.axis_index("y"), lax.axis_index("c")


def _all_gather(shard, *, name):
    R, C = shard.shape

    def body(x_ref, out_ref, send_sems, recv_sems, local_sem):
        x, y, c = _my_place()
        me, sibling = (x, y, c), (x, y, 1 - c)
        chips = [(1 - x, y), (x, 1 - y), (1 - x, 1 - y)]

        def slot(px, py, pc):
            return out_ref.at[4 * px + 2 * py + pc]

        def copy(k, block, to, src=None):
            return pltpu.make_async_remote_copy(
                src_ref=slot(*block) if src is None else src, dst_ref=slot(*block),
                send_sem=send_sems.at[k], recv_sem=recv_sems.at[k], device_id=to, device_id_type=MESH)

        mine = pltpu.make_async_copy(x_ref, slot(*me), local_sem)
        mine.start()
        first = [copy(0, me, sibling, src=x_ref)]
        first += [copy(1 + j, me, (*chip, c), src=x_ref) for j, chip in enumerate(chips)]
        for cp in first:
            cp.start()
        passed = [copy(4 + j, (*chip, c), sibling) for j, chip in enumerate(chips)]
        for j, chip in enumerate(chips):
            copy(1 + j, (*chip, c), me).wait_recv()
            passed[j].start()
        copy(0, sibling, me).wait_recv()
        for j, chip in enumerate(chips):
            copy(4 + j, (*chip, 1 - c), me).wait_recv()
        for cp in first + passed:
            cp.wait_send()
        mine.wait()

    return pl.pallas_call(
        body, name=name, out_shape=jax.ShapeDtypeStruct((N_DEV, R, C), shard.dtype),
        in_specs=[pl.BlockSpec(memory_space=pl.ANY)], out_specs=pl.BlockSpec(memory_space=pl.ANY),
        scratch_shapes=[pltpu.SemaphoreType.DMA((7,)), pltpu.SemaphoreType.DMA((7,)), pltpu.SemaphoreType.DMA],
    )(shard)


def _all_to_all(parts, *, name):
    _, R, C = parts.shape

    def body(p_ref, out_ref, send_sems, recv_sems, local_sem):
        x, y, c = _my_place()
        me = 4 * x + 2 * y + c
        mine = pltpu.make_async_copy(p_ref.at[me], out_ref.at[me], local_sem)
        mine.start()
        copies = []
        for k in range(1, N_DEV):
            px, py, pc = x ^ (k >> 2), y ^ ((k >> 1) & 1), c ^ (k & 1)
            copies.append(pltpu.make_async_remote_copy(
                src_ref=p_ref.at[4 * px + 2 * py + pc], dst_ref=out_ref.at[me],
                send_sem=send_sems.at[k - 1], recv_sem=recv_sems.at[k - 1], device_id=(px, py, pc),
                device_id_type=MESH))
        for cp in copies:
            cp.start()
        for cp in copies:
            cp.wait_recv()
        for cp in copies:
            cp.wait_send()
        mine.wait()

    return pl.pallas_call(
        body, name=name, out_shape=jax.ShapeDtypeStruct(parts.shape, parts.dtype),
        in_specs=[pl.BlockSpec(memory_space=pl.ANY)], out_specs=pl.BlockSpec(memory_space=pl.ANY),
        scratch_shapes=[pltpu.SemaphoreType.DMA((7,)), pltpu.SemaphoreType.DMA((7,)), pltpu.SemaphoreType.DMA],
    )(parts)


def _adamw(w, m, v, gparts, *, tr, name):
    R = w.shape[0]

    def body(w_ref, m_ref, v_ref, gp_ref, g_ref, d_ref, nm_ref, nv_ref):
        g = gp_ref[0].astype(F32)
        for p in range(1, N_DEV):
            g = g + gp_ref[p].astype(F32)
        wv = w_ref[...]
        m_new = ADAM_B1 * m_ref[...] + (1.0 - ADAM_B1) * g
        v_new = ADAM_B2 * v_ref[...] + (1.0 - ADAM_B2) * (g * g)
        m_hat = m_new / (1.0 - ADAM_B1 ** ADAM_STEP)
        v_hat = v_new / (1.0 - ADAM_B2 ** ADAM_STEP)
        g_ref[...] = g
        d_ref[...] = -ADAM_LR * (m_hat / (jnp.sqrt(v_hat) + ADAM_EPS) + ADAM_WD * wv)
        nm_ref[...] = m_new
        nv_ref[...] = v_new

    row = pl.BlockSpec((tr, LANES), lambda i: (i, 0))
    shp = jax.ShapeDtypeStruct((R, LANES), F32)
    return pl.pallas_call(
        body, name=name, grid=(R // tr,),
        in_specs=[row, row, row, pl.BlockSpec((N_DEV, tr, LANES), lambda i: (0, i, 0))],
        out_specs=[row, row, row, row], out_shape=[shp, shp, shp, shp],
        compiler_params=_params(("parallel",)),
    )(w, m, v, gparts)


def _rope_tables(s):
    pos = jnp.arange(s, dtype=F32)
    inv_freq = 1.0 / (ROPE_THETA ** (jnp.arange(0, QK_ROPE, 2, dtype=F32) / QK_ROPE))
    ang = pos[:, None] * inv_freq[None, :]
    cos, sin = jnp.cos(ang), jnp.sin(ang)
    zero = jnp.zeros((s, LANES - QK_ROPE), F32)
    return jnp.concatenate([cos, cos, zero], -1), jnp.concatenate([sin, sin, zero], -1)


def _pick(n, want):
    t = min(n, want)
    assert n % t == 0
    return t


def _local_step(x, target, wts, small):
    S = x.shape[0]
    H = MLA_HEADS
    ts = _pick(S, 512)
    tm = _pick(S, 512)
    tk_s = _pick(S, 2048)
    row = lambda v: v.reshape(1, -1)
    w_in = wts["w_in"]
    w_main = jnp.concatenate([w_in[:, 0:2048], w_in[:, 2624:4672]], axis=1)
    w_ckv = jnp.concatenate([w_in[:, 2048:2624], jnp.zeros((D_MODEL, CKV_W - 576), BF16)], axis=1)
    w_uq3 = wts["w_uq"].reshape(Q_LORA, H, QK_NOPE + QK_ROPE)
    w_uq_p = jnp.concatenate(
        [w_uq3[:, :, :QK_NOPE].reshape(Q_LORA, H * QK_NOPE),
         jnp.pad(w_uq3[:, :, QK_NOPE:], ((0, 0), (0, 0), (0, LANES - QK_ROPE))).reshape(Q_LORA, H * LANES)], axis=1)
    w_ukv, w_out, w_up, w_down = wts["w_ukv"], wts["w_out"], wts["w_up"], wts["w_down"]
    cos, sin = _rope_tables(S)
    conv_w, conv_b = small["conv_w"], row(small["conv_b"])
    wa, wx = small["lru_wa"].astype(BF16), small["lru_wx"].astype(BF16)
    wat, wxt = jnp.swapaxes(wa, 1, 2), jnp.swapaxes(wx, 1, 2)
    ba, bx = small["lru_ba"].reshape(RNN_BLOCKS, 1, RNN_BLOCK_W), small["lru_bx"].reshape(RNN_BLOCKS, 1, RNN_BLOCK_W)
    lam = row(small["lru_lambda"])
    q_norm, kv_norm = row(small["q_norm"]), row(small["kv_norm"])
    norm_mix, norm_mlp, norm_final = row(small["norm_mix"]), row(small["norm_mlp"]), row(small["norm_final"])

    xn = _rmsnorm_cast(x, norm_mix, ts=ts, name="norm_mix")
    ident = lambda acc: (acc,)
    (z_main,) = _mm(xn, w_main, name="z_main", tm=tm, tn=1024, tk=1024, outs=[("tile", F32)], epilogue=ident)
    (z_ckv,) = _mm(xn, w_ckv, name="z_ckv", tm=tm, tn=CKV_W, tk=1024, outs=[("tile", F32)], epilogue=ident)
    tt = _pick(S, 256)
    h = _lru_fwd(z_main, conv_w, conv_b, wa, ba, wx, bx, lam, tt=tt)
    q, k, v = _mla_proj(z_ckv, q_norm, kv_norm, w_uq_p, w_ukv, cos, sin, ts=_pick(S, 256))
    ta = _pick(S, 1024)
    o, lse = _attn_fwd(q, k, v, t=ta, hb=2)
    merged = _merge_fwd(h, z_main, o, ts=_pick(S, 256))

    def ep_h1(acc, xv, g):
        h1 = acc + xv
        n2, _ = _rms_fwd(h1, g)
        return h1, n2

    h1, n2 = _mm(merged, w_out, name="h1", tm=tm, tn=1024, tk=1024, outs=[("tile", F32), ("tile", BF16)],
                 epilogue=ep_h1, extras=[("tile", x), ("row", norm_mlp)])

    def ep_up(acc):
        r = jnp.maximum(acc, 0.0)
        return r * r, r

    act, relu = _mm(n2, w_up, name="up", tm=tm, tn=1024, tk=1024, outs=[("tile", BF16), ("tile", BF16)],
                    epilogue=ep_up)

    def ep_loss(acc, h1v, tgt, g):
        h2 = acc + h1v
        y, _ = _rms_fwd(h2, g)
        err = y - tgt
        loss_rows = 0.5 * jnp.mean(err * err, axis=-1, keepdims=True)
        dy = err * (1.0 / D_MODEL)
        dh2, dg_rows = _rms_bwd(dy, h2, g)
        lsum = jnp.sum(loss_rows, axis=0, keepdims=True)
        return dh2, dh2, jnp.sum(dg_rows, axis=0, keepdims=True), jnp.broadcast_to(lsum, (1, D_MODEL))

    dh2, dh2b, dnf_p, loss_p = _mm(
        act, w_down, name="down_loss", tm=tm, tn=1024, tk=D_FF,
        outs=[("tile", F32), ("tile", BF16), ("rowpart", F32), ("rowpart", F32)], epilogue=ep_loss,
        extras=[("tile", h1), ("tile", target), ("row", norm_final)])
    loss_part = jnp.sum(loss_p[:, 0, 0])
    d_norm_final = jnp.sum(dnf_p, axis=(0, 1))

    w_down_t, w_up_t, w_out_t = w_down.T, w_up.T, w_out.T

    def ep_du(acc, r):
        return (acc * (2.0 * r.astype(F32)),)

    (du,) = _mm(dh2b, w_down_t, name="d_act", tm=tm, tn=1024, tk=1024, outs=[("tile", BF16)], epilogue=ep_du,
                extras=[("tile", relu)])

    def ep_dh1(acc, h1v, dh2v, g):
        dv, dg_rows = _rms_bwd(acc, h1v, g)
        dh1 = dh2v + dv
        return dh1, dh1, jnp.sum(dg_rows, axis=0, keepdims=True)

    dh1, dh1b, dnm_p = _mm(du, w_up_t, name="d_n2", tm=tm, tn=1024, tk=D_FF,
                           outs=[("tile", F32), ("tile", BF16), ("rowpart", F32)], epilogue=ep_dh1,
                           extras=[("tile", h1), ("tile", dh2), ("row", norm_mlp)])
    d_norm_mlp = jnp.sum(dnm_p, axis=(0, 1))
    tn_mm = functools.partial(_mm, ta=True, tk=tk_s, outs=[("tile", F32)], epilogue=ident)
    (d_w_down,) = tn_mm(act, dh2b, name="dw_down", tm=1024, tn=1024)
    (d_w_up,) = tn_mm(n2, du, name="dw_up", tm=1024, tn=1024)
    (d_w_out,) = tn_mm(merged, dh1b, name="dw_out", tm=1024, tn=1024)

    tmm = _pick(S, 256)

    def ep_dmerge(dm, hv, rg, ga, gb, ov):
        gl, dgl = _gelu_and_grad(rg)
        sa, sb = _sigmoid(ga), _sigmoid(gb)
        ya = hv * gl
        dya = dm * sa
        do = dm * sb
        dga = dm * ya * sa * (1.0 - sa)
        dgb = dm * ov * sb * (1.0 - sb)
        dh = dya * gl
        drg = dya * hv * dgl
        dov = do * ov
        lane = lax.broadcasted_iota(jnp.int32, (dm.shape[0], LANES), 1)
        delta = jnp.zeros((dm.shape[0], LANES), F32)
        for hh in range(H):
            dsum = jnp.sum(dov[:, hh * V_HEAD:(hh + 1) * V_HEAD], axis=1, keepdims=True)
            delta = jnp.where(lane == hh, dsum, delta)
        return dh, drg, dga, dgb, do, delta

    dh_lru, d_rg, d_ga, d_gb, do, delta_w = _mm(
        dh1b, w_out_t, name="d_merge", tm=tmm, tn=1024, tk=1024,
        outs=[("tile", F32), ("tile", BF16), ("tile", BF16), ("tile", BF16), ("tile", BF16), ("side", F32)],
        epilogue=ep_dmerge,
        extras=[("tile", h), ("tilecol", z_main, 1), ("tilecol", z_main, 2), ("tilecol", z_main, 3), ("tile", o)])
    delta_row = delta_w[:, :H].T.reshape(H, 1, S)
    lse_row = lse.reshape(H, 1, S)

    dq, dk, dv = _attn_bwd(q, k, v, do, lse_row, delta_row, t=ta)
    dz_ckv, d_w_uq_p, d_w_ukv, d_q_norm, d_kv_norm = _mla_proj_bwd(
        z_ckv, dq, dk, dv, q_norm, kv_norm, w_uq_p.T, w_ukv.T, cos, sin, ts=_pick(S, 256))
    d_w_uq = jnp.concatenate(
        [d_w_uq_p[:, :H * QK_NOPE].reshape(Q_LORA, H, QK_NOPE),
         d_w_uq_p[:, H * QK_NOPE:].reshape(Q_LORA, H, LANES)[:, :, :QK_ROPE]], axis=2).reshape(Q_LORA, -1)

    d_rx, d_wa, d_wx, d_ba, d_bx, d_lam, d_conv_w, d_conv_b = _lru_bwd(
        z_main, h, dh_lru, conv_w, conv_b, wa, wat, ba, wx, wxt, bx, lam, tt=tt)

    dz_main = jnp.concatenate([d_rx, d_rg, d_ga, d_gb], axis=1)
    (dxn_ckv,) = _mm(dz_ckv, w_ckv.T, name="dxn_ckv", tm=tm, tn=1024, tk=CKV_W, outs=[("tile", F32)], epilogue=ident)

    def ep_dx(acc, part, xv, dh1v, g):
        dv, dg_rows = _rms_bwd(acc + part, xv, g)
        return dh1v + dv, jnp.sum(dg_rows, axis=0, keepdims=True)

    grad_x, dnx_p = _mm(dz_main, w_main.T, name="dx", tm=tm, tn=1024, tk=4 * D_MODEL,
                        outs=[("tile", F32), ("rowpart", F32)], epilogue=ep_dx,
                        extras=[("tile", dxn_ckv), ("tile", x), ("tile", dh1), ("row", norm_mix)])
    d_norm_mix = jnp.sum(dnx_p, axis=(0, 1))
    (d_w_main,) = tn_mm(xn, dz_main, name="dw_main", tm=1024, tn=1024)
    (d_w_ckv,) = tn_mm(xn, dz_ckv, name="dw_ckv", tm=1024, tn=CKV_W)
    d_w_in = jnp.concatenate([d_w_main[:, 0:2048], d_w_ckv[:, 0:576], d_w_main[:, 2048:4096]], axis=1)

    big = {"w_in": d_w_in, "w_uq": d_w_uq, "w_ukv": d_w_ukv, "w_out": d_w_out, "w_up": d_w_up, "w_down": d_w_down}
    sm = {"norm_mix": d_norm_mix, "conv_w": d_conv_w, "conv_b": d_conv_b.reshape(-1), "lru_wa": d_wa,
          "lru_ba": d_ba.reshape(RNN_BLOCKS, RNN_BLOCK_W), "lru_wx": d_wx, "lru_bx": d_bx.reshape(RNN_BLOCKS, RNN_BLOCK_W),
          "lru_lambda": d_lam.reshape(-1), "q_norm": d_q_norm.reshape(-1), "kv_norm": d_kv_norm.reshape(-1),
          "norm_mlp": d_norm_mlp, "norm_final": d_norm_final}
    return loss_part, grad_x, big, sm


BIG = ("w_in", "w_uq", "w_ukv", "w_out", "w_up", "w_down")
BIG_COL_SHARDED = {"w_in": True, "w_uq": True, "w_ukv": True, "w_out": False, "w_up": True, "w_down": False}
SMALL = ("norm_mix", "conv_b", "lru_wa", "lru_ba", "lru_wx", "lru_bx", "lru_lambda", "q_norm", "kv_norm", "norm_mlp",
         "norm_final")
WEIGHTS = ("norm_mix", "w_in", "conv_w", "conv_b", "lru_wa", "lru_ba", "lru_wx", "lru_bx", "lru_lambda", "q_norm", "w_uq",
           "kv_norm", "w_ukv", "w_out", "norm_mlp", "w_up", "w_down", "norm_final")
ADAM_ROWS = 512


def _rows(a):
    return a.reshape(-1, LANES)


def _pad_rows(a, mult):
    r = a.shape[-2]
    pad = (-r) % mult
    if pad == 0:
        return a
    cfg = [(0, 0)] * (a.ndim - 2) + [(0, pad), (0, 0)]
    return jnp.pad(a, cfg)


def _full_from_shards(g, shard_shape, col_sharded):
    r, c = shard_shape
    g = g.reshape(N_DEV, r, c)
    if col_sharded:
        return jnp.transpose(g, (1, 0, 2)).reshape(r, N_DEV * c)
    return g.reshape(N_DEV * r, c)


def _shards_from_full(full, shard_shape, col_sharded):
    r, c = shard_shape
    if col_sharded:
        parts = jnp.transpose(full.reshape(r, N_DEV, c), (1, 0, 2))
    else:
        parts = full.reshape(N_DEV, r, c)
    return parts.reshape(N_DEV, -1, LANES)


def kernel(x, norm_mix, w_in, conv_w, conv_b, lru_wa, lru_ba, lru_wx, lru_bx, lru_lambda, q_norm, w_uq, kv_norm, w_ukv, w_out, norm_mlp, w_up, w_down, norm_final, loss_target, m_norm_mix, m_w_in, m_conv_w, m_conv_b, m_lru_wa, m_lru_ba, m_lru_wx, m_lru_bx, m_lru_lambda, m_q_norm, m_w_uq, m_kv_norm, m_w_ukv, m_w_out, m_norm_mlp, m_w_up, m_w_down, m_norm_final, v_norm_mix, v_w_in, v_conv_w, v_conv_b, v_lru_wa, v_lru_ba, v_lru_wx, v_lru_bx, v_lru_lambda, v_q_norm, v_w_uq, v_kv_norm, v_w_ukv, v_w_out, v_norm_mlp, v_w_up, v_w_down, v_norm_final):
    W = dict(norm_mix=norm_mix, w_in=w_in, conv_w=conv_w, conv_b=conv_b, lru_wa=lru_wa, lru_ba=lru_ba, lru_wx=lru_wx,
             lru_bx=lru_bx, lru_lambda=lru_lambda, q_norm=q_norm, w_uq=w_uq, kv_norm=kv_norm, w_ukv=w_ukv, w_out=w_out,
             norm_mlp=norm_mlp, w_up=w_up, w_down=w_down, norm_final=norm_final)
    M = dict(norm_mix=m_norm_mix, w_in=m_w_in, conv_w=m_conv_w, conv_b=m_conv_b, lru_wa=m_lru_wa, lru_ba=m_lru_ba,
             lru_wx=m_lru_wx, lru_bx=m_lru_bx, lru_lambda=m_lru_lambda, q_norm=m_q_norm, w_uq=m_w_uq, kv_norm=m_kv_norm,
             w_ukv=m_w_ukv, w_out=m_w_out, norm_mlp=m_norm_mlp, w_up=m_w_up, w_down=m_w_down, norm_final=m_norm_final)
    V = dict(norm_mix=v_norm_mix, w_in=v_w_in, conv_w=v_conv_w, conv_b=v_conv_b, lru_wa=v_lru_wa, lru_ba=v_lru_ba,
             lru_wx=v_lru_wx, lru_bx=v_lru_bx, lru_lambda=v_lru_lambda, q_norm=v_q_norm, w_uq=v_w_uq, kv_norm=v_kv_norm,
             w_ukv=v_w_ukv, w_out=v_w_out, norm_mlp=v_norm_mlp, w_up=v_w_up, w_down=v_w_down, norm_final=v_norm_final)
    conv_bits = lax.bitcast_convert_type(conv_w, BF16).reshape(-1, LANES)
    pack = jnp.concatenate([_rows(W[n].astype(BF16)) for n in BIG] + [_pad_rows(conv_bits, 16)], axis=0)
    gathered = _all_gather(pack, name="gather_weights")
    wts, off = {}, 0
    for n in BIG:
        r = W[n].size // LANES
        wts[n] = _full_from_shards(gathered[:, off:off + r], W[n].shape, BIG_COL_SHARDED[n])
        off += r
    conv_full = lax.bitcast_convert_type(gathered[:, off:off + 8].reshape(N_DEV, CONV_WIDTH, LANES, 2), F32)
    conv_full = jnp.transpose(conv_full, (1, 0, 2)).reshape(CONV_WIDTH, D_MODEL)
    small = {n: W[n] for n in SMALL}
    small["conv_w"] = conv_full

    loss_part, grad_x, g_big, g_small = _local_step(x[0], loss_target[0], wts, small)

    conv_parts = _pad_rows(jnp.transpose(g_small["conv_w"].reshape(CONV_WIDTH, N_DEV, LANES), (1, 0, 2)), 8)
    parts = jnp.concatenate([_shards_from_full(g_big[n], W[n].shape, BIG_COL_SHARDED[n]) for n in BIG] + [conv_parts],
                            axis=1)
    parts = _pad_rows(parts, ADAM_ROWS).astype(BF16)
    received = _all_to_all(parts, name="exchange_grads")
    sharded = list(BIG) + ["conv_w"]

    def pack_sharded(D):
        return _pad_rows(jnp.concatenate([_rows(D[n]) for n in BIG] + [_pad_rows(D["conv_w"], 8)], axis=0), ADAM_ROWS)

    g_s, d_s, nm_s, nv_s = _adamw(pack_sharded(W), pack_sharded(M), pack_sharded(V), received, tr=ADAM_ROWS,
                                  name="adamw_sharded")

    loss_row = jnp.zeros((1, LANES), F32).at[0, 0].set(loss_part)
    small_pack = _pad_rows(jnp.concatenate([_rows(g_small[n]) for n in SMALL] + [loss_row], axis=0), 8)
    small_all = _all_gather(small_pack, name="gather_small")

    def pack_small(D):
        return _pad_rows(jnp.concatenate([_rows(D[n]) for n in SMALL] + [jnp.zeros((1, LANES), F32)], axis=0), 8)

    g_r, d_r, nm_r, nv_r = _adamw(pack_small(W), pack_small(M), pack_small(V), small_all, tr=small_pack.shape[0],
                                  name="adamw_small")

    def unpack(ps, pr):
        out, off = {}, 0
        for n in sharded:
            r = 8 if n == "conv_w" else W[n].size // LANES
            out[n] = ps[off:off + r][:W[n].size // LANES].reshape(W[n].shape)
            off += r
        off = 0
        for n in SMALL:
            r = W[n].size // LANES
            out[n] = pr[off:off + r].reshape(W[n].shape)
            off += r
        return out

    G, Dl, NM, NV = unpack(g_s, g_r), unpack(d_s, d_r), unpack(nm_s, nm_r), unpack(nv_s, nv_r)
    loss = g_r[sum(W[n].size // LANES for n in SMALL), 0]
    return (loss, grad_x[None], *[G[n] for n in WEIGHTS], *[Dl[n] for n in WEIGHTS], *[NM[n] for n in WEIGHTS],
            *[NV[n] for n in WEIGHTS])
```

```python
import functools

import numpy as np
import jax
import jax.numpy as jnp
from jax import lax
from jax.experimental import pallas as pl
from jax.experimental.pallas import tpu as pltpu

F32 = jnp.float32
BF16 = jnp.bfloat16
MESH = pl.DeviceIdType.MESH

D_MODEL = 1024
N_DEV = 8
LANES = 128
RNN_BLOCKS = 8
RNN_BLOCK_W = 128
CONV_WIDTH = 4
LRU_C = 8.0
MLA_HEADS = 8
QK_NOPE = 128
QK_ROPE = 64
V_HEAD = 128
QK_PAD = 256
Q_LORA = 256
KV_LORA = 256
CKV_W = 640
ROPE_THETA = 10000.0
D_FF = 4096
EPS = 1e-6
ATTN_SCALE = (QK_NOPE + QK_ROPE) ** -0.5
LOG2E = 1.4426950408889634
LN2 = 0.6931471805599453
NEG = -1e30

ADAM_LR = 0.001
ADAM_B1 = 0.9
ADAM_B2 = 0.999
ADAM_EPS = 1e-08
ADAM_WD = 0.01
ADAM_STEP = 10

VMEM_LIMIT = 56 * 1024 * 1024


def _params(sem=None):
    return pltpu.CompilerParams(dimension_semantics=sem, vmem_limit_bytes=VMEM_LIMIT)


def _sigmoid(v):
    return 1.0 / (1.0 + jnp.exp(-v))


def _neg_expm1(y):
    u = jnp.exp(y)
    lu = jnp.log(u)
    safe = jnp.where(lu == 0.0, 1.0, lu)
    return jnp.where(lu == 0.0, -y, (1.0 - u) * y / safe)


def _softplus(y):
    e = jnp.exp(-jnp.abs(y))
    u = 1.0 + e
    d = u - 1.0
    l1p = jnp.where(d == 0.0, e, jnp.log(u) * e / jnp.where(d == 0.0, 1.0, d))
    return jnp.maximum(y, 0.0) + l1p


_GELU_K = 0.7978845608028654
_GELU_C = 0.044715


def _gelu_and_grad(v):
    t = jnp.tanh(_GELU_K * (v + _GELU_C * v * v * v))
    g = 0.5 * v * (1.0 + t)
    dg = 0.5 * (1.0 + t) + 0.5 * v * (1.0 - t * t) * _GELU_K * (1.0 + 3.0 * _GELU_C * v * v)
    return g, dg


def _rms_fwd(v, g):
    rstd = lax.rsqrt(jnp.mean(v * v, axis=-1, keepdims=True) + EPS)
    return v * rstd * g, rstd


def _rms_bwd(dy, v, g):
    rstd = lax.rsqrt(jnp.mean(v * v, axis=-1, keepdims=True) + EPS)
    vh = v * rstd
    dvh = dy * g
    dv = rstd * (dvh - vh * jnp.mean(dvh * vh, axis=-1, keepdims=True))
    return dv, dy * vh


def _shift_down(v, s, fill, row):
    return jnp.where(row >= s, pltpu.roll(v, s, 0), fill)


def _shift_up(v, s, fill, row, n):
    return jnp.where(row < n - s, pltpu.roll(v, n - s, 0), fill)


def _rot_half(v, lane):
    n = v.shape[-1]
    l = lane & (LANES - 1)
    up = pltpu.roll(v, n - QK_ROPE // 2, 1)
    dn = pltpu.roll(v, QK_ROPE // 2, 1)
    return jnp.where(l < QK_ROPE // 2, -up, jnp.where(l < QK_ROPE, dn, 0.0))


def _mm(a, b, *, name, tm, tn, tk, outs, epilogue, extras=(), ta=False, tb=False):
    assert not (ta and tb)
    if ta:
        K, M = a.shape
    else:
        M, K = a.shape
    if tb:
        N, K2 = b.shape
    else:
        K2, N = b.shape
    assert K == K2 and M % tm == 0 and N % tn == 0 and K % tk == 0, (name, a.shape, b.shape)
    n_i, n_j, n_k = M // tm, N // tn, K // tk
    n_ex, n_out = len(extras), len(outs)

    def body(*refs):
        a_ref, b_ref = refs[0], refs[1]
        ex_refs = refs[2:2 + n_ex]
        out_refs = refs[2 + n_ex:2 + n_ex + n_out]
        if ta:
            part = lax.dot_general(a_ref[...], b_ref[...], (((0,), (0,)), ((), ())), preferred_element_type=F32)
        elif tb:
            part = lax.dot_general(a_ref[...], b_ref[...], (((1,), (1,)), ((), ())), preferred_element_type=F32)
        else:
            part = jnp.dot(a_ref[...], b_ref[...], preferred_element_type=F32)

        def finish(acc):
            res = epilogue(acc, *[r[...] for r in ex_refs])
            for o_ref, r in zip(out_refs, res):
                o_ref[...] = r.astype(o_ref.dtype).reshape(o_ref.shape)

        if n_k == 1:
            finish(part)
        else:
            acc_ref = refs[-1]
            k = pl.program_id(2)

            @pl.when(k == 0)
            def _():
                acc_ref[...] = part

            @pl.when(k > 0)
            def _():
                acc_ref[...] += part

            @pl.when(k == n_k - 1)
            def _():
                finish(acc_ref[...])

    a_spec = pl.BlockSpec((tk, tm), lambda j, i, k: (k, i)) if ta else pl.BlockSpec((tm, tk), lambda j, i, k: (i, k))
    b_once = dict(pipeline_mode=pl.Buffered(1)) if (n_j == 1 and n_k == 1) else {}
    if tb:
        in_specs = [a_spec, pl.BlockSpec((tn, tk), lambda j, i, k: (j, k), **b_once)]
    else:
        in_specs = [a_spec, pl.BlockSpec((tk, tn), lambda j, i, k: (k, j), **b_once)]
    for ex in extras:
        kind = ex[0]
        if kind == "tile":
            in_specs.append(pl.BlockSpec((tm, tn), lambda j, i, k: (i, j)))
        elif kind == "tilecol":
            assert n_j == 1
            in_specs.append(pl.BlockSpec((tm, tn), functools.partial(lambda c, j, i, k: (i, c), ex[2])))
        else:
            in_specs.append(pl.BlockSpec((1, tn), lambda j, i, k: (0, j)))
    out_specs, out_shape = [], []
    for kind, dt in outs:
        if kind == "tile":
            out_specs.append(pl.BlockSpec((tm, tn), lambda j, i, k: (i, j)))
            out_shape.append(jax.ShapeDtypeStruct((M, N), dt))
        elif kind == "colshard":
            out_specs.append(pl.BlockSpec((1, tm, tn), lambda j, i, k: (j, i, 0)))
            out_shape.append(jax.ShapeDtypeStruct((n_j, M, tn), dt))
        elif kind == "side":
            assert n_j == 1
            out_specs.append(pl.BlockSpec((tm, LANES), lambda j, i, k: (i, 0)))
            out_shape.append(jax.ShapeDtypeStruct((M, LANES), dt))
        else:
            out_specs.append(pl.BlockSpec((1, 1, tn), lambda j, i, k: (i, 0, j)))
            out_shape.append(jax.ShapeDtypeStruct((n_i, 1, N), dt))
    scratch = [pltpu.VMEM((tm, tn), F32)] if n_k > 1 else []
    return pl.pallas_call(
        body, name=name, grid=(n_j, n_i, n_k), in_specs=in_specs, out_specs=out_specs, out_shape=out_shape,
        scratch_shapes=scratch, compiler_params=_params(("parallel", "parallel", "arbitrary")),
    )(a, b, *[ex[1] for ex in extras])


def _rmsnorm_cast(x, g, *, ts, name):
    S, D = x.shape

    def body(x_ref, g_ref, o_ref):
        y, _ = _rms_fwd(x_ref[...], g_ref[...])
        o_ref[...] = y.astype(BF16)

    return pl.pallas_call(
        body, name=name, grid=(S // ts,),
        in_specs=[pl.BlockSpec((ts, D), lambda i: (i, 0)), pl.BlockSpec((1, D), lambda i: (0, 0))],
        out_specs=pl.BlockSpec((ts, D), lambda i: (i, 0)), out_shape=jax.ShapeDtypeStruct((S, D), BF16),
        compiler_params=_params(("parallel",)),
    )(x, g)


def _lru_gates(xa, wa, ba, wx, bx, lam):
    xab = xa.astype(BF16)
    r = _sigmoid(jnp.dot(xab, wa, preferred_element_type=F32) + ba)
    i = _sigmoid(jnp.dot(xab, wx, preferred_element_type=F32) + bx)
    sp = _softplus(-lam)
    log_a = (-LRU_C * r) * sp
    a = jnp.exp(log_a)
    mult = jnp.sqrt(_neg_expm1(2.0 * log_a))
    return r, i, sp, a, mult


def _lru_fwd(z_main, conv_w, conv_b, wa, ba, wx, bx, lam, *, tt):
    S = z_main.shape[0]
    n_t = S // tt
    W = RNN_BLOCK_W

    def body(x_ref, cw_ref, cb_ref, wa_ref, ba_ref, wx_ref, bx_ref, lam_ref, h_ref, ext, hc):
        t = pl.program_id(1)

        @pl.when(t == 0)
        def _():
            ext[0:8, :] = jnp.zeros((8, W), F32)
            hc[...] = jnp.zeros((8, W), F32)

        x = x_ref[...]
        ext[8:8 + tt, :] = x
        cw = cw_ref[...]
        xa = (cb_ref[...] + cw[3:4] * x + cw[2:3] * ext[7:7 + tt, :] + cw[1:2] * ext[6:6 + tt, :]
              + cw[0:1] * ext[5:5 + tt, :])
        ext[0:8, :] = x[tt - 8:tt, :]
        _r, i, _sp, a, mult = _lru_gates(xa, wa_ref[0], ba_ref[0], wx_ref[0], bx_ref[0], lam_ref[...])
        b = mult * (i * xa)
        row = lax.broadcasted_iota(jnp.int32, (tt, W), 0)
        A, B = a, b
        s = 1
        while s < tt:
            B = A * _shift_down(B, s, 0.0, row) + B
            A = A * _shift_down(A, s, 1.0, row)
            s *= 2
        h = A * hc[0:1, :] + B
        h_ref[...] = h
        hc[...] = jnp.broadcast_to(h[tt - 1:tt, :], (8, W))

    blk = lambda n, t: (t, n)
    vec = pl.BlockSpec((1, W), lambda n, t: (0, n))
    mat = pl.BlockSpec((1, W, W), lambda n, t: (n, 0, 0))
    bias = pl.BlockSpec((1, 1, W), lambda n, t: (n, 0, 0))
    return pl.pallas_call(
        body, name="lru_fwd", grid=(RNN_BLOCKS, n_t),
        in_specs=[pl.BlockSpec((tt, W), blk), pl.BlockSpec((CONV_WIDTH, W), lambda n, t: (0, n)), vec, mat, bias, mat,
                  bias, vec],
        out_specs=pl.BlockSpec((tt, W), blk), out_shape=jax.ShapeDtypeStruct((S, D_MODEL), F32),
        scratch_shapes=[pltpu.VMEM((tt + 8, W), F32), pltpu.VMEM((8, W), F32)],
        compiler_params=_params(("parallel", "arbitrary")),
    )(z_main, conv_w, conv_b, wa, ba, wx, bx, lam)


def _lru_bwd(z_main, h, dh, conv_w, conv_b, wa, wat, ba, wx, wxt, bx, lam, *, tt):
    S = z_main.shape[0]
    n_t = S // tt
    W = RNN_BLOCK_W
    t8 = tt // 8

    def body(x_ref, xp_ref, h_ref, hp_ref, dh_ref, cw_ref, cb_ref, wa_ref, wat_ref, ba_ref, wx_ref, wxt_ref, bx_ref,
             lam_ref, dx_ref, dwa_ref, dwx_ref, dba_ref, dbx_ref, dlam_ref, dcw_ref, dcb_ref, ext, dext, a_c, g_c):
        t = pl.program_id(1)
        tile = n_t - 1 - t

        @pl.when(t == 0)
        def _():
            a_c[...] = jnp.zeros((8, W), F32)
            g_c[...] = jnp.zeros((8, W), F32)
            dext[tt:tt + 8, :] = jnp.zeros((8, W), F32)
            dwa_ref[...] = jnp.zeros_like(dwa_ref)
            dwx_ref[...] = jnp.zeros_like(dwx_ref)
            dba_ref[...] = jnp.zeros_like(dba_ref)
            dbx_ref[...] = jnp.zeros_like(dbx_ref)
            dlam_ref[...] = jnp.zeros_like(dlam_ref)
            dcw_ref[...] = jnp.zeros_like(dcw_ref)
            dcb_ref[...] = jnp.zeros_like(dcb_ref)

        has_prev = (tile > 0).astype(F32)
        x = x_ref[...]
        ext[0:8, :] = xp_ref[...] * has_prev
        ext[8:8 + tt, :] = x
        xm1, xm2, xm3 = ext[7:7 + tt, :], ext[6:6 + tt, :], ext[5:5 + tt, :]
        cw = cw_ref[...]
        xa = cb_ref[...] + cw[3:4] * x + cw[2:3] * xm1 + cw[1:2] * xm2 + cw[0:1] * xm3
        lam = lam_ref[...]
        r, i, sp, a, mult = _lru_gates(xa, wa_ref[0], ba_ref[0], wx_ref[0], bx_ref[0], lam)
        gated = i * xa
        row = lax.broadcasted_iota(jnp.int32, (tt, W), 0)
        hcur = h_ref[...]
        h_prev = _shift_down(hcur, 1, hp_ref[7:8, :] * has_prev, row)
        C = _shift_up(a, 1, a_c[0:1, :], row, tt)
        G = dh_ref[...]
        s = 1
        while s < tt:
            G = G + C * _shift_up(G, s, 0.0, row, tt)
            C = C * _shift_up(C, s, 1.0, row, tt)
            s *= 2
        g = G + C * g_c[0:1, :]
        a_c[...] = jnp.broadcast_to(a[0:1, :], (8, W))
        g_c[...] = jnp.broadcast_to(g[0:1, :], (8, W))
        dlog_a = g * h_prev * a - g * gated * (a * a) / mult
        dgated = g * mult
        di = dgated * xa
        dxa = dgated * i
        dr = dlog_a * (-LRU_C * sp)
        dsp = jnp.sum(dlog_a * (-LRU_C * r), axis=0, keepdims=True)
        dlam_ref[0] += dsp * (-_sigmoid(-lam))
        dpr = dr * r * (1.0 - r)
        dpi = di * i * (1.0 - i)
        xab, dprb, dpib = xa.astype(BF16), dpr.astype(BF16), dpi.astype(BF16)
        tn_dims = (((0,), (0,)), ((), ()))
        dwa_ref[0] += lax.dot_general(xab, dprb, tn_dims, preferred_element_type=F32)
        dwx_ref[0] += lax.dot_general(xab, dpib, tn_dims, preferred_element_type=F32)
        dba_ref[0] += jnp.sum(dpr, axis=0, keepdims=True)
        dbx_ref[0] += jnp.sum(dpi, axis=0, keepdims=True)
        dxa = (dxa + jnp.dot(dprb, wat_ref[0], preferred_element_type=F32)
               + jnp.dot(dpib, wxt_ref[0], preferred_element_type=F32))
        dext[0:tt, :] = dxa
        dx = cw[3:4] * dxa + cw[2:3] * dext[1:1 + tt, :] + cw[1:2] * dext[2:2 + tt, :] + cw[0:1] * dext[3:3 + tt, :]
        dext[tt:tt + 8, :] = dxa[0:8, :]
        dx_ref[...] = dx.astype(BF16)
        dcw_ref[3:4, :] += jnp.sum(dxa * x, axis=0, keepdims=True)
        dcw_ref[2:3, :] += jnp.sum(dxa * xm1, axis=0, keepdims=True)
        dcw_ref[1:2, :] += jnp.sum(dxa * xm2, axis=0, keepdims=True)
        dcw_ref[0:1, :] += jnp.sum(dxa * xm3, axis=0, keepdims=True)
        dcb_ref[...] += jnp.sum(dxa, axis=0, keepdims=True)

    blk = lambda n, t: (n_t - 1 - t, n)
    prev = lambda n, t: (jnp.maximum((n_t - 1 - t) * t8 - 1, 0), n)
    vec = pl.BlockSpec((1, W), lambda n, t: (0, n))
    mat = pl.BlockSpec((1, W, W), lambda n, t: (n, 0, 0))
    bias = pl.BlockSpec((1, 1, W), lambda n, t: (n, 0, 0))
    cws = pl.BlockSpec((CONV_WIDTH, W), lambda n, t: (0, n))
    tile = pl.BlockSpec((tt, W), blk)
    prev8 = pl.BlockSpec((8, W), prev)
    return pl.pallas_call(
        body, name="lru_bwd", grid=(RNN_BLOCKS, n_t),
        in_specs=[tile, prev8, tile, prev8, tile, cws, vec, mat, mat, bias, mat, mat, bias, vec],
        out_specs=[tile, mat, mat, bias, bias, bias, cws, vec],
        out_shape=[jax.ShapeDtypeStruct((S, D_MODEL), BF16),
                   jax.ShapeDtypeStruct((RNN_BLOCKS, W, W), F32), jax.ShapeDtypeStruct((RNN_BLOCKS, W, W), F32),
                   jax.ShapeDtypeStruct((RNN_BLOCKS, 1, W), F32), jax.ShapeDtypeStruct((RNN_BLOCKS, 1, W), F32),
                   jax.ShapeDtypeStruct((RNN_BLOCKS, 1, W), F32),
                   jax.ShapeDtypeStruct((CONV_WIDTH, D_MODEL), F32), jax.ShapeDtypeStruct((1, D_MODEL), F32)],
        scratch_shapes=[pltpu.VMEM((tt + 8, W), F32), pltpu.VMEM((tt + 8, W), F32), pltpu.VMEM((8, W), F32),
                        pltpu.VMEM((8, W), F32)],
        compiler_params=_params(("parallel", "arbitrary")),
    )(z_main, z_main, h, h, dh, conv_w, conv_b, wa, wat, ba, wx, wxt, bx, lam)


def _mla_proj(z_ckv, q_norm, kv_norm, w_uq, w_ukv, cos, sin, *, ts):
    S = z_ckv.shape[0]
    H = MLA_HEADS

    def body(c_ref, qn_ref, kn_ref, wq_ref, wkv_ref, cos_ref, sin_ref, q_ref, k_ref, v_ref):
        c = c_ref[...]
        cqn, _ = _rms_fwd(c[:, 0:Q_LORA], qn_ref[...])
        ckn, _ = _rms_fwd(c[:, Q_LORA:Q_LORA + KV_LORA], kn_ref[...])
        q = jnp.dot(cqn.astype(BF16), wq_ref[...], preferred_element_type=F32) * (ATTN_SCALE * LOG2E)
        kv = jnp.dot(ckn.astype(BF16), wkv_ref[...], preferred_element_type=F32)
        cos1, sin1 = cos_ref[...], sin_ref[...]
        cos8 = jnp.concatenate([cos1] * H, axis=1)
        sin8 = jnp.concatenate([sin1] * H, axis=1)
        qr = q[:, H * QK_NOPE:]
        lane8 = lax.broadcasted_iota(jnp.int32, qr.shape, 1)
        qr = qr * cos8 + _rot_half(qr, lane8) * sin8
        kr = c[:, Q_LORA + KV_LORA:]
        lane1 = lax.broadcasted_iota(jnp.int32, kr.shape, 1)
        kr = (kr * cos1 + _rot_half(kr, lane1) * sin1).astype(BF16)
        for h in range(H):
            q_ref[h, :, 0:QK_NOPE] = q[:, h * QK_NOPE:(h + 1) * QK_NOPE].astype(BF16)
            q_ref[h, :, QK_NOPE:] = qr[:, h * LANES:(h + 1) * LANES].astype(BF16)
            k_ref[h, :, 0:QK_NOPE] = kv[:, h * 2 * LANES:h * 2 * LANES + LANES].astype(BF16)
            k_ref[h, :, QK_NOPE:] = kr
            v_ref[h] = kv[:, h * 2 * LANES + LANES:(h + 1) * 2 * LANES].astype(BF16)

    full = lambda shape: pl.BlockSpec(shape, lambda i: (0,) * len(shape))
    return pl.pallas_call(
        body, name="mla_proj", grid=(S // ts,),
        in_specs=[pl.BlockSpec((ts, CKV_W), lambda i: (i, 0)), full((1, Q_LORA)), full((1, KV_LORA)),
                  full(w_uq.shape), full(w_ukv.shape), pl.BlockSpec((ts, LANES), lambda i: (i, 0)),
                  pl.BlockSpec((ts, LANES), lambda i: (i, 0))],
        out_specs=[pl.BlockSpec((H, ts, QK_PAD), lambda i: (0, i, 0)), pl.BlockSpec((H, ts, QK_PAD), lambda i: (0, i, 0)),
                   pl.BlockSpec((H, ts, V_HEAD), lambda i: (0, i, 0))],
        out_shape=[jax.ShapeDtypeStruct((H, S, QK_PAD), BF16), jax.ShapeDtypeStruct((H, S, QK_PAD), BF16),
                   jax.ShapeDtypeStruct((H, S, V_HEAD), BF16)],
        compiler_params=_params(("parallel",)),
    )(z_ckv, q_norm, kv_norm, w_uq, w_ukv, cos, sin)


def _mla_proj_bwd(z_ckv, dq, dk, dv, q_norm, kv_norm, w_uqt, w_ukvt, cos, sin, *, ts):
    S = z_ckv.shape[0]
    H = MLA_HEADS

    def body(c_ref, dq_ref, dk_ref, dv_ref, qn_ref, kn_ref, wqt_ref, wkvt_ref, cos_ref, sin_ref,
             dz_ref, dwq_ref, dwkv_ref, dqn_ref, dkn_ref):
        @pl.when(pl.program_id(0) == 0)
        def _():
            dwq_ref[...] = jnp.zeros_like(dwq_ref)
            dwkv_ref[...] = jnp.zeros_like(dwkv_ref)
            dqn_ref[...] = jnp.zeros_like(dqn_ref)
            dkn_ref[...] = jnp.zeros_like(dkn_ref)

        c = c_ref[...]
        cq, ck = c[:, 0:Q_LORA], c[:, Q_LORA:Q_LORA + KV_LORA]
        qn, kn = qn_ref[...], kn_ref[...]
        cqn, _ = _rms_fwd(cq, qn)
        ckn, _ = _rms_fwd(ck, kn)
        cos1, sin1 = cos_ref[...], sin_ref[...]
        lane1 = lax.broadcasted_iota(jnp.int32, cos1.shape, 1)

        def unrope(g):
            return g * cos1 - _rot_half(g * sin1, lane1)

        dq_all = jnp.concatenate([dq_ref[h, :, 0:QK_NOPE] for h in range(H)]
                                 + [unrope(dq_ref[h, :, QK_NOPE:]) for h in range(H)], axis=1)
        dq_all = (dq_all * ATTN_SCALE).astype(BF16)
        dkv_all = jnp.concatenate([p for h in range(H) for p in (dk_ref[h, :, 0:QK_NOPE], dv_ref[h])],
                                  axis=1).astype(BF16)
        dkr = dk_ref[0, :, QK_NOPE:]
        for h in range(1, H):
            dkr = dkr + dk_ref[h, :, QK_NOPE:]
        dkr = unrope(dkr)
        tn_dims = (((0,), (0,)), ((), ()))
        dwq_ref[...] += lax.dot_general(cqn.astype(BF16), dq_all, tn_dims, preferred_element_type=F32)
        dwkv_ref[...] += lax.dot_general(ckn.astype(BF16), dkv_all, tn_dims, preferred_element_type=F32)
        dcqn = jnp.dot(dq_all, wqt_ref[...], preferred_element_type=F32)
        dckn = jnp.dot(dkv_all, wkvt_ref[...], preferred_element_type=F32)
        dcq, dqn_rows = _rms_bwd(dcqn, cq, qn)
        dck, dkn_rows = _rms_bwd(dckn, ck, kn)
        dqn_ref[...] += jnp.sum(dqn_rows, axis=0, keepdims=True)
        dkn_ref[...] += jnp.sum(dkn_rows, axis=0, keepdims=True)
        dz_ref[:, 0:Q_LORA] = dcq.astype(BF16)
        dz_ref[:, Q_LORA:Q_LORA + KV_LORA] = dck.astype(BF16)
        dz_ref[:, Q_LORA + KV_LORA:] = dkr.astype(BF16)

    full = lambda shape: pl.BlockSpec(shape, lambda i: (0,) * len(shape))
    return pl.pallas_call(
        body, name="mla_proj_bwd", grid=(S // ts,),
        in_specs=[pl.BlockSpec((ts, CKV_W), lambda i: (i, 0)), pl.BlockSpec((H, ts, QK_PAD), lambda i: (0, i, 0)),
                  pl.BlockSpec((H, ts, QK_PAD), lambda i: (0, i, 0)), pl.BlockSpec((H, ts, V_HEAD), lambda i: (0, i, 0)),
                  full((1, Q_LORA)), full((1, KV_LORA)), full(w_uqt.shape), full(w_ukvt.shape),
                  pl.BlockSpec((ts, LANES), lambda i: (i, 0)), pl.BlockSpec((ts, LANES), lambda i: (i, 0))],
        out_specs=[pl.BlockSpec((ts, CKV_W), lambda i: (i, 0)), full((Q_LORA, w_uqt.shape[0])),
                   full((KV_LORA, w_ukvt.shape[0])), full((1, Q_LORA)), full((1, KV_LORA))],
        out_shape=[jax.ShapeDtypeStruct((S, CKV_W), BF16), jax.ShapeDtypeStruct((Q_LORA, w_uqt.shape[0]), F32),
                   jax.ShapeDtypeStruct((KV_LORA, w_ukvt.shape[0]), F32), jax.ShapeDtypeStruct((1, Q_LORA), F32),
                   jax.ShapeDtypeStruct((1, KV_LORA), F32)],
        compiler_params=_params(("arbitrary",)),
    )(z_ckv, dq, dk, dv, q_norm, kv_norm, w_uqt, w_ukvt, cos, sin)


NT_DIMS = (((1,), (1,)), ((), ()))
TN_DIMS = (((0,), (0,)), ((), ()))


def _attn_fwd(q, k, v, *, t, hb):
    H, S, _ = q.shape
    n = S // t
    nc = t // LANES
    pairs = [(i, j) for i in range(n) for j in range(i + 1)]
    qi = jnp.asarray(np.array([p[0] for p in pairs], np.int32))
    ki = jnp.asarray(np.array([p[1] for p in pairs], np.int32))

    def body(qi_ref, ki_ref, q_ref, k_ref, v_ref, o_ref, lse_ref, m_s, l_s, acc_s):
        p = pl.program_id(1)
        i, j = qi_ref[p], ki_ref[p]

        @pl.when(j == 0)
        def _():
            m_s[...] = jnp.full(m_s.shape, NEG, F32)
            l_s[...] = jnp.zeros(l_s.shape, F32)
            acc_s[...] = jnp.zeros(acc_s.shape, F32)

        def step(masked):
            for hh in range(hb):
                s = lax.dot_general(q_ref[hh], k_ref[hh], NT_DIMS, preferred_element_type=F32)
                if masked:
                    row = lax.broadcasted_iota(jnp.int32, (t, t), 0)
                    col = lax.broadcasted_iota(jnp.int32, (t, t), 1)
                    s = jnp.where(row >= col, s, NEG)
                mc = s[:, 0:LANES]
                for c in range(1, nc):
                    mc = jnp.maximum(mc, s[:, c * LANES:(c + 1) * LANES])
                m_prev = m_s[hh]
                m_new = jnp.maximum(m_prev, jnp.max(mc, axis=1, keepdims=True))
                alpha = jnp.exp2(m_prev - m_new)
                pr = jnp.exp2(s - jnp.concatenate([m_new] * nc, axis=1))
                ls = pr[:, 0:LANES]
                for c in range(1, nc):
                    ls = ls + pr[:, c * LANES:(c + 1) * LANES]
                l_s[hh] = alpha * l_s[hh] + ls
                acc_s[hh] = alpha * acc_s[hh] + jnp.dot(pr.astype(BF16), v_ref[hh], preferred_element_type=F32)
                m_s[hh] = m_new

        @pl.when(j < i)
        def _():
            step(False)

        @pl.when(j == i)
        def _():
            step(True)
            for hh in range(hb):
                l = jnp.sum(l_s[hh], axis=1, keepdims=True)
                o_ref[:, hh * V_HEAD:(hh + 1) * V_HEAD] = acc_s[hh] / l
                lse_ref[hh] = m_s[hh][:, 0:1] + jnp.log2(l)

    grid_spec = pltpu.PrefetchScalarGridSpec(
        num_scalar_prefetch=2, grid=(H // hb, len(pairs)),
        in_specs=[pl.BlockSpec((hb, t, QK_PAD), lambda h, p, qi, ki: (h, qi[p], 0)),
                  pl.BlockSpec((hb, t, QK_PAD), lambda h, p, qi, ki: (h, ki[p], 0)),
                  pl.BlockSpec((hb, t, V_HEAD), lambda h, p, qi, ki: (h, ki[p], 0))],
        out_specs=[pl.BlockSpec((t, hb * V_HEAD), lambda h, p, qi, ki: (qi[p], h)),
                   pl.BlockSpec((hb, t, 1), lambda h, p, qi, ki: (h, qi[p], 0))],
        scratch_shapes=[pltpu.VMEM((hb, t, LANES), F32), pltpu.VMEM((hb, t, LANES), F32),
                        pltpu.VMEM((hb, t, V_HEAD), F32)],
    )
    return pl.pallas_call(
        body, name="attn_fwd", grid_spec=grid_spec,
        out_shape=[jax.ShapeDtypeStruct((S, H * V_HEAD), F32), jax.ShapeDtypeStruct((H, S, 1), F32)],
        compiler_params=_params(("parallel", "arbitrary")),
    )(qi, ki, q, k, v)


def _attn_bwd(q, k, v, do, lse_row, delta_row, *, t):
    H, S, _ = q.shape
    n = S // t
    pairs = [(i, j) for j in range(n) for i in range(j, n)]
    qi = jnp.asarray(np.array([p[0] for p in pairs], np.int32))
    ki = jnp.asarray(np.array([p[1] for p in pairs], np.int32))

    def body(qi_ref, ki_ref, q_ref, k_ref, v_ref, do_ref, lse_ref, dl_ref, dq_ref, dk_ref, dv_ref, dk_s, dv_s):
        p = pl.program_id(1)
        i, j = qi_ref[p], ki_ref[p]

        @pl.when(p == 0)
        def _():
            dq_ref[...] = jnp.zeros_like(dq_ref)

        def step(masked):
            qb, kb, vb, dob = q_ref[0], k_ref[0], v_ref[0], do_ref[...]
            st = lax.dot_general(kb, qb, NT_DIMS, preferred_element_type=F32)
            if masked:
                krow = lax.broadcasted_iota(jnp.int32, (t, t), 0)
                qcol = lax.broadcasted_iota(jnp.int32, (t, t), 1)
                st = jnp.where(krow <= qcol, st, NEG)
            pt = jnp.exp2(st - lse_ref[0])
            dvp = jnp.dot(pt.astype(BF16), dob, preferred_element_type=F32)
            dpt = lax.dot_general(vb, dob, NT_DIMS, preferred_element_type=F32)
            dst = (pt * (dpt - dl_ref[0])).astype(BF16)
            dkp = jnp.dot(dst, qb, preferred_element_type=F32)
            rows = pl.ds(pl.multiple_of(i * t, t), t)
            dq_ref[0, rows, :] += lax.dot_general(dst, kb, TN_DIMS, preferred_element_type=F32)
            return dkp, dvp

        @pl.when(i == j)
        def _():
            dkp, dvp = step(True)
            dk_s[...] = dkp
            dv_s[...] = dvp

        @pl.when(i != j)
        def _():
            dkp, dvp = step(False)
            dk_s[...] += dkp
            dv_s[...] += dvp

        @pl.when(i == n - 1)
        def _():
            dk_ref[0] = dk_s[...] * LN2
            dv_ref[0] = dv_s[...]

    grid_spec = pltpu.PrefetchScalarGridSpec(
        num_scalar_prefetch=2, grid=(H, len(pairs)),
        in_specs=[pl.BlockSpec((1, t, QK_PAD), lambda h, p, qi, ki: (h, qi[p], 0)),
                  pl.BlockSpec((1, t, QK_PAD), lambda h, p, qi, ki: (h, ki[p], 0)),
                  pl.BlockSpec((1, t, V_HEAD), lambda h, p, qi, ki: (h, ki[p], 0)),
                  pl.BlockSpec((t, V_HEAD), lambda h, p, qi, ki: (qi[p], h)),
                  pl.BlockSpec((1, 1, t), lambda h, p, qi, ki: (h, 0, qi[p])),
                  pl.BlockSpec((1, 1, t), lambda h, p, qi, ki: (h, 0, qi[p]))],
        out_specs=[pl.BlockSpec((1, S, QK_PAD), lambda h, p, qi, ki: (h, 0, 0)),
                   pl.BlockSpec((1, t, QK_PAD), lambda h, p, qi, ki: (h, ki[p], 0)),
                   pl.BlockSpec((1, t, V_HEAD), lambda h, p, qi, ki: (h, ki[p], 0))],
        scratch_shapes=[pltpu.VMEM((t, QK_PAD), F32), pltpu.VMEM((t, V_HEAD), F32)],
    )
    return pl.pallas_call(
        body, name="attn_bwd", grid_spec=grid_spec,
        out_shape=[jax.ShapeDtypeStruct((H, S, QK_PAD), F32), jax.ShapeDtypeStruct((H, S, QK_PAD), F32),
                   jax.ShapeDtypeStruct((H, S, V_HEAD), F32)],
        compiler_params=_params(("parallel", "arbitrary")),
    )(qi, ki, q, k, v, do, lse_row, delta_row)


def _merge_fwd(h, z_main, o, *, ts):
    S = h.shape[0]
    D = D_MODEL

    def body(h_ref, rg_ref, ga_ref, gb_ref, o_ref, m_ref):
        gl, _ = _gelu_and_grad(rg_ref[...])
        m = _sigmoid(ga_ref[...]) * (h_ref[...] * gl) + _sigmoid(gb_ref[...]) * o_ref[...]
        m_ref[...] = m.astype(BF16)

    col = lambda c: pl.BlockSpec((ts, D), lambda i: (i, c))
    return pl.pallas_call(
        body, name="merge_fwd", grid=(S // ts,),
        in_specs=[col(0), col(1), col(2), col(3), col(0)],
        out_specs=col(0), out_shape=jax.ShapeDtypeStruct((S, D), BF16),
        compiler_params=_params(("parallel",)),
    )(h, z_main, z_main, z_main, o)


def _my_place():
    return lax.axis_index("x"), lax.axis_index("y"), lax.axis_index("c")


def _all_gather(shards, *, name):
    n = len(shards)

    def body(*refs):
        x_refs, out_refs = refs[:n], refs[n:2 * n]
        send_sems, recv_sems, local_sems = refs[2 * n:]
        x, y, c = _my_place()
        me, sibling = (x, y, c), (x, y, 1 - c)
        chips = [(1 - x, y), (x, 1 - y), (1 - x, 1 - y)]

        def slot(a, px, py, pc):
            return out_refs[a].at[4 * px + 2 * py + pc]

        def copy(a, k, block, to, src=None):
            return pltpu.make_async_remote_copy(
                src_ref=slot(a, *block) if src is None else src, dst_ref=slot(a, *block),
                send_sem=send_sems.at[7 * a + k], recv_sem=recv_sems.at[7 * a + k], device_id=to, device_id_type=MESH)

        mine = [pltpu.make_async_copy(x_refs[a], slot(a, *me), local_sems.at[a]) for a in range(n)]
        for cp in mine:
            cp.start()
        first = []
        for a in range(n):
            first.append(copy(a, 0, me, sibling, src=x_refs[a]))
            first += [copy(a, 1 + j, me, (*chip, c), src=x_refs[a]) for j, chip in enumerate(chips)]
        for cp in first:
            cp.start()
        passed = []
        for a in range(n):
            for j, chip in enumerate(chips):
                copy(a, 1 + j, (*chip, c), me).wait_recv()
                fwd = copy(a, 4 + j, (*chip, c), sibling)
                fwd.start()
                passed.append(fwd)
        for a in range(n):
            copy(a, 0, sibling, me).wait_recv()
            for j, chip in enumerate(chips):
                copy(a, 4 + j, (*chip, 1 - c), me).wait_recv()
        for cp in first + passed:
            cp.wait_send()
        for cp in mine:
            cp.wait()

    hbm = pl.BlockSpec(memory_space=pl.ANY)
    return pl.pallas_call(
        body, name=name, out_shape=[jax.ShapeDtypeStruct((N_DEV, *s.shape), s.dtype) for s in shards],
        in_specs=[hbm] * n, out_specs=[hbm] * n,
        scratch_shapes=[pltpu.SemaphoreType.DMA((7 * n,)), pltpu.SemaphoreType.DMA((7 * n,)),
                        pltpu.SemaphoreType.DMA((n,))],
    )(*shards)


def _all_to_all(parts, *, name):
    n = len(parts)

    def body(*refs):
        p_refs, out_refs = refs[:n], refs[n:2 * n]
        send_sems, recv_sems, local_sems = refs[2 * n:]
        x, y, c = _my_place()
        me = 4 * x + 2 * y + c
        mine = [pltpu.make_async_copy(p_refs[a].at[me], out_refs[a].at[me], local_sems.at[a]) for a in range(n)]
        for cp in mine:
            cp.start()
        copies = []
        for a in range(n):
            for k in range(1, N_DEV):
                px, py, pc = x ^ (k >> 2), y ^ ((k >> 1) & 1), c ^ (k & 1)
                copies.append(pltpu.make_async_remote_copy(
                    src_ref=p_refs[a].at[4 * px + 2 * py + pc], dst_ref=out_refs[a].at[me],
                    send_sem=send_sems.at[7 * a + k - 1], recv_sem=recv_sems.at[7 * a + k - 1],
                    device_id=(px, py, pc), device_id_type=MESH))
        for cp in copies:
            cp.start()
        for cp in copies:
            cp.wait_recv()
        for cp in copies:
            cp.wait_send()
        for cp in mine:
            cp.wait()

    hbm = pl.BlockSpec(memory_space=pl.ANY)
    return pl.pallas_call(
        body, name=name, out_shape=[jax.ShapeDtypeStruct(p.shape, p.dtype) for p in parts],
        in_specs=[hbm] * n, out_specs=[hbm] * n,
        scratch_shapes=[pltpu.SemaphoreType.DMA((7 * n,)), pltpu.SemaphoreType.DMA((7 * n,)),
                        pltpu.SemaphoreType.DMA((n,))],
    )(*parts)


def _adamw(w, m, v, gparts, *, tr, name):
    R, C = w.shape
    n_parts = gparts.shape[0]

    def body(w_ref, m_ref, v_ref, gp_ref, g_ref, d_ref, nm_ref, nv_ref):
        g = gp_ref[0].astype(F32)
        for p in range(1, n_parts):
            g = g + gp_ref[p].astype(F32)
        wv = w_ref[...]
        m_new = ADAM_B1 * m_ref[...] + (1.0 - ADAM_B1) * g
        v_new = ADAM_B2 * v_ref[...] + (1.0 - ADAM_B2) * (g * g)
        m_hat = m_new / (1.0 - ADAM_B1 ** ADAM_STEP)
        v_hat = v_new / (1.0 - ADAM_B2 ** ADAM_STEP)
        g_ref[...] = g
        d_ref[...] = -ADAM_LR * (m_hat / (jnp.sqrt(v_hat) + ADAM_EPS) + ADAM_WD * wv)
        nm_ref[...] = m_new
        nv_ref[...] = v_new

    row = pl.BlockSpec((tr, C), lambda i: (i, 0))
    shp = jax.ShapeDtypeStruct((R, C), F32)
    return pl.pallas_call(
        body, name=name, grid=(R // tr,),
        in_specs=[row, row, row, pl.BlockSpec((n_parts, tr, C), lambda i: (0, i, 0))],
        out_specs=[row, row, row, row], out_shape=[shp, shp, shp, shp],
        compiler_params=_params(("parallel",)),
    )(w, m, v, gparts)


def _rope_tables(s):
    pos = jnp.arange(s, dtype=F32)
    inv_freq = 1.0 / (ROPE_THETA ** (jnp.arange(0, QK_ROPE, 2, dtype=F32) / QK_ROPE))
    ang = pos[:, None] * inv_freq[None, :]
    cos, sin = jnp.cos(ang), jnp.sin(ang)
    zero = jnp.zeros((s, LANES - QK_ROPE), F32)
    return jnp.concatenate([cos, cos, zero], -1), jnp.concatenate([sin, sin, zero], -1)


def _pick(n, want):
    t = min(n, want)
    assert n % t == 0
    return t


def _local_step(x, target, wts, small):
    S = x.shape[0]
    H = MLA_HEADS
    ts = _pick(S, 512)
    tm = _pick(S, 512)
    tk_s = _pick(S, 2048)
    row = lambda v: v.reshape(1, -1)
    w_in = wts["w_in"]
    w_main = jnp.concatenate([w_in[:, 0:2048], w_in[:, 2624:4672]], axis=1)
    w_ckv = jnp.concatenate([w_in[:, 2048:2624], jnp.zeros((D_MODEL, CKV_W - 576), BF16)], axis=1)
    w_uq3 = wts["w_uq"].reshape(Q_LORA, H, QK_NOPE + QK_ROPE)
    w_uq_p = jnp.concatenate(
        [w_uq3[:, :, :QK_NOPE].reshape(Q_LORA, H * QK_NOPE),
         jnp.pad(w_uq3[:, :, QK_NOPE:], ((0, 0), (0, 0), (0, LANES - QK_ROPE))).reshape(Q_LORA, H * LANES)], axis=1)
    w_ukv, w_out, w_up, w_down = wts["w_ukv"], wts["w_out"], wts["w_up"], wts["w_down"]
    cos, sin = _rope_tables(S)
    conv_w, conv_b = small["conv_w"], row(small["conv_b"])
    wa, wx = small["lru_wa"].astype(BF16), small["lru_wx"].astype(BF16)
    wat, wxt = jnp.swapaxes(wa, 1, 2), jnp.swapaxes(wx, 1, 2)
    ba, bx = small["lru_ba"].reshape(RNN_BLOCKS, 1, RNN_BLOCK_W), small["lru_bx"].reshape(RNN_BLOCKS, 1, RNN_BLOCK_W)
    lam = row(small["lru_lambda"])
    q_norm, kv_norm = row(small["q_norm"]), row(small["kv_norm"])
    norm_mix, norm_mlp, norm_final = row(small["norm_mix"]), row(small["norm_mlp"]), row(small["norm_final"])

    xn = _rmsnorm_cast(x, norm_mix, ts=ts, name="norm_mix")
    ident = lambda acc: (acc,)
    (z_main,) = _mm(xn, w_main, name="z_main", tm=tm, tn=1024, tk=1024, outs=[("tile", F32)], epilogue=ident)
    (z_ckv,) = _mm(xn, w_ckv, name="z_ckv", tm=tm, tn=CKV_W, tk=1024, outs=[("tile", F32)], epilogue=ident)
    tt = _pick(S, 256)
    h = _lru_fwd(z_main, conv_w, conv_b, wa, ba, wx, bx, lam, tt=tt)
    q, k, v = _mla_proj(z_ckv, q_norm, kv_norm, w_uq_p, w_ukv, cos, sin, ts=_pick(S, 256))
    ta = _pick(S, 1024)
    o, lse = _attn_fwd(q, k, v, t=ta, hb=2)
    merged = _merge_fwd(h, z_main, o, ts=_pick(S, 256))

    def ep_h1(acc, xv, g):
        h1 = acc + xv
        n2, _ = _rms_fwd(h1, g)
        return h1, n2

    h1, n2 = _mm(merged, w_out, name="h1", tm=tm, tn=1024, tk=1024, outs=[("tile", F32), ("tile", BF16)],
                 epilogue=ep_h1, extras=[("tile", x), ("row", norm_mlp)])

    def ep_up(acc):
        r = jnp.maximum(acc, 0.0)
        return r * r, r

    act, relu = _mm(n2, w_up, name="up", tm=tm, tn=1024, tk=1024, outs=[("tile", BF16), ("tile", BF16)],
                    epilogue=ep_up)

    def ep_loss(acc, h1v, tgt, g):
        h2 = acc + h1v
        y, _ = _rms_fwd(h2, g)
        err = y - tgt
        loss_rows = 0.5 * jnp.mean(err * err, axis=-1, keepdims=True)
        dy = err * (1.0 / D_MODEL)
        dh2, dg_rows = _rms_bwd(dy, h2, g)
        lsum = jnp.sum(loss_rows, axis=0, keepdims=True)
        return dh2, dh2, jnp.sum(dg_rows, axis=0, keepdims=True), jnp.broadcast_to(lsum, (1, D_MODEL))

    dh2, dh2b, dnf_p, loss_p = _mm(
        act, w_down, name="down_loss", tm=tm, tn=1024, tk=D_FF,
        outs=[("tile", F32), ("tile", BF16), ("rowpart", F32), ("rowpart", F32)], epilogue=ep_loss,
        extras=[("tile", h1), ("tile", target), ("row", norm_final)])
    loss_part = jnp.sum(loss_p[:, 0, 0])
    d_norm_final = jnp.sum(dnf_p, axis=(0, 1))

    def ep_du(acc, r):
        return (acc * (2.0 * r.astype(F32)),)

    (du,) = _mm(dh2b, w_down, name="d_act", tb=True, tm=tm, tn=1024, tk=1024, outs=[("tile", BF16)], epilogue=ep_du,
                extras=[("tile", relu)])

    def ep_dh1(acc, h1v, dh2v, g):
        dv, dg_rows = _rms_bwd(acc, h1v, g)
        dh1 = dh2v + dv
        return dh1, dh1, jnp.sum(dg_rows, axis=0, keepdims=True)

    dh1, dh1b, dnm_p = _mm(du, w_up, name="d_n2", tb=True, tm=tm, tn=1024, tk=D_FF,
                           outs=[("tile", F32), ("tile", BF16), ("rowpart", F32)], epilogue=ep_dh1,
                           extras=[("tile", h1), ("tile", dh2), ("row", norm_mlp)])
    d_norm_mlp = jnp.sum(dnm_p, axis=(0, 1))
    tn_mm = functools.partial(_mm, ta=True, tk=tk_s, outs=[("tile", BF16)], epilogue=ident)
    (d_w_down,) = tn_mm(act, dh2b, name="dw_down", tm=1024, tn=1024)
    (p_w_up,) = _mm(n2, du, name="dw_up", ta=True, tk=tk_s, tm=1024, tn=D_FF // N_DEV, outs=[("colshard", BF16)],
                    epilogue=ident)
    (d_w_out,) = tn_mm(merged, dh1b, name="dw_out", tm=1024, tn=1024)

    tmm = _pick(S, 256)

    def ep_dmerge(dm, hv, rg, ga, gb, ov):
        gl, dgl = _gelu_and_grad(rg)
        sa, sb = _sigmoid(ga), _sigmoid(gb)
        ya = hv * gl
        dya = dm * sa
        do = dm * sb
        dga = dm * ya * sa * (1.0 - sa)
        dgb = dm * ov * sb * (1.0 - sb)
        dh = dya * gl
        drg = dya * hv * dgl
        dov = do * ov
        lane = lax.broadcasted_iota(jnp.int32, (dm.shape[0], LANES), 1)
        delta = jnp.zeros((dm.shape[0], LANES), F32)
        for hh in range(H):
            dsum = jnp.sum(dov[:, hh * V_HEAD:(hh + 1) * V_HEAD], axis=1, keepdims=True)
            delta = jnp.where(lane == hh, dsum, delta)
        return dh, drg, dga, dgb, do, delta

    dh_lru, d_rg, d_ga, d_gb, do, delta_w = _mm(
        dh1b, w_out, name="d_merge", tb=True, tm=tmm, tn=1024, tk=1024,
        outs=[("tile", F32), ("tile", BF16), ("tile", BF16), ("tile", BF16), ("tile", BF16), ("side", F32)],
        epilogue=ep_dmerge,
        extras=[("tile", h), ("tilecol", z_main, 1), ("tilecol", z_main, 2), ("tilecol", z_main, 3), ("tile", o)])
    delta_row = delta_w[:, :H].T.reshape(H, 1, S)
    lse_row = lse.reshape(H, 1, S)

    dq, dk, dv = _attn_bwd(q, k, v, do, lse_row, delta_row, t=ta)
    dz_ckv, d_w_uq_p, d_w_ukv, d_q_norm, d_kv_norm = _mla_proj_bwd(
        z_ckv, dq, dk, dv, q_norm, kv_norm, w_uq_p.T, w_ukv.T, cos, sin, ts=_pick(S, 256))
    d_w_uq = jnp.concatenate(
        [d_w_uq_p[:, :H * QK_NOPE].reshape(Q_LORA, H, QK_NOPE),
         d_w_uq_p[:, H * QK_NOPE:].reshape(Q_LORA, H, LANES)[:, :, :QK_ROPE]], axis=2).reshape(Q_LORA, -1)

    d_rx, d_wa, d_wx, d_ba, d_bx, d_lam, d_conv_w, d_conv_b = _lru_bwd(
        z_main, h, dh_lru, conv_w, conv_b, wa, wat, ba, wx, wxt, bx, lam, tt=tt)

    dz_main = jnp.concatenate([d_rx, d_rg, d_ga, d_gb], axis=1)
    (dxn_ckv,) = _mm(dz_ckv, w_ckv, name="dxn_ckv", tb=True, tm=tm, tn=1024, tk=CKV_W, outs=[("tile", F32)],
                     epilogue=ident)

    def ep_dx(acc, part, xv, dh1v, g):
        dv, dg_rows = _rms_bwd(acc + part, xv, g)
        return dh1v + dv, jnp.sum(dg_rows, axis=0, keepdims=True)

    grad_x, dnx_p = _mm(dz_main, w_main, name="dx", tb=True, tm=tm, tn=1024, tk=4 * D_MODEL,
                        outs=[("tile", F32), ("rowpart", F32)], epilogue=ep_dx,
                        extras=[("tile", dxn_ckv), ("tile", x), ("tile", dh1), ("row", norm_mix)])
    d_norm_mix = jnp.sum(dnx_p, axis=(0, 1))
    (d_w_main,) = tn_mm(xn, dz_main, name="dw_main", tm=1024, tn=1024)
    (d_w_ckv,) = tn_mm(xn, dz_ckv, name="dw_ckv", tm=1024, tn=CKV_W)
    d_w_in = jnp.concatenate([d_w_main[:, 0:2048], d_w_ckv[:, 0:576], d_w_main[:, 2048:4096]], axis=1)

    def col_parts(full):
        r = full.shape[0]
        return jnp.transpose(full.astype(BF16).reshape(r, N_DEV, -1), (1, 0, 2))

    big = {"w_in": col_parts(d_w_in), "w_uq": col_parts(d_w_uq), "w_ukv": col_parts(d_w_ukv),
           "w_out": d_w_out.reshape(N_DEV, -1, D_MODEL), "w_up": p_w_up, "w_down": d_w_down.reshape(N_DEV, -1, D_MODEL)}
    sm = {"norm_mix": d_norm_mix, "conv_w": d_conv_w, "conv_b": d_conv_b.reshape(-1), "lru_wa": d_wa,
          "lru_ba": d_ba.reshape(RNN_BLOCKS, RNN_BLOCK_W), "lru_wx": d_wx, "lru_bx": d_bx.reshape(RNN_BLOCKS, RNN_BLOCK_W),
          "lru_lambda": d_lam.reshape(-1), "q_norm": d_q_norm.reshape(-1), "kv_norm": d_kv_norm.reshape(-1),
          "norm_mlp": d_norm_mlp, "norm_final": d_norm_final}
    return loss_part, grad_x, big, sm


BIG = ("w_in", "w_uq", "w_ukv", "w_out", "w_up", "w_down")
SMALL = ("norm_mix", "conv_b", "lru_wa", "lru_ba", "lru_wx", "lru_bx", "lru_lambda", "q_norm", "kv_norm", "norm_mlp",
         "norm_final")
WEIGHTS = ("norm_mix", "w_in", "conv_w", "conv_b", "lru_wa", "lru_ba", "lru_wx", "lru_bx", "lru_lambda", "q_norm", "w_uq",
           "kv_norm", "w_ukv", "w_out", "norm_mlp", "w_up", "w_down", "norm_final")
ADAM_TILE_ROWS = {"w_in": 256, "w_uq": 128, "w_ukv": 128, "w_out": 64, "w_up": 256, "w_down": 128}
CONV_ROWS = N_DEV * 8


def _rows(a):
    return a.reshape(-1, LANES)


def _pad_rows(a, mult):
    r = a.shape[-2]
    pad = (-r) % mult
    if pad == 0:
        return a
    cfg = [(0, 0)] * (a.ndim - 2) + [(0, pad), (0, 0)]
    return jnp.pad(a, cfg)


def _cols_from_shards(g):
    return jnp.transpose(g, (1, 0, 2)).reshape(g.shape[1], -1)


def kernel(x, norm_mix, w_in, conv_w, conv_b, lru_wa, lru_ba, lru_wx, lru_bx, lru_lambda, q_norm, w_uq, kv_norm, w_ukv, w_out, norm_mlp, w_up, w_down, norm_final, loss_target, m_norm_mix, m_w_in, m_conv_w, m_conv_b, m_lru_wa, m_lru_ba, m_lru_wx, m_lru_bx, m_lru_lambda, m_q_norm, m_w_uq, m_kv_norm, m_w_ukv, m_w_out, m_norm_mlp, m_w_up, m_w_down, m_norm_final, v_norm_mix, v_w_in, v_conv_w, v_conv_b, v_lru_wa, v_lru_ba, v_lru_wx, v_lru_bx, v_lru_lambda, v_q_norm, v_w_uq, v_kv_norm, v_w_ukv, v_w_out, v_norm_mlp, v_w_up, v_w_down, v_norm_final):
    W = dict(norm_mix=norm_mix, w_in=w_in, conv_w=conv_w, conv_b=conv_b, lru_wa=lru_wa, lru_ba=lru_ba, lru_wx=lru_wx,
             lru_bx=lru_bx, lru_lambda=lru_lambda, q_norm=q_norm, w_uq=w_uq, kv_norm=kv_norm, w_ukv=w_ukv, w_out=w_out,
             norm_mlp=norm_mlp, w_up=w_up, w_down=w_down, norm_final=norm_final)
    M = dict(norm_mix=m_norm_mix, w_in=m_w_in, conv_w=m_conv_w, conv_b=m_conv_b, lru_wa=m_lru_wa, lru_ba=m_lru_ba,
             lru_wx=m_lru_wx, lru_bx=m_lru_bx, lru_lambda=m_lru_lambda, q_norm=m_q_norm, w_uq=m_w_uq, kv_norm=m_kv_norm,
             w_ukv=m_w_ukv, w_out=m_w_out, norm_mlp=m_norm_mlp, w_up=m_w_up, w_down=m_w_down, norm_final=m_norm_final)
    V = dict(norm_mix=v_norm_mix, w_in=v_w_in, conv_w=v_conv_w, conv_b=v_conv_b, lru_wa=v_lru_wa, lru_ba=v_lru_ba,
             lru_wx=v_lru_wx, lru_bx=v_lru_bx, lru_lambda=v_lru_lambda, q_norm=v_q_norm, w_uq=v_w_uq, kv_norm=v_kv_norm,
             w_ukv=v_w_ukv, w_out=v_w_out, norm_mlp=v_norm_mlp, w_up=v_w_up, w_down=v_w_down, norm_final=v_norm_final)
    me = 4 * lax.axis_index("x") + 2 * lax.axis_index("y") + lax.axis_index("c")

    first = ("w_in", "w_uq", "w_ukv")
    later = ("w_out", "w_up", "w_down")
    got = _all_gather([W[n].astype(BF16) for n in first] + [_pad_rows(conv_w, 8)] + [W[n].astype(BF16) for n in later],
                      name="gather_weights")
    wts = {"w_in": _cols_from_shards(got[0]), "w_uq": _cols_from_shards(got[1]), "w_ukv": _cols_from_shards(got[2]),
           "w_out": got[4].reshape(-1, D_MODEL), "w_up": _cols_from_shards(got[5]), "w_down": got[6].reshape(-1, D_MODEL)}
    small = {n: W[n] for n in SMALL}
    small["conv_w"] = _cols_from_shards(got[3][:, :CONV_WIDTH])

    loss_part, grad_x, g_big, g_small = _local_step(x[0], loss_target[0], wts, small)

    received = _all_to_all([g_big[n] for n in BIG], name="exchange_grads")
    G, Dl, NM, NV = {}, {}, {}, {}
    for n, parts in zip(BIG, received):
        G[n], Dl[n], NM[n], NV[n] = _adamw(W[n], M[n], V[n], parts, tr=ADAM_TILE_ROWS[n], name="adamw_" + n)

    conv_rows = _pad_rows(jnp.transpose(g_small["conv_w"].reshape(CONV_WIDTH, N_DEV, LANES), (1, 0, 2)), 8)
    loss_rows = jnp.zeros((8, LANES), F32).at[0, 0].set(loss_part)
    rows8 = lambda a: _pad_rows(_rows(a), 8)
    small_pack = jnp.concatenate(
        [conv_rows.reshape(CONV_ROWS, LANES)] + [rows8(g_small[n]) for n in SMALL] + [loss_rows], axis=0)
    (small_all,) = _all_gather([small_pack], name="gather_small")

    def pack_small(D):
        return jnp.concatenate(
            [jnp.zeros((CONV_ROWS, LANES), F32)] + [rows8(D[n]) for n in SMALL] + [jnp.zeros((8, LANES), F32)], axis=0)

    packed = _adamw(pack_small(W), pack_small(M), pack_small(V), small_all, tr=small_pack.shape[0], name="adamw_small")
    off = CONV_ROWS
    for n in SMALL:
        r = W[n].size // LANES
        for out, pk in zip((G, Dl, NM, NV), packed):
            out[n] = pk[off:off + r].reshape(W[n].shape)
        off += r + (-r) % 8
    loss = packed[0][off, 0]

    g_conv = lax.dynamic_slice(packed[0], (me * 8, 0), (8, LANES))
    conv_out = _adamw(_pad_rows(conv_w, 8), _pad_rows(m_conv_w, 8), _pad_rows(v_conv_w, 8), g_conv[None], tr=8,
                      name="adamw_conv_w")
    for out, pk in zip((G, Dl, NM, NV), conv_out):
        out["conv_w"] = pk[:CONV_WIDTH]
    return (loss, grad_x[None], *[G[n] for n in WEIGHTS], *[Dl[n] for n in WEIGHTS], *[NM[n] for n in WEIGHTS],
            *[NV[n] for n in WEIGHTS])
```

```python
import functools

import numpy as np
import jax
import jax.numpy as jnp
from jax import lax
from jax.experimental import pallas as pl
from jax.experimental.pallas import tpu as pltpu

F32 = jnp.float32
BF16 = jnp.bfloat16
MESH = pl.DeviceIdType.MESH

D_MODEL = 1024
N_DEV = 8
LANES = 128
RNN_BLOCKS = 8
RNN_BLOCK_W = 128
CONV_WIDTH = 4
LRU_C = 8.0
MLA_HEADS = 8
QK_NOPE = 128
QK_ROPE = 64
V_HEAD = 128
QK_PAD = 256
Q_LORA = 256
KV_LORA = 256
CKV_W = 640
ROPE_THETA = 10000.0
D_FF = 4096
EPS = 1e-6
ATTN_SCALE = (QK_NOPE + QK_ROPE) ** -0.5
LOG2E = 1.4426950408889634
LN2 = 0.6931471805599453
NEG = -1e30

ADAM_LR = 0.001
ADAM_B1 = 0.9
ADAM_B2 = 0.999
ADAM_EPS = 1e-08
ADAM_WD = 0.01
ADAM_STEP = 10

VMEM_LIMIT = 56 * 1024 * 1024


def _params(sem=None):
    return pltpu.CompilerParams(dimension_semantics=sem, vmem_limit_bytes=VMEM_LIMIT)


def _sigmoid(v):
    return 1.0 / (1.0 + jnp.exp(-v))


def _neg_expm1(y):
    u = jnp.exp(y)
    lu = jnp.log(u)
    safe = jnp.where(lu == 0.0, 1.0, lu)
    return jnp.where(lu == 0.0, -y, (1.0 - u) * y / safe)


def _softplus(y):
    e = jnp.exp(-jnp.abs(y))
    u = 1.0 + e
    d = u - 1.0
    l1p = jnp.where(d == 0.0, e, jnp.log(u) * e / jnp.where(d == 0.0, 1.0, d))
    return jnp.maximum(y, 0.0) + l1p


_GELU_K = 0.7978845608028654
_GELU_C = 0.044715


def _gelu_and_grad(v):
    t = jnp.tanh(_GELU_K * (v + _GELU_C * v * v * v))
    g = 0.5 * v * (1.0 + t)
    dg = 0.5 * (1.0 + t) + 0.5 * v * (1.0 - t * t) * _GELU_K * (1.0 + 3.0 * _GELU_C * v * v)
    return g, dg


def _rms_fwd(v, g):
    rstd = lax.rsqrt(jnp.mean(v * v, axis=-1, keepdims=True) + EPS)
    return v * rstd * g, rstd


def _rms_bwd(dy, v, g):
    rstd = lax.rsqrt(jnp.mean(v * v, axis=-1, keepdims=True) + EPS)
    vh = v * rstd
    dvh = dy * g
    dv = rstd * (dvh - vh * jnp.mean(dvh * vh, axis=-1, keepdims=True))
    return dv, dy * vh


def _shift_down(v, s, fill, row):
    return jnp.where(row >= s, pltpu.roll(v, s, 0), fill)


def _shift_up(v, s, fill, row, n):
    return jnp.where(row < n - s, pltpu.roll(v, n - s, 0), fill)


def _rot_half(v, lane):
    n = v.shape[-1]
    l = lane & (LANES - 1)
    up = pltpu.roll(v, n - QK_ROPE // 2, 1)
    dn = pltpu.roll(v, QK_ROPE // 2, 1)
    return jnp.where(l < QK_ROPE // 2, -up, jnp.where(l < QK_ROPE, dn, 0.0))


def _mm(a, b, *, name, tm, tn, tk, outs, epilogue, extras=(), ta=False, tb=False):
    assert not (ta and tb)
    if ta:
        K, M = a.shape
    else:
        M, K = a.shape
    if tb:
        N, K2 = b.shape
    else:
        K2, N = b.shape
    assert K == K2 and M % tm == 0 and N % tn == 0 and K % tk == 0, (name, a.shape, b.shape)
    n_i, n_j, n_k = M // tm, N // tn, K // tk
    n_ex, n_out = len(extras), len(outs)

    def body(*refs):
        a_ref, b_ref = refs[0], refs[1]
        ex_refs = refs[2:2 + n_ex]
        out_refs = refs[2 + n_ex:2 + n_ex + n_out]
        if ta:
            part = lax.dot_general(a_ref[...], b_ref[...], (((0,), (0,)), ((), ())), preferred_element_type=F32)
        elif tb:
            part = lax.dot_general(a_ref[...], b_ref[...], (((1,), (1,)), ((), ())), preferred_element_type=F32)
        else:
            part = jnp.dot(a_ref[...], b_ref[...], preferred_element_type=F32)

        def finish(acc):
            res = epilogue(acc, *[r[...] for r in ex_refs])
            for o_ref, r in zip(out_refs, res):
                o_ref[...] = r.astype(o_ref.dtype).reshape(o_ref.shape)

        if n_k == 1:
            finish(part)
        else:
            acc_ref = refs[-1]
            k = pl.program_id(2)

            @pl.when(k == 0)
            def _():
                acc_ref[...] = part

            @pl.when(k > 0)
            def _():
                acc_ref[...] += part

            @pl.when(k == n_k - 1)
            def _():
                finish(acc_ref[...])

    a_spec = pl.BlockSpec((tk, tm), lambda j, i, k: (k, i)) if ta else pl.BlockSpec((tm, tk), lambda j, i, k: (i, k))
    b_once = dict(pipeline_mode=pl.Buffered(1)) if (n_j == 1 and n_k == 1) else {}
    if tb:
        in_specs = [a_spec, pl.BlockSpec((tn, tk), lambda j, i, k: (j, k), **b_once)]
    else:
        in_specs = [a_spec, pl.BlockSpec((tk, tn), lambda j, i, k: (k, j), **b_once)]
    for ex in extras:
        kind = ex[0]
        if kind == "tile":
            in_specs.append(pl.BlockSpec((tm, tn), lambda j, i, k: (i, j)))
        elif kind == "tilecol":
            assert n_j == 1
            in_specs.append(pl.BlockSpec((tm, tn), functools.partial(lambda c, j, i, k: (i, c), ex[2])))
        else:
            in_specs.append(pl.BlockSpec((1, tn), lambda j, i, k: (0, j)))
    out_specs, out_shape = [], []
    for kind, dt in outs:
        if kind == "tile":
            out_specs.append(pl.BlockSpec((tm, tn), lambda j, i, k: (i, j)))
            out_shape.append(jax.ShapeDtypeStruct((M, N), dt))
        elif kind == "colshard":
            out_specs.append(pl.BlockSpec((1, tm, tn), lambda j, i, k: (j, i, 0)))
            out_shape.append(jax.ShapeDtypeStruct((n_j, M, tn), dt))
        elif kind == "side":
            assert n_j == 1
            out_specs.append(pl.BlockSpec((tm, LANES), lambda j, i, k: (i, 0)))
            out_shape.append(jax.ShapeDtypeStruct((M, LANES), dt))
        else:
            out_specs.append(pl.BlockSpec((1, 1, tn), lambda j, i, k: (i, 0, j)))
            out_shape.append(jax.ShapeDtypeStruct((n_i, 1, N), dt))
    scratch = [pltpu.VMEM((tm, tn), F32)] if n_k > 1 else []
    return pl.pallas_call(
        body, name=name, grid=(n_j, n_i, n_k), in_specs=in_specs, out_specs=out_specs, out_shape=out_shape,
        scratch_shapes=scratch, compiler_params=_params(("parallel", "parallel", "arbitrary")),
    )(a, b, *[ex[1] for ex in extras])


def _rmsnorm_cast(x, g, *, ts, name):
    S, D = x.shape

    def body(x_ref, g_ref, o_ref):
        y, _ = _rms_fwd(x_ref[...], g_ref[...])
        o_ref[...] = y.astype(BF16)

    return pl.pallas_call(
        body, name=name, grid=(S // ts,),
        in_specs=[pl.BlockSpec((ts, D), lambda i: (i, 0)), pl.BlockSpec((1, D), lambda i: (0, 0))],
        out_specs=pl.BlockSpec((ts, D), lambda i: (i, 0)), out_shape=jax.ShapeDtypeStruct((S, D), BF16),
        compiler_params=_params(("parallel",)),
    )(x, g)


def _lru_gates(xa, wa, ba, wx, bx, lam):
    xab = xa.astype(BF16)
    r = _sigmoid(jnp.dot(xab, wa, preferred_element_type=F32) + ba)
    i = _sigmoid(jnp.dot(xab, wx, preferred_element_type=F32) + bx)
    sp = _softplus(-lam)
    log_a = (-LRU_C * r) * sp
    a = jnp.exp(log_a)
    mult = jnp.sqrt(_neg_expm1(2.0 * log_a))
    return r, i, sp, a, mult


def _lru_fwd(z_main, conv_w, conv_b, wa, ba, wx, bx, lam, *, tt):
    S = z_main.shape[0]
    n_t = S // tt
    W = RNN_BLOCK_W

    def body(x_ref, cw_ref, cb_ref, wa_ref, ba_ref, wx_ref, bx_ref, lam_ref, h_ref, ext, hc):
        t = pl.program_id(1)

        @pl.when(t == 0)
        def _():
            ext[0:8, :] = jnp.zeros((8, W), F32)
            hc[...] = jnp.zeros((8, W), F32)

        x = x_ref[...]
        ext[8:8 + tt, :] = x
        cw = cw_ref[...]
        xa = (cb_ref[...] + cw[3:4] * x + cw[2:3] * ext[7:7 + tt, :] + cw[1:2] * ext[6:6 + tt, :]
              + cw[0:1] * ext[5:5 + tt, :])
        ext[0:8, :] = x[tt - 8:tt, :]
        _r, i, _sp, a, mult = _lru_gates(xa, wa_ref[0], ba_ref[0], wx_ref[0], bx_ref[0], lam_ref[...])
        b = mult * (i * xa)
        row = lax.broadcasted_iota(jnp.int32, (tt, W), 0)
        A, B = a, b
        s = 1
        while s < tt:
            B = A * _shift_down(B, s, 0.0, row) + B
            A = A * _shift_down(A, s, 1.0, row)
            s *= 2
        h = A * hc[0:1, :] + B
        h_ref[...] = h
        hc[...] = jnp.broadcast_to(h[tt - 1:tt, :], (8, W))

    blk = lambda n, t: (t, n)
    vec = pl.BlockSpec((1, W), lambda n, t: (0, n))
    mat = pl.BlockSpec((1, W, W), lambda n, t: (n, 0, 0))
    bias = pl.BlockSpec((1, 1, W), lambda n, t: (n, 0, 0))
    return pl.pallas_call(
        body, name="lru_fwd", grid=(RNN_BLOCKS, n_t),
        in_specs=[pl.BlockSpec((tt, W), blk), pl.BlockSpec((CONV_WIDTH, W), lambda n, t: (0, n)), vec, mat, bias, mat,
                  bias, vec],
        out_specs=pl.BlockSpec((tt, W), blk), out_shape=jax.ShapeDtypeStruct((S, D_MODEL), F32),
        scratch_shapes=[pltpu.VMEM((tt + 8, W), F32), pltpu.VMEM((8, W), F32)],
        compiler_params=_params(("parallel", "arbitrary")),
    )(z_main, conv_w, conv_b, wa, ba, wx, bx, lam)


def _lru_bwd(z_main, h, dh, conv_w, conv_b, wa, wat, ba, wx, wxt, bx, lam, *, tt):
    S = z_main.shape[0]
    n_t = S // tt
    W = RNN_BLOCK_W
    t8 = tt // 8

    def body(x_ref, xp_ref, h_ref, hp_ref, dh_ref, cw_ref, cb_ref, wa_ref, wat_ref, ba_ref, wx_ref, wxt_ref, bx_ref,
             lam_ref, dx_ref, dwa_ref, dwx_ref, dba_ref, dbx_ref, dlam_ref, dcw_ref, dcb_ref, ext, dext, a_c, g_c):
        t = pl.program_id(1)
        tile = n_t - 1 - t

        @pl.when(t == 0)
        def _():
            a_c[...] = jnp.zeros((8, W), F32)
            g_c[...] = jnp.zeros((8, W), F32)
            dext[tt:tt + 8, :] = jnp.zeros((8, W), F32)
            dwa_ref[...] = jnp.zeros_like(dwa_ref)
            dwx_ref[...] = jnp.zeros_like(dwx_ref)
            dba_ref[...] = jnp.zeros_like(dba_ref)
            dbx_ref[...] = jnp.zeros_like(dbx_ref)
            dlam_ref[...] = jnp.zeros_like(dlam_ref)
            dcw_ref[...] = jnp.zeros_like(dcw_ref)
            dcb_ref[...] = jnp.zeros_like(dcb_ref)

        has_prev = (tile > 0).astype(F32)
        x = x_ref[...]
        ext[0:8, :] = xp_ref[...] * has_prev
        ext[8:8 + tt, :] = x
        xm1, xm2, xm3 = ext[7:7 + tt, :], ext[6:6 + tt, :], ext[5:5 + tt, :]
        cw = cw_ref[...]
        xa = cb_ref[...] + cw[3:4] * x + cw[2:3] * xm1 + cw[1:2] * xm2 + cw[0:1] * xm3
        lam = lam_ref[...]
        r, i, sp, a, mult = _lru_gates(xa, wa_ref[0], ba_ref[0], wx_ref[0], bx_ref[0], lam)
        gated = i * xa
        row = lax.broadcasted_iota(jnp.int32, (tt, W), 0)
        hcur = h_ref[...]
        h_prev = _shift_down(hcur, 1, hp_ref[7:8, :] * has_prev, row)
        C = _shift_up(a, 1, a_c[0:1, :], row, tt)
        G = dh_ref[...]
        s = 1
        while s < tt:
            G = G + C * _shift_up(G, s, 0.0, row, tt)
            C = C * _shift_up(C, s, 1.0, row, tt)
            s *= 2
        g = G + C * g_c[0:1, :]
        a_c[...] = jnp.broadcast_to(a[0:1, :], (8, W))
        g_c[...] = jnp.broadcast_to(g[0:1, :], (8, W))
        dlog_a = g * h_prev * a - g * gated * (a * a) / mult
        dgated = g * mult
        di = dgated * xa
        dxa = dgated * i
        dr = dlog_a * (-LRU_C * sp)
        dsp = jnp.sum(dlog_a * (-LRU_C * r), axis=0, keepdims=True)
        dlam_ref[0] += dsp * (-_sigmoid(-lam))
        dpr = dr * r * (1.0 - r)
        dpi = di * i * (1.0 - i)
        xab, dprb, dpib = xa.astype(BF16), dpr.astype(BF16), dpi.astype(BF16)
        tn_dims = (((0,), (0,)), ((), ()))
        dwa_ref[0] += lax.dot_general(xab, dprb, tn_dims, preferred_element_type=F32)
        dwx_ref[0] += lax.dot_general(xab, dpib, tn_dims, preferred_element_type=F32)
        dba_ref[0] += jnp.sum(dpr, axis=0, keepdims=True)
        dbx_ref[0] += jnp.sum(dpi, axis=0, keepdims=True)
        dxa = (dxa + jnp.dot(dprb, wat_ref[0], preferred_element_type=F32)
               + jnp.dot(dpib, wxt_ref[0], preferred_element_type=F32))
        dext[0:tt, :] = dxa
        dx = cw[3:4] * dxa + cw[2:3] * dext[1:1 + tt, :] + cw[1:2] * dext[2:2 + tt, :] + cw[0:1] * dext[3:3 + tt, :]
        dext[tt:tt + 8, :] = dxa[0:8, :]
        dx_ref[...] = dx.astype(BF16)
        dcw_ref[3:4, :] += jnp.sum(dxa * x, axis=0, keepdims=True)
        dcw_ref[2:3, :] += jnp.sum(dxa * xm1, axis=0, keepdims=True)
        dcw_ref[1:2, :] += jnp.sum(dxa * xm2, axis=0, keepdims=True)
        dcw_ref[0:1, :] += jnp.sum(dxa * xm3, axis=0, keepdims=True)
        dcb_ref[...] += jnp.sum(dxa, axis=0, keepdims=True)

    blk = lambda n, t: (n_t - 1 - t, n)
    prev = lambda n, t: (jnp.maximum((n_t - 1 - t) * t8 - 1, 0), n)
    vec = pl.BlockSpec((1, W), lambda n, t: (0, n))
    mat = pl.BlockSpec((1, W, W), lambda n, t: (n, 0, 0))
    bias = pl.BlockSpec((1, 1, W), lambda n, t: (n, 0, 0))
    cws = pl.BlockSpec((CONV_WIDTH, W), lambda n, t: (0, n))
    tile = pl.BlockSpec((tt, W), blk)
    prev8 = pl.BlockSpec((8, W), prev)
    return pl.pallas_call(
        body, name="lru_bwd", grid=(RNN_BLOCKS, n_t),
        in_specs=[tile, prev8, tile, prev8, tile, cws, vec, mat, mat, bias, mat, mat, bias, vec],
        out_specs=[tile, mat, mat, bias, bias, bias, cws, vec],
        out_shape=[jax.ShapeDtypeStruct((S, D_MODEL), BF16),
                   jax.ShapeDtypeStruct((RNN_BLOCKS, W, W), F32), jax.ShapeDtypeStruct((RNN_BLOCKS, W, W), F32),
                   jax.ShapeDtypeStruct((RNN_BLOCKS, 1, W), F32), jax.ShapeDtypeStruct((RNN_BLOCKS, 1, W), F32),
                   jax.ShapeDtypeStruct((RNN_BLOCKS, 1, W), F32),
                   jax.ShapeDtypeStruct((CONV_WIDTH, D_MODEL), F32), jax.ShapeDtypeStruct((1, D_MODEL), F32)],
        scratch_shapes=[pltpu.VMEM((tt + 8, W), F32), pltpu.VMEM((tt + 8, W), F32), pltpu.VMEM((8, W), F32),
                        pltpu.VMEM((8, W), F32)],
        compiler_params=_params(("parallel", "arbitrary")),
    )(z_main, z_main, h, h, dh, conv_w, conv_b, wa, wat, ba, wx, wxt, bx, lam)


def _mla_proj(z_ckv, q_norm, kv_norm, w_uq, w_ukv, cos, sin, *, ts):
    S = z_ckv.shape[0]
    H = MLA_HEADS

    def body(c_ref, qn_ref, kn_ref, wq_ref, wkv_ref, cos_ref, sin_ref, q_ref, k_ref, v_ref):
        c = c_ref[...]
        cqn, _ = _rms_fwd(c[:, 0:Q_LORA], qn_ref[...])
        ckn, _ = _rms_fwd(c[:, Q_LORA:Q_LORA + KV_LORA], kn_ref[...])
        q = jnp.dot(cqn.astype(BF16), wq_ref[...], preferred_element_type=F32) * (ATTN_SCALE * LOG2E)
        kv = jnp.dot(ckn.astype(BF16), wkv_ref[...], preferred_element_type=F32)
        cos1, sin1 = cos_ref[...], sin_ref[...]
        cos8 = jnp.concatenate([cos1] * H, axis=1)
        sin8 = jnp.concatenate([sin1] * H, axis=1)
        qr = q[:, H * QK_NOPE:]
        lane8 = lax.broadcasted_iota(jnp.int32, qr.shape, 1)
        qr = qr * cos8 + _rot_half(qr, lane8) * sin8
        kr = c[:, Q_LORA + KV_LORA:]
        lane1 = lax.broadcasted_iota(jnp.int32, kr.shape, 1)
        kr = (kr * cos1 + _rot_half(kr, lane1) * sin1).astype(BF16)
        for h in range(H):
            q_ref[h, :, 0:QK_NOPE] = q[:, h * QK_NOPE:(h + 1) * QK_NOPE].astype(BF16)
            q_ref[h, :, QK_NOPE:] = qr[:, h * LANES:(h + 1) * LANES].astype(BF16)
            k_ref[h, :, 0:QK_NOPE] = kv[:, h * 2 * LANES:h * 2 * LANES + LANES].astype(BF16)
            k_ref[h, :, QK_NOPE:] = kr
            v_ref[h] = kv[:, h * 2 * LANES + LANES:(h + 1) * 2 * LANES].astype(BF16)

    full = lambda shape: pl.BlockSpec(shape, lambda i: (0,) * len(shape))
    return pl.pallas_call(
        body, name="mla_proj", grid=(S // ts,),
        in_specs=[pl.BlockSpec((ts, CKV_W), lambda i: (i, 0)), full((1, Q_LORA)), full((1, KV_LORA)),
                  full(w_uq.shape), full(w_ukv.shape), pl.BlockSpec((ts, LANES), lambda i: (i, 0)),
                  pl.BlockSpec((ts, LANES), lambda i: (i, 0))],
        out_specs=[pl.BlockSpec((H, ts, QK_PAD), lambda i: (0, i, 0)), pl.BlockSpec((H, ts, QK_PAD), lambda i: (0, i, 0)),
                   pl.BlockSpec((H, ts, V_HEAD), lambda i: (0, i, 0))],
        out_shape=[jax.ShapeDtypeStruct((H, S, QK_PAD), BF16), jax.ShapeDtypeStruct((H, S, QK_PAD), BF16),
                   jax.ShapeDtypeStruct((H, S, V_HEAD), BF16)],
        compiler_params=_params(("parallel",)),
    )(z_ckv, q_norm, kv_norm, w_uq, w_ukv, cos, sin)


def _mla_proj_bwd(z_ckv, dq, dk, dv, q_norm, kv_norm, w_uqt, w_ukvt, cos, sin, *, ts):
    S = z_ckv.shape[0]
    H = MLA_HEADS

    def body(c_ref, dq_ref, dk_ref, dv_ref, qn_ref, kn_ref, wqt_ref, wkvt_ref, cos_ref, sin_ref,
             dz_ref, dwq_ref, dwkv_ref, dqn_ref, dkn_ref):
        @pl.when(pl.program_id(0) == 0)
        def _():
            dwq_ref[...] = jnp.zeros_like(dwq_ref)
            dwkv_ref[...] = jnp.zeros_like(dwkv_ref)
            dqn_ref[...] = jnp.zeros_like(dqn_ref)
            dkn_ref[...] = jnp.zeros_like(dkn_ref)

        c = c_ref[...]
        cq, ck = c[:, 0:Q_LORA], c[:, Q_LORA:Q_LORA + KV_LORA]
        qn, kn = qn_ref[...], kn_ref[...]
        cqn, _ = _rms_fwd(cq, qn)
        ckn, _ = _rms_fwd(ck, kn)
        cos1, sin1 = cos_ref[...], sin_ref[...]
        lane1 = lax.broadcasted_iota(jnp.int32, cos1.shape, 1)

        def unrope(g):
            return g * cos1 - _rot_half(g * sin1, lane1)

        dq_all = jnp.concatenate([dq_ref[h, :, 0:QK_NOPE] for h in range(H)]
                                 + [unrope(dq_ref[h, :, QK_NOPE:]) for h in range(H)], axis=1)
        dq_all = (dq_all * ATTN_SCALE).astype(BF16)
        dkv_all = jnp.concatenate([p for h in range(H) for p in (dk_ref[h, :, 0:QK_NOPE], dv_ref[h])],
                                  axis=1).astype(BF16)
        dkr = dk_ref[0, :, QK_NOPE:]
        for h in range(1, H):
            dkr = dkr + dk_ref[h, :, QK_NOPE:]
        dkr = unrope(dkr)
        tn_dims = (((0,), (0,)), ((), ()))
        dwq_ref[...] += lax.dot_general(cqn.astype(BF16), dq_all, tn_dims, preferred_element_type=F32)
        dwkv_ref[...] += lax.dot_general(ckn.astype(BF16), dkv_all, tn_dims, preferred_element_type=F32)
        dcqn = jnp.dot(dq_all, wqt_ref[...], preferred_element_type=F32)
        dckn = jnp.dot(dkv_all, wkvt_ref[...], preferred_element_type=F32)
        dcq, dqn_rows = _rms_bwd(dcqn, cq, qn)
        dck, dkn_rows = _rms_bwd(dckn, ck, kn)
        dqn_ref[...] += jnp.sum(dqn_rows, axis=0, keepdims=True)
        dkn_ref[...] += jnp.sum(dkn_rows, axis=0, keepdims=True)
        dz_ref[:, 0:Q_LORA] = dcq.astype(BF16)
        dz_ref[:, Q_LORA:Q_LORA + KV_LORA] = dck.astype(BF16)
        dz_ref[:, Q_LORA + KV_LORA:] = dkr.astype(BF16)

    full = lambda shape: pl.BlockSpec(shape, lambda i: (0,) * len(shape))
    return pl.pallas_call(
        body, name="mla_proj_bwd", grid=(S // ts,),
        in_specs=[pl.BlockSpec((ts, CKV_W), lambda i: (i, 0)), pl.BlockSpec((H, ts, QK_PAD), lambda i: (0, i, 0)),
                  pl.BlockSpec((H, ts, QK_PAD), lambda i: (0, i, 0)), pl.BlockSpec((H, ts, V_HEAD), lambda i: (0, i, 0)),
                  full((1, Q_LORA)), full((1, KV_LORA)), full(w_uqt.shape), full(w_ukvt.shape),
                  pl.BlockSpec((ts, LANES), lambda i: (i, 0)), pl.BlockSpec((ts, LANES), lambda i: (i, 0))],
        out_specs=[pl.BlockSpec((ts, CKV_W), lambda i: (i, 0)), full((Q_LORA, w_uqt.shape[0])),
                   full((KV_LORA, w_ukvt.shape[0])), full((1, Q_LORA)), full((1, KV_LORA))],
        out_shape=[jax.ShapeDtypeStruct((S, CKV_W), BF16), jax.ShapeDtypeStruct((Q_LORA, w_uqt.shape[0]), F32),
                   jax.ShapeDtypeStruct((KV_LORA, w_ukvt.shape[0]), F32), jax.ShapeDtypeStruct((1, Q_LORA), F32),
                   jax.ShapeDtypeStruct((1, KV_LORA), F32)],
        compiler_params=_params(("arbitrary",)),
    )(z_ckv, dq, dk, dv, q_norm, kv_norm, w_uqt, w_ukvt, cos, sin)


NT_DIMS = (((1,), (1,)), ((), ()))
TN_DIMS = (((0,), (0,)), ((), ()))


def _attn_fwd(q, k, v, *, t, hb):
    H, S, _ = q.shape
    n = S // t
    nc = t // LANES
    pairs = [(i, j) for i in range(n) for j in range(i + 1)]
    qi = jnp.asarray(np.array([p[0] for p in pairs], np.int32))
    ki = jnp.asarray(np.array([p[1] for p in pairs], np.int32))

    def body(qi_ref, ki_ref, q_ref, k_ref, v_ref, o_ref, lse_ref, m_s, l_s, acc_s):
        p = pl.program_id(1)
        i, j = qi_ref[p], ki_ref[p]

        @pl.when(j == 0)
        def _():
            m_s[...] = jnp.full(m_s.shape, NEG, F32)
            l_s[...] = jnp.zeros(l_s.shape, F32)
            acc_s[...] = jnp.zeros(acc_s.shape, F32)

        def step(masked):
            for hh in range(hb):
                s = lax.dot_general(q_ref[hh], k_ref[hh], NT_DIMS, preferred_element_type=F32)
                if masked:
                    row = lax.broadcasted_iota(jnp.int32, (t, t), 0)
                    col = lax.broadcasted_iota(jnp.int32, (t, t), 1)
                    s = jnp.where(row >= col, s, NEG)
                mc = s[:, 0:LANES]
                for c in range(1, nc):
                    mc = jnp.maximum(mc, s[:, c * LANES:(c + 1) * LANES])
                m_prev = m_s[hh]
                m_new = jnp.maximum(m_prev, jnp.max(mc, axis=1, keepdims=True))
                alpha = jnp.exp2(m_prev - m_new)
                pr = jnp.exp2(s - jnp.concatenate([m_new] * nc, axis=1))
                ls = pr[:, 0:LANES]
                for c in range(1, nc):
                    ls = ls + pr[:, c * LANES:(c + 1) * LANES]
                l_s[hh] = alpha * l_s[hh] + ls
                acc_s[hh] = alpha * acc_s[hh] + jnp.dot(pr.astype(BF16), v_ref[hh], preferred_element_type=F32)
                m_s[hh] = m_new

        @pl.when(j < i)
        def _():
            step(False)

        @pl.when(j == i)
        def _():
            step(True)
            for hh in range(hb):
                l = jnp.sum(l_s[hh], axis=1, keepdims=True)
                o_ref[:, hh * V_HEAD:(hh + 1) * V_HEAD] = acc_s[hh] / l
                lse_ref[hh] = m_s[hh][:, 0:1] + jnp.log2(l)

    grid_spec = pltpu.PrefetchScalarGridSpec(
        num_scalar_prefetch=2, grid=(H // hb, len(pairs)),
        in_specs=[pl.BlockSpec((hb, t, QK_PAD), lambda h, p, qi, ki: (h, qi[p], 0)),
                  pl.BlockSpec((hb, t, QK_PAD), lambda h, p, qi, ki: (h, ki[p], 0)),
                  pl.BlockSpec((hb, t, V_HEAD), lambda h, p, qi, ki: (h, ki[p], 0))],
        out_specs=[pl.BlockSpec((t, hb * V_HEAD), lambda h, p, qi, ki: (qi[p], h)),
                   pl.BlockSpec((hb, t, 1), lambda h, p, qi, ki: (h, qi[p], 0))],
        scratch_shapes=[pltpu.VMEM((hb, t, LANES), F32), pltpu.VMEM((hb, t, LANES), F32),
                        pltpu.VMEM((hb, t, V_HEAD), F32)],
    )
    return pl.pallas_call(
        body, name="attn_fwd", grid_spec=grid_spec,
        out_shape=[jax.ShapeDtypeStruct((S, H * V_HEAD), F32), jax.ShapeDtypeStruct((H, S, 1), F32)],
        compiler_params=_params(("parallel", "arbitrary")),
    )(qi, ki, q, k, v)


def _attn_bwd(q, k, v, do, lse_row, delta_row, *, t):
    H, S, _ = q.shape
    n = S // t
    pairs = [(i, j) for j in range(n) for i in range(j, n)]
    qi = jnp.asarray(np.array([p[0] for p in pairs], np.int32))
    ki = jnp.asarray(np.array([p[1] for p in pairs], np.int32))

    def body(qi_ref, ki_ref, q_ref, k_ref, v_ref, do_ref, lse_ref, dl_ref, dq_ref, dk_ref, dv_ref, dk_s, dv_s):
        p = pl.program_id(1)
        i, j = qi_ref[p], ki_ref[p]

        @pl.when(p == 0)
        def _():
            dq_ref[...] = jnp.zeros_like(dq_ref)

        def step(masked):
            qb, kb, vb, dob = q_ref[0], k_ref[0], v_ref[0], do_ref[...]
            st = lax.dot_general(kb, qb, NT_DIMS, preferred_element_type=F32)
            if masked:
                krow = lax.broadcasted_iota(jnp.int32, (t, t), 0)
                qcol = lax.broadcasted_iota(jnp.int32, (t, t), 1)
                st = jnp.where(krow <= qcol, st, NEG)
            pt = jnp.exp2(st - lse_ref[0])
            dvp = jnp.dot(pt.astype(BF16), dob, preferred_element_type=F32)
            dpt = lax.dot_general(vb, dob, NT_DIMS, preferred_element_type=F32)
            dst = (pt * (dpt - dl_ref[0])).astype(BF16)
            dkp = jnp.dot(dst, qb, preferred_element_type=F32)
            rows = pl.ds(pl.multiple_of(i * t, t), t)
            dq_ref[0, rows, :] += lax.dot_general(dst, kb, TN_DIMS, preferred_element_type=F32)
            return dkp, dvp

        @pl.when(i == j)
        def _():
            dkp, dvp = step(True)
            dk_s[...] = dkp
            dv_s[...] = dvp

        @pl.when(i != j)
        def _():
            dkp, dvp = step(False)
            dk_s[...] += dkp
            dv_s[...] += dvp

        @pl.when(i == n - 1)
        def _():
            dk_ref[0] = dk_s[...] * LN2
            dv_ref[0] = dv_s[...]

    grid_spec = pltpu.PrefetchScalarGridSpec(
        num_scalar_prefetch=2, grid=(H, len(pairs)),
        in_specs=[pl.BlockSpec((1, t, QK_PAD), lambda h, p, qi, ki: (h, qi[p], 0)),
                  pl.BlockSpec((1, t, QK_PAD), lambda h, p, qi, ki: (h, ki[p], 0)),
                  pl.BlockSpec((1, t, V_HEAD), lambda h, p, qi, ki: (h, ki[p], 0)),
                  pl.BlockSpec((t, V_HEAD), lambda h, p, qi, ki: (qi[p], h)),
                  pl.BlockSpec((1, 1, t), lambda h, p, qi, ki: (h, 0, qi[p])),
                  pl.BlockSpec((1, 1, t), lambda h, p, qi, ki: (h, 0, qi[p]))],
        out_specs=[pl.BlockSpec((1, S, QK_PAD), lambda h, p, qi, ki: (h, 0, 0)),
                   pl.BlockSpec((1, t, QK_PAD), lambda h, p, qi, ki: (h, ki[p], 0)),
                   pl.BlockSpec((1, t, V_HEAD), lambda h, p, qi, ki: (h, ki[p], 0))],
        scratch_shapes=[pltpu.VMEM((t, QK_PAD), F32), pltpu.VMEM((t, V_HEAD), F32)],
    )
    return pl.pallas_call(
        body, name="attn_bwd", grid_spec=grid_spec,
        out_shape=[jax.ShapeDtypeStruct((H, S, QK_PAD), F32), jax.ShapeDtypeStruct((H, S, QK_PAD), F32),
                   jax.ShapeDtypeStruct((H, S, V_HEAD), F32)],
        compiler_params=_params(("parallel", "arbitrary")),
    )(qi, ki, q, k, v, do, lse_row, delta_row)


def _merge_fwd(h, z_main, o, *, ts):
    S = h.shape[0]
    D = D_MODEL

    def body(h_ref, rg_ref, ga_ref, gb_ref, o_ref, m_ref):
        gl, _ = _gelu_and_grad(rg_ref[...])
        m = _sigmoid(ga_ref[...]) * (h_ref[...] * gl) + _sigmoid(gb_ref[...]) * o_ref[...]
        m_ref[...] = m.astype(BF16)

    col = lambda c: pl.BlockSpec((ts, D), lambda i: (i, c))
    return pl.pallas_call(
        body, name="merge_fwd", grid=(S // ts,),
        in_specs=[col(0), col(1), col(2), col(3), col(0)],
        out_specs=col(0), out_shape=jax.ShapeDtypeStruct((S, D), BF16),
        compiler_params=_params(("parallel",)),
    )(h, z_main, z_main, z_main, o)


def _my_place():
    return lax.axis_index("x"), lax.axis_index("y"), lax.axis_index("c")


def _all_gather(shards, *, name):
    n = len(shards)

    def body(*refs):
        x_refs, out_refs = refs[:n], refs[n:2 * n]
        send_sems, recv_sems, local_sems = refs[2 * n:]
        x, y, c = _my_place()
        me, sibling = (x, y, c), (x, y, 1 - c)
        chips = [(1 - x, y), (x, 1 - y), (1 - x, 1 - y)]

        def slot(a, px, py, pc):
            return out_refs[a].at[4 * px + 2 * py + pc]

        def copy(a, k, block, to, src=None):
            return pltpu.make_async_remote_copy(
                src_ref=slot(a, *block) if src is None else src, dst_ref=slot(a, *block),
                send_sem=send_sems.at[7 * a + k], recv_sem=recv_sems.at[7 * a + k], device_id=to, device_id_type=MESH)

        mine = [pltpu.make_async_copy(x_refs[a], slot(a, *me), local_sems.at[a]) for a in range(n)]
        for cp in mine:
            cp.start()
        first = []
        for a in range(n):
            first.append(copy(a, 0, me, sibling, src=x_refs[a]))
            first += [copy(a, 1 + j, me, (*chip, c), src=x_refs[a]) for j, chip in enumerate(chips)]
        for cp in first:
            cp.start()
        passed = []
        for a in range(n):
            for j, chip in enumerate(chips):
                copy(a, 1 + j, (*chip, c), me).wait_recv()
                fwd = copy(a, 4 + j, (*chip, c), sibling)
                fwd.start()
                passed.append(fwd)
        for a in range(n):
            copy(a, 0, sibling, me).wait_recv()
            for j, chip in enumerate(chips):
                copy(a, 4 + j, (*chip, 1 - c), me).wait_recv()
        for cp in first + passed:
            cp.wait_send()
        for cp in mine:
            cp.wait()

    hbm = pl.BlockSpec(memory_space=pl.ANY)
    return pl.pallas_call(
        body, name=name, out_shape=[jax.ShapeDtypeStruct((N_DEV, *s.shape), s.dtype) for s in shards],
        in_specs=[hbm] * n, out_specs=[hbm] * n,
        scratch_shapes=[pltpu.SemaphoreType.DMA((7 * n,)), pltpu.SemaphoreType.DMA((7 * n,)),
                        pltpu.SemaphoreType.DMA((n,))],
    )(*shards)


def _pushes(src_refs, land_refs, send_sems, recv_sems, slab_per_peer):
    x, y, c = _my_place()
    me = 4 * x + 2 * y + c
    copies = []
    for a in range(len(src_refs)):
        for k in range(1, N_DEV):
            px, py, pc = x ^ (k >> 2), y ^ ((k >> 1) & 1), c ^ (k & 1)
            src = src_refs[a].at[4 * px + 2 * py + pc] if slab_per_peer else src_refs[a]
            copies.append(pltpu.make_async_remote_copy(
                src_ref=src, dst_ref=land_refs[a].at[me], send_sem=send_sems.at[7 * a + k - 1],
                recv_sem=recv_sems.at[7 * a + k - 1], device_id=(px, py, pc), device_id_type=MESH))
    return copies


def _push_start(srcs, *, name, slab_per_peer):
    n = len(srcs)
    lands = [lax.empty((N_DEV, *(s.shape[1:] if slab_per_peer else s.shape)), s.dtype) for s in srcs]

    def body(*refs):
        src_refs, land_refs = refs[:n], refs[n:2 * n]
        send_sems, recv_sems, token = refs[2 * n], refs[2 * n + 1], refs[-1]
        for cp in _pushes(src_refs, land_refs, send_sems, recv_sems, slab_per_peer):
            cp.start()
        token[...] = jnp.zeros_like(token)

    hbm = pl.BlockSpec(memory_space=pltpu.HBM)
    sem = pl.BlockSpec(memory_space=pltpu.SEMAPHORE)
    out = pl.pallas_call(
        body, name=name,
        out_shape=(pltpu.SemaphoreType.DMA((7 * n,)), pltpu.SemaphoreType.DMA((7 * n,)),
                   *[pltpu.HBM(a.shape, a.dtype) for a in srcs + lands], jax.ShapeDtypeStruct((8, LANES), F32)),
        in_specs=[hbm] * (2 * n), out_specs=(sem, sem, *[hbm] * (2 * n), pl.BlockSpec(memory_space=pltpu.VMEM)),
        input_output_aliases={i: 2 + i for i in range(2 * n)},
        compiler_params=pltpu.CompilerParams(has_side_effects=pltpu.SideEffectType.DATAFLOW_SIDE_EFFECTING),
    )(*[pltpu.with_memory_space_constraint(a, pltpu.HBM) for a in srcs + lands])
    return out[0], out[1], list(out[2:2 + n]), list(out[2 + n:2 + 2 * n]), out[-1]


def _push_wait(send_sems, recv_sems, srcs, lands, after, *, name, slab_per_peer):
    n = len(srcs)

    def body(*refs):
        src_refs, land_refs = refs[:n], refs[n:2 * n]
        s_sems, r_sems = refs[2 * n], refs[2 * n + 1]
        for cp in _pushes(src_refs, land_refs, s_sems, r_sems, slab_per_peer):
            cp.wait_send()
            cp.wait_recv()

    hbm = pl.BlockSpec(memory_space=pltpu.HBM)
    sem = pl.BlockSpec(memory_space=pltpu.SEMAPHORE)
    out = pl.pallas_call(
        body, name=name, out_shape=tuple(pltpu.HBM(a.shape, a.dtype) for a in srcs + lands),
        in_specs=[hbm] * (2 * n) + [sem, sem, pl.BlockSpec(memory_space=pl.ANY)], out_specs=tuple([hbm] * (2 * n)),
        input_output_aliases={i: i for i in range(2 * n)},
        compiler_params=pltpu.CompilerParams(has_side_effects=pltpu.SideEffectType.DATAFLOW_SIDE_EFFECTING),
    )(*srcs, *lands, send_sems, recv_sems, after)
    return list(out[:n]), list(out[n:])


def _adamw(w, m, v, gparts, *, tr, name):
    R, C = w.shape
    n_parts = gparts.shape[0]

    def body(w_ref, m_ref, v_ref, gp_ref, g_ref, d_ref, nm_ref, nv_ref):
        g = gp_ref[0].astype(F32)
        for p in range(1, n_parts):
            g = g + gp_ref[p].astype(F32)
        wv = w_ref[...]
        m_new = ADAM_B1 * m_ref[...] + (1.0 - ADAM_B1) * g
        v_new = ADAM_B2 * v_ref[...] + (1.0 - ADAM_B2) * (g * g)
        m_hat = m_new / (1.0 - ADAM_B1 ** ADAM_STEP)
        v_hat = v_new / (1.0 - ADAM_B2 ** ADAM_STEP)
        g_ref[...] = g
        d_ref[...] = -ADAM_LR * (m_hat / (jnp.sqrt(v_hat) + ADAM_EPS) + ADAM_WD * wv)
        nm_ref[...] = m_new
        nv_ref[...] = v_new

    row = pl.BlockSpec((tr, C), lambda i: (i, 0))
    shp = jax.ShapeDtypeStruct((R, C), F32)
    return pl.pallas_call(
        body, name=name, grid=(R // tr,),
        in_specs=[row, row, row, pl.BlockSpec((n_parts, tr, C), lambda i: (0, i, 0))],
        out_specs=[row, row, row, row], out_shape=[shp, shp, shp, shp],
        compiler_params=_params(("parallel",)),
    )(w, m, v, gparts)


def _rope_tables(s):
    pos = jnp.arange(s, dtype=F32)
    inv_freq = 1.0 / (ROPE_THETA ** (jnp.arange(0, QK_ROPE, 2, dtype=F32) / QK_ROPE))
    ang = pos[:, None] * inv_freq[None, :]
    cos, sin = jnp.cos(ang), jnp.sin(ang)
    zero = jnp.zeros((s, LANES - QK_ROPE), F32)
    return jnp.concatenate([cos, cos, zero], -1), jnp.concatenate([sin, sin, zero], -1)


def _pick(n, want):
    t = min(n, want)
    assert n % t == 0
    return t


def _local_step(x, target, wts, small, hooks):
    S = x.shape[0]
    H = MLA_HEADS
    ts = _pick(S, 512)
    tm = _pick(S, 512)
    tm_wide = _pick(S, 1024)
    tk_s = _pick(S, 2048)
    row = lambda v: v.reshape(1, -1)
    w_in = wts["w_in"]
    w_main = jnp.concatenate([w_in[:, 0:2048], w_in[:, 2624:4672]], axis=1)
    w_ckv = jnp.concatenate([w_in[:, 2048:2624], jnp.zeros((D_MODEL, CKV_W - 576), BF16)], axis=1)
    w_uq3 = wts["w_uq"].reshape(Q_LORA, H, QK_NOPE + QK_ROPE)
    w_uq_p = jnp.concatenate(
        [w_uq3[:, :, :QK_NOPE].reshape(Q_LORA, H * QK_NOPE),
         jnp.pad(w_uq3[:, :, QK_NOPE:], ((0, 0), (0, 0), (0, LANES - QK_ROPE))).reshape(Q_LORA, H * LANES)], axis=1)
    w_ukv = wts["w_ukv"]
    cos, sin = _rope_tables(S)
    conv_w, conv_b = small["conv_w"], row(small["conv_b"])
    wa, wx = small["lru_wa"].astype(BF16), small["lru_wx"].astype(BF16)
    wat, wxt = jnp.swapaxes(wa, 1, 2), jnp.swapaxes(wx, 1, 2)
    ba, bx = small["lru_ba"].reshape(RNN_BLOCKS, 1, RNN_BLOCK_W), small["lru_bx"].reshape(RNN_BLOCKS, 1, RNN_BLOCK_W)
    lam = row(small["lru_lambda"])
    q_norm, kv_norm = row(small["q_norm"]), row(small["kv_norm"])
    norm_mix, norm_mlp, norm_final = row(small["norm_mix"]), row(small["norm_mlp"]), row(small["norm_final"])

    xn = _rmsnorm_cast(x, norm_mix, ts=ts, name="norm_mix")
    ident = lambda acc: (acc,)
    (z_main,) = _mm(xn, w_main, name="z_main", tm=tm_wide, tn=1024, tk=1024, outs=[("tile", F32)], epilogue=ident)
    (z_ckv,) = _mm(xn, w_ckv, name="z_ckv", tm=tm, tn=CKV_W, tk=1024, outs=[("tile", F32)], epilogue=ident)
    tt = _pick(S, 256)
    h = _lru_fwd(z_main, conv_w, conv_b, wa, ba, wx, bx, lam, tt=tt)
    q, k, v = _mla_proj(z_ckv, q_norm, kv_norm, w_uq_p, w_ukv, cos, sin, ts=_pick(S, 256))
    ta = _pick(S, 1024)
    o, lse = _attn_fwd(q, k, v, t=ta, hb=2)
    merged = _merge_fwd(h, z_main, o, ts=_pick(S, 256))
    w_out, w_up, w_down = hooks["weights_later"](merged)

    def ep_h1(acc, xv, g):
        h1 = acc + xv
        n2, _ = _rms_fwd(h1, g)
        return h1, n2

    h1, n2 = _mm(merged, w_out, name="h1", tm=tm, tn=1024, tk=1024, outs=[("tile", F32), ("tile", BF16)],
                 epilogue=ep_h1, extras=[("tile", x), ("row", norm_mlp)])

    def ep_up(acc):
        r = jnp.maximum(acc, 0.0)
        return r * r, r

    act, relu = _mm(n2, w_up, name="up", tm=tm_wide, tn=1024, tk=1024, outs=[("tile", BF16), ("tile", BF16)],
                    epilogue=ep_up)

    def ep_loss(acc, h1v, tgt, g):
        h2 = acc + h1v
        y, _ = _rms_fwd(h2, g)
        err = y - tgt
        loss_rows = 0.5 * jnp.mean(err * err, axis=-1, keepdims=True)
        dy = err * (1.0 / D_MODEL)
        dh2, dg_rows = _rms_bwd(dy, h2, g)
        lsum = jnp.sum(loss_rows, axis=0, keepdims=True)
        return dh2, dh2, jnp.sum(dg_rows, axis=0, keepdims=True), jnp.broadcast_to(lsum, (1, D_MODEL))

    dh2, dh2b, dnf_p, loss_p = _mm(
        act, w_down, name="down_loss", tm=tm, tn=1024, tk=D_FF,
        outs=[("tile", F32), ("tile", BF16), ("rowpart", F32), ("rowpart", F32)], epilogue=ep_loss,
        extras=[("tile", h1), ("tile", target), ("row", norm_final)])
    loss_part = jnp.sum(loss_p[:, 0, 0])
    d_norm_final = jnp.sum(dnf_p, axis=(0, 1))

    def ep_du(acc, r):
        return (acc * (2.0 * r.astype(F32)),)

    (du,) = _mm(dh2b, w_down, name="d_act", tb=True, tm=tm_wide, tn=1024, tk=1024, outs=[("tile", BF16)], epilogue=ep_du,
                extras=[("tile", relu)])

    def ep_dh1(acc, h1v, dh2v, g):
        dv, dg_rows = _rms_bwd(acc, h1v, g)
        dh1 = dh2v + dv
        return dh1, dh1, jnp.sum(dg_rows, axis=0, keepdims=True)

    dh1, dh1b, dnm_p = _mm(du, w_up, name="d_n2", tb=True, tm=tm, tn=1024, tk=D_FF,
                           outs=[("tile", F32), ("tile", BF16), ("rowpart", F32)], epilogue=ep_dh1,
                           extras=[("tile", h1), ("tile", dh2), ("row", norm_mlp)])
    d_norm_mlp = jnp.sum(dnm_p, axis=(0, 1))
    tn_mm = functools.partial(_mm, ta=True, tk=tk_s, outs=[("tile", BF16)], epilogue=ident)
    (d_w_down,) = tn_mm(act, dh2b, name="dw_down", tm=1024, tn=1024)
    (p_w_up,) = _mm(n2, du, name="dw_up", ta=True, tk=tk_s, tm=1024, tn=D_FF // N_DEV, outs=[("colshard", BF16)],
                    epilogue=ident)
    (d_w_out,) = tn_mm(merged, dh1b, name="dw_out", tm=1024, tn=1024)
    early = [d_w_out.reshape(N_DEV, -1, D_MODEL), p_w_up, d_w_down.reshape(N_DEV, -1, D_MODEL)]
    w_out = w_out + hooks["send"]("early", early)[0, 0].astype(BF16)

    tmm = _pick(S, 256)

    def ep_dmerge(dm, hv, rg, ga, gb, ov):
        gl, dgl = _gelu_and_grad(rg)
        sa, sb = _sigmoid(ga), _sigmoid(gb)
        ya = hv * gl
        dya = dm * sa
        do = dm * sb
        dga = dm * ya * sa * (1.0 - sa)
        dgb = dm * ov * sb * (1.0 - sb)
        dh = dya * gl
        drg = dya * hv * dgl
        dov = do * ov
        lane = lax.broadcasted_iota(jnp.int32, (dm.shape[0], LANES), 1)
        delta = jnp.zeros((dm.shape[0], LANES), F32)
        for hh in range(H):
            dsum = jnp.sum(dov[:, hh * V_HEAD:(hh + 1) * V_HEAD], axis=1, keepdims=True)
            delta = jnp.where(lane == hh, dsum, delta)
        return dh, drg, dga, dgb, do, delta

    dh_lru, d_rg, d_ga, d_gb, do, delta_w = _mm(
        dh1b, w_out, name="d_merge", tb=True, tm=tmm, tn=1024, tk=1024,
        outs=[("tile", F32), ("tile", BF16), ("tile", BF16), ("tile", BF16), ("tile", BF16), ("side", F32)],
        epilogue=ep_dmerge,
        extras=[("tile", h), ("tilecol", z_main, 1), ("tilecol", z_main, 2), ("tilecol", z_main, 3), ("tile", o)])
    delta_row = delta_w[:, :H].T.reshape(H, 1, S)
    lse_row = lse.reshape(H, 1, S)

    dq, dk, dv = _attn_bwd(q, k, v, do, lse_row, delta_row, t=ta)
    dz_ckv, d_w_uq_p, d_w_ukv, d_q_norm, d_kv_norm = _mla_proj_bwd(
        z_ckv, dq, dk, dv, q_norm, kv_norm, w_uq_p.T, w_ukv.T, cos, sin, ts=_pick(S, 256))
    d_w_uq = jnp.concatenate(
        [d_w_uq_p[:, :H * QK_NOPE].reshape(Q_LORA, H, QK_NOPE),
         d_w_uq_p[:, H * QK_NOPE:].reshape(Q_LORA, H, LANES)[:, :, :QK_ROPE]], axis=2).reshape(Q_LORA, -1)

    d_rx, d_wa, d_wx, d_ba, d_bx, d_lam, d_conv_w, d_conv_b = _lru_bwd(
        z_main, h, dh_lru, conv_w, conv_b, wa, wat, ba, wx, wxt, bx, lam, tt=tt)

    dz_main = jnp.concatenate([d_rx, d_rg, d_ga, d_gb], axis=1)
    (d_w_main,) = tn_mm(xn, dz_main, name="dw_main", tm=1024, tn=1024)
    (d_w_ckv,) = tn_mm(xn, dz_ckv, name="dw_ckv", tm=1024, tn=CKV_W)
    d_w_in = jnp.concatenate([d_w_main[:, 0:2048], d_w_ckv[:, 0:576], d_w_main[:, 2048:4096]], axis=1)

    def col_parts(full):
        r = full.shape[0]
        return jnp.transpose(full.astype(BF16).reshape(r, N_DEV, -1), (1, 0, 2))

    late = [col_parts(d_w_in), col_parts(d_w_uq), col_parts(d_w_ukv)]
    norm_mix = norm_mix + hooks["send"]("late", late)[0, 0]
    (dxn_ckv,) = _mm(dz_ckv, w_ckv, name="dxn_ckv", tb=True, tm=tm, tn=1024, tk=CKV_W, outs=[("tile", F32)],
                     epilogue=ident)

    def ep_dx(acc, part, xv, dh1v, g):
        dv, dg_rows = _rms_bwd(acc + part, xv, g)
        return dh1v + dv, jnp.sum(dg_rows, axis=0, keepdims=True)

    grad_x, dnx_p = _mm(dz_main, w_main, name="dx", tb=True, tm=tm, tn=1024, tk=4 * D_MODEL,
                        outs=[("tile", F32), ("rowpart", F32)], epilogue=ep_dx,
                        extras=[("tile", dxn_ckv), ("tile", x), ("tile", dh1), ("row", norm_mix)])
    d_norm_mix = jnp.sum(dnx_p, axis=(0, 1))
    sm = {"norm_mix": d_norm_mix, "conv_w": d_conv_w, "conv_b": d_conv_b.reshape(-1), "lru_wa": d_wa,
          "lru_ba": d_ba.reshape(RNN_BLOCKS, RNN_BLOCK_W), "lru_wx": d_wx, "lru_bx": d_bx.reshape(RNN_BLOCKS, RNN_BLOCK_W),
          "lru_lambda": d_lam.reshape(-1), "q_norm": d_q_norm.reshape(-1), "kv_norm": d_kv_norm.reshape(-1),
          "norm_mlp": d_norm_mlp, "norm_final": d_norm_final}
    return loss_part, grad_x, sm


BIG = ("w_in", "w_uq", "w_ukv", "w_out", "w_up", "w_down")
SMALL = ("norm_mix", "conv_b", "lru_wa", "lru_ba", "lru_wx", "lru_bx", "lru_lambda", "q_norm", "kv_norm", "norm_mlp",
         "norm_final")
WEIGHTS = ("norm_mix", "w_in", "conv_w", "conv_b", "lru_wa", "lru_ba", "lru_wx", "lru_bx", "lru_lambda", "q_norm", "w_uq",
           "kv_norm", "w_ukv", "w_out", "norm_mlp", "w_up", "w_down", "norm_final")
ADAM_TILE_ROWS = {"w_in": 256, "w_uq": 128, "w_ukv": 128, "w_out": 64, "w_up": 256, "w_down": 128}
CONV_ROWS = N_DEV * 8


def _rows(a):
    return a.reshape(-1, LANES)


def _pad_rows(a, mult):
    r = a.shape[-2]
    pad = (-r) % mult
    if pad == 0:
        return a
    cfg = [(0, 0)] * (a.ndim - 2) + [(0, pad), (0, 0)]
    return jnp.pad(a, cfg)


def _cols_from_shards(g):
    return jnp.transpose(g, (1, 0, 2)).reshape(g.shape[1], -1)


def kernel(x, norm_mix, w_in, conv_w, conv_b, lru_wa, lru_ba, lru_wx, lru_bx, lru_lambda, q_norm, w_uq, kv_norm, w_ukv, w_out, norm_mlp, w_up, w_down, norm_final, loss_target, m_norm_mix, m_w_in, m_conv_w, m_conv_b, m_lru_wa, m_lru_ba, m_lru_wx, m_lru_bx, m_lru_lambda, m_q_norm, m_w_uq, m_kv_norm, m_w_ukv, m_w_out, m_norm_mlp, m_w_up, m_w_down, m_norm_final, v_norm_mix, v_w_in, v_conv_w, v_conv_b, v_lru_wa, v_lru_ba, v_lru_wx, v_lru_bx, v_lru_lambda, v_q_norm, v_w_uq, v_kv_norm, v_w_ukv, v_w_out, v_norm_mlp, v_w_up, v_w_down, v_norm_final):
    W = dict(norm_mix=norm_mix, w_in=w_in, conv_w=conv_w, conv_b=conv_b, lru_wa=lru_wa, lru_ba=lru_ba, lru_wx=lru_wx,
             lru_bx=lru_bx, lru_lambda=lru_lambda, q_norm=q_norm, w_uq=w_uq, kv_norm=kv_norm, w_ukv=w_ukv, w_out=w_out,
             norm_mlp=norm_mlp, w_up=w_up, w_down=w_down, norm_final=norm_final)
    M = dict(norm_mix=m_norm_mix, w_in=m_w_in, conv_w=m_conv_w, conv_b=m_conv_b, lru_wa=m_lru_wa, lru_ba=m_lru_ba,
             lru_wx=m_lru_wx, lru_bx=m_lru_bx, lru_lambda=m_lru_lambda, q_norm=m_q_norm, w_uq=m_w_uq, kv_norm=m_kv_norm,
             w_ukv=m_w_ukv, w_out=m_w_out, norm_mlp=m_norm_mlp, w_up=m_w_up, w_down=m_w_down, norm_final=m_norm_final)
    V = dict(norm_mix=v_norm_mix, w_in=v_w_in, conv_w=v_conv_w, conv_b=v_conv_b, lru_wa=v_lru_wa, lru_ba=v_lru_ba,
             lru_wx=v_lru_wx, lru_bx=v_lru_bx, lru_lambda=v_lru_lambda, q_norm=v_q_norm, w_uq=v_w_uq, kv_norm=v_kv_norm,
             w_ukv=v_w_ukv, w_out=v_w_out, norm_mlp=v_norm_mlp, w_up=v_w_up, w_down=v_w_down, norm_final=v_norm_final)
    me = 4 * lax.axis_index("x") + 2 * lax.axis_index("y") + lax.axis_index("c")

    first, later = ("w_in", "w_uq", "w_ukv"), ("w_out", "w_up", "w_down")
    got = _all_gather([W[n].astype(BF16) for n in first] + [_pad_rows(conv_w, 8)], name="gather_weights")
    wts = {"w_in": _cols_from_shards(got[0]), "w_uq": _cols_from_shards(got[1]), "w_ukv": _cols_from_shards(got[2])}
    w_send, w_recv, w_src, w_land, zeros = _push_start([W[n].astype(BF16) for n in later], name="gather_later_start",
                                                       slab_per_peer=False)
    small = {n: W[n] for n in SMALL}
    small["conv_w"] = _cols_from_shards(got[3][:, :CONV_WIDTH])
    small["norm_mix"] = norm_mix + zeros[0, 0]

    def with_own_slab(land, mine):
        return lax.dynamic_update_slice(land, mine, (me, 0, 0))

    def weights_later(after):
        srcs, lands = _push_wait(w_send, w_recv, w_src, w_land, after, name="gather_later_wait", slab_per_peer=False)
        w_out_g, w_up_g, w_down_g = [with_own_slab(l, s[None]) for l, s in zip(lands, srcs)]
        return w_out_g.reshape(-1, D_MODEL), _cols_from_shards(w_up_g), w_down_g.reshape(-1, D_MODEL)

    sent = {}

    def send(group, parts):
        sent[group] = _push_start(parts, name="exchange_" + group + "_start", slab_per_peer=True)
        return sent[group][4]

    loss_part, grad_x, g_small = _local_step(x[0], loss_target[0], wts, small,
                                              {"weights_later": weights_later, "send": send})

    G, Dl, NM, NV = {}, {}, {}, {}
    for group, names in (("early", later), ("late", first)):
        s_sems, r_sems, srcs, lands, _ = sent[group]
        srcs, lands = _push_wait(s_sems, r_sems, srcs, lands, grad_x, name="exchange_" + group + "_wait",
                                 slab_per_peer=True)
        for n, src, land in zip(names, srcs, lands):
            parts = with_own_slab(land, lax.dynamic_slice(src, (me, 0, 0), (1, *src.shape[1:])))
            G[n], Dl[n], NM[n], NV[n] = _adamw(W[n], M[n], V[n], parts, tr=ADAM_TILE_ROWS[n], name="adamw_" + n)

    conv_rows = _pad_rows(jnp.transpose(g_small["conv_w"].reshape(CONV_WIDTH, N_DEV, LANES), (1, 0, 2)), 8)
    loss_rows = jnp.zeros((8, LANES), F32).at[0, 0].set(loss_part)
    rows8 = lambda a: _pad_rows(_rows(a), 8)
    small_pack = jnp.concatenate(
        [conv_rows.reshape(CONV_ROWS, LANES)] + [rows8(g_small[n]) for n in SMALL] + [loss_rows], axis=0)
    (small_all,) = _all_gather([small_pack], name="gather_small")

    def pack_small(D):
        return jnp.concatenate(
            [jnp.zeros((CONV_ROWS, LANES), F32)] + [rows8(D[n]) for n in SMALL] + [jnp.zeros((8, LANES), F32)], axis=0)

    packed = _adamw(pack_small(W), pack_small(M), pack_small(V), small_all, tr=small_pack.shape[0], name="adamw_small")
    off = CONV_ROWS
    for n in SMALL:
        r = W[n].size // LANES
        for out, pk in zip((G, Dl, NM, NV), packed):
            out[n] = pk[off:off + r].reshape(W[n].shape)
        off += r + (-r) % 8
    loss = packed[0][off, 0]

    g_conv = lax.dynamic_slice(packed[0], (me * 8, 0), (8, LANES))
    conv_out = _adamw(_pad_rows(conv_w, 8), _pad_rows(m_conv_w, 8), _pad_rows(v_conv_w, 8), g_conv[None], tr=8,
                      name="adamw_conv_w")
    for out, pk in zip((G, Dl, NM, NV), conv_out):
        out["conv_w"] = pk[:CONV_WIDTH]
    return (loss, grad_x[None], *[G[n] for n in WEIGHTS], *[Dl[n] for n in WEIGHTS], *[NM[n] for n in WEIGHTS],
            *[NV[n] for n in WEIGHTS])
```

```python
import functools

import numpy as np
import jax
import jax.numpy as jnp
from jax import lax
from jax.experimental import pallas as pl
from jax.experimental.pallas import tpu as pltpu

F32 = jnp.float32
BF16 = jnp.bfloat16
MESH = pl.DeviceIdType.MESH

D_MODEL = 1024
N_DEV = 8
LANES = 128
RNN_BLOCKS = 8
RNN_BLOCK_W = 128
CONV_WIDTH = 4
LRU_C = 8.0
MLA_HEADS = 8
QK_NOPE = 128
QK_ROPE = 64
V_HEAD = 128
QK_PAD = 256
Q_LORA = 256
KV_LORA = 256
CKV_W = 640
ROPE_THETA = 10000.0
D_FF = 4096
EPS = 1e-6
ATTN_SCALE = (QK_NOPE + QK_ROPE) ** -0.5
LOG2E = 1.4426950408889634
LN2 = 0.6931471805599453
NEG = -1e30

ADAM_LR = 0.001
ADAM_B1 = 0.9
ADAM_B2 = 0.999
ADAM_EPS = 1e-08
ADAM_WD = 0.01
ADAM_STEP = 10

VMEM_LIMIT = 56 * 1024 * 1024


def _params(sem=None):
    return pltpu.CompilerParams(dimension_semantics=sem, vmem_limit_bytes=VMEM_LIMIT)


def _sigmoid(v):
    return 1.0 / (1.0 + jnp.exp(-v))


def _softplus(y):
    e = jnp.exp(-jnp.abs(y))
    u = 1.0 + e
    d = u - 1.0
    l1p = jnp.where(d == 0.0, e, jnp.log(u) * e / jnp.where(d == 0.0, 1.0, d))
    return jnp.maximum(y, 0.0) + l1p


_GELU_K = 0.7978845608028654
_GELU_C = 0.044715


def _gelu_and_grad(v):
    t = jnp.tanh(_GELU_K * (v + _GELU_C * v * v * v))
    g = 0.5 * v * (1.0 + t)
    dg = 0.5 * (1.0 + t) + 0.5 * v * (1.0 - t * t) * _GELU_K * (1.0 + 3.0 * _GELU_C * v * v)
    return g, dg


def _rms_fwd(v, g):
    rstd = lax.rsqrt(jnp.mean(v * v, axis=-1, keepdims=True) + EPS)
    return v * rstd * g, rstd


def _rms_bwd(dy, v, g):
    rstd = lax.rsqrt(jnp.mean(v * v, axis=-1, keepdims=True) + EPS)
    vh = v * rstd
    dvh = dy * g
    dv = rstd * (dvh - vh * jnp.mean(dvh * vh, axis=-1, keepdims=True))
    return dv, dy * vh


def _shift_down(v, s, fill, row):
    return jnp.where(row >= s, pltpu.roll(v, s, 0), fill)


def _shift_up(v, s, fill, row, n):
    return jnp.where(row < n - s, pltpu.roll(v, n - s, 0), fill)


def _rot_half(v, lane):
    n = v.shape[-1]
    l = lane & (LANES - 1)
    up = pltpu.roll(v, n - QK_ROPE // 2, 1)
    dn = pltpu.roll(v, QK_ROPE // 2, 1)
    return jnp.where(l < QK_ROPE // 2, -up, jnp.where(l < QK_ROPE, dn, 0.0))


def _mm(a, b, *, name, tm, tn, tk, outs, epilogue, extras=(), ta=False, tb=False):
    assert not (ta and tb)
    if ta:
        K, M = a.shape
    else:
        M, K = a.shape
    if tb:
        N, K2 = b.shape
    else:
        K2, N = b.shape
    assert K == K2 and M % tm == 0 and N % tn == 0 and K % tk == 0, (name, a.shape, b.shape)
    n_i, n_j, n_k = M // tm, N // tn, K // tk
    n_ex, n_out = len(extras), len(outs)

    def body(*refs):
        a_ref, b_ref = refs[0], refs[1]
        ex_refs = refs[2:2 + n_ex]
        out_refs = refs[2 + n_ex:2 + n_ex + n_out]
        if ta:
            part = lax.dot_general(a_ref[...], b_ref[...], (((0,), (0,)), ((), ())), preferred_element_type=F32)
        elif tb:
            part = lax.dot_general(a_ref[...], b_ref[...], (((1,), (1,)), ((), ())), preferred_element_type=F32)
        else:
            part = jnp.dot(a_ref[...], b_ref[...], preferred_element_type=F32)

        def finish(acc):
            res = epilogue(acc, *[r[...] for r in ex_refs])
            for o_ref, r in zip(out_refs, res):
                o_ref[...] = r.astype(o_ref.dtype).reshape(o_ref.shape)

        if n_k == 1:
            finish(part)
        else:
            acc_ref = refs[-1]
            k = pl.program_id(2)

            @pl.when(k == 0)
            def _():
                acc_ref[...] = part

            @pl.when(k > 0)
            def _():
                acc_ref[...] += part

            @pl.when(k == n_k - 1)
            def _():
                finish(acc_ref[...])

    a_spec = pl.BlockSpec((tk, tm), lambda j, i, k: (k, i)) if ta else pl.BlockSpec((tm, tk), lambda j, i, k: (i, k))
    b_once = dict(pipeline_mode=pl.Buffered(1)) if (n_j == 1 and n_k == 1) else {}
    if tb:
        in_specs = [a_spec, pl.BlockSpec((tn, tk), lambda j, i, k: (j, k), **b_once)]
    else:
        in_specs = [a_spec, pl.BlockSpec((tk, tn), lambda j, i, k: (k, j), **b_once)]
    for ex in extras:
        kind = ex[0]
        if kind == "tile":
            in_specs.append(pl.BlockSpec((tm, tn), lambda j, i, k: (i, j)))
        elif kind == "tilecol":
            assert n_j == 1
            in_specs.append(pl.BlockSpec((tm, tn), functools.partial(lambda c, j, i, k: (i, c), ex[2])))
        else:
            in_specs.append(pl.BlockSpec((1, tn), lambda j, i, k: (0, j)))
    out_specs, out_shape = [], []
    for kind, dt in outs:
        if kind == "tile":
            out_specs.append(pl.BlockSpec((tm, tn), lambda j, i, k: (i, j)))
            out_shape.append(jax.ShapeDtypeStruct((M, N), dt))
        elif kind == "colshard":
            out_specs.append(pl.BlockSpec((1, tm, tn), lambda j, i, k: (j, i, 0)))
            out_shape.append(jax.ShapeDtypeStruct((n_j, M, tn), dt))
        elif kind == "side":
            assert n_j == 1
            out_specs.append(pl.BlockSpec((tm, LANES), lambda j, i, k: (i, 0)))
            out_shape.append(jax.ShapeDtypeStruct((M, LANES), dt))
        else:
            out_specs.append(pl.BlockSpec((1, 1, tn), lambda j, i, k: (i, 0, j)))
            out_shape.append(jax.ShapeDtypeStruct((n_i, 1, N), dt))
    scratch = [pltpu.VMEM((tm, tn), F32)] if n_k > 1 else []
    return pl.pallas_call(
        body, name=name, grid=(n_j, n_i, n_k), in_specs=in_specs, out_specs=out_specs, out_shape=out_shape,
        scratch_shapes=scratch, compiler_params=_params(("parallel", "parallel", "arbitrary")),
    )(a, b, *[ex[1] for ex in extras])


def _rmsnorm_cast(x, g, *, ts, name):
    S, D = x.shape

    def body(x_ref, g_ref, o_ref):
        y, _ = _rms_fwd(x_ref[...], g_ref[...])
        o_ref[...] = y.astype(BF16)

    return pl.pallas_call(
        body, name=name, grid=(S // ts,),
        in_specs=[pl.BlockSpec((ts, D), lambda i: (i, 0)), pl.BlockSpec((1, D), lambda i: (0, 0))],
        out_specs=pl.BlockSpec((ts, D), lambda i: (i, 0)), out_shape=jax.ShapeDtypeStruct((S, D), BF16),
        compiler_params=_params(("parallel",)),
    )(x, g)


LRU_NB = 4


def _lru_gates(xa, wa_ref, ba_ref, wx_ref, bx_ref, lam):
    xab = xa.astype(BF16)
    W = RNN_BLOCK_W
    rs, is_ = [], []
    for j in range(LRU_NB):
        xj = xab[:, j * W:(j + 1) * W]
        rs.append(_sigmoid(jnp.dot(xj, wa_ref[j], preferred_element_type=F32) + ba_ref[j]))
        is_.append(_sigmoid(jnp.dot(xj, wx_ref[j], preferred_element_type=F32) + bx_ref[j]))
    r = jnp.concatenate(rs, axis=1)
    i = jnp.concatenate(is_, axis=1)
    sp = _softplus(-lam)
    log_a = (-LRU_C * r) * sp
    a = jnp.exp(log_a)
    y = 2.0 * log_a
    one_m = jnp.where(y > -0.01, -y * (1.0 + 0.5 * y * (1.0 + y * (1.0 / 3.0))), 1.0 - a * a)
    return r, i, sp, a, jnp.sqrt(one_m)


def _rows_before(x, tail8, k):
    e16 = jnp.concatenate([tail8, x[0:8, :]], axis=0)
    return jnp.concatenate([pltpu.roll(e16, k, 0)[8:16, :], pltpu.roll(x, k, 0)[8:, :]], axis=0)


def _rows_after(x, head8, k):
    tt = x.shape[0]
    e16 = jnp.concatenate([x[tt - 8:tt, :], head8], axis=0)
    return jnp.concatenate([pltpu.roll(x, tt - k, 0)[:tt - 8, :], pltpu.roll(e16, 16 - k, 0)[0:8, :]], axis=0)


def _scan_down(a, b, h0, a_s, b_s, c_s):
    tt, C = a.shape
    G, nch = tt // 8, C // LANES
    rin = lax.broadcasted_iota(jnp.int32, (tt, C), 0) & 7

    def in_group(v, s):
        return pltpu.roll(v.reshape(G, 8, C), s, 1).reshape(tt, C)

    A, B = a, b
    for s in (1, 2, 4):
        B = A * jnp.where(rin >= s, in_group(B, s), 0.0) + B
        A = A * jnp.where(rin >= s, in_group(A, s), 1.0)
    for j in range(nch):
        a_s[j] = A[:, j * LANES:(j + 1) * LANES]
        b_s[j] = B[:, j * LANES:(j + 1) * LANES]
    At = jnp.concatenate([a_s.at[j][pl.ds(7, G, stride=8), :] for j in range(nch)], axis=1)
    Bt = jnp.concatenate([b_s.at[j][pl.ds(7, G, stride=8), :] for j in range(nch)], axis=1)
    rowg = lax.broadcasted_iota(jnp.int32, (G, C), 0)
    s = 1
    while s < G:
        Bt = At * _shift_down(Bt, s, 0.0, rowg) + Bt
        At = At * _shift_down(At, s, 1.0, rowg)
        s *= 2
    hg = At * h0 + Bt
    cin = _shift_down(hg, 1, h0, rowg)
    for j in range(nch):
        for r in range(8):
            c_s.at[j][pl.ds(r, G, stride=8), :] = cin[:, j * LANES:(j + 1) * LANES]
    return A * jnp.concatenate([c_s[j] for j in range(nch)], axis=1) + B, hg[G - 1:G, :]


def _scan_up(c, g_in, g_next, a_s, b_s, c_s):
    tt, C = c.shape
    G, nch = tt // 8, C // LANES
    rin = lax.broadcasted_iota(jnp.int32, (tt, C), 0) & 7

    def in_group(v, s):
        return pltpu.roll(v.reshape(G, 8, C), 8 - s, 1).reshape(tt, C)

    Cc, Gv = c, g_in
    for s in (1, 2, 4):
        Gv = Gv + Cc * jnp.where(rin < 8 - s, in_group(Gv, s), 0.0)
        Cc = Cc * jnp.where(rin < 8 - s, in_group(Cc, s), 1.0)
    for j in range(nch):
        a_s[j] = Cc[:, j * LANES:(j + 1) * LANES]
        b_s[j] = Gv[:, j * LANES:(j + 1) * LANES]
    Ct = jnp.concatenate([a_s.at[j][pl.ds(0, G, stride=8), :] for j in range(nch)], axis=1)
    Gt = jnp.concatenate([b_s.at[j][pl.ds(0, G, stride=8), :] for j in range(nch)], axis=1)
    rowg = lax.broadcasted_iota(jnp.int32, (G, C), 0)
    s = 1
    while s < G:
        Gt = Gt + Ct * _shift_up(Gt, s, 0.0, rowg, G)
        Ct = Ct * _shift_up(Ct, s, 1.0, rowg, G)
        s *= 2
    gg = Gt + Ct * g_next
    cin = _shift_up(gg, 1, g_next, rowg, G)
    for j in range(nch):
        for r in range(8):
            c_s.at[j][pl.ds(r, G, stride=8), :] = cin[:, j * LANES:(j + 1) * LANES]
    return Gv + Cc * jnp.concatenate([c_s[j] for j in range(nch)], axis=1), gg[0:1, :]


def _lru_fwd(z_main, conv_w, conv_b, wa, ba, wx, bx, lam, *, tt):
    S = z_main.shape[0]
    n_t = S // tt
    BW = RNN_BLOCK_W
    W = LRU_NB * BW

    def body(x_ref, cw_ref, cb_ref, wa_ref, ba_ref, wx_ref, bx_ref, lam_ref, h_ref, tail, hc, a_s, b_s, c_s):
        t = pl.program_id(1)

        @pl.when(t == 0)
        def _():
            tail[...] = jnp.zeros((8, W), F32)
            hc[...] = jnp.zeros((8, W), F32)

        x = x_ref[...]
        before = tail[...]
        cw = cw_ref[...]
        xa = (cb_ref[...] + cw[3:4] * x + cw[2:3] * _rows_before(x, before, 1) + cw[1:2] * _rows_before(x, before, 2)
              + cw[0:1] * _rows_before(x, before, 3))
        tail[...] = x[tt - 8:tt, :]
        _r, i, _sp, a, mult = _lru_gates(xa, wa_ref, ba_ref, wx_ref, bx_ref, lam_ref[...])
        h, h_last = _scan_down(a, mult * (i * xa), hc[0:1, :], a_s, b_s, c_s)
        h_ref[...] = h
        hc[...] = jnp.broadcast_to(h_last, (8, W))

    blk = lambda n, t: (t, n)
    vec = pl.BlockSpec((1, W), lambda n, t: (0, n))
    mat = pl.BlockSpec((LRU_NB, BW, BW), lambda n, t: (n, 0, 0))
    bias = pl.BlockSpec((LRU_NB, 1, BW), lambda n, t: (n, 0, 0))
    row8 = pltpu.VMEM((8, W), F32)
    wide = pltpu.VMEM((LRU_NB, tt, LANES), F32)
    return pl.pallas_call(
        body, name="lru_fwd", grid=(RNN_BLOCKS // LRU_NB, n_t),
        in_specs=[pl.BlockSpec((tt, W), blk), pl.BlockSpec((CONV_WIDTH, W), lambda n, t: (0, n)), vec, mat, bias, mat,
                  bias, vec],
        out_specs=pl.BlockSpec((tt, W), blk), out_shape=jax.ShapeDtypeStruct((S, D_MODEL), F32),
        scratch_shapes=[row8, row8, wide, wide, wide],
        compiler_params=_params(("parallel", "arbitrary")),
    )(z_main, conv_w, conv_b, wa, ba, wx, bx, lam)


def _lru_bwd(z_main, h, dh, conv_w, conv_b, wa, wat, ba, wx, wxt, bx, lam, *, tt):
    S = z_main.shape[0]
    n_t = S // tt
    BW = RNN_BLOCK_W
    W = LRU_NB * BW
    t8 = tt // 8

    def body(x_ref, xp_ref, h_ref, hp_ref, dh_ref, cw_ref, cb_ref, wa_ref, wat_ref, ba_ref, wx_ref, wxt_ref, bx_ref,
             lam_ref, dx_ref, dwa_ref, dwx_ref, dba_ref, dbx_ref, dlam_ref, dcw_ref, dcb_ref, nxt, a_c, g_c, a_s, b_s,
             c_s):
        t = pl.program_id(1)
        tile = n_t - 1 - t

        @pl.when(t == 0)
        def _():
            a_c[...] = jnp.zeros((8, W), F32)
            g_c[...] = jnp.zeros((8, W), F32)
            nxt[...] = jnp.zeros((8, W), F32)
            dwa_ref[...] = jnp.zeros_like(dwa_ref)
            dwx_ref[...] = jnp.zeros_like(dwx_ref)
            dba_ref[...] = jnp.zeros_like(dba_ref)
            dbx_ref[...] = jnp.zeros_like(dbx_ref)
            dlam_ref[...] = jnp.zeros_like(dlam_ref)
            dcw_ref[...] = jnp.zeros_like(dcw_ref)
            dcb_ref[...] = jnp.zeros_like(dcb_ref)

        has_prev = (tile > 0).astype(F32)
        x = x_ref[...]
        before = xp_ref[...] * has_prev
        xm1, xm2, xm3 = _rows_before(x, before, 1), _rows_before(x, before, 2), _rows_before(x, before, 3)
        cw = cw_ref[...]
        xa = cb_ref[...] + cw[3:4] * x + cw[2:3] * xm1 + cw[1:2] * xm2 + cw[0:1] * xm3
        lam = lam_ref[...]
        r, i, sp, a, mult = _lru_gates(xa, wa_ref, ba_ref, wx_ref, bx_ref, lam)
        gated = i * xa
        h_prev = _rows_before(h_ref[...], hp_ref[...] * has_prev, 1)
        g, g_first = _scan_up(_rows_after(a, a_c[...], 1), dh_ref[...], g_c[0:1, :], a_s, b_s, c_s)
        a_c[...] = jnp.broadcast_to(a[0:1, :], (8, W))
        g_c[...] = jnp.broadcast_to(g_first, (8, W))
        dlog_a = g * h_prev * a - g * gated * (a * a) / mult
        dgated = g * mult
        di = dgated * xa
        dxa = dgated * i
        dr = dlog_a * (-LRU_C * sp)
        dlam_ref[...] += jnp.sum(dlog_a * (-LRU_C * r), axis=0, keepdims=True) * (-_sigmoid(-lam))
        dpr = dr * r * (1.0 - r)
        dpi = di * i * (1.0 - i)
        xab, dprb, dpib = xa.astype(BF16), dpr.astype(BF16), dpi.astype(BF16)
        tn_dims = (((0,), (0,)), ((), ()))
        back = []
        for j in range(LRU_NB):
            sl = slice(j * BW, (j + 1) * BW)
            dwa_ref[j] += lax.dot_general(xab[:, sl], dprb[:, sl], tn_dims, preferred_element_type=F32)
            dwx_ref[j] += lax.dot_general(xab[:, sl], dpib[:, sl], tn_dims, preferred_element_type=F32)
            dba_ref[j] += jnp.sum(dpr[:, sl], axis=0, keepdims=True)
            dbx_ref[j] += jnp.sum(dpi[:, sl], axis=0, keepdims=True)
            back.append(jnp.dot(dprb[:, sl], wat_ref[j], preferred_element_type=F32)
                        + jnp.dot(dpib[:, sl], wxt_ref[j], preferred_element_type=F32))
        dxa = dxa + jnp.concatenate(back, axis=1)
        after = nxt[...]
        dx = (cw[3:4] * dxa + cw[2:3] * _rows_after(dxa, after, 1) + cw[1:2] * _rows_after(dxa, after, 2)
              + cw[0:1] * _rows_after(dxa, after, 3))
        nxt[...] = dxa[0:8, :]
        dx_ref[...] = dx.astype(BF16)
        dcw_ref[3:4, :] += jnp.sum(dxa * x, axis=0, keepdims=True)
        dcw_ref[2:3, :] += jnp.sum(dxa * xm1, axis=0, keepdims=True)
        dcw_ref[1:2, :] += jnp.sum(dxa * xm2, axis=0, keepdims=True)
        dcw_ref[0:1, :] += jnp.sum(dxa * xm3, axis=0, keepdims=True)
        dcb_ref[...] += jnp.sum(dxa, axis=0, keepdims=True)

    blk = lambda n, t: (n_t - 1 - t, n)
    prev = lambda n, t: (jnp.maximum((n_t - 1 - t) * t8 - 1, 0), n)
    vec = pl.BlockSpec((1, W), lambda n, t: (0, n))
    mat = pl.BlockSpec((LRU_NB, BW, BW), lambda n, t: (n, 0, 0))
    bias = pl.BlockSpec((LRU_NB, 1, BW), lambda n, t: (n, 0, 0))
    cws = pl.BlockSpec((CONV_WIDTH, W), lambda n, t: (0, n))
    tile = pl.BlockSpec((tt, W), blk)
    prev8 = pl.BlockSpec((8, W), prev)
    row8 = pltpu.VMEM((8, W), F32)
    wide = pltpu.VMEM((LRU_NB, tt, LANES), F32)
    return pl.pallas_call(
        body, name="lru_bwd", grid=(RNN_BLOCKS // LRU_NB, n_t),
        in_specs=[tile, prev8, tile, prev8, tile, cws, vec, mat, mat, bias, mat, mat, bias, vec],
        out_specs=[tile, mat, mat, bias, bias, vec, cws, vec],
        out_shape=[jax.ShapeDtypeStruct((S, D_MODEL), BF16),
                   jax.ShapeDtypeStruct((RNN_BLOCKS, BW, BW), F32), jax.ShapeDtypeStruct((RNN_BLOCKS, BW, BW), F32),
                   jax.ShapeDtypeStruct((RNN_BLOCKS, 1, BW), F32), jax.ShapeDtypeStruct((RNN_BLOCKS, 1, BW), F32),
                   jax.ShapeDtypeStruct((1, D_MODEL), F32),
                   jax.ShapeDtypeStruct((CONV_WIDTH, D_MODEL), F32), jax.ShapeDtypeStruct((1, D_MODEL), F32)],
        scratch_shapes=[row8, row8, row8, wide, wide, wide],
        compiler_params=_params(("parallel", "arbitrary")),
    )(z_main, z_main, h, h, dh, conv_w, conv_b, wa, wat, ba, wx, wxt, bx, lam)


def _mla_proj(z_ckv, q_norm, kv_norm, w_uq, w_ukv, cos, sin, *, ts):
    S = z_ckv.shape[0]
    H = MLA_HEADS

    def body(c_ref, qn_ref, kn_ref, wq_ref, wkv_ref, cos_ref, sin_ref, q_ref, k_ref, v_ref):
        c = c_ref[...]
        cqn, _ = _rms_fwd(c[:, 0:Q_LORA], qn_ref[...])
        ckn, _ = _rms_fwd(c[:, Q_LORA:Q_LORA + KV_LORA], kn_ref[...])
        q = jnp.dot(cqn.astype(BF16), wq_ref[...], preferred_element_type=F32) * (ATTN_SCALE * LOG2E)
        kv = jnp.dot(ckn.astype(BF16), wkv_ref[...], preferred_element_type=F32)
        cos1, sin1 = cos_ref[...], sin_ref[...]
        cos8 = jnp.concatenate([cos1] * H, axis=1)
        sin8 = jnp.concatenate([sin1] * H, axis=1)
        qr = q[:, H * QK_NOPE:]
        lane8 = lax.broadcasted_iota(jnp.int32, qr.shape, 1)
        qr = qr * cos8 + _rot_half(qr, lane8) * sin8
        kr = c[:, Q_LORA + KV_LORA:]
        lane1 = lax.broadcasted_iota(jnp.int32, kr.shape, 1)
        kr = (kr * cos1 + _rot_half(kr, lane1) * sin1).astype(BF16)
        for h in range(H):
            q_ref[h, :, 0:QK_NOPE] = q[:, h * QK_NOPE:(h + 1) * QK_NOPE].astype(BF16)
            q_ref[h, :, QK_NOPE:] = qr[:, h * LANES:(h + 1) * LANES].astype(BF16)
            k_ref[h, :, 0:QK_NOPE] = kv[:, h * 2 * LANES:h * 2 * LANES + LANES].astype(BF16)
            k_ref[h, :, QK_NOPE:] = kr
            v_ref[h] = kv[:, h * 2 * LANES + LANES:(h + 1) * 2 * LANES].astype(BF16)

    full = lambda shape: pl.BlockSpec(shape, lambda i: (0,) * len(shape))
    return pl.pallas_call(
        body, name="mla_proj", grid=(S // ts,),
        in_specs=[pl.BlockSpec((ts, CKV_W), lambda i: (i, 0)), full((1, Q_LORA)), full((1, KV_LORA)),
                  full(w_uq.shape), full(w_ukv.shape), pl.BlockSpec((ts, LANES), lambda i: (i, 0)),
                  pl.BlockSpec((ts, LANES), lambda i: (i, 0))],
        out_specs=[pl.BlockSpec((H, ts, QK_PAD), lambda i: (0, i, 0)), pl.BlockSpec((H, ts, QK_PAD), lambda i: (0, i, 0)),
                   pl.BlockSpec((H, ts, V_HEAD), lambda i: (0, i, 0))],
        out_shape=[jax.ShapeDtypeStruct((H, S, QK_PAD), BF16), jax.ShapeDtypeStruct((H, S, QK_PAD), BF16),
                   jax.ShapeDtypeStruct((H, S, V_HEAD), BF16)],
        compiler_params=_params(("parallel",)),
    )(z_ckv, q_norm, kv_norm, w_uq, w_ukv, cos, sin)


def _mla_proj_bwd(z_ckv, dq, dk, dv, q_norm, kv_norm, w_uqt, w_ukvt, cos, sin, *, ts):
    S = z_ckv.shape[0]
    H = MLA_HEADS

    def body(c_ref, dq_ref, dk_ref, dv_ref, qn_ref, kn_ref, wqt_ref, wkvt_ref, cos_ref, sin_ref,
             dz_ref, dwq_ref, dwkv_ref, dqn_ref, dkn_ref):
        @pl.when(pl.program_id(0) == 0)
        def _():
            dwq_ref[...] = jnp.zeros_like(dwq_ref)
            dwkv_ref[...] = jnp.zeros_like(dwkv_ref)
            dqn_ref[...] = jnp.zeros_like(dqn_ref)
            dkn_ref[...] = jnp.zeros_like(dkn_ref)

        c = c_ref[...]
        cq, ck = c[:, 0:Q_LORA], c[:, Q_LORA:Q_LORA + KV_LORA]
        qn, kn = qn_ref[...], kn_ref[...]
        cqn, _ = _rms_fwd(cq, qn)
        ckn, _ = _rms_fwd(ck, kn)
        cos1, sin1 = cos_ref[...], sin_ref[...]
        lane1 = lax.broadcasted_iota(jnp.int32, cos1.shape, 1)

        def unrope(g):
            return g * cos1 - _rot_half(g * sin1, lane1)

        dq_all = jnp.concatenate([dq_ref[h, :, 0:QK_NOPE] for h in range(H)]
                                 + [unrope(dq_ref[h, :, QK_NOPE:]) for h in range(H)], axis=1)
        dq_all = (dq_all * ATTN_SCALE).astype(BF16)
        dkv_all = jnp.concatenate([p for h in range(H) for p in (dk_ref[h, :, 0:QK_NOPE], dv_ref[h])],
                                  axis=1).astype(BF16)
        dkr = dk_ref[0, :, QK_NOPE:]
        for h in range(1, H):
            dkr = dkr + dk_ref[h, :, QK_NOPE:]
        dkr = unrope(dkr)
        tn_dims = (((0,), (0,)), ((), ()))
        dwq_ref[...] += lax.dot_general(cqn.astype(BF16), dq_all, tn_dims, preferred_element_type=F32)
        dwkv_ref[...] += lax.dot_general(ckn.astype(BF16), dkv_all, tn_dims, preferred_element_type=F32)
        dcqn = jnp.dot(dq_all, wqt_ref[...], preferred_element_type=F32)
        dckn = jnp.dot(dkv_all, wkvt_ref[...], preferred_element_type=F32)
        dcq, dqn_rows = _rms_bwd(dcqn, cq, qn)
        dck, dkn_rows = _rms_bwd(dckn, ck, kn)
        dqn_ref[...] += jnp.sum(dqn_rows, axis=0, keepdims=True)
        dkn_ref[...] += jnp.sum(dkn_rows, axis=0, keepdims=True)
        dz_ref[:, 0:Q_LORA] = dcq.astype(BF16)
        dz_ref[:, Q_LORA:Q_LORA + KV_LORA] = dck.astype(BF16)
        dz_ref[:, Q_LORA + KV_LORA:] = dkr.astype(BF16)

    full = lambda shape: pl.BlockSpec(shape, lambda i: (0,) * len(shape))
    return pl.pallas_call(
        body, name="mla_proj_bwd", grid=(S // ts,),
        in_specs=[pl.BlockSpec((ts, CKV_W), lambda i: (i, 0)), pl.BlockSpec((H, ts, QK_PAD), lambda i: (0, i, 0)),
                  pl.BlockSpec((H, ts, QK_PAD), lambda i: (0, i, 0)), pl.BlockSpec((H, ts, V_HEAD), lambda i: (0, i, 0)),
                  full((1, Q_LORA)), full((1, KV_LORA)), full(w_uqt.shape), full(w_ukvt.shape),
                  pl.BlockSpec((ts, LANES), lambda i: (i, 0)), pl.BlockSpec((ts, LANES), lambda i: (i, 0))],
        out_specs=[pl.BlockSpec((ts, CKV_W), lambda i: (i, 0)), full((Q_LORA, w_uqt.shape[0])),
                   full((KV_LORA, w_ukvt.shape[0])), full((1, Q_LORA)), full((1, KV_LORA))],
        out_shape=[jax.ShapeDtypeStruct((S, CKV_W), BF16), jax.ShapeDtypeStruct((Q_LORA, w_uqt.shape[0]), F32),
                   jax.ShapeDtypeStruct((KV_LORA, w_ukvt.shape[0]), F32), jax.ShapeDtypeStruct((1, Q_LORA), F32),
                   jax.ShapeDtypeStruct((1, KV_LORA), F32)],
        compiler_params=_params(("arbitrary",)),
    )(z_ckv, dq, dk, dv, q_norm, kv_norm, w_uqt, w_ukvt, cos, sin)


NT_DIMS = (((1,), (1,)), ((), ()))
TN_DIMS = (((0,), (0,)), ((), ()))


def _attn_fwd(q, k, v, *, t, hb):
    H, S, _ = q.shape
    n = S // t
    nc = t // LANES
    pairs = [(i, j) for i in range(n) for j in range(i + 1)]
    qi = jnp.asarray(np.array([p[0] for p in pairs], np.int32))
    ki = jnp.asarray(np.array([p[1] for p in pairs], np.int32))

    def body(qi_ref, ki_ref, q_ref, k_ref, v_ref, o_ref, lse_ref, m_s, l_s, acc_s):
        p = pl.program_id(1)
        i, j = qi_ref[p], ki_ref[p]

        @pl.when(j == 0)
        def _():
            m_s[...] = jnp.full(m_s.shape, NEG, F32)
            l_s[...] = jnp.zeros(l_s.shape, F32)
            acc_s[...] = jnp.zeros(acc_s.shape, F32)

        def step(masked):
            for hh in range(hb):
                s = lax.dot_general(q_ref[hh], k_ref[hh], NT_DIMS, preferred_element_type=F32)
                if masked:
                    row = lax.broadcasted_iota(jnp.int32, (t, t), 0)
                    col = lax.broadcasted_iota(jnp.int32, (t, t), 1)
                    s = jnp.where(row >= col, s, NEG)
                mc = s[:, 0:LANES]
                for c in range(1, nc):
                    mc = jnp.maximum(mc, s[:, c * LANES:(c + 1) * LANES])
                m_prev = m_s[hh]
                m_new = jnp.maximum(m_prev, jnp.max(mc, axis=1, keepdims=True))
                alpha = jnp.exp2(m_prev - m_new)
                pr = jnp.exp2(s - jnp.concatenate([m_new] * nc, axis=1))
                ls = pr[:, 0:LANES]
                for c in range(1, nc):
                    ls = ls + pr[:, c * LANES:(c + 1) * LANES]
                l_s[hh] = alpha * l_s[hh] + ls
                acc_s[hh] = alpha * acc_s[hh] + jnp.dot(pr.astype(BF16), v_ref[hh], preferred_element_type=F32)
                m_s[hh] = m_new

        @pl.when(j < i)
        def _():
            step(False)

        @pl.when(j == i)
        def _():
            step(True)
            for hh in range(hb):
                l = jnp.sum(l_s[hh], axis=1, keepdims=True)
                o_ref[:, hh * V_HEAD:(hh + 1) * V_HEAD] = acc_s[hh] / l
                lse_ref[hh] = m_s[hh][:, 0:1] + jnp.log2(l)

    grid_spec = pltpu.PrefetchScalarGridSpec(
        num_scalar_prefetch=2, grid=(H // hb, len(pairs)),
        in_specs=[pl.BlockSpec((hb, t, QK_PAD), lambda h, p, qi, ki: (h, qi[p], 0)),
                  pl.BlockSpec((hb, t, QK_PAD), lambda h, p, qi, ki: (h, ki[p], 0)),
                  pl.BlockSpec((hb, t, V_HEAD), lambda h, p, qi, ki: (h, ki[p], 0))],
        out_specs=[pl.BlockSpec((t, hb * V_HEAD), lambda h, p, qi, ki: (qi[p], h)),
                   pl.BlockSpec((hb, t, 1), lambda h, p, qi, ki: (h, qi[p], 0))],
        scratch_shapes=[pltpu.VMEM((hb, t, LANES), F32), pltpu.VMEM((hb, t, LANES), F32),
                        pltpu.VMEM((hb, t, V_HEAD), F32)],
    )
    return pl.pallas_call(
        body, name="attn_fwd", grid_spec=grid_spec,
        out_shape=[jax.ShapeDtypeStruct((S, H * V_HEAD), F32), jax.ShapeDtypeStruct((H, S, 1), F32)],
        compiler_params=_params(("parallel", "arbitrary")),
    )(qi, ki, q, k, v)


def _attn_bwd(q, k, v, do, lse_row, delta_row, *, t):
    H, S, _ = q.shape
    n = S // t
    pairs = [(i, j) for j in range(n) for i in range(j, n)]
    qi = jnp.asarray(np.array([p[0] for p in pairs], np.int32))
    ki = jnp.asarray(np.array([p[1] for p in pairs], np.int32))

    def body(qi_ref, ki_ref, q_ref, k_ref, v_ref, do_ref, lse_ref, dl_ref, dq_ref, dk_ref, dv_ref, dk_s, dv_s):
        p = pl.program_id(1)
        i, j = qi_ref[p], ki_ref[p]

        @pl.when(p == 0)
        def _():
            dq_ref[...] = jnp.zeros_like(dq_ref)

        def step(masked):
            qb, kb, vb, dob = q_ref[0], k_ref[0], v_ref[0], do_ref[...]
            st = lax.dot_general(kb, qb, NT_DIMS, preferred_element_type=F32)
            if masked:
                krow = lax.broadcasted_iota(jnp.int32, (t, t), 0)
                qcol = lax.broadcasted_iota(jnp.int32, (t, t), 1)
                st = jnp.where(krow <= qcol, st, NEG)
            pt = jnp.exp2(st - lse_ref[0])
            dvp = jnp.dot(pt.astype(BF16), dob, preferred_element_type=F32)
            dpt = lax.dot_general(vb, dob, NT_DIMS, preferred_element_type=F32)
            dst = (pt * (dpt - dl_ref[0])).astype(BF16)
            dkp = jnp.dot(dst, qb, preferred_element_type=F32)
            rows = pl.ds(pl.multiple_of(i * t, t), t)
            dq_ref[0, rows, :] += lax.dot_general(dst, kb, TN_DIMS, preferred_element_type=F32)
            return dkp, dvp

        @pl.when(i == j)
        def _():
            dkp, dvp = step(True)
            dk_s[...] = dkp
            dv_s[...] = dvp

        @pl.when(i != j)
        def _():
            dkp, dvp = step(False)
            dk_s[...] += dkp
            dv_s[...] += dvp

        @pl.when(i == n - 1)
        def _():
            dk_ref[0] = dk_s[...] * LN2
            dv_ref[0] = dv_s[...]

    grid_spec = pltpu.PrefetchScalarGridSpec(
        num_scalar_prefetch=2, grid=(H, len(pairs)),
        in_specs=[pl.BlockSpec((1, t, QK_PAD), lambda h, p, qi, ki: (h, qi[p], 0)),
                  pl.BlockSpec((1, t, QK_PAD), lambda h, p, qi, ki: (h, ki[p], 0)),
                  pl.BlockSpec((1, t, V_HEAD), lambda h, p, qi, ki: (h, ki[p], 0)),
                  pl.BlockSpec((t, V_HEAD), lambda h, p, qi, ki: (qi[p], h)),
                  pl.BlockSpec((1, 1, t), lambda h, p, qi, ki: (h, 0, qi[p])),
                  pl.BlockSpec((1, 1, t), lambda h, p, qi, ki: (h, 0, qi[p]))],
        out_specs=[pl.BlockSpec((1, S, QK_PAD), lambda h, p, qi, ki: (h, 0, 0)),
                   pl.BlockSpec((1, t, QK_PAD), lambda h, p, qi, ki: (h, ki[p], 0)),
                   pl.BlockSpec((1, t, V_HEAD), lambda h, p, qi, ki: (h, ki[p], 0))],
        scratch_shapes=[pltpu.VMEM((t, QK_PAD), F32), pltpu.VMEM((t, V_HEAD), F32)],
    )
    return pl.pallas_call(
        body, name="attn_bwd", grid_spec=grid_spec,
        out_shape=[jax.ShapeDtypeStruct((H, S, QK_PAD), F32), jax.ShapeDtypeStruct((H, S, QK_PAD), F32),
                   jax.ShapeDtypeStruct((H, S, V_HEAD), F32)],
        compiler_params=_params(("parallel", "arbitrary")),
    )(qi, ki, q, k, v, do, lse_row, delta_row)


def _merge_fwd(h, z_main, o, *, ts):
    S = h.shape[0]
    D = D_MODEL

    def body(h_ref, rg_ref, ga_ref, gb_ref, o_ref, m_ref):
        gl, _ = _gelu_and_grad(rg_ref[...])
        m = _sigmoid(ga_ref[...]) * (h_ref[...] * gl) + _sigmoid(gb_ref[...]) * o_ref[...]
        m_ref[...] = m.astype(BF16)

    col = lambda c: pl.BlockSpec((ts, D), lambda i: (i, c))
    return pl.pallas_call(
        body, name="merge_fwd", grid=(S // ts,),
        in_specs=[col(0), col(1), col(2), col(3), col(0)],
        out_specs=col(0), out_shape=jax.ShapeDtypeStruct((S, D), BF16),
        compiler_params=_params(("parallel",)),
    )(h, z_main, z_main, z_main, o)


def _my_place():
    return lax.axis_index("x"), lax.axis_index("y"), lax.axis_index("c")


def _all_gather(shards, *, name):
    n = len(shards)

    def body(*refs):
        x_refs, out_refs = refs[:n], refs[n:2 * n]
        send_sems, recv_sems, local_sems = refs[2 * n:]
        x, y, c = _my_place()
        me, sibling = (x, y, c), (x, y, 1 - c)
        chips = [(1 - x, y), (x, 1 - y), (1 - x, 1 - y)]

        def slot(a, px, py, pc):
            return out_refs[a].at[4 * px + 2 * py + pc]

        def copy(a, k, block, to, src=None):
            return pltpu.make_async_remote_copy(
                src_ref=slot(a, *block) if src is None else src, dst_ref=slot(a, *block),
                send_sem=send_sems.at[7 * a + k], recv_sem=recv_sems.at[7 * a + k], device_id=to, device_id_type=MESH)

        mine = [pltpu.make_async_copy(x_refs[a], slot(a, *me), local_sems.at[a]) for a in range(n)]
        for cp in mine:
            cp.start()
        first = []
        for a in range(n):
            first.append(copy(a, 0, me, sibling, src=x_refs[a]))
            first += [copy(a, 1 + j, me, (*chip, c), src=x_refs[a]) for j, chip in enumerate(chips)]
        for cp in first:
            cp.start()
        passed = []
        for a in range(n):
            for j, chip in enumerate(chips):
                copy(a, 1 + j, (*chip, c), me).wait_recv()
                fwd = copy(a, 4 + j, (*chip, c), sibling)
                fwd.start()
                passed.append(fwd)
        for a in range(n):
            copy(a, 0, sibling, me).wait_recv()
            for j, chip in enumerate(chips):
                copy(a, 4 + j, (*chip, 1 - c), me).wait_recv()
        for cp in first + passed:
            cp.wait_send()
        for cp in mine:
            cp.wait()

    hbm = pl.BlockSpec(memory_space=pl.ANY)
    return pl.pallas_call(
        body, name=name, out_shape=[jax.ShapeDtypeStruct((N_DEV, *s.shape), s.dtype) for s in shards],
        in_specs=[hbm] * n, out_specs=[hbm] * n,
        scratch_shapes=[pltpu.SemaphoreType.DMA((7 * n,)), pltpu.SemaphoreType.DMA((7 * n,)),
                        pltpu.SemaphoreType.DMA((n,))],
    )(*shards)


def _pushes(src_refs, land_refs, send_sems, recv_sems, slab_per_peer):
    x, y, c = _my_place()
    me = 4 * x + 2 * y + c
    copies = []
    for a in range(len(src_refs)):
        for k in range(1, N_DEV):
            px, py, pc = x ^ (k >> 2), y ^ ((k >> 1) & 1), c ^ (k & 1)
            src = src_refs[a].at[4 * px + 2 * py + pc] if slab_per_peer else src_refs[a]
            copies.append(pltpu.make_async_remote_copy(
                src_ref=src, dst_ref=land_refs[a].at[me], send_sem=send_sems.at[7 * a + k - 1],
                recv_sem=recv_sems.at[7 * a + k - 1], device_id=(px, py, pc), device_id_type=MESH))
    return copies


def _push_start(srcs, *, name, slab_per_peer):
    n = len(srcs)
    lands = [lax.empty((N_DEV, *(s.shape[1:] if slab_per_peer else s.shape)), s.dtype) for s in srcs]

    def body(*refs):
        src_refs, land_refs = refs[:n], refs[n:2 * n]
        send_sems, recv_sems, token = refs[2 * n], refs[2 * n + 1], refs[-1]
        for cp in _pushes(src_refs, land_refs, send_sems, recv_sems, slab_per_peer):
            cp.start()
        token[...] = jnp.zeros_like(token)

    hbm = pl.BlockSpec(memory_space=pltpu.HBM)
    sem = pl.BlockSpec(memory_space=pltpu.SEMAPHORE)
    out = pl.pallas_call(
        body, name=name,
        out_shape=(pltpu.SemaphoreType.DMA((7 * n,)), pltpu.SemaphoreType.DMA((7 * n,)),
                   *[pltpu.HBM(a.shape, a.dtype) for a in srcs + lands], jax.ShapeDtypeStruct((8, LANES), F32)),
        in_specs=[hbm] * (2 * n), out_specs=(sem, sem, *[hbm] * (2 * n), pl.BlockSpec(memory_space=pltpu.VMEM)),
        input_output_aliases={i: 2 + i for i in range(2 * n)},
        compiler_params=pltpu.CompilerParams(has_side_effects=pltpu.SideEffectType.DATAFLOW_SIDE_EFFECTING),
    )(*[pltpu.with_memory_space_constraint(a, pltpu.HBM) for a in srcs + lands])
    return out[0], out[1], list(out[2:2 + n]), list(out[2 + n:2 + 2 * n]), out[-1]


def _push_wait(send_sems, recv_sems, srcs, lands, after, *, name, slab_per_peer):
    n = len(srcs)

    def body(*refs):
        src_refs, land_refs = refs[:n], refs[n:2 * n]
        s_sems, r_sems = refs[2 * n], refs[2 * n + 1]
        for cp in _pushes(src_refs, land_refs, s_sems, r_sems, slab_per_peer):
            cp.wait_send()
            cp.wait_recv()

    hbm = pl.BlockSpec(memory_space=pltpu.HBM)
    sem = pl.BlockSpec(memory_space=pltpu.SEMAPHORE)
    out = pl.pallas_call(
        body, name=name, out_shape=tuple(pltpu.HBM(a.shape, a.dtype) for a in srcs + lands),
        in_specs=[hbm] * (2 * n) + [sem, sem, pl.BlockSpec(memory_space=pl.ANY)], out_specs=tuple([hbm] * (2 * n)),
        input_output_aliases={i: i for i in range(2 * n)},
        compiler_params=pltpu.CompilerParams(has_side_effects=pltpu.SideEffectType.DATAFLOW_SIDE_EFFECTING),
    )(*srcs, *lands, send_sems, recv_sems, after)
    return list(out[:n]), list(out[n:])


def _adamw(w, m, v, gparts, *, tr, name):
    R, C = w.shape
    n_parts = gparts.shape[0]

    def body(w_ref, m_ref, v_ref, gp_ref, g_ref, d_ref, nm_ref, nv_ref):
        g = gp_ref[0].astype(F32)
        for p in range(1, n_parts):
            g = g + gp_ref[p].astype(F32)
        wv = w_ref[...]
        m_new = ADAM_B1 * m_ref[...] + (1.0 - ADAM_B1) * g
        v_new = ADAM_B2 * v_ref[...] + (1.0 - ADAM_B2) * (g * g)
        m_hat = m_new / (1.0 - ADAM_B1 ** ADAM_STEP)
        v_hat = v_new / (1.0 - ADAM_B2 ** ADAM_STEP)
        g_ref[...] = g
        d_ref[...] = -ADAM_LR * (m_hat / (jnp.sqrt(v_hat) + ADAM_EPS) + ADAM_WD * wv)
        nm_ref[...] = m_new
        nv_ref[...] = v_new

    row = pl.BlockSpec((tr, C), lambda i: (i, 0))
    shp = jax.ShapeDtypeStruct((R, C), F32)
    return pl.pallas_call(
        body, name=name, grid=(R // tr,),
        in_specs=[row, row, row, pl.BlockSpec((n_parts, tr, C), lambda i: (0, i, 0))],
        out_specs=[row, row, row, row], out_shape=[shp, shp, shp, shp],
        compiler_params=_params(("parallel",)),
    )(w, m, v, gparts)


def _rope_tables(s):
    pos = jnp.arange(s, dtype=F32)
    inv_freq = 1.0 / (ROPE_THETA ** (jnp.arange(0, QK_ROPE, 2, dtype=F32) / QK_ROPE))
    ang = pos[:, None] * inv_freq[None, :]
    cos, sin = jnp.cos(ang), jnp.sin(ang)
    zero = jnp.zeros((s, LANES - QK_ROPE), F32)
    return jnp.concatenate([cos, cos, zero], -1), jnp.concatenate([sin, sin, zero], -1)


def _pick(n, want):
    t = min(n, want)
    assert n % t == 0
    return t


def _local_step(x, target, wts, small, hooks):
    S = x.shape[0]
    H = MLA_HEADS
    ts = _pick(S, 512)
    tm = _pick(S, 512)
    tm_wide = _pick(S, 1024)
    tk_s = _pick(S, 2048)
    row = lambda v: v.reshape(1, -1)
    w_in = wts["w_in"]
    w_main = jnp.concatenate([w_in[:, 0:2048], w_in[:, 2624:4672]], axis=1)
    w_ckv = jnp.concatenate([w_in[:, 2048:2624], jnp.zeros((D_MODEL, CKV_W - 576), BF16)], axis=1)
    w_uq3 = wts["w_uq"].reshape(Q_LORA, H, QK_NOPE + QK_ROPE)
    w_uq_p = jnp.concatenate(
        [w_uq3[:, :, :QK_NOPE].reshape(Q_LORA, H * QK_NOPE),
         jnp.pad(w_uq3[:, :, QK_NOPE:], ((0, 0), (0, 0), (0, LANES - QK_ROPE))).reshape(Q_LORA, H * LANES)], axis=1)
    w_ukv = wts["w_ukv"]
    cos, sin = _rope_tables(S)
    conv_w, conv_b = small["conv_w"], row(small["conv_b"])
    wa, wx = small["lru_wa"].astype(BF16), small["lru_wx"].astype(BF16)
    wat, wxt = jnp.swapaxes(wa, 1, 2), jnp.swapaxes(wx, 1, 2)
    ba, bx = small["lru_ba"].reshape(RNN_BLOCKS, 1, RNN_BLOCK_W), small["lru_bx"].reshape(RNN_BLOCKS, 1, RNN_BLOCK_W)
    lam = row(small["lru_lambda"])
    q_norm, kv_norm = row(small["q_norm"]), row(small["kv_norm"])
    norm_mix, norm_mlp, norm_final = row(small["norm_mix"]), row(small["norm_mlp"]), row(small["norm_final"])

    xn = _rmsnorm_cast(x, norm_mix, ts=ts, name="norm_mix")
    ident = lambda acc: (acc,)
    (z_main,) = _mm(xn, w_main, name="z_main", tm=tm_wide, tn=1024, tk=1024, outs=[("tile", F32)], epilogue=ident)
    (z_ckv,) = _mm(xn, w_ckv, name="z_ckv", tm=tm, tn=CKV_W, tk=1024, outs=[("tile", F32)], epilogue=ident)
    tt = _pick(S, 256)
    h = _lru_fwd(z_main, conv_w, conv_b, wa, ba, wx, bx, lam, tt=tt)
    q, k, v = _mla_proj(z_ckv, q_norm, kv_norm, w_uq_p, w_ukv, cos, sin, ts=_pick(S, 256))
    ta = _pick(S, 1024)
    o, lse = _attn_fwd(q, k, v, t=ta, hb=2)
    merged = _merge_fwd(h, z_main, o, ts=_pick(S, 256))
    w_out, w_up, w_down = hooks["weights_later"](merged)

    def ep_h1(acc, xv, g):
        h1 = acc + xv
        n2, _ = _rms_fwd(h1, g)
        return h1, n2

    h1, n2 = _mm(merged, w_out, name="h1", tm=tm, tn=1024, tk=1024, outs=[("tile", F32), ("tile", BF16)],
                 epilogue=ep_h1, extras=[("tile", x), ("row", norm_mlp)])

    def ep_up(acc):
        r = jnp.maximum(acc, 0.0)
        return r * r, r

    act, relu = _mm(n2, w_up, name="up", tm=tm_wide, tn=1024, tk=1024, outs=[("tile", BF16), ("tile", BF16)],
                    epilogue=ep_up)

    def ep_loss(acc, h1v, tgt, g):
        h2 = acc + h1v
        y, _ = _rms_fwd(h2, g)
        err = y - tgt
        loss_rows = 0.5 * jnp.mean(err * err, axis=-1, keepdims=True)
        dy = err * (1.0 / D_MODEL)
        dh2, dg_rows = _rms_bwd(dy, h2, g)
        lsum = jnp.sum(loss_rows, axis=0, keepdims=True)
        return dh2, dh2, jnp.sum(dg_rows, axis=0, keepdims=True), jnp.broadcast_to(lsum, (1, D_MODEL))

    dh2, dh2b, dnf_p, loss_p = _mm(
        act, w_down, name="down_loss", tm=tm, tn=1024, tk=D_FF,
        outs=[("tile", F32), ("tile", BF16), ("rowpart", F32), ("rowpart", F32)], epilogue=ep_loss,
        extras=[("tile", h1), ("tile", target), ("row", norm_final)])
    loss_part = jnp.sum(loss_p[:, 0, 0])
    d_norm_final = jnp.sum(dnf_p, axis=(0, 1))

    def ep_du(acc, r):
        return (acc * (2.0 * r.astype(F32)),)

    (du,) = _mm(dh2b, w_down, name="d_act", tb=True, tm=tm_wide, tn=1024, tk=1024, outs=[("tile", BF16)], epilogue=ep_du,
                extras=[("tile", relu)])

    def ep_dh1(acc, h1v, dh2v, g):
        dv, dg_rows = _rms_bwd(acc, h1v, g)
        dh1 = dh2v + dv
        return dh1, dh1, jnp.sum(dg_rows, axis=0, keepdims=True)

    dh1, dh1b, dnm_p = _mm(du, w_up, name="d_n2", tb=True, tm=tm, tn=1024, tk=D_FF,
                           outs=[("tile", F32), ("tile", BF16), ("rowpart", F32)], epilogue=ep_dh1,
                           extras=[("tile", h1), ("tile", dh2), ("row", norm_mlp)])
    d_norm_mlp = jnp.sum(dnm_p, axis=(0, 1))
    tn_mm = functools.partial(_mm, ta=True, tk=tk_s, outs=[("tile", BF16)], epilogue=ident)
    (d_w_down,) = tn_mm(act, dh2b, name="dw_down", tm=1024, tn=1024)
    (p_w_up,) = _mm(n2, du, name="dw_up", ta=True, tk=tk_s, tm=1024, tn=D_FF // N_DEV, outs=[("colshard", BF16)],
                    epilogue=ident)
    (d_w_out,) = tn_mm(merged, dh1b, name="dw_out", tm=1024, tn=1024)
    early = [d_w_out.reshape(N_DEV, -1, D_MODEL), p_w_up, d_w_down.reshape(N_DEV, -1, D_MODEL)]
    w_out = w_out + hooks["send"]("early", early)[0, 0].astype(BF16)

    tmm = _pick(S, 256)

    def ep_dmerge(dm, hv, rg, ga, gb, ov):
        gl, dgl = _gelu_and_grad(rg)
        sa, sb = _sigmoid(ga), _sigmoid(gb)
        ya = hv * gl
        dya = dm * sa
        do = dm * sb
        dga = dm * ya * sa * (1.0 - sa)
        dgb = dm * ov * sb * (1.0 - sb)
        dh = dya * gl
        drg = dya * hv * dgl
        dov = do * ov
        lane = lax.broadcasted_iota(jnp.int32, (dm.shape[0], LANES), 1)
        delta = jnp.zeros((dm.shape[0], LANES), F32)
        for hh in range(H):
            dsum = jnp.sum(dov[:, hh * V_HEAD:(hh + 1) * V_HEAD], axis=1, keepdims=True)
            delta = jnp.where(lane == hh, dsum, delta)
        return dh, drg, dga, dgb, do, delta

    dh_lru, d_rg, d_ga, d_gb, do, delta_w = _mm(
        dh1b, w_out, name="d_merge", tb=True, tm=tmm, tn=1024, tk=1024,
        outs=[("tile", F32), ("tile", BF16), ("tile", BF16), ("tile", BF16), ("tile", BF16), ("side", F32)],
        epilogue=ep_dmerge,
        extras=[("tile", h), ("tilecol", z_main, 1), ("tilecol", z_main, 2), ("tilecol", z_main, 3), ("tile", o)])
    delta_row = delta_w[:, :H].T.reshape(H, 1, S)
    lse_row = lse.reshape(H, 1, S)

    dq, dk, dv = _attn_bwd(q, k, v, do, lse_row, delta_row, t=ta)
    dz_ckv, d_w_uq_p, d_w_ukv, d_q_norm, d_kv_norm = _mla_proj_bwd(
        z_ckv, dq, dk, dv, q_norm, kv_norm, w_uq_p.T, w_ukv.T, cos, sin, ts=_pick(S, 256))
    d_w_uq = jnp.concatenate(
        [d_w_uq_p[:, :H * QK_NOPE].reshape(Q_LORA, H, QK_NOPE),
         d_w_uq_p[:, H * QK_NOPE:].reshape(Q_LORA, H, LANES)[:, :, :QK_ROPE]], axis=2).reshape(Q_LORA, -1)

    d_rx, d_wa, d_wx, d_ba, d_bx, d_lam, d_conv_w, d_conv_b = _lru_bwd(
        z_main, h, dh_lru, conv_w, conv_b, wa, wat, ba, wx, wxt, bx, lam, tt=tt)

    dz_main = jnp.concatenate([d_rx, d_rg, d_ga, d_gb], axis=1)
    (d_w_main,) = tn_mm(xn, dz_main, name="dw_main", tm=1024, tn=1024)
    (d_w_ckv,) = tn_mm(xn, dz_ckv, name="dw_ckv", tm=1024, tn=CKV_W)
    d_w_in = jnp.concatenate([d_w_main[:, 0:2048], d_w_ckv[:, 0:576], d_w_main[:, 2048:4096]], axis=1)

    def col_parts(full):
        r = full.shape[0]
        return jnp.transpose(full.astype(BF16).reshape(r, N_DEV, -1), (1, 0, 2))

    late = [col_parts(d_w_in), col_parts(d_w_uq), col_parts(d_w_ukv)]
    norm_mix = norm_mix + hooks["send"]("late", late)[0, 0]
    (dxn_ckv,) = _mm(dz_ckv, w_ckv, name="dxn_ckv", tb=True, tm=tm, tn=1024, tk=CKV_W, outs=[("tile", F32)],
                     epilogue=ident)

    def ep_dx(acc, part, xv, dh1v, g):
        dv, dg_rows = _rms_bwd(acc + part, xv, g)
        return dh1v + dv, jnp.sum(dg_rows, axis=0, keepdims=True)

    grad_x, dnx_p = _mm(dz_main, w_main, name="dx", tb=True, tm=tm, tn=1024, tk=4 * D_MODEL,
                        outs=[("tile", F32), ("rowpart", F32)], epilogue=ep_dx,
                        extras=[("tile", dxn_ckv), ("tile", x), ("tile", dh1), ("row", norm_mix)])
    d_norm_mix = jnp.sum(dnx_p, axis=(0, 1))
    sm = {"norm_mix": d_norm_mix, "conv_w": d_conv_w, "conv_b": d_conv_b.reshape(-1), "lru_wa": d_wa,
          "lru_ba": d_ba.reshape(RNN_BLOCKS, RNN_BLOCK_W), "lru_wx": d_wx, "lru_bx": d_bx.reshape(RNN_BLOCKS, RNN_BLOCK_W),
          "lru_lambda": d_lam.reshape(-1), "q_norm": d_q_norm.reshape(-1), "kv_norm": d_kv_norm.reshape(-1),
          "norm_mlp": d_norm_mlp, "norm_final": d_norm_final}
    return loss_part, grad_x, sm


BIG = ("w_in", "w_uq", "w_ukv", "w_out", "w_up", "w_down")
SMALL = ("norm_mix", "conv_b", "lru_wa", "lru_ba", "lru_wx", "lru_bx", "lru_lambda", "q_norm", "kv_norm", "norm_mlp",
         "norm_final")
WEIGHTS = ("norm_mix", "w_in", "conv_w", "conv_b", "lru_wa", "lru_ba", "lru_wx", "lru_bx", "lru_lambda", "q_norm", "w_uq",
           "kv_norm", "w_ukv", "w_out", "norm_mlp", "w_up", "w_down", "norm_final")
ADAM_TILE_ROWS = {"w_in": 256, "w_uq": 128, "w_ukv": 128, "w_out": 64, "w_up": 256, "w_down": 128}
CONV_ROWS = N_DEV * 8


def _rows(a):
    return a.reshape(-1, LANES)


def _pad_rows(a, mult):
    r = a.shape[-2]
    pad = (-r) % mult
    if pad == 0:
        return a
    cfg = [(0, 0)] * (a.ndim - 2) + [(0, pad), (0, 0)]
    return jnp.pad(a, cfg)


def _cols_from_shards(g):
    return jnp.transpose(g, (1, 0, 2)).reshape(g.shape[1], -1)


def kernel(x, norm_mix, w_in, conv_w, conv_b, lru_wa, lru_ba, lru_wx, lru_bx, lru_lambda, q_norm, w_uq, kv_norm, w_ukv, w_out, norm_mlp, w_up, w_down, norm_final, loss_target, m_norm_mix, m_w_in, m_conv_w, m_conv_b, m_lru_wa, m_lru_ba, m_lru_wx, m_lru_bx, m_lru_lambda, m_q_norm, m_w_uq, m_kv_norm, m_w_ukv, m_w_out, m_norm_mlp, m_w_up, m_w_down, m_norm_final, v_norm_mix, v_w_in, v_conv_w, v_conv_b, v_lru_wa, v_lru_ba, v_lru_wx, v_lru_bx, v_lru_lambda, v_q_norm, v_w_uq, v_kv_norm, v_w_ukv, v_w_out, v_norm_mlp, v_w_up, v_w_down, v_norm_final):
    W = dict(norm_mix=norm_mix, w_in=w_in, conv_w=conv_w, conv_b=conv_b, lru_wa=lru_wa, lru_ba=lru_ba, lru_wx=lru_wx,
             lru_bx=lru_bx, lru_lambda=lru_lambda, q_norm=q_norm, w_uq=w_uq, kv_norm=kv_norm, w_ukv=w_ukv, w_out=w_out,
             norm_mlp=norm_mlp, w_up=w_up, w_down=w_down, norm_final=norm_final)
    M = dict(norm_mix=m_norm_mix, w_in=m_w_in, conv_w=m_conv_w, conv_b=m_conv_b, lru_wa=m_lru_wa, lru_ba=m_lru_ba,
             lru_wx=m_lru_wx, lru_bx=m_lru_bx, lru_lambda=m_lru_lambda, q_norm=m_q_norm, w_uq=m_w_uq, kv_norm=m_kv_norm,
             w_ukv=m_w_ukv, w_out=m_w_out, norm_mlp=m_norm_mlp, w_up=m_w_up, w_down=m_w_down, norm_final=m_norm_final)
    V = dict(norm_mix=v_norm_mix, w_in=v_w_in, conv_w=v_conv_w, conv_b=v_conv_b, lru_wa=v_lru_wa, lru_ba=v_lru_ba,
             lru_wx=v_lru_wx, lru_bx=v_lru_bx, lru_lambda=v_lru_lambda, q_norm=v_q_norm, w_uq=v_w_uq, kv_norm=v_kv_norm,
             w_ukv=v_w_ukv, w_out=v_w_out, norm_mlp=v_norm_mlp, w_up=v_w_up, w_down=v_w_down, norm_final=v_norm_final)
    me = 4 * lax.axis_index("x") + 2 * lax.axis_index("y") + lax.axis_index("c")

    first, later = ("w_in", "w_uq", "w_ukv"), ("w_out", "w_up", "w_down")
    got = _all_gather([W[n].astype(BF16) for n in first] + [_pad_rows(conv_w, 8)], name="gather_weights")
    wts = {"w_in": _cols_from_shards(got[0]), "w_uq": _cols_from_shards(got[1]), "w_ukv": _cols_from_shards(got[2])}
    w_send, w_recv, w_src, w_land, zeros = _push_start([W[n].astype(BF16) for n in later], name="gather_later_start",
                                                       slab_per_peer=False)
    small = {n: W[n] for n in SMALL}
    small["conv_w"] = _cols_from_shards(got[3][:, :CONV_WIDTH])
    small["norm_mix"] = norm_mix + zeros[0, 0]

    def with_own_slab(land, mine):
        return lax.dynamic_update_slice(land, mine, (me, 0, 0))

    def weights_later(after):
        srcs, lands = _push_wait(w_send, w_recv, w_src, w_land, after, name="gather_later_wait", slab_per_peer=False)
        w_out_g, w_up_g, w_down_g = [with_own_slab(l, s[None]) for l, s in zip(lands, srcs)]
        return w_out_g.reshape(-1, D_MODEL), _cols_from_shards(w_up_g), w_down_g.reshape(-1, D_MODEL)

    sent = {}

    def send(group, parts):
        sent[group] = _push_start(parts, name="exchange_" + group + "_start", slab_per_peer=True)
        return sent[group][4]

    loss_part, grad_x, g_small = _local_step(x[0], loss_target[0], wts, small,
                                              {"weights_later": weights_later, "send": send})

    G, Dl, NM, NV = {}, {}, {}, {}
    for group, names in (("early", later), ("late", first)):
        s_sems, r_sems, srcs, lands, _ = sent[group]
        srcs, lands = _push_wait(s_sems, r_sems, srcs, lands, grad_x, name="exchange_" + group + "_wait",
                                 slab_per_peer=True)
        for n, src, land in zip(names, srcs, lands):
            parts = with_own_slab(land, lax.dynamic_slice(src, (me, 0, 0), (1, *src.shape[1:])))
            G[n], Dl[n], NM[n], NV[n] = _adamw(W[n], M[n], V[n], parts, tr=ADAM_TILE_ROWS[n], name="adamw_" + n)

    conv_rows = _pad_rows(jnp.transpose(g_small["conv_w"].reshape(CONV_WIDTH, N_DEV, LANES), (1, 0, 2)), 8)
    loss_rows = jnp.zeros((8, LANES), F32).at[0, 0].set(loss_part)
    rows8 = lambda a: _pad_rows(_rows(a), 8)
    small_pack = jnp.concatenate(
        [conv_rows.reshape(CONV_ROWS, LANES)] + [rows8(g_small[n]) for n in SMALL] + [loss_rows], axis=0)
    (small_all,) = _all_gather([small_pack], name="gather_small")

    def pack_small(D):
        return jnp.concatenate(
            [jnp.zeros((CONV_ROWS, LANES), F32)] + [rows8(D[n]) for n in SMALL] + [jnp.zeros((8, LANES), F32)], axis=0)

    packed = _adamw(pack_small(W), pack_small(M), pack_small(V), small_all, tr=small_pack.shape[0], name="adamw_small")
    off = CONV_ROWS
    for n in SMALL:
        r = W[n].size // LANES
        for out, pk in zip((G, Dl, NM, NV), packed):
            out[n] = pk[off:off + r].reshape(W[n].shape)
        off += r + (-r) % 8
    loss = packed[0][off, 0]

    g_conv = lax.dynamic_slice(packed[0], (me * 8, 0), (8, LANES))
    conv_out = _adamw(_pad_rows(conv_w, 8), _pad_rows(m_conv_w, 8), _pad_rows(v_conv_w, 8), g_conv[None], tr=8,
                      name="adamw_conv_w")
    for out, pk in zip((G, Dl, NM, NV), conv_out):
        out["conv_w"] = pk[:CONV_WIDTH]
    return (loss, grad_x[None], *[G[n] for n in WEIGHTS], *[Dl[n] for n in WEIGHTS], *[NM[n] for n in WEIGHTS],
            *[NV[n] for n in WEIGHTS])
```

```python
import functools

import numpy as np
import jax
import jax.numpy as jnp
from jax import lax
from jax.experimental import pallas as pl
from jax.experimental.pallas import tpu as pltpu

F32 = jnp.float32
BF16 = jnp.bfloat16
MESH = pl.DeviceIdType.MESH

D_MODEL = 1024
N_DEV = 8
LANES = 128
RNN_BLOCKS = 8
RNN_BLOCK_W = 128
CONV_WIDTH = 4
LRU_C = 8.0
MLA_HEADS = 8
QK_NOPE = 128
QK_ROPE = 64
V_HEAD = 128
QK_PAD = 256
Q_LORA = 256
KV_LORA = 256
CKV_W = 640
ROPE_THETA = 10000.0
D_FF = 4096
EPS = 1e-6
ATTN_SCALE = (QK_NOPE + QK_ROPE) ** -0.5
LOG2E = 1.4426950408889634
LN2 = 0.6931471805599453
NEG = -1e30

ADAM_LR = 0.001
ADAM_B1 = 0.9
ADAM_B2 = 0.999
ADAM_EPS = 1e-08
ADAM_WD = 0.01
ADAM_STEP = 10

VMEM_LIMIT = 56 * 1024 * 1024


def _params(sem=None):
    return pltpu.CompilerParams(dimension_semantics=sem, vmem_limit_bytes=VMEM_LIMIT)


def _sigmoid(v):
    return 1.0 / (1.0 + jnp.exp(-v))


def _softplus(y):
    e = jnp.exp(-jnp.abs(y))
    u = 1.0 + e
    d = u - 1.0
    l1p = jnp.where(d == 0.0, e, jnp.log(u) * e / jnp.where(d == 0.0, 1.0, d))
    return jnp.maximum(y, 0.0) + l1p


_GELU_K = 0.7978845608028654
_GELU_C = 0.044715


def _gelu_and_grad(v):
    t = jnp.tanh(_GELU_K * (v + _GELU_C * v * v * v))
    g = 0.5 * v * (1.0 + t)
    dg = 0.5 * (1.0 + t) + 0.5 * v * (1.0 - t * t) * _GELU_K * (1.0 + 3.0 * _GELU_C * v * v)
    return g, dg


def _rms_fwd(v, g):
    rstd = lax.rsqrt(jnp.mean(v * v, axis=-1, keepdims=True) + EPS)
    return v * rstd * g, rstd


def _rms_bwd(dy, v, g):
    rstd = lax.rsqrt(jnp.mean(v * v, axis=-1, keepdims=True) + EPS)
    vh = v * rstd
    dvh = dy * g
    dv = rstd * (dvh - vh * jnp.mean(dvh * vh, axis=-1, keepdims=True))
    return dv, dy * vh


def _shift_down(v, s, fill, row):
    return jnp.where(row >= s, pltpu.roll(v, s, 0), fill)


def _shift_up(v, s, fill, row, n):
    return jnp.where(row < n - s, pltpu.roll(v, n - s, 0), fill)


def _rot_half(v, lane):
    n = v.shape[-1]
    l = lane & (LANES - 1)
    up = pltpu.roll(v, n - QK_ROPE // 2, 1)
    dn = pltpu.roll(v, QK_ROPE // 2, 1)
    return jnp.where(l < QK_ROPE // 2, -up, jnp.where(l < QK_ROPE, dn, 0.0))


def _mm(a, b, *, name, tm, tn, tk, outs, epilogue, extras=(), ta=False, tb=False):
    assert not (ta and tb)
    if ta:
        K, M = a.shape
    else:
        M, K = a.shape
    if tb:
        N, K2 = b.shape
    else:
        K2, N = b.shape
    assert K == K2 and M % tm == 0 and N % tn == 0 and K % tk == 0, (name, a.shape, b.shape)
    n_i, n_j, n_k = M // tm, N // tn, K // tk
    n_ex, n_out = len(extras), len(outs)

    def body(*refs):
        a_ref, b_ref = refs[0], refs[1]
        ex_refs = refs[2:2 + n_ex]
        out_refs = refs[2 + n_ex:2 + n_ex + n_out]
        if ta:
            part = lax.dot_general(a_ref[...], b_ref[...], (((0,), (0,)), ((), ())), preferred_element_type=F32)
        elif tb:
            part = lax.dot_general(a_ref[...], b_ref[...], (((1,), (1,)), ((), ())), preferred_element_type=F32)
        else:
            part = jnp.dot(a_ref[...], b_ref[...], preferred_element_type=F32)

        def finish(acc):
            res = epilogue(acc, *[r[...] for r in ex_refs])
            for o_ref, r in zip(out_refs, res):
                o_ref[...] = r.astype(o_ref.dtype).reshape(o_ref.shape)

        if n_k == 1:
            finish(part)
        else:
            acc_ref = refs[-1]
            k = pl.program_id(2)

            @pl.when(k == 0)
            def _():
                acc_ref[...] = part

            @pl.when(k > 0)
            def _():
                acc_ref[...] += part

            @pl.when(k == n_k - 1)
            def _():
                finish(acc_ref[...])

    a_spec = pl.BlockSpec((tk, tm), lambda j, i, k: (k, i)) if ta else pl.BlockSpec((tm, tk), lambda j, i, k: (i, k))
    b_once = dict(pipeline_mode=pl.Buffered(1)) if (n_j == 1 and n_k == 1) else {}
    if tb:
        in_specs = [a_spec, pl.BlockSpec((tn, tk), lambda j, i, k: (j, k), **b_once)]
    else:
        in_specs = [a_spec, pl.BlockSpec((tk, tn), lambda j, i, k: (k, j), **b_once)]
    for ex in extras:
        kind = ex[0]
        if kind == "tile":
            in_specs.append(pl.BlockSpec((tm, tn), lambda j, i, k: (i, j)))
        elif kind == "tilecol":
            assert n_j == 1
            in_specs.append(pl.BlockSpec((tm, tn), functools.partial(lambda c, j, i, k: (i, c), ex[2])))
        else:
            in_specs.append(pl.BlockSpec((1, tn), lambda j, i, k: (0, j)))
    out_specs, out_shape = [], []
    for kind, dt in outs:
        if kind == "tile":
            out_specs.append(pl.BlockSpec((tm, tn), lambda j, i, k: (i, j)))
            out_shape.append(jax.ShapeDtypeStruct((M, N), dt))
        elif kind == "colshard":
            out_specs.append(pl.BlockSpec((1, tm, tn), lambda j, i, k: (j, i, 0)))
            out_shape.append(jax.ShapeDtypeStruct((n_j, M, tn), dt))
        elif kind == "side":
            assert n_j == 1
            out_specs.append(pl.BlockSpec((tm, LANES), lambda j, i, k: (i, 0)))
            out_shape.append(jax.ShapeDtypeStruct((M, LANES), dt))
        else:
            out_specs.append(pl.BlockSpec((1, 1, tn), lambda j, i, k: (i, 0, j)))
            out_shape.append(jax.ShapeDtypeStruct((n_i, 1, N), dt))
    scratch = [pltpu.VMEM((tm, tn), F32)] if n_k > 1 else []
    return pl.pallas_call(
        body, name=name, grid=(n_j, n_i, n_k), in_specs=in_specs, out_specs=out_specs, out_shape=out_shape,
        scratch_shapes=scratch, compiler_params=_params(("parallel", "parallel", "arbitrary")),
    )(a, b, *[ex[1] for ex in extras])


def _rmsnorm_cast(x, g, *, ts, name):
    S, D = x.shape

    def body(x_ref, g_ref, o_ref):
        y, _ = _rms_fwd(x_ref[...], g_ref[...])
        o_ref[...] = y.astype(BF16)

    return pl.pallas_call(
        body, name=name, grid=(S // ts,),
        in_specs=[pl.BlockSpec((ts, D), lambda i: (i, 0)), pl.BlockSpec((1, D), lambda i: (0, 0))],
        out_specs=pl.BlockSpec((ts, D), lambda i: (i, 0)), out_shape=jax.ShapeDtypeStruct((S, D), BF16),
        compiler_params=_params(("parallel",)),
    )(x, g)


LRU_NB = 4


def _lru_gates(xa, wa_ref, ba_ref, wx_ref, bx_ref, lam):
    xab = xa.astype(BF16)
    W = RNN_BLOCK_W
    rs, is_ = [], []
    for j in range(LRU_NB):
        xj = xab[:, j * W:(j + 1) * W]
        rs.append(_sigmoid(jnp.dot(xj, wa_ref[j], preferred_element_type=F32) + ba_ref[j]))
        is_.append(_sigmoid(jnp.dot(xj, wx_ref[j], preferred_element_type=F32) + bx_ref[j]))
    r = jnp.concatenate(rs, axis=1)
    i = jnp.concatenate(is_, axis=1)
    sp = _softplus(-lam)
    log_a = (-LRU_C * r) * sp
    a = jnp.exp(log_a)
    y = 2.0 * log_a
    one_m = jnp.where(y > -0.01, -y * (1.0 + 0.5 * y * (1.0 + y * (1.0 / 3.0))), 1.0 - a * a)
    return r, i, sp, a, jnp.sqrt(one_m)


def _rows_before(x, tail8, k):
    e16 = jnp.concatenate([tail8, x[0:8, :]], axis=0)
    return jnp.concatenate([pltpu.roll(e16, k, 0)[8:16, :], pltpu.roll(x, k, 0)[8:, :]], axis=0)


def _rows_after(x, head8, k):
    tt = x.shape[0]
    e16 = jnp.concatenate([x[tt - 8:tt, :], head8], axis=0)
    return jnp.concatenate([pltpu.roll(x, tt - k, 0)[:tt - 8, :], pltpu.roll(e16, 16 - k, 0)[0:8, :]], axis=0)


def _scan_down(a, b, h0, a_s, b_s, c_s):
    tt, C = a.shape
    G, nch = tt // 8, C // LANES
    rin = lax.broadcasted_iota(jnp.int32, (tt, C), 0) & 7

    def in_group(v, s):
        return pltpu.roll(v.reshape(G, 8, C), s, 1).reshape(tt, C)

    A, B = a, b
    for s in (1, 2, 4):
        B = A * jnp.where(rin >= s, in_group(B, s), 0.0) + B
        A = A * jnp.where(rin >= s, in_group(A, s), 1.0)
    for j in range(nch):
        a_s[j] = A[:, j * LANES:(j + 1) * LANES]
        b_s[j] = B[:, j * LANES:(j + 1) * LANES]
    At = jnp.concatenate([a_s.at[j][pl.ds(7, G, stride=8), :] for j in range(nch)], axis=1)
    Bt = jnp.concatenate([b_s.at[j][pl.ds(7, G, stride=8), :] for j in range(nch)], axis=1)
    rowg = lax.broadcasted_iota(jnp.int32, (G, C), 0)
    s = 1
    while s < G:
        Bt = At * _shift_down(Bt, s, 0.0, rowg) + Bt
        At = At * _shift_down(At, s, 1.0, rowg)
        s *= 2
    hg = At * h0 + Bt
    cin = _shift_down(hg, 1, h0, rowg)
    for j in range(nch):
        for r in range(8):
            c_s.at[j][pl.ds(r, G, stride=8), :] = cin[:, j * LANES:(j + 1) * LANES]
    return A * jnp.concatenate([c_s[j] for j in range(nch)], axis=1) + B, hg[G - 1:G, :]


def _scan_up(c, g_in, g_next, a_s, b_s, c_s):
    tt, C = c.shape
    G, nch = tt // 8, C // LANES
    rin = lax.broadcasted_iota(jnp.int32, (tt, C), 0) & 7

    def in_group(v, s):
        return pltpu.roll(v.reshape(G, 8, C), 8 - s, 1).reshape(tt, C)

    Cc, Gv = c, g_in
    for s in (1, 2, 4):
        Gv = Gv + Cc * jnp.where(rin < 8 - s, in_group(Gv, s), 0.0)
        Cc = Cc * jnp.where(rin < 8 - s, in_group(Cc, s), 1.0)
    for j in range(nch):
        a_s[j] = Cc[:, j * LANES:(j + 1) * LANES]
        b_s[j] = Gv[:, j * LANES:(j + 1) * LANES]
    Ct = jnp.concatenate([a_s.at[j][pl.ds(0, G, stride=8), :] for j in range(nch)], axis=1)
    Gt = jnp.concatenate([b_s.at[j][pl.ds(0, G, stride=8), :] for j in range(nch)], axis=1)
    rowg = lax.broadcasted_iota(jnp.int32, (G, C), 0)
    s = 1
    while s < G:
        Gt = Gt + Ct * _shift_up(Gt, s, 0.0, rowg, G)
        Ct = Ct * _shift_up(Ct, s, 1.0, rowg, G)
        s *= 2
    gg = Gt + Ct * g_next
    cin = _shift_up(gg, 1, g_next, rowg, G)
    for j in range(nch):
        for r in range(8):
            c_s.at[j][pl.ds(r, G, stride=8), :] = cin[:, j * LANES:(j + 1) * LANES]
    return Gv + Cc * jnp.concatenate([c_s[j] for j in range(nch)], axis=1), gg[0:1, :]


def _lru_fwd(z_main, conv_w, conv_b, wa, ba, wx, bx, lam, *, tt):
    S = z_main.shape[0]
    n_t = S // tt
    BW = RNN_BLOCK_W
    W = LRU_NB * BW

    def body(x_ref, cw_ref, cb_ref, wa_ref, ba_ref, wx_ref, bx_ref, lam_ref, h_ref, tail, hc, a_s, b_s, c_s):
        t = pl.program_id(1)

        @pl.when(t == 0)
        def _():
            tail[...] = jnp.zeros((8, W), F32)
            hc[...] = jnp.zeros((8, W), F32)

        x = x_ref[...]
        before = tail[...]
        cw = cw_ref[...]
        xa = (cb_ref[...] + cw[3:4] * x + cw[2:3] * _rows_before(x, before, 1) + cw[1:2] * _rows_before(x, before, 2)
              + cw[0:1] * _rows_before(x, before, 3))
        tail[...] = x[tt - 8:tt, :]
        _r, i, _sp, a, mult = _lru_gates(xa, wa_ref, ba_ref, wx_ref, bx_ref, lam_ref[...])
        h, h_last = _scan_down(a, mult * (i * xa), hc[0:1, :], a_s, b_s, c_s)
        h_ref[...] = h
        hc[...] = jnp.broadcast_to(h_last, (8, W))

    blk = lambda n, t: (t, n)
    vec = pl.BlockSpec((1, W), lambda n, t: (0, n))
    mat = pl.BlockSpec((LRU_NB, BW, BW), lambda n, t: (n, 0, 0))
    bias = pl.BlockSpec((LRU_NB, 1, BW), lambda n, t: (n, 0, 0))
    row8 = pltpu.VMEM((8, W), F32)
    wide = pltpu.VMEM((LRU_NB, tt, LANES), F32)
    return pl.pallas_call(
        body, name="lru_fwd", grid=(RNN_BLOCKS // LRU_NB, n_t),
        in_specs=[pl.BlockSpec((tt, W), blk), pl.BlockSpec((CONV_WIDTH, W), lambda n, t: (0, n)), vec, mat, bias, mat,
                  bias, vec],
        out_specs=pl.BlockSpec((tt, W), blk), out_shape=jax.ShapeDtypeStruct((S, D_MODEL), F32),
        scratch_shapes=[row8, row8, wide, wide, wide],
        compiler_params=_params(("parallel", "arbitrary")),
    )(z_main, conv_w, conv_b, wa, ba, wx, bx, lam)


def _lru_bwd(z_main, h, dh, conv_w, conv_b, wa, wat, ba, wx, wxt, bx, lam, *, tt):
    S = z_main.shape[0]
    n_t = S // tt
    BW = RNN_BLOCK_W
    W = LRU_NB * BW
    t8 = tt // 8

    def body(x_ref, xp_ref, h_ref, hp_ref, dh_ref, cw_ref, cb_ref, wa_ref, wat_ref, ba_ref, wx_ref, wxt_ref, bx_ref,
             lam_ref, dx_ref, dwa_ref, dwx_ref, dba_ref, dbx_ref, dlam_ref, dcw_ref, dcb_ref, nxt, a_c, g_c, a_s, b_s,
             c_s):
        t = pl.program_id(1)
        tile = n_t - 1 - t

        @pl.when(t == 0)
        def _():
            a_c[...] = jnp.zeros((8, W), F32)
            g_c[...] = jnp.zeros((8, W), F32)
            nxt[...] = jnp.zeros((8, W), F32)
            dwa_ref[...] = jnp.zeros_like(dwa_ref)
            dwx_ref[...] = jnp.zeros_like(dwx_ref)
            dba_ref[...] = jnp.zeros_like(dba_ref)
            dbx_ref[...] = jnp.zeros_like(dbx_ref)
            dlam_ref[...] = jnp.zeros_like(dlam_ref)
            dcw_ref[...] = jnp.zeros_like(dcw_ref)
            dcb_ref[...] = jnp.zeros_like(dcb_ref)

        has_prev = (tile > 0).astype(F32)
        x = x_ref[...]
        before = xp_ref[...] * has_prev
        xm1, xm2, xm3 = _rows_before(x, before, 1), _rows_before(x, before, 2), _rows_before(x, before, 3)
        cw = cw_ref[...]
        xa = cb_ref[...] + cw[3:4] * x + cw[2:3] * xm1 + cw[1:2] * xm2 + cw[0:1] * xm3
        lam = lam_ref[...]
        r, i, sp, a, mult = _lru_gates(xa, wa_ref, ba_ref, wx_ref, bx_ref, lam)
        gated = i * xa
        h_prev = _rows_before(h_ref[...], hp_ref[...] * has_prev, 1)
        g, g_first = _scan_up(_rows_after(a, a_c[...], 1), dh_ref[...], g_c[0:1, :], a_s, b_s, c_s)
        a_c[...] = jnp.broadcast_to(a[0:1, :], (8, W))
        g_c[...] = jnp.broadcast_to(g_first, (8, W))
        dlog_a = g * h_prev * a - g * gated * (a * a) / mult
        dgated = g * mult
        di = dgated * xa
        dxa = dgated * i
        dr = dlog_a * (-LRU_C * sp)
        dlam_ref[...] += jnp.sum(dlog_a * (-LRU_C * r), axis=0, keepdims=True) * (-_sigmoid(-lam))
        dpr = dr * r * (1.0 - r)
        dpi = di * i * (1.0 - i)
        xab, dprb, dpib = xa.astype(BF16), dpr.astype(BF16), dpi.astype(BF16)
        tn_dims = (((0,), (0,)), ((), ()))
        back = []
        for j in range(LRU_NB):
            sl = slice(j * BW, (j + 1) * BW)
            dwa_ref[j] += lax.dot_general(xab[:, sl], dprb[:, sl], tn_dims, preferred_element_type=F32)
            dwx_ref[j] += lax.dot_general(xab[:, sl], dpib[:, sl], tn_dims, preferred_element_type=F32)
            dba_ref[j] += jnp.sum(dpr[:, sl], axis=0, keepdims=True)
            dbx_ref[j] += jnp.sum(dpi[:, sl], axis=0, keepdims=True)
            back.append(jnp.dot(dprb[:, sl], wat_ref[j], preferred_element_type=F32)
                        + jnp.dot(dpib[:, sl], wxt_ref[j], preferred_element_type=F32))
        dxa = dxa + jnp.concatenate(back, axis=1)
        after = nxt[...]
        dx = (cw[3:4] * dxa + cw[2:3] * _rows_after(dxa, after, 1) + cw[1:2] * _rows_after(dxa, after, 2)
              + cw[0:1] * _rows_after(dxa, after, 3))
        nxt[...] = dxa[0:8, :]
        dx_ref[...] = dx.astype(BF16)
        dcw_ref[3:4, :] += jnp.sum(dxa * x, axis=0, keepdims=True)
        dcw_ref[2:3, :] += jnp.sum(dxa * xm1, axis=0, keepdims=True)
        dcw_ref[1:2, :] += jnp.sum(dxa * xm2, axis=0, keepdims=True)
        dcw_ref[0:1, :] += jnp.sum(dxa * xm3, axis=0, keepdims=True)
        dcb_ref[...] += jnp.sum(dxa, axis=0, keepdims=True)

    blk = lambda n, t: (n_t - 1 - t, n)
    prev = lambda n, t: (jnp.maximum((n_t - 1 - t) * t8 - 1, 0), n)
    vec = pl.BlockSpec((1, W), lambda n, t: (0, n))
    mat = pl.BlockSpec((LRU_NB, BW, BW), lambda n, t: (n, 0, 0))
    bias = pl.BlockSpec((LRU_NB, 1, BW), lambda n, t: (n, 0, 0))
    cws = pl.BlockSpec((CONV_WIDTH, W), lambda n, t: (0, n))
    tile = pl.BlockSpec((tt, W), blk)
    prev8 = pl.BlockSpec((8, W), prev)
    row8 = pltpu.VMEM((8, W), F32)
    wide = pltpu.VMEM((LRU_NB, tt, LANES), F32)
    return pl.pallas_call(
        body, name="lru_bwd", grid=(RNN_BLOCKS // LRU_NB, n_t),
        in_specs=[tile, prev8, tile, prev8, tile, cws, vec, mat, mat, bias, mat, mat, bias, vec],
        out_specs=[tile, mat, mat, bias, bias, vec, cws, vec],
        out_shape=[jax.ShapeDtypeStruct((S, D_MODEL), BF16),
                   jax.ShapeDtypeStruct((RNN_BLOCKS, BW, BW), F32), jax.ShapeDtypeStruct((RNN_BLOCKS, BW, BW), F32),
                   jax.ShapeDtypeStruct((RNN_BLOCKS, 1, BW), F32), jax.ShapeDtypeStruct((RNN_BLOCKS, 1, BW), F32),
                   jax.ShapeDtypeStruct((1, D_MODEL), F32),
                   jax.ShapeDtypeStruct((CONV_WIDTH, D_MODEL), F32), jax.ShapeDtypeStruct((1, D_MODEL), F32)],
        scratch_shapes=[row8, row8, row8, wide, wide, wide],
        compiler_params=_params(("parallel", "arbitrary")),
    )(z_main, z_main, h, h, dh, conv_w, conv_b, wa, wat, ba, wx, wxt, bx, lam)


def _mla_proj(z_ckv, q_norm, kv_norm, w_uq, w_ukv, cos, sin, *, ts):
    S = z_ckv.shape[0]
    H = MLA_HEADS

    def body(c_ref, qn_ref, kn_ref, wq_ref, wkv_ref, cos_ref, sin_ref, q_ref, k_ref, v_ref):
        c = c_ref[...]
        cqn, _ = _rms_fwd(c[:, 0:Q_LORA], qn_ref[...])
        ckn, _ = _rms_fwd(c[:, Q_LORA:Q_LORA + KV_LORA], kn_ref[...])
        q = jnp.dot(cqn.astype(BF16), wq_ref[...], preferred_element_type=F32) * (ATTN_SCALE * LOG2E)
        kv = jnp.dot(ckn.astype(BF16), wkv_ref[...], preferred_element_type=F32)
        cos1, sin1 = cos_ref[...], sin_ref[...]
        cos8 = jnp.concatenate([cos1] * H, axis=1)
        sin8 = jnp.concatenate([sin1] * H, axis=1)
        qr = q[:, H * QK_NOPE:]
        lane8 = lax.broadcasted_iota(jnp.int32, qr.shape, 1)
        qr = qr * cos8 + _rot_half(qr, lane8) * sin8
        kr = c[:, Q_LORA + KV_LORA:]
        lane1 = lax.broadcasted_iota(jnp.int32, kr.shape, 1)
        kr = (kr * cos1 + _rot_half(kr, lane1) * sin1).astype(BF16)
        for h in range(H):
            q_ref[h, :, 0:QK_NOPE] = q[:, h * QK_NOPE:(h + 1) * QK_NOPE].astype(BF16)
            q_ref[h, :, QK_NOPE:] = qr[:, h * LANES:(h + 1) * LANES].astype(BF16)
            k_ref[h, :, 0:QK_NOPE] = kv[:, h * 2 * LANES:h * 2 * LANES + LANES].astype(BF16)
            k_ref[h, :, QK_NOPE:] = kr
            v_ref[h] = kv[:, h * 2 * LANES + LANES:(h + 1) * 2 * LANES].astype(BF16)

    full = lambda shape: pl.BlockSpec(shape, lambda i: (0,) * len(shape))
    return pl.pallas_call(
        body, name="mla_proj", grid=(S // ts,),
        in_specs=[pl.BlockSpec((ts, CKV_W), lambda i: (i, 0)), full((1, Q_LORA)), full((1, KV_LORA)),
                  full(w_uq.shape), full(w_ukv.shape), pl.BlockSpec((ts, LANES), lambda i: (i, 0)),
                  pl.BlockSpec((ts, LANES), lambda i: (i, 0))],
        out_specs=[pl.BlockSpec((H, ts, QK_PAD), lambda i: (0, i, 0)), pl.BlockSpec((H, ts, QK_PAD), lambda i: (0, i, 0)),
                   pl.BlockSpec((H, ts, V_HEAD), lambda i: (0, i, 0))],
        out_shape=[jax.ShapeDtypeStruct((H, S, QK_PAD), BF16), jax.ShapeDtypeStruct((H, S, QK_PAD), BF16),
                   jax.ShapeDtypeStruct((H, S, V_HEAD), BF16)],
        compiler_params=_params(("parallel",)),
    )(z_ckv, q_norm, kv_norm, w_uq, w_ukv, cos, sin)


def _mla_proj_bwd(z_ckv, dq, dk, dv, q_norm, kv_norm, w_uqt, w_ukvt, cos, sin, *, ts):
    S = z_ckv.shape[0]
    H = MLA_HEADS

    def body(c_ref, dq_ref, dk_ref, dv_ref, qn_ref, kn_ref, wqt_ref, wkvt_ref, cos_ref, sin_ref,
             dz_ref, dwq_ref, dwkv_ref, dqn_ref, dkn_ref):
        @pl.when(pl.program_id(0) == 0)
        def _():
            dwq_ref[...] = jnp.zeros_like(dwq_ref)
            dwkv_ref[...] = jnp.zeros_like(dwkv_ref)
            dqn_ref[...] = jnp.zeros_like(dqn_ref)
            dkn_ref[...] = jnp.zeros_like(dkn_ref)

        c = c_ref[...]
        cq, ck = c[:, 0:Q_LORA], c[:, Q_LORA:Q_LORA + KV_LORA]
        qn, kn = qn_ref[...], kn_ref[...]
        cqn, _ = _rms_fwd(cq, qn)
        ckn, _ = _rms_fwd(ck, kn)
        cos1, sin1 = cos_ref[...], sin_ref[...]
        lane1 = lax.broadcasted_iota(jnp.int32, cos1.shape, 1)

        def unrope(g):
            return g * cos1 - _rot_half(g * sin1, lane1)

        dq_all = jnp.concatenate([dq_ref[h, :, 0:QK_NOPE] for h in range(H)]
                                 + [unrope(dq_ref[h, :, QK_NOPE:]) for h in range(H)], axis=1)
        dq_all = (dq_all * ATTN_SCALE).astype(BF16)
        dkv_all = jnp.concatenate([p for h in range(H) for p in (dk_ref[h, :, 0:QK_NOPE], dv_ref[h])],
                                  axis=1).astype(BF16)
        dkr = dk_ref[0, :, QK_NOPE:]
        for h in range(1, H):
            dkr = dkr + dk_ref[h, :, QK_NOPE:]
        dkr = unrope(dkr)
        tn_dims = (((0,), (0,)), ((), ()))
        dwq_ref[...] += lax.dot_general(cqn.astype(BF16), dq_all, tn_dims, preferred_element_type=F32)
        dwkv_ref[...] += lax.dot_general(ckn.astype(BF16), dkv_all, tn_dims, preferred_element_type=F32)
        dcqn = jnp.dot(dq_all, wqt_ref[...], preferred_element_type=F32)
        dckn = jnp.dot(dkv_all, wkvt_ref[...], preferred_element_type=F32)
        dcq, dqn_rows = _rms_bwd(dcqn, cq, qn)
        dck, dkn_rows = _rms_bwd(dckn, ck, kn)
        dqn_ref[...] += jnp.sum(dqn_rows, axis=0, keepdims=True)
        dkn_ref[...] += jnp.sum(dkn_rows, axis=0, keepdims=True)
        dz_ref[:, 0:Q_LORA] = dcq.astype(BF16)
        dz_ref[:, Q_LORA:Q_LORA + KV_LORA] = dck.astype(BF16)
        dz_ref[:, Q_LORA + KV_LORA:] = dkr.astype(BF16)

    full = lambda shape: pl.BlockSpec(shape, lambda i: (0,) * len(shape))
    return pl.pallas_call(
        body, name="mla_proj_bwd", grid=(S // ts,),
        in_specs=[pl.BlockSpec((ts, CKV_W), lambda i: (i, 0)), pl.BlockSpec((H, ts, QK_PAD), lambda i: (0, i, 0)),
                  pl.BlockSpec((H, ts, QK_PAD), lambda i: (0, i, 0)), pl.BlockSpec((H, ts, V_HEAD), lambda i: (0, i, 0)),
                  full((1, Q_LORA)), full((1, KV_LORA)), full(w_uqt.shape), full(w_ukvt.shape),
                  pl.BlockSpec((ts, LANES), lambda i: (i, 0)), pl.BlockSpec((ts, LANES), lambda i: (i, 0))],
        out_specs=[pl.BlockSpec((ts, CKV_W), lambda i: (i, 0)), full((Q_LORA, w_uqt.shape[0])),
                   full((KV_LORA, w_ukvt.shape[0])), full((1, Q_LORA)), full((1, KV_LORA))],
        out_shape=[jax.ShapeDtypeStruct((S, CKV_W), BF16), jax.ShapeDtypeStruct((Q_LORA, w_uqt.shape[0]), F32),
                   jax.ShapeDtypeStruct((KV_LORA, w_ukvt.shape[0]), F32), jax.ShapeDtypeStruct((1, Q_LORA), F32),
                   jax.ShapeDtypeStruct((1, KV_LORA), F32)],
        compiler_params=_params(("arbitrary",)),
    )(z_ckv, dq, dk, dv, q_norm, kv_norm, w_uqt, w_ukvt, cos, sin)


NT_DIMS = (((1,), (1,)), ((), ()))
TN_DIMS = (((0,), (0,)), ((), ()))


def _attn_fwd(q, k, v, *, t, hb):
    H, S, _ = q.shape
    n = S // t
    nc = t // LANES
    pairs = [(i, j) for i in range(n) for j in range(i + 1)]
    qi = jnp.asarray(np.array([p[0] for p in pairs], np.int32))
    ki = jnp.asarray(np.array([p[1] for p in pairs], np.int32))

    def body(qi_ref, ki_ref, q_ref, k_ref, v_ref, o_ref, lse_ref, m_s, l_s, acc_s):
        p = pl.program_id(1)
        i, j = qi_ref[p], ki_ref[p]

        @pl.when(j == 0)
        def _():
            m_s[...] = jnp.full(m_s.shape, NEG, F32)
            l_s[...] = jnp.zeros(l_s.shape, F32)
            acc_s[...] = jnp.zeros(acc_s.shape, F32)

        def step(masked):
            for hh in range(hb):
                s = lax.dot_general(q_ref[hh], k_ref[hh], NT_DIMS, preferred_element_type=F32)
                if masked:
                    row = lax.broadcasted_iota(jnp.int32, (t, t), 0)
                    col = lax.broadcasted_iota(jnp.int32, (t, t), 1)
                    s = jnp.where(row >= col, s, NEG)
                mc = s[:, 0:LANES]
                for c in range(1, nc):
                    mc = jnp.maximum(mc, s[:, c * LANES:(c + 1) * LANES])
                m_prev = m_s[hh]
                m_new = jnp.maximum(m_prev, jnp.max(mc, axis=1, keepdims=True))
                alpha = jnp.exp2(m_prev - m_new)
                pr = jnp.exp2(s - jnp.concatenate([m_new] * nc, axis=1))
                ls = pr[:, 0:LANES]
                for c in range(1, nc):
                    ls = ls + pr[:, c * LANES:(c + 1) * LANES]
                l_s[hh] = alpha * l_s[hh] + ls
                acc_s[hh] = alpha * acc_s[hh] + jnp.dot(pr.astype(BF16), v_ref[hh], preferred_element_type=F32)
                m_s[hh] = m_new

        @pl.when(j < i)
        def _():
            step(False)

        @pl.when(j == i)
        def _():
            step(True)
            for hh in range(hb):
                l = jnp.sum(l_s[hh], axis=1, keepdims=True)
                o_ref[:, hh * V_HEAD:(hh + 1) * V_HEAD] = acc_s[hh] / l
                lse_ref[hh] = (m_s[hh] + jnp.log2(l)).T[0:1, :]

    grid_spec = pltpu.PrefetchScalarGridSpec(
        num_scalar_prefetch=2, grid=(H // hb, len(pairs)),
        in_specs=[pl.BlockSpec((hb, t, QK_PAD), lambda h, p, qi, ki: (h, qi[p], 0)),
                  pl.BlockSpec((hb, t, QK_PAD), lambda h, p, qi, ki: (h, ki[p], 0)),
                  pl.BlockSpec((hb, t, V_HEAD), lambda h, p, qi, ki: (h, ki[p], 0))],
        out_specs=[pl.BlockSpec((t, hb * V_HEAD), lambda h, p, qi, ki: (qi[p], h)),
                   pl.BlockSpec((hb, 1, t), lambda h, p, qi, ki: (h, 0, qi[p]))],
        scratch_shapes=[pltpu.VMEM((hb, t, LANES), F32), pltpu.VMEM((hb, t, LANES), F32),
                        pltpu.VMEM((hb, t, V_HEAD), F32)],
    )
    return pl.pallas_call(
        body, name="attn_fwd", grid_spec=grid_spec,
        out_shape=[jax.ShapeDtypeStruct((S, H * V_HEAD), F32), jax.ShapeDtypeStruct((H, 1, S), F32)],
        compiler_params=_params(("parallel", "arbitrary")),
    )(qi, ki, q, k, v)


def _attn_bwd(q, k, v, do, lse_row, delta_row, *, t):
    H, S, _ = q.shape
    n = S // t
    pairs = [(i, j) for j in range(n) for i in range(j, n)]
    qi = jnp.asarray(np.array([p[0] for p in pairs], np.int32))
    ki = jnp.asarray(np.array([p[1] for p in pairs], np.int32))

    def body(qi_ref, ki_ref, q_ref, k_ref, v_ref, do_ref, lse_ref, dl_ref, dq_ref, dk_ref, dv_ref, dk_s, dv_s):
        p = pl.program_id(1)
        i, j = qi_ref[p], ki_ref[p]

        @pl.when(p == 0)
        def _():
            dq_ref[...] = jnp.zeros_like(dq_ref)

        def step(masked):
            qb, kb, vb, dob = q_ref[0], k_ref[0], v_ref[0], do_ref[...]
            st = lax.dot_general(kb, qb, NT_DIMS, preferred_element_type=F32)
            if masked:
                krow = lax.broadcasted_iota(jnp.int32, (t, t), 0)
                qcol = lax.broadcasted_iota(jnp.int32, (t, t), 1)
                st = jnp.where(krow <= qcol, st, NEG)
            pt = jnp.exp2(st - lse_ref[0])
            dvp = jnp.dot(pt.astype(BF16), dob, preferred_element_type=F32)
            dpt = lax.dot_general(vb, dob, NT_DIMS, preferred_element_type=F32)
            dst = (pt * (dpt - dl_ref[0])).astype(BF16)
            dkp = jnp.dot(dst, qb, preferred_element_type=F32)
            rows = pl.ds(pl.multiple_of(i * t, t), t)
            dq_ref[0, rows, :] += lax.dot_general(dst, kb, TN_DIMS, preferred_element_type=F32)
            return dkp, dvp

        @pl.when(i == j)
        def _():
            dkp, dvp = step(True)
            dk_s[...] = dkp
            dv_s[...] = dvp

        @pl.when(i != j)
        def _():
            dkp, dvp = step(False)
            dk_s[...] += dkp
            dv_s[...] += dvp

        @pl.when(i == n - 1)
        def _():
            dk_ref[0] = dk_s[...] * LN2
            dv_ref[0] = dv_s[...]

    grid_spec = pltpu.PrefetchScalarGridSpec(
        num_scalar_prefetch=2, grid=(H, len(pairs)),
        in_specs=[pl.BlockSpec((1, t, QK_PAD), lambda h, p, qi, ki: (h, qi[p], 0)),
                  pl.BlockSpec((1, t, QK_PAD), lambda h, p, qi, ki: (h, ki[p], 0)),
                  pl.BlockSpec((1, t, V_HEAD), lambda h, p, qi, ki: (h, ki[p], 0)),
                  pl.BlockSpec((t, V_HEAD), lambda h, p, qi, ki: (qi[p], h)),
                  pl.BlockSpec((1, 1, t), lambda h, p, qi, ki: (h, 0, qi[p])),
                  pl.BlockSpec((1, 1, t), lambda h, p, qi, ki: (h, 0, qi[p]))],
        out_specs=[pl.BlockSpec((1, S, QK_PAD), lambda h, p, qi, ki: (h, 0, 0)),
                   pl.BlockSpec((1, t, QK_PAD), lambda h, p, qi, ki: (h, ki[p], 0)),
                   pl.BlockSpec((1, t, V_HEAD), lambda h, p, qi, ki: (h, ki[p], 0))],
        scratch_shapes=[pltpu.VMEM((t, QK_PAD), F32), pltpu.VMEM((t, V_HEAD), F32)],
    )
    return pl.pallas_call(
        body, name="attn_bwd", grid_spec=grid_spec,
        out_shape=[jax.ShapeDtypeStruct((H, S, QK_PAD), F32), jax.ShapeDtypeStruct((H, S, QK_PAD), F32),
                   jax.ShapeDtypeStruct((H, S, V_HEAD), F32)],
        compiler_params=_params(("parallel", "arbitrary")),
    )(qi, ki, q, k, v, do, lse_row, delta_row)


def _merge_fwd(h, z_main, o, *, ts):
    S = h.shape[0]
    D = D_MODEL

    def body(h_ref, rg_ref, ga_ref, gb_ref, o_ref, m_ref):
        gl, _ = _gelu_and_grad(rg_ref[...])
        m = _sigmoid(ga_ref[...]) * (h_ref[...] * gl) + _sigmoid(gb_ref[...]) * o_ref[...]
        m_ref[...] = m.astype(BF16)

    col = lambda c: pl.BlockSpec((ts, D), lambda i: (i, c))
    return pl.pallas_call(
        body, name="merge_fwd", grid=(S // ts,),
        in_specs=[col(0), col(1), col(2), col(3), col(0)],
        out_specs=col(0), out_shape=jax.ShapeDtypeStruct((S, D), BF16),
        compiler_params=_params(("parallel",)),
    )(h, z_main, z_main, z_main, o)


def _my_place():
    return lax.axis_index("x"), lax.axis_index("y"), lax.axis_index("c")


def _all_gather(shards, *, name):
    n = len(shards)

    def body(*refs):
        x_refs, out_refs = refs[:n], refs[n:2 * n]
        send_sems, recv_sems, local_sems = refs[2 * n:]
        x, y, c = _my_place()
        me, sibling = (x, y, c), (x, y, 1 - c)
        chips = [(1 - x, y), (x, 1 - y), (1 - x, 1 - y)]

        def slot(a, px, py, pc):
            return out_refs[a].at[4 * px + 2 * py + pc]

        def copy(a, k, block, to, src=None):
            return pltpu.make_async_remote_copy(
                src_ref=slot(a, *block) if src is None else src, dst_ref=slot(a, *block),
                send_sem=send_sems.at[7 * a + k], recv_sem=recv_sems.at[7 * a + k], device_id=to, device_id_type=MESH)

        mine = [pltpu.make_async_copy(x_refs[a], slot(a, *me), local_sems.at[a]) for a in range(n)]
        for cp in mine:
            cp.start()
        first = []
        for a in range(n):
            first.append(copy(a, 0, me, sibling, src=x_refs[a]))
            first += [copy(a, 1 + j, me, (*chip, c), src=x_refs[a]) for j, chip in enumerate(chips)]
        for cp in first:
            cp.start()
        passed = []
        for a in range(n):
            for j, chip in enumerate(chips):
                copy(a, 1 + j, (*chip, c), me).wait_recv()
                fwd = copy(a, 4 + j, (*chip, c), sibling)
                fwd.start()
                passed.append(fwd)
        for a in range(n):
            copy(a, 0, sibling, me).wait_recv()
            for j, chip in enumerate(chips):
                copy(a, 4 + j, (*chip, 1 - c), me).wait_recv()
        for cp in first + passed:
            cp.wait_send()
        for cp in mine:
            cp.wait()

    hbm = pl.BlockSpec(memory_space=pl.ANY)
    return pl.pallas_call(
        body, name=name, out_shape=[jax.ShapeDtypeStruct((N_DEV, *s.shape), s.dtype) for s in shards],
        in_specs=[hbm] * n, out_specs=[hbm] * n,
        scratch_shapes=[pltpu.SemaphoreType.DMA((7 * n,)), pltpu.SemaphoreType.DMA((7 * n,)),
                        pltpu.SemaphoreType.DMA((n,))],
    )(*shards)


def _pushes(src_refs, land_refs, send_sems, recv_sems, slab_per_peer):
    x, y, c = _my_place()
    me = 4 * x + 2 * y + c
    copies = []
    for a in range(len(src_refs)):
        for k in range(1, N_DEV):
            px, py, pc = x ^ (k >> 2), y ^ ((k >> 1) & 1), c ^ (k & 1)
            src = src_refs[a].at[4 * px + 2 * py + pc] if slab_per_peer else src_refs[a]
            copies.append(pltpu.make_async_remote_copy(
                src_ref=src, dst_ref=land_refs[a].at[me], send_sem=send_sems.at[7 * a + k - 1],
                recv_sem=recv_sems.at[7 * a + k - 1], device_id=(px, py, pc), device_id_type=MESH))
    return copies


def _push_start(srcs, *, name, slab_per_peer):
    n = len(srcs)
    lands = [lax.empty((N_DEV, *(s.shape[1:] if slab_per_peer else s.shape)), s.dtype) for s in srcs]

    def body(*refs):
        src_refs, land_refs = refs[:n], refs[n:2 * n]
        send_sems, recv_sems, token = refs[2 * n], refs[2 * n + 1], refs[-1]
        for cp in _pushes(src_refs, land_refs, send_sems, recv_sems, slab_per_peer):
            cp.start()
        token[...] = jnp.zeros_like(token)

    hbm = pl.BlockSpec(memory_space=pltpu.HBM)
    sem = pl.BlockSpec(memory_space=pltpu.SEMAPHORE)
    out = pl.pallas_call(
        body, name=name,
        out_shape=(pltpu.SemaphoreType.DMA((7 * n,)), pltpu.SemaphoreType.DMA((7 * n,)),
                   *[pltpu.HBM(a.shape, a.dtype) for a in srcs + lands], jax.ShapeDtypeStruct((8, LANES), F32)),
        in_specs=[hbm] * (2 * n), out_specs=(sem, sem, *[hbm] * (2 * n), pl.BlockSpec(memory_space=pltpu.VMEM)),
        input_output_aliases={i: 2 + i for i in range(2 * n)},
        compiler_params=pltpu.CompilerParams(has_side_effects=pltpu.SideEffectType.DATAFLOW_SIDE_EFFECTING),
    )(*[pltpu.with_memory_space_constraint(a, pltpu.HBM) for a in srcs + lands])
    return out[0], out[1], list(out[2:2 + n]), list(out[2 + n:2 + 2 * n]), out[-1]


def _push_wait(send_sems, recv_sems, srcs, lands, after, *, name, slab_per_peer):
    n = len(srcs)

    def body(*refs):
        src_refs, land_refs = refs[:n], refs[n:2 * n]
        s_sems, r_sems = refs[2 * n], refs[2 * n + 1]
        for cp in _pushes(src_refs, land_refs, s_sems, r_sems, slab_per_peer):
            cp.wait_send()
            cp.wait_recv()

    hbm = pl.BlockSpec(memory_space=pltpu.HBM)
    sem = pl.BlockSpec(memory_space=pltpu.SEMAPHORE)
    out = pl.pallas_call(
        body, name=name, out_shape=tuple(pltpu.HBM(a.shape, a.dtype) for a in srcs + lands),
        in_specs=[hbm] * (2 * n) + [sem, sem, pl.BlockSpec(memory_space=pl.ANY)], out_specs=tuple([hbm] * (2 * n)),
        input_output_aliases={i: i for i in range(2 * n)},
        compiler_params=pltpu.CompilerParams(has_side_effects=pltpu.SideEffectType.DATAFLOW_SIDE_EFFECTING),
    )(*srcs, *lands, send_sems, recv_sems, after)
    return list(out[:n]), list(out[n:])


def _sum_parts(gp_ref, rows):
    g = gp_ref[0, 0:rows, :].astype(F32)
    for p in range(1, gp_ref.shape[0]):
        g = g + gp_ref[p, 0:rows, :].astype(F32)
    return g


def _adamw_update(w, m, v, g):
    m_new = ADAM_B1 * m + (1.0 - ADAM_B1) * g
    v_new = ADAM_B2 * v + (1.0 - ADAM_B2) * (g * g)
    m_hat = m_new / (1.0 - ADAM_B1 ** ADAM_STEP)
    v_hat = v_new / (1.0 - ADAM_B2 ** ADAM_STEP)
    return -ADAM_LR * (m_hat / (jnp.sqrt(v_hat) + ADAM_EPS) + ADAM_WD * w), m_new, v_new


def _adamw_many(ws, ms, vs, gparts, sums, *, name):
    n, k = len(ws), len(sums)

    def body(*refs):
        w_refs, m_refs, v_refs = refs[:n], refs[n:2 * n], refs[2 * n:3 * n]
        g_refs, s_refs, outs = refs[3 * n:4 * n], refs[4 * n:4 * n + k], refs[4 * n + k:]
        for a in range(n):
            g = _sum_parts(g_refs[a], w_refs[a].shape[0])
            d, m_new, v_new = _adamw_update(w_refs[a][...], m_refs[a][...], v_refs[a][...], g)
            for o_ref, val in zip(outs[4 * a:4 * a + 4], (g, d, m_new, v_new)):
                o_ref[...] = val
        for b in range(k):
            outs[4 * n + b][...] = _sum_parts(s_refs[b], s_refs[b].shape[1])

    out_shape = [jax.ShapeDtypeStruct(w.shape, F32) for w in ws for _ in range(4)]
    out_shape += [jax.ShapeDtypeStruct(s.shape[1:], F32) for s in sums]
    return pl.pallas_call(body, name=name, out_shape=out_shape, compiler_params=_params())(
        *ws, *ms, *vs, *gparts, *sums)


def _adamw(w, m, v, gparts, *, tr, name):
    R, C = w.shape
    n_parts = gparts.shape[0]

    def body(w_ref, m_ref, v_ref, gp_ref, g_ref, d_ref, nm_ref, nv_ref):
        g = _sum_parts(gp_ref, tr)
        d_ref[...], nm_ref[...], nv_ref[...] = _adamw_update(w_ref[...], m_ref[...], v_ref[...], g)
        g_ref[...] = g

    row = pl.BlockSpec((tr, C), lambda i: (i, 0))
    shp = jax.ShapeDtypeStruct((R, C), F32)
    return pl.pallas_call(
        body, name=name, grid=(R // tr,),
        in_specs=[row, row, row, pl.BlockSpec((n_parts, tr, C), lambda i: (0, i, 0))],
        out_specs=[row, row, row, row], out_shape=[shp, shp, shp, shp],
        compiler_params=_params(("parallel",)),
    )(w, m, v, gparts)


def _rope_tables(s):
    pos = jnp.arange(s, dtype=F32)
    inv_freq = 1.0 / (ROPE_THETA ** (jnp.arange(0, QK_ROPE, 2, dtype=F32) / QK_ROPE))
    ang = pos[:, None] * inv_freq[None, :]
    cos, sin = jnp.cos(ang), jnp.sin(ang)
    zero = jnp.zeros((s, LANES - QK_ROPE), F32)
    return jnp.concatenate([cos, cos, zero], -1), jnp.concatenate([sin, sin, zero], -1)


def _pick(n, want):
    t = min(n, want)
    assert n % t == 0
    return t


def _local_step(x, target, wts, small, hooks):
    S = x.shape[0]
    H = MLA_HEADS
    ts = _pick(S, 512)
    tm = _pick(S, 512)
    tm_wide = _pick(S, 1024)
    tk_s = _pick(S, 2048)
    row = lambda v: v.reshape(1, -1)
    w_in = wts["w_in"]
    w_main = jnp.concatenate([w_in[:, 0:2048], w_in[:, 2624:4672]], axis=1)
    w_ckv = jnp.concatenate([w_in[:, 2048:2624], jnp.zeros((D_MODEL, CKV_W - 576), BF16)], axis=1)
    w_uq3 = wts["w_uq"].reshape(Q_LORA, H, QK_NOPE + QK_ROPE)
    w_uq_p = jnp.concatenate(
        [w_uq3[:, :, :QK_NOPE].reshape(Q_LORA, H * QK_NOPE),
         jnp.pad(w_uq3[:, :, QK_NOPE:], ((0, 0), (0, 0), (0, LANES - QK_ROPE))).reshape(Q_LORA, H * LANES)], axis=1)
    w_ukv = wts["w_ukv"]
    cos, sin = _rope_tables(S)
    conv_w, conv_b = small["conv_w"], row(small["conv_b"])
    wa, wx = small["lru_wa"].astype(BF16), small["lru_wx"].astype(BF16)
    wat, wxt = jnp.swapaxes(wa, 1, 2), jnp.swapaxes(wx, 1, 2)
    ba, bx = small["lru_ba"].reshape(RNN_BLOCKS, 1, RNN_BLOCK_W), small["lru_bx"].reshape(RNN_BLOCKS, 1, RNN_BLOCK_W)
    lam = row(small["lru_lambda"])
    q_norm, kv_norm = row(small["q_norm"]), row(small["kv_norm"])
    norm_mix, norm_mlp, norm_final = row(small["norm_mix"]), row(small["norm_mlp"]), row(small["norm_final"])

    xn = _rmsnorm_cast(x, norm_mix, ts=ts, name="norm_mix")
    ident = lambda acc: (acc,)
    (z_main,) = _mm(xn, w_main, name="z_main", tm=tm_wide, tn=1024, tk=1024, outs=[("tile", F32)], epilogue=ident)
    (z_ckv,) = _mm(xn, w_ckv, name="z_ckv", tm=tm, tn=CKV_W, tk=1024, outs=[("tile", F32)], epilogue=ident)
    tt = _pick(S, 256)
    h = _lru_fwd(z_main, conv_w, conv_b, wa, ba, wx, bx, lam, tt=tt)
    q, k, v = _mla_proj(z_ckv, q_norm, kv_norm, w_uq_p, w_ukv, cos, sin, ts=_pick(S, 256))
    ta = _pick(S, 1024)
    o, lse = _attn_fwd(q, k, v, t=ta, hb=2)
    merged = _merge_fwd(h, z_main, o, ts=_pick(S, 256))
    w_out, w_up, w_down = hooks["weights_later"](merged)

    def ep_h1(acc, xv, g):
        h1 = acc + xv
        n2, _ = _rms_fwd(h1, g)
        return h1, n2

    h1, n2 = _mm(merged, w_out, name="h1", tm=tm, tn=1024, tk=1024, outs=[("tile", F32), ("tile", BF16)],
                 epilogue=ep_h1, extras=[("tile", x), ("row", norm_mlp)])

    def ep_up(acc):
        r = jnp.maximum(acc, 0.0)
        return r * r, r

    act, relu = _mm(n2, w_up, name="up", tm=tm_wide, tn=1024, tk=1024, outs=[("tile", BF16), ("tile", BF16)],
                    epilogue=ep_up)

    def ep_loss(acc, h1v, tgt, g):
        h2 = acc + h1v
        y, _ = _rms_fwd(h2, g)
        err = y - tgt
        loss_rows = 0.5 * jnp.mean(err * err, axis=-1, keepdims=True)
        dy = err * (1.0 / D_MODEL)
        dh2, dg_rows = _rms_bwd(dy, h2, g)
        lsum = jnp.sum(loss_rows, axis=0, keepdims=True)
        return dh2, dh2, jnp.sum(dg_rows, axis=0, keepdims=True), jnp.broadcast_to(lsum, (1, D_MODEL))

    dh2, dh2b, dnf_p, loss_p = _mm(
        act, w_down, name="down_loss", tm=tm, tn=1024, tk=D_FF,
        outs=[("tile", F32), ("tile", BF16), ("rowpart", F32), ("rowpart", F32)], epilogue=ep_loss,
        extras=[("tile", h1), ("tile", target), ("row", norm_final)])
    loss_part = jnp.sum(loss_p[:, 0, 0])
    d_norm_final = jnp.sum(dnf_p, axis=(0, 1))

    def ep_du(acc, r):
        return (acc * (2.0 * r.astype(F32)),)

    (du,) = _mm(dh2b, w_down, name="d_act", tb=True, tm=tm_wide, tn=1024, tk=1024, outs=[("tile", BF16)], epilogue=ep_du,
                extras=[("tile", relu)])

    def ep_dh1(acc, h1v, dh2v, g):
        dv, dg_rows = _rms_bwd(acc, h1v, g)
        dh1 = dh2v + dv
        return dh1, dh1, jnp.sum(dg_rows, axis=0, keepdims=True)

    dh1, dh1b, dnm_p = _mm(du, w_up, name="d_n2", tb=True, tm=tm, tn=1024, tk=D_FF,
                           outs=[("tile", F32), ("tile", BF16), ("rowpart", F32)], epilogue=ep_dh1,
                           extras=[("tile", h1), ("tile", dh2), ("row", norm_mlp)])
    d_norm_mlp = jnp.sum(dnm_p, axis=(0, 1))
    tn_mm = functools.partial(_mm, ta=True, tk=tk_s, outs=[("tile", BF16)], epilogue=ident)
    (d_w_down,) = tn_mm(act, dh2b, name="dw_down", tm=1024, tn=1024)
    (p_w_up,) = _mm(n2, du, name="dw_up", ta=True, tk=tk_s, tm=1024, tn=D_FF // N_DEV, outs=[("colshard", BF16)],
                    epilogue=ident)
    (d_w_out,) = tn_mm(merged, dh1b, name="dw_out", tm=1024, tn=1024)
    early = [d_w_out.reshape(N_DEV, -1, D_MODEL), p_w_up, d_w_down.reshape(N_DEV, -1, D_MODEL)]
    w_out = w_out + hooks["send"]("early", early)[0, 0].astype(BF16)

    tmm = _pick(S, 256)

    def ep_dmerge(dm, hv, rg, ga, gb, ov):
        gl, dgl = _gelu_and_grad(rg)
        sa, sb = _sigmoid(ga), _sigmoid(gb)
        ya = hv * gl
        dya = dm * sa
        do = dm * sb
        dga = dm * ya * sa * (1.0 - sa)
        dgb = dm * ov * sb * (1.0 - sb)
        dh = dya * gl
        drg = dya * hv * dgl
        dov = do * ov
        lane = lax.broadcasted_iota(jnp.int32, (dm.shape[0], LANES), 1)
        delta = jnp.zeros((dm.shape[0], LANES), F32)
        for hh in range(H):
            dsum = jnp.sum(dov[:, hh * V_HEAD:(hh + 1) * V_HEAD], axis=1, keepdims=True)
            delta = jnp.where(lane == hh, dsum, delta)
        return dh, drg, dga, dgb, do, delta

    dh_lru, d_rg, d_ga, d_gb, do, delta_w = _mm(
        dh1b, w_out, name="d_merge", tb=True, tm=tmm, tn=1024, tk=1024,
        outs=[("tile", F32), ("tile", BF16), ("tile", BF16), ("tile", BF16), ("tile", BF16), ("side", F32)],
        epilogue=ep_dmerge,
        extras=[("tile", h), ("tilecol", z_main, 1), ("tilecol", z_main, 2), ("tilecol", z_main, 3), ("tile", o)])
    delta_row = delta_w[:, :H].T.reshape(H, 1, S)
    lse_row = lse

    dq, dk, dv = _attn_bwd(q, k, v, do, lse_row, delta_row, t=ta)
    dz_ckv, d_w_uq_p, d_w_ukv, d_q_norm, d_kv_norm = _mla_proj_bwd(
        z_ckv, dq, dk, dv, q_norm, kv_norm, w_uq_p.T, w_ukv.T, cos, sin, ts=_pick(S, 256))
    d_w_uq = jnp.concatenate(
        [d_w_uq_p[:, :H * QK_NOPE].reshape(Q_LORA, H, QK_NOPE),
         d_w_uq_p[:, H * QK_NOPE:].reshape(Q_LORA, H, LANES)[:, :, :QK_ROPE]], axis=2).reshape(Q_LORA, -1)

    d_rx, d_wa, d_wx, d_ba, d_bx, d_lam, d_conv_w, d_conv_b = _lru_bwd(
        z_main, h, dh_lru, conv_w, conv_b, wa, wat, ba, wx, wxt, bx, lam, tt=tt)

    dz_main = jnp.concatenate([d_rx, d_rg, d_ga, d_gb], axis=1)
    (d_w_main,) = tn_mm(xn, dz_main, name="dw_main", tm=1024, tn=1024)
    (d_w_ckv,) = tn_mm(xn, dz_ckv, name="dw_ckv", tm=1024, tn=CKV_W)
    d_w_in = jnp.concatenate([d_w_main[:, 0:2048], d_w_ckv[:, 0:576], d_w_main[:, 2048:4096]], axis=1)

    def col_parts(full):
        r = full.shape[0]
        return jnp.transpose(full.astype(BF16).reshape(r, N_DEV, -1), (1, 0, 2))

    late = [col_parts(d_w_in), col_parts(d_w_uq), col_parts(d_w_ukv)]
    norm_mix = norm_mix + hooks["send"]("late", late)[0, 0]
    (dxn_ckv,) = _mm(dz_ckv, w_ckv, name="dxn_ckv", tb=True, tm=tm, tn=1024, tk=CKV_W, outs=[("tile", F32)],
                     epilogue=ident)

    def ep_dx(acc, part, xv, dh1v, g):
        dv, dg_rows = _rms_bwd(acc + part, xv, g)
        return dh1v + dv, jnp.sum(dg_rows, axis=0, keepdims=True)

    grad_x, dnx_p = _mm(dz_main, w_main, name="dx", tb=True, tm=tm, tn=1024, tk=4 * D_MODEL,
                        outs=[("tile", F32), ("rowpart", F32)], epilogue=ep_dx,
                        extras=[("tile", dxn_ckv), ("tile", x), ("tile", dh1), ("row", norm_mix)])
    d_norm_mix = jnp.sum(dnx_p, axis=(0, 1))
    sm = {"norm_mix": d_norm_mix, "conv_w": d_conv_w, "conv_b": d_conv_b.reshape(-1), "lru_wa": d_wa,
          "lru_ba": d_ba.reshape(RNN_BLOCKS, RNN_BLOCK_W), "lru_wx": d_wx, "lru_bx": d_bx.reshape(RNN_BLOCKS, RNN_BLOCK_W),
          "lru_lambda": d_lam.reshape(-1), "q_norm": d_q_norm.reshape(-1), "kv_norm": d_kv_norm.reshape(-1),
          "norm_mlp": d_norm_mlp, "norm_final": d_norm_final}
    return loss_part, grad_x, sm


BIG = ("w_in", "w_uq", "w_ukv", "w_out", "w_up", "w_down")
SMALL = ("norm_mix", "conv_b", "lru_wa", "lru_ba", "lru_wx", "lru_bx", "lru_lambda", "q_norm", "kv_norm", "norm_mlp",
         "norm_final")
WEIGHTS = ("norm_mix", "w_in", "conv_w", "conv_b", "lru_wa", "lru_ba", "lru_wx", "lru_bx", "lru_lambda", "q_norm", "w_uq",
           "kv_norm", "w_ukv", "w_out", "norm_mlp", "w_up", "w_down", "norm_final")
ADAM_TILE_ROWS = {"w_in": 256, "w_uq": 128, "w_ukv": 128, "w_out": 64, "w_up": 256, "w_down": 128}
CONV_ROWS = N_DEV * 8


def _rows(a):
    return a.reshape(-1, LANES)


def _pad_rows(a, mult):
    r = a.shape[-2]
    pad = (-r) % mult
    if pad == 0:
        return a
    cfg = [(0, 0)] * (a.ndim - 2) + [(0, pad), (0, 0)]
    return jnp.pad(a, cfg)


def _cols_from_shards(g):
    return jnp.transpose(g, (1, 0, 2)).reshape(g.shape[1], -1)


def kernel(x, norm_mix, w_in, conv_w, conv_b, lru_wa, lru_ba, lru_wx, lru_bx, lru_lambda, q_norm, w_uq, kv_norm, w_ukv, w_out, norm_mlp, w_up, w_down, norm_final, loss_target, m_norm_mix, m_w_in, m_conv_w, m_conv_b, m_lru_wa, m_lru_ba, m_lru_wx, m_lru_bx, m_lru_lambda, m_q_norm, m_w_uq, m_kv_norm, m_w_ukv, m_w_out, m_norm_mlp, m_w_up, m_w_down, m_norm_final, v_norm_mix, v_w_in, v_conv_w, v_conv_b, v_lru_wa, v_lru_ba, v_lru_wx, v_lru_bx, v_lru_lambda, v_q_norm, v_w_uq, v_kv_norm, v_w_ukv, v_w_out, v_norm_mlp, v_w_up, v_w_down, v_norm_final):
    W = dict(norm_mix=norm_mix, w_in=w_in, conv_w=conv_w, conv_b=conv_b, lru_wa=lru_wa, lru_ba=lru_ba, lru_wx=lru_wx,
             lru_bx=lru_bx, lru_lambda=lru_lambda, q_norm=q_norm, w_uq=w_uq, kv_norm=kv_norm, w_ukv=w_ukv, w_out=w_out,
             norm_mlp=norm_mlp, w_up=w_up, w_down=w_down, norm_final=norm_final)
    M = dict(norm_mix=m_norm_mix, w_in=m_w_in, conv_w=m_conv_w, conv_b=m_conv_b, lru_wa=m_lru_wa, lru_ba=m_lru_ba,
             lru_wx=m_lru_wx, lru_bx=m_lru_bx, lru_lambda=m_lru_lambda, q_norm=m_q_norm, w_uq=m_w_uq, kv_norm=m_kv_norm,
             w_ukv=m_w_ukv, w_out=m_w_out, norm_mlp=m_norm_mlp, w_up=m_w_up, w_down=m_w_down, norm_final=m_norm_final)
    V = dict(norm_mix=v_norm_mix, w_in=v_w_in, conv_w=v_conv_w, conv_b=v_conv_b, lru_wa=v_lru_wa, lru_ba=v_lru_ba,
             lru_wx=v_lru_wx, lru_bx=v_lru_bx, lru_lambda=v_lru_lambda, q_norm=v_q_norm, w_uq=v_w_uq, kv_norm=v_kv_norm,
             w_ukv=v_w_ukv, w_out=v_w_out, norm_mlp=v_norm_mlp, w_up=v_w_up, w_down=v_w_down, norm_final=v_norm_final)
    me = 4 * lax.axis_index("x") + 2 * lax.axis_index("y") + lax.axis_index("c")

    first, later = ("w_in", "w_uq", "w_ukv"), ("w_out", "w_up", "w_down")
    got = _all_gather([W[n].astype(BF16) for n in first] + [_pad_rows(conv_w, 8)], name="gather_weights")
    wts = {"w_in": _cols_from_shards(got[0]), "w_uq": _cols_from_shards(got[1]), "w_ukv": _cols_from_shards(got[2])}
    w_send, w_recv, w_src, w_land, zeros = _push_start([W[n].astype(BF16) for n in later], name="gather_later_start",
                                                       slab_per_peer=False)
    small = {n: W[n] for n in SMALL}
    small["conv_w"] = _cols_from_shards(got[3][:, :CONV_WIDTH])
    small["norm_mix"] = norm_mix + zeros[0, 0]

    def with_own_slab(land, mine):
        return lax.dynamic_update_slice(land, mine, (me, 0, 0))

    def weights_later(after):
        srcs, lands = _push_wait(w_send, w_recv, w_src, w_land, after, name="gather_later_wait", slab_per_peer=False)
        w_out_g, w_up_g, w_down_g = [with_own_slab(l, s[None]) for l, s in zip(lands, srcs)]
        return w_out_g.reshape(-1, D_MODEL), _cols_from_shards(w_up_g), w_down_g.reshape(-1, D_MODEL)

    sent = {}

    def send(group, parts):
        sent[group] = _push_start(parts, name="exchange_" + group + "_start", slab_per_peer=True)
        return sent[group][4]

    loss_part, grad_x, g_small = _local_step(x[0], loss_target[0], wts, small,
                                              {"weights_later": weights_later, "send": send})

    G, Dl, NM, NV = {}, {}, {}, {}
    for group, names in (("early", later), ("late", first)):
        s_sems, r_sems, srcs, lands, _ = sent[group]
        srcs, lands = _push_wait(s_sems, r_sems, srcs, lands, grad_x, name="exchange_" + group + "_wait",
                                 slab_per_peer=True)
        for n, src, land in zip(names, srcs, lands):
            parts = with_own_slab(land, lax.dynamic_slice(src, (me, 0, 0), (1, *src.shape[1:])))
            G[n], Dl[n], NM[n], NV[n] = _adamw(W[n], M[n], V[n], parts, tr=ADAM_TILE_ROWS[n], name="adamw_" + n)

    conv_rows = _pad_rows(jnp.transpose(g_small["conv_w"].reshape(CONV_WIDTH, N_DEV, LANES), (1, 0, 2)), 8)
    loss_rows = jnp.zeros((8, LANES), F32).at[0, 0].set(loss_part)
    gathered = _all_gather([_pad_rows(_rows(g_small[n]), 8) for n in SMALL]
                           + [conv_rows.reshape(CONV_ROWS, LANES), loss_rows], name="gather_small")
    k = len(SMALL)
    outs = _adamw_many([_rows(W[n]) for n in SMALL], [_rows(M[n]) for n in SMALL], [_rows(V[n]) for n in SMALL],
                       gathered[:k], gathered[k:], name="adamw_small")
    for j, n in enumerate(SMALL):
        for out, o in zip((G, Dl, NM, NV), outs[4 * j:4 * j + 4]):
            out[n] = o.reshape(W[n].shape)
    conv_sum, loss_sum = outs[4 * k:]
    loss = loss_sum[0, 0]

    g_conv = lax.dynamic_slice(conv_sum, (me * 8, 0), (8, LANES))
    conv_out = _adamw(_pad_rows(conv_w, 8), _pad_rows(m_conv_w, 8), _pad_rows(v_conv_w, 8), g_conv[None], tr=8,
                      name="adamw_conv_w")
    for out, pk in zip((G, Dl, NM, NV), conv_out):
        out["conv_w"] = pk[:CONV_WIDTH]
    return (loss, grad_x[None], *[G[n] for n in WEIGHTS], *[Dl[n] for n in WEIGHTS], *[NM[n] for n in WEIGHTS],
            *[NV[n] for n in WEIGHTS])
```

```python
import functools

import numpy as np
import jax
import jax.numpy as jnp
from jax import lax
from jax.experimental import pallas as pl
from jax.experimental.pallas import tpu as pltpu

F32 = jnp.float32
BF16 = jnp.bfloat16
MESH = pl.DeviceIdType.MESH

D_MODEL = 1024
N_DEV = 8
LANES = 128
RNN_BLOCKS = 8
RNN_BLOCK_W = 128
CONV_WIDTH = 4
LRU_C = 8.0
MLA_HEADS = 8
QK_NOPE = 128
QK_ROPE = 64
V_HEAD = 128
QK_PAD = 256
Q_LORA = 256
KV_LORA = 256
CKV_W = 640
ROPE_THETA = 10000.0
D_FF = 4096
EPS = 1e-6
ATTN_SCALE = (QK_NOPE + QK_ROPE) ** -0.5
LOG2E = 1.4426950408889634
LN2 = 0.6931471805599453
NEG = -1e30

ADAM_LR = 0.001
ADAM_B1 = 0.9
ADAM_B2 = 0.999
ADAM_EPS = 1e-08
ADAM_WD = 0.01
ADAM_STEP = 10

VMEM_LIMIT = 56 * 1024 * 1024


def _params(sem=None):
    return pltpu.CompilerParams(dimension_semantics=sem, vmem_limit_bytes=VMEM_LIMIT)


def _sigmoid(v):
    return 1.0 / (1.0 + jnp.exp(-v))


def _softplus(y):
    e = jnp.exp(-jnp.abs(y))
    u = 1.0 + e
    d = u - 1.0
    l1p = jnp.where(d == 0.0, e, jnp.log(u) * e / jnp.where(d == 0.0, 1.0, d))
    return jnp.maximum(y, 0.0) + l1p


_GELU_K = 0.7978845608028654
_GELU_C = 0.044715


def _gelu_and_grad(v):
    t = jnp.tanh(_GELU_K * (v + _GELU_C * v * v * v))
    g = 0.5 * v * (1.0 + t)
    dg = 0.5 * (1.0 + t) + 0.5 * v * (1.0 - t * t) * _GELU_K * (1.0 + 3.0 * _GELU_C * v * v)
    return g, dg


def _rms_fwd(v, g):
    rstd = lax.rsqrt(jnp.mean(v * v, axis=-1, keepdims=True) + EPS)
    return v * rstd * g, rstd


def _rms_bwd(dy, v, g):
    rstd = lax.rsqrt(jnp.mean(v * v, axis=-1, keepdims=True) + EPS)
    vh = v * rstd
    dvh = dy * g
    dv = rstd * (dvh - vh * jnp.mean(dvh * vh, axis=-1, keepdims=True))
    return dv, dy * vh


def _shift_down(v, s, fill, row):
    return jnp.where(row >= s, pltpu.roll(v, s, 0), fill)


def _shift_up(v, s, fill, row, n):
    return jnp.where(row < n - s, pltpu.roll(v, n - s, 0), fill)


def _rot_half(v, lane):
    n = v.shape[-1]
    l = lane & (LANES - 1)
    up = pltpu.roll(v, n - QK_ROPE // 2, 1)
    dn = pltpu.roll(v, QK_ROPE // 2, 1)
    return jnp.where(l < QK_ROPE // 2, -up, jnp.where(l < QK_ROPE, dn, 0.0))


def _mm(a, b, *, name, tm, tn, tk, outs, epilogue, extras=(), ta=False, tb=False):
    assert not (ta and tb)
    if ta:
        K, M = a.shape
    else:
        M, K = a.shape
    if tb:
        N, K2 = b.shape
    else:
        K2, N = b.shape
    assert K == K2 and M % tm == 0 and N % tn == 0 and K % tk == 0, (name, a.shape, b.shape)
    n_i, n_j, n_k = M // tm, N // tn, K // tk
    n_ex, n_out = len(extras), len(outs)

    def body(*refs):
        a_ref, b_ref = refs[0], refs[1]
        ex_refs = refs[2:2 + n_ex]
        out_refs = refs[2 + n_ex:2 + n_ex + n_out]
        if ta:
            part = lax.dot_general(a_ref[...], b_ref[...], (((0,), (0,)), ((), ())), preferred_element_type=F32)
        elif tb:
            part = lax.dot_general(a_ref[...], b_ref[...], (((1,), (1,)), ((), ())), preferred_element_type=F32)
        else:
            part = jnp.dot(a_ref[...], b_ref[...], preferred_element_type=F32)

        def finish(acc):
            res = epilogue(acc, *[r[...] for r in ex_refs])
            for o_ref, r in zip(out_refs, res):
                o_ref[...] = r.astype(o_ref.dtype).reshape(o_ref.shape)

        if n_k == 1:
            finish(part)
        else:
            acc_ref = refs[-1]
            k = pl.program_id(2)

            @pl.when(k == 0)
            def _():
                acc_ref[...] = part

            @pl.when(k > 0)
            def _():
                acc_ref[...] += part

            @pl.when(k == n_k - 1)
            def _():
                finish(acc_ref[...])

    a_spec = pl.BlockSpec((tk, tm), lambda j, i, k: (k, i)) if ta else pl.BlockSpec((tm, tk), lambda j, i, k: (i, k))
    b_once = dict(pipeline_mode=pl.Buffered(1)) if (n_j == 1 and n_k == 1) else {}
    if tb:
        in_specs = [a_spec, pl.BlockSpec((tn, tk), lambda j, i, k: (j, k), **b_once)]
    else:
        in_specs = [a_spec, pl.BlockSpec((tk, tn), lambda j, i, k: (k, j), **b_once)]
    for ex in extras:
        kind = ex[0]
        if kind == "tile":
            in_specs.append(pl.BlockSpec((tm, tn), lambda j, i, k: (i, j)))
        elif kind == "tilecol":
            assert n_j == 1
            in_specs.append(pl.BlockSpec((tm, tn), functools.partial(lambda c, j, i, k: (i, c), ex[2])))
        else:
            in_specs.append(pl.BlockSpec((1, tn), lambda j, i, k: (0, j)))
    out_specs, out_shape = [], []
    for kind, dt in outs:
        if kind == "tile":
            out_specs.append(pl.BlockSpec((tm, tn), lambda j, i, k: (i, j)))
            out_shape.append(jax.ShapeDtypeStruct((M, N), dt))
        elif kind == "colshard":
            out_specs.append(pl.BlockSpec((1, tm, tn), lambda j, i, k: (j, i, 0)))
            out_shape.append(jax.ShapeDtypeStruct((n_j, M, tn), dt))
        elif kind == "side":
            assert n_j == 1
            out_specs.append(pl.BlockSpec((tm, LANES), lambda j, i, k: (i, 0)))
            out_shape.append(jax.ShapeDtypeStruct((M, LANES), dt))
        else:
            out_specs.append(pl.BlockSpec((1, 1, tn), lambda j, i, k: (i, 0, j)))
            out_shape.append(jax.ShapeDtypeStruct((n_i, 1, N), dt))
    scratch = [pltpu.VMEM((tm, tn), F32)] if n_k > 1 else []
    return pl.pallas_call(
        body, name=name, grid=(n_j, n_i, n_k), in_specs=in_specs, out_specs=out_specs, out_shape=out_shape,
        scratch_shapes=scratch, compiler_params=_params(("parallel", "parallel", "arbitrary")),
    )(a, b, *[ex[1] for ex in extras])


def _rmsnorm_cast(x, g, *, ts, name):
    S, D = x.shape

    def body(x_ref, g_ref, o_ref):
        y, _ = _rms_fwd(x_ref[...], g_ref[...])
        o_ref[...] = y.astype(BF16)

    return pl.pallas_call(
        body, name=name, grid=(S // ts,),
        in_specs=[pl.BlockSpec((ts, D), lambda i: (i, 0)), pl.BlockSpec((1, D), lambda i: (0, 0))],
        out_specs=pl.BlockSpec((ts, D), lambda i: (i, 0)), out_shape=jax.ShapeDtypeStruct((S, D), BF16),
        compiler_params=_params(("parallel",)),
    )(x, g)


LRU_NB = 4


def _lru_gates(xa, wa_ref, ba_ref, wx_ref, bx_ref, lam):
    xab = xa.astype(BF16)
    W = RNN_BLOCK_W
    rs, is_ = [], []
    for j in range(LRU_NB):
        xj = xab[:, j * W:(j + 1) * W]
        rs.append(_sigmoid(jnp.dot(xj, wa_ref[j], preferred_element_type=F32) + ba_ref[j]))
        is_.append(_sigmoid(jnp.dot(xj, wx_ref[j], preferred_element_type=F32) + bx_ref[j]))
    r = jnp.concatenate(rs, axis=1)
    i = jnp.concatenate(is_, axis=1)
    sp = _softplus(-lam)
    log_a = (-LRU_C * r) * sp
    a = jnp.exp(log_a)
    y = 2.0 * log_a
    one_m = jnp.where(y > -0.01, -y * (1.0 + 0.5 * y * (1.0 + y * (1.0 / 3.0))), 1.0 - a * a)
    return r, i, sp, a, jnp.sqrt(one_m)


def _rows_before(x, tail8, k):
    e16 = jnp.concatenate([tail8, x[0:8, :]], axis=0)
    return jnp.concatenate([pltpu.roll(e16, k, 0)[8:16, :], pltpu.roll(x, k, 0)[8:, :]], axis=0)


def _rows_after(x, head8, k):
    tt = x.shape[0]
    e16 = jnp.concatenate([x[tt - 8:tt, :], head8], axis=0)
    return jnp.concatenate([pltpu.roll(x, tt - k, 0)[:tt - 8, :], pltpu.roll(e16, 16 - k, 0)[0:8, :]], axis=0)


def _scan_down(a, b, h0, a_s, b_s, c_s):
    tt, C = a.shape
    G, nch = tt // 8, C // LANES
    rin = lax.broadcasted_iota(jnp.int32, (tt, C), 0) & 7

    def in_group(v, s):
        return pltpu.roll(v.reshape(G, 8, C), s, 1).reshape(tt, C)

    A, B = a, b
    for s in (1, 2, 4):
        B = A * jnp.where(rin >= s, in_group(B, s), 0.0) + B
        A = A * jnp.where(rin >= s, in_group(A, s), 1.0)
    for j in range(nch):
        a_s[j] = A[:, j * LANES:(j + 1) * LANES]
        b_s[j] = B[:, j * LANES:(j + 1) * LANES]
    At = jnp.concatenate([a_s.at[j][pl.ds(7, G, stride=8), :] for j in range(nch)], axis=1)
    Bt = jnp.concatenate([b_s.at[j][pl.ds(7, G, stride=8), :] for j in range(nch)], axis=1)
    rowg = lax.broadcasted_iota(jnp.int32, (G, C), 0)
    s = 1
    while s < G:
        Bt = At * _shift_down(Bt, s, 0.0, rowg) + Bt
        At = At * _shift_down(At, s, 1.0, rowg)
        s *= 2
    hg = At * h0 + Bt
    cin = _shift_down(hg, 1, h0, rowg)
    for j in range(nch):
        for r in range(8):
            c_s.at[j][pl.ds(r, G, stride=8), :] = cin[:, j * LANES:(j + 1) * LANES]
    return A * jnp.concatenate([c_s[j] for j in range(nch)], axis=1) + B, hg[G - 1:G, :]


def _scan_up(c, g_in, g_next, a_s, b_s, c_s):
    tt, C = c.shape
    G, nch = tt // 8, C // LANES
    rin = lax.broadcasted_iota(jnp.int32, (tt, C), 0) & 7

    def in_group(v, s):
        return pltpu.roll(v.reshape(G, 8, C), 8 - s, 1).reshape(tt, C)

    Cc, Gv = c, g_in
    for s in (1, 2, 4):
        Gv = Gv + Cc * jnp.where(rin < 8 - s, in_group(Gv, s), 0.0)
        Cc = Cc * jnp.where(rin < 8 - s, in_group(Cc, s), 1.0)
    for j in range(nch):
        a_s[j] = Cc[:, j * LANES:(j + 1) * LANES]
        b_s[j] = Gv[:, j * LANES:(j + 1) * LANES]
    Ct = jnp.concatenate([a_s.at[j][pl.ds(0, G, stride=8), :] for j in range(nch)], axis=1)
    Gt = jnp.concatenate([b_s.at[j][pl.ds(0, G, stride=8), :] for j in range(nch)], axis=1)
    rowg = lax.broadcasted_iota(jnp.int32, (G, C), 0)
    s = 1
    while s < G:
        Gt = Gt + Ct * _shift_up(Gt, s, 0.0, rowg, G)
        Ct = Ct * _shift_up(Ct, s, 1.0, rowg, G)
        s *= 2
    gg = Gt + Ct * g_next
    cin = _shift_up(gg, 1, g_next, rowg, G)
    for j in range(nch):
        for r in range(8):
            c_s.at[j][pl.ds(r, G, stride=8), :] = cin[:, j * LANES:(j + 1) * LANES]
    return Gv + Cc * jnp.concatenate([c_s[j] for j in range(nch)], axis=1), gg[0:1, :]


def _lru_fwd(z_main, conv_w, conv_b, wa, ba, wx, bx, lam, *, tt):
    S = z_main.shape[0]
    n_t = S // tt
    BW = RNN_BLOCK_W
    W = LRU_NB * BW

    def body(x_ref, cw_ref, cb_ref, wa_ref, ba_ref, wx_ref, bx_ref, lam_ref, h_ref, tail, hc, a_s, b_s, c_s):
        t = pl.program_id(1)

        @pl.when(t == 0)
        def _():
            tail[...] = jnp.zeros((8, W), F32)
            hc[...] = jnp.zeros((8, W), F32)

        x = x_ref[...]
        before = tail[...]
        cw = cw_ref[...]
        xa = (cb_ref[...] + cw[3:4] * x + cw[2:3] * _rows_before(x, before, 1) + cw[1:2] * _rows_before(x, before, 2)
              + cw[0:1] * _rows_before(x, before, 3))
        tail[...] = x[tt - 8:tt, :]
        _r, i, _sp, a, mult = _lru_gates(xa, wa_ref, ba_ref, wx_ref, bx_ref, lam_ref[...])
        h, h_last = _scan_down(a, mult * (i * xa), hc[0:1, :], a_s, b_s, c_s)
        h_ref[...] = h
        hc[...] = jnp.broadcast_to(h_last, (8, W))

    blk = lambda n, t: (t, n)
    vec = pl.BlockSpec((1, W), lambda n, t: (0, n))
    mat = pl.BlockSpec((LRU_NB, BW, BW), lambda n, t: (n, 0, 0))
    bias = pl.BlockSpec((LRU_NB, 1, BW), lambda n, t: (n, 0, 0))
    row8 = pltpu.VMEM((8, W), F32)
    wide = pltpu.VMEM((LRU_NB, tt, LANES), F32)
    return pl.pallas_call(
        body, name="lru_fwd", grid=(RNN_BLOCKS // LRU_NB, n_t),
        in_specs=[pl.BlockSpec((tt, W), blk), pl.BlockSpec((CONV_WIDTH, W), lambda n, t: (0, n)), vec, mat, bias, mat,
                  bias, vec],
        out_specs=pl.BlockSpec((tt, W), blk), out_shape=jax.ShapeDtypeStruct((S, D_MODEL), F32),
        scratch_shapes=[row8, row8, wide, wide, wide],
        compiler_params=_params(("parallel", "arbitrary")),
    )(z_main, conv_w, conv_b, wa, ba, wx, bx, lam)


def _lru_bwd(z_main, h, dh, conv_w, conv_b, wa, wat, ba, wx, wxt, bx, lam, *, tt):
    S = z_main.shape[0]
    n_t = S // tt
    BW = RNN_BLOCK_W
    W = LRU_NB * BW
    t8 = tt // 8

    def body(x_ref, xp_ref, h_ref, hp_ref, dh_ref, cw_ref, cb_ref, wa_ref, wat_ref, ba_ref, wx_ref, wxt_ref, bx_ref,
             lam_ref, dx_ref, dwa_ref, dwx_ref, dba_ref, dbx_ref, dlam_ref, dcw_ref, dcb_ref, nxt, a_c, g_c, a_s, b_s,
             c_s):
        t = pl.program_id(1)
        tile = n_t - 1 - t

        @pl.when(t == 0)
        def _():
            a_c[...] = jnp.zeros((8, W), F32)
            g_c[...] = jnp.zeros((8, W), F32)
            nxt[...] = jnp.zeros((8, W), F32)
            dwa_ref[...] = jnp.zeros_like(dwa_ref)
            dwx_ref[...] = jnp.zeros_like(dwx_ref)
            dba_ref[...] = jnp.zeros_like(dba_ref)
            dbx_ref[...] = jnp.zeros_like(dbx_ref)
            dlam_ref[...] = jnp.zeros_like(dlam_ref)
            dcw_ref[...] = jnp.zeros_like(dcw_ref)
            dcb_ref[...] = jnp.zeros_like(dcb_ref)

        has_prev = (tile > 0).astype(F32)
        x = x_ref[...]
        before = xp_ref[...] * has_prev
        xm1, xm2, xm3 = _rows_before(x, before, 1), _rows_before(x, before, 2), _rows_before(x, before, 3)
        cw = cw_ref[...]
        xa = cb_ref[...] + cw[3:4] * x + cw[2:3] * xm1 + cw[1:2] * xm2 + cw[0:1] * xm3
        lam = lam_ref[...]
        r, i, sp, a, mult = _lru_gates(xa, wa_ref, ba_ref, wx_ref, bx_ref, lam)
        gated = i * xa
        h_prev = _rows_before(h_ref[...], hp_ref[...] * has_prev, 1)
        g, g_first = _scan_up(_rows_after(a, a_c[...], 1), dh_ref[...], g_c[0:1, :], a_s, b_s, c_s)
        a_c[...] = jnp.broadcast_to(a[0:1, :], (8, W))
        g_c[...] = jnp.broadcast_to(g_first, (8, W))
        dlog_a = g * h_prev * a - g * gated * (a * a) / mult
        dgated = g * mult
        di = dgated * xa
        dxa = dgated * i
        dr = dlog_a * (-LRU_C * sp)
        dlam_ref[...] += jnp.sum(dlog_a * (-LRU_C * r), axis=0, keepdims=True) * (-_sigmoid(-lam))
        dpr = dr * r * (1.0 - r)
        dpi = di * i * (1.0 - i)
        xab, dprb, dpib = xa.astype(BF16), dpr.astype(BF16), dpi.astype(BF16)
        tn_dims = (((0,), (0,)), ((), ()))
        back = []
        for j in range(LRU_NB):
            sl = slice(j * BW, (j + 1) * BW)
            dwa_ref[j] += lax.dot_general(xab[:, sl], dprb[:, sl], tn_dims, preferred_element_type=F32)
            dwx_ref[j] += lax.dot_general(xab[:, sl], dpib[:, sl], tn_dims, preferred_element_type=F32)
            dba_ref[j] += jnp.sum(dpr[:, sl], axis=0, keepdims=True)
            dbx_ref[j] += jnp.sum(dpi[:, sl], axis=0, keepdims=True)
            back.append(jnp.dot(dprb[:, sl], wat_ref[j], preferred_element_type=F32)
                        + jnp.dot(dpib[:, sl], wxt_ref[j], preferred_element_type=F32))
        dxa = dxa + jnp.concatenate(back, axis=1)
        after = nxt[...]
        dx = (cw[3:4] * dxa + cw[2:3] * _rows_after(dxa, after, 1) + cw[1:2] * _rows_after(dxa, after, 2)
              + cw[0:1] * _rows_after(dxa, after, 3))
        nxt[...] = dxa[0:8, :]
        dx_ref[...] = dx.astype(BF16)
        dcw_ref[3:4, :] += jnp.sum(dxa * x, axis=0, keepdims=True)
        dcw_ref[2:3, :] += jnp.sum(dxa * xm1, axis=0, keepdims=True)
        dcw_ref[1:2, :] += jnp.sum(dxa * xm2, axis=0, keepdims=True)
        dcw_ref[0:1, :] += jnp.sum(dxa * xm3, axis=0, keepdims=True)
        dcb_ref[...] += jnp.sum(dxa, axis=0, keepdims=True)

    blk = lambda n, t: (n_t - 1 - t, n)
    prev = lambda n, t: (jnp.maximum((n_t - 1 - t) * t8 - 1, 0), n)
    vec = pl.BlockSpec((1, W), lambda n, t: (0, n))
    mat = pl.BlockSpec((LRU_NB, BW, BW), lambda n, t: (n, 0, 0))
    bias = pl.BlockSpec((LRU_NB, 1, BW), lambda n, t: (n, 0, 0))
    cws = pl.BlockSpec((CONV_WIDTH, W), lambda n, t: (0, n))
    tile = pl.BlockSpec((tt, W), blk)
    prev8 = pl.BlockSpec((8, W), prev)
    row8 = pltpu.VMEM((8, W), F32)
    wide = pltpu.VMEM((LRU_NB, tt, LANES), F32)
    return pl.pallas_call(
        body, name="lru_bwd", grid=(RNN_BLOCKS // LRU_NB, n_t),
        in_specs=[tile, prev8, tile, prev8, tile, cws, vec, mat, mat, bias, mat, mat, bias, vec],
        out_specs=[tile, mat, mat, bias, bias, vec, cws, vec],
        out_shape=[jax.ShapeDtypeStruct((S, D_MODEL), BF16),
                   jax.ShapeDtypeStruct((RNN_BLOCKS, BW, BW), F32), jax.ShapeDtypeStruct((RNN_BLOCKS, BW, BW), F32),
                   jax.ShapeDtypeStruct((RNN_BLOCKS, 1, BW), F32), jax.ShapeDtypeStruct((RNN_BLOCKS, 1, BW), F32),
                   jax.ShapeDtypeStruct((1, D_MODEL), F32),
                   jax.ShapeDtypeStruct((CONV_WIDTH, D_MODEL), F32), jax.ShapeDtypeStruct((1, D_MODEL), F32)],
        scratch_shapes=[row8, row8, row8, wide, wide, wide],
        compiler_params=_params(("parallel", "arbitrary")),
    )(z_main, z_main, h, h, dh, conv_w, conv_b, wa, wat, ba, wx, wxt, bx, lam)


def _mla_proj(z_ckv, q_norm, kv_norm, w_uq, w_ukv, cos, sin, *, ts):
    S = z_ckv.shape[0]
    H = MLA_HEADS

    def body(c_ref, qn_ref, kn_ref, wq_ref, wkv_ref, cos_ref, sin_ref, q_ref, k_ref, v_ref):
        c = c_ref[...]
        cqn, _ = _rms_fwd(c[:, 0:Q_LORA], qn_ref[...])
        ckn, _ = _rms_fwd(c[:, Q_LORA:Q_LORA + KV_LORA], kn_ref[...])
        q = jnp.dot(cqn.astype(BF16), wq_ref[...], preferred_element_type=F32) * (ATTN_SCALE * LOG2E)
        kv = jnp.dot(ckn.astype(BF16), wkv_ref[...], preferred_element_type=F32)
        cos1, sin1 = cos_ref[...], sin_ref[...]
        cos8 = jnp.concatenate([cos1] * H, axis=1)
        sin8 = jnp.concatenate([sin1] * H, axis=1)
        qr = q[:, H * QK_NOPE:]
        lane8 = lax.broadcasted_iota(jnp.int32, qr.shape, 1)
        qr = qr * cos8 + _rot_half(qr, lane8) * sin8
        kr = c[:, Q_LORA + KV_LORA:]
        lane1 = lax.broadcasted_iota(jnp.int32, kr.shape, 1)
        kr = (kr * cos1 + _rot_half(kr, lane1) * sin1).astype(BF16)
        for h in range(H):
            q_ref[h, :, 0:QK_NOPE] = q[:, h * QK_NOPE:(h + 1) * QK_NOPE].astype(BF16)
            q_ref[h, :, QK_NOPE:] = qr[:, h * LANES:(h + 1) * LANES].astype(BF16)
            k_ref[h, :, 0:QK_NOPE] = kv[:, h * 2 * LANES:h * 2 * LANES + LANES].astype(BF16)
            k_ref[h, :, QK_NOPE:] = kr
            v_ref[h] = kv[:, h * 2 * LANES + LANES:(h + 1) * 2 * LANES].astype(BF16)

    full = lambda shape: pl.BlockSpec(shape, lambda i: (0,) * len(shape))
    return pl.pallas_call(
        body, name="mla_proj", grid=(S // ts,),
        in_specs=[pl.BlockSpec((ts, CKV_W), lambda i: (i, 0)), full((1, Q_LORA)), full((1, KV_LORA)),
                  full(w_uq.shape), full(w_ukv.shape), pl.BlockSpec((ts, LANES), lambda i: (i, 0)),
                  pl.BlockSpec((ts, LANES), lambda i: (i, 0))],
        out_specs=[pl.BlockSpec((H, ts, QK_PAD), lambda i: (0, i, 0)), pl.BlockSpec((H, ts, QK_PAD), lambda i: (0, i, 0)),
                   pl.BlockSpec((H, ts, V_HEAD), lambda i: (0, i, 0))],
        out_shape=[jax.ShapeDtypeStruct((H, S, QK_PAD), BF16), jax.ShapeDtypeStruct((H, S, QK_PAD), BF16),
                   jax.ShapeDtypeStruct((H, S, V_HEAD), BF16)],
        compiler_params=_params(("parallel",)),
    )(z_ckv, q_norm, kv_norm, w_uq, w_ukv, cos, sin)


def _mla_proj_bwd(z_ckv, dq, dk, dv, q_norm, kv_norm, w_uqt, w_ukvt, cos, sin, *, ts):
    S = z_ckv.shape[0]
    H = MLA_HEADS

    def body(c_ref, dq_ref, dk_ref, dv_ref, qn_ref, kn_ref, wqt_ref, wkvt_ref, cos_ref, sin_ref,
             dz_ref, dwq_ref, dwkv_ref, dqn_ref, dkn_ref):
        @pl.when(pl.program_id(0) == 0)
        def _():
            dwq_ref[...] = jnp.zeros_like(dwq_ref)
            dwkv_ref[...] = jnp.zeros_like(dwkv_ref)
            dqn_ref[...] = jnp.zeros_like(dqn_ref)
            dkn_ref[...] = jnp.zeros_like(dkn_ref)

        c = c_ref[...]
        cq, ck = c[:, 0:Q_LORA], c[:, Q_LORA:Q_LORA + KV_LORA]
        qn, kn = qn_ref[...], kn_ref[...]
        cqn, _ = _rms_fwd(cq, qn)
        ckn, _ = _rms_fwd(ck, kn)
        cos1, sin1 = cos_ref[...], sin_ref[...]
        lane1 = lax.broadcasted_iota(jnp.int32, cos1.shape, 1)

        def unrope(g):
            return g * cos1 - _rot_half(g * sin1, lane1)

        dq_all = jnp.concatenate([dq_ref[h, :, 0:QK_NOPE] for h in range(H)]
                                 + [unrope(dq_ref[h, :, QK_NOPE:]) for h in range(H)], axis=1)
        dq_all = (dq_all * ATTN_SCALE).astype(BF16)
        dkv_all = jnp.concatenate([p for h in range(H) for p in (dk_ref[h, :, 0:QK_NOPE], dv_ref[h])],
                                  axis=1).astype(BF16)
        dkr = dk_ref[0, :, QK_NOPE:].astype(F32)
        for h in range(1, H):
            dkr = dkr + dk_ref[h, :, QK_NOPE:].astype(F32)
        dkr = unrope(dkr)
        tn_dims = (((0,), (0,)), ((), ()))
        dwq_ref[...] += lax.dot_general(cqn.astype(BF16), dq_all, tn_dims, preferred_element_type=F32)
        dwkv_ref[...] += lax.dot_general(ckn.astype(BF16), dkv_all, tn_dims, preferred_element_type=F32)
        dcqn = jnp.dot(dq_all, wqt_ref[...], preferred_element_type=F32)
        dckn = jnp.dot(dkv_all, wkvt_ref[...], preferred_element_type=F32)
        dcq, dqn_rows = _rms_bwd(dcqn, cq, qn)
        dck, dkn_rows = _rms_bwd(dckn, ck, kn)
        dqn_ref[...] += jnp.sum(dqn_rows, axis=0, keepdims=True)
        dkn_ref[...] += jnp.sum(dkn_rows, axis=0, keepdims=True)
        dz_ref[:, 0:Q_LORA] = dcq.astype(BF16)
        dz_ref[:, Q_LORA:Q_LORA + KV_LORA] = dck.astype(BF16)
        dz_ref[:, Q_LORA + KV_LORA:] = dkr.astype(BF16)

    full = lambda shape: pl.BlockSpec(shape, lambda i: (0,) * len(shape))
    return pl.pallas_call(
        body, name="mla_proj_bwd", grid=(S // ts,),
        in_specs=[pl.BlockSpec((ts, CKV_W), lambda i: (i, 0)), pl.BlockSpec((H, ts, QK_PAD), lambda i: (0, i, 0)),
                  pl.BlockSpec((H, ts, QK_PAD), lambda i: (0, i, 0)), pl.BlockSpec((H, ts, V_HEAD), lambda i: (0, i, 0)),
                  full((1, Q_LORA)), full((1, KV_LORA)), full(w_uqt.shape), full(w_ukvt.shape),
                  pl.BlockSpec((ts, LANES), lambda i: (i, 0)), pl.BlockSpec((ts, LANES), lambda i: (i, 0))],
        out_specs=[pl.BlockSpec((ts, CKV_W), lambda i: (i, 0)), full((Q_LORA, w_uqt.shape[0])),
                   full((KV_LORA, w_ukvt.shape[0])), full((1, Q_LORA)), full((1, KV_LORA))],
        out_shape=[jax.ShapeDtypeStruct((S, CKV_W), BF16), jax.ShapeDtypeStruct((Q_LORA, w_uqt.shape[0]), F32),
                   jax.ShapeDtypeStruct((KV_LORA, w_ukvt.shape[0]), F32), jax.ShapeDtypeStruct((1, Q_LORA), F32),
                   jax.ShapeDtypeStruct((1, KV_LORA), F32)],
        compiler_params=_params(("arbitrary",)),
    )(z_ckv, dq, dk, dv, q_norm, kv_norm, w_uqt, w_ukvt, cos, sin)


NT_DIMS = (((1,), (1,)), ((), ()))
TN_DIMS = (((0,), (0,)), ((), ()))


def _attn_fwd(q, k, v, *, t, hb):
    H, S, _ = q.shape
    n = S // t
    nc = t // LANES
    pairs = [(i, j) for i in range(n) for j in range(i + 1)]
    qi = jnp.asarray(np.array([p[0] for p in pairs], np.int32))
    ki = jnp.asarray(np.array([p[1] for p in pairs], np.int32))

    def body(qi_ref, ki_ref, q_ref, k_ref, v_ref, o_ref, lse_ref, m_s, l_s, acc_s):
        p = pl.program_id(1)
        i, j = qi_ref[p], ki_ref[p]

        @pl.when(j == 0)
        def _():
            m_s[...] = jnp.full(m_s.shape, NEG, F32)
            l_s[...] = jnp.zeros(l_s.shape, F32)
            acc_s[...] = jnp.zeros(acc_s.shape, F32)

        def step(masked):
            for hh in range(hb):
                s = lax.dot_general(q_ref[hh], k_ref[hh], NT_DIMS, preferred_element_type=F32)
                if masked:
                    row = lax.broadcasted_iota(jnp.int32, (t, t), 0)
                    col = lax.broadcasted_iota(jnp.int32, (t, t), 1)
                    s = jnp.where(row >= col, s, NEG)
                mc = s[:, 0:LANES]
                for c in range(1, nc):
                    mc = jnp.maximum(mc, s[:, c * LANES:(c + 1) * LANES])
                m_prev = m_s[hh]
                m_new = jnp.maximum(m_prev, jnp.max(mc, axis=1, keepdims=True))
                alpha = jnp.exp2(m_prev - m_new)
                pr = jnp.exp2(s - jnp.concatenate([m_new] * nc, axis=1))
                ls = pr[:, 0:LANES]
                for c in range(1, nc):
                    ls = ls + pr[:, c * LANES:(c + 1) * LANES]
                l_s[hh] = alpha * l_s[hh] + ls
                acc_s[hh] = alpha * acc_s[hh] + jnp.dot(pr.astype(BF16), v_ref[hh], preferred_element_type=F32)
                m_s[hh] = m_new

        @pl.when(j < i)
        def _():
            step(False)

        @pl.when(j == i)
        def _():
            step(True)
            for hh in range(hb):
                l = jnp.sum(l_s[hh], axis=1, keepdims=True)
                o_ref[:, hh * V_HEAD:(hh + 1) * V_HEAD] = acc_s[hh] / l
                lse_ref[hh] = (m_s[hh] + jnp.log2(l)).T[0:1, :]

    grid_spec = pltpu.PrefetchScalarGridSpec(
        num_scalar_prefetch=2, grid=(H // hb, len(pairs)),
        in_specs=[pl.BlockSpec((hb, t, QK_PAD), lambda h, p, qi, ki: (h, qi[p], 0)),
                  pl.BlockSpec((hb, t, QK_PAD), lambda h, p, qi, ki: (h, ki[p], 0)),
                  pl.BlockSpec((hb, t, V_HEAD), lambda h, p, qi, ki: (h, ki[p], 0))],
        out_specs=[pl.BlockSpec((t, hb * V_HEAD), lambda h, p, qi, ki: (qi[p], h)),
                   pl.BlockSpec((hb, 1, t), lambda h, p, qi, ki: (h, 0, qi[p]))],
        scratch_shapes=[pltpu.VMEM((hb, t, LANES), F32), pltpu.VMEM((hb, t, LANES), F32),
                        pltpu.VMEM((hb, t, V_HEAD), F32)],
    )
    return pl.pallas_call(
        body, name="attn_fwd", grid_spec=grid_spec,
        out_shape=[jax.ShapeDtypeStruct((S, H * V_HEAD), F32), jax.ShapeDtypeStruct((H, 1, S), F32)],
        compiler_params=_params(("parallel", "arbitrary")),
    )(qi, ki, q, k, v)


def _attn_bwd(q, k, v, do, lse_row, delta_row, *, t):
    H, S, _ = q.shape
    n = S // t
    pairs = [(i, j) for j in range(n) for i in range(j, n)]
    qi = jnp.asarray(np.array([p[0] for p in pairs], np.int32))
    ki = jnp.asarray(np.array([p[1] for p in pairs], np.int32))

    def body(qi_ref, ki_ref, q_ref, k_ref, v_ref, do_ref, lse_ref, dl_ref, dq_ref, dk_ref, dv_ref, dk_s, dv_s, dq_s):
        p = pl.program_id(1)
        i, j = qi_ref[p], ki_ref[p]

        @pl.when(p == 0)
        def _():
            dq_s[...] = jnp.zeros_like(dq_s)

        def step(masked):
            qb, kb, vb, dob = q_ref[0], k_ref[0], v_ref[0], do_ref[...]
            st = lax.dot_general(kb, qb, NT_DIMS, preferred_element_type=F32)
            if masked:
                krow = lax.broadcasted_iota(jnp.int32, (t, t), 0)
                qcol = lax.broadcasted_iota(jnp.int32, (t, t), 1)
                st = jnp.where(krow <= qcol, st, NEG)
            pt = jnp.exp2(st - lse_ref[0])
            dvp = jnp.dot(pt.astype(BF16), dob, preferred_element_type=F32)
            dpt = lax.dot_general(vb, dob, NT_DIMS, preferred_element_type=F32)
            dst = (pt * (dpt - dl_ref[0])).astype(BF16)
            dkp = jnp.dot(dst, qb, preferred_element_type=F32)
            rows = pl.ds(pl.multiple_of(i * t, t), t)
            dq_s[rows, :] += lax.dot_general(dst, kb, TN_DIMS, preferred_element_type=F32)
            return dkp, dvp

        @pl.when(i == j)
        def _():
            dkp, dvp = step(True)
            dk_s[...] = dkp
            dv_s[...] = dvp

        @pl.when(i != j)
        def _():
            dkp, dvp = step(False)
            dk_s[...] += dkp
            dv_s[...] += dvp

        @pl.when(i == n - 1)
        def _():
            dk_ref[0] = (dk_s[...] * LN2).astype(BF16)
            dv_ref[0] = dv_s[...].astype(BF16)

        @pl.when(p == len(pairs) - 1)
        def _():
            dq_ref[0] = dq_s[...].astype(BF16)

    grid_spec = pltpu.PrefetchScalarGridSpec(
        num_scalar_prefetch=2, grid=(H, len(pairs)),
        in_specs=[pl.BlockSpec((1, t, QK_PAD), lambda h, p, qi, ki: (h, qi[p], 0)),
                  pl.BlockSpec((1, t, QK_PAD), lambda h, p, qi, ki: (h, ki[p], 0)),
                  pl.BlockSpec((1, t, V_HEAD), lambda h, p, qi, ki: (h, ki[p], 0)),
                  pl.BlockSpec((t, V_HEAD), lambda h, p, qi, ki: (qi[p], h)),
                  pl.BlockSpec((1, 1, t), lambda h, p, qi, ki: (h, 0, qi[p])),
                  pl.BlockSpec((1, 1, t), lambda h, p, qi, ki: (h, 0, qi[p]))],
        out_specs=[pl.BlockSpec((1, S, QK_PAD), lambda h, p, qi, ki: (h, 0, 0)),
                   pl.BlockSpec((1, t, QK_PAD), lambda h, p, qi, ki: (h, ki[p], 0)),
                   pl.BlockSpec((1, t, V_HEAD), lambda h, p, qi, ki: (h, ki[p], 0))],
        scratch_shapes=[pltpu.VMEM((t, QK_PAD), F32), pltpu.VMEM((t, V_HEAD), F32), pltpu.VMEM((S, QK_PAD), F32)],
    )
    return pl.pallas_call(
        body, name="attn_bwd", grid_spec=grid_spec,
        out_shape=[jax.ShapeDtypeStruct((H, S, QK_PAD), BF16), jax.ShapeDtypeStruct((H, S, QK_PAD), BF16),
                   jax.ShapeDtypeStruct((H, S, V_HEAD), BF16)],
        compiler_params=_params(("parallel", "arbitrary")),
    )(qi, ki, q, k, v, do, lse_row, delta_row)


def _merge_h1(h, z_main, o, x, w_out, norm_mlp, *, ts):
    S = h.shape[0]
    D = D_MODEL

    def body(h_ref, rg_ref, ga_ref, gb_ref, o_ref, x_ref, w_ref, g_ref, m_ref, h1_ref, n2_ref):
        gl, _ = _gelu_and_grad(rg_ref[...])
        m = (_sigmoid(ga_ref[...]) * (h_ref[...] * gl) + _sigmoid(gb_ref[...]) * o_ref[...]).astype(BF16)
        m_ref[...] = m
        h1 = x_ref[...] + jnp.dot(m, w_ref[...], preferred_element_type=F32)
        h1_ref[...] = h1
        n2, _ = _rms_fwd(h1, g_ref[...])
        n2_ref[...] = n2.astype(BF16)

    col = lambda c: pl.BlockSpec((ts, D), lambda i: (i, c))
    fixed = lambda shape: pl.BlockSpec(shape, lambda i: (0, 0), pipeline_mode=pl.Buffered(1))
    return pl.pallas_call(
        body, name="merge_h1", grid=(S // ts,),
        in_specs=[col(0), col(1), col(2), col(3), col(0), col(0), fixed((D, D)), fixed((1, D))],
        out_specs=[col(0), col(0), col(0)],
        out_shape=[jax.ShapeDtypeStruct((S, D), BF16), jax.ShapeDtypeStruct((S, D), F32),
                   jax.ShapeDtypeStruct((S, D), BF16)],
        compiler_params=_params(("parallel",)),
    )(h, z_main, z_main, z_main, o, x, w_out, norm_mlp)


def _my_place():
    return lax.axis_index("x"), lax.axis_index("y"), lax.axis_index("c")


def _all_gather(shards, *, name):
    n = len(shards)

    def body(*refs):
        x_refs, out_refs = refs[:n], refs[n:2 * n]
        send_sems, recv_sems, local_sems = refs[2 * n:]
        x, y, c = _my_place()
        me, sibling = (x, y, c), (x, y, 1 - c)
        chips = [(1 - x, y), (x, 1 - y), (1 - x, 1 - y)]

        def slot(a, px, py, pc):
            return out_refs[a].at[4 * px + 2 * py + pc]

        def copy(a, k, block, to, src=None):
            return pltpu.make_async_remote_copy(
                src_ref=slot(a, *block) if src is None else src, dst_ref=slot(a, *block),
                send_sem=send_sems.at[7 * a + k], recv_sem=recv_sems.at[7 * a + k], device_id=to, device_id_type=MESH)

        mine = [pltpu.make_async_copy(x_refs[a], slot(a, *me), local_sems.at[a]) for a in range(n)]
        for cp in mine:
            cp.start()
        first = []
        for a in range(n):
            first.append(copy(a, 0, me, sibling, src=x_refs[a]))
            first += [copy(a, 1 + j, me, (*chip, c), src=x_refs[a]) for j, chip in enumerate(chips)]
        for cp in first:
            cp.start()
        passed = []
        for a in range(n):
            for j, chip in enumerate(chips):
                copy(a, 1 + j, (*chip, c), me).wait_recv()
                fwd = copy(a, 4 + j, (*chip, c), sibling)
                fwd.start()
                passed.append(fwd)
        for a in range(n):
            copy(a, 0, sibling, me).wait_recv()
            for j, chip in enumerate(chips):
                copy(a, 4 + j, (*chip, 1 - c), me).wait_recv()
        for cp in first + passed:
            cp.wait_send()
        for cp in mine:
            cp.wait()

    hbm = pl.BlockSpec(memory_space=pl.ANY)
    return pl.pallas_call(
        body, name=name, out_shape=[jax.ShapeDtypeStruct((N_DEV, *s.shape), s.dtype) for s in shards],
        in_specs=[hbm] * n, out_specs=[hbm] * n,
        scratch_shapes=[pltpu.SemaphoreType.DMA((7 * n,)), pltpu.SemaphoreType.DMA((7 * n,)),
                        pltpu.SemaphoreType.DMA((n,))],
    )(*shards)


def _pushes(src_refs, land_refs, send_sems, recv_sems, slab_per_peer):
    x, y, c = _my_place()
    me = 4 * x + 2 * y + c
    copies = []
    for a in range(len(src_refs)):
        for k in range(1, N_DEV):
            px, py, pc = x ^ (k >> 2), y ^ ((k >> 1) & 1), c ^ (k & 1)
            src = src_refs[a].at[4 * px + 2 * py + pc] if slab_per_peer else src_refs[a]
            copies.append(pltpu.make_async_remote_copy(
                src_ref=src, dst_ref=land_refs[a].at[me], send_sem=send_sems.at[7 * a + k - 1],
                recv_sem=recv_sems.at[7 * a + k - 1], device_id=(px, py, pc), device_id_type=MESH))
    return copies


def _push_start(srcs, *, name, slab_per_peer):
    n = len(srcs)
    lands = [lax.empty((N_DEV, *(s.shape[1:] if slab_per_peer else s.shape)), s.dtype) for s in srcs]

    def body(*refs):
        src_refs, land_refs = refs[:n], refs[n:2 * n]
        send_sems, recv_sems, token = refs[2 * n], refs[2 * n + 1], refs[-1]
        for cp in _pushes(src_refs, land_refs, send_sems, recv_sems, slab_per_peer):
            cp.start()
        token[...] = jnp.zeros_like(token)

    hbm = pl.BlockSpec(memory_space=pltpu.HBM)
    sem = pl.BlockSpec(memory_space=pltpu.SEMAPHORE)
    out = pl.pallas_call(
        body, name=name,
        out_shape=(pltpu.SemaphoreType.DMA((7 * n,)), pltpu.SemaphoreType.DMA((7 * n,)),
                   *[pltpu.HBM(a.shape, a.dtype) for a in srcs + lands], jax.ShapeDtypeStruct((8, LANES), F32)),
        in_specs=[hbm] * (2 * n), out_specs=(sem, sem, *[hbm] * (2 * n), pl.BlockSpec(memory_space=pltpu.VMEM)),
        input_output_aliases={i: 2 + i for i in range(2 * n)},
        compiler_params=pltpu.CompilerParams(has_side_effects=pltpu.SideEffectType.DATAFLOW_SIDE_EFFECTING),
    )(*[pltpu.with_memory_space_constraint(a, pltpu.HBM) for a in srcs + lands])
    return out[0], out[1], list(out[2:2 + n]), list(out[2 + n:2 + 2 * n]), out[-1]


def _push_wait(send_sems, recv_sems, srcs, lands, after, *, name, slab_per_peer):
    n = len(srcs)

    def body(*refs):
        src_refs, land_refs = refs[:n], refs[n:2 * n]
        s_sems, r_sems = refs[2 * n], refs[2 * n + 1]
        for cp in _pushes(src_refs, land_refs, s_sems, r_sems, slab_per_peer):
            cp.wait_send()
            cp.wait_recv()

    hbm = pl.BlockSpec(memory_space=pltpu.HBM)
    sem = pl.BlockSpec(memory_space=pltpu.SEMAPHORE)
    out = pl.pallas_call(
        body, name=name, out_shape=tuple(pltpu.HBM(a.shape, a.dtype) for a in srcs + lands),
        in_specs=[hbm] * (2 * n) + [sem, sem, pl.BlockSpec(memory_space=pl.ANY)], out_specs=tuple([hbm] * (2 * n)),
        input_output_aliases={i: i for i in range(2 * n)},
        compiler_params=pltpu.CompilerParams(has_side_effects=pltpu.SideEffectType.DATAFLOW_SIDE_EFFECTING),
    )(*srcs, *lands, send_sems, recv_sems, after)
    return list(out[:n]), list(out[n:])


def _sum_parts(gp_ref, rows):
    g = gp_ref[0, 0:rows, :].astype(F32)
    for p in range(1, gp_ref.shape[0]):
        g = g + gp_ref[p, 0:rows, :].astype(F32)
    return g


def _adamw_update(w, m, v, g):
    m_new = ADAM_B1 * m + (1.0 - ADAM_B1) * g
    v_new = ADAM_B2 * v + (1.0 - ADAM_B2) * (g * g)
    m_hat = m_new / (1.0 - ADAM_B1 ** ADAM_STEP)
    v_hat = v_new / (1.0 - ADAM_B2 ** ADAM_STEP)
    return -ADAM_LR * (m_hat / (jnp.sqrt(v_hat) + ADAM_EPS) + ADAM_WD * w), m_new, v_new


def _adamw_many(ws, ms, vs, gparts, sums, *, name):
    n, k = len(ws), len(sums)

    def body(*refs):
        w_refs, m_refs, v_refs = refs[:n], refs[n:2 * n], refs[2 * n:3 * n]
        g_refs, s_refs, outs = refs[3 * n:4 * n], refs[4 * n:4 * n + k], refs[4 * n + k:]
        for a in range(n):
            g = _sum_parts(g_refs[a], w_refs[a].shape[0])
            d, m_new, v_new = _adamw_update(w_refs[a][...], m_refs[a][...], v_refs[a][...], g)
            for o_ref, val in zip(outs[4 * a:4 * a + 4], (g, d, m_new, v_new)):
                o_ref[...] = val
        for b in range(k):
            outs[4 * n + b][...] = _sum_parts(s_refs[b], s_refs[b].shape[1])

    out_shape = [jax.ShapeDtypeStruct(w.shape, F32) for w in ws for _ in range(4)]
    out_shape += [jax.ShapeDtypeStruct(s.shape[1:], F32) for s in sums]
    return pl.pallas_call(body, name=name, out_shape=out_shape, compiler_params=_params())(
        *ws, *ms, *vs, *gparts, *sums)


def _adamw(w, m, v, gparts, *, tr, name):
    R, C = w.shape
    n_parts = gparts.shape[0]

    def body(w_ref, m_ref, v_ref, gp_ref, g_ref, d_ref, nm_ref, nv_ref):
        g = _sum_parts(gp_ref, tr)
        d_ref[...], nm_ref[...], nv_ref[...] = _adamw_update(w_ref[...], m_ref[...], v_ref[...], g)
        g_ref[...] = g

    row = pl.BlockSpec((tr, C), lambda i: (i, 0))
    shp = jax.ShapeDtypeStruct((R, C), F32)
    return pl.pallas_call(
        body, name=name, grid=(R // tr,),
        in_specs=[row, row, row, pl.BlockSpec((n_parts, tr, C), lambda i: (0, i, 0))],
        out_specs=[row, row, row, row], out_shape=[shp, shp, shp, shp],
        compiler_params=_params(("parallel",)),
    )(w, m, v, gparts)


def _rope_tables(s):
    pos = jnp.arange(s, dtype=F32)
    inv_freq = 1.0 / (ROPE_THETA ** (jnp.arange(0, QK_ROPE, 2, dtype=F32) / QK_ROPE))
    ang = pos[:, None] * inv_freq[None, :]
    cos, sin = jnp.cos(ang), jnp.sin(ang)
    zero = jnp.zeros((s, LANES - QK_ROPE), F32)
    return jnp.concatenate([cos, cos, zero], -1), jnp.concatenate([sin, sin, zero], -1)


def _pick(n, want):
    t = min(n, want)
    assert n % t == 0
    return t


def _local_step(x, target, wts, small, hooks):
    S = x.shape[0]
    H = MLA_HEADS
    ts = _pick(S, 512)
    tm = _pick(S, 512)
    tm_wide = _pick(S, 1024)
    tk_s = _pick(S, 2048)
    row = lambda v: v.reshape(1, -1)
    w_in = wts["w_in"]
    w_main = jnp.concatenate([w_in[:, 0:2048], w_in[:, 2624:4672]], axis=1)
    w_ckv = jnp.concatenate([w_in[:, 2048:2624], jnp.zeros((D_MODEL, CKV_W - 576), BF16)], axis=1)
    w_uq3 = wts["w_uq"].reshape(Q_LORA, H, QK_NOPE + QK_ROPE)
    w_uq_p = jnp.concatenate(
        [w_uq3[:, :, :QK_NOPE].reshape(Q_LORA, H * QK_NOPE),
         jnp.pad(w_uq3[:, :, QK_NOPE:], ((0, 0), (0, 0), (0, LANES - QK_ROPE))).reshape(Q_LORA, H * LANES)], axis=1)
    w_ukv = wts["w_ukv"]
    cos, sin = _rope_tables(S)
    conv_w, conv_b = small["conv_w"], row(small["conv_b"])
    wa, wx = small["lru_wa"].astype(BF16), small["lru_wx"].astype(BF16)
    wat, wxt = jnp.swapaxes(wa, 1, 2), jnp.swapaxes(wx, 1, 2)
    ba, bx = small["lru_ba"].reshape(RNN_BLOCKS, 1, RNN_BLOCK_W), small["lru_bx"].reshape(RNN_BLOCKS, 1, RNN_BLOCK_W)
    lam = row(small["lru_lambda"])
    q_norm, kv_norm = row(small["q_norm"]), row(small["kv_norm"])
    norm_mix, norm_mlp, norm_final = row(small["norm_mix"]), row(small["norm_mlp"]), row(small["norm_final"])

    xn = _rmsnorm_cast(x, norm_mix, ts=ts, name="norm_mix")
    ident = lambda acc: (acc,)
    (z_main,) = _mm(xn, w_main, name="z_main", tm=tm_wide, tn=1024, tk=1024, outs=[("tile", F32)], epilogue=ident)
    (z_ckv,) = _mm(xn, w_ckv, name="z_ckv", tm=tm, tn=CKV_W, tk=1024, outs=[("tile", F32)], epilogue=ident)
    tt = _pick(S, 256)
    h = _lru_fwd(z_main, conv_w, conv_b, wa, ba, wx, bx, lam, tt=tt)
    q, k, v = _mla_proj(z_ckv, q_norm, kv_norm, w_uq_p, w_ukv, cos, sin, ts=_pick(S, 256))
    ta = _pick(S, 1024)
    o, lse = _attn_fwd(q, k, v, t=ta, hb=2)
    w_out, w_up, w_down = hooks["weights_later"](o)
    merged, h1, n2 = _merge_h1(h, z_main, o, x, w_out, norm_mlp, ts=_pick(S, 256))

    def ep_up(acc):
        r = jnp.maximum(acc, 0.0)
        return r * r, r

    act, relu = _mm(n2, w_up, name="up", tm=tm_wide, tn=1024, tk=1024, outs=[("tile", BF16), ("tile", BF16)],
                    epilogue=ep_up)

    def ep_loss(acc, h1v, tgt, g):
        h2 = acc + h1v
        y, _ = _rms_fwd(h2, g)
        err = y - tgt
        loss_rows = 0.5 * jnp.mean(err * err, axis=-1, keepdims=True)
        dy = err * (1.0 / D_MODEL)
        dh2, dg_rows = _rms_bwd(dy, h2, g)
        lsum = jnp.sum(loss_rows, axis=0, keepdims=True)
        return dh2, dh2, jnp.sum(dg_rows, axis=0, keepdims=True), jnp.broadcast_to(lsum, (1, D_MODEL))

    dh2, dh2b, dnf_p, loss_p = _mm(
        act, w_down, name="down_loss", tm=tm, tn=1024, tk=D_FF,
        outs=[("tile", F32), ("tile", BF16), ("rowpart", F32), ("rowpart", F32)], epilogue=ep_loss,
        extras=[("tile", h1), ("tile", target), ("row", norm_final)])
    loss_part = jnp.sum(loss_p[:, 0, 0])
    d_norm_final = jnp.sum(dnf_p, axis=(0, 1))

    def ep_du(acc, r):
        return (acc * (2.0 * r.astype(F32)),)

    (du,) = _mm(dh2b, w_down, name="d_act", tb=True, tm=tm_wide, tn=1024, tk=1024, outs=[("tile", BF16)], epilogue=ep_du,
                extras=[("tile", relu)])

    def ep_dh1(acc, h1v, dh2v, g):
        dv, dg_rows = _rms_bwd(acc, h1v, g)
        dh1 = dh2v + dv
        return dh1, dh1, jnp.sum(dg_rows, axis=0, keepdims=True)

    dh1, dh1b, dnm_p = _mm(du, w_up, name="d_n2", tb=True, tm=tm, tn=1024, tk=D_FF,
                           outs=[("tile", F32), ("tile", BF16), ("rowpart", F32)], epilogue=ep_dh1,
                           extras=[("tile", h1), ("tile", dh2), ("row", norm_mlp)])
    d_norm_mlp = jnp.sum(dnm_p, axis=(0, 1))
    tn_mm = functools.partial(_mm, ta=True, tk=tk_s, outs=[("tile", BF16)], epilogue=ident)
    (d_w_down,) = tn_mm(act, dh2b, name="dw_down", tm=1024, tn=1024)
    (p_w_up,) = _mm(n2, du, name="dw_up", ta=True, tk=tk_s, tm=1024, tn=D_FF // N_DEV, outs=[("colshard", BF16)],
                    epilogue=ident)
    (d_w_out,) = tn_mm(merged, dh1b, name="dw_out", tm=1024, tn=1024)
    early = [d_w_out.reshape(N_DEV, -1, D_MODEL), p_w_up, d_w_down.reshape(N_DEV, -1, D_MODEL)]
    w_out = w_out + hooks["send"]("early", early)[0, 0].astype(BF16)

    tmm = _pick(S, 256)

    def ep_dmerge(dm, hv, rg, ga, gb, ov):
        gl, dgl = _gelu_and_grad(rg)
        sa, sb = _sigmoid(ga), _sigmoid(gb)
        ya = hv * gl
        dya = dm * sa
        do = dm * sb
        dga = dm * ya * sa * (1.0 - sa)
        dgb = dm * ov * sb * (1.0 - sb)
        dh = dya * gl
        drg = dya * hv * dgl
        dov = do * ov
        lane = lax.broadcasted_iota(jnp.int32, (dm.shape[0], LANES), 1)
        delta = jnp.zeros((dm.shape[0], LANES), F32)
        for hh in range(H):
            dsum = jnp.sum(dov[:, hh * V_HEAD:(hh + 1) * V_HEAD], axis=1, keepdims=True)
            delta = jnp.where(lane == hh, dsum, delta)
        return dh, drg, dga, dgb, do, delta

    dh_lru, d_rg, d_ga, d_gb, do, delta_w = _mm(
        dh1b, w_out, name="d_merge", tb=True, tm=tmm, tn=1024, tk=1024,
        outs=[("tile", F32), ("tile", BF16), ("tile", BF16), ("tile", BF16), ("tile", BF16), ("side", F32)],
        epilogue=ep_dmerge,
        extras=[("tile", h), ("tilecol", z_main, 1), ("tilecol", z_main, 2), ("tilecol", z_main, 3), ("tile", o)])
    delta_row = delta_w[:, :H].T.reshape(H, 1, S)
    lse_row = lse

    dq, dk, dv = _attn_bwd(q, k, v, do, lse_row, delta_row, t=ta)
    dz_ckv, d_w_uq_p, d_w_ukv, d_q_norm, d_kv_norm = _mla_proj_bwd(
        z_ckv, dq, dk, dv, q_norm, kv_norm, w_uq_p.T, w_ukv.T, cos, sin, ts=_pick(S, 256))
    d_w_uq = jnp.concatenate(
        [d_w_uq_p[:, :H * QK_NOPE].reshape(Q_LORA, H, QK_NOPE),
         d_w_uq_p[:, H * QK_NOPE:].reshape(Q_LORA, H, LANES)[:, :, :QK_ROPE]], axis=2).reshape(Q_LORA, -1)

    d_rx, d_wa, d_wx, d_ba, d_bx, d_lam, d_conv_w, d_conv_b = _lru_bwd(
        z_main, h, dh_lru, conv_w, conv_b, wa, wat, ba, wx, wxt, bx, lam, tt=tt)

    dz_main = jnp.concatenate([d_rx, d_rg, d_ga, d_gb], axis=1)
    (d_w_main,) = tn_mm(xn, dz_main, name="dw_main", tm=1024, tn=1024)
    (d_w_ckv,) = tn_mm(xn, dz_ckv, name="dw_ckv", tm=1024, tn=CKV_W)
    d_w_in = jnp.concatenate([d_w_main[:, 0:2048], d_w_ckv[:, 0:576], d_w_main[:, 2048:4096]], axis=1)

    def col_parts(full):
        r = full.shape[0]
        return jnp.transpose(full.astype(BF16).reshape(r, N_DEV, -1), (1, 0, 2))

    late = [col_parts(d_w_in), col_parts(d_w_uq), col_parts(d_w_ukv)]
    norm_mix = norm_mix + hooks["send"]("late", late)[0, 0]
    (dxn_ckv,) = _mm(dz_ckv, w_ckv, name="dxn_ckv", tb=True, tm=tm, tn=1024, tk=CKV_W, outs=[("tile", F32)],
                     epilogue=ident)

    def ep_dx(acc, part, xv, dh1v, g):
        dv, dg_rows = _rms_bwd(acc + part, xv, g)
        return dh1v + dv, jnp.sum(dg_rows, axis=0, keepdims=True)

    grad_x, dnx_p = _mm(dz_main, w_main, name="dx", tb=True, tm=tm, tn=1024, tk=4 * D_MODEL,
                        outs=[("tile", F32), ("rowpart", F32)], epilogue=ep_dx,
                        extras=[("tile", dxn_ckv), ("tile", x), ("tile", dh1), ("row", norm_mix)])
    d_norm_mix = jnp.sum(dnx_p, axis=(0, 1))
    sm = {"norm_mix": d_norm_mix, "conv_w": d_conv_w, "conv_b": d_conv_b.reshape(-1), "lru_wa": d_wa,
          "lru_ba": d_ba.reshape(RNN_BLOCKS, RNN_BLOCK_W), "lru_wx": d_wx, "lru_bx": d_bx.reshape(RNN_BLOCKS, RNN_BLOCK_W),
          "lru_lambda": d_lam.reshape(-1), "q_norm": d_q_norm.reshape(-1), "kv_norm": d_kv_norm.reshape(-1),
          "norm_mlp": d_norm_mlp, "norm_final": d_norm_final}
    return loss_part, grad_x, sm


BIG = ("w_in", "w_uq", "w_ukv", "w_out", "w_up", "w_down")
SMALL = ("norm_mix", "conv_b", "lru_wa", "lru_ba", "lru_wx", "lru_bx", "lru_lambda", "q_norm", "kv_norm", "norm_mlp",
         "norm_final")
WEIGHTS = ("norm_mix", "w_in", "conv_w", "conv_b", "lru_wa", "lru_ba", "lru_wx", "lru_bx", "lru_lambda", "q_norm", "w_uq",
           "kv_norm", "w_ukv", "w_out", "norm_mlp", "w_up", "w_down", "norm_final")
ADAM_TILE_ROWS = {"w_in": 256, "w_uq": 128, "w_ukv": 128, "w_out": 64, "w_up": 256, "w_down": 128}
CONV_ROWS = N_DEV * 8


def _rows(a):
    return a.reshape(-1, LANES)


def _pad_rows(a, mult):
    r = a.shape[-2]
    pad = (-r) % mult
    if pad == 0:
        return a
    cfg = [(0, 0)] * (a.ndim - 2) + [(0, pad), (0, 0)]
    return jnp.pad(a, cfg)


def _cols_from_shards(g):
    return jnp.transpose(g, (1, 0, 2)).reshape(g.shape[1], -1)


def kernel(x, norm_mix, w_in, conv_w, conv_b, lru_wa, lru_ba, lru_wx, lru_bx, lru_lambda, q_norm, w_uq, kv_norm, w_ukv, w_out, norm_mlp, w_up, w_down, norm_final, loss_target, m_norm_mix, m_w_in, m_conv_w, m_conv_b, m_lru_wa, m_lru_ba, m_lru_wx, m_lru_bx, m_lru_lambda, m_q_norm, m_w_uq, m_kv_norm, m_w_ukv, m_w_out, m_norm_mlp, m_w_up, m_w_down, m_norm_final, v_norm_mix, v_w_in, v_conv_w, v_conv_b, v_lru_wa, v_lru_ba, v_lru_wx, v_lru_bx, v_lru_lambda, v_q_norm, v_w_uq, v_kv_norm, v_w_ukv, v_w_out, v_norm_mlp, v_w_up, v_w_down, v_norm_final):
    W = dict(norm_mix=norm_mix, w_in=w_in, conv_w=conv_w, conv_b=conv_b, lru_wa=lru_wa, lru_ba=lru_ba, lru_wx=lru_wx,
             lru_bx=lru_bx, lru_lambda=lru_lambda, q_norm=q_norm, w_uq=w_uq, kv_norm=kv_norm, w_ukv=w_ukv, w_out=w_out,
             norm_mlp=norm_mlp, w_up=w_up, w_down=w_down, norm_final=norm_final)
    M = dict(norm_mix=m_norm_mix, w_in=m_w_in, conv_w=m_conv_w, conv_b=m_conv_b, lru_wa=m_lru_wa, lru_ba=m_lru_ba,
             lru_wx=m_lru_wx, lru_bx=m_lru_bx, lru_lambda=m_lru_lambda, q_norm=m_q_norm, w_uq=m_w_uq, kv_norm=m_kv_norm,
             w_ukv=m_w_ukv, w_out=m_w_out, norm_mlp=m_norm_mlp, w_up=m_w_up, w_down=m_w_down, norm_final=m_norm_final)
    V = dict(norm_mix=v_norm_mix, w_in=v_w_in, conv_w=v_conv_w, conv_b=v_conv_b, lru_wa=v_lru_wa, lru_ba=v_lru_ba,
             lru_wx=v_lru_wx, lru_bx=v_lru_bx, lru_lambda=v_lru_lambda, q_norm=v_q_norm, w_uq=v_w_uq, kv_norm=v_kv_norm,
             w_ukv=v_w_ukv, w_out=v_w_out, norm_mlp=v_norm_mlp, w_up=v_w_up, w_down=v_w_down, norm_final=v_norm_final)
    me = 4 * lax.axis_index("x") + 2 * lax.axis_index("y") + lax.axis_index("c")

    first, later = ("w_in", "w_uq", "w_ukv"), ("w_out", "w_up", "w_down")
    got = _all_gather([W[n].astype(BF16) for n in first] + [_pad_rows(conv_w, 8)], name="gather_weights")
    wts = {"w_in": _cols_from_shards(got[0]), "w_uq": _cols_from_shards(got[1]), "w_ukv": _cols_from_shards(got[2])}
    w_send, w_recv, w_src, w_land, zeros = _push_start([W[n].astype(BF16) for n in later], name="gather_later_start",
                                                       slab_per_peer=False)
    small = {n: W[n] for n in SMALL}
    small["conv_w"] = _cols_from_shards(got[3][:, :CONV_WIDTH])
    small["norm_mix"] = norm_mix + zeros[0, 0]

    def with_own_slab(land, mine):
        return lax.dynamic_update_slice(land, mine, (me, 0, 0))

    def weights_later(after):
        srcs, lands = _push_wait(w_send, w_recv, w_src, w_land, after, name="gather_later_wait", slab_per_peer=False)
        w_out_g, w_up_g, w_down_g = [with_own_slab(l, s[None]) for l, s in zip(lands, srcs)]
        return w_out_g.reshape(-1, D_MODEL), _cols_from_shards(w_up_g), w_down_g.reshape(-1, D_MODEL)

    sent = {}
    G, Dl, NM, NV = {}, {}, {}, {}

    def finish(group, names, after):
        s_sems, r_sems, srcs, lands, _ = sent[group]
        srcs, lands = _push_wait(s_sems, r_sems, srcs, lands, after, name="exchange_" + group + "_wait",
                                 slab_per_peer=True)
        for n, src, land in zip(names, srcs, lands):
            parts = with_own_slab(land, lax.dynamic_slice(src, (me, 0, 0), (1, *src.shape[1:])))
            G[n], Dl[n], NM[n], NV[n] = _adamw(W[n], M[n], V[n], parts, tr=ADAM_TILE_ROWS[n], name="adamw_" + n)

    def send(group, parts):
        sent[group] = _push_start(parts, name="exchange_" + group + "_start", slab_per_peer=True)
        zeros = sent[group][4]
        if group == "late":
            finish("early", later, zeros)
            zeros = zeros + 0.0 * (Dl["w_out"][0:8, 0:LANES] + Dl["w_up"][0:8, 0:LANES] + Dl["w_down"][0:8, 0:LANES])
        return zeros

    loss_part, grad_x, g_small = _local_step(x[0], loss_target[0], wts, small,
                                              {"weights_later": weights_later, "send": send})
    finish("late", first, grad_x)

    conv_rows = _pad_rows(jnp.transpose(g_small["conv_w"].reshape(CONV_WIDTH, N_DEV, LANES), (1, 0, 2)), 8)
    loss_rows = jnp.zeros((8, LANES), F32).at[0, 0].set(loss_part)
    gathered = _all_gather([_pad_rows(_rows(g_small[n]), 8) for n in SMALL]
                           + [conv_rows.reshape(CONV_ROWS, LANES), loss_rows], name="gather_small")
    k = len(SMALL)
    outs = _adamw_many([_rows(W[n]) for n in SMALL], [_rows(M[n]) for n in SMALL], [_rows(V[n]) for n in SMALL],
                       gathered[:k], gathered[k:], name="adamw_small")
    for j, n in enumerate(SMALL):
        for out, o in zip((G, Dl, NM, NV), outs[4 * j:4 * j + 4]):
            out[n] = o.reshape(W[n].shape)
    conv_sum, loss_sum = outs[4 * k:]
    loss = loss_sum[0, 0]

    g_conv = lax.dynamic_slice(conv_sum, (me * 8, 0), (8, LANES))
    conv_out = _adamw(_pad_rows(conv_w, 8), _pad_rows(m_conv_w, 8), _pad_rows(v_conv_w, 8), g_conv[None], tr=8,
                      name="adamw_conv_w")
    for out, pk in zip((G, Dl, NM, NV), conv_out):
        out["conv_w"] = pk[:CONV_WIDTH]
    return (loss, grad_x[None], *[G[n] for n in WEIGHTS], *[Dl[n] for n in WEIGHTS], *[NM[n] for n in WEIGHTS],
            *[NV[n] for n in WEIGHTS])
```

```python
import functools

import numpy as np
import jax
import jax.numpy as jnp
from jax import lax
from jax.experimental import pallas as pl
from jax.experimental.pallas import tpu as pltpu

F32 = jnp.float32
BF16 = jnp.bfloat16
MESH = pl.DeviceIdType.MESH

D_MODEL = 1024
N_DEV = 8
LANES = 128
RNN_BLOCKS = 8
RNN_BLOCK_W = 128
CONV_WIDTH = 4
LRU_C = 8.0
MLA_HEADS = 8
QK_NOPE = 128
QK_ROPE = 64
V_HEAD = 128
QK_PAD = 256
Q_LORA = 256
KV_LORA = 256
CKV_W = 640
ROPE_THETA = 10000.0
D_FF = 4096
EPS = 1e-6
ATTN_SCALE = (QK_NOPE + QK_ROPE) ** -0.5
LOG2E = 1.4426950408889634
LN2 = 0.6931471805599453
NEG = -1e30

ADAM_LR = 0.001
ADAM_B1 = 0.9
ADAM_B2 = 0.999
ADAM_EPS = 1e-08
ADAM_WD = 0.01
ADAM_STEP = 10

VMEM_LIMIT = 56 * 1024 * 1024


def _params(sem=None):
    return pltpu.CompilerParams(dimension_semantics=sem, vmem_limit_bytes=VMEM_LIMIT)


def _sigmoid(v):
    return 1.0 / (1.0 + jnp.exp(-v))


def _softplus(y):
    e = jnp.exp(-jnp.abs(y))
    u = 1.0 + e
    d = u - 1.0
    l1p = jnp.where(d == 0.0, e, jnp.log(u) * e / jnp.where(d == 0.0, 1.0, d))
    return jnp.maximum(y, 0.0) + l1p


_GELU_K = 0.7978845608028654
_GELU_C = 0.044715


def _gelu_and_grad(v):
    t = jnp.tanh(_GELU_K * (v + _GELU_C * v * v * v))
    g = 0.5 * v * (1.0 + t)
    dg = 0.5 * (1.0 + t) + 0.5 * v * (1.0 - t * t) * _GELU_K * (1.0 + 3.0 * _GELU_C * v * v)
    return g, dg


def _rms_fwd(v, g):
    rstd = lax.rsqrt(jnp.mean(v * v, axis=-1, keepdims=True) + EPS)
    return v * rstd * g, rstd


def _rms_bwd(dy, v, g):
    rstd = lax.rsqrt(jnp.mean(v * v, axis=-1, keepdims=True) + EPS)
    vh = v * rstd
    dvh = dy * g
    dv = rstd * (dvh - vh * jnp.mean(dvh * vh, axis=-1, keepdims=True))
    return dv, dy * vh


def _shift_down(v, s, fill, row):
    return jnp.where(row >= s, pltpu.roll(v, s, 0), fill)


def _shift_up(v, s, fill, row, n):
    return jnp.where(row < n - s, pltpu.roll(v, n - s, 0), fill)


def _rot_half(v, lane):
    n = v.shape[-1]
    l = lane & (LANES - 1)
    up = pltpu.roll(v, n - QK_ROPE // 2, 1)
    dn = pltpu.roll(v, QK_ROPE // 2, 1)
    return jnp.where(l < QK_ROPE // 2, -up, jnp.where(l < QK_ROPE, dn, 0.0))


def _mm(a, b, *, name, tm, tn, tk, outs, epilogue, extras=(), ta=False, tb=False):
    assert not (ta and tb)
    if ta:
        K, M = a.shape
    else:
        M, K = a.shape
    if tb:
        N, K2 = b.shape
    else:
        K2, N = b.shape
    assert K == K2 and M % tm == 0 and N % tn == 0 and K % tk == 0, (name, a.shape, b.shape)
    n_i, n_j, n_k = M // tm, N // tn, K // tk
    n_ex, n_out = len(extras), len(outs)

    def body(*refs):
        a_ref, b_ref = refs[0], refs[1]
        ex_refs = refs[2:2 + n_ex]
        out_refs = refs[2 + n_ex:2 + n_ex + n_out]
        if ta:
            part = lax.dot_general(a_ref[...], b_ref[...], (((0,), (0,)), ((), ())), preferred_element_type=F32)
        elif tb:
            part = lax.dot_general(a_ref[...], b_ref[...], (((1,), (1,)), ((), ())), preferred_element_type=F32)
        else:
            part = jnp.dot(a_ref[...], b_ref[...], preferred_element_type=F32)

        def finish(acc):
            res = epilogue(acc, *[r[...] for r in ex_refs])
            for o_ref, r, spec in zip(out_refs, res, outs):
                if spec[0] == "cols":
                    o_ref[:, spec[3]:spec[3] + r.shape[1]] = r.astype(o_ref.dtype)
                else:
                    o_ref[...] = r.astype(o_ref.dtype).reshape(o_ref.shape)

        if n_k == 1:
            finish(part)
        else:
            acc_ref = refs[-1]
            k = pl.program_id(2)

            @pl.when(k == 0)
            def _():
                acc_ref[...] = part

            @pl.when(k > 0)
            def _():
                acc_ref[...] += part

            @pl.when(k == n_k - 1)
            def _():
                finish(acc_ref[...])

    a_spec = pl.BlockSpec((tk, tm), lambda j, i, k: (k, i)) if ta else pl.BlockSpec((tm, tk), lambda j, i, k: (i, k))
    b_once = dict(pipeline_mode=pl.Buffered(1)) if (n_j == 1 and n_k == 1) else {}
    if tb:
        in_specs = [a_spec, pl.BlockSpec((tn, tk), lambda j, i, k: (j, k), **b_once)]
    else:
        in_specs = [a_spec, pl.BlockSpec((tk, tn), lambda j, i, k: (k, j), **b_once)]
    for ex in extras:
        kind = ex[0]
        if kind == "tile":
            in_specs.append(pl.BlockSpec((tm, tn), lambda j, i, k: (i, j)))
        elif kind == "tilecol":
            assert n_j == 1
            in_specs.append(pl.BlockSpec((tm, tn), functools.partial(lambda c, j, i, k: (i, c), ex[2])))
        else:
            in_specs.append(pl.BlockSpec((1, tn), lambda j, i, k: (0, j)))
    out_specs, out_shape = [], []
    for kind, dt, *more in outs:
        if kind == "tile":
            out_specs.append(pl.BlockSpec((tm, tn), lambda j, i, k: (i, j)))
            out_shape.append(jax.ShapeDtypeStruct((M, N), dt))
        elif kind == "colshard":
            out_specs.append(pl.BlockSpec((1, tm, tn), lambda j, i, k: (j, i, 0)))
            out_shape.append(jax.ShapeDtypeStruct((n_j, M, tn), dt))
        elif kind == "cols":
            assert n_j == 1
            out_specs.append(pl.BlockSpec((tm, more[0]), lambda j, i, k: (i, 0)))
            out_shape.append(jax.ShapeDtypeStruct((M, more[0]), dt))
        elif kind == "side":
            assert n_j == 1
            out_specs.append(pl.BlockSpec((tm, LANES), lambda j, i, k: (i, 0)))
            out_shape.append(jax.ShapeDtypeStruct((M, LANES), dt))
        else:
            out_specs.append(pl.BlockSpec((1, 1, tn), lambda j, i, k: (i, 0, j)))
            out_shape.append(jax.ShapeDtypeStruct((n_i, 1, N), dt))
    scratch = [pltpu.VMEM((tm, tn), F32)] if n_k > 1 else []
    return pl.pallas_call(
        body, name=name, grid=(n_j, n_i, n_k), in_specs=in_specs, out_specs=out_specs, out_shape=out_shape,
        scratch_shapes=scratch, compiler_params=_params(("parallel", "parallel", "arbitrary")),
    )(a, b, *[ex[1] for ex in extras])


def _rmsnorm_cast(x, g, *, ts, name):
    S, D = x.shape

    def body(x_ref, g_ref, o_ref):
        y, _ = _rms_fwd(x_ref[...], g_ref[...])
        o_ref[...] = y.astype(BF16)

    return pl.pallas_call(
        body, name=name, grid=(S // ts,),
        in_specs=[pl.BlockSpec((ts, D), lambda i: (i, 0)), pl.BlockSpec((1, D), lambda i: (0, 0))],
        out_specs=pl.BlockSpec((ts, D), lambda i: (i, 0)), out_shape=jax.ShapeDtypeStruct((S, D), BF16),
        compiler_params=_params(("parallel",)),
    )(x, g)


LRU_NB = 4


def _lru_gates(xa, wa_ref, ba_ref, wx_ref, bx_ref, lam):
    xab = xa.astype(BF16)
    W = RNN_BLOCK_W
    rs, is_ = [], []
    for j in range(LRU_NB):
        xj = xab[:, j * W:(j + 1) * W]
        rs.append(_sigmoid(jnp.dot(xj, wa_ref[j], preferred_element_type=F32) + ba_ref[j]))
        is_.append(_sigmoid(jnp.dot(xj, wx_ref[j], preferred_element_type=F32) + bx_ref[j]))
    r = jnp.concatenate(rs, axis=1)
    i = jnp.concatenate(is_, axis=1)
    sp = _softplus(-lam)
    log_a = (-LRU_C * r) * sp
    a = jnp.exp(log_a)
    y = 2.0 * log_a
    one_m = jnp.where(y > -0.01, -y * (1.0 + 0.5 * y * (1.0 + y * (1.0 / 3.0))), 1.0 - a * a)
    return r, i, sp, a, jnp.sqrt(one_m)


def _rows_before(x, tail8, k):
    e16 = jnp.concatenate([tail8, x[0:8, :]], axis=0)
    return jnp.concatenate([pltpu.roll(e16, k, 0)[8:16, :], pltpu.roll(x, k, 0)[8:, :]], axis=0)


def _rows_after(x, head8, k):
    tt = x.shape[0]
    e16 = jnp.concatenate([x[tt - 8:tt, :], head8], axis=0)
    return jnp.concatenate([pltpu.roll(x, tt - k, 0)[:tt - 8, :], pltpu.roll(e16, 16 - k, 0)[0:8, :]], axis=0)


def _scan_down(a, b, h0, a_s, b_s, c_s):
    tt, C = a.shape
    G, nch = tt // 8, C // LANES
    rin = lax.broadcasted_iota(jnp.int32, (tt, C), 0) & 7

    def in_group(v, s):
        return pltpu.roll(v.reshape(G, 8, C), s, 1).reshape(tt, C)

    A, B = a, b
    for s in (1, 2, 4):
        B = A * jnp.where(rin >= s, in_group(B, s), 0.0) + B
        A = A * jnp.where(rin >= s, in_group(A, s), 1.0)
    for j in range(nch):
        a_s[j] = A[:, j * LANES:(j + 1) * LANES]
        b_s[j] = B[:, j * LANES:(j + 1) * LANES]
    At = jnp.concatenate([a_s.at[j][pl.ds(7, G, stride=8), :] for j in range(nch)], axis=1)
    Bt = jnp.concatenate([b_s.at[j][pl.ds(7, G, stride=8), :] for j in range(nch)], axis=1)
    rowg = lax.broadcasted_iota(jnp.int32, (G, C), 0)
    s = 1
    while s < G:
        Bt = At * _shift_down(Bt, s, 0.0, rowg) + Bt
        At = At * _shift_down(At, s, 1.0, rowg)
        s *= 2
    hg = At * h0 + Bt
    cin = _shift_down(hg, 1, h0, rowg)
    for j in range(nch):
        for r in range(8):
            c_s.at[j][pl.ds(r, G, stride=8), :] = cin[:, j * LANES:(j + 1) * LANES]
    return A * jnp.concatenate([c_s[j] for j in range(nch)], axis=1) + B, hg[G - 1:G, :]


def _scan_up(c, g_in, g_next, a_s, b_s, c_s):
    tt, C = c.shape
    G, nch = tt // 8, C // LANES
    rin = lax.broadcasted_iota(jnp.int32, (tt, C), 0) & 7

    def in_group(v, s):
        return pltpu.roll(v.reshape(G, 8, C), 8 - s, 1).reshape(tt, C)

    Cc, Gv = c, g_in
    for s in (1, 2, 4):
        Gv = Gv + Cc * jnp.where(rin < 8 - s, in_group(Gv, s), 0.0)
        Cc = Cc * jnp.where(rin < 8 - s, in_group(Cc, s), 1.0)
    for j in range(nch):
        a_s[j] = Cc[:, j * LANES:(j + 1) * LANES]
        b_s[j] = Gv[:, j * LANES:(j + 1) * LANES]
    Ct = jnp.concatenate([a_s.at[j][pl.ds(0, G, stride=8), :] for j in range(nch)], axis=1)
    Gt = jnp.concatenate([b_s.at[j][pl.ds(0, G, stride=8), :] for j in range(nch)], axis=1)
    rowg = lax.broadcasted_iota(jnp.int32, (G, C), 0)
    s = 1
    while s < G:
        Gt = Gt + Ct * _shift_up(Gt, s, 0.0, rowg, G)
        Ct = Ct * _shift_up(Ct, s, 1.0, rowg, G)
        s *= 2
    gg = Gt + Ct * g_next
    cin = _shift_up(gg, 1, g_next, rowg, G)
    for j in range(nch):
        for r in range(8):
            c_s.at[j][pl.ds(r, G, stride=8), :] = cin[:, j * LANES:(j + 1) * LANES]
    return Gv + Cc * jnp.concatenate([c_s[j] for j in range(nch)], axis=1), gg[0:1, :]


def _lru_fwd(z_main, conv_w, conv_b, wa, ba, wx, bx, lam, *, tt):
    S = z_main.shape[0]
    n_t = S // tt
    BW = RNN_BLOCK_W
    W = LRU_NB * BW

    def body(x_ref, cw_ref, cb_ref, wa_ref, ba_ref, wx_ref, bx_ref, lam_ref, h_ref, tail, hc, a_s, b_s, c_s):
        t = pl.program_id(1)

        @pl.when(t == 0)
        def _():
            tail[...] = jnp.zeros((8, W), F32)
            hc[...] = jnp.zeros((8, W), F32)

        x = x_ref[...]
        before = tail[...]
        cw = cw_ref[...]
        xa = (cb_ref[...] + cw[3:4] * x + cw[2:3] * _rows_before(x, before, 1) + cw[1:2] * _rows_before(x, before, 2)
              + cw[0:1] * _rows_before(x, before, 3))
        tail[...] = x[tt - 8:tt, :]
        _r, i, _sp, a, mult = _lru_gates(xa, wa_ref, ba_ref, wx_ref, bx_ref, lam_ref[...])
        h, h_last = _scan_down(a, mult * (i * xa), hc[0:1, :], a_s, b_s, c_s)
        h_ref[...] = h
        hc[...] = jnp.broadcast_to(h_last, (8, W))

    blk = lambda n, t: (t, n)
    vec = pl.BlockSpec((1, W), lambda n, t: (0, n))
    mat = pl.BlockSpec((LRU_NB, BW, BW), lambda n, t: (n, 0, 0))
    bias = pl.BlockSpec((LRU_NB, 1, BW), lambda n, t: (n, 0, 0))
    row8 = pltpu.VMEM((8, W), F32)
    wide = pltpu.VMEM((LRU_NB, tt, LANES), F32)
    return pl.pallas_call(
        body, name="lru_fwd", grid=(RNN_BLOCKS // LRU_NB, n_t),
        in_specs=[pl.BlockSpec((tt, W), blk), pl.BlockSpec((CONV_WIDTH, W), lambda n, t: (0, n)), vec, mat, bias, mat,
                  bias, vec],
        out_specs=pl.BlockSpec((tt, W), blk), out_shape=jax.ShapeDtypeStruct((S, D_MODEL), F32),
        scratch_shapes=[row8, row8, wide, wide, wide],
        compiler_params=_params(("parallel", "arbitrary")),
    )(z_main, conv_w, conv_b, wa, ba, wx, bx, lam)


def _lru_bwd(z_main, h, dh, dz, conv_w, conv_b, wa, wat, ba, wx, wxt, bx, lam, *, tt):
    S = z_main.shape[0]
    n_t = S // tt
    BW = RNN_BLOCK_W
    W = LRU_NB * BW
    t8 = tt // 8

    def body(x_ref, xp_ref, h_ref, hp_ref, dh_ref, _dz_ref, cw_ref, cb_ref, wa_ref, wat_ref, ba_ref, wx_ref, wxt_ref,
             bx_ref, lam_ref, dx_ref, dwa_ref, dwx_ref, dba_ref, dbx_ref, dlam_ref, dcw_ref, dcb_ref, nxt, a_c, g_c, a_s,
             b_s, c_s):
        t = pl.program_id(1)
        tile = n_t - 1 - t

        @pl.when(t == 0)
        def _():
            a_c[...] = jnp.zeros((8, W), F32)
            g_c[...] = jnp.zeros((8, W), F32)
            nxt[...] = jnp.zeros((8, W), F32)
            dwa_ref[...] = jnp.zeros_like(dwa_ref)
            dwx_ref[...] = jnp.zeros_like(dwx_ref)
            dba_ref[...] = jnp.zeros_like(dba_ref)
            dbx_ref[...] = jnp.zeros_like(dbx_ref)
            dlam_ref[...] = jnp.zeros_like(dlam_ref)
            dcw_ref[...] = jnp.zeros_like(dcw_ref)
            dcb_ref[...] = jnp.zeros_like(dcb_ref)

        has_prev = (tile > 0).astype(F32)
        x = x_ref[...]
        before = xp_ref[...] * has_prev
        xm1, xm2, xm3 = _rows_before(x, before, 1), _rows_before(x, before, 2), _rows_before(x, before, 3)
        cw = cw_ref[...]
        xa = cb_ref[...] + cw[3:4] * x + cw[2:3] * xm1 + cw[1:2] * xm2 + cw[0:1] * xm3
        lam = lam_ref[...]
        r, i, sp, a, mult = _lru_gates(xa, wa_ref, ba_ref, wx_ref, bx_ref, lam)
        gated = i * xa
        h_prev = _rows_before(h_ref[...], hp_ref[...] * has_prev, 1)
        g, g_first = _scan_up(_rows_after(a, a_c[...], 1), dh_ref[...], g_c[0:1, :], a_s, b_s, c_s)
        a_c[...] = jnp.broadcast_to(a[0:1, :], (8, W))
        g_c[...] = jnp.broadcast_to(g_first, (8, W))
        dlog_a = g * h_prev * a - g * gated * (a * a) / mult
        dgated = g * mult
        di = dgated * xa
        dxa = dgated * i
        dr = dlog_a * (-LRU_C * sp)
        dlam_ref[...] += jnp.sum(dlog_a * (-LRU_C * r), axis=0, keepdims=True) * (-_sigmoid(-lam))
        dpr = dr * r * (1.0 - r)
        dpi = di * i * (1.0 - i)
        xab, dprb, dpib = xa.astype(BF16), dpr.astype(BF16), dpi.astype(BF16)
        tn_dims = (((0,), (0,)), ((), ()))
        back = []
        for j in range(LRU_NB):
            sl = slice(j * BW, (j + 1) * BW)
            dwa_ref[j] += lax.dot_general(xab[:, sl], dprb[:, sl], tn_dims, preferred_element_type=F32)
            dwx_ref[j] += lax.dot_general(xab[:, sl], dpib[:, sl], tn_dims, preferred_element_type=F32)
            dba_ref[j] += jnp.sum(dpr[:, sl], axis=0, keepdims=True)
            dbx_ref[j] += jnp.sum(dpi[:, sl], axis=0, keepdims=True)
            back.append(jnp.dot(dprb[:, sl], wat_ref[j], preferred_element_type=F32)
                        + jnp.dot(dpib[:, sl], wxt_ref[j], preferred_element_type=F32))
        dxa = dxa + jnp.concatenate(back, axis=1)
        after = nxt[...]
        dx = (cw[3:4] * dxa + cw[2:3] * _rows_after(dxa, after, 1) + cw[1:2] * _rows_after(dxa, after, 2)
              + cw[0:1] * _rows_after(dxa, after, 3))
        nxt[...] = dxa[0:8, :]
        dx_ref[...] = dx.astype(BF16)
        dcw_ref[3:4, :] += jnp.sum(dxa * x, axis=0, keepdims=True)
        dcw_ref[2:3, :] += jnp.sum(dxa * xm1, axis=0, keepdims=True)
        dcw_ref[1:2, :] += jnp.sum(dxa * xm2, axis=0, keepdims=True)
        dcw_ref[0:1, :] += jnp.sum(dxa * xm3, axis=0, keepdims=True)
        dcb_ref[...] += jnp.sum(dxa, axis=0, keepdims=True)

    blk = lambda n, t: (n_t - 1 - t, n)
    prev = lambda n, t: (jnp.maximum((n_t - 1 - t) * t8 - 1, 0), n)
    vec = pl.BlockSpec((1, W), lambda n, t: (0, n))
    mat = pl.BlockSpec((LRU_NB, BW, BW), lambda n, t: (n, 0, 0))
    bias = pl.BlockSpec((LRU_NB, 1, BW), lambda n, t: (n, 0, 0))
    cws = pl.BlockSpec((CONV_WIDTH, W), lambda n, t: (0, n))
    tile = pl.BlockSpec((tt, W), blk)
    prev8 = pl.BlockSpec((8, W), prev)
    row8 = pltpu.VMEM((8, W), F32)
    wide = pltpu.VMEM((LRU_NB, tt, LANES), F32)
    return pl.pallas_call(
        body, name="lru_bwd", grid=(RNN_BLOCKS // LRU_NB, n_t),
        in_specs=[tile, prev8, tile, prev8, tile, pl.BlockSpec(memory_space=pl.ANY), cws, vec, mat, mat, bias, mat, mat,
                  bias, vec],
        out_specs=[tile, mat, mat, bias, bias, vec, cws, vec], input_output_aliases={5: 0},
        out_shape=[jax.ShapeDtypeStruct(dz.shape, BF16),
                   jax.ShapeDtypeStruct((RNN_BLOCKS, BW, BW), F32), jax.ShapeDtypeStruct((RNN_BLOCKS, BW, BW), F32),
                   jax.ShapeDtypeStruct((RNN_BLOCKS, 1, BW), F32), jax.ShapeDtypeStruct((RNN_BLOCKS, 1, BW), F32),
                   jax.ShapeDtypeStruct((1, D_MODEL), F32),
                   jax.ShapeDtypeStruct((CONV_WIDTH, D_MODEL), F32), jax.ShapeDtypeStruct((1, D_MODEL), F32)],
        scratch_shapes=[row8, row8, row8, wide, wide, wide],
        compiler_params=_params(("parallel", "arbitrary")),
    )(z_main, z_main, h, h, dh, dz, conv_w, conv_b, wa, wat, ba, wx, wxt, bx, lam)


def _mla_proj(z_ckv, q_norm, kv_norm, w_uq, w_ukv, cos, sin, *, ts):
    S = z_ckv.shape[0]
    H = MLA_HEADS

    def body(c_ref, qn_ref, kn_ref, wq_ref, wkv_ref, cos_ref, sin_ref, q_ref, k_ref, v_ref):
        c = c_ref[...]
        cqn, _ = _rms_fwd(c[:, 0:Q_LORA], qn_ref[...])
        ckn, _ = _rms_fwd(c[:, Q_LORA:Q_LORA + KV_LORA], kn_ref[...])
        q = jnp.dot(cqn.astype(BF16), wq_ref[...], preferred_element_type=F32) * (ATTN_SCALE * LOG2E)
        kv = jnp.dot(ckn.astype(BF16), wkv_ref[...], preferred_element_type=F32)
        cos1, sin1 = cos_ref[...], sin_ref[...]
        cos8 = jnp.concatenate([cos1] * H, axis=1)
        sin8 = jnp.concatenate([sin1] * H, axis=1)
        qr = q[:, H * QK_NOPE:]
        lane8 = lax.broadcasted_iota(jnp.int32, qr.shape, 1)
        qr = qr * cos8 + _rot_half(qr, lane8) * sin8
        kr = c[:, Q_LORA + KV_LORA:]
        lane1 = lax.broadcasted_iota(jnp.int32, kr.shape, 1)
        kr = (kr * cos1 + _rot_half(kr, lane1) * sin1).astype(BF16)
        for h in range(H):
            q_ref[h, :, 0:QK_NOPE] = q[:, h * QK_NOPE:(h + 1) * QK_NOPE].astype(BF16)
            q_ref[h, :, QK_NOPE:] = qr[:, h * LANES:(h + 1) * LANES].astype(BF16)
            k_ref[h, :, 0:QK_NOPE] = kv[:, h * 2 * LANES:h * 2 * LANES + LANES].astype(BF16)
            k_ref[h, :, QK_NOPE:] = kr
            v_ref[h] = kv[:, h * 2 * LANES + LANES:(h + 1) * 2 * LANES].astype(BF16)

    full = lambda shape: pl.BlockSpec(shape, lambda i: (0,) * len(shape))
    return pl.pallas_call(
        body, name="mla_proj", grid=(S // ts,),
        in_specs=[pl.BlockSpec((ts, CKV_W), lambda i: (i, 0)), full((1, Q_LORA)), full((1, KV_LORA)),
                  full(w_uq.shape), full(w_ukv.shape), pl.BlockSpec((ts, LANES), lambda i: (i, 0)),
                  pl.BlockSpec((ts, LANES), lambda i: (i, 0))],
        out_specs=[pl.BlockSpec((H, ts, QK_PAD), lambda i: (0, i, 0)), pl.BlockSpec((H, ts, QK_PAD), lambda i: (0, i, 0)),
                   pl.BlockSpec((H, ts, V_HEAD), lambda i: (0, i, 0))],
        out_shape=[jax.ShapeDtypeStruct((H, S, QK_PAD), BF16), jax.ShapeDtypeStruct((H, S, QK_PAD), BF16),
                   jax.ShapeDtypeStruct((H, S, V_HEAD), BF16)],
        compiler_params=_params(("parallel",)),
    )(z_ckv, q_norm, kv_norm, w_uq, w_ukv, cos, sin)


def _mla_proj_bwd(z_ckv, dq, dk, dv, q_norm, kv_norm, w_uqt, w_ukvt, cos, sin, *, ts):
    S = z_ckv.shape[0]
    H = MLA_HEADS

    def body(c_ref, dq_ref, dk_ref, dv_ref, qn_ref, kn_ref, wqt_ref, wkvt_ref, cos_ref, sin_ref,
             dz_ref, dwq_ref, dwkv_ref, dqn_ref, dkn_ref):
        @pl.when(pl.program_id(0) == 0)
        def _():
            dwq_ref[...] = jnp.zeros_like(dwq_ref)
            dwkv_ref[...] = jnp.zeros_like(dwkv_ref)
            dqn_ref[...] = jnp.zeros_like(dqn_ref)
            dkn_ref[...] = jnp.zeros_like(dkn_ref)

        c = c_ref[...]
        cq, ck = c[:, 0:Q_LORA], c[:, Q_LORA:Q_LORA + KV_LORA]
        qn, kn = qn_ref[...], kn_ref[...]
        cqn, _ = _rms_fwd(cq, qn)
        ckn, _ = _rms_fwd(ck, kn)
        cos1, sin1 = cos_ref[...], sin_ref[...]
        lane1 = lax.broadcasted_iota(jnp.int32, cos1.shape, 1)

        def unrope(g):
            return g * cos1 - _rot_half(g * sin1, lane1)

        dq_all = jnp.concatenate([dq_ref[h, :, 0:QK_NOPE] for h in range(H)]
                                 + [unrope(dq_ref[h, :, QK_NOPE:]) for h in range(H)], axis=1)
        dq_all = (dq_all * ATTN_SCALE).astype(BF16)
        dkv_all = jnp.concatenate([p for h in range(H) for p in (dk_ref[h, :, 0:QK_NOPE], dv_ref[h])],
                                  axis=1).astype(BF16)
        dkr = dk_ref[0, :, QK_NOPE:].astype(F32)
        for h in range(1, H):
            dkr = dkr + dk_ref[h, :, QK_NOPE:].astype(F32)
        dkr = unrope(dkr)
        tn_dims = (((0,), (0,)), ((), ()))
        dwq_ref[...] += lax.dot_general(cqn.astype(BF16), dq_all, tn_dims, preferred_element_type=F32)
        dwkv_ref[...] += lax.dot_general(ckn.astype(BF16), dkv_all, tn_dims, preferred_element_type=F32)
        dcqn = jnp.dot(dq_all, wqt_ref[...], preferred_element_type=F32)
        dckn = jnp.dot(dkv_all, wkvt_ref[...], preferred_element_type=F32)
        dcq, dqn_rows = _rms_bwd(dcqn, cq, qn)
        dck, dkn_rows = _rms_bwd(dckn, ck, kn)
        dqn_ref[...] += jnp.sum(dqn_rows, axis=0, keepdims=True)
        dkn_ref[...] += jnp.sum(dkn_rows, axis=0, keepdims=True)
        dz_ref[:, 0:Q_LORA] = dcq.astype(BF16)
        dz_ref[:, Q_LORA:Q_LORA + KV_LORA] = dck.astype(BF16)
        dz_ref[:, Q_LORA + KV_LORA:] = dkr.astype(BF16)

    full = lambda shape: pl.BlockSpec(shape, lambda i: (0,) * len(shape))
    return pl.pallas_call(
        body, name="mla_proj_bwd", grid=(S // ts,),
        in_specs=[pl.BlockSpec((ts, CKV_W), lambda i: (i, 0)), pl.BlockSpec((H, ts, QK_PAD), lambda i: (0, i, 0)),
                  pl.BlockSpec((H, ts, QK_PAD), lambda i: (0, i, 0)), pl.BlockSpec((H, ts, V_HEAD), lambda i: (0, i, 0)),
                  full((1, Q_LORA)), full((1, KV_LORA)), full(w_uqt.shape), full(w_ukvt.shape),
                  pl.BlockSpec((ts, LANES), lambda i: (i, 0)), pl.BlockSpec((ts, LANES), lambda i: (i, 0))],
        out_specs=[pl.BlockSpec((ts, CKV_W), lambda i: (i, 0)), full((Q_LORA, w_uqt.shape[0])),
                   full((KV_LORA, w_ukvt.shape[0])), full((1, Q_LORA)), full((1, KV_LORA))],
        out_shape=[jax.ShapeDtypeStruct((S, CKV_W), BF16), jax.ShapeDtypeStruct((Q_LORA, w_uqt.shape[0]), F32),
                   jax.ShapeDtypeStruct((KV_LORA, w_ukvt.shape[0]), F32), jax.ShapeDtypeStruct((1, Q_LORA), F32),
                   jax.ShapeDtypeStruct((1, KV_LORA), F32)],
        compiler_params=_params(("arbitrary",)),
    )(z_ckv, dq, dk, dv, q_norm, kv_norm, w_uqt, w_ukvt, cos, sin)


NT_DIMS = (((1,), (1,)), ((), ()))
TN_DIMS = (((0,), (0,)), ((), ()))


def _attn_fwd(q, k, v, *, t, hb):
    H, S, _ = q.shape
    n = S // t
    pairs = [(i, j) for i in range(n) for j in range(i + 1)]
    qi = jnp.asarray(np.array([p[0] for p in pairs], np.int32))
    ki = jnp.asarray(np.array([p[1] for p in pairs], np.int32))

    def body(qi_ref, ki_ref, q_ref, k_ref, v_ref, o_ref, lse_ref, m_s, l_s, acc_s):
        p = pl.program_id(1)
        i, j = qi_ref[p], ki_ref[p]

        @pl.when(j == 0)
        def _():
            m_s[...] = jnp.full(m_s.shape, NEG, F32)
            l_s[...] = jnp.zeros(l_s.shape, F32)
            acc_s[...] = jnp.zeros(acc_s.shape, F32)

        def block(hh, r0, nr, nk, masked):
            rows = slice(r0, r0 + nr)
            s = lax.dot_general(q_ref[hh, rows, :], k_ref[hh, 0:nk, :], NT_DIMS, preferred_element_type=F32)
            if masked:
                row = lax.broadcasted_iota(jnp.int32, (nr, nk), 0) + r0
                col = lax.broadcasted_iota(jnp.int32, (nr, nk), 1)
                s = jnp.where(row >= col, s, NEG)
            chunks = nk // LANES
            mc = s[:, 0:LANES]
            for c in range(1, chunks):
                mc = jnp.maximum(mc, s[:, c * LANES:(c + 1) * LANES])
            m_prev = m_s[hh, rows, :]
            m_new = jnp.maximum(m_prev, jnp.max(mc, axis=1, keepdims=True))
            alpha = jnp.exp2(m_prev - m_new)
            pr = jnp.exp2(s - jnp.concatenate([m_new] * chunks, axis=1))
            ls = pr[:, 0:LANES]
            for c in range(1, chunks):
                ls = ls + pr[:, c * LANES:(c + 1) * LANES]
            l_s[hh, rows, :] = alpha * l_s[hh, rows, :] + ls
            acc_s[hh, rows, :] = alpha * acc_s[hh, rows, :] + jnp.dot(pr.astype(BF16), v_ref[hh, 0:nk, :],
                                                                      preferred_element_type=F32)
            m_s[hh, rows, :] = m_new

        def step(diagonal):
            for hh in range(hb):
                if diagonal:
                    block(hh, 0, t // 2, t // 2, True)
                    block(hh, t // 2, t // 2, t, True)
                else:
                    block(hh, 0, t, t, False)

        @pl.when(j < i)
        def _():
            step(False)

        @pl.when(j == i)
        def _():
            step(True)
            for hh in range(hb):
                l = jnp.sum(l_s[hh], axis=1, keepdims=True)
                o_ref[:, hh * V_HEAD:(hh + 1) * V_HEAD] = acc_s[hh] / l
                lse_ref[hh] = (m_s[hh] + jnp.log2(l)).T[0:1, :]

    grid_spec = pltpu.PrefetchScalarGridSpec(
        num_scalar_prefetch=2, grid=(H // hb, len(pairs)),
        in_specs=[pl.BlockSpec((hb, t, QK_PAD), lambda h, p, qi, ki: (h, qi[p], 0)),
                  pl.BlockSpec((hb, t, QK_PAD), lambda h, p, qi, ki: (h, ki[p], 0)),
                  pl.BlockSpec((hb, t, V_HEAD), lambda h, p, qi, ki: (h, ki[p], 0))],
        out_specs=[pl.BlockSpec((t, hb * V_HEAD), lambda h, p, qi, ki: (qi[p], h)),
                   pl.BlockSpec((hb, 1, t), lambda h, p, qi, ki: (h, 0, qi[p]))],
        scratch_shapes=[pltpu.VMEM((hb, t, LANES), F32), pltpu.VMEM((hb, t, LANES), F32),
                        pltpu.VMEM((hb, t, V_HEAD), F32)],
    )
    return pl.pallas_call(
        body, name="attn_fwd", grid_spec=grid_spec,
        out_shape=[jax.ShapeDtypeStruct((S, H * V_HEAD), F32), jax.ShapeDtypeStruct((H, 1, S), F32)],
        compiler_params=_params(("parallel", "arbitrary")),
    )(qi, ki, q, k, v)


def _attn_bwd(q, k, v, do, lse_row, delta_row, *, t):
    H, S, _ = q.shape
    n = S // t
    pairs = [(i, j) for j in range(n) for i in range(j, n)]
    qi = jnp.asarray(np.array([p[0] for p in pairs], np.int32))
    ki = jnp.asarray(np.array([p[1] for p in pairs], np.int32))

    def body(qi_ref, ki_ref, q_ref, k_ref, v_ref, do_ref, lse_ref, dl_ref, dq_ref, dk_ref, dv_ref, dk_s, dv_s, dq_s):
        p = pl.program_id(1)
        i, j = qi_ref[p], ki_ref[p]

        @pl.when(p == 0)
        def _():
            dq_s[...] = jnp.zeros_like(dq_s)

        def block(k0, nk, q0, nq, masked):
            qb, dob = q_ref[0, q0:q0 + nq, :], do_ref[q0:q0 + nq, :]
            kb, vb = k_ref[0, k0:k0 + nk, :], v_ref[0, k0:k0 + nk, :]
            st = lax.dot_general(kb, qb, NT_DIMS, preferred_element_type=F32)
            if masked:
                krow = lax.broadcasted_iota(jnp.int32, (nk, nq), 0) + k0
                qcol = lax.broadcasted_iota(jnp.int32, (nk, nq), 1) + q0
                st = jnp.where(krow <= qcol, st, NEG)
            pt = jnp.exp2(st - lse_ref[0][:, q0:q0 + nq])
            dvp = jnp.dot(pt.astype(BF16), dob, preferred_element_type=F32)
            dpt = lax.dot_general(vb, dob, NT_DIMS, preferred_element_type=F32)
            dst = (pt * (dpt - dl_ref[0][:, q0:q0 + nq])).astype(BF16)
            dkp = jnp.dot(dst, qb, preferred_element_type=F32)
            rows = pl.ds(pl.multiple_of(i * t + q0, LANES), nq)
            dq_s[rows, :] += lax.dot_general(dst, kb, TN_DIMS, preferred_element_type=F32)
            return dkp, dvp

        @pl.when(i == j)
        def _():
            half = t // 2
            dk_s[0:half, :], dv_s[0:half, :] = block(0, half, 0, t, True)
            dk_s[half:t, :], dv_s[half:t, :] = block(half, half, half, half, True)

        @pl.when(i != j)
        def _():
            dkp, dvp = block(0, t, 0, t, False)
            dk_s[...] += dkp
            dv_s[...] += dvp

        @pl.when(i == n - 1)
        def _():
            dk_ref[0] = (dk_s[...] * LN2).astype(BF16)
            dv_ref[0] = dv_s[...].astype(BF16)

        @pl.when(p == len(pairs) - 1)
        def _():
            dq_ref[0] = dq_s[...].astype(BF16)

    grid_spec = pltpu.PrefetchScalarGridSpec(
        num_scalar_prefetch=2, grid=(H, len(pairs)),
        in_specs=[pl.BlockSpec((1, t, QK_PAD), lambda h, p, qi, ki: (h, qi[p], 0)),
                  pl.BlockSpec((1, t, QK_PAD), lambda h, p, qi, ki: (h, ki[p], 0)),
                  pl.BlockSpec((1, t, V_HEAD), lambda h, p, qi, ki: (h, ki[p], 0)),
                  pl.BlockSpec((t, V_HEAD), lambda h, p, qi, ki: (qi[p], h)),
                  pl.BlockSpec((1, 1, t), lambda h, p, qi, ki: (h, 0, qi[p])),
                  pl.BlockSpec((1, 1, t), lambda h, p, qi, ki: (h, 0, qi[p]))],
        out_specs=[pl.BlockSpec((1, S, QK_PAD), lambda h, p, qi, ki: (h, 0, 0)),
                   pl.BlockSpec((1, t, QK_PAD), lambda h, p, qi, ki: (h, ki[p], 0)),
                   pl.BlockSpec((1, t, V_HEAD), lambda h, p, qi, ki: (h, ki[p], 0))],
        scratch_shapes=[pltpu.VMEM((t, QK_PAD), F32), pltpu.VMEM((t, V_HEAD), F32), pltpu.VMEM((S, QK_PAD), F32)],
    )
    return pl.pallas_call(
        body, name="attn_bwd", grid_spec=grid_spec,
        out_shape=[jax.ShapeDtypeStruct((H, S, QK_PAD), BF16), jax.ShapeDtypeStruct((H, S, QK_PAD), BF16),
                   jax.ShapeDtypeStruct((H, S, V_HEAD), BF16)],
        compiler_params=_params(("parallel", "arbitrary")),
    )(qi, ki, q, k, v, do, lse_row, delta_row)


def _merge_h1(h, z_main, o, x, w_out, norm_mlp, *, ts):
    S = h.shape[0]
    D = D_MODEL

    def body(h_ref, rg_ref, ga_ref, gb_ref, o_ref, x_ref, w_ref, g_ref, m_ref, h1_ref, n2_ref):
        gl, _ = _gelu_and_grad(rg_ref[...])
        m = (_sigmoid(ga_ref[...]) * (h_ref[...] * gl) + _sigmoid(gb_ref[...]) * o_ref[...]).astype(BF16)
        m_ref[...] = m
        h1 = x_ref[...] + jnp.dot(m, w_ref[...], preferred_element_type=F32)
        h1_ref[...] = h1
        n2, _ = _rms_fwd(h1, g_ref[...])
        n2_ref[...] = n2.astype(BF16)

    col = lambda c: pl.BlockSpec((ts, D), lambda i: (i, c))
    fixed = lambda shape: pl.BlockSpec(shape, lambda i: (0, 0), pipeline_mode=pl.Buffered(1))
    return pl.pallas_call(
        body, name="merge_h1", grid=(S // ts,),
        in_specs=[col(0), col(1), col(2), col(3), col(0), col(0), fixed((D, D)), fixed((1, D))],
        out_specs=[col(0), col(0), col(0)],
        out_shape=[jax.ShapeDtypeStruct((S, D), BF16), jax.ShapeDtypeStruct((S, D), F32),
                   jax.ShapeDtypeStruct((S, D), BF16)],
        compiler_params=_params(("parallel",)),
    )(h, z_main, z_main, z_main, o, x, w_out, norm_mlp)


def _my_place():
    return lax.axis_index("x"), lax.axis_index("y"), lax.axis_index("c")


def _all_gather(shards, *, name):
    n = len(shards)

    def body(*refs):
        x_refs, out_refs = refs[:n], refs[n:2 * n]
        send_sems, recv_sems, local_sems = refs[2 * n:]
        x, y, c = _my_place()
        me, sibling = (x, y, c), (x, y, 1 - c)
        chips = [(1 - x, y), (x, 1 - y), (1 - x, 1 - y)]

        def slot(a, px, py, pc):
            return out_refs[a].at[4 * px + 2 * py + pc]

        def copy(a, k, block, to, src=None):
            return pltpu.make_async_remote_copy(
                src_ref=slot(a, *block) if src is None else src, dst_ref=slot(a, *block),
                send_sem=send_sems.at[7 * a + k], recv_sem=recv_sems.at[7 * a + k], device_id=to, device_id_type=MESH)

        mine = [pltpu.make_async_copy(x_refs[a], slot(a, *me), local_sems.at[a]) for a in range(n)]
        for cp in mine:
            cp.start()
        first = []
        for a in range(n):
            first.append(copy(a, 0, me, sibling, src=x_refs[a]))
            first += [copy(a, 1 + j, me, (*chip, c), src=x_refs[a]) for j, chip in enumerate(chips)]
        for cp in first:
            cp.start()
        passed = []
        for a in range(n):
            for j, chip in enumerate(chips):
                copy(a, 1 + j, (*chip, c), me).wait_recv()
                fwd = copy(a, 4 + j, (*chip, c), sibling)
                fwd.start()
                passed.append(fwd)
        for a in range(n):
            copy(a, 0, sibling, me).wait_recv()
            for j, chip in enumerate(chips):
                copy(a, 4 + j, (*chip, 1 - c), me).wait_recv()
        for cp in first + passed:
            cp.wait_send()
        for cp in mine:
            cp.wait()

    hbm = pl.BlockSpec(memory_space=pl.ANY)
    return pl.pallas_call(
        body, name=name, out_shape=[jax.ShapeDtypeStruct((N_DEV, *s.shape), s.dtype) for s in shards],
        in_specs=[hbm] * n, out_specs=[hbm] * n,
        scratch_shapes=[pltpu.SemaphoreType.DMA((7 * n,)), pltpu.SemaphoreType.DMA((7 * n,)),
                        pltpu.SemaphoreType.DMA((n,))],
    )(*shards)


def _pushes(src_refs, land_refs, send_sems, recv_sems, slab_per_peer):
    x, y, c = _my_place()
    me = 4 * x + 2 * y + c
    copies = []
    for a in range(len(src_refs)):
        for k in range(1, N_DEV):
            px, py, pc = x ^ (k >> 2), y ^ ((k >> 1) & 1), c ^ (k & 1)
            src = src_refs[a].at[4 * px + 2 * py + pc] if slab_per_peer else src_refs[a]
            copies.append(pltpu.make_async_remote_copy(
                src_ref=src, dst_ref=land_refs[a].at[me], send_sem=send_sems.at[7 * a + k - 1],
                recv_sem=recv_sems.at[7 * a + k - 1], device_id=(px, py, pc), device_id_type=MESH))
    return copies


def _push_start(srcs, *, name, slab_per_peer):
    n = len(srcs)
    lands = [lax.empty((N_DEV, *(s.shape[1:] if slab_per_peer else s.shape)), s.dtype) for s in srcs]

    def body(*refs):
        src_refs, land_refs = refs[:n], refs[n:2 * n]
        send_sems, recv_sems, token = refs[2 * n], refs[2 * n + 1], refs[-1]
        for cp in _pushes(src_refs, land_refs, send_sems, recv_sems, slab_per_peer):
            cp.start()
        token[...] = jnp.zeros_like(token)

    hbm = pl.BlockSpec(memory_space=pltpu.HBM)
    sem = pl.BlockSpec(memory_space=pltpu.SEMAPHORE)
    out = pl.pallas_call(
        body, name=name,
        out_shape=(pltpu.SemaphoreType.DMA((7 * n,)), pltpu.SemaphoreType.DMA((7 * n,)),
                   *[pltpu.HBM(a.shape, a.dtype) for a in srcs + lands], jax.ShapeDtypeStruct((8, LANES), F32)),
        in_specs=[hbm] * (2 * n), out_specs=(sem, sem, *[hbm] * (2 * n), pl.BlockSpec(memory_space=pltpu.VMEM)),
        input_output_aliases={i: 2 + i for i in range(2 * n)},
        compiler_params=pltpu.CompilerParams(has_side_effects=pltpu.SideEffectType.DATAFLOW_SIDE_EFFECTING),
    )(*[pltpu.with_memory_space_constraint(a, pltpu.HBM) for a in srcs + lands])
    return out[0], out[1], list(out[2:2 + n]), list(out[2 + n:2 + 2 * n]), out[-1]


def _push_wait(send_sems, recv_sems, srcs, lands, after, *, name, slab_per_peer):
    n = len(srcs)

    def body(*refs):
        src_refs, land_refs = refs[:n], refs[n:2 * n]
        s_sems, r_sems = refs[2 * n], refs[2 * n + 1]
        for cp in _pushes(src_refs, land_refs, s_sems, r_sems, slab_per_peer):
            cp.wait_send()
            cp.wait_recv()

    hbm = pl.BlockSpec(memory_space=pltpu.HBM)
    sem = pl.BlockSpec(memory_space=pltpu.SEMAPHORE)
    out = pl.pallas_call(
        body, name=name, out_shape=tuple(pltpu.HBM(a.shape, a.dtype) for a in srcs + lands),
        in_specs=[hbm] * (2 * n) + [sem, sem, pl.BlockSpec(memory_space=pl.ANY)], out_specs=tuple([hbm] * (2 * n)),
        input_output_aliases={i: i for i in range(2 * n)},
        compiler_params=pltpu.CompilerParams(has_side_effects=pltpu.SideEffectType.DATAFLOW_SIDE_EFFECTING),
    )(*srcs, *lands, send_sems, recv_sems, after)
    return list(out[:n]), list(out[n:])


def _sum_parts(gp_ref, rows):
    g = gp_ref[0, 0:rows, :].astype(F32)
    for p in range(1, gp_ref.shape[0]):
        g = g + gp_ref[p, 0:rows, :].astype(F32)
    return g


def _adamw_update(w, m, v, g):
    m_new = ADAM_B1 * m + (1.0 - ADAM_B1) * g
    v_new = ADAM_B2 * v + (1.0 - ADAM_B2) * (g * g)
    m_hat = m_new / (1.0 - ADAM_B1 ** ADAM_STEP)
    v_hat = v_new / (1.0 - ADAM_B2 ** ADAM_STEP)
    return -ADAM_LR * (m_hat / (jnp.sqrt(v_hat) + ADAM_EPS) + ADAM_WD * w), m_new, v_new


def _adamw_many(ws, ms, vs, gparts, sums, *, name):
    n, k = len(ws), len(sums)

    def body(*refs):
        w_refs, m_refs, v_refs = refs[:n], refs[n:2 * n], refs[2 * n:3 * n]
        g_refs, s_refs, outs = refs[3 * n:4 * n], refs[4 * n:4 * n + k], refs[4 * n + k:]
        for a in range(n):
            g = _sum_parts(g_refs[a], w_refs[a].shape[0])
            d, m_new, v_new = _adamw_update(w_refs[a][...], m_refs[a][...], v_refs[a][...], g)
            for o_ref, val in zip(outs[4 * a:4 * a + 4], (g, d, m_new, v_new)):
                o_ref[...] = val
        for b in range(k):
            outs[4 * n + b][...] = _sum_parts(s_refs[b], s_refs[b].shape[1])

    out_shape = [jax.ShapeDtypeStruct(w.shape, F32) for w in ws for _ in range(4)]
    out_shape += [jax.ShapeDtypeStruct(s.shape[1:], F32) for s in sums]
    return pl.pallas_call(body, name=name, out_shape=out_shape, compiler_params=_params())(
        *ws, *ms, *vs, *gparts, *sums)


def _adamw(w, m, v, gparts, *, tr, name):
    R, C = w.shape
    n_parts = gparts.shape[0]

    def body(w_ref, m_ref, v_ref, gp_ref, g_ref, d_ref, nm_ref, nv_ref):
        g = _sum_parts(gp_ref, tr)
        d_ref[...], nm_ref[...], nv_ref[...] = _adamw_update(w_ref[...], m_ref[...], v_ref[...], g)
        g_ref[...] = g

    row = pl.BlockSpec((tr, C), lambda i: (i, 0))
    shp = jax.ShapeDtypeStruct((R, C), F32)
    return pl.pallas_call(
        body, name=name, grid=(R // tr,),
        in_specs=[row, row, row, pl.BlockSpec((n_parts, tr, C), lambda i: (0, i, 0))],
        out_specs=[row, row, row, row], out_shape=[shp, shp, shp, shp],
        compiler_params=_params(("parallel",)),
    )(w, m, v, gparts)


def _rope_tables(s):
    pos = jnp.arange(s, dtype=F32)
    inv_freq = 1.0 / (ROPE_THETA ** (jnp.arange(0, QK_ROPE, 2, dtype=F32) / QK_ROPE))
    ang = pos[:, None] * inv_freq[None, :]
    cos, sin = jnp.cos(ang), jnp.sin(ang)
    zero = jnp.zeros((s, LANES - QK_ROPE), F32)
    return jnp.concatenate([cos, cos, zero], -1), jnp.concatenate([sin, sin, zero], -1)


def _pick(n, want):
    t = min(n, want)
    assert n % t == 0
    return t


def _local_step(x, target, wts, small, hooks):
    S = x.shape[0]
    H = MLA_HEADS
    ts = _pick(S, 512)
    tm = _pick(S, 512)
    tm_wide = _pick(S, 1024)
    tk_s = _pick(S, 2048)
    row = lambda v: v.reshape(1, -1)
    w_in = wts["w_in"]
    w_main = jnp.concatenate([w_in[:, 0:2048], w_in[:, 2624:4672]], axis=1)
    w_ckv = jnp.concatenate([w_in[:, 2048:2624], jnp.zeros((D_MODEL, CKV_W - 576), BF16)], axis=1)
    w_uq3 = wts["w_uq"].reshape(Q_LORA, H, QK_NOPE + QK_ROPE)
    w_uq_p = jnp.concatenate(
        [w_uq3[:, :, :QK_NOPE].reshape(Q_LORA, H * QK_NOPE),
         jnp.pad(w_uq3[:, :, QK_NOPE:], ((0, 0), (0, 0), (0, LANES - QK_ROPE))).reshape(Q_LORA, H * LANES)], axis=1)
    w_ukv = wts["w_ukv"]
    cos, sin = _rope_tables(S)
    conv_w, conv_b = small["conv_w"], row(small["conv_b"])
    wa, wx = small["lru_wa"].astype(BF16), small["lru_wx"].astype(BF16)
    wat, wxt = jnp.swapaxes(wa, 1, 2), jnp.swapaxes(wx, 1, 2)
    ba, bx = small["lru_ba"].reshape(RNN_BLOCKS, 1, RNN_BLOCK_W), small["lru_bx"].reshape(RNN_BLOCKS, 1, RNN_BLOCK_W)
    lam = row(small["lru_lambda"])
    q_norm, kv_norm = row(small["q_norm"]), row(small["kv_norm"])
    norm_mix, norm_mlp, norm_final = row(small["norm_mix"]), row(small["norm_mlp"]), row(small["norm_final"])

    xn = _rmsnorm_cast(x, norm_mix, ts=ts, name="norm_mix")
    ident = lambda acc: (acc,)
    (z_main,) = _mm(xn, w_main, name="z_main", tm=tm_wide, tn=1024, tk=1024, outs=[("tile", F32)], epilogue=ident)
    (z_ckv,) = _mm(xn, w_ckv, name="z_ckv", tm=tm, tn=CKV_W, tk=1024, outs=[("tile", F32)], epilogue=ident)
    tt = _pick(S, 256)
    h = _lru_fwd(z_main, conv_w, conv_b, wa, ba, wx, bx, lam, tt=tt)
    q, k, v = _mla_proj(z_ckv, q_norm, kv_norm, w_uq_p, w_ukv, cos, sin, ts=_pick(S, 256))
    ta = _pick(S, 1024)
    o, lse = _attn_fwd(q, k, v, t=ta, hb=2)
    w_out, w_up, w_down = hooks["weights_later"](o)
    merged, h1, n2 = _merge_h1(h, z_main, o, x, w_out, norm_mlp, ts=_pick(S, 256))

    def ep_up(acc):
        r = jnp.maximum(acc, 0.0)
        return r * r, r

    act, relu = _mm(n2, w_up, name="up", tm=tm_wide, tn=1024, tk=1024, outs=[("tile", BF16), ("tile", BF16)],
                    epilogue=ep_up)

    def ep_loss(acc, h1v, tgt, g):
        h2 = acc + h1v
        y, _ = _rms_fwd(h2, g)
        err = y - tgt
        loss_rows = 0.5 * jnp.mean(err * err, axis=-1, keepdims=True)
        dy = err * (1.0 / D_MODEL)
        dh2, dg_rows = _rms_bwd(dy, h2, g)
        lsum = jnp.sum(loss_rows, axis=0, keepdims=True)
        return dh2, dh2, jnp.sum(dg_rows, axis=0, keepdims=True), jnp.broadcast_to(lsum, (1, D_MODEL))

    dh2, dh2b, dnf_p, loss_p = _mm(
        act, w_down, name="down_loss", tm=tm, tn=1024, tk=D_FF,
        outs=[("tile", F32), ("tile", BF16), ("rowpart", F32), ("rowpart", F32)], epilogue=ep_loss,
        extras=[("tile", h1), ("tile", target), ("row", norm_final)])
    loss_part = jnp.sum(loss_p[:, 0, 0])
    d_norm_final = jnp.sum(dnf_p, axis=(0, 1))

    def ep_du(acc, r):
        return (acc * (2.0 * r.astype(F32)),)

    (du,) = _mm(dh2b, w_down, name="d_act", tb=True, tm=tm_wide, tn=1024, tk=1024, outs=[("tile", BF16)], epilogue=ep_du,
                extras=[("tile", relu)])

    def ep_dh1(acc, h1v, dh2v, g):
        dv, dg_rows = _rms_bwd(acc, h1v, g)
        dh1 = dh2v + dv
        return dh1, dh1, jnp.sum(dg_rows, axis=0, keepdims=True)

    dh1, dh1b, dnm_p = _mm(du, w_up, name="d_n2", tb=True, tm=tm, tn=1024, tk=D_FF,
                           outs=[("tile", F32), ("tile", BF16), ("rowpart", F32)], epilogue=ep_dh1,
                           extras=[("tile", h1), ("tile", dh2), ("row", norm_mlp)])
    d_norm_mlp = jnp.sum(dnm_p, axis=(0, 1))
    tn_mm = functools.partial(_mm, ta=True, tk=tk_s, outs=[("tile", BF16)], epilogue=ident)
    (d_w_down,) = tn_mm(act, dh2b, name="dw_down", tm=1024, tn=1024)
    (p_w_up,) = _mm(n2, du, name="dw_up", ta=True, tk=tk_s, tm=1024, tn=D_FF // N_DEV, outs=[("colshard", BF16)],
                    epilogue=ident)
    (d_w_out,) = tn_mm(merged, dh1b, name="dw_out", tm=1024, tn=1024)
    early = [d_w_out.reshape(N_DEV, -1, D_MODEL), p_w_up, d_w_down.reshape(N_DEV, -1, D_MODEL)]
    w_out = w_out + hooks["send"]("early", early)[0, 0].astype(BF16)

    tmm = _pick(S, 256)

    def ep_dmerge(dm, hv, rg, ga, gb, ov):
        gl, dgl = _gelu_and_grad(rg)
        sa, sb = _sigmoid(ga), _sigmoid(gb)
        ya = hv * gl
        dya = dm * sa
        do = dm * sb
        dga = dm * ya * sa * (1.0 - sa)
        dgb = dm * ov * sb * (1.0 - sb)
        dh = dya * gl
        drg = dya * hv * dgl
        dov = do * ov
        lane = lax.broadcasted_iota(jnp.int32, (dm.shape[0], LANES), 1)
        delta = jnp.zeros((dm.shape[0], LANES), F32)
        for hh in range(H):
            dsum = jnp.sum(dov[:, hh * V_HEAD:(hh + 1) * V_HEAD], axis=1, keepdims=True)
            delta = jnp.where(lane == hh, dsum, delta)
        return dh, jnp.concatenate([drg, dga, dgb], axis=1), do, delta

    dh_lru, dz_part, do, delta_w = _mm(
        dh1b, w_out, name="d_merge", tb=True, tm=tmm, tn=1024, tk=1024,
        outs=[("tile", F32), ("cols", BF16, 4 * D_MODEL, D_MODEL), ("tile", BF16), ("side", F32)],
        epilogue=ep_dmerge,
        extras=[("tile", h), ("tilecol", z_main, 1), ("tilecol", z_main, 2), ("tilecol", z_main, 3), ("tile", o)])
    delta_row = delta_w[:, :H].T.reshape(H, 1, S)
    lse_row = lse

    dq, dk, dv = _attn_bwd(q, k, v, do, lse_row, delta_row, t=ta)
    dz_ckv, d_w_uq_p, d_w_ukv, d_q_norm, d_kv_norm = _mla_proj_bwd(
        z_ckv, dq, dk, dv, q_norm, kv_norm, w_uq_p.T, w_ukv.T, cos, sin, ts=_pick(S, 256))
    d_w_uq = jnp.concatenate(
        [d_w_uq_p[:, :H * QK_NOPE].reshape(Q_LORA, H, QK_NOPE),
         d_w_uq_p[:, H * QK_NOPE:].reshape(Q_LORA, H, LANES)[:, :, :QK_ROPE]], axis=2).reshape(Q_LORA, -1)

    dz_main, d_wa, d_wx, d_ba, d_bx, d_lam, d_conv_w, d_conv_b = _lru_bwd(
        z_main, h, dh_lru, dz_part, conv_w, conv_b, wa, wat, ba, wx, wxt, bx, lam, tt=tt)

    (d_w_main,) = tn_mm(xn, dz_main, name="dw_main", tm=1024, tn=1024)
    (d_w_ckv,) = tn_mm(xn, dz_ckv, name="dw_ckv", tm=1024, tn=CKV_W)
    d_w_in = jnp.concatenate([d_w_main[:, 0:2048], d_w_ckv[:, 0:576], d_w_main[:, 2048:4096]], axis=1)

    def col_parts(full):
        r = full.shape[0]
        return jnp.transpose(full.astype(BF16).reshape(r, N_DEV, -1), (1, 0, 2))

    late = [col_parts(d_w_in), col_parts(d_w_uq), col_parts(d_w_ukv)]
    norm_mix = norm_mix + hooks["send"]("late", late)[0, 0]
    (dxn_ckv,) = _mm(dz_ckv, w_ckv, name="dxn_ckv", tb=True, tm=tm, tn=1024, tk=CKV_W, outs=[("tile", F32)],
                     epilogue=ident)

    def ep_dx(acc, part, xv, dh1v, g):
        dv, dg_rows = _rms_bwd(acc + part, xv, g)
        return dh1v + dv, jnp.sum(dg_rows, axis=0, keepdims=True)

    grad_x, dnx_p = _mm(dz_main, w_main, name="dx", tb=True, tm=tm, tn=1024, tk=4 * D_MODEL,
                        outs=[("tile", F32), ("rowpart", F32)], epilogue=ep_dx,
                        extras=[("tile", dxn_ckv), ("tile", x), ("tile", dh1), ("row", norm_mix)])
    d_norm_mix = jnp.sum(dnx_p, axis=(0, 1))
    sm = {"norm_mix": d_norm_mix, "conv_w": d_conv_w, "conv_b": d_conv_b.reshape(-1), "lru_wa": d_wa,
          "lru_ba": d_ba.reshape(RNN_BLOCKS, RNN_BLOCK_W), "lru_wx": d_wx, "lru_bx": d_bx.reshape(RNN_BLOCKS, RNN_BLOCK_W),
          "lru_lambda": d_lam.reshape(-1), "q_norm": d_q_norm.reshape(-1), "kv_norm": d_kv_norm.reshape(-1),
          "norm_mlp": d_norm_mlp, "norm_final": d_norm_final}
    return loss_part, grad_x, sm


BIG = ("w_in", "w_uq", "w_ukv", "w_out", "w_up", "w_down")
SMALL = ("norm_mix", "conv_b", "lru_wa", "lru_ba", "lru_wx", "lru_bx", "lru_lambda", "q_norm", "kv_norm", "norm_mlp",
         "norm_final")
WEIGHTS = ("norm_mix", "w_in", "conv_w", "conv_b", "lru_wa", "lru_ba", "lru_wx", "lru_bx", "lru_lambda", "q_norm", "w_uq",
           "kv_norm", "w_ukv", "w_out", "norm_mlp", "w_up", "w_down", "norm_final")
ADAM_TILE_ROWS = {"w_in": 256, "w_uq": 128, "w_ukv": 128, "w_out": 64, "w_up": 256, "w_down": 128}
CONV_ROWS = N_DEV * 8


def _rows(a):
    return a.reshape(-1, LANES)


def _pad_rows(a, mult):
    r = a.shape[-2]
    pad = (-r) % mult
    if pad == 0:
        return a
    cfg = [(0, 0)] * (a.ndim - 2) + [(0, pad), (0, 0)]
    return jnp.pad(a, cfg)


def _cols_from_shards(g):
    return jnp.transpose(g, (1, 0, 2)).reshape(g.shape[1], -1)


def kernel(x, norm_mix, w_in, conv_w, conv_b, lru_wa, lru_ba, lru_wx, lru_bx, lru_lambda, q_norm, w_uq, kv_norm, w_ukv, w_out, norm_mlp, w_up, w_down, norm_final, loss_target, m_norm_mix, m_w_in, m_conv_w, m_conv_b, m_lru_wa, m_lru_ba, m_lru_wx, m_lru_bx, m_lru_lambda, m_q_norm, m_w_uq, m_kv_norm, m_w_ukv, m_w_out, m_norm_mlp, m_w_up, m_w_down, m_norm_final, v_norm_mix, v_w_in, v_conv_w, v_conv_b, v_lru_wa, v_lru_ba, v_lru_wx, v_lru_bx, v_lru_lambda, v_q_norm, v_w_uq, v_kv_norm, v_w_ukv, v_w_out, v_norm_mlp, v_w_up, v_w_down, v_norm_final):
    W = dict(norm_mix=norm_mix, w_in=w_in, conv_w=conv_w, conv_b=conv_b, lru_wa=lru_wa, lru_ba=lru_ba, lru_wx=lru_wx,
             lru_bx=lru_bx, lru_lambda=lru_lambda, q_norm=q_norm, w_uq=w_uq, kv_norm=kv_norm, w_ukv=w_ukv, w_out=w_out,
             norm_mlp=norm_mlp, w_up=w_up, w_down=w_down, norm_final=norm_final)
    M = dict(norm_mix=m_norm_mix, w_in=m_w_in, conv_w=m_conv_w, conv_b=m_conv_b, lru_wa=m_lru_wa, lru_ba=m_lru_ba,
             lru_wx=m_lru_wx, lru_bx=m_lru_bx, lru_lambda=m_lru_lambda, q_norm=m_q_norm, w_uq=m_w_uq, kv_norm=m_kv_norm,
             w_ukv=m_w_ukv, w_out=m_w_out, norm_mlp=m_norm_mlp, w_up=m_w_up, w_down=m_w_down, norm_final=m_norm_final)
    V = dict(norm_mix=v_norm_mix, w_in=v_w_in, conv_w=v_conv_w, conv_b=v_conv_b, lru_wa=v_lru_wa, lru_ba=v_lru_ba,
             lru_wx=v_lru_wx, lru_bx=v_lru_bx, lru_lambda=v_lru_lambda, q_norm=v_q_norm, w_uq=v_w_uq, kv_norm=v_kv_norm,
             w_ukv=v_w_ukv, w_out=v_w_out, norm_mlp=v_norm_mlp, w_up=v_w_up, w_down=v_w_down, norm_final=v_norm_final)
    me = 4 * lax.axis_index("x") + 2 * lax.axis_index("y") + lax.axis_index("c")

    first, later = ("w_in", "w_uq", "w_ukv"), ("w_out", "w_up", "w_down")
    got = _all_gather([W[n].astype(BF16) for n in first] + [_pad_rows(conv_w, 8)], name="gather_weights")
    wts = {"w_in": _cols_from_shards(got[0]), "w_uq": _cols_from_shards(got[1]), "w_ukv": _cols_from_shards(got[2])}
    w_send, w_recv, w_src, w_land, zeros = _push_start([W[n].astype(BF16) for n in later], name="gather_later_start",
                                                       slab_per_peer=False)
    small = {n: W[n] for n in SMALL}
    small["conv_w"] = _cols_from_shards(got[3][:, :CONV_WIDTH])
    small["norm_mix"] = norm_mix + zeros[0, 0]

    def with_own_slab(land, mine):
        return lax.dynamic_update_slice(land, mine, (me, 0, 0))

    def weights_later(after):
        srcs, lands = _push_wait(w_send, w_recv, w_src, w_land, after, name="gather_later_wait", slab_per_peer=False)
        w_out_g, w_up_g, w_down_g = [with_own_slab(l, s[None]) for l, s in zip(lands, srcs)]
        return w_out_g.reshape(-1, D_MODEL), _cols_from_shards(w_up_g), w_down_g.reshape(-1, D_MODEL)

    sent = {}
    G, Dl, NM, NV = {}, {}, {}, {}

    def finish(group, names, after):
        s_sems, r_sems, srcs, lands, _ = sent[group]
        srcs, lands = _push_wait(s_sems, r_sems, srcs, lands, after, name="exchange_" + group + "_wait",
                                 slab_per_peer=True)
        for n, src, land in zip(names, srcs, lands):
            parts = with_own_slab(land, lax.dynamic_slice(src, (me, 0, 0), (1, *src.shape[1:])))
            G[n], Dl[n], NM[n], NV[n] = _adamw(W[n], M[n], V[n], parts, tr=ADAM_TILE_ROWS[n], name="adamw_" + n)

    def send(group, parts):
        sent[group] = _push_start(parts, name="exchange_" + group + "_start", slab_per_peer=True)
        zeros = sent[group][4]
        if group == "late":
            finish("early", later, zeros)
            zeros = zeros + 0.0 * (Dl["w_out"][0:8, 0:LANES] + Dl["w_up"][0:8, 0:LANES] + Dl["w_down"][0:8, 0:LANES])
        return zeros

    loss_part, grad_x, g_small = _local_step(x[0], loss_target[0], wts, small,
                                              {"weights_later": weights_later, "send": send})
    finish("late", first, grad_x)

    conv_rows = _pad_rows(jnp.transpose(g_small["conv_w"].reshape(CONV_WIDTH, N_DEV, LANES), (1, 0, 2)), 8)
    loss_rows = jnp.zeros((8, LANES), F32).at[0, 0].set(loss_part)
    gathered = _all_gather([_pad_rows(_rows(g_small[n]), 8) for n in SMALL]
                           + [conv_rows.reshape(CONV_ROWS, LANES), loss_rows], name="gather_small")
    k = len(SMALL)
    outs = _adamw_many([_rows(W[n]) for n in SMALL], [_rows(M[n]) for n in SMALL], [_rows(V[n]) for n in SMALL],
                       gathered[:k], gathered[k:], name="adamw_small")
    for j, n in enumerate(SMALL):
        for out, o in zip((G, Dl, NM, NV), outs[4 * j:4 * j + 4]):
            out[n] = o.reshape(W[n].shape)
    conv_sum, loss_sum = outs[4 * k:]
    loss = loss_sum[0, 0]

    g_conv = lax.dynamic_slice(conv_sum, (me * 8, 0), (8, LANES))
    conv_out = _adamw(_pad_rows(conv_w, 8), _pad_rows(m_conv_w, 8), _pad_rows(v_conv_w, 8), g_conv[None], tr=8,
                      name="adamw_conv_w")
    for out, pk in zip((G, Dl, NM, NV), conv_out):
        out["conv_w"] = pk[:CONV_WIDTH]
    return (loss, grad_x[None], *[G[n] for n in WEIGHTS], *[Dl[n] for n in WEIGHTS], *[NM[n] for n in WEIGHTS],
            *[NV[n] for n in WEIGHTS])
```

```python
import functools

import numpy as np
import jax
import jax.numpy as jnp
from jax import lax
from jax.experimental import pallas as pl
from jax.experimental.pallas import tpu as pltpu

F32 = jnp.float32
BF16 = jnp.bfloat16
MESH = pl.DeviceIdType.MESH

D_MODEL = 1024
N_DEV = 8
LANES = 128
RNN_BLOCKS = 8
RNN_BLOCK_W = 128
CONV_WIDTH = 4
LRU_C = 8.0
MLA_HEADS = 8
QK_NOPE = 128
QK_ROPE = 64
V_HEAD = 128
QK_PAD = 256
Q_LORA = 256
KV_LORA = 256
CKV_W = 640
ROPE_THETA = 10000.0
D_FF = 4096
EPS = 1e-6
ATTN_SCALE = (QK_NOPE + QK_ROPE) ** -0.5
LOG2E = 1.4426950408889634
LN2 = 0.6931471805599453
NEG = -1e30

ADAM_LR = 0.001
ADAM_B1 = 0.9
ADAM_B2 = 0.999
ADAM_EPS = 1e-08
ADAM_WD = 0.01
ADAM_STEP = 10

VMEM_LIMIT = 56 * 1024 * 1024


def _params(sem=None):
    return pltpu.CompilerParams(dimension_semantics=sem, vmem_limit_bytes=VMEM_LIMIT)


def _sigmoid(v):
    return 1.0 / (1.0 + jnp.exp(-v))


def _softplus(y):
    e = jnp.exp(-jnp.abs(y))
    u = 1.0 + e
    d = u - 1.0
    l1p = jnp.where(d == 0.0, e, jnp.log(u) * e / jnp.where(d == 0.0, 1.0, d))
    return jnp.maximum(y, 0.0) + l1p


_GELU_K = 0.7978845608028654
_GELU_C = 0.044715


def _gelu_and_grad(v):
    t = jnp.tanh(_GELU_K * (v + _GELU_C * v * v * v))
    g = 0.5 * v * (1.0 + t)
    dg = 0.5 * (1.0 + t) + 0.5 * v * (1.0 - t * t) * _GELU_K * (1.0 + 3.0 * _GELU_C * v * v)
    return g, dg


def _rms_fwd(v, g):
    rstd = lax.rsqrt(jnp.mean(v * v, axis=-1, keepdims=True) + EPS)
    return v * rstd * g, rstd


def _rms_bwd(dy, v, g):
    rstd = lax.rsqrt(jnp.mean(v * v, axis=-1, keepdims=True) + EPS)
    vh = v * rstd
    dvh = dy * g
    dv = rstd * (dvh - vh * jnp.mean(dvh * vh, axis=-1, keepdims=True))
    return dv, dy * vh


def _shift_down(v, s, fill, row):
    return jnp.where(row >= s, pltpu.roll(v, s, 0), fill)


def _shift_up(v, s, fill, row, n):
    return jnp.where(row < n - s, pltpu.roll(v, n - s, 0), fill)


def _rot_half(v, lane):
    n = v.shape[-1]
    l = lane & (LANES - 1)
    up = pltpu.roll(v, n - QK_ROPE // 2, 1)
    dn = pltpu.roll(v, QK_ROPE // 2, 1)
    return jnp.where(l < QK_ROPE // 2, -up, jnp.where(l < QK_ROPE, dn, 0.0))


def _mm(a, b, *, name, tm, tn, tk, outs, epilogue, extras=(), ta=False, tb=False, more=None):
    assert not (ta and tb)
    if ta:
        K, M = a.shape
    else:
        M, K = a.shape
    if tb:
        N, K2 = b.shape
    else:
        K2, N = b.shape
    assert K == K2 and M % tm == 0 and N % tn == 0 and K % tk == 0, (name, a.shape, b.shape)
    n_i, n_j, n_k = M // tm, N // tn, K // tk
    n_ex, n_out = len(extras), len(outs)
    n_more = 0 if more is None else 2
    assert more is None or (n_k == 1 and not ta)

    def body(*refs):
        a_ref, b_ref = refs[0], refs[1]
        ex_refs = refs[2 + n_more:2 + n_more + n_ex]
        out_refs = refs[2 + n_more + n_ex:2 + n_more + n_ex + n_out]
        if ta:
            part = lax.dot_general(a_ref[...], b_ref[...], (((0,), (0,)), ((), ())), preferred_element_type=F32)
        elif tb:
            part = lax.dot_general(a_ref[...], b_ref[...], (((1,), (1,)), ((), ())), preferred_element_type=F32)
        else:
            part = jnp.dot(a_ref[...], b_ref[...], preferred_element_type=F32)
        if more is not None:
            part = part + lax.dot_general(refs[2][...], refs[3][...], (((1,), (1,)), ((), ())),
                                          preferred_element_type=F32)

        def finish(acc):
            res = epilogue(acc, *[r[...] for r in ex_refs])
            for o_ref, r, spec in zip(out_refs, res, outs):
                if spec[0] == "cols":
                    o_ref[:, spec[3]:spec[3] + r.shape[1]] = r.astype(o_ref.dtype)
                else:
                    o_ref[...] = r.astype(o_ref.dtype).reshape(o_ref.shape)

        if n_k == 1:
            finish(part)
        else:
            acc_ref = refs[-1]
            k = pl.program_id(2)

            @pl.when(k == 0)
            def _():
                acc_ref[...] = part

            @pl.when(k > 0)
            def _():
                acc_ref[...] += part

            @pl.when(k == n_k - 1)
            def _():
                finish(acc_ref[...])

    a_spec = pl.BlockSpec((tk, tm), lambda j, i, k: (k, i)) if ta else pl.BlockSpec((tm, tk), lambda j, i, k: (i, k))
    b_once = dict(pipeline_mode=pl.Buffered(1)) if (n_j == 1 and n_k == 1) else {}
    if tb:
        in_specs = [a_spec, pl.BlockSpec((tn, tk), lambda j, i, k: (j, k), **b_once)]
    else:
        in_specs = [a_spec, pl.BlockSpec((tk, tn), lambda j, i, k: (k, j), **b_once)]
    if more is not None:
        k2 = more[0].shape[1]
        in_specs += [pl.BlockSpec((tm, k2), lambda j, i, k: (i, 0)), pl.BlockSpec((tn, k2), lambda j, i, k: (j, 0), **b_once)]
    for ex in extras:
        kind = ex[0]
        if kind == "tile":
            in_specs.append(pl.BlockSpec((tm, tn), lambda j, i, k: (i, j)))
        elif kind == "tilecol":
            assert n_j == 1
            in_specs.append(pl.BlockSpec((tm, tn), functools.partial(lambda c, j, i, k: (i, c), ex[2])))
        else:
            in_specs.append(pl.BlockSpec((1, tn), lambda j, i, k: (0, j)))
    out_specs, out_shape = [], []
    for kind, dt, *rest in outs:
        if kind == "tile":
            out_specs.append(pl.BlockSpec((tm, tn), lambda j, i, k: (i, j)))
            out_shape.append(jax.ShapeDtypeStruct((M, N), dt))
        elif kind == "colshard":
            out_specs.append(pl.BlockSpec((1, tm, tn), lambda j, i, k: (j, i, 0)))
            out_shape.append(jax.ShapeDtypeStruct((n_j, M, tn), dt))
        elif kind == "cols":
            assert n_j == 1
            out_specs.append(pl.BlockSpec((tm, rest[0]), lambda j, i, k: (i, 0)))
            out_shape.append(jax.ShapeDtypeStruct((M, rest[0]), dt))
        elif kind == "side":
            assert n_j == 1
            out_specs.append(pl.BlockSpec((tm, LANES), lambda j, i, k: (i, 0)))
            out_shape.append(jax.ShapeDtypeStruct((M, LANES), dt))
        else:
            out_specs.append(pl.BlockSpec((1, 1, tn), lambda j, i, k: (i, 0, j)))
            out_shape.append(jax.ShapeDtypeStruct((n_i, 1, N), dt))
    scratch = [pltpu.VMEM((tm, tn), F32)] if n_k > 1 else []
    return pl.pallas_call(
        body, name=name, grid=(n_j, n_i, n_k), in_specs=in_specs, out_specs=out_specs, out_shape=out_shape,
        scratch_shapes=scratch, compiler_params=_params(("parallel", "parallel", "arbitrary")),
    )(a, b, *(more or ()), *[ex[1] for ex in extras])


def _rmsnorm_cast(x, g, *, ts, name):
    S, D = x.shape

    def body(x_ref, g_ref, o_ref):
        y, _ = _rms_fwd(x_ref[...], g_ref[...])
        o_ref[...] = y.astype(BF16)

    return pl.pallas_call(
        body, name=name, grid=(S // ts,),
        in_specs=[pl.BlockSpec((ts, D), lambda i: (i, 0)), pl.BlockSpec((1, D), lambda i: (0, 0))],
        out_specs=pl.BlockSpec((ts, D), lambda i: (i, 0)), out_shape=jax.ShapeDtypeStruct((S, D), BF16),
        compiler_params=_params(("parallel",)),
    )(x, g)


LRU_NB = 4


def _lru_gates(xa, wa_ref, ba_ref, wx_ref, bx_ref, lam):
    xab = xa.astype(BF16)
    W = RNN_BLOCK_W
    rs, is_ = [], []
    for j in range(LRU_NB):
        xj = xab[:, j * W:(j + 1) * W]
        rs.append(_sigmoid(jnp.dot(xj, wa_ref[j], preferred_element_type=F32) + ba_ref[j]))
        is_.append(_sigmoid(jnp.dot(xj, wx_ref[j], preferred_element_type=F32) + bx_ref[j]))
    r = jnp.concatenate(rs, axis=1)
    i = jnp.concatenate(is_, axis=1)
    sp = _softplus(-lam)
    log_a = (-LRU_C * r) * sp
    a = jnp.exp(log_a)
    y = 2.0 * log_a
    one_m = jnp.where(y > -0.01, -y * (1.0 + 0.5 * y * (1.0 + y * (1.0 / 3.0))), 1.0 - a * a)
    return r, i, sp, a, jnp.sqrt(one_m)


def _rows_before(x, tail8, k):
    e16 = jnp.concatenate([tail8, x[0:8, :]], axis=0)
    return jnp.concatenate([pltpu.roll(e16, k, 0)[8:16, :], pltpu.roll(x, k, 0)[8:, :]], axis=0)


def _rows_after(x, head8, k):
    tt = x.shape[0]
    e16 = jnp.concatenate([x[tt - 8:tt, :], head8], axis=0)
    return jnp.concatenate([pltpu.roll(x, tt - k, 0)[:tt - 8, :], pltpu.roll(e16, 16 - k, 0)[0:8, :]], axis=0)


def _scan_down(a, b, h0, a_s, b_s, c_s):
    tt, C = a.shape
    G, nch = tt // 8, C // LANES
    rin = lax.broadcasted_iota(jnp.int32, (tt, C), 0) & 7

    def in_group(v, s):
        return pltpu.roll(v.reshape(G, 8, C), s, 1).reshape(tt, C)

    A, B = a, b
    for s in (1, 2, 4):
        B = A * jnp.where(rin >= s, in_group(B, s), 0.0) + B
        A = A * jnp.where(rin >= s, in_group(A, s), 1.0)
    for j in range(nch):
        a_s[j] = A[:, j * LANES:(j + 1) * LANES]
        b_s[j] = B[:, j * LANES:(j + 1) * LANES]
    At = jnp.concatenate([a_s.at[j][pl.ds(7, G, stride=8), :] for j in range(nch)], axis=1)
    Bt = jnp.concatenate([b_s.at[j][pl.ds(7, G, stride=8), :] for j in range(nch)], axis=1)
    rowg = lax.broadcasted_iota(jnp.int32, (G, C), 0)
    s = 1
    while s < G:
        Bt = At * _shift_down(Bt, s, 0.0, rowg) + Bt
        At = At * _shift_down(At, s, 1.0, rowg)
        s *= 2
    hg = At * h0 + Bt
    cin = _shift_down(hg, 1, h0, rowg)
    for j in range(nch):
        for r in range(8):
            c_s.at[j][pl.ds(r, G, stride=8), :] = cin[:, j * LANES:(j + 1) * LANES]
    return A * jnp.concatenate([c_s[j] for j in range(nch)], axis=1) + B, hg[G - 1:G, :]


def _scan_up(c, g_in, g_next, a_s, b_s, c_s):
    tt, C = c.shape
    G, nch = tt // 8, C // LANES
    rin = lax.broadcasted_iota(jnp.int32, (tt, C), 0) & 7

    def in_group(v, s):
        return pltpu.roll(v.reshape(G, 8, C), 8 - s, 1).reshape(tt, C)

    Cc, Gv = c, g_in
    for s in (1, 2, 4):
        Gv = Gv + Cc * jnp.where(rin < 8 - s, in_group(Gv, s), 0.0)
        Cc = Cc * jnp.where(rin < 8 - s, in_group(Cc, s), 1.0)
    for j in range(nch):
        a_s[j] = Cc[:, j * LANES:(j + 1) * LANES]
        b_s[j] = Gv[:, j * LANES:(j + 1) * LANES]
    Ct = jnp.concatenate([a_s.at[j][pl.ds(0, G, stride=8), :] for j in range(nch)], axis=1)
    Gt = jnp.concatenate([b_s.at[j][pl.ds(0, G, stride=8), :] for j in range(nch)], axis=1)
    rowg = lax.broadcasted_iota(jnp.int32, (G, C), 0)
    s = 1
    while s < G:
        Gt = Gt + Ct * _shift_up(Gt, s, 0.0, rowg, G)
        Ct = Ct * _shift_up(Ct, s, 1.0, rowg, G)
        s *= 2
    gg = Gt + Ct * g_next
    cin = _shift_up(gg, 1, g_next, rowg, G)
    for j in range(nch):
        for r in range(8):
            c_s.at[j][pl.ds(r, G, stride=8), :] = cin[:, j * LANES:(j + 1) * LANES]
    return Gv + Cc * jnp.concatenate([c_s[j] for j in range(nch)], axis=1), gg[0:1, :]


def _lru_fwd(z_rx, conv_w, conv_b, wa, ba, wx, bx, lam, *, tt):
    S = z_rx.shape[0]
    n_t = S // tt
    BW = RNN_BLOCK_W
    W = LRU_NB * BW

    def body(x_ref, cw_ref, cb_ref, wa_ref, ba_ref, wx_ref, bx_ref, lam_ref, h_ref, tail, hc, a_s, b_s, c_s):
        t = pl.program_id(1)

        @pl.when(t == 0)
        def _():
            tail[...] = jnp.zeros((8, W), F32)
            hc[...] = jnp.zeros((8, W), F32)

        x = x_ref[...]
        before = tail[...]
        cw = cw_ref[...]
        xa = (cb_ref[...] + cw[3:4] * x + cw[2:3] * _rows_before(x, before, 1) + cw[1:2] * _rows_before(x, before, 2)
              + cw[0:1] * _rows_before(x, before, 3))
        tail[...] = x[tt - 8:tt, :]
        _r, i, _sp, a, mult = _lru_gates(xa, wa_ref, ba_ref, wx_ref, bx_ref, lam_ref[...])
        h, h_last = _scan_down(a, mult * (i * xa), hc[0:1, :], a_s, b_s, c_s)
        h_ref[...] = h
        hc[...] = jnp.broadcast_to(h_last, (8, W))

    blk = lambda n, t: (t, n)
    vec = pl.BlockSpec((1, W), lambda n, t: (0, n))
    mat = pl.BlockSpec((LRU_NB, BW, BW), lambda n, t: (n, 0, 0))
    bias = pl.BlockSpec((LRU_NB, 1, BW), lambda n, t: (n, 0, 0))
    row8 = pltpu.VMEM((8, W), F32)
    wide = pltpu.VMEM((LRU_NB, tt, LANES), F32)
    return pl.pallas_call(
        body, name="lru_fwd", grid=(RNN_BLOCKS // LRU_NB, n_t),
        in_specs=[pl.BlockSpec((tt, W), blk), pl.BlockSpec((CONV_WIDTH, W), lambda n, t: (0, n)), vec, mat, bias, mat,
                  bias, vec],
        out_specs=pl.BlockSpec((tt, W), blk), out_shape=jax.ShapeDtypeStruct((S, D_MODEL), F32),
        scratch_shapes=[row8, row8, wide, wide, wide],
        compiler_params=_params(("parallel", "arbitrary")),
    )(z_rx, conv_w, conv_b, wa, ba, wx, bx, lam)


def _lru_bwd(z_rx, h, dh, dz, conv_w, conv_b, wa, wat, ba, wx, wxt, bx, lam, *, tt):
    S = z_rx.shape[0]
    n_t = S // tt
    BW = RNN_BLOCK_W
    W = LRU_NB * BW
    t8 = tt // 8

    def body(x_ref, xp_ref, h_ref, hp_ref, dh_ref, _dz_ref, cw_ref, cb_ref, wa_ref, wat_ref, ba_ref, wx_ref, wxt_ref,
             bx_ref, lam_ref, dx_ref, dwa_ref, dwx_ref, dba_ref, dbx_ref, dlam_ref, dcw_ref, dcb_ref, nxt, a_c, g_c, a_s,
             b_s, c_s):
        t = pl.program_id(1)
        tile = n_t - 1 - t

        @pl.when(t == 0)
        def _():
            a_c[...] = jnp.zeros((8, W), F32)
            g_c[...] = jnp.zeros((8, W), F32)
            nxt[...] = jnp.zeros((8, W), F32)
            dwa_ref[...] = jnp.zeros_like(dwa_ref)
            dwx_ref[...] = jnp.zeros_like(dwx_ref)
            dba_ref[...] = jnp.zeros_like(dba_ref)
            dbx_ref[...] = jnp.zeros_like(dbx_ref)
            dlam_ref[...] = jnp.zeros_like(dlam_ref)
            dcw_ref[...] = jnp.zeros_like(dcw_ref)
            dcb_ref[...] = jnp.zeros_like(dcb_ref)

        has_prev = (tile > 0).astype(F32)
        x = x_ref[...]
        before = xp_ref[...] * has_prev
        xm1, xm2, xm3 = _rows_before(x, before, 1), _rows_before(x, before, 2), _rows_before(x, before, 3)
        cw = cw_ref[...]
        xa = cb_ref[...] + cw[3:4] * x + cw[2:3] * xm1 + cw[1:2] * xm2 + cw[0:1] * xm3
        lam = lam_ref[...]
        r, i, sp, a, mult = _lru_gates(xa, wa_ref, ba_ref, wx_ref, bx_ref, lam)
        gated = i * xa
        h_prev = _rows_before(h_ref[...], hp_ref[...] * has_prev, 1)
        g, g_first = _scan_up(_rows_after(a, a_c[...], 1), dh_ref[...], g_c[0:1, :], a_s, b_s, c_s)
        a_c[...] = jnp.broadcast_to(a[0:1, :], (8, W))
        g_c[...] = jnp.broadcast_to(g_first, (8, W))
        dlog_a = g * h_prev * a - g * gated * (a * a) / mult
        dgated = g * mult
        di = dgated * xa
        dxa = dgated * i
        dr = dlog_a * (-LRU_C * sp)
        dlam_ref[...] += jnp.sum(dlog_a * (-LRU_C * r), axis=0, keepdims=True) * (-_sigmoid(-lam))
        dpr = dr * r * (1.0 - r)
        dpi = di * i * (1.0 - i)
        xab, dprb, dpib = xa.astype(BF16), dpr.astype(BF16), dpi.astype(BF16)
        tn_dims = (((0,), (0,)), ((), ()))
        back = []
        for j in range(LRU_NB):
            sl = slice(j * BW, (j + 1) * BW)
            dwa_ref[j] += lax.dot_general(xab[:, sl], dprb[:, sl], tn_dims, preferred_element_type=F32)
            dwx_ref[j] += lax.dot_general(xab[:, sl], dpib[:, sl], tn_dims, preferred_element_type=F32)
            dba_ref[j] += jnp.sum(dpr[:, sl], axis=0, keepdims=True)
            dbx_ref[j] += jnp.sum(dpi[:, sl], axis=0, keepdims=True)
            back.append(jnp.dot(dprb[:, sl], wat_ref[j], preferred_element_type=F32)
                        + jnp.dot(dpib[:, sl], wxt_ref[j], preferred_element_type=F32))
        dxa = dxa + jnp.concatenate(back, axis=1)
        after = nxt[...]
        dx = (cw[3:4] * dxa + cw[2:3] * _rows_after(dxa, after, 1) + cw[1:2] * _rows_after(dxa, after, 2)
              + cw[0:1] * _rows_after(dxa, after, 3))
        nxt[...] = dxa[0:8, :]
        dx_ref[...] = dx.astype(BF16)
        dcw_ref[3:4, :] += jnp.sum(dxa * x, axis=0, keepdims=True)
        dcw_ref[2:3, :] += jnp.sum(dxa * xm1, axis=0, keepdims=True)
        dcw_ref[1:2, :] += jnp.sum(dxa * xm2, axis=0, keepdims=True)
        dcw_ref[0:1, :] += jnp.sum(dxa * xm3, axis=0, keepdims=True)
        dcb_ref[...] += jnp.sum(dxa, axis=0, keepdims=True)

    blk = lambda n, t: (n_t - 1 - t, n)
    prev = lambda n, t: (jnp.maximum((n_t - 1 - t) * t8 - 1, 0), n)
    vec = pl.BlockSpec((1, W), lambda n, t: (0, n))
    mat = pl.BlockSpec((LRU_NB, BW, BW), lambda n, t: (n, 0, 0))
    bias = pl.BlockSpec((LRU_NB, 1, BW), lambda n, t: (n, 0, 0))
    cws = pl.BlockSpec((CONV_WIDTH, W), lambda n, t: (0, n))
    tile = pl.BlockSpec((tt, W), blk)
    prev8 = pl.BlockSpec((8, W), prev)
    row8 = pltpu.VMEM((8, W), F32)
    wide = pltpu.VMEM((LRU_NB, tt, LANES), F32)
    return pl.pallas_call(
        body, name="lru_bwd", grid=(RNN_BLOCKS // LRU_NB, n_t),
        in_specs=[tile, prev8, tile, prev8, tile, pl.BlockSpec(memory_space=pl.ANY), cws, vec, mat, mat, bias, mat, mat,
                  bias, vec],
        out_specs=[tile, mat, mat, bias, bias, vec, cws, vec], input_output_aliases={5: 0},
        out_shape=[jax.ShapeDtypeStruct(dz.shape, BF16),
                   jax.ShapeDtypeStruct((RNN_BLOCKS, BW, BW), F32), jax.ShapeDtypeStruct((RNN_BLOCKS, BW, BW), F32),
                   jax.ShapeDtypeStruct((RNN_BLOCKS, 1, BW), F32), jax.ShapeDtypeStruct((RNN_BLOCKS, 1, BW), F32),
                   jax.ShapeDtypeStruct((1, D_MODEL), F32),
                   jax.ShapeDtypeStruct((CONV_WIDTH, D_MODEL), F32), jax.ShapeDtypeStruct((1, D_MODEL), F32)],
        scratch_shapes=[row8, row8, row8, wide, wide, wide],
        compiler_params=_params(("parallel", "arbitrary")),
    )(z_rx, z_rx, h, h, dh, dz, conv_w, conv_b, wa, wat, ba, wx, wxt, bx, lam)


def _mla_proj(z_ckv, q_norm, kv_norm, w_uq, w_ukv, cos, sin, *, ts):
    S = z_ckv.shape[0]
    H = MLA_HEADS

    def body(c_ref, qn_ref, kn_ref, wq_ref, wkv_ref, cos_ref, sin_ref, q_ref, k_ref, v_ref):
        c = c_ref[...]
        cqn, _ = _rms_fwd(c[:, 0:Q_LORA], qn_ref[...])
        ckn, _ = _rms_fwd(c[:, Q_LORA:Q_LORA + KV_LORA], kn_ref[...])
        q = jnp.dot(cqn.astype(BF16), wq_ref[...], preferred_element_type=F32) * (ATTN_SCALE * LOG2E)
        kv = jnp.dot(ckn.astype(BF16), wkv_ref[...], preferred_element_type=F32)
        cos1, sin1 = cos_ref[...], sin_ref[...]
        cos8 = jnp.concatenate([cos1] * H, axis=1)
        sin8 = jnp.concatenate([sin1] * H, axis=1)
        qr = q[:, H * QK_NOPE:]
        lane8 = lax.broadcasted_iota(jnp.int32, qr.shape, 1)
        qr = qr * cos8 + _rot_half(qr, lane8) * sin8
        kr = c[:, Q_LORA + KV_LORA:]
        lane1 = lax.broadcasted_iota(jnp.int32, kr.shape, 1)
        kr = (kr * cos1 + _rot_half(kr, lane1) * sin1).astype(BF16)
        for h in range(H):
            q_ref[h, :, 0:QK_NOPE] = q[:, h * QK_NOPE:(h + 1) * QK_NOPE].astype(BF16)
            q_ref[h, :, QK_NOPE:] = qr[:, h * LANES:(h + 1) * LANES].astype(BF16)
            k_ref[h, :, 0:QK_NOPE] = kv[:, h * 2 * LANES:h * 2 * LANES + LANES].astype(BF16)
            k_ref[h, :, QK_NOPE:] = kr
            v_ref[h] = kv[:, h * 2 * LANES + LANES:(h + 1) * 2 * LANES].astype(BF16)

    full = lambda shape: pl.BlockSpec(shape, lambda i: (0,) * len(shape))
    return pl.pallas_call(
        body, name="mla_proj", grid=(S // ts,),
        in_specs=[pl.BlockSpec((ts, CKV_W), lambda i: (i, 0)), full((1, Q_LORA)), full((1, KV_LORA)),
                  full(w_uq.shape), full(w_ukv.shape), pl.BlockSpec((ts, LANES), lambda i: (i, 0)),
                  pl.BlockSpec((ts, LANES), lambda i: (i, 0))],
        out_specs=[pl.BlockSpec((H, ts, QK_PAD), lambda i: (0, i, 0)), pl.BlockSpec((H, ts, QK_PAD), lambda i: (0, i, 0)),
                   pl.BlockSpec((H, ts, V_HEAD), lambda i: (0, i, 0))],
        out_shape=[jax.ShapeDtypeStruct((H, S, QK_PAD), BF16), jax.ShapeDtypeStruct((H, S, QK_PAD), BF16),
                   jax.ShapeDtypeStruct((H, S, V_HEAD), BF16)],
        compiler_params=_params(("parallel",)),
    )(z_ckv, q_norm, kv_norm, w_uq, w_ukv, cos, sin)


def _mla_proj_bwd(z_ckv, dq, dk, dv, q_norm, kv_norm, w_uqt, w_ukvt, cos, sin, *, ts):
    S = z_ckv.shape[0]
    H = MLA_HEADS

    def body(c_ref, dq_ref, dk_ref, dv_ref, qn_ref, kn_ref, wqt_ref, wkvt_ref, cos_ref, sin_ref,
             dz_ref, dwq_ref, dwkv_ref, dqn_ref, dkn_ref):
        @pl.when(pl.program_id(0) == 0)
        def _():
            dwq_ref[...] = jnp.zeros_like(dwq_ref)
            dwkv_ref[...] = jnp.zeros_like(dwkv_ref)
            dqn_ref[...] = jnp.zeros_like(dqn_ref)
            dkn_ref[...] = jnp.zeros_like(dkn_ref)

        c = c_ref[...]
        cq, ck = c[:, 0:Q_LORA], c[:, Q_LORA:Q_LORA + KV_LORA]
        qn, kn = qn_ref[...], kn_ref[...]
        cqn, _ = _rms_fwd(cq, qn)
        ckn, _ = _rms_fwd(ck, kn)
        cos1, sin1 = cos_ref[...], sin_ref[...]
        lane1 = lax.broadcasted_iota(jnp.int32, cos1.shape, 1)

        def unrope(g):
            return g * cos1 - _rot_half(g * sin1, lane1)

        dq_all = jnp.concatenate([dq_ref[h, :, 0:QK_NOPE] for h in range(H)]
                                 + [unrope(dq_ref[h, :, QK_NOPE:]) for h in range(H)], axis=1)
        dq_all = (dq_all * ATTN_SCALE).astype(BF16)
        dkv_all = jnp.concatenate([p for h in range(H) for p in (dk_ref[h, :, 0:QK_NOPE], dv_ref[h])],
                                  axis=1).astype(BF16)
        dkr = dk_ref[0, :, QK_NOPE:].astype(F32)
        for h in range(1, H):
            dkr = dkr + dk_ref[h, :, QK_NOPE:].astype(F32)
        dkr = unrope(dkr)
        tn_dims = (((0,), (0,)), ((), ()))
        dwq_ref[...] += lax.dot_general(cqn.astype(BF16), dq_all, tn_dims, preferred_element_type=F32)
        dwkv_ref[...] += lax.dot_general(ckn.astype(BF16), dkv_all, tn_dims, preferred_element_type=F32)
        dcqn = jnp.dot(dq_all, wqt_ref[...], preferred_element_type=F32)
        dckn = jnp.dot(dkv_all, wkvt_ref[...], preferred_element_type=F32)
        dcq, dqn_rows = _rms_bwd(dcqn, cq, qn)
        dck, dkn_rows = _rms_bwd(dckn, ck, kn)
        dqn_ref[...] += jnp.sum(dqn_rows, axis=0, keepdims=True)
        dkn_ref[...] += jnp.sum(dkn_rows, axis=0, keepdims=True)
        dz_ref[:, 0:Q_LORA] = dcq.astype(BF16)
        dz_ref[:, Q_LORA:Q_LORA + KV_LORA] = dck.astype(BF16)
        dz_ref[:, Q_LORA + KV_LORA:] = dkr.astype(BF16)

    full = lambda shape: pl.BlockSpec(shape, lambda i: (0,) * len(shape))
    return pl.pallas_call(
        body, name="mla_proj_bwd", grid=(S // ts,),
        in_specs=[pl.BlockSpec((ts, CKV_W), lambda i: (i, 0)), pl.BlockSpec((H, ts, QK_PAD), lambda i: (0, i, 0)),
                  pl.BlockSpec((H, ts, QK_PAD), lambda i: (0, i, 0)), pl.BlockSpec((H, ts, V_HEAD), lambda i: (0, i, 0)),
                  full((1, Q_LORA)), full((1, KV_LORA)), full(w_uqt.shape), full(w_ukvt.shape),
                  pl.BlockSpec((ts, LANES), lambda i: (i, 0)), pl.BlockSpec((ts, LANES), lambda i: (i, 0))],
        out_specs=[pl.BlockSpec((ts, CKV_W), lambda i: (i, 0)), full((Q_LORA, w_uqt.shape[0])),
                   full((KV_LORA, w_ukvt.shape[0])), full((1, Q_LORA)), full((1, KV_LORA))],
        out_shape=[jax.ShapeDtypeStruct((S, CKV_W), BF16), jax.ShapeDtypeStruct((Q_LORA, w_uqt.shape[0]), F32),
                   jax.ShapeDtypeStruct((KV_LORA, w_ukvt.shape[0]), F32), jax.ShapeDtypeStruct((1, Q_LORA), F32),
                   jax.ShapeDtypeStruct((1, KV_LORA), F32)],
        compiler_params=_params(("arbitrary",)),
    )(z_ckv, dq, dk, dv, q_norm, kv_norm, w_uqt, w_ukvt, cos, sin)


NT_DIMS = (((1,), (1,)), ((), ()))
TN_DIMS = (((0,), (0,)), ((), ()))


def _attn_fwd(q, k, v, *, t, hb):
    H, S, _ = q.shape
    n = S // t
    pairs = [(i, j) for i in range(n) for j in range(i + 1)]
    qi = jnp.asarray(np.array([p[0] for p in pairs], np.int32))
    ki = jnp.asarray(np.array([p[1] for p in pairs], np.int32))

    def body(qi_ref, ki_ref, q_ref, k_ref, v_ref, o_ref, lse_ref, m_s, l_s, acc_s):
        p = pl.program_id(1)
        i, j = qi_ref[p], ki_ref[p]

        @pl.when(j == 0)
        def _():
            m_s[...] = jnp.full(m_s.shape, NEG, F32)
            l_s[...] = jnp.zeros(l_s.shape, F32)
            acc_s[...] = jnp.zeros(acc_s.shape, F32)

        def block(hh, r0, nr, nk, masked):
            rows = slice(r0, r0 + nr)
            s = lax.dot_general(q_ref[hh, rows, :], k_ref[hh, 0:nk, :], NT_DIMS, preferred_element_type=F32)
            if masked:
                row = lax.broadcasted_iota(jnp.int32, (nr, nk), 0) + r0
                col = lax.broadcasted_iota(jnp.int32, (nr, nk), 1)
                s = jnp.where(row >= col, s, NEG)
            chunks = nk // LANES
            mc = s[:, 0:LANES]
            for c in range(1, chunks):
                mc = jnp.maximum(mc, s[:, c * LANES:(c + 1) * LANES])
            m_prev = m_s[hh, rows, :]
            m_new = jnp.maximum(m_prev, jnp.max(mc, axis=1, keepdims=True))
            alpha = jnp.exp2(m_prev - m_new)
            pr = jnp.exp2(s - jnp.concatenate([m_new] * chunks, axis=1))
            ls = pr[:, 0:LANES]
            for c in range(1, chunks):
                ls = ls + pr[:, c * LANES:(c + 1) * LANES]
            l_s[hh, rows, :] = alpha * l_s[hh, rows, :] + ls
            acc_s[hh, rows, :] = alpha * acc_s[hh, rows, :] + jnp.dot(pr.astype(BF16), v_ref[hh, 0:nk, :],
                                                                      preferred_element_type=F32)
            m_s[hh, rows, :] = m_new

        def step(diagonal):
            for hh in range(hb):
                if diagonal:
                    block(hh, 0, t // 2, t // 2, True)
                    block(hh, t // 2, t // 2, t, True)
                else:
                    block(hh, 0, t, t, False)

        @pl.when(j < i)
        def _():
            step(False)

        @pl.when(j == i)
        def _():
            step(True)
            for hh in range(hb):
                l = jnp.sum(l_s[hh], axis=1, keepdims=True)
                o_ref[:, hh * V_HEAD:(hh + 1) * V_HEAD] = acc_s[hh] / l
                lse_ref[hh] = (m_s[hh] + jnp.log2(l)).T[0:1, :]

    grid_spec = pltpu.PrefetchScalarGridSpec(
        num_scalar_prefetch=2, grid=(H // hb, len(pairs)),
        in_specs=[pl.BlockSpec((hb, t, QK_PAD), lambda h, p, qi, ki: (h, qi[p], 0)),
                  pl.BlockSpec((hb, t, QK_PAD), lambda h, p, qi, ki: (h, ki[p], 0)),
                  pl.BlockSpec((hb, t, V_HEAD), lambda h, p, qi, ki: (h, ki[p], 0))],
        out_specs=[pl.BlockSpec((t, hb * V_HEAD), lambda h, p, qi, ki: (qi[p], h)),
                   pl.BlockSpec((hb, 1, t), lambda h, p, qi, ki: (h, 0, qi[p]))],
        scratch_shapes=[pltpu.VMEM((hb, t, LANES), F32), pltpu.VMEM((hb, t, LANES), F32),
                        pltpu.VMEM((hb, t, V_HEAD), F32)],
    )
    return pl.pallas_call(
        body, name="attn_fwd", grid_spec=grid_spec,
        out_shape=[jax.ShapeDtypeStruct((S, H * V_HEAD), F32), jax.ShapeDtypeStruct((H, 1, S), F32)],
        compiler_params=_params(("parallel", "arbitrary")),
    )(qi, ki, q, k, v)


def _attn_bwd(q, k, v, do, lse_row, delta_row, *, t):
    H, S, _ = q.shape
    n = S // t
    pairs = [(i, j) for j in range(n) for i in range(j, n)]
    qi = jnp.asarray(np.array([p[0] for p in pairs], np.int32))
    ki = jnp.asarray(np.array([p[1] for p in pairs], np.int32))

    def body(qi_ref, ki_ref, q_ref, k_ref, v_ref, do_ref, lse_ref, dl_ref, dq_ref, dk_ref, dv_ref, dk_s, dv_s, dq_s):
        p = pl.program_id(1)
        i, j = qi_ref[p], ki_ref[p]

        @pl.when(p == 0)
        def _():
            dq_s[...] = jnp.zeros_like(dq_s)

        def block(k0, nk, q0, nq, masked):
            qb, dob = q_ref[0, q0:q0 + nq, :], do_ref[q0:q0 + nq, :]
            kb, vb = k_ref[0, k0:k0 + nk, :], v_ref[0, k0:k0 + nk, :]
            st = lax.dot_general(kb, qb, NT_DIMS, preferred_element_type=F32)
            if masked:
                krow = lax.broadcasted_iota(jnp.int32, (nk, nq), 0) + k0
                qcol = lax.broadcasted_iota(jnp.int32, (nk, nq), 1) + q0
                st = jnp.where(krow <= qcol, st, NEG)
            pt = jnp.exp2(st - lse_ref[0][:, q0:q0 + nq])
            dvp = jnp.dot(pt.astype(BF16), dob, preferred_element_type=F32)
            dpt = lax.dot_general(vb, dob, NT_DIMS, preferred_element_type=F32)
            dst = (pt * (dpt - dl_ref[0][:, q0:q0 + nq])).astype(BF16)
            dkp = jnp.dot(dst, qb, preferred_element_type=F32)
            rows = pl.ds(pl.multiple_of(i * t + q0, LANES), nq)
            dq_s[rows, :] += lax.dot_general(dst, kb, TN_DIMS, preferred_element_type=F32)
            return dkp, dvp

        @pl.when(i == j)
        def _():
            half = t // 2
            dk_s[0:half, :], dv_s[0:half, :] = block(0, half, 0, t, True)
            dk_s[half:t, :], dv_s[half:t, :] = block(half, half, half, half, True)

        @pl.when(i != j)
        def _():
            dkp, dvp = block(0, t, 0, t, False)
            dk_s[...] += dkp
            dv_s[...] += dvp

        @pl.when(i == n - 1)
        def _():
            dk_ref[0] = (dk_s[...] * LN2).astype(BF16)
            dv_ref[0] = dv_s[...].astype(BF16)

        @pl.when(p == len(pairs) - 1)
        def _():
            dq_ref[0] = dq_s[...].astype(BF16)

    grid_spec = pltpu.PrefetchScalarGridSpec(
        num_scalar_prefetch=2, grid=(H, len(pairs)),
        in_specs=[pl.BlockSpec((1, t, QK_PAD), lambda h, p, qi, ki: (h, qi[p], 0)),
                  pl.BlockSpec((1, t, QK_PAD), lambda h, p, qi, ki: (h, ki[p], 0)),
                  pl.BlockSpec((1, t, V_HEAD), lambda h, p, qi, ki: (h, ki[p], 0)),
                  pl.BlockSpec((t, V_HEAD), lambda h, p, qi, ki: (qi[p], h)),
                  pl.BlockSpec((1, 1, t), lambda h, p, qi, ki: (h, 0, qi[p])),
                  pl.BlockSpec((1, 1, t), lambda h, p, qi, ki: (h, 0, qi[p]))],
        out_specs=[pl.BlockSpec((1, S, QK_PAD), lambda h, p, qi, ki: (h, 0, 0)),
                   pl.BlockSpec((1, t, QK_PAD), lambda h, p, qi, ki: (h, ki[p], 0)),
                   pl.BlockSpec((1, t, V_HEAD), lambda h, p, qi, ki: (h, ki[p], 0))],
        scratch_shapes=[pltpu.VMEM((t, QK_PAD), F32), pltpu.VMEM((t, V_HEAD), F32), pltpu.VMEM((S, QK_PAD), F32)],
    )
    return pl.pallas_call(
        body, name="attn_bwd", grid_spec=grid_spec,
        out_shape=[jax.ShapeDtypeStruct((H, S, QK_PAD), BF16), jax.ShapeDtypeStruct((H, S, QK_PAD), BF16),
                   jax.ShapeDtypeStruct((H, S, V_HEAD), BF16)],
        compiler_params=_params(("parallel", "arbitrary")),
    )(qi, ki, q, k, v, do, lse_row, delta_row)


def _merge_h1(h, z_gates, o, x, w_out, norm_mlp, *, ts):
    S = h.shape[0]
    D = D_MODEL

    def body(h_ref, rg_ref, ga_ref, gb_ref, o_ref, x_ref, w_ref, g_ref, m_ref, h1_ref, n2_ref):
        gl, _ = _gelu_and_grad(rg_ref[...].astype(F32))
        m = (_sigmoid(ga_ref[...].astype(F32)) * (h_ref[...] * gl)
             + _sigmoid(gb_ref[...].astype(F32)) * o_ref[...]).astype(BF16)
        m_ref[...] = m
        h1 = x_ref[...] + jnp.dot(m, w_ref[...], preferred_element_type=F32)
        h1_ref[...] = h1
        n2, _ = _rms_fwd(h1, g_ref[...])
        n2_ref[...] = n2.astype(BF16)

    col = lambda c: pl.BlockSpec((ts, D), lambda i: (i, c))
    fixed = lambda shape: pl.BlockSpec(shape, lambda i: (0, 0), pipeline_mode=pl.Buffered(1))
    return pl.pallas_call(
        body, name="merge_h1", grid=(S // ts,),
        in_specs=[col(0), col(0), col(1), col(2), col(0), col(0), fixed((D, D)), fixed((1, D))],
        out_specs=[col(0), col(0), col(0)],
        out_shape=[jax.ShapeDtypeStruct((S, D), BF16), jax.ShapeDtypeStruct((S, D), F32),
                   jax.ShapeDtypeStruct((S, D), BF16)],
        compiler_params=_params(("parallel",)),
    )(h, z_gates, z_gates, z_gates, o, x, w_out, norm_mlp)


def _my_place():
    return lax.axis_index("x"), lax.axis_index("y"), lax.axis_index("c")


def _all_gather(shards, *, name):
    n = len(shards)

    def body(*refs):
        x_refs, out_refs = refs[:n], refs[n:2 * n]
        send_sems, recv_sems, local_sems = refs[2 * n:]
        x, y, c = _my_place()
        me, sibling = (x, y, c), (x, y, 1 - c)
        chips = [(1 - x, y), (x, 1 - y), (1 - x, 1 - y)]

        def slot(a, px, py, pc):
            return out_refs[a].at[4 * px + 2 * py + pc]

        def copy(a, k, block, to, src=None):
            return pltpu.make_async_remote_copy(
                src_ref=slot(a, *block) if src is None else src, dst_ref=slot(a, *block),
                send_sem=send_sems.at[7 * a + k], recv_sem=recv_sems.at[7 * a + k], device_id=to, device_id_type=MESH)

        mine = [pltpu.make_async_copy(x_refs[a], slot(a, *me), local_sems.at[a]) for a in range(n)]
        for cp in mine:
            cp.start()
        first = []
        for a in range(n):
            first.append(copy(a, 0, me, sibling, src=x_refs[a]))
            first += [copy(a, 1 + j, me, (*chip, c), src=x_refs[a]) for j, chip in enumerate(chips)]
        for cp in first:
            cp.start()
        passed = []
        for a in range(n):
            for j, chip in enumerate(chips):
                copy(a, 1 + j, (*chip, c), me).wait_recv()
                fwd = copy(a, 4 + j, (*chip, c), sibling)
                fwd.start()
                passed.append(fwd)
        for a in range(n):
            copy(a, 0, sibling, me).wait_recv()
            for j, chip in enumerate(chips):
                copy(a, 4 + j, (*chip, 1 - c), me).wait_recv()
        for cp in first + passed:
            cp.wait_send()
        for cp in mine:
            cp.wait()

    hbm = pl.BlockSpec(memory_space=pl.ANY)
    return pl.pallas_call(
        body, name=name, out_shape=[jax.ShapeDtypeStruct((N_DEV, *s.shape), s.dtype) for s in shards],
        in_specs=[hbm] * n, out_specs=[hbm] * n,
        scratch_shapes=[pltpu.SemaphoreType.DMA((7 * n,)), pltpu.SemaphoreType.DMA((7 * n,)),
                        pltpu.SemaphoreType.DMA((n,))],
    )(*shards)


def _pushes(src_refs, land_refs, send_sems, recv_sems, slab_per_peer):
    x, y, c = _my_place()
    me = 4 * x + 2 * y + c
    copies = []
    for a in range(len(src_refs)):
        for k in range(1, N_DEV):
            px, py, pc = x ^ (k >> 2), y ^ ((k >> 1) & 1), c ^ (k & 1)
            src = src_refs[a].at[4 * px + 2 * py + pc] if slab_per_peer else src_refs[a]
            copies.append(pltpu.make_async_remote_copy(
                src_ref=src, dst_ref=land_refs[a].at[me], send_sem=send_sems.at[7 * a + k - 1],
                recv_sem=recv_sems.at[7 * a + k - 1], device_id=(px, py, pc), device_id_type=MESH))
    return copies


def _push_start(srcs, *, name, slab_per_peer):
    n = len(srcs)
    lands = [lax.empty((N_DEV, *(s.shape[1:] if slab_per_peer else s.shape)), s.dtype) for s in srcs]

    def body(*refs):
        src_refs, land_refs = refs[:n], refs[n:2 * n]
        send_sems, recv_sems, token = refs[2 * n], refs[2 * n + 1], refs[-1]
        for cp in _pushes(src_refs, land_refs, send_sems, recv_sems, slab_per_peer):
            cp.start()
        token[...] = jnp.zeros_like(token)

    hbm = pl.BlockSpec(memory_space=pltpu.HBM)
    sem = pl.BlockSpec(memory_space=pltpu.SEMAPHORE)
    out = pl.pallas_call(
        body, name=name,
        out_shape=(pltpu.SemaphoreType.DMA((7 * n,)), pltpu.SemaphoreType.DMA((7 * n,)),
                   *[pltpu.HBM(a.shape, a.dtype) for a in srcs + lands], jax.ShapeDtypeStruct((8, LANES), F32)),
        in_specs=[hbm] * (2 * n), out_specs=(sem, sem, *[hbm] * (2 * n), pl.BlockSpec(memory_space=pltpu.VMEM)),
        input_output_aliases={i: 2 + i for i in range(2 * n)},
        compiler_params=pltpu.CompilerParams(has_side_effects=pltpu.SideEffectType.DATAFLOW_SIDE_EFFECTING),
    )(*[pltpu.with_memory_space_constraint(a, pltpu.HBM) for a in srcs + lands])
    return out[0], out[1], list(out[2:2 + n]), list(out[2 + n:2 + 2 * n]), out[-1]


def _push_wait(send_sems, recv_sems, srcs, lands, after, *, name, slab_per_peer):
    n = len(srcs)

    def body(*refs):
        src_refs, land_refs = refs[:n], refs[n:2 * n]
        s_sems, r_sems = refs[2 * n], refs[2 * n + 1]
        for cp in _pushes(src_refs, land_refs, s_sems, r_sems, slab_per_peer):
            cp.wait_send()
            cp.wait_recv()

    hbm = pl.BlockSpec(memory_space=pltpu.HBM)
    sem = pl.BlockSpec(memory_space=pltpu.SEMAPHORE)
    out = pl.pallas_call(
        body, name=name, out_shape=tuple(pltpu.HBM(a.shape, a.dtype) for a in srcs + lands),
        in_specs=[hbm] * (2 * n) + [sem, sem, pl.BlockSpec(memory_space=pl.ANY)], out_specs=tuple([hbm] * (2 * n)),
        input_output_aliases={i: i for i in range(2 * n)},
        compiler_params=pltpu.CompilerParams(has_side_effects=pltpu.SideEffectType.DATAFLOW_SIDE_EFFECTING),
    )(*srcs, *lands, send_sems, recv_sems, after)
    return list(out[:n]), list(out[n:])


def _sum_parts(gp_ref, rows):
    g = gp_ref[0, 0:rows, :].astype(F32)
    for p in range(1, gp_ref.shape[0]):
        g = g + gp_ref[p, 0:rows, :].astype(F32)
    return g


def _adamw_update(w, m, v, g):
    m_new = ADAM_B1 * m + (1.0 - ADAM_B1) * g
    v_new = ADAM_B2 * v + (1.0 - ADAM_B2) * (g * g)
    m_hat = m_new / (1.0 - ADAM_B1 ** ADAM_STEP)
    v_hat = v_new / (1.0 - ADAM_B2 ** ADAM_STEP)
    return -ADAM_LR * (m_hat / (jnp.sqrt(v_hat) + ADAM_EPS) + ADAM_WD * w), m_new, v_new


def _adamw_many(ws, ms, vs, gparts, sums, *, name):
    n, k = len(ws), len(sums)

    def body(*refs):
        w_refs, m_refs, v_refs = refs[:n], refs[n:2 * n], refs[2 * n:3 * n]
        g_refs, s_refs, outs = refs[3 * n:4 * n], refs[4 * n:4 * n + k], refs[4 * n + k:]
        for a in range(n):
            g = _sum_parts(g_refs[a], w_refs[a].shape[0])
            d, m_new, v_new = _adamw_update(w_refs[a][...], m_refs[a][...], v_refs[a][...], g)
            for o_ref, val in zip(outs[4 * a:4 * a + 4], (g, d, m_new, v_new)):
                o_ref[...] = val
        for b in range(k):
            outs[4 * n + b][...] = _sum_parts(s_refs[b], s_refs[b].shape[1])

    out_shape = [jax.ShapeDtypeStruct(w.shape, F32) for w in ws for _ in range(4)]
    out_shape += [jax.ShapeDtypeStruct(s.shape[1:], F32) for s in sums]
    return pl.pallas_call(body, name=name, out_shape=out_shape, compiler_params=_params())(
        *ws, *ms, *vs, *gparts, *sums)


def _adamw(w, m, v, gparts, *, tr, name):
    R, C = w.shape
    n_parts = gparts.shape[0]

    def body(w_ref, m_ref, v_ref, gp_ref, g_ref, d_ref, nm_ref, nv_ref):
        g = _sum_parts(gp_ref, tr)
        d_ref[...], nm_ref[...], nv_ref[...] = _adamw_update(w_ref[...], m_ref[...], v_ref[...], g)
        g_ref[...] = g

    row = pl.BlockSpec((tr, C), lambda i: (i, 0))
    shp = jax.ShapeDtypeStruct((R, C), F32)
    return pl.pallas_call(
        body, name=name, grid=(R // tr,),
        in_specs=[row, row, row, pl.BlockSpec((n_parts, tr, C), lambda i: (0, i, 0))],
        out_specs=[row, row, row, row], out_shape=[shp, shp, shp, shp],
        compiler_params=_params(("parallel",)),
    )(w, m, v, gparts)


def _rope_tables(s):
    pos = jnp.arange(s, dtype=F32)
    inv_freq = 1.0 / (ROPE_THETA ** (jnp.arange(0, QK_ROPE, 2, dtype=F32) / QK_ROPE))
    ang = pos[:, None] * inv_freq[None, :]
    cos, sin = jnp.cos(ang), jnp.sin(ang)
    zero = jnp.zeros((s, LANES - QK_ROPE), F32)
    return jnp.concatenate([cos, cos, zero], -1), jnp.concatenate([sin, sin, zero], -1)


def _pick(n, want):
    t = min(n, want)
    assert n % t == 0
    return t


def _local_step(x, target, wts, small, hooks):
    S = x.shape[0]
    H = MLA_HEADS
    ts = _pick(S, 512)
    tm = _pick(S, 512)
    tm_wide = _pick(S, 1024)
    tk_s = _pick(S, 2048)
    row = lambda v: v.reshape(1, -1)
    w_in = wts["w_in"]
    w_main = jnp.concatenate([w_in[:, 0:2048], w_in[:, 2624:4672]], axis=1)
    w_ckv = jnp.concatenate([w_in[:, 2048:2624], jnp.zeros((D_MODEL, CKV_W - 576), BF16)], axis=1)
    w_uq3 = wts["w_uq"].reshape(Q_LORA, H, QK_NOPE + QK_ROPE)
    w_uq_p = jnp.concatenate(
        [w_uq3[:, :, :QK_NOPE].reshape(Q_LORA, H * QK_NOPE),
         jnp.pad(w_uq3[:, :, QK_NOPE:], ((0, 0), (0, 0), (0, LANES - QK_ROPE))).reshape(Q_LORA, H * LANES)], axis=1)
    w_ukv = wts["w_ukv"]
    cos, sin = _rope_tables(S)
    conv_w, conv_b = small["conv_w"], row(small["conv_b"])
    wa, wx = small["lru_wa"].astype(BF16), small["lru_wx"].astype(BF16)
    wat, wxt = jnp.swapaxes(wa, 1, 2), jnp.swapaxes(wx, 1, 2)
    ba, bx = small["lru_ba"].reshape(RNN_BLOCKS, 1, RNN_BLOCK_W), small["lru_bx"].reshape(RNN_BLOCKS, 1, RNN_BLOCK_W)
    lam = row(small["lru_lambda"])
    q_norm, kv_norm = row(small["q_norm"]), row(small["kv_norm"])
    norm_mix, norm_mlp, norm_final = row(small["norm_mix"]), row(small["norm_mlp"]), row(small["norm_final"])

    xn = _rmsnorm_cast(x, norm_mix, ts=ts, name="norm_mix")
    ident = lambda acc: (acc,)
    (z_rx,) = _mm(xn, w_main[:, :D_MODEL], name="z_rx", tm=tm_wide, tn=1024, tk=1024, outs=[("tile", F32)],
                  epilogue=ident)
    (z_gates,) = _mm(xn, w_main[:, D_MODEL:], name="z_gates", tm=tm_wide, tn=1024, tk=1024, outs=[("tile", BF16)],
                     epilogue=ident)
    (z_ckv,) = _mm(xn, w_ckv, name="z_ckv", tm=tm, tn=CKV_W, tk=1024, outs=[("tile", F32)], epilogue=ident)
    tt = _pick(S, 256)
    h = _lru_fwd(z_rx, conv_w, conv_b, wa, ba, wx, bx, lam, tt=tt)
    q, k, v = _mla_proj(z_ckv, q_norm, kv_norm, w_uq_p, w_ukv, cos, sin, ts=_pick(S, 256))
    ta = _pick(S, 1024)
    o, lse = _attn_fwd(q, k, v, t=ta, hb=2)
    w_out, w_up, w_down = hooks["weights_later"](o)
    merged, h1, n2 = _merge_h1(h, z_gates, o, x, w_out, norm_mlp, ts=_pick(S, 256))

    def ep_up(acc):
        r = jnp.maximum(acc, 0.0)
        return r * r, r

    act, relu = _mm(n2, w_up, name="up", tm=tm_wide, tn=1024, tk=1024, outs=[("tile", BF16), ("tile", BF16)],
                    epilogue=ep_up)

    def ep_loss(acc, h1v, tgt, g):
        h2 = acc + h1v
        y, _ = _rms_fwd(h2, g)
        err = y - tgt
        loss_rows = 0.5 * jnp.mean(err * err, axis=-1, keepdims=True)
        dy = err * (1.0 / D_MODEL)
        dh2, dg_rows = _rms_bwd(dy, h2, g)
        lsum = jnp.sum(loss_rows, axis=0, keepdims=True)
        return dh2, dh2, jnp.sum(dg_rows, axis=0, keepdims=True), jnp.broadcast_to(lsum, (1, D_MODEL))

    dh2, dh2b, dnf_p, loss_p = _mm(
        act, w_down, name="down_loss", tm=tm, tn=1024, tk=D_FF,
        outs=[("tile", F32), ("tile", BF16), ("rowpart", F32), ("rowpart", F32)], epilogue=ep_loss,
        extras=[("tile", h1), ("tile", target), ("row", norm_final)])
    loss_part = jnp.sum(loss_p[:, 0, 0])
    d_norm_final = jnp.sum(dnf_p, axis=(0, 1))

    def ep_du(acc, r):
        return (acc * (2.0 * r.astype(F32)),)

    (du,) = _mm(dh2b, w_down, name="d_act", tb=True, tm=tm_wide, tn=1024, tk=1024, outs=[("tile", BF16)], epilogue=ep_du,
                extras=[("tile", relu)])

    def ep_dh1(acc, h1v, dh2v, g):
        dv, dg_rows = _rms_bwd(acc, h1v, g)
        dh1 = dh2v + dv
        return dh1, dh1, jnp.sum(dg_rows, axis=0, keepdims=True)

    dh1, dh1b, dnm_p = _mm(du, w_up, name="d_n2", tb=True, tm=tm, tn=1024, tk=D_FF,
                           outs=[("tile", F32), ("tile", BF16), ("rowpart", F32)], epilogue=ep_dh1,
                           extras=[("tile", h1), ("tile", dh2), ("row", norm_mlp)])
    d_norm_mlp = jnp.sum(dnm_p, axis=(0, 1))
    tn_mm = functools.partial(_mm, ta=True, tk=tk_s, outs=[("tile", BF16)], epilogue=ident)
    (d_w_down,) = tn_mm(act, dh2b, name="dw_down", tm=1024, tn=1024)
    (p_w_up,) = _mm(n2, du, name="dw_up", ta=True, tk=tk_s, tm=1024, tn=D_FF // N_DEV, outs=[("colshard", BF16)],
                    epilogue=ident)
    (d_w_out,) = tn_mm(merged, dh1b, name="dw_out", tm=1024, tn=1024)
    early = [d_w_out.reshape(N_DEV, -1, D_MODEL), p_w_up, d_w_down.reshape(N_DEV, -1, D_MODEL)]
    w_out = w_out + hooks["send"]("early", early)[0, 0].astype(BF16)

    tmm = _pick(S, 256)

    def ep_dmerge(dm, hv, rg, ga, gb, ov):
        rg, ga, gb = rg.astype(F32), ga.astype(F32), gb.astype(F32)
        gl, dgl = _gelu_and_grad(rg)
        sa, sb = _sigmoid(ga), _sigmoid(gb)
        ya = hv * gl
        dya = dm * sa
        do = dm * sb
        dga = dm * ya * sa * (1.0 - sa)
        dgb = dm * ov * sb * (1.0 - sb)
        dh = dya * gl
        drg = dya * hv * dgl
        dov = do * ov
        lane = lax.broadcasted_iota(jnp.int32, (dm.shape[0], LANES), 1)
        delta = jnp.zeros((dm.shape[0], LANES), F32)
        for hh in range(H):
            dsum = jnp.sum(dov[:, hh * V_HEAD:(hh + 1) * V_HEAD], axis=1, keepdims=True)
            delta = jnp.where(lane == hh, dsum, delta)
        return dh, jnp.concatenate([drg, dga, dgb], axis=1), do, delta

    dh_lru, dz_part, do, delta_w = _mm(
        dh1b, w_out, name="d_merge", tb=True, tm=tmm, tn=1024, tk=1024,
        outs=[("tile", F32), ("cols", BF16, 4 * D_MODEL, D_MODEL), ("tile", BF16), ("side", F32)],
        epilogue=ep_dmerge,
        extras=[("tile", h), ("tilecol", z_gates, 0), ("tilecol", z_gates, 1), ("tilecol", z_gates, 2), ("tile", o)])
    delta_row = delta_w[:, :H].T.reshape(H, 1, S)
    lse_row = lse

    dq, dk, dv = _attn_bwd(q, k, v, do, lse_row, delta_row, t=ta)
    dz_ckv, d_w_uq_p, d_w_ukv, d_q_norm, d_kv_norm = _mla_proj_bwd(
        z_ckv, dq, dk, dv, q_norm, kv_norm, w_uq_p.T, w_ukv.T, cos, sin, ts=_pick(S, 256))
    d_w_uq = jnp.concatenate(
        [d_w_uq_p[:, :H * QK_NOPE].reshape(Q_LORA, H, QK_NOPE),
         d_w_uq_p[:, H * QK_NOPE:].reshape(Q_LORA, H, LANES)[:, :, :QK_ROPE]], axis=2).reshape(Q_LORA, -1)

    dz_main, d_wa, d_wx, d_ba, d_bx, d_lam, d_conv_w, d_conv_b = _lru_bwd(
        z_rx, h, dh_lru, dz_part, conv_w, conv_b, wa, wat, ba, wx, wxt, bx, lam, tt=tt)

    (d_w_main,) = tn_mm(xn, dz_main, name="dw_main", tm=1024, tn=1024)
    (d_w_ckv,) = tn_mm(xn, dz_ckv, name="dw_ckv", tm=1024, tn=CKV_W)
    d_w_in = jnp.concatenate([d_w_main[:, 0:2048], d_w_ckv[:, 0:576], d_w_main[:, 2048:4096]], axis=1)

    def col_parts(full):
        r = full.shape[0]
        return jnp.transpose(full.astype(BF16).reshape(r, N_DEV, -1), (1, 0, 2))

    late = [col_parts(d_w_in), col_parts(d_w_uq), col_parts(d_w_ukv)]
    norm_mix = norm_mix + hooks["send"]("late", late)[0, 0]

    def ep_dx(acc, xv, dh1v, g):
        dv, dg_rows = _rms_bwd(acc, xv, g)
        return dh1v + dv, jnp.sum(dg_rows, axis=0, keepdims=True)

    grad_x, dnx_p = _mm(dz_main, w_main, name="dx", tb=True, tm=tm, tn=1024, tk=4 * D_MODEL,
                        outs=[("tile", F32), ("rowpart", F32)], epilogue=ep_dx, more=(dz_ckv, w_ckv),
                        extras=[("tile", x), ("tile", dh1), ("row", norm_mix)])
    d_norm_mix = jnp.sum(dnx_p, axis=(0, 1))
    sm = {"norm_mix": d_norm_mix, "conv_w": d_conv_w, "conv_b": d_conv_b.reshape(-1), "lru_wa": d_wa,
          "lru_ba": d_ba.reshape(RNN_BLOCKS, RNN_BLOCK_W), "lru_wx": d_wx, "lru_bx": d_bx.reshape(RNN_BLOCKS, RNN_BLOCK_W),
          "lru_lambda": d_lam.reshape(-1), "q_norm": d_q_norm.reshape(-1), "kv_norm": d_kv_norm.reshape(-1),
          "norm_mlp": d_norm_mlp, "norm_final": d_norm_final}
    return loss_part, grad_x, sm


BIG = ("w_in", "w_uq", "w_ukv", "w_out", "w_up", "w_down")
SMALL = ("norm_mix", "conv_b", "lru_wa", "lru_ba", "lru_wx", "lru_bx", "lru_lambda", "q_norm", "kv_norm", "norm_mlp",
         "norm_final")
WEIGHTS = ("norm_mix", "w_in", "conv_w", "conv_b", "lru_wa", "lru_ba", "lru_wx", "lru_bx", "lru_lambda", "q_norm", "w_uq",
           "kv_norm", "w_ukv", "w_out", "norm_mlp", "w_up", "w_down", "norm_final")
ADAM_TILE_ROWS = {"w_in": 256, "w_uq": 128, "w_ukv": 128, "w_out": 64, "w_up": 256, "w_down": 128}
CONV_ROWS = N_DEV * 8


def _rows(a):
    return a.reshape(-1, LANES)


def _pad_rows(a, mult):
    r = a.shape[-2]
    pad = (-r) % mult
    if pad == 0:
        return a
    cfg = [(0, 0)] * (a.ndim - 2) + [(0, pad), (0, 0)]
    return jnp.pad(a, cfg)


def _cols_from_shards(g):
    return jnp.transpose(g, (1, 0, 2)).reshape(g.shape[1], -1)


def kernel(x, norm_mix, w_in, conv_w, conv_b, lru_wa, lru_ba, lru_wx, lru_bx, lru_lambda, q_norm, w_uq, kv_norm, w_ukv, w_out, norm_mlp, w_up, w_down, norm_final, loss_target, m_norm_mix, m_w_in, m_conv_w, m_conv_b, m_lru_wa, m_lru_ba, m_lru_wx, m_lru_bx, m_lru_lambda, m_q_norm, m_w_uq, m_kv_norm, m_w_ukv, m_w_out, m_norm_mlp, m_w_up, m_w_down, m_norm_final, v_norm_mix, v_w_in, v_conv_w, v_conv_b, v_lru_wa, v_lru_ba, v_lru_wx, v_lru_bx, v_lru_lambda, v_q_norm, v_w_uq, v_kv_norm, v_w_ukv, v_w_out, v_norm_mlp, v_w_up, v_w_down, v_norm_final):
    W = dict(norm_mix=norm_mix, w_in=w_in, conv_w=conv_w, conv_b=conv_b, lru_wa=lru_wa, lru_ba=lru_ba, lru_wx=lru_wx,
             lru_bx=lru_bx, lru_lambda=lru_lambda, q_norm=q_norm, w_uq=w_uq, kv_norm=kv_norm, w_ukv=w_ukv, w_out=w_out,
             norm_mlp=norm_mlp, w_up=w_up, w_down=w_down, norm_final=norm_final)
    M = dict(norm_mix=m_norm_mix, w_in=m_w_in, conv_w=m_conv_w, conv_b=m_conv_b, lru_wa=m_lru_wa, lru_ba=m_lru_ba,
             lru_wx=m_lru_wx, lru_bx=m_lru_bx, lru_lambda=m_lru_lambda, q_norm=m_q_norm, w_uq=m_w_uq, kv_norm=m_kv_norm,
             w_ukv=m_w_ukv, w_out=m_w_out, norm_mlp=m_norm_mlp, w_up=m_w_up, w_down=m_w_down, norm_final=m_norm_final)
    V = dict(norm_mix=v_norm_mix, w_in=v_w_in, conv_w=v_conv_w, conv_b=v_conv_b, lru_wa=v_lru_wa, lru_ba=v_lru_ba,
             lru_wx=v_lru_wx, lru_bx=v_lru_bx, lru_lambda=v_lru_lambda, q_norm=v_q_norm, w_uq=v_w_uq, kv_norm=v_kv_norm,
             w_ukv=v_w_ukv, w_out=v_w_out, norm_mlp=v_norm_mlp, w_up=v_w_up, w_down=v_w_down, norm_final=v_norm_final)
    me = 4 * lax.axis_index("x") + 2 * lax.axis_index("y") + lax.axis_index("c")

    first, later = ("w_in", "w_uq", "w_ukv"), ("w_out", "w_up", "w_down")
    got = _all_gather([W[n].astype(BF16) for n in first] + [_pad_rows(conv_w, 8)], name="gather_weights")
    wts = {"w_in": _cols_from_shards(got[0]), "w_uq": _cols_from_shards(got[1]), "w_ukv": _cols_from_shards(got[2])}
    w_send, w_recv, w_src, w_land, zeros = _push_start([W[n].astype(BF16) for n in later], name="gather_later_start",
                                                       slab_per_peer=False)
    small = {n: W[n] for n in SMALL}
    small["conv_w"] = _cols_from_shards(got[3][:, :CONV_WIDTH])
    small["norm_mix"] = norm_mix + zeros[0, 0]

    def with_own_slab(land, mine):
        return lax.dynamic_update_slice(land, mine, (me, 0, 0))

    def weights_later(after):
        srcs, lands = _push_wait(w_send, w_recv, w_src, w_land, after, name="gather_later_wait", slab_per_peer=False)
        w_out_g, w_up_g, w_down_g = [with_own_slab(l, s[None]) for l, s in zip(lands, srcs)]
        return w_out_g.reshape(-1, D_MODEL), _cols_from_shards(w_up_g), w_down_g.reshape(-1, D_MODEL)

    sent = {}
    G, Dl, NM, NV = {}, {}, {}, {}

    def finish(group, names, after):
        s_sems, r_sems, srcs, lands, _ = sent[group]
        srcs, lands = _push_wait(s_sems, r_sems, srcs, lands, after, name="exchange_" + group + "_wait",
                                 slab_per_peer=True)
        for n, src, land in zip(names, srcs, lands):
            parts = with_own_slab(land, lax.dynamic_slice(src, (me, 0, 0), (1, *src.shape[1:])))
            G[n], Dl[n], NM[n], NV[n] = _adamw(W[n], M[n], V[n], parts, tr=ADAM_TILE_ROWS[n], name="adamw_" + n)

    def send(group, parts):
        sent[group] = _push_start(parts, name="exchange_" + group + "_start", slab_per_peer=True)
        zeros = sent[group][4]
        if group == "late":
            finish("early", later, zeros)
            zeros = zeros + 0.0 * (Dl["w_out"][0:8, 0:LANES] + Dl["w_up"][0:8, 0:LANES] + Dl["w_down"][0:8, 0:LANES])
        return zeros

    loss_part, grad_x, g_small = _local_step(x[0], loss_target[0], wts, small,
                                              {"weights_later": weights_later, "send": send})
    finish("late", first, grad_x)

    conv_rows = _pad_rows(jnp.transpose(g_small["conv_w"].reshape(CONV_WIDTH, N_DEV, LANES), (1, 0, 2)), 8)
    loss_rows = jnp.zeros((8, LANES), F32).at[0, 0].set(loss_part)
    gathered = _all_gather([_pad_rows(_rows(g_small[n]), 8) for n in SMALL]
                           + [conv_rows.reshape(CONV_ROWS, LANES), loss_rows], name="gather_small")
    k = len(SMALL)
    outs = _adamw_many([_rows(W[n]) for n in SMALL], [_rows(M[n]) for n in SMALL], [_rows(V[n]) for n in SMALL],
                       gathered[:k], gathered[k:], name="adamw_small")
    for j, n in enumerate(SMALL):
        for out, o in zip((G, Dl, NM, NV), outs[4 * j:4 * j + 4]):
            out[n] = o.reshape(W[n].shape)
    conv_sum, loss_sum = outs[4 * k:]
    loss = loss_sum[0, 0]

    g_conv = lax.dynamic_slice(conv_sum, (me * 8, 0), (8, LANES))
    conv_out = _adamw(_pad_rows(conv_w, 8), _pad_rows(m_conv_w, 8), _pad_rows(v_conv_w, 8), g_conv[None], tr=8,
                      name="adamw_conv_w")
    for out, pk in zip((G, Dl, NM, NV), conv_out):
        out["conv_w"] = pk[:CONV_WIDTH]
    return (loss, grad_x[None], *[G[n] for n in WEIGHTS], *[Dl[n] for n in WEIGHTS], *[NM[n] for n in WEIGHTS],
            *[NV[n] for n in WEIGHTS])
```

```python
import functools

import numpy as np
import jax
import jax.numpy as jnp
from jax import lax
from jax.experimental import pallas as pl
from jax.experimental.pallas import tpu as pltpu

F32 = jnp.float32
BF16 = jnp.bfloat16
MESH = pl.DeviceIdType.MESH

D_MODEL = 1024
N_DEV = 8
LANES = 128
RNN_BLOCKS = 8
RNN_BLOCK_W = 128
CONV_WIDTH = 4
LRU_C = 8.0
MLA_HEADS = 8
QK_NOPE = 128
QK_ROPE = 64
V_HEAD = 128
QK_PAD = 256
Q_LORA = 256
KV_LORA = 256
CKV_W = 640
ROPE_THETA = 10000.0
D_FF = 4096
EPS = 1e-6
ATTN_SCALE = (QK_NOPE + QK_ROPE) ** -0.5
LOG2E = 1.4426950408889634
LN2 = 0.6931471805599453
NEG = -1e30

ADAM_LR = 0.001
ADAM_B1 = 0.9
ADAM_B2 = 0.999
ADAM_EPS = 1e-08
ADAM_WD = 0.01
ADAM_STEP = 10

VMEM_LIMIT = 56 * 1024 * 1024


def _params(sem=None):
    return pltpu.CompilerParams(dimension_semantics=sem, vmem_limit_bytes=VMEM_LIMIT)


def _sigmoid(v):
    return 1.0 / (1.0 + jnp.exp(-v))


def _softplus(y):
    e = jnp.exp(-jnp.abs(y))
    u = 1.0 + e
    d = u - 1.0
    l1p = jnp.where(d == 0.0, e, jnp.log(u) * e / jnp.where(d == 0.0, 1.0, d))
    return jnp.maximum(y, 0.0) + l1p


_GELU_K = 0.7978845608028654
_GELU_C = 0.044715


def _gelu_and_grad(v):
    t = jnp.tanh(_GELU_K * (v + _GELU_C * v * v * v))
    g = 0.5 * v * (1.0 + t)
    dg = 0.5 * (1.0 + t) + 0.5 * v * (1.0 - t * t) * _GELU_K * (1.0 + 3.0 * _GELU_C * v * v)
    return g, dg


def _rms_fwd(v, g):
    rstd = lax.rsqrt(jnp.mean(v * v, axis=-1, keepdims=True) + EPS)
    return v * rstd * g, rstd


def _rms_bwd(dy, v, g):
    rstd = lax.rsqrt(jnp.mean(v * v, axis=-1, keepdims=True) + EPS)
    vh = v * rstd
    dvh = dy * g
    dv = rstd * (dvh - vh * jnp.mean(dvh * vh, axis=-1, keepdims=True))
    return dv, dy * vh


def _shift_down(v, s, fill, row):
    return jnp.where(row >= s, pltpu.roll(v, s, 0), fill)


def _shift_up(v, s, fill, row, n):
    return jnp.where(row < n - s, pltpu.roll(v, n - s, 0), fill)


def _rot_half(v, lane):
    n = v.shape[-1]
    l = lane & (LANES - 1)
    up = pltpu.roll(v, n - QK_ROPE // 2, 1)
    dn = pltpu.roll(v, QK_ROPE // 2, 1)
    return jnp.where(l < QK_ROPE // 2, -up, jnp.where(l < QK_ROPE, dn, 0.0))


def _mm(a, b, *, name, tm, tn, tk, outs, epilogue, extras=(), ta=False, tb=False, more=None):
    assert not (ta and tb)
    if ta:
        K, M = a.shape
    else:
        M, K = a.shape
    if tb:
        N, K2 = b.shape
    else:
        K2, N = b.shape
    assert K == K2 and M % tm == 0 and N % tn == 0 and K % tk == 0, (name, a.shape, b.shape)
    n_i, n_j, n_k = M // tm, N // tn, K // tk
    n_ex, n_out = len(extras), len(outs)
    n_more = 0 if more is None else 2
    assert more is None or (n_k == 1 and not ta)

    def body(*refs):
        a_ref, b_ref = refs[0], refs[1]
        ex_refs = refs[2 + n_more:2 + n_more + n_ex]
        out_refs = refs[2 + n_more + n_ex:2 + n_more + n_ex + n_out]
        if ta:
            part = lax.dot_general(a_ref[...], b_ref[...], (((0,), (0,)), ((), ())), preferred_element_type=F32)
        elif tb:
            part = lax.dot_general(a_ref[...], b_ref[...], (((1,), (1,)), ((), ())), preferred_element_type=F32)
        else:
            part = jnp.dot(a_ref[...], b_ref[...], preferred_element_type=F32)
        if more is not None:
            part = part + lax.dot_general(refs[2][...], refs[3][...], (((1,), (1,)), ((), ())),
                                          preferred_element_type=F32)

        def finish(acc):
            res = epilogue(acc, *[r[...] for r in ex_refs])
            for o_ref, r, spec in zip(out_refs, res, outs):
                if spec[0] == "cols":
                    o_ref[:, spec[3]:spec[3] + r.shape[1]] = r.astype(o_ref.dtype)
                else:
                    o_ref[...] = r.astype(o_ref.dtype).reshape(o_ref.shape)

        if n_k == 1:
            finish(part)
        else:
            acc_ref = refs[-1]
            k = pl.program_id(2)

            @pl.when(k == 0)
            def _():
                acc_ref[...] = part

            @pl.when(k > 0)
            def _():
                acc_ref[...] += part

            @pl.when(k == n_k - 1)
            def _():
                finish(acc_ref[...])

    a_spec = pl.BlockSpec((tk, tm), lambda j, i, k: (k, i)) if ta else pl.BlockSpec((tm, tk), lambda j, i, k: (i, k))
    b_once = dict(pipeline_mode=pl.Buffered(1)) if (n_j == 1 and n_k == 1) else {}
    if tb:
        in_specs = [a_spec, pl.BlockSpec((tn, tk), lambda j, i, k: (j, k), **b_once)]
    else:
        in_specs = [a_spec, pl.BlockSpec((tk, tn), lambda j, i, k: (k, j), **b_once)]
    if more is not None:
        k2 = more[0].shape[1]
        in_specs += [pl.BlockSpec((tm, k2), lambda j, i, k: (i, 0)), pl.BlockSpec((tn, k2), lambda j, i, k: (j, 0), **b_once)]
    for ex in extras:
        kind = ex[0]
        if kind == "tile":
            in_specs.append(pl.BlockSpec((tm, tn), lambda j, i, k: (i, j)))
        elif kind == "tilecol":
            assert n_j == 1
            in_specs.append(pl.BlockSpec((tm, tn), functools.partial(lambda c, j, i, k: (i, c), ex[2])))
        else:
            in_specs.append(pl.BlockSpec((1, tn), lambda j, i, k: (0, j)))
    out_specs, out_shape = [], []
    for kind, dt, *rest in outs:
        if kind == "tile":
            out_specs.append(pl.BlockSpec((tm, tn), lambda j, i, k: (i, j)))
            out_shape.append(jax.ShapeDtypeStruct((M, N), dt))
        elif kind == "colshard":
            out_specs.append(pl.BlockSpec((1, tm, tn), lambda j, i, k: (j, i, 0)))
            out_shape.append(jax.ShapeDtypeStruct((n_j, M, tn), dt))
        elif kind == "cols":
            assert n_j == 1
            out_specs.append(pl.BlockSpec((tm, rest[0]), lambda j, i, k: (i, 0)))
            out_shape.append(jax.ShapeDtypeStruct((M, rest[0]), dt))
        elif kind == "side":
            assert n_j == 1
            out_specs.append(pl.BlockSpec((tm, LANES), lambda j, i, k: (i, 0)))
            out_shape.append(jax.ShapeDtypeStruct((M, LANES), dt))
        else:
            out_specs.append(pl.BlockSpec((1, 1, tn), lambda j, i, k: (i, 0, j)))
            out_shape.append(jax.ShapeDtypeStruct((n_i, 1, N), dt))
    scratch = [pltpu.VMEM((tm, tn), F32)] if n_k > 1 else []
    return pl.pallas_call(
        body, name=name, grid=(n_j, n_i, n_k), in_specs=in_specs, out_specs=out_specs, out_shape=out_shape,
        scratch_shapes=scratch, compiler_params=_params(("parallel", "parallel", "arbitrary")),
    )(a, b, *(more or ()), *[ex[1] for ex in extras])


def _rmsnorm_cast(x, g, *, ts, name):
    S, D = x.shape

    def body(x_ref, g_ref, o_ref):
        y, _ = _rms_fwd(x_ref[...], g_ref[...])
        o_ref[...] = y.astype(BF16)

    return pl.pallas_call(
        body, name=name, grid=(S // ts,),
        in_specs=[pl.BlockSpec((ts, D), lambda i: (i, 0)), pl.BlockSpec((1, D), lambda i: (0, 0))],
        out_specs=pl.BlockSpec((ts, D), lambda i: (i, 0)), out_shape=jax.ShapeDtypeStruct((S, D), BF16),
        compiler_params=_params(("parallel",)),
    )(x, g)


LRU_NB = 4


def _lru_gates(xa, wa_ref, ba_ref, wx_ref, bx_ref, lam):
    xab = xa.astype(BF16)
    W = RNN_BLOCK_W
    rs, is_ = [], []
    for j in range(LRU_NB):
        xj = xab[:, j * W:(j + 1) * W]
        rs.append(_sigmoid(jnp.dot(xj, wa_ref[j], preferred_element_type=F32) + ba_ref[j]))
        is_.append(_sigmoid(jnp.dot(xj, wx_ref[j], preferred_element_type=F32) + bx_ref[j]))
    r = jnp.concatenate(rs, axis=1)
    i = jnp.concatenate(is_, axis=1)
    sp = _softplus(-lam)
    log_a = (-LRU_C * r) * sp
    a = jnp.exp(log_a)
    y = 2.0 * log_a
    one_m = jnp.where(y > -0.01, -y * (1.0 + 0.5 * y * (1.0 + y * (1.0 / 3.0))), 1.0 - a * a)
    return r, i, sp, a, jnp.sqrt(one_m)


def _rows_before(x, tail8, k):
    e16 = jnp.concatenate([tail8, x[0:8, :]], axis=0)
    return jnp.concatenate([pltpu.roll(e16, k, 0)[8:16, :], pltpu.roll(x, k, 0)[8:, :]], axis=0)


def _rows_after(x, head8, k):
    tt = x.shape[0]
    e16 = jnp.concatenate([x[tt - 8:tt, :], head8], axis=0)
    return jnp.concatenate([pltpu.roll(x, tt - k, 0)[:tt - 8, :], pltpu.roll(e16, 16 - k, 0)[0:8, :]], axis=0)


def _scan_down(a, b, h0, a_s, b_s, c_s):
    tt, C = a.shape
    G, nch = tt // 8, C // LANES
    rin = lax.broadcasted_iota(jnp.int32, (tt, C), 0) & 7

    def in_group(v, s):
        return pltpu.roll(v.reshape(G, 8, C), s, 1).reshape(tt, C)

    A, B = a, b
    for s in (1, 2, 4):
        B = A * jnp.where(rin >= s, in_group(B, s), 0.0) + B
        A = A * jnp.where(rin >= s, in_group(A, s), 1.0)
    for j in range(nch):
        a_s[j] = A[:, j * LANES:(j + 1) * LANES]
        b_s[j] = B[:, j * LANES:(j + 1) * LANES]
    At = jnp.concatenate([a_s.at[j][pl.ds(7, G, stride=8), :] for j in range(nch)], axis=1)
    Bt = jnp.concatenate([b_s.at[j][pl.ds(7, G, stride=8), :] for j in range(nch)], axis=1)
    rowg = lax.broadcasted_iota(jnp.int32, (G, C), 0)
    s = 1
    while s < G:
        Bt = At * _shift_down(Bt, s, 0.0, rowg) + Bt
        At = At * _shift_down(At, s, 1.0, rowg)
        s *= 2
    hg = At * h0 + Bt
    cin = _shift_down(hg, 1, h0, rowg)
    for j in range(nch):
        for r in range(8):
            c_s.at[j][pl.ds(r, G, stride=8), :] = cin[:, j * LANES:(j + 1) * LANES]
    return A * jnp.concatenate([c_s[j] for j in range(nch)], axis=1) + B, hg[G - 1:G, :]


def _scan_up(c, g_in, g_next, a_s, b_s, c_s):
    tt, C = c.shape
    G, nch = tt // 8, C // LANES
    rin = lax.broadcasted_iota(jnp.int32, (tt, C), 0) & 7

    def in_group(v, s):
        return pltpu.roll(v.reshape(G, 8, C), 8 - s, 1).reshape(tt, C)

    Cc, Gv = c, g_in
    for s in (1, 2, 4):
        Gv = Gv + Cc * jnp.where(rin < 8 - s, in_group(Gv, s), 0.0)
        Cc = Cc * jnp.where(rin < 8 - s, in_group(Cc, s), 1.0)
    for j in range(nch):
        a_s[j] = Cc[:, j * LANES:(j + 1) * LANES]
        b_s[j] = Gv[:, j * LANES:(j + 1) * LANES]
    Ct = jnp.concatenate([a_s.at[j][pl.ds(0, G, stride=8), :] for j in range(nch)], axis=1)
    Gt = jnp.concatenate([b_s.at[j][pl.ds(0, G, stride=8), :] for j in range(nch)], axis=1)
    rowg = lax.broadcasted_iota(jnp.int32, (G, C), 0)
    s = 1
    while s < G:
        Gt = Gt + Ct * _shift_up(Gt, s, 0.0, rowg, G)
        Ct = Ct * _shift_up(Ct, s, 1.0, rowg, G)
        s *= 2
    gg = Gt + Ct * g_next
    cin = _shift_up(gg, 1, g_next, rowg, G)
    for j in range(nch):
        for r in range(8):
            c_s.at[j][pl.ds(r, G, stride=8), :] = cin[:, j * LANES:(j + 1) * LANES]
    return Gv + Cc * jnp.concatenate([c_s[j] for j in range(nch)], axis=1), gg[0:1, :]


def _lru_fwd(z_rx, conv_w, conv_b, wa, ba, wx, bx, lam, *, tt):
    S = z_rx.shape[0]
    n_t = S // tt
    BW = RNN_BLOCK_W
    W = LRU_NB * BW

    def body(x_ref, cw_ref, cb_ref, wa_ref, ba_ref, wx_ref, bx_ref, lam_ref, h_ref, tail, hc, a_s, b_s, c_s):
        t = pl.program_id(1)

        @pl.when(t == 0)
        def _():
            tail[...] = jnp.zeros((8, W), F32)
            hc[...] = jnp.zeros((8, W), F32)

        x = x_ref[...]
        before = tail[...]
        cw = cw_ref[...]
        xa = (cb_ref[...] + cw[3:4] * x + cw[2:3] * _rows_before(x, before, 1) + cw[1:2] * _rows_before(x, before, 2)
              + cw[0:1] * _rows_before(x, before, 3))
        tail[...] = x[tt - 8:tt, :]
        _r, i, _sp, a, mult = _lru_gates(xa, wa_ref, ba_ref, wx_ref, bx_ref, lam_ref[...])
        h, h_last = _scan_down(a, mult * (i * xa), hc[0:1, :], a_s, b_s, c_s)
        h_ref[...] = h
        hc[...] = jnp.broadcast_to(h_last, (8, W))

    blk = lambda n, t: (t, n)
    vec = pl.BlockSpec((1, W), lambda n, t: (0, n))
    mat = pl.BlockSpec((LRU_NB, BW, BW), lambda n, t: (n, 0, 0))
    bias = pl.BlockSpec((LRU_NB, 1, BW), lambda n, t: (n, 0, 0))
    row8 = pltpu.VMEM((8, W), F32)
    wide = pltpu.VMEM((LRU_NB, tt, LANES), F32)
    return pl.pallas_call(
        body, name="lru_fwd", grid=(RNN_BLOCKS // LRU_NB, n_t),
        in_specs=[pl.BlockSpec((tt, W), blk), pl.BlockSpec((CONV_WIDTH, W), lambda n, t: (0, n)), vec, mat, bias, mat,
                  bias, vec],
        out_specs=pl.BlockSpec((tt, W), blk), out_shape=jax.ShapeDtypeStruct((S, D_MODEL), F32),
        scratch_shapes=[row8, row8, wide, wide, wide],
        compiler_params=_params(("parallel", "arbitrary")),
    )(z_rx, conv_w, conv_b, wa, ba, wx, bx, lam)


def _lru_bwd(z_rx, h, dh, dz, conv_w, conv_b, wa, wat, ba, wx, wxt, bx, lam, *, tt):
    S = z_rx.shape[0]
    n_t = S // tt
    BW = RNN_BLOCK_W
    W = LRU_NB * BW
    t8 = tt // 8

    def body(x_ref, xp_ref, h_ref, hp_ref, dh_ref, _dz_ref, cw_ref, cb_ref, wa_ref, wat_ref, ba_ref, wx_ref, wxt_ref,
             bx_ref, lam_ref, dx_ref, dwa_ref, dwx_ref, dba_ref, dbx_ref, dlam_ref, dcw_ref, dcb_ref, nxt, a_c, g_c, a_s,
             b_s, c_s):
        t = pl.program_id(1)
        tile = n_t - 1 - t

        @pl.when(t == 0)
        def _():
            a_c[...] = jnp.zeros((8, W), F32)
            g_c[...] = jnp.zeros((8, W), F32)
            nxt[...] = jnp.zeros((8, W), F32)
            dwa_ref[...] = jnp.zeros_like(dwa_ref)
            dwx_ref[...] = jnp.zeros_like(dwx_ref)
            dba_ref[...] = jnp.zeros_like(dba_ref)
            dbx_ref[...] = jnp.zeros_like(dbx_ref)
            dlam_ref[...] = jnp.zeros_like(dlam_ref)
            dcw_ref[...] = jnp.zeros_like(dcw_ref)
            dcb_ref[...] = jnp.zeros_like(dcb_ref)

        has_prev = (tile > 0).astype(F32)
        x = x_ref[...]
        before = xp_ref[...] * has_prev
        xm1, xm2, xm3 = _rows_before(x, before, 1), _rows_before(x, before, 2), _rows_before(x, before, 3)
        cw = cw_ref[...]
        xa = cb_ref[...] + cw[3:4] * x + cw[2:3] * xm1 + cw[1:2] * xm2 + cw[0:1] * xm3
        lam = lam_ref[...]
        r, i, sp, a, mult = _lru_gates(xa, wa_ref, ba_ref, wx_ref, bx_ref, lam)
        gated = i * xa
        h_prev = _rows_before(h_ref[...], hp_ref[...] * has_prev, 1)
        g, g_first = _scan_up(_rows_after(a, a_c[...], 1), dh_ref[...], g_c[0:1, :], a_s, b_s, c_s)
        a_c[...] = jnp.broadcast_to(a[0:1, :], (8, W))
        g_c[...] = jnp.broadcast_to(g_first, (8, W))
        dlog_a = g * h_prev * a - g * gated * (a * a) / mult
        dgated = g * mult
        di = dgated * xa
        dxa = dgated * i
        dr = dlog_a * (-LRU_C * sp)
        dlam_ref[...] += jnp.sum(dlog_a * (-LRU_C * r), axis=0, keepdims=True) * (-_sigmoid(-lam))
        dpr = dr * r * (1.0 - r)
        dpi = di * i * (1.0 - i)
        xab, dprb, dpib = xa.astype(BF16), dpr.astype(BF16), dpi.astype(BF16)
        tn_dims = (((0,), (0,)), ((), ()))
        back = []
        for j in range(LRU_NB):
            sl = slice(j * BW, (j + 1) * BW)
            dwa_ref[j] += lax.dot_general(xab[:, sl], dprb[:, sl], tn_dims, preferred_element_type=F32)
            dwx_ref[j] += lax.dot_general(xab[:, sl], dpib[:, sl], tn_dims, preferred_element_type=F32)
            dba_ref[j] += jnp.sum(dpr[:, sl], axis=0, keepdims=True)
            dbx_ref[j] += jnp.sum(dpi[:, sl], axis=0, keepdims=True)
            back.append(jnp.dot(dprb[:, sl], wat_ref[j], preferred_element_type=F32)
                        + jnp.dot(dpib[:, sl], wxt_ref[j], preferred_element_type=F32))
        dxa = dxa + jnp.concatenate(back, axis=1)
        after = nxt[...]
        dx = (cw[3:4] * dxa + cw[2:3] * _rows_after(dxa, after, 1) + cw[1:2] * _rows_after(dxa, after, 2)
              + cw[0:1] * _rows_after(dxa, after, 3))
        nxt[...] = dxa[0:8, :]
        dx_ref[...] = dx.astype(BF16)
        dcw_ref[3:4, :] += jnp.sum(dxa * x, axis=0, keepdims=True)
        dcw_ref[2:3, :] += jnp.sum(dxa * xm1, axis=0, keepdims=True)
        dcw_ref[1:2, :] += jnp.sum(dxa * xm2, axis=0, keepdims=True)
        dcw_ref[0:1, :] += jnp.sum(dxa * xm3, axis=0, keepdims=True)
        dcb_ref[...] += jnp.sum(dxa, axis=0, keepdims=True)

    blk = lambda n, t: (n_t - 1 - t, n)
    prev = lambda n, t: (jnp.maximum((n_t - 1 - t) * t8 - 1, 0), n)
    vec = pl.BlockSpec((1, W), lambda n, t: (0, n))
    mat = pl.BlockSpec((LRU_NB, BW, BW), lambda n, t: (n, 0, 0))
    bias = pl.BlockSpec((LRU_NB, 1, BW), lambda n, t: (n, 0, 0))
    cws = pl.BlockSpec((CONV_WIDTH, W), lambda n, t: (0, n))
    tile = pl.BlockSpec((tt, W), blk)
    prev8 = pl.BlockSpec((8, W), prev)
    row8 = pltpu.VMEM((8, W), F32)
    wide = pltpu.VMEM((LRU_NB, tt, LANES), F32)
    return pl.pallas_call(
        body, name="lru_bwd", grid=(RNN_BLOCKS // LRU_NB, n_t),
        in_specs=[tile, prev8, tile, prev8, tile, pl.BlockSpec(memory_space=pl.ANY), cws, vec, mat, mat, bias, mat, mat,
                  bias, vec],
        out_specs=[tile, mat, mat, bias, bias, vec, cws, vec], input_output_aliases={5: 0},
        out_shape=[jax.ShapeDtypeStruct(dz.shape, BF16),
                   jax.ShapeDtypeStruct((RNN_BLOCKS, BW, BW), F32), jax.ShapeDtypeStruct((RNN_BLOCKS, BW, BW), F32),
                   jax.ShapeDtypeStruct((RNN_BLOCKS, 1, BW), F32), jax.ShapeDtypeStruct((RNN_BLOCKS, 1, BW), F32),
                   jax.ShapeDtypeStruct((1, D_MODEL), F32),
                   jax.ShapeDtypeStruct((CONV_WIDTH, D_MODEL), F32), jax.ShapeDtypeStruct((1, D_MODEL), F32)],
        scratch_shapes=[row8, row8, row8, wide, wide, wide],
        compiler_params=_params(("parallel", "arbitrary")),
    )(z_rx, z_rx, h, h, dh, dz, conv_w, conv_b, wa, wat, ba, wx, wxt, bx, lam)


def _mla_proj(z_ckv, q_norm, kv_norm, w_uq, w_ukv, cos, sin, *, ts):
    S = z_ckv.shape[0]
    H = MLA_HEADS

    def body(c_ref, qn_ref, kn_ref, wq_ref, wkv_ref, cos_ref, sin_ref, q_ref, k_ref, v_ref):
        c = c_ref[...]
        cqn, _ = _rms_fwd(c[:, 0:Q_LORA], qn_ref[...])
        ckn, _ = _rms_fwd(c[:, Q_LORA:Q_LORA + KV_LORA], kn_ref[...])
        q = jnp.dot(cqn.astype(BF16), wq_ref[...], preferred_element_type=F32) * (ATTN_SCALE * LOG2E)
        kv = jnp.dot(ckn.astype(BF16), wkv_ref[...], preferred_element_type=F32)
        cos1, sin1 = cos_ref[...], sin_ref[...]
        cos8 = jnp.concatenate([cos1] * H, axis=1)
        sin8 = jnp.concatenate([sin1] * H, axis=1)
        qr = q[:, H * QK_NOPE:]
        lane8 = lax.broadcasted_iota(jnp.int32, qr.shape, 1)
        qr = qr * cos8 + _rot_half(qr, lane8) * sin8
        kr = c[:, Q_LORA + KV_LORA:]
        lane1 = lax.broadcasted_iota(jnp.int32, kr.shape, 1)
        kr = (kr * cos1 + _rot_half(kr, lane1) * sin1).astype(BF16)
        for h in range(H):
            q_ref[h, :, 0:QK_NOPE] = q[:, h * QK_NOPE:(h + 1) * QK_NOPE].astype(BF16)
            q_ref[h, :, QK_NOPE:] = qr[:, h * LANES:(h + 1) * LANES].astype(BF16)
            k_ref[h, :, 0:QK_NOPE] = kv[:, h * 2 * LANES:h * 2 * LANES + LANES].astype(BF16)
            k_ref[h, :, QK_NOPE:] = kr
            v_ref[h] = kv[:, h * 2 * LANES + LANES:(h + 1) * 2 * LANES].astype(BF16)

    full = lambda shape: pl.BlockSpec(shape, lambda i: (0,) * len(shape))
    return pl.pallas_call(
        body, name="mla_proj", grid=(S // ts,),
        in_specs=[pl.BlockSpec((ts, CKV_W), lambda i: (i, 0)), full((1, Q_LORA)), full((1, KV_LORA)),
                  full(w_uq.shape), full(w_ukv.shape), pl.BlockSpec((ts, LANES), lambda i: (i, 0)),
                  pl.BlockSpec((ts, LANES), lambda i: (i, 0))],
        out_specs=[pl.BlockSpec((H, ts, QK_PAD), lambda i: (0, i, 0)), pl.BlockSpec((H, ts, QK_PAD), lambda i: (0, i, 0)),
                   pl.BlockSpec((H, ts, V_HEAD), lambda i: (0, i, 0))],
        out_shape=[jax.ShapeDtypeStruct((H, S, QK_PAD), BF16), jax.ShapeDtypeStruct((H, S, QK_PAD), BF16),
                   jax.ShapeDtypeStruct((H, S, V_HEAD), BF16)],
        compiler_params=_params(("parallel",)),
    )(z_ckv, q_norm, kv_norm, w_uq, w_ukv, cos, sin)


def _mla_proj_bwd(z_ckv, dq, dk, dv, q_norm, kv_norm, w_uqt, w_ukvt, cos, sin, *, ts):
    S = z_ckv.shape[0]
    H = MLA_HEADS

    def body(c_ref, dq_ref, dk_ref, dv_ref, qn_ref, kn_ref, wqt_ref, wkvt_ref, cos_ref, sin_ref,
             dz_ref, dwq_ref, dwkv_ref, dqn_ref, dkn_ref):
        @pl.when(pl.program_id(0) == 0)
        def _():
            dwq_ref[...] = jnp.zeros_like(dwq_ref)
            dwkv_ref[...] = jnp.zeros_like(dwkv_ref)
            dqn_ref[...] = jnp.zeros_like(dqn_ref)
            dkn_ref[...] = jnp.zeros_like(dkn_ref)

        c = c_ref[...]
        cq, ck = c[:, 0:Q_LORA], c[:, Q_LORA:Q_LORA + KV_LORA]
        qn, kn = qn_ref[...], kn_ref[...]
        cqn, _ = _rms_fwd(cq, qn)
        ckn, _ = _rms_fwd(ck, kn)
        cos1, sin1 = cos_ref[...], sin_ref[...]
        lane1 = lax.broadcasted_iota(jnp.int32, cos1.shape, 1)

        def unrope(g):
            return g * cos1 - _rot_half(g * sin1, lane1)

        dq_all = jnp.concatenate([dq_ref[h, :, 0:QK_NOPE] for h in range(H)]
                                 + [unrope(dq_ref[h, :, QK_NOPE:]) for h in range(H)], axis=1)
        dq_all = (dq_all * ATTN_SCALE).astype(BF16)
        dkv_all = jnp.concatenate([p for h in range(H) for p in (dk_ref[h, :, 0:QK_NOPE], dv_ref[h])],
                                  axis=1).astype(BF16)
        dkr = dk_ref[0, :, QK_NOPE:].astype(F32)
        for h in range(1, H):
            dkr = dkr + dk_ref[h, :, QK_NOPE:].astype(F32)
        dkr = unrope(dkr)
        tn_dims = (((0,), (0,)), ((), ()))
        dwq_ref[...] += lax.dot_general(cqn.astype(BF16), dq_all, tn_dims, preferred_element_type=F32)
        dwkv_ref[...] += lax.dot_general(ckn.astype(BF16), dkv_all, tn_dims, preferred_element_type=F32)
        dcqn = jnp.dot(dq_all, wqt_ref[...], preferred_element_type=F32)
        dckn = jnp.dot(dkv_all, wkvt_ref[...], preferred_element_type=F32)
        dcq, dqn_rows = _rms_bwd(dcqn, cq, qn)
        dck, dkn_rows = _rms_bwd(dckn, ck, kn)
        dqn_ref[...] += jnp.sum(dqn_rows, axis=0, keepdims=True)
        dkn_ref[...] += jnp.sum(dkn_rows, axis=0, keepdims=True)
        dz_ref[:, 0:Q_LORA] = dcq.astype(BF16)
        dz_ref[:, Q_LORA:Q_LORA + KV_LORA] = dck.astype(BF16)
        dz_ref[:, Q_LORA + KV_LORA:] = dkr.astype(BF16)

    full = lambda shape: pl.BlockSpec(shape, lambda i: (0,) * len(shape))
    return pl.pallas_call(
        body, name="mla_proj_bwd", grid=(S // ts,),
        in_specs=[pl.BlockSpec((ts, CKV_W), lambda i: (i, 0)), pl.BlockSpec((H, ts, QK_PAD), lambda i: (0, i, 0)),
                  pl.BlockSpec((H, ts, QK_PAD), lambda i: (0, i, 0)), pl.BlockSpec((H, ts, V_HEAD), lambda i: (0, i, 0)),
                  full((1, Q_LORA)), full((1, KV_LORA)), full(w_uqt.shape), full(w_ukvt.shape),
                  pl.BlockSpec((ts, LANES), lambda i: (i, 0)), pl.BlockSpec((ts, LANES), lambda i: (i, 0))],
        out_specs=[pl.BlockSpec((ts, CKV_W), lambda i: (i, 0)), full((Q_LORA, w_uqt.shape[0])),
                   full((KV_LORA, w_ukvt.shape[0])), full((1, Q_LORA)), full((1, KV_LORA))],
        out_shape=[jax.ShapeDtypeStruct((S, CKV_W), BF16), jax.ShapeDtypeStruct((Q_LORA, w_uqt.shape[0]), F32),
                   jax.ShapeDtypeStruct((KV_LORA, w_ukvt.shape[0]), F32), jax.ShapeDtypeStruct((1, Q_LORA), F32),
                   jax.ShapeDtypeStruct((1, KV_LORA), F32)],
        compiler_params=_params(("arbitrary",)),
    )(z_ckv, dq, dk, dv, q_norm, kv_norm, w_uqt, w_ukvt, cos, sin)


NT_DIMS = (((1,), (1,)), ((), ()))
TN_DIMS = (((0,), (0,)), ((), ()))


def _attn_fwd(q, k, v, *, t, hb):
    H, S, _ = q.shape
    n = S // t
    pairs = [(i, j) for i in range(n) for j in range(i + 1)]
    qi = jnp.asarray(np.array([p[0] for p in pairs], np.int32))
    ki = jnp.asarray(np.array([p[1] for p in pairs], np.int32))

    def body(qi_ref, ki_ref, q_ref, k_ref, v_ref, o_ref, lse_ref, m_s, l_s, acc_s):
        p = pl.program_id(1)
        i, j = qi_ref[p], ki_ref[p]

        @pl.when(j == 0)
        def _():
            m_s[...] = jnp.full(m_s.shape, NEG, F32)
            l_s[...] = jnp.zeros(l_s.shape, F32)
            acc_s[...] = jnp.zeros(acc_s.shape, F32)

        def block(hh, r0, nr, nk, masked):
            rows = slice(r0, r0 + nr)
            s = lax.dot_general(q_ref[hh, rows, :], k_ref[hh, 0:nk, :], NT_DIMS, preferred_element_type=F32)
            if masked:
                row = lax.broadcasted_iota(jnp.int32, (nr, nk), 0) + r0
                col = lax.broadcasted_iota(jnp.int32, (nr, nk), 1)
                s = jnp.where(row >= col, s, NEG)
            chunks = nk // LANES
            mc = s[:, 0:LANES]
            for c in range(1, chunks):
                mc = jnp.maximum(mc, s[:, c * LANES:(c + 1) * LANES])
            m_prev = m_s[hh, rows, :]
            m_new = jnp.maximum(m_prev, jnp.max(mc, axis=1, keepdims=True))
            alpha = jnp.exp2(m_prev - m_new)
            pr = jnp.exp2(s - jnp.concatenate([m_new] * chunks, axis=1))
            ls = pr[:, 0:LANES]
            for c in range(1, chunks):
                ls = ls + pr[:, c * LANES:(c + 1) * LANES]
            l_s[hh, rows, :] = alpha * l_s[hh, rows, :] + ls
            acc_s[hh, rows, :] = alpha * acc_s[hh, rows, :] + jnp.dot(pr.astype(BF16), v_ref[hh, 0:nk, :],
                                                                      preferred_element_type=F32)
            m_s[hh, rows, :] = m_new

        def step(diagonal):
            for hh in range(hb):
                if diagonal:
                    block(hh, 0, t // 2, t // 2, True)
                    block(hh, t // 2, t // 2, t, True)
                else:
                    block(hh, 0, t, t, False)

        @pl.when(j < i)
        def _():
            step(False)

        @pl.when(j == i)
        def _():
            step(True)
            for hh in range(hb):
                l = jnp.sum(l_s[hh], axis=1, keepdims=True)
                o_ref[:, hh * V_HEAD:(hh + 1) * V_HEAD] = acc_s[hh] / l
                lse_ref[hh] = (m_s[hh] + jnp.log2(l)).T[0:1, :]

    grid_spec = pltpu.PrefetchScalarGridSpec(
        num_scalar_prefetch=2, grid=(H // hb, len(pairs)),
        in_specs=[pl.BlockSpec((hb, t, QK_PAD), lambda h, p, qi, ki: (h, qi[p], 0)),
                  pl.BlockSpec((hb, t, QK_PAD), lambda h, p, qi, ki: (h, ki[p], 0)),
                  pl.BlockSpec((hb, t, V_HEAD), lambda h, p, qi, ki: (h, ki[p], 0))],
        out_specs=[pl.BlockSpec((t, hb * V_HEAD), lambda h, p, qi, ki: (qi[p], h)),
                   pl.BlockSpec((hb, 1, t), lambda h, p, qi, ki: (h, 0, qi[p]))],
        scratch_shapes=[pltpu.VMEM((hb, t, LANES), F32), pltpu.VMEM((hb, t, LANES), F32),
                        pltpu.VMEM((hb, t, V_HEAD), F32)],
    )
    return pl.pallas_call(
        body, name="attn_fwd", grid_spec=grid_spec,
        out_shape=[jax.ShapeDtypeStruct((S, H * V_HEAD), F32), jax.ShapeDtypeStruct((H, 1, S), F32)],
        compiler_params=_params(("parallel", "arbitrary")),
    )(qi, ki, q, k, v)


def _attn_bwd(q, k, v, do, lse_row, delta_row, *, t):
    H, S, _ = q.shape
    n = S // t
    pairs = [(i, j) for j in range(n) for i in range(j, n)]
    qi = jnp.asarray(np.array([p[0] for p in pairs], np.int32))
    ki = jnp.asarray(np.array([p[1] for p in pairs], np.int32))

    def body(qi_ref, ki_ref, q_ref, k_ref, v_ref, do_ref, lse_ref, dl_ref, dq_ref, dk_ref, dv_ref, dk_s, dv_s, dq_s):
        p = pl.program_id(1)
        i, j = qi_ref[p], ki_ref[p]

        @pl.when(p == 0)
        def _():
            dq_s[...] = jnp.zeros_like(dq_s)

        def block(k0, nk, q0, nq, masked):
            qb, dob = q_ref[0, q0:q0 + nq, :], do_ref[q0:q0 + nq, :]
            kb, vb = k_ref[0, k0:k0 + nk, :], v_ref[0, k0:k0 + nk, :]
            st = lax.dot_general(kb, qb, NT_DIMS, preferred_element_type=F32)
            if masked:
                krow = lax.broadcasted_iota(jnp.int32, (nk, nq), 0) + k0
                qcol = lax.broadcasted_iota(jnp.int32, (nk, nq), 1) + q0
                st = jnp.where(krow <= qcol, st, NEG)
            pt = jnp.exp2(st - lse_ref[0][:, q0:q0 + nq])
            dvp = jnp.dot(pt.astype(BF16), dob, preferred_element_type=F32)
            dpt = lax.dot_general(vb, dob, NT_DIMS, preferred_element_type=F32)
            dst = (pt * (dpt - dl_ref[0][:, q0:q0 + nq])).astype(BF16)
            dkp = jnp.dot(dst, qb, preferred_element_type=F32)
            rows = pl.ds(pl.multiple_of(i * t + q0, LANES), nq)
            dq_s[rows, :] += lax.dot_general(dst, kb, TN_DIMS, preferred_element_type=F32)
            return dkp, dvp

        @pl.when(i == j)
        def _():
            half = t // 2
            dk_s[0:half, :], dv_s[0:half, :] = block(0, half, 0, t, True)
            dk_s[half:t, :], dv_s[half:t, :] = block(half, half, half, half, True)

        @pl.when(i != j)
        def _():
            dkp, dvp = block(0, t, 0, t, False)
            dk_s[...] += dkp
            dv_s[...] += dvp

        @pl.when(i == n - 1)
        def _():
            dk_ref[0] = (dk_s[...] * LN2).astype(BF16)
            dv_ref[0] = dv_s[...].astype(BF16)

        @pl.when(p == len(pairs) - 1)
        def _():
            dq_ref[0] = dq_s[...].astype(BF16)

    grid_spec = pltpu.PrefetchScalarGridSpec(
        num_scalar_prefetch=2, grid=(H, len(pairs)),
        in_specs=[pl.BlockSpec((1, t, QK_PAD), lambda h, p, qi, ki: (h, qi[p], 0)),
                  pl.BlockSpec((1, t, QK_PAD), lambda h, p, qi, ki: (h, ki[p], 0)),
                  pl.BlockSpec((1, t, V_HEAD), lambda h, p, qi, ki: (h, ki[p], 0)),
                  pl.BlockSpec((t, V_HEAD), lambda h, p, qi, ki: (qi[p], h)),
                  pl.BlockSpec((1, 1, t), lambda h, p, qi, ki: (h, 0, qi[p])),
                  pl.BlockSpec((1, 1, t), lambda h, p, qi, ki: (h, 0, qi[p]))],
        out_specs=[pl.BlockSpec((1, S, QK_PAD), lambda h, p, qi, ki: (h, 0, 0)),
                   pl.BlockSpec((1, t, QK_PAD), lambda h, p, qi, ki: (h, ki[p], 0)),
                   pl.BlockSpec((1, t, V_HEAD), lambda h, p, qi, ki: (h, ki[p], 0))],
        scratch_shapes=[pltpu.VMEM((t, QK_PAD), F32), pltpu.VMEM((t, V_HEAD), F32), pltpu.VMEM((S, QK_PAD), F32)],
    )
    return pl.pallas_call(
        body, name="attn_bwd", grid_spec=grid_spec,
        out_shape=[jax.ShapeDtypeStruct((H, S, QK_PAD), BF16), jax.ShapeDtypeStruct((H, S, QK_PAD), BF16),
                   jax.ShapeDtypeStruct((H, S, V_HEAD), BF16)],
        compiler_params=_params(("parallel", "arbitrary")),
    )(qi, ki, q, k, v, do, lse_row, delta_row)


def _merge_h1(h, z_gates, o, x, w_out, norm_mlp, *, ts):
    S = h.shape[0]
    D = D_MODEL

    def body(h_ref, rg_ref, ga_ref, gb_ref, o_ref, x_ref, w_ref, g_ref, m_ref, h1_ref, n2_ref):
        gl, _ = _gelu_and_grad(rg_ref[...].astype(F32))
        m = (_sigmoid(ga_ref[...].astype(F32)) * (h_ref[...] * gl)
             + _sigmoid(gb_ref[...].astype(F32)) * o_ref[...]).astype(BF16)
        m_ref[...] = m
        h1 = x_ref[...] + jnp.dot(m, w_ref[...], preferred_element_type=F32)
        h1_ref[...] = h1
        n2, _ = _rms_fwd(h1, g_ref[...])
        n2_ref[...] = n2.astype(BF16)

    col = lambda c: pl.BlockSpec((ts, D), lambda i: (i, c))
    fixed = lambda shape: pl.BlockSpec(shape, lambda i: (0, 0), pipeline_mode=pl.Buffered(1))
    return pl.pallas_call(
        body, name="merge_h1", grid=(S // ts,),
        in_specs=[col(0), col(0), col(1), col(2), col(0), col(0), fixed((D, D)), fixed((1, D))],
        out_specs=[col(0), col(0), col(0)],
        out_shape=[jax.ShapeDtypeStruct((S, D), BF16), jax.ShapeDtypeStruct((S, D), F32),
                   jax.ShapeDtypeStruct((S, D), BF16)],
        compiler_params=_params(("parallel",)),
    )(h, z_gates, z_gates, z_gates, o, x, w_out, norm_mlp)


def _my_place():
    return lax.axis_index("x"), lax.axis_index("y"), lax.axis_index("c")


def _all_gather(shards, *, name):
    n = len(shards)

    def body(*refs):
        x_refs, out_refs = refs[:n], refs[n:2 * n]
        send_sems, recv_sems, local_sems = refs[2 * n:]
        x, y, c = _my_place()
        me, sibling = (x, y, c), (x, y, 1 - c)
        chips = [(1 - x, y), (x, 1 - y), (1 - x, 1 - y)]

        def slot(a, px, py, pc):
            return out_refs[a].at[4 * px + 2 * py + pc]

        def copy(a, k, block, to, src=None):
            return pltpu.make_async_remote_copy(
                src_ref=slot(a, *block) if src is None else src, dst_ref=slot(a, *block),
                send_sem=send_sems.at[7 * a + k], recv_sem=recv_sems.at[7 * a + k], device_id=to, device_id_type=MESH)

        mine = [pltpu.make_async_copy(x_refs[a], slot(a, *me), local_sems.at[a]) for a in range(n)]
        for cp in mine:
            cp.start()
        first = []
        for a in range(n):
            first.append(copy(a, 0, me, sibling, src=x_refs[a]))
            first += [copy(a, 1 + j, me, (*chip, c), src=x_refs[a]) for j, chip in enumerate(chips)]
        for cp in first:
            cp.start()
        passed = []
        for a in range(n):
            for j, chip in enumerate(chips):
                copy(a, 1 + j, (*chip, c), me).wait_recv()
                fwd = copy(a, 4 + j, (*chip, c), sibling)
                fwd.start()
                passed.append(fwd)
        for a in range(n):
            copy(a, 0, sibling, me).wait_recv()
            for j, chip in enumerate(chips):
                copy(a, 4 + j, (*chip, 1 - c), me).wait_recv()
        for cp in first + passed:
            cp.wait_send()
        for cp in mine:
            cp.wait()

    hbm = pl.BlockSpec(memory_space=pl.ANY)
    return pl.pallas_call(
        body, name=name, out_shape=[jax.ShapeDtypeStruct((N_DEV, *s.shape), s.dtype) for s in shards],
        in_specs=[hbm] * n, out_specs=[hbm] * n,
        scratch_shapes=[pltpu.SemaphoreType.DMA((7 * n,)), pltpu.SemaphoreType.DMA((7 * n,)),
                        pltpu.SemaphoreType.DMA((n,))],
    )(*shards)


def _pushes(src_refs, land_refs, send_sems, recv_sems, slab_per_peer):
    x, y, c = _my_place()
    me = 4 * x + 2 * y + c
    copies = []
    for a in range(len(src_refs)):
        for k in range(1, N_DEV):
            px, py, pc = x ^ (k >> 2), y ^ ((k >> 1) & 1), c ^ (k & 1)
            src = src_refs[a].at[4 * px + 2 * py + pc] if slab_per_peer else src_refs[a]
            copies.append(pltpu.make_async_remote_copy(
                src_ref=src, dst_ref=land_refs[a].at[me], send_sem=send_sems.at[7 * a + k - 1],
                recv_sem=recv_sems.at[7 * a + k - 1], device_id=(px, py, pc), device_id_type=MESH))
    return copies


def _push_start(srcs, *, name, slab_per_peer):
    n = len(srcs)
    lands = [lax.empty((N_DEV, *(s.shape[1:] if slab_per_peer else s.shape)), s.dtype) for s in srcs]

    def body(*refs):
        src_refs, land_refs = refs[:n], refs[n:2 * n]
        send_sems, recv_sems, token = refs[2 * n], refs[2 * n + 1], refs[-1]
        for cp in _pushes(src_refs, land_refs, send_sems, recv_sems, slab_per_peer):
            cp.start()
        token[...] = jnp.zeros_like(token)

    hbm = pl.BlockSpec(memory_space=pltpu.HBM)
    sem = pl.BlockSpec(memory_space=pltpu.SEMAPHORE)
    out = pl.pallas_call(
        body, name=name,
        out_shape=(pltpu.SemaphoreType.DMA((7 * n,)), pltpu.SemaphoreType.DMA((7 * n,)),
                   *[pltpu.HBM(a.shape, a.dtype) for a in srcs + lands], jax.ShapeDtypeStruct((8, LANES), F32)),
        in_specs=[hbm] * (2 * n), out_specs=(sem, sem, *[hbm] * (2 * n), pl.BlockSpec(memory_space=pltpu.VMEM)),
        input_output_aliases={i: 2 + i for i in range(2 * n)},
        compiler_params=pltpu.CompilerParams(has_side_effects=pltpu.SideEffectType.DATAFLOW_SIDE_EFFECTING),
    )(*[pltpu.with_memory_space_constraint(a, pltpu.HBM) for a in srcs + lands])
    return out[0], out[1], list(out[2:2 + n]), list(out[2 + n:2 + 2 * n]), out[-1]


def _push_wait(send_sems, recv_sems, srcs, lands, after, *, name, slab_per_peer):
    n = len(srcs)

    def body(*refs):
        src_refs, land_refs = refs[:n], refs[n:2 * n]
        s_sems, r_sems = refs[2 * n], refs[2 * n + 1]
        for cp in _pushes(src_refs, land_refs, s_sems, r_sems, slab_per_peer):
            cp.wait_send()
            cp.wait_recv()

    hbm = pl.BlockSpec(memory_space=pltpu.HBM)
    sem = pl.BlockSpec(memory_space=pltpu.SEMAPHORE)
    out = pl.pallas_call(
        body, name=name, out_shape=tuple(pltpu.HBM(a.shape, a.dtype) for a in srcs + lands),
        in_specs=[hbm] * (2 * n) + [sem, sem, pl.BlockSpec(memory_space=pl.ANY)], out_specs=tuple([hbm] * (2 * n)),
        input_output_aliases={i: i for i in range(2 * n)},
        compiler_params=pltpu.CompilerParams(has_side_effects=pltpu.SideEffectType.DATAFLOW_SIDE_EFFECTING),
    )(*srcs, *lands, send_sems, recv_sems, after)
    return list(out[:n]), list(out[n:])


def _sum_parts(gp_ref, rows):
    g = gp_ref[0, 0:rows, :].astype(F32)
    for p in range(1, gp_ref.shape[0]):
        g = g + gp_ref[p, 0:rows, :].astype(F32)
    return g


def _adamw_update(w, m, v, g):
    m_new = ADAM_B1 * m + (1.0 - ADAM_B1) * g
    v_new = ADAM_B2 * v + (1.0 - ADAM_B2) * (g * g)
    m_hat = m_new / (1.0 - ADAM_B1 ** ADAM_STEP)
    v_hat = v_new / (1.0 - ADAM_B2 ** ADAM_STEP)
    return -ADAM_LR * (m_hat / (jnp.sqrt(v_hat) + ADAM_EPS) + ADAM_WD * w), m_new, v_new


def _adamw_many(ws, ms, vs, gparts, sums, *, name):
    n, k = len(ws), len(sums)

    def body(*refs):
        w_refs, m_refs, v_refs = refs[:n], refs[n:2 * n], refs[2 * n:3 * n]
        g_refs, s_refs, outs = refs[3 * n:4 * n], refs[4 * n:4 * n + k], refs[4 * n + k:]
        for a in range(n):
            g = _sum_parts(g_refs[a], w_refs[a].shape[0])
            d, m_new, v_new = _adamw_update(w_refs[a][...], m_refs[a][...], v_refs[a][...], g)
            for o_ref, val in zip(outs[4 * a:4 * a + 4], (g, d, m_new, v_new)):
                o_ref[...] = val
        for b in range(k):
            outs[4 * n + b][...] = _sum_parts(s_refs[b], s_refs[b].shape[1])

    out_shape = [jax.ShapeDtypeStruct(w.shape, F32) for w in ws for _ in range(4)]
    out_shape += [jax.ShapeDtypeStruct(s.shape[1:], F32) for s in sums]
    return pl.pallas_call(body, name=name, out_shape=out_shape, compiler_params=_params())(
        *ws, *ms, *vs, *gparts, *sums)


def _adamw(w, m, v, gparts, *, tr, name):
    R, C = w.shape
    n_parts = gparts.shape[0]

    def body(w_ref, m_ref, v_ref, gp_ref, g_ref, d_ref, nm_ref, nv_ref):
        g = _sum_parts(gp_ref, tr)
        d_ref[...], nm_ref[...], nv_ref[...] = _adamw_update(w_ref[...], m_ref[...], v_ref[...], g)
        g_ref[...] = g

    row = pl.BlockSpec((tr, C), lambda i: (i, 0))
    shp = jax.ShapeDtypeStruct((R, C), F32)
    return pl.pallas_call(
        body, name=name, grid=(R // tr,),
        in_specs=[row, row, row, pl.BlockSpec((n_parts, tr, C), lambda i: (0, i, 0))],
        out_specs=[row, row, row, row], out_shape=[shp, shp, shp, shp],
        compiler_params=_params(("parallel",)),
    )(w, m, v, gparts)


def _rope_tables(s):
    pos = jnp.arange(s, dtype=F32)
    inv_freq = 1.0 / (ROPE_THETA ** (jnp.arange(0, QK_ROPE, 2, dtype=F32) / QK_ROPE))
    ang = pos[:, None] * inv_freq[None, :]
    cos, sin = jnp.cos(ang), jnp.sin(ang)
    zero = jnp.zeros((s, LANES - QK_ROPE), F32)
    return jnp.concatenate([cos, cos, zero], -1), jnp.concatenate([sin, sin, zero], -1)


def _pick(n, want):
    t = min(n, want)
    assert n % t == 0
    return t


def _local_step(x, target, wts, small, hooks):
    S = x.shape[0]
    H = MLA_HEADS
    ts = _pick(S, 512)
    tm = _pick(S, 512)
    tm_wide = _pick(S, 1024)
    tk_s = _pick(S, 2048)
    row = lambda v: v.reshape(1, -1)
    w_in = wts["w_in"]
    w_main = jnp.concatenate([w_in[:, 0:2048], w_in[:, 2624:4672]], axis=1)
    w_ckv = jnp.concatenate([w_in[:, 2048:2624], jnp.zeros((D_MODEL, CKV_W - 576), BF16)], axis=1)
    w_uq3 = wts["w_uq"].reshape(Q_LORA, H, QK_NOPE + QK_ROPE)
    w_uq_p = jnp.concatenate(
        [w_uq3[:, :, :QK_NOPE].reshape(Q_LORA, H * QK_NOPE),
         jnp.pad(w_uq3[:, :, QK_NOPE:], ((0, 0), (0, 0), (0, LANES - QK_ROPE))).reshape(Q_LORA, H * LANES)], axis=1)
    w_ukv = wts["w_ukv"]
    cos, sin = _rope_tables(S)
    conv_w, conv_b = small["conv_w"], row(small["conv_b"])
    wa, wx = small["lru_wa"].astype(BF16), small["lru_wx"].astype(BF16)
    wat, wxt = jnp.swapaxes(wa, 1, 2), jnp.swapaxes(wx, 1, 2)
    ba, bx = small["lru_ba"].reshape(RNN_BLOCKS, 1, RNN_BLOCK_W), small["lru_bx"].reshape(RNN_BLOCKS, 1, RNN_BLOCK_W)
    lam = row(small["lru_lambda"])
    q_norm, kv_norm = row(small["q_norm"]), row(small["kv_norm"])
    norm_mix, norm_mlp, norm_final = row(small["norm_mix"]), row(small["norm_mlp"]), row(small["norm_final"])

    xn = _rmsnorm_cast(x, norm_mix, ts=ts, name="norm_mix")
    ident = lambda acc: (acc,)
    (z_rx,) = _mm(xn, w_main[:, :D_MODEL], name="z_rx", tm=tm_wide, tn=1024, tk=1024, outs=[("tile", F32)],
                  epilogue=ident)
    (z_gates,) = _mm(xn, w_main[:, D_MODEL:], name="z_gates", tm=tm_wide, tn=1024, tk=1024, outs=[("tile", BF16)],
                     epilogue=ident)
    (z_ckv,) = _mm(xn, w_ckv, name="z_ckv", tm=tm, tn=CKV_W, tk=1024, outs=[("tile", F32)], epilogue=ident)
    tt = _pick(S, 256)
    h = _lru_fwd(z_rx, conv_w, conv_b, wa, ba, wx, bx, lam, tt=tt)
    q, k, v = _mla_proj(z_ckv, q_norm, kv_norm, w_uq_p, w_ukv, cos, sin, ts=_pick(S, 256))
    ta = _pick(S, 1024)
    o, lse = _attn_fwd(q, k, v, t=ta, hb=2)
    w_out, w_up, w_down = hooks["weights_later"](o)
    merged, h1, n2 = _merge_h1(h, z_gates, o, x, w_out, norm_mlp, ts=_pick(S, 256))

    def ep_up(acc):
        r = jnp.maximum(acc, 0.0)
        return r * r, r

    act, relu = _mm(n2, w_up, name="up", tm=tm_wide, tn=2048, tk=1024, outs=[("tile", BF16), ("tile", BF16)],
                    epilogue=ep_up)

    def ep_loss(acc, h1v, tgt, g):
        h2 = acc + h1v
        y, _ = _rms_fwd(h2, g)
        err = y - tgt
        loss_rows = 0.5 * jnp.mean(err * err, axis=-1, keepdims=True)
        dy = err * (1.0 / D_MODEL)
        dh2, dg_rows = _rms_bwd(dy, h2, g)
        lsum = jnp.sum(loss_rows, axis=0, keepdims=True)
        return dh2, dh2, jnp.sum(dg_rows, axis=0, keepdims=True), jnp.broadcast_to(lsum, (1, D_MODEL))

    dh2, dh2b, dnf_p, loss_p = _mm(
        act, w_down, name="down_loss", tm=tm, tn=1024, tk=D_FF,
        outs=[("tile", F32), ("tile", BF16), ("rowpart", F32), ("rowpart", F32)], epilogue=ep_loss,
        extras=[("tile", h1), ("tile", target), ("row", norm_final)])
    loss_part = jnp.sum(loss_p[:, 0, 0])
    d_norm_final = jnp.sum(dnf_p, axis=(0, 1))

    def ep_du(acc, r):
        return (acc * (2.0 * r.astype(F32)),)

    (du,) = _mm(dh2b, w_down, name="d_act", tb=True, tm=tm_wide, tn=2048, tk=1024, outs=[("tile", BF16)], epilogue=ep_du,
                extras=[("tile", relu)])

    def ep_dh1(acc, h1v, dh2v, g):
        dv, dg_rows = _rms_bwd(acc, h1v, g)
        dh1 = dh2v + dv
        return dh1, dh1, jnp.sum(dg_rows, axis=0, keepdims=True)

    dh1, dh1b, dnm_p = _mm(du, w_up, name="d_n2", tb=True, tm=tm, tn=1024, tk=D_FF,
                           outs=[("tile", F32), ("tile", BF16), ("rowpart", F32)], epilogue=ep_dh1,
                           extras=[("tile", h1), ("tile", dh2), ("row", norm_mlp)])
    d_norm_mlp = jnp.sum(dnm_p, axis=(0, 1))
    tn_mm = functools.partial(_mm, ta=True, tk=tk_s, outs=[("tile", BF16)], epilogue=ident)
    (d_w_down,) = tn_mm(act, dh2b, name="dw_down", tm=1024, tn=1024)
    (p_w_up,) = _mm(n2, du, name="dw_up", ta=True, tk=tk_s, tm=1024, tn=D_FF // N_DEV, outs=[("colshard", BF16)],
                    epilogue=ident)
    (d_w_out,) = tn_mm(merged, dh1b, name="dw_out", tm=1024, tn=1024)
    early = [d_w_out.reshape(N_DEV, -1, D_MODEL), p_w_up, d_w_down.reshape(N_DEV, -1, D_MODEL)]
    w_out = w_out + hooks["send"]("early", early)[0, 0].astype(BF16)

    tmm = _pick(S, 256)

    def ep_dmerge(dm, hv, rg, ga, gb, ov):
        rg, ga, gb = rg.astype(F32), ga.astype(F32), gb.astype(F32)
        gl, dgl = _gelu_and_grad(rg)
        sa, sb = _sigmoid(ga), _sigmoid(gb)
        ya = hv * gl
        dya = dm * sa
        do = dm * sb
        dga = dm * ya * sa * (1.0 - sa)
        dgb = dm * ov * sb * (1.0 - sb)
        dh = dya * gl
        drg = dya * hv * dgl
        dov = do * ov
        lane = lax.broadcasted_iota(jnp.int32, (dm.shape[0], LANES), 1)
        delta = jnp.zeros((dm.shape[0], LANES), F32)
        for hh in range(H):
            dsum = jnp.sum(dov[:, hh * V_HEAD:(hh + 1) * V_HEAD], axis=1, keepdims=True)
            delta = jnp.where(lane == hh, dsum, delta)
        return dh, jnp.concatenate([drg, dga, dgb], axis=1), do, delta

    dh_lru, dz_part, do, delta_w = _mm(
        dh1b, w_out, name="d_merge", tb=True, tm=tmm, tn=1024, tk=1024,
        outs=[("tile", F32), ("cols", BF16, 4 * D_MODEL, D_MODEL), ("tile", BF16), ("side", F32)],
        epilogue=ep_dmerge,
        extras=[("tile", h), ("tilecol", z_gates, 0), ("tilecol", z_gates, 1), ("tilecol", z_gates, 2), ("tile", o)])
    delta_row = delta_w[:, :H].T.reshape(H, 1, S)
    lse_row = lse

    dq, dk, dv = _attn_bwd(q, k, v, do, lse_row, delta_row, t=ta)
    dz_ckv, d_w_uq_p, d_w_ukv, d_q_norm, d_kv_norm = _mla_proj_bwd(
        z_ckv, dq, dk, dv, q_norm, kv_norm, w_uq_p.T, w_ukv.T, cos, sin, ts=_pick(S, 256))
    d_w_uq = jnp.concatenate(
        [d_w_uq_p[:, :H * QK_NOPE].reshape(Q_LORA, H, QK_NOPE),
         d_w_uq_p[:, H * QK_NOPE:].reshape(Q_LORA, H, LANES)[:, :, :QK_ROPE]], axis=2).reshape(Q_LORA, -1)

    dz_main, d_wa, d_wx, d_ba, d_bx, d_lam, d_conv_w, d_conv_b = _lru_bwd(
        z_rx, h, dh_lru, dz_part, conv_w, conv_b, wa, wat, ba, wx, wxt, bx, lam, tt=tt)

    (d_w_main,) = tn_mm(xn, dz_main, name="dw_main", tm=1024, tn=1024)
    (d_w_ckv,) = tn_mm(xn, dz_ckv, name="dw_ckv", tm=1024, tn=CKV_W)
    d_w_in = jnp.concatenate([d_w_main[:, 0:2048], d_w_ckv[:, 0:576], d_w_main[:, 2048:4096]], axis=1)

    def col_parts(full):
        r = full.shape[0]
        return jnp.transpose(full.astype(BF16).reshape(r, N_DEV, -1), (1, 0, 2))

    late = [col_parts(d_w_in), col_parts(d_w_uq), col_parts(d_w_ukv)]
    norm_mix = norm_mix + hooks["send"]("late", late)[0, 0]

    def ep_dx(acc, xv, dh1v, g):
        dv, dg_rows = _rms_bwd(acc, xv, g)
        return dh1v + dv, jnp.sum(dg_rows, axis=0, keepdims=True)

    grad_x, dnx_p = _mm(dz_main, w_main, name="dx", tb=True, tm=tm, tn=1024, tk=4 * D_MODEL,
                        outs=[("tile", F32), ("rowpart", F32)], epilogue=ep_dx, more=(dz_ckv, w_ckv),
                        extras=[("tile", x), ("tile", dh1), ("row", norm_mix)])
    d_norm_mix = jnp.sum(dnx_p, axis=(0, 1))
    sm = {"norm_mix": d_norm_mix, "conv_w": d_conv_w, "conv_b": d_conv_b.reshape(-1), "lru_wa": d_wa,
          "lru_ba": d_ba.reshape(RNN_BLOCKS, RNN_BLOCK_W), "lru_wx": d_wx, "lru_bx": d_bx.reshape(RNN_BLOCKS, RNN_BLOCK_W),
          "lru_lambda": d_lam.reshape(-1), "q_norm": d_q_norm.reshape(-1), "kv_norm": d_kv_norm.reshape(-1),
          "norm_mlp": d_norm_mlp, "norm_final": d_norm_final}
    return loss_part, grad_x, sm


BIG = ("w_in", "w_uq", "w_ukv", "w_out", "w_up", "w_down")
SMALL = ("norm_mix", "conv_b", "lru_wa", "lru_ba", "lru_wx", "lru_bx", "lru_lambda", "q_norm", "kv_norm", "norm_mlp",
         "norm_final")
WEIGHTS = ("norm_mix", "w_in", "conv_w", "conv_b", "lru_wa", "lru_ba", "lru_wx", "lru_bx", "lru_lambda", "q_norm", "w_uq",
           "kv_norm", "w_ukv", "w_out", "norm_mlp", "w_up", "w_down", "norm_final")
ADAM_TILE_ROWS = {"w_in": 256, "w_uq": 128, "w_ukv": 128, "w_out": 64, "w_up": 256, "w_down": 128}
CONV_ROWS = N_DEV * 8


def _rows(a):
    return a.reshape(-1, LANES)


def _pad_rows(a, mult):
    r = a.shape[-2]
    pad = (-r) % mult
    if pad == 0:
        return a
    cfg = [(0, 0)] * (a.ndim - 2) + [(0, pad), (0, 0)]
    return jnp.pad(a, cfg)


def _cols_from_shards(g):
    return jnp.transpose(g, (1, 0, 2)).reshape(g.shape[1], -1)


def kernel(x, norm_mix, w_in, conv_w, conv_b, lru_wa, lru_ba, lru_wx, lru_bx, lru_lambda, q_norm, w_uq, kv_norm, w_ukv, w_out, norm_mlp, w_up, w_down, norm_final, loss_target, m_norm_mix, m_w_in, m_conv_w, m_conv_b, m_lru_wa, m_lru_ba, m_lru_wx, m_lru_bx, m_lru_lambda, m_q_norm, m_w_uq, m_kv_norm, m_w_ukv, m_w_out, m_norm_mlp, m_w_up, m_w_down, m_norm_final, v_norm_mix, v_w_in, v_conv_w, v_conv_b, v_lru_wa, v_lru_ba, v_lru_wx, v_lru_bx, v_lru_lambda, v_q_norm, v_w_uq, v_kv_norm, v_w_ukv, v_w_out, v_norm_mlp, v_w_up, v_w_down, v_norm_final):
    W = dict(norm_mix=norm_mix, w_in=w_in, conv_w=conv_w, conv_b=conv_b, lru_wa=lru_wa, lru_ba=lru_ba, lru_wx=lru_wx,
             lru_bx=lru_bx, lru_lambda=lru_lambda, q_norm=q_norm, w_uq=w_uq, kv_norm=kv_norm, w_ukv=w_ukv, w_out=w_out,
             norm_mlp=norm_mlp, w_up=w_up, w_down=w_down, norm_final=norm_final)
    M = dict(norm_mix=m_norm_mix, w_in=m_w_in, conv_w=m_conv_w, conv_b=m_conv_b, lru_wa=m_lru_wa, lru_ba=m_lru_ba,
             lru_wx=m_lru_wx, lru_bx=m_lru_bx, lru_lambda=m_lru_lambda, q_norm=m_q_norm, w_uq=m_w_uq, kv_norm=m_kv_norm,
             w_ukv=m_w_ukv, w_out=m_w_out, norm_mlp=m_norm_mlp, w_up=m_w_up, w_down=m_w_down, norm_final=m_norm_final)
    V = dict(norm_mix=v_norm_mix, w_in=v_w_in, conv_w=v_conv_w, conv_b=v_conv_b, lru_wa=v_lru_wa, lru_ba=v_lru_ba,
             lru_wx=v_lru_wx, lru_bx=v_lru_bx, lru_lambda=v_lru_lambda, q_norm=v_q_norm, w_uq=v_w_uq, kv_norm=v_kv_norm,
             w_ukv=v_w_ukv, w_out=v_w_out, norm_mlp=v_norm_mlp, w_up=v_w_up, w_down=v_w_down, norm_final=v_norm_final)
    me = 4 * lax.axis_index("x") + 2 * lax.axis_index("y") + lax.axis_index("c")

    first, later = ("w_in", "w_uq", "w_ukv"), ("w_out", "w_up", "w_down")
    got = _all_gather([W[n].astype(BF16) for n in first] + [_pad_rows(conv_w, 8)], name="gather_weights")
    wts = {"w_in": _cols_from_shards(got[0]), "w_uq": _cols_from_shards(got[1]), "w_ukv": _cols_from_shards(got[2])}
    w_send, w_recv, w_src, w_land, zeros = _push_start([W[n].astype(BF16) for n in later], name="gather_later_start",
                                                       slab_per_peer=False)
    small = {n: W[n] for n in SMALL}
    small["conv_w"] = _cols_from_shards(got[3][:, :CONV_WIDTH])
    small["norm_mix"] = norm_mix + zeros[0, 0]

    def with_own_slab(land, mine):
        return lax.dynamic_update_slice(land, mine, (me, 0, 0))

    def weights_later(after):
        srcs, lands = _push_wait(w_send, w_recv, w_src, w_land, after, name="gather_later_wait", slab_per_peer=False)
        w_out_g, w_up_g, w_down_g = [with_own_slab(l, s[None]) for l, s in zip(lands, srcs)]
        return w_out_g.reshape(-1, D_MODEL), _cols_from_shards(w_up_g), w_down_g.reshape(-1, D_MODEL)

    sent = {}
    G, Dl, NM, NV = {}, {}, {}, {}

    def finish(group, names, after):
        s_sems, r_sems, srcs, lands, _ = sent[group]
        srcs, lands = _push_wait(s_sems, r_sems, srcs, lands, after, name="exchange_" + group + "_wait",
                                 slab_per_peer=True)
        for n, src, land in zip(names, srcs, lands):
            parts = with_own_slab(land, lax.dynamic_slice(src, (me, 0, 0), (1, *src.shape[1:])))
            G[n], Dl[n], NM[n], NV[n] = _adamw(W[n], M[n], V[n], parts, tr=ADAM_TILE_ROWS[n], name="adamw_" + n)

    def send(group, parts):
        sent[group] = _push_start(parts, name="exchange_" + group + "_start", slab_per_peer=True)
        zeros = sent[group][4]
        if group == "late":
            finish("early", later, zeros)
            zeros = zeros + 0.0 * (Dl["w_out"][0:8, 0:LANES] + Dl["w_up"][0:8, 0:LANES] + Dl["w_down"][0:8, 0:LANES])
        return zeros

    loss_part, grad_x, g_small = _local_step(x[0], loss_target[0], wts, small,
                                              {"weights_later": weights_later, "send": send})
    finish("late", first, grad_x)

    conv_rows = _pad_rows(jnp.transpose(g_small["conv_w"].reshape(CONV_WIDTH, N_DEV, LANES), (1, 0, 2)), 8)
    loss_rows = jnp.zeros((8, LANES), F32).at[0, 0].set(loss_part)
    as_sent = lambda n: g_small[n].astype(BF16) if n in ("lru_wa", "lru_wx") else g_small[n]
    gathered = _all_gather([_pad_rows(_rows(as_sent(n)), 8) for n in SMALL]
                           + [conv_rows.reshape(CONV_ROWS, LANES), loss_rows], name="gather_small")
    k = len(SMALL)
    outs = _adamw_many([_rows(W[n]) for n in SMALL], [_rows(M[n]) for n in SMALL], [_rows(V[n]) for n in SMALL],
                       gathered[:k], gathered[k:], name="adamw_small")
    for j, n in enumerate(SMALL):
        for out, o in zip((G, Dl, NM, NV), outs[4 * j:4 * j + 4]):
            out[n] = o.reshape(W[n].shape)
    conv_sum, loss_sum = outs[4 * k:]
    loss = loss_sum[0, 0]

    g_conv = lax.dynamic_slice(conv_sum, (me * 8, 0), (8, LANES))
    conv_out = _adamw(_pad_rows(conv_w, 8), _pad_rows(m_conv_w, 8), _pad_rows(v_conv_w, 8), g_conv[None], tr=8,
                      name="adamw_conv_w")
    for out, pk in zip((G, Dl, NM, NV), conv_out):
        out["conv_w"] = pk[:CONV_WIDTH]
    return (loss, grad_x[None], *[G[n] for n in WEIGHTS], *[Dl[n] for n in WEIGHTS], *[NM[n] for n in WEIGHTS],
            *[NV[n] for n in WEIGHTS])
```

```python
import functools

import numpy as np
import jax
import jax.numpy as jnp
from jax import lax
from jax.experimental import pallas as pl
from jax.experimental.pallas import tpu as pltpu

F32 = jnp.float32
BF16 = jnp.bfloat16
MESH = pl.DeviceIdType.MESH

D_MODEL = 1024
N_DEV = 8
LANES = 128
RNN_BLOCKS = 8
RNN_BLOCK_W = 128
CONV_WIDTH = 4
LRU_C = 8.0
MLA_HEADS = 8
QK_NOPE = 128
QK_ROPE = 64
V_HEAD = 128
QK_PAD = 256
Q_LORA = 256
KV_LORA = 256
CKV_W = 640
ROPE_THETA = 10000.0
D_FF = 4096
EPS = 1e-6
ATTN_SCALE = (QK_NOPE + QK_ROPE) ** -0.5
LOG2E = 1.4426950408889634
LN2 = 0.6931471805599453
NEG = -1e30

ADAM_LR = 0.001
ADAM_B1 = 0.9
ADAM_B2 = 0.999
ADAM_EPS = 1e-08
ADAM_WD = 0.01
ADAM_STEP = 10

VMEM_LIMIT = 56 * 1024 * 1024


def _params(sem=None):
    return pltpu.CompilerParams(dimension_semantics=sem, vmem_limit_bytes=VMEM_LIMIT)


def _sigmoid(v):
    return 1.0 / (1.0 + jnp.exp(-v))


def _softplus(y):
    e = jnp.exp(-jnp.abs(y))
    u = 1.0 + e
    d = u - 1.0
    l1p = jnp.where(d == 0.0, e, jnp.log(u) * e / jnp.where(d == 0.0, 1.0, d))
    return jnp.maximum(y, 0.0) + l1p


_GELU_K = 0.7978845608028654
_GELU_C = 0.044715


def _gelu_and_grad(v):
    t = jnp.tanh(_GELU_K * (v + _GELU_C * v * v * v))
    g = 0.5 * v * (1.0 + t)
    dg = 0.5 * (1.0 + t) + 0.5 * v * (1.0 - t * t) * _GELU_K * (1.0 + 3.0 * _GELU_C * v * v)
    return g, dg


def _rms_fwd(v, g):
    rstd = lax.rsqrt(jnp.mean(v * v, axis=-1, keepdims=True) + EPS)
    return v * rstd * g, rstd


def _rms_bwd(dy, v, g):
    rstd = lax.rsqrt(jnp.mean(v * v, axis=-1, keepdims=True) + EPS)
    vh = v * rstd
    dvh = dy * g
    dv = rstd * (dvh - vh * jnp.mean(dvh * vh, axis=-1, keepdims=True))
    return dv, dy * vh


def _shift_down(v, s, fill, row):
    return jnp.where(row >= s, pltpu.roll(v, s, 0), fill)


def _shift_up(v, s, fill, row, n):
    return jnp.where(row < n - s, pltpu.roll(v, n - s, 0), fill)


def _rot_half(v, lane):
    n = v.shape[-1]
    l = lane & (LANES - 1)
    up = pltpu.roll(v, n - QK_ROPE // 2, 1)
    dn = pltpu.roll(v, QK_ROPE // 2, 1)
    return jnp.where(l < QK_ROPE // 2, -up, jnp.where(l < QK_ROPE, dn, 0.0))


def _mm(a, b, *, name, tm, tn, tk, outs, epilogue, extras=(), ta=False, tb=False, more=None):
    assert not (ta and tb)
    if ta:
        K, M = a.shape
    else:
        M, K = a.shape
    if tb:
        N, K2 = b.shape
    else:
        K2, N = b.shape
    assert K == K2 and M % tm == 0 and N % tn == 0 and K % tk == 0, (name, a.shape, b.shape)
    n_i, n_j, n_k = M // tm, N // tn, K // tk
    n_ex, n_out = len(extras), len(outs)
    n_more = 0 if more is None else 2
    assert more is None or (n_k == 1 and not ta)

    def body(*refs):
        a_ref, b_ref = refs[0], refs[1]
        ex_refs = refs[2 + n_more:2 + n_more + n_ex]
        out_refs = refs[2 + n_more + n_ex:2 + n_more + n_ex + n_out]
        if ta:
            part = lax.dot_general(a_ref[...], b_ref[...], (((0,), (0,)), ((), ())), preferred_element_type=F32)
        elif tb:
            part = lax.dot_general(a_ref[...], b_ref[...], (((1,), (1,)), ((), ())), preferred_element_type=F32)
        else:
            part = jnp.dot(a_ref[...], b_ref[...], preferred_element_type=F32)
        if more is not None:
            part = part + lax.dot_general(refs[2][...], refs[3][...], (((1,), (1,)), ((), ())),
                                          preferred_element_type=F32)

        def finish(acc):
            res = epilogue(acc, *[r[...] for r in ex_refs])
            for o_ref, r, spec in zip(out_refs, res, outs):
                if spec[0] == "cols":
                    o_ref[:, spec[3]:spec[3] + r.shape[1]] = r.astype(o_ref.dtype)
                else:
                    o_ref[...] = r.astype(o_ref.dtype).reshape(o_ref.shape)

        if n_k == 1:
            finish(part)
        else:
            acc_ref = refs[-1]
            k = pl.program_id(2)

            @pl.when(k == 0)
            def _():
                acc_ref[...] = part

            @pl.when(k > 0)
            def _():
                acc_ref[...] += part

            @pl.when(k == n_k - 1)
            def _():
                finish(acc_ref[...])

    a_spec = pl.BlockSpec((tk, tm), lambda j, i, k: (k, i)) if ta else pl.BlockSpec((tm, tk), lambda j, i, k: (i, k))
    b_once = dict(pipeline_mode=pl.Buffered(1)) if (n_j == 1 and n_k == 1) else {}
    if tb:
        in_specs = [a_spec, pl.BlockSpec((tn, tk), lambda j, i, k: (j, k), **b_once)]
    else:
        in_specs = [a_spec, pl.BlockSpec((tk, tn), lambda j, i, k: (k, j), **b_once)]
    if more is not None:
        k2 = more[0].shape[1]
        in_specs += [pl.BlockSpec((tm, k2), lambda j, i, k: (i, 0)), pl.BlockSpec((tn, k2), lambda j, i, k: (j, 0), **b_once)]
    for ex in extras:
        kind = ex[0]
        if kind == "tile":
            in_specs.append(pl.BlockSpec((tm, tn), lambda j, i, k: (i, j)))
        elif kind == "tilecol":
            assert n_j == 1
            in_specs.append(pl.BlockSpec((tm, tn), functools.partial(lambda c, j, i, k: (i, c), ex[2])))
        else:
            in_specs.append(pl.BlockSpec((1, tn), lambda j, i, k: (0, j)))
    out_specs, out_shape = [], []
    for kind, dt, *rest in outs:
        if kind == "tile":
            out_specs.append(pl.BlockSpec((tm, tn), lambda j, i, k: (i, j)))
            out_shape.append(jax.ShapeDtypeStruct((M, N), dt))
        elif kind == "colshard":
            out_specs.append(pl.BlockSpec((1, tm, tn), lambda j, i, k: (j, i, 0)))
            out_shape.append(jax.ShapeDtypeStruct((n_j, M, tn), dt))
        elif kind == "cols":
            assert n_j == 1
            out_specs.append(pl.BlockSpec((tm, rest[0]), lambda j, i, k: (i, 0)))
            out_shape.append(jax.ShapeDtypeStruct((M, rest[0]), dt))
        elif kind == "side":
            assert n_j == 1
            out_specs.append(pl.BlockSpec((tm, LANES), lambda j, i, k: (i, 0)))
            out_shape.append(jax.ShapeDtypeStruct((M, LANES), dt))
        else:
            out_specs.append(pl.BlockSpec((1, 1, tn), lambda j, i, k: (i, 0, j)))
            out_shape.append(jax.ShapeDtypeStruct((n_i, 1, N), dt))
    scratch = [pltpu.VMEM((tm, tn), F32)] if n_k > 1 else []
    return pl.pallas_call(
        body, name=name, grid=(n_j, n_i, n_k), in_specs=in_specs, out_specs=out_specs, out_shape=out_shape,
        scratch_shapes=scratch, compiler_params=_params(("parallel", "parallel", "arbitrary")),
    )(a, b, *(more or ()), *[ex[1] for ex in extras])


def _rmsnorm_cast(x, g, *, ts, name):
    S, D = x.shape

    def body(x_ref, g_ref, o_ref):
        y, _ = _rms_fwd(x_ref[...], g_ref[...])
        o_ref[...] = y.astype(BF16)

    return pl.pallas_call(
        body, name=name, grid=(S // ts,),
        in_specs=[pl.BlockSpec((ts, D), lambda i: (i, 0)), pl.BlockSpec((1, D), lambda i: (0, 0))],
        out_specs=pl.BlockSpec((ts, D), lambda i: (i, 0)), out_shape=jax.ShapeDtypeStruct((S, D), BF16),
        compiler_params=_params(("parallel",)),
    )(x, g)


LRU_NB = 4


def _lru_gates(xa, wa_ref, ba_ref, wx_ref, bx_ref, lam):
    xab = xa.astype(BF16)
    W = RNN_BLOCK_W
    rs, is_ = [], []
    for j in range(LRU_NB):
        xj = xab[:, j * W:(j + 1) * W]
        rs.append(_sigmoid(jnp.dot(xj, wa_ref[j], preferred_element_type=F32) + ba_ref[j]))
        is_.append(_sigmoid(jnp.dot(xj, wx_ref[j], preferred_element_type=F32) + bx_ref[j]))
    r = jnp.concatenate(rs, axis=1)
    i = jnp.concatenate(is_, axis=1)
    sp = _softplus(-lam)
    log_a = (-LRU_C * r) * sp
    a = jnp.exp(log_a)
    y = 2.0 * log_a
    one_m = jnp.where(y > -0.01, -y * (1.0 + 0.5 * y * (1.0 + y * (1.0 / 3.0))), 1.0 - a * a)
    return r, i, sp, a, jnp.sqrt(one_m)


def _rows_before(x, tail8, k):
    e16 = jnp.concatenate([tail8, x[0:8, :]], axis=0)
    return jnp.concatenate([pltpu.roll(e16, k, 0)[8:16, :], pltpu.roll(x, k, 0)[8:, :]], axis=0)


def _rows_after(x, head8, k):
    tt = x.shape[0]
    e16 = jnp.concatenate([x[tt - 8:tt, :], head8], axis=0)
    return jnp.concatenate([pltpu.roll(x, tt - k, 0)[:tt - 8, :], pltpu.roll(e16, 16 - k, 0)[0:8, :]], axis=0)


def _scan_down(a, b, h0, a_s, b_s, c_s):
    tt, C = a.shape
    G, nch = tt // 8, C // LANES
    rin = lax.broadcasted_iota(jnp.int32, (tt, C), 0) & 7

    def in_group(v, s):
        return pltpu.roll(v.reshape(G, 8, C), s, 1).reshape(tt, C)

    A, B = a, b
    for s in (1, 2, 4):
        B = A * jnp.where(rin >= s, in_group(B, s), 0.0) + B
        A = A * jnp.where(rin >= s, in_group(A, s), 1.0)
    for j in range(nch):
        a_s[j] = A[:, j * LANES:(j + 1) * LANES]
        b_s[j] = B[:, j * LANES:(j + 1) * LANES]
    At = jnp.concatenate([a_s.at[j][pl.ds(7, G, stride=8), :] for j in range(nch)], axis=1)
    Bt = jnp.concatenate([b_s.at[j][pl.ds(7, G, stride=8), :] for j in range(nch)], axis=1)
    rowg = lax.broadcasted_iota(jnp.int32, (G, C), 0)
    s = 1
    while s < G:
        Bt = At * _shift_down(Bt, s, 0.0, rowg) + Bt
        At = At * _shift_down(At, s, 1.0, rowg)
        s *= 2
    hg = At * h0 + Bt
    cin = _shift_down(hg, 1, h0, rowg)
    for j in range(nch):
        for r in range(8):
            c_s.at[j][pl.ds(r, G, stride=8), :] = cin[:, j * LANES:(j + 1) * LANES]
    return A * jnp.concatenate([c_s[j] for j in range(nch)], axis=1) + B, hg[G - 1:G, :]


def _scan_up(c, g_in, g_next, a_s, b_s, c_s):
    tt, C = c.shape
    G, nch = tt // 8, C // LANES
    rin = lax.broadcasted_iota(jnp.int32, (tt, C), 0) & 7

    def in_group(v, s):
        return pltpu.roll(v.reshape(G, 8, C), 8 - s, 1).reshape(tt, C)

    Cc, Gv = c, g_in
    for s in (1, 2, 4):
        Gv = Gv + Cc * jnp.where(rin < 8 - s, in_group(Gv, s), 0.0)
        Cc = Cc * jnp.where(rin < 8 - s, in_group(Cc, s), 1.0)
    for j in range(nch):
        a_s[j] = Cc[:, j * LANES:(j + 1) * LANES]
        b_s[j] = Gv[:, j * LANES:(j + 1) * LANES]
    Ct = jnp.concatenate([a_s.at[j][pl.ds(0, G, stride=8), :] for j in range(nch)], axis=1)
    Gt = jnp.concatenate([b_s.at[j][pl.ds(0, G, stride=8), :] for j in range(nch)], axis=1)
    rowg = lax.broadcasted_iota(jnp.int32, (G, C), 0)
    s = 1
    while s < G:
        Gt = Gt + Ct * _shift_up(Gt, s, 0.0, rowg, G)
        Ct = Ct * _shift_up(Ct, s, 1.0, rowg, G)
        s *= 2
    gg = Gt + Ct * g_next
    cin = _shift_up(gg, 1, g_next, rowg, G)
    for j in range(nch):
        for r in range(8):
            c_s.at[j][pl.ds(r, G, stride=8), :] = cin[:, j * LANES:(j + 1) * LANES]
    return Gv + Cc * jnp.concatenate([c_s[j] for j in range(nch)], axis=1), gg[0:1, :]


def _lru_fwd(z_rx, conv_w, conv_b, wa, ba, wx, bx, lam, *, tt):
    S = z_rx.shape[0]
    n_t = S // tt
    BW = RNN_BLOCK_W
    W = LRU_NB * BW

    def body(x_ref, cw_ref, cb_ref, wa_ref, ba_ref, wx_ref, bx_ref, lam_ref, h_ref, tail, hc, a_s, b_s, c_s):
        t = pl.program_id(1)

        @pl.when(t == 0)
        def _():
            tail[...] = jnp.zeros((8, W), F32)
            hc[...] = jnp.zeros((8, W), F32)

        x = x_ref[...]
        before = tail[...]
        cw = cw_ref[...]
        xa = (cb_ref[...] + cw[3:4] * x + cw[2:3] * _rows_before(x, before, 1) + cw[1:2] * _rows_before(x, before, 2)
              + cw[0:1] * _rows_before(x, before, 3))
        tail[...] = x[tt - 8:tt, :]
        _r, i, _sp, a, mult = _lru_gates(xa, wa_ref, ba_ref, wx_ref, bx_ref, lam_ref[...])
        h, h_last = _scan_down(a, mult * (i * xa), hc[0:1, :], a_s, b_s, c_s)
        h_ref[...] = h
        hc[...] = jnp.broadcast_to(h_last, (8, W))

    blk = lambda n, t: (t, n)
    vec = pl.BlockSpec((1, W), lambda n, t: (0, n))
    mat = pl.BlockSpec((LRU_NB, BW, BW), lambda n, t: (n, 0, 0))
    bias = pl.BlockSpec((LRU_NB, 1, BW), lambda n, t: (n, 0, 0))
    row8 = pltpu.VMEM((8, W), F32)
    wide = pltpu.VMEM((LRU_NB, tt, LANES), F32)
    return pl.pallas_call(
        body, name="lru_fwd", grid=(RNN_BLOCKS // LRU_NB, n_t),
        in_specs=[pl.BlockSpec((tt, W), blk), pl.BlockSpec((CONV_WIDTH, W), lambda n, t: (0, n)), vec, mat, bias, mat,
                  bias, vec],
        out_specs=pl.BlockSpec((tt, W), blk), out_shape=jax.ShapeDtypeStruct((S, D_MODEL), F32),
        scratch_shapes=[row8, row8, wide, wide, wide],
        compiler_params=_params(("parallel", "arbitrary")),
    )(z_rx, conv_w, conv_b, wa, ba, wx, bx, lam)


def _lru_bwd(z_rx, h, dh, dz, conv_w, conv_b, wa, wat, ba, wx, wxt, bx, lam, *, tt):
    S = z_rx.shape[0]
    n_t = S // tt
    BW = RNN_BLOCK_W
    W = LRU_NB * BW
    t8 = tt // 8

    def body(x_ref, xp_ref, h_ref, hp_ref, dh_ref, _dz_ref, cw_ref, cb_ref, wa_ref, wat_ref, ba_ref, wx_ref, wxt_ref,
             bx_ref, lam_ref, dx_ref, dwa_ref, dwx_ref, dba_ref, dbx_ref, dlam_ref, dcw_ref, dcb_ref, nxt, a_c, g_c, a_s,
             b_s, c_s):
        t = pl.program_id(1)
        tile = n_t - 1 - t

        @pl.when(t == 0)
        def _():
            a_c[...] = jnp.zeros((8, W), F32)
            g_c[...] = jnp.zeros((8, W), F32)
            nxt[...] = jnp.zeros((8, W), F32)
            dwa_ref[...] = jnp.zeros_like(dwa_ref)
            dwx_ref[...] = jnp.zeros_like(dwx_ref)
            dba_ref[...] = jnp.zeros_like(dba_ref)
            dbx_ref[...] = jnp.zeros_like(dbx_ref)
            dlam_ref[...] = jnp.zeros_like(dlam_ref)
            dcw_ref[...] = jnp.zeros_like(dcw_ref)
            dcb_ref[...] = jnp.zeros_like(dcb_ref)

        has_prev = (tile > 0).astype(F32)
        x = x_ref[...]
        before = xp_ref[...] * has_prev
        xm1, xm2, xm3 = _rows_before(x, before, 1), _rows_before(x, before, 2), _rows_before(x, before, 3)
        cw = cw_ref[...]
        xa = cb_ref[...] + cw[3:4] * x + cw[2:3] * xm1 + cw[1:2] * xm2 + cw[0:1] * xm3
        lam = lam_ref[...]
        r, i, sp, a, mult = _lru_gates(xa, wa_ref, ba_ref, wx_ref, bx_ref, lam)
        gated = i * xa
        h_prev = _rows_before(h_ref[...], hp_ref[...] * has_prev, 1)
        g, g_first = _scan_up(_rows_after(a, a_c[...], 1), dh_ref[...], g_c[0:1, :], a_s, b_s, c_s)
        a_c[...] = jnp.broadcast_to(a[0:1, :], (8, W))
        g_c[...] = jnp.broadcast_to(g_first, (8, W))
        dlog_a = g * h_prev * a - g * gated * (a * a) / mult
        dgated = g * mult
        di = dgated * xa
        dxa = dgated * i
        dr = dlog_a * (-LRU_C * sp)
        dlam_ref[...] += jnp.sum(dlog_a * (-LRU_C * r), axis=0, keepdims=True) * (-_sigmoid(-lam))
        dpr = dr * r * (1.0 - r)
        dpi = di * i * (1.0 - i)
        xab, dprb, dpib = xa.astype(BF16), dpr.astype(BF16), dpi.astype(BF16)
        tn_dims = (((0,), (0,)), ((), ()))
        back = []
        for j in range(LRU_NB):
            sl = slice(j * BW, (j + 1) * BW)
            dwa_ref[j] += lax.dot_general(xab[:, sl], dprb[:, sl], tn_dims, preferred_element_type=F32)
            dwx_ref[j] += lax.dot_general(xab[:, sl], dpib[:, sl], tn_dims, preferred_element_type=F32)
            dba_ref[j] += jnp.sum(dpr[:, sl], axis=0, keepdims=True)
            dbx_ref[j] += jnp.sum(dpi[:, sl], axis=0, keepdims=True)
            back.append(jnp.dot(dprb[:, sl], wat_ref[j], preferred_element_type=F32)
                        + jnp.dot(dpib[:, sl], wxt_ref[j], preferred_element_type=F32))
        dxa = dxa + jnp.concatenate(back, axis=1)
        after = nxt[...]
        dx = (cw[3:4] * dxa + cw[2:3] * _rows_after(dxa, after, 1) + cw[1:2] * _rows_after(dxa, after, 2)
              + cw[0:1] * _rows_after(dxa, after, 3))
        nxt[...] = dxa[0:8, :]
        dx_ref[...] = dx.astype(BF16)
        dcw_ref[3:4, :] += jnp.sum(dxa * x, axis=0, keepdims=True)
        dcw_ref[2:3, :] += jnp.sum(dxa * xm1, axis=0, keepdims=True)
        dcw_ref[1:2, :] += jnp.sum(dxa * xm2, axis=0, keepdims=True)
        dcw_ref[0:1, :] += jnp.sum(dxa * xm3, axis=0, keepdims=True)
        dcb_ref[...] += jnp.sum(dxa, axis=0, keepdims=True)

    blk = lambda n, t: (n_t - 1 - t, n)
    prev = lambda n, t: (jnp.maximum((n_t - 1 - t) * t8 - 1, 0), n)
    vec = pl.BlockSpec((1, W), lambda n, t: (0, n))
    mat = pl.BlockSpec((LRU_NB, BW, BW), lambda n, t: (n, 0, 0))
    bias = pl.BlockSpec((LRU_NB, 1, BW), lambda n, t: (n, 0, 0))
    cws = pl.BlockSpec((CONV_WIDTH, W), lambda n, t: (0, n))
    tile = pl.BlockSpec((tt, W), blk)
    prev8 = pl.BlockSpec((8, W), prev)
    row8 = pltpu.VMEM((8, W), F32)
    wide = pltpu.VMEM((LRU_NB, tt, LANES), F32)
    return pl.pallas_call(
        body, name="lru_bwd", grid=(RNN_BLOCKS // LRU_NB, n_t),
        in_specs=[tile, prev8, tile, prev8, tile, pl.BlockSpec(memory_space=pl.ANY), cws, vec, mat, mat, bias, mat, mat,
                  bias, vec],
        out_specs=[tile, mat, mat, bias, bias, vec, cws, vec], input_output_aliases={5: 0},
        out_shape=[jax.ShapeDtypeStruct(dz.shape, BF16),
                   jax.ShapeDtypeStruct((RNN_BLOCKS, BW, BW), F32), jax.ShapeDtypeStruct((RNN_BLOCKS, BW, BW), F32),
                   jax.ShapeDtypeStruct((RNN_BLOCKS, 1, BW), F32), jax.ShapeDtypeStruct((RNN_BLOCKS, 1, BW), F32),
                   jax.ShapeDtypeStruct((1, D_MODEL), F32),
                   jax.ShapeDtypeStruct((CONV_WIDTH, D_MODEL), F32), jax.ShapeDtypeStruct((1, D_MODEL), F32)],
        scratch_shapes=[row8, row8, row8, wide, wide, wide],
        compiler_params=_params(("parallel", "arbitrary")),
    )(z_rx, z_rx, h, h, dh, dz, conv_w, conv_b, wa, wat, ba, wx, wxt, bx, lam)


def _mla_proj(z_ckv, q_norm, kv_norm, w_uq, w_ukv, cos, sin, *, ts):
    S = z_ckv.shape[0]
    H = MLA_HEADS

    def body(c_ref, qn_ref, kn_ref, wq_ref, wkv_ref, cos_ref, sin_ref, q_ref, k_ref, v_ref):
        c = c_ref[...]
        cqn, _ = _rms_fwd(c[:, 0:Q_LORA], qn_ref[...])
        ckn, _ = _rms_fwd(c[:, Q_LORA:Q_LORA + KV_LORA], kn_ref[...])
        q = jnp.dot(cqn.astype(BF16), wq_ref[...], preferred_element_type=F32) * (ATTN_SCALE * LOG2E)
        kv = jnp.dot(ckn.astype(BF16), wkv_ref[...], preferred_element_type=F32)
        cos1, sin1 = cos_ref[...], sin_ref[...]
        cos8 = jnp.concatenate([cos1] * H, axis=1)
        sin8 = jnp.concatenate([sin1] * H, axis=1)
        qr = q[:, H * QK_NOPE:]
        lane8 = lax.broadcasted_iota(jnp.int32, qr.shape, 1)
        qr = qr * cos8 + _rot_half(qr, lane8) * sin8
        kr = c[:, Q_LORA + KV_LORA:]
        lane1 = lax.broadcasted_iota(jnp.int32, kr.shape, 1)
        kr = (kr * cos1 + _rot_half(kr, lane1) * sin1).astype(BF16)
        for h in range(H):
            q_ref[h, :, 0:QK_NOPE] = q[:, h * QK_NOPE:(h + 1) * QK_NOPE].astype(BF16)
            q_ref[h, :, QK_NOPE:] = qr[:, h * LANES:(h + 1) * LANES].astype(BF16)
            k_ref[h, :, 0:QK_NOPE] = kv[:, h * 2 * LANES:h * 2 * LANES + LANES].astype(BF16)
            k_ref[h, :, QK_NOPE:] = kr
            v_ref[h] = kv[:, h * 2 * LANES + LANES:(h + 1) * 2 * LANES].astype(BF16)

    full = lambda shape: pl.BlockSpec(shape, lambda i: (0,) * len(shape))
    return pl.pallas_call(
        body, name="mla_proj", grid=(S // ts,),
        in_specs=[pl.BlockSpec((ts, CKV_W), lambda i: (i, 0)), full((1, Q_LORA)), full((1, KV_LORA)),
                  full(w_uq.shape), full(w_ukv.shape), pl.BlockSpec((ts, LANES), lambda i: (i, 0)),
                  pl.BlockSpec((ts, LANES), lambda i: (i, 0))],
        out_specs=[pl.BlockSpec((H, ts, QK_PAD), lambda i: (0, i, 0)), pl.BlockSpec((H, ts, QK_PAD), lambda i: (0, i, 0)),
                   pl.BlockSpec((H, ts, V_HEAD), lambda i: (0, i, 0))],
        out_shape=[jax.ShapeDtypeStruct((H, S, QK_PAD), BF16), jax.ShapeDtypeStruct((H, S, QK_PAD), BF16),
                   jax.ShapeDtypeStruct((H, S, V_HEAD), BF16)],
        compiler_params=_params(("parallel",)),
    )(z_ckv, q_norm, kv_norm, w_uq, w_ukv, cos, sin)


def _mla_proj_bwd(z_ckv, dq, dk, dv, q_norm, kv_norm, w_uqt, w_ukvt, cos, sin, *, ts):
    S = z_ckv.shape[0]
    H = MLA_HEADS

    def body(c_ref, dq_ref, dk_ref, dv_ref, qn_ref, kn_ref, wqt_ref, wkvt_ref, cos_ref, sin_ref,
             dz_ref, dwq_ref, dwkv_ref, dqn_ref, dkn_ref):
        @pl.when(pl.program_id(0) == 0)
        def _():
            dwq_ref[...] = jnp.zeros_like(dwq_ref)
            dwkv_ref[...] = jnp.zeros_like(dwkv_ref)
            dqn_ref[...] = jnp.zeros_like(dqn_ref)
            dkn_ref[...] = jnp.zeros_like(dkn_ref)

        c = c_ref[...]
        cq, ck = c[:, 0:Q_LORA], c[:, Q_LORA:Q_LORA + KV_LORA]
        qn, kn = qn_ref[...], kn_ref[...]
        cqn, _ = _rms_fwd(cq, qn)
        ckn, _ = _rms_fwd(ck, kn)
        cos1, sin1 = cos_ref[...], sin_ref[...]
        lane1 = lax.broadcasted_iota(jnp.int32, cos1.shape, 1)

        def unrope(g):
            return g * cos1 - _rot_half(g * sin1, lane1)

        dq_all = jnp.concatenate([dq_ref[h, :, 0:QK_NOPE] for h in range(H)]
                                 + [unrope(dq_ref[h, :, QK_NOPE:]) for h in range(H)], axis=1)
        dq_all = (dq_all * ATTN_SCALE).astype(BF16)
        dkv_all = jnp.concatenate([p for h in range(H) for p in (dk_ref[h, :, 0:QK_NOPE], dv_ref[h])],
                                  axis=1).astype(BF16)
        dkr = dk_ref[0, :, QK_NOPE:].astype(F32)
        for h in range(1, H):
            dkr = dkr + dk_ref[h, :, QK_NOPE:].astype(F32)
        dkr = unrope(dkr)
        tn_dims = (((0,), (0,)), ((), ()))
        dwq_ref[...] += lax.dot_general(cqn.astype(BF16), dq_all, tn_dims, preferred_element_type=F32)
        dwkv_ref[...] += lax.dot_general(ckn.astype(BF16), dkv_all, tn_dims, preferred_element_type=F32)
        dcqn = jnp.dot(dq_all, wqt_ref[...], preferred_element_type=F32)
        dckn = jnp.dot(dkv_all, wkvt_ref[...], preferred_element_type=F32)
        dcq, dqn_rows = _rms_bwd(dcqn, cq, qn)
        dck, dkn_rows = _rms_bwd(dckn, ck, kn)
        dqn_ref[...] += jnp.sum(dqn_rows, axis=0, keepdims=True)
        dkn_ref[...] += jnp.sum(dkn_rows, axis=0, keepdims=True)
        dz_ref[:, 0:Q_LORA] = dcq.astype(BF16)
        dz_ref[:, Q_LORA:Q_LORA + KV_LORA] = dck.astype(BF16)
        dz_ref[:, Q_LORA + KV_LORA:] = dkr.astype(BF16)

    full = lambda shape: pl.BlockSpec(shape, lambda i: (0,) * len(shape))
    return pl.pallas_call(
        body, name="mla_proj_bwd", grid=(S // ts,),
        in_specs=[pl.BlockSpec((ts, CKV_W), lambda i: (i, 0)), pl.BlockSpec((H, ts, QK_PAD), lambda i: (0, i, 0)),
                  pl.BlockSpec((H, ts, QK_PAD), lambda i: (0, i, 0)), pl.BlockSpec((H, ts, V_HEAD), lambda i: (0, i, 0)),
                  full((1, Q_LORA)), full((1, KV_LORA)), full(w_uqt.shape), full(w_ukvt.shape),
                  pl.BlockSpec((ts, LANES), lambda i: (i, 0)), pl.BlockSpec((ts, LANES), lambda i: (i, 0))],
        out_specs=[pl.BlockSpec((ts, CKV_W), lambda i: (i, 0)), full((Q_LORA, w_uqt.shape[0])),
                   full((KV_LORA, w_ukvt.shape[0])), full((1, Q_LORA)), full((1, KV_LORA))],
        out_shape=[jax.ShapeDtypeStruct((S, CKV_W), BF16), jax.ShapeDtypeStruct((Q_LORA, w_uqt.shape[0]), F32),
                   jax.ShapeDtypeStruct((KV_LORA, w_ukvt.shape[0]), F32), jax.ShapeDtypeStruct((1, Q_LORA), F32),
                   jax.ShapeDtypeStruct((1, KV_LORA), F32)],
        compiler_params=_params(("arbitrary",)),
    )(z_ckv, dq, dk, dv, q_norm, kv_norm, w_uqt, w_ukvt, cos, sin)


NT_DIMS = (((1,), (1,)), ((), ()))
TN_DIMS = (((0,), (0,)), ((), ()))


def _attn_fwd(q, k, v, *, t, hb):
    H, S, _ = q.shape
    n = S // t
    pairs = [(i, j) for i in range(n) for j in range(i + 1)]
    qi = jnp.asarray(np.array([p[0] for p in pairs], np.int32))
    ki = jnp.asarray(np.array([p[1] for p in pairs], np.int32))

    def body(qi_ref, ki_ref, q_ref, k_ref, v_ref, o_ref, lse_ref, m_s, l_s, acc_s):
        p = pl.program_id(1)
        i, j = qi_ref[p], ki_ref[p]

        @pl.when(j == 0)
        def _():
            m_s[...] = jnp.full(m_s.shape, NEG, F32)
            l_s[...] = jnp.zeros(l_s.shape, F32)
            acc_s[...] = jnp.zeros(acc_s.shape, F32)

        def block(hh, r0, nr, nk, masked):
            rows = slice(r0, r0 + nr)
            s = lax.dot_general(q_ref[hh, rows, :], k_ref[hh, 0:nk, :], NT_DIMS, preferred_element_type=F32)
            if masked:
                row = lax.broadcasted_iota(jnp.int32, (nr, nk), 0) + r0
                col = lax.broadcasted_iota(jnp.int32, (nr, nk), 1)
                s = jnp.where(row >= col, s, NEG)
            chunks = nk // LANES
            mc = s[:, 0:LANES]
            for c in range(1, chunks):
                mc = jnp.maximum(mc, s[:, c * LANES:(c + 1) * LANES])
            m_prev = m_s[hh, rows, :]
            m_new = jnp.maximum(m_prev, jnp.max(mc, axis=1, keepdims=True))
            alpha = jnp.exp2(m_prev - m_new)
            pr = jnp.exp2(s - jnp.concatenate([m_new] * chunks, axis=1))
            ls = pr[:, 0:LANES]
            for c in range(1, chunks):
                ls = ls + pr[:, c * LANES:(c + 1) * LANES]
            l_s[hh, rows, :] = alpha * l_s[hh, rows, :] + ls
            acc_s[hh, rows, :] = alpha * acc_s[hh, rows, :] + jnp.dot(pr.astype(BF16), v_ref[hh, 0:nk, :],
                                                                      preferred_element_type=F32)
            m_s[hh, rows, :] = m_new

        def step(diagonal):
            for hh in range(hb):
                if diagonal:
                    block(hh, 0, t // 2, t // 2, True)
                    block(hh, t // 2, t // 2, t, True)
                else:
                    block(hh, 0, t, t, False)

        @pl.when(j < i)
        def _():
            step(False)

        @pl.when(j == i)
        def _():
            step(True)
            for hh in range(hb):
                l = jnp.sum(l_s[hh], axis=1, keepdims=True)
                o_ref[:, hh * V_HEAD:(hh + 1) * V_HEAD] = acc_s[hh] / l
                lse_ref[hh] = (m_s[hh] + jnp.log2(l)).T[0:1, :]

    grid_spec = pltpu.PrefetchScalarGridSpec(
        num_scalar_prefetch=2, grid=(H // hb, len(pairs)),
        in_specs=[pl.BlockSpec((hb, t, QK_PAD), lambda h, p, qi, ki: (h, qi[p], 0)),
                  pl.BlockSpec((hb, t, QK_PAD), lambda h, p, qi, ki: (h, ki[p], 0)),
                  pl.BlockSpec((hb, t, V_HEAD), lambda h, p, qi, ki: (h, ki[p], 0))],
        out_specs=[pl.BlockSpec((t, hb * V_HEAD), lambda h, p, qi, ki: (qi[p], h)),
                   pl.BlockSpec((hb, 1, t), lambda h, p, qi, ki: (h, 0, qi[p]))],
        scratch_shapes=[pltpu.VMEM((hb, t, LANES), F32), pltpu.VMEM((hb, t, LANES), F32),
                        pltpu.VMEM((hb, t, V_HEAD), F32)],
    )
    return pl.pallas_call(
        body, name="attn_fwd", grid_spec=grid_spec,
        out_shape=[jax.ShapeDtypeStruct((S, H * V_HEAD), F32), jax.ShapeDtypeStruct((H, 1, S), F32)],
        compiler_params=_params(("parallel", "arbitrary")),
    )(qi, ki, q, k, v)


def _attn_bwd(q, k, v, do, lse_row, delta_row, *, t):
    H, S, _ = q.shape
    n = S // t
    pairs = [(i, j) for j in range(n) for i in range(j, n)]
    qi = jnp.asarray(np.array([p[0] for p in pairs], np.int32))
    ki = jnp.asarray(np.array([p[1] for p in pairs], np.int32))

    def body(qi_ref, ki_ref, q_ref, k_ref, v_ref, do_ref, lse_ref, dl_ref, dq_ref, dk_ref, dv_ref, dk_s, dv_s, dq_s):
        p = pl.program_id(1)
        i, j = qi_ref[p], ki_ref[p]

        @pl.when(p == 0)
        def _():
            dq_s[...] = jnp.zeros_like(dq_s)

        def block(k0, nk, q0, nq, masked):
            qb, dob = q_ref[0, q0:q0 + nq, :], do_ref[q0:q0 + nq, :]
            kb, vb = k_ref[0, k0:k0 + nk, :], v_ref[0, k0:k0 + nk, :]
            st = lax.dot_general(kb, qb, NT_DIMS, preferred_element_type=F32)
            if masked:
                krow = lax.broadcasted_iota(jnp.int32, (nk, nq), 0) + k0
                qcol = lax.broadcasted_iota(jnp.int32, (nk, nq), 1) + q0
                st = jnp.where(krow <= qcol, st, NEG)
            pt = jnp.exp2(st - lse_ref[0][:, q0:q0 + nq])
            dvp = jnp.dot(pt.astype(BF16), dob, preferred_element_type=F32)
            dpt = lax.dot_general(vb, dob, NT_DIMS, preferred_element_type=F32)
            dst = (pt * (dpt - dl_ref[0][:, q0:q0 + nq])).astype(BF16)
            dkp = jnp.dot(dst, qb, preferred_element_type=F32)
            rows = pl.ds(pl.multiple_of(i * t + q0, LANES), nq)
            dq_s[rows, :] += lax.dot_general(dst, kb, TN_DIMS, preferred_element_type=F32)
            return dkp, dvp

        @pl.when(i == j)
        def _():
            half = t // 2
            dk_s[0:half, :], dv_s[0:half, :] = block(0, half, 0, t, True)
            dk_s[half:t, :], dv_s[half:t, :] = block(half, half, half, half, True)

        @pl.when(i != j)
        def _():
            dkp, dvp = block(0, t, 0, t, False)
            dk_s[...] += dkp
            dv_s[...] += dvp

        @pl.when(i == n - 1)
        def _():
            dk_ref[0] = (dk_s[...] * LN2).astype(BF16)
            dv_ref[0] = dv_s[...].astype(BF16)

        @pl.when(p == len(pairs) - 1)
        def _():
            dq_ref[0] = dq_s[...].astype(BF16)

    grid_spec = pltpu.PrefetchScalarGridSpec(
        num_scalar_prefetch=2, grid=(H, len(pairs)),
        in_specs=[pl.BlockSpec((1, t, QK_PAD), lambda h, p, qi, ki: (h, qi[p], 0)),
                  pl.BlockSpec((1, t, QK_PAD), lambda h, p, qi, ki: (h, ki[p], 0)),
                  pl.BlockSpec((1, t, V_HEAD), lambda h, p, qi, ki: (h, ki[p], 0)),
                  pl.BlockSpec((t, V_HEAD), lambda h, p, qi, ki: (qi[p], h)),
                  pl.BlockSpec((1, 1, t), lambda h, p, qi, ki: (h, 0, qi[p])),
                  pl.BlockSpec((1, 1, t), lambda h, p, qi, ki: (h, 0, qi[p]))],
        out_specs=[pl.BlockSpec((1, S, QK_PAD), lambda h, p, qi, ki: (h, 0, 0)),
                   pl.BlockSpec((1, t, QK_PAD), lambda h, p, qi, ki: (h, ki[p], 0)),
                   pl.BlockSpec((1, t, V_HEAD), lambda h, p, qi, ki: (h, ki[p], 0))],
        scratch_shapes=[pltpu.VMEM((t, QK_PAD), F32), pltpu.VMEM((t, V_HEAD), F32), pltpu.VMEM((S, QK_PAD), F32)],
    )
    return pl.pallas_call(
        body, name="attn_bwd", grid_spec=grid_spec,
        out_shape=[jax.ShapeDtypeStruct((H, S, QK_PAD), BF16), jax.ShapeDtypeStruct((H, S, QK_PAD), BF16),
                   jax.ShapeDtypeStruct((H, S, V_HEAD), BF16)],
        compiler_params=_params(("parallel", "arbitrary")),
    )(qi, ki, q, k, v, do, lse_row, delta_row)


def _merge_h1(h, z_gates, o, x, w_out, norm_mlp, *, ts):
    S = h.shape[0]
    D = D_MODEL

    def body(h_ref, rg_ref, ga_ref, gb_ref, o_ref, x_ref, w_ref, g_ref, m_ref, h1_ref, n2_ref):
        gl, _ = _gelu_and_grad(rg_ref[...].astype(F32))
        m = (_sigmoid(ga_ref[...].astype(F32)) * (h_ref[...] * gl)
             + _sigmoid(gb_ref[...].astype(F32)) * o_ref[...]).astype(BF16)
        m_ref[...] = m
        h1 = x_ref[...] + jnp.dot(m, w_ref[...], preferred_element_type=F32)
        h1_ref[...] = h1
        n2, _ = _rms_fwd(h1, g_ref[...])
        n2_ref[...] = n2.astype(BF16)

    col = lambda c: pl.BlockSpec((ts, D), lambda i: (i, c))
    fixed = lambda shape: pl.BlockSpec(shape, lambda i: (0, 0), pipeline_mode=pl.Buffered(1))
    return pl.pallas_call(
        body, name="merge_h1", grid=(S // ts,),
        in_specs=[col(0), col(0), col(1), col(2), col(0), col(0), fixed((D, D)), fixed((1, D))],
        out_specs=[col(0), col(0), col(0)],
        out_shape=[jax.ShapeDtypeStruct((S, D), BF16), jax.ShapeDtypeStruct((S, D), F32),
                   jax.ShapeDtypeStruct((S, D), BF16)],
        compiler_params=_params(("parallel",)),
    )(h, z_gates, z_gates, z_gates, o, x, w_out, norm_mlp)


def _my_place():
    return lax.axis_index("x"), lax.axis_index("y"), lax.axis_index("c")


def _all_gather(shards, *, name):
    n = len(shards)

    def body(*refs):
        x_refs, out_refs = refs[:n], refs[n:2 * n]
        send_sems, recv_sems, local_sems = refs[2 * n:]
        x, y, c = _my_place()
        me, sibling = (x, y, c), (x, y, 1 - c)
        chips = [(1 - x, y), (x, 1 - y), (1 - x, 1 - y)]

        def slot(a, px, py, pc):
            return out_refs[a].at[4 * px + 2 * py + pc]

        def copy(a, k, block, to, src=None):
            return pltpu.make_async_remote_copy(
                src_ref=slot(a, *block) if src is None else src, dst_ref=slot(a, *block),
                send_sem=send_sems.at[7 * a + k], recv_sem=recv_sems.at[7 * a + k], device_id=to, device_id_type=MESH)

        mine = [pltpu.make_async_copy(x_refs[a], slot(a, *me), local_sems.at[a]) for a in range(n)]
        for cp in mine:
            cp.start()
        first = []
        for a in range(n):
            first.append(copy(a, 0, me, sibling, src=x_refs[a]))
            first += [copy(a, 1 + j, me, (*chip, c), src=x_refs[a]) for j, chip in enumerate(chips)]
        for cp in first:
            cp.start()
        passed = []
        for a in range(n):
            for j, chip in enumerate(chips):
                copy(a, 1 + j, (*chip, c), me).wait_recv()
                fwd = copy(a, 4 + j, (*chip, c), sibling)
                fwd.start()
                passed.append(fwd)
        for a in range(n):
            copy(a, 0, sibling, me).wait_recv()
            for j, chip in enumerate(chips):
                copy(a, 4 + j, (*chip, 1 - c), me).wait_recv()
        for cp in first + passed:
            cp.wait_send()
        for cp in mine:
            cp.wait()

    hbm = pl.BlockSpec(memory_space=pl.ANY)
    return pl.pallas_call(
        body, name=name, out_shape=[jax.ShapeDtypeStruct((N_DEV, *s.shape), s.dtype) for s in shards],
        in_specs=[hbm] * n, out_specs=[hbm] * n,
        scratch_shapes=[pltpu.SemaphoreType.DMA((7 * n,)), pltpu.SemaphoreType.DMA((7 * n,)),
                        pltpu.SemaphoreType.DMA((n,))],
    )(*shards)


def _pushes(src_refs, land_refs, send_sems, recv_sems, slab_per_peer):
    x, y, c = _my_place()
    me = 4 * x + 2 * y + c
    copies = []
    for a in range(len(src_refs)):
        for k in range(1, N_DEV):
            px, py, pc = x ^ (k >> 2), y ^ ((k >> 1) & 1), c ^ (k & 1)
            src = src_refs[a].at[4 * px + 2 * py + pc] if slab_per_peer else src_refs[a]
            copies.append(pltpu.make_async_remote_copy(
                src_ref=src, dst_ref=land_refs[a].at[me], send_sem=send_sems.at[7 * a + k - 1],
                recv_sem=recv_sems.at[7 * a + k - 1], device_id=(px, py, pc), device_id_type=MESH))
    return copies


def _push_start(srcs, *, name, slab_per_peer):
    n = len(srcs)
    lands = [lax.empty((N_DEV, *(s.shape[1:] if slab_per_peer else s.shape)), s.dtype) for s in srcs]

    def body(*refs):
        src_refs, land_refs = refs[:n], refs[n:2 * n]
        send_sems, recv_sems, token = refs[2 * n], refs[2 * n + 1], refs[-1]
        for cp in _pushes(src_refs, land_refs, send_sems, recv_sems, slab_per_peer):
            cp.start()
        token[...] = jnp.zeros_like(token)

    hbm = pl.BlockSpec(memory_space=pltpu.HBM)
    sem = pl.BlockSpec(memory_space=pltpu.SEMAPHORE)
    out = pl.pallas_call(
        body, name=name,
        out_shape=(pltpu.SemaphoreType.DMA((7 * n,)), pltpu.SemaphoreType.DMA((7 * n,)),
                   *[pltpu.HBM(a.shape, a.dtype) for a in srcs + lands], jax.ShapeDtypeStruct((8, LANES), F32)),
        in_specs=[hbm] * (2 * n), out_specs=(sem, sem, *[hbm] * (2 * n), pl.BlockSpec(memory_space=pltpu.VMEM)),
        input_output_aliases={i: 2 + i for i in range(2 * n)},
        compiler_params=pltpu.CompilerParams(has_side_effects=pltpu.SideEffectType.DATAFLOW_SIDE_EFFECTING),
    )(*[pltpu.with_memory_space_constraint(a, pltpu.HBM) for a in srcs + lands])
    return out[0], out[1], list(out[2:2 + n]), list(out[2 + n:2 + 2 * n]), out[-1]


def _push_wait(send_sems, recv_sems, srcs, lands, after, *, name, slab_per_peer):
    n = len(srcs)

    def body(*refs):
        src_refs, land_refs = refs[:n], refs[n:2 * n]
        s_sems, r_sems = refs[2 * n], refs[2 * n + 1]
        for cp in _pushes(src_refs, land_refs, s_sems, r_sems, slab_per_peer):
            cp.wait_send()
            cp.wait_recv()

    hbm = pl.BlockSpec(memory_space=pltpu.HBM)
    sem = pl.BlockSpec(memory_space=pltpu.SEMAPHORE)
    out = pl.pallas_call(
        body, name=name, out_shape=tuple(pltpu.HBM(a.shape, a.dtype) for a in srcs + lands),
        in_specs=[hbm] * (2 * n) + [sem, sem, pl.BlockSpec(memory_space=pl.ANY)], out_specs=tuple([hbm] * (2 * n)),
        input_output_aliases={i: i for i in range(2 * n)},
        compiler_params=pltpu.CompilerParams(has_side_effects=pltpu.SideEffectType.DATAFLOW_SIDE_EFFECTING),
    )(*srcs, *lands, send_sems, recv_sems, after)
    return list(out[:n]), list(out[n:])


def _sum_parts(gp_ref, rows):
    g = gp_ref[0, 0:rows, :].astype(F32)
    for p in range(1, gp_ref.shape[0]):
        g = g + gp_ref[p, 0:rows, :].astype(F32)
    return g


def _adamw_update(w, m, v, g):
    m_new = ADAM_B1 * m + (1.0 - ADAM_B1) * g
    v_new = ADAM_B2 * v + (1.0 - ADAM_B2) * (g * g)
    m_hat = m_new / (1.0 - ADAM_B1 ** ADAM_STEP)
    v_hat = v_new / (1.0 - ADAM_B2 ** ADAM_STEP)
    return -ADAM_LR * (m_hat / (jnp.sqrt(v_hat) + ADAM_EPS) + ADAM_WD * w), m_new, v_new


def _adamw_many(ws, ms, vs, gparts, sums, *, name):
    n, k = len(ws), len(sums)

    def body(*refs):
        w_refs, m_refs, v_refs = refs[:n], refs[n:2 * n], refs[2 * n:3 * n]
        g_refs, s_refs, outs = refs[3 * n:4 * n], refs[4 * n:4 * n + k], refs[4 * n + k:]
        for a in range(n):
            g = _sum_parts(g_refs[a], w_refs[a].shape[0])
            d, m_new, v_new = _adamw_update(w_refs[a][...], m_refs[a][...], v_refs[a][...], g)
            for o_ref, val in zip(outs[4 * a:4 * a + 4], (g, d, m_new, v_new)):
                o_ref[...] = val
        for b in range(k):
            outs[4 * n + b][...] = _sum_parts(s_refs[b], s_refs[b].shape[1])

    out_shape = [jax.ShapeDtypeStruct(w.shape, F32) for w in ws for _ in range(4)]
    out_shape += [jax.ShapeDtypeStruct(s.shape[1:], F32) for s in sums]
    return pl.pallas_call(body, name=name, out_shape=out_shape, compiler_params=_params())(
        *ws, *ms, *vs, *gparts, *sums)


def _adamw(w, m, v, gparts, *, tr, name):
    R, C = w.shape
    n_parts = gparts.shape[0]

    def body(w_ref, m_ref, v_ref, gp_ref, g_ref, d_ref, nm_ref, nv_ref):
        g = _sum_parts(gp_ref, tr)
        d_ref[...], nm_ref[...], nv_ref[...] = _adamw_update(w_ref[...], m_ref[...], v_ref[...], g)
        g_ref[...] = g

    row = pl.BlockSpec((tr, C), lambda i: (i, 0))
    shp = jax.ShapeDtypeStruct((R, C), F32)
    return pl.pallas_call(
        body, name=name, grid=(R // tr,),
        in_specs=[row, row, row, pl.BlockSpec((n_parts, tr, C), lambda i: (0, i, 0))],
        out_specs=[row, row, row, row], out_shape=[shp, shp, shp, shp],
        compiler_params=_params(("parallel",)),
    )(w, m, v, gparts)


def _rope_tables(s):
    pos = jnp.arange(s, dtype=F32)
    inv_freq = 1.0 / (ROPE_THETA ** (jnp.arange(0, QK_ROPE, 2, dtype=F32) / QK_ROPE))
    ang = pos[:, None] * inv_freq[None, :]
    cos, sin = jnp.cos(ang), jnp.sin(ang)
    zero = jnp.zeros((s, LANES - QK_ROPE), F32)
    return jnp.concatenate([cos, cos, zero], -1), jnp.concatenate([sin, sin, zero], -1)


def _pick(n, want):
    t = min(n, want)
    assert n % t == 0
    return t


def _local_step(x, target, wts, small, hooks):
    S = x.shape[0]
    H = MLA_HEADS
    ts = _pick(S, 512)
    tm = _pick(S, 512)
    tm_wide = _pick(S, 1024)
    tk_s = _pick(S, 4096)
    row = lambda v: v.reshape(1, -1)
    w_in = wts["w_in"]
    w_main = jnp.concatenate([w_in[:, 0:2048], w_in[:, 2624:4672]], axis=1)
    w_ckv = jnp.concatenate([w_in[:, 2048:2624], jnp.zeros((D_MODEL, CKV_W - 576), BF16)], axis=1)
    w_uq3 = wts["w_uq"].reshape(Q_LORA, H, QK_NOPE + QK_ROPE)
    w_uq_p = jnp.concatenate(
        [w_uq3[:, :, :QK_NOPE].reshape(Q_LORA, H * QK_NOPE),
         jnp.pad(w_uq3[:, :, QK_NOPE:], ((0, 0), (0, 0), (0, LANES - QK_ROPE))).reshape(Q_LORA, H * LANES)], axis=1)
    w_ukv = wts["w_ukv"]
    cos, sin = _rope_tables(S)
    conv_w, conv_b = small["conv_w"], row(small["conv_b"])
    wa, wx = small["lru_wa"].astype(BF16), small["lru_wx"].astype(BF16)
    wat, wxt = jnp.swapaxes(wa, 1, 2), jnp.swapaxes(wx, 1, 2)
    ba, bx = small["lru_ba"].reshape(RNN_BLOCKS, 1, RNN_BLOCK_W), small["lru_bx"].reshape(RNN_BLOCKS, 1, RNN_BLOCK_W)
    lam = row(small["lru_lambda"])
    q_norm, kv_norm = row(small["q_norm"]), row(small["kv_norm"])
    norm_mix, norm_mlp, norm_final = row(small["norm_mix"]), row(small["norm_mlp"]), row(small["norm_final"])

    xn = _rmsnorm_cast(x, norm_mix, ts=ts, name="norm_mix")
    ident = lambda acc: (acc,)
    (z_rx,) = _mm(xn, w_main[:, :D_MODEL], name="z_rx", tm=tm_wide, tn=1024, tk=1024, outs=[("tile", F32)],
                  epilogue=ident)
    (z_gates,) = _mm(xn, w_main[:, D_MODEL:], name="z_gates", tm=tm_wide, tn=3 * D_MODEL, tk=1024, outs=[("tile", BF16)],
                     epilogue=ident)
    (z_ckv,) = _mm(xn, w_ckv, name="z_ckv", tm=tm, tn=CKV_W, tk=1024, outs=[("tile", F32)], epilogue=ident)
    tt = _pick(S, 256)
    h = _lru_fwd(z_rx, conv_w, conv_b, wa, ba, wx, bx, lam, tt=tt)
    q, k, v = _mla_proj(z_ckv, q_norm, kv_norm, w_uq_p, w_ukv, cos, sin, ts=_pick(S, 256))
    ta = _pick(S, 1024)
    o, lse = _attn_fwd(q, k, v, t=ta, hb=2)
    w_out, w_up, w_down = hooks["weights_later"](o)
    merged, h1, n2 = _merge_h1(h, z_gates, o, x, w_out, norm_mlp, ts=_pick(S, 256))

    def ep_up(acc):
        r = jnp.maximum(acc, 0.0)
        return r * r, r

    act, relu = _mm(n2, w_up, name="up", tm=tm_wide, tn=2048, tk=1024, outs=[("tile", BF16), ("tile", BF16)],
                    epilogue=ep_up)

    def ep_loss(acc, h1v, tgt, g):
        h2 = acc + h1v
        y, _ = _rms_fwd(h2, g)
        err = y - tgt
        loss_rows = 0.5 * jnp.mean(err * err, axis=-1, keepdims=True)
        dy = err * (1.0 / D_MODEL)
        dh2, dg_rows = _rms_bwd(dy, h2, g)
        lsum = jnp.sum(loss_rows, axis=0, keepdims=True)
        return dh2, dh2, jnp.sum(dg_rows, axis=0, keepdims=True), jnp.broadcast_to(lsum, (1, D_MODEL))

    dh2, dh2b, dnf_p, loss_p = _mm(
        act, w_down, name="down_loss", tm=tm, tn=1024, tk=D_FF,
        outs=[("tile", F32), ("tile", BF16), ("rowpart", F32), ("rowpart", F32)], epilogue=ep_loss,
        extras=[("tile", h1), ("tile", target), ("row", norm_final)])
    loss_part = jnp.sum(loss_p[:, 0, 0])
    d_norm_final = jnp.sum(dnf_p, axis=(0, 1))

    def ep_du(acc, r):
        return (acc * (2.0 * r.astype(F32)),)

    (du,) = _mm(dh2b, w_down, name="d_act", tb=True, tm=tm_wide, tn=2048, tk=1024, outs=[("tile", BF16)], epilogue=ep_du,
                extras=[("tile", relu)])

    def ep_dh1(acc, h1v, dh2v, g):
        dv, dg_rows = _rms_bwd(acc, h1v, g)
        dh1 = dh2v + dv
        return dh1, dh1, jnp.sum(dg_rows, axis=0, keepdims=True)

    dh1, dh1b, dnm_p = _mm(du, w_up, name="d_n2", tb=True, tm=tm, tn=1024, tk=D_FF,
                           outs=[("tile", F32), ("tile", BF16), ("rowpart", F32)], epilogue=ep_dh1,
                           extras=[("tile", h1), ("tile", dh2), ("row", norm_mlp)])
    d_norm_mlp = jnp.sum(dnm_p, axis=(0, 1))
    tn_mm = functools.partial(_mm, ta=True, tk=tk_s, outs=[("tile", BF16)], epilogue=ident)
    (d_w_down,) = tn_mm(act, dh2b, name="dw_down", tm=1024, tn=1024)
    (p_w_up,) = _mm(n2, du, name="dw_up", ta=True, tk=tk_s, tm=1024, tn=D_FF // N_DEV, outs=[("colshard", BF16)],
                    epilogue=ident)
    (d_w_out,) = tn_mm(merged, dh1b, name="dw_out", tm=1024, tn=1024)
    early = [d_w_out.reshape(N_DEV, -1, D_MODEL), p_w_up, d_w_down.reshape(N_DEV, -1, D_MODEL)]
    w_out = w_out + hooks["send"]("early", early)[0, 0].astype(BF16)

    tmm = _pick(S, 256)

    def ep_dmerge(dm, hv, rg, ga, gb, ov):
        rg, ga, gb = rg.astype(F32), ga.astype(F32), gb.astype(F32)
        gl, dgl = _gelu_and_grad(rg)
        sa, sb = _sigmoid(ga), _sigmoid(gb)
        ya = hv * gl
        dya = dm * sa
        do = dm * sb
        dga = dm * ya * sa * (1.0 - sa)
        dgb = dm * ov * sb * (1.0 - sb)
        dh = dya * gl
        drg = dya * hv * dgl
        dov = do * ov
        lane = lax.broadcasted_iota(jnp.int32, (dm.shape[0], LANES), 1)
        delta = jnp.zeros((dm.shape[0], LANES), F32)
        for hh in range(H):
            dsum = jnp.sum(dov[:, hh * V_HEAD:(hh + 1) * V_HEAD], axis=1, keepdims=True)
            delta = jnp.where(lane == hh, dsum, delta)
        return dh, jnp.concatenate([drg, dga, dgb], axis=1), do, delta

    dh_lru, dz_part, do, delta_w = _mm(
        dh1b, w_out, name="d_merge", tb=True, tm=tmm, tn=1024, tk=1024,
        outs=[("tile", F32), ("cols", BF16, 4 * D_MODEL, D_MODEL), ("tile", BF16), ("side", F32)],
        epilogue=ep_dmerge,
        extras=[("tile", h), ("tilecol", z_gates, 0), ("tilecol", z_gates, 1), ("tilecol", z_gates, 2), ("tile", o)])
    delta_row = delta_w[:, :H].T.reshape(H, 1, S)
    lse_row = lse

    dq, dk, dv = _attn_bwd(q, k, v, do, lse_row, delta_row, t=ta)
    dz_ckv, d_w_uq_p, d_w_ukv, d_q_norm, d_kv_norm = _mla_proj_bwd(
        z_ckv, dq, dk, dv, q_norm, kv_norm, w_uq_p.T, w_ukv.T, cos, sin, ts=_pick(S, 256))
    d_w_uq = jnp.concatenate(
        [d_w_uq_p[:, :H * QK_NOPE].reshape(Q_LORA, H, QK_NOPE),
         d_w_uq_p[:, H * QK_NOPE:].reshape(Q_LORA, H, LANES)[:, :, :QK_ROPE]], axis=2).reshape(Q_LORA, -1)

    dz_main, d_wa, d_wx, d_ba, d_bx, d_lam, d_conv_w, d_conv_b = _lru_bwd(
        z_rx, h, dh_lru, dz_part, conv_w, conv_b, wa, wat, ba, wx, wxt, bx, lam, tt=tt)

    (d_w_main,) = tn_mm(xn, dz_main, name="dw_main", tm=1024, tn=1024)
    (d_w_ckv,) = tn_mm(xn, dz_ckv, name="dw_ckv", tm=1024, tn=CKV_W)
    d_w_in = jnp.concatenate([d_w_main[:, 0:2048], d_w_ckv[:, 0:576], d_w_main[:, 2048:4096]], axis=1)

    def col_parts(full):
        r = full.shape[0]
        return jnp.transpose(full.astype(BF16).reshape(r, N_DEV, -1), (1, 0, 2))

    late = [col_parts(d_w_in), col_parts(d_w_uq), col_parts(d_w_ukv)]
    norm_mix = norm_mix + hooks["send"]("late", late)[0, 0]

    def ep_dx(acc, xv, dh1v, g):
        dv, dg_rows = _rms_bwd(acc, xv, g)
        return dh1v + dv, jnp.sum(dg_rows, axis=0, keepdims=True)

    grad_x, dnx_p = _mm(dz_main, w_main, name="dx", tb=True, tm=tm, tn=1024, tk=4 * D_MODEL,
                        outs=[("tile", F32), ("rowpart", F32)], epilogue=ep_dx, more=(dz_ckv, w_ckv),
                        extras=[("tile", x), ("tile", dh1), ("row", norm_mix)])
    d_norm_mix = jnp.sum(dnx_p, axis=(0, 1))
    sm = {"norm_mix": d_norm_mix, "conv_w": d_conv_w, "conv_b": d_conv_b.reshape(-1), "lru_wa": d_wa,
          "lru_ba": d_ba.reshape(RNN_BLOCKS, RNN_BLOCK_W), "lru_wx": d_wx, "lru_bx": d_bx.reshape(RNN_BLOCKS, RNN_BLOCK_W),
          "lru_lambda": d_lam.reshape(-1), "q_norm": d_q_norm.reshape(-1), "kv_norm": d_kv_norm.reshape(-1),
          "norm_mlp": d_norm_mlp, "norm_final": d_norm_final}
    return loss_part, grad_x, sm


BIG = ("w_in", "w_uq", "w_ukv", "w_out", "w_up", "w_down")
SMALL = ("norm_mix", "conv_b", "lru_wa", "lru_ba", "lru_wx", "lru_bx", "lru_lambda", "q_norm", "kv_norm", "norm_mlp",
         "norm_final")
WEIGHTS = ("norm_mix", "w_in", "conv_w", "conv_b", "lru_wa", "lru_ba", "lru_wx", "lru_bx", "lru_lambda", "q_norm", "w_uq",
           "kv_norm", "w_ukv", "w_out", "norm_mlp", "w_up", "w_down", "norm_final")
ADAM_TILE_ROWS = {"w_in": 256, "w_uq": 128, "w_ukv": 128, "w_out": 64, "w_up": 256, "w_down": 128}
CONV_ROWS = N_DEV * 8


def _rows(a):
    return a.reshape(-1, LANES)


def _pad_rows(a, mult):
    r = a.shape[-2]
    pad = (-r) % mult
    if pad == 0:
        return a
    cfg = [(0, 0)] * (a.ndim - 2) + [(0, pad), (0, 0)]
    return jnp.pad(a, cfg)


def _cols_from_shards(g):
    return jnp.transpose(g, (1, 0, 2)).reshape(g.shape[1], -1)


def kernel(x, norm_mix, w_in, conv_w, conv_b, lru_wa, lru_ba, lru_wx, lru_bx, lru_lambda, q_norm, w_uq, kv_norm, w_ukv, w_out, norm_mlp, w_up, w_down, norm_final, loss_target, m_norm_mix, m_w_in, m_conv_w, m_conv_b, m_lru_wa, m_lru_ba, m_lru_wx, m_lru_bx, m_lru_lambda, m_q_norm, m_w_uq, m_kv_norm, m_w_ukv, m_w_out, m_norm_mlp, m_w_up, m_w_down, m_norm_final, v_norm_mix, v_w_in, v_conv_w, v_conv_b, v_lru_wa, v_lru_ba, v_lru_wx, v_lru_bx, v_lru_lambda, v_q_norm, v_w_uq, v_kv_norm, v_w_ukv, v_w_out, v_norm_mlp, v_w_up, v_w_down, v_norm_final):
    W = dict(norm_mix=norm_mix, w_in=w_in, conv_w=conv_w, conv_b=conv_b, lru_wa=lru_wa, lru_ba=lru_ba, lru_wx=lru_wx,
             lru_bx=lru_bx, lru_lambda=lru_lambda, q_norm=q_norm, w_uq=w_uq, kv_norm=kv_norm, w_ukv=w_ukv, w_out=w_out,
             norm_mlp=norm_mlp, w_up=w_up, w_down=w_down, norm_final=norm_final)
    M = dict(norm_mix=m_norm_mix, w_in=m_w_in, conv_w=m_conv_w, conv_b=m_conv_b, lru_wa=m_lru_wa, lru_ba=m_lru_ba,
             lru_wx=m_lru_wx, lru_bx=m_lru_bx, lru_lambda=m_lru_lambda, q_norm=m_q_norm, w_uq=m_w_uq, kv_norm=m_kv_norm,
             w_ukv=m_w_ukv, w_out=m_w_out, norm_mlp=m_norm_mlp, w_up=m_w_up, w_down=m_w_down, norm_final=m_norm_final)
    V = dict(norm_mix=v_norm_mix, w_in=v_w_in, conv_w=v_conv_w, conv_b=v_conv_b, lru_wa=v_lru_wa, lru_ba=v_lru_ba,
             lru_wx=v_lru_wx, lru_bx=v_lru_bx, lru_lambda=v_lru_lambda, q_norm=v_q_norm, w_uq=v_w_uq, kv_norm=v_kv_norm,
             w_ukv=v_w_ukv, w_out=v_w_out, norm_mlp=v_norm_mlp, w_up=v_w_up, w_down=v_w_down, norm_final=v_norm_final)
    me = 4 * lax.axis_index("x") + 2 * lax.axis_index("y") + lax.axis_index("c")

    first, later = ("w_in", "w_uq", "w_ukv"), ("w_out", "w_up", "w_down")
    got = _all_gather([W[n].astype(BF16) for n in first] + [_pad_rows(conv_w, 8)], name="gather_weights")
    wts = {"w_in": _cols_from_shards(got[0]), "w_uq": _cols_from_shards(got[1]), "w_ukv": _cols_from_shards(got[2])}
    w_send, w_recv, w_src, w_land, zeros = _push_start([W[n].astype(BF16) for n in later], name="gather_later_start",
                                                       slab_per_peer=False)
    small = {n: W[n] for n in SMALL}
    small["conv_w"] = _cols_from_shards(got[3][:, :CONV_WIDTH])
    small["norm_mix"] = norm_mix + zeros[0, 0]

    def with_own_slab(land, mine):
        return lax.dynamic_update_slice(land, mine, (me, 0, 0))

    def weights_later(after):
        srcs, lands = _push_wait(w_send, w_recv, w_src, w_land, after, name="gather_later_wait", slab_per_peer=False)
        w_out_g, w_up_g, w_down_g = [with_own_slab(l, s[None]) for l, s in zip(lands, srcs)]
        return w_out_g.reshape(-1, D_MODEL), _cols_from_shards(w_up_g), w_down_g.reshape(-1, D_MODEL)

    sent = {}
    G, Dl, NM, NV = {}, {}, {}, {}

    def finish(group, names, after):
        s_sems, r_sems, srcs, lands, _ = sent[group]
        srcs, lands = _push_wait(s_sems, r_sems, srcs, lands, after, name="exchange_" + group + "_wait",
                                 slab_per_peer=True)
        for n, src, land in zip(names, srcs, lands):
            parts = with_own_slab(land, lax.dynamic_slice(src, (me, 0, 0), (1, *src.shape[1:])))
            G[n], Dl[n], NM[n], NV[n] = _adamw(W[n], M[n], V[n], parts, tr=ADAM_TILE_ROWS[n], name="adamw_" + n)

    def send(group, parts):
        sent[group] = _push_start(parts, name="exchange_" + group + "_start", slab_per_peer=True)
        zeros = sent[group][4]
        if group == "late":
            finish("early", later, zeros)
            zeros = zeros + 0.0 * (Dl["w_out"][0:8, 0:LANES] + Dl["w_up"][0:8, 0:LANES] + Dl["w_down"][0:8, 0:LANES])
        return zeros

    loss_part, grad_x, g_small = _local_step(x[0], loss_target[0], wts, small,
                                              {"weights_later": weights_later, "send": send})
    finish("late", first, grad_x)

    conv_rows = _pad_rows(jnp.transpose(g_small["conv_w"].reshape(CONV_WIDTH, N_DEV, LANES), (1, 0, 2)), 8)
    loss_rows = jnp.zeros((8, LANES), F32).at[0, 0].set(loss_part)
    as_sent = lambda n: g_small[n].astype(BF16) if n in ("lru_wa", "lru_wx") else g_small[n]
    gathered = _all_gather([_pad_rows(_rows(as_sent(n)), 8) for n in SMALL]
                           + [conv_rows.reshape(CONV_ROWS, LANES), loss_rows], name="gather_small")
    k = len(SMALL)
    outs = _adamw_many([_rows(W[n]) for n in SMALL], [_rows(M[n]) for n in SMALL], [_rows(V[n]) for n in SMALL],
                       gathered[:k], gathered[k:], name="adamw_small")
    for j, n in enumerate(SMALL):
        for out, o in zip((G, Dl, NM, NV), outs[4 * j:4 * j + 4]):
            out[n] = o.reshape(W[n].shape)
    conv_sum, loss_sum = outs[4 * k:]
    loss = loss_sum[0, 0]

    g_conv = lax.dynamic_slice(conv_sum, (me * 8, 0), (8, LANES))
    conv_out = _adamw(_pad_rows(conv_w, 8), _pad_rows(m_conv_w, 8), _pad_rows(v_conv_w, 8), g_conv[None], tr=8,
                      name="adamw_conv_w")
    for out, pk in zip((G, Dl, NM, NV), conv_out):
        out["conv_w"] = pk[:CONV_WIDTH]
    return (loss, grad_x[None], *[G[n] for n in WEIGHTS], *[Dl[n] for n in WEIGHTS], *[NM[n] for n in WEIGHTS],
            *[NV[n] for n in WEIGHTS])
```

```python
import functools

import numpy as np
import jax
import jax.numpy as jnp
from jax import lax
from jax.experimental import pallas as pl
from jax.experimental.pallas import tpu as pltpu

F32 = jnp.float32
BF16 = jnp.bfloat16
MESH = pl.DeviceIdType.MESH

D_MODEL = 1024
N_DEV = 8
LANES = 128
RNN_BLOCKS = 8
RNN_BLOCK_W = 128
CONV_WIDTH = 4
LRU_C = 8.0
MLA_HEADS = 8
QK_NOPE = 128
QK_ROPE = 64
V_HEAD = 128
QK_PAD = 256
Q_LORA = 256
KV_LORA = 256
CKV_W = 640
ROPE_THETA = 10000.0
D_FF = 4096
EPS = 1e-6
ATTN_SCALE = (QK_NOPE + QK_ROPE) ** -0.5
LOG2E = 1.4426950408889634
LN2 = 0.6931471805599453
NEG = -1e30

ADAM_LR = 0.001
ADAM_B1 = 0.9
ADAM_B2 = 0.999
ADAM_EPS = 1e-08
ADAM_WD = 0.01
ADAM_STEP = 10

VMEM_LIMIT = 56 * 1024 * 1024


def _params(sem=None):
    return pltpu.CompilerParams(dimension_semantics=sem, vmem_limit_bytes=VMEM_LIMIT)


def _sigmoid(v):
    return 1.0 / (1.0 + jnp.exp(-v))


def _softplus(y):
    e = jnp.exp(-jnp.abs(y))
    u = 1.0 + e
    d = u - 1.0
    l1p = jnp.where(d == 0.0, e, jnp.log(u) * e / jnp.where(d == 0.0, 1.0, d))
    return jnp.maximum(y, 0.0) + l1p


_GELU_K = 0.7978845608028654
_GELU_C = 0.044715


def _gelu_and_grad(v):
    t = jnp.tanh(_GELU_K * (v + _GELU_C * v * v * v))
    g = 0.5 * v * (1.0 + t)
    dg = 0.5 * (1.0 + t) + 0.5 * v * (1.0 - t * t) * _GELU_K * (1.0 + 3.0 * _GELU_C * v * v)
    return g, dg


def _rms_fwd(v, g):
    rstd = lax.rsqrt(jnp.mean(v * v, axis=-1, keepdims=True) + EPS)
    return v * rstd * g, rstd


def _rms_bwd(dy, v, g):
    rstd = lax.rsqrt(jnp.mean(v * v, axis=-1, keepdims=True) + EPS)
    vh = v * rstd
    dvh = dy * g
    dv = rstd * (dvh - vh * jnp.mean(dvh * vh, axis=-1, keepdims=True))
    return dv, dy * vh


def _shift_down(v, s, fill, row):
    return jnp.where(row >= s, pltpu.roll(v, s, 0), fill)


def _shift_up(v, s, fill, row, n):
    return jnp.where(row < n - s, pltpu.roll(v, n - s, 0), fill)


def _rot_half(v, lane):
    n = v.shape[-1]
    l = lane & (LANES - 1)
    up = pltpu.roll(v, n - QK_ROPE // 2, 1)
    dn = pltpu.roll(v, QK_ROPE // 2, 1)
    return jnp.where(l < QK_ROPE // 2, -up, jnp.where(l < QK_ROPE, dn, 0.0))


def _mm(a, b, *, name, tm, tn, tk, outs, epilogue, extras=(), ta=False, tb=False, more=None):
    assert not (ta and tb)
    if ta:
        K, M = a.shape
    else:
        M, K = a.shape
    if tb:
        N, K2 = b.shape
    else:
        K2, N = b.shape
    assert K == K2 and M % tm == 0 and N % tn == 0 and K % tk == 0, (name, a.shape, b.shape)
    n_i, n_j, n_k = M // tm, N // tn, K // tk
    n_ex, n_out = len(extras), len(outs)
    n_more = 0 if more is None else 2
    assert more is None or (n_k == 1 and not ta)

    def body(*refs):
        a_ref, b_ref = refs[0], refs[1]
        ex_refs = refs[2 + n_more:2 + n_more + n_ex]
        out_refs = refs[2 + n_more + n_ex:2 + n_more + n_ex + n_out]
        if ta:
            part = lax.dot_general(a_ref[...], b_ref[...], (((0,), (0,)), ((), ())), preferred_element_type=F32)
        elif tb:
            part = lax.dot_general(a_ref[...], b_ref[...], (((1,), (1,)), ((), ())), preferred_element_type=F32)
        else:
            part = jnp.dot(a_ref[...], b_ref[...], preferred_element_type=F32)
        if more is not None:
            part = part + lax.dot_general(refs[2][...], refs[3][...], (((1,), (1,)), ((), ())),
                                          preferred_element_type=F32)

        def finish(acc):
            res = epilogue(acc, *[r[...] for r in ex_refs])
            for o_ref, r, spec in zip(out_refs, res, outs):
                if spec[0] == "cols":
                    o_ref[:, spec[3]:spec[3] + r.shape[1]] = r.astype(o_ref.dtype)
                else:
                    o_ref[...] = r.astype(o_ref.dtype).reshape(o_ref.shape)

        if n_k == 1:
            finish(part)
        else:
            acc_ref = refs[-1]
            k = pl.program_id(2)

            @pl.when(k == 0)
            def _():
                acc_ref[...] = part

            @pl.when(k > 0)
            def _():
                acc_ref[...] += part

            @pl.when(k == n_k - 1)
            def _():
                finish(acc_ref[...])

    a_spec = pl.BlockSpec((tk, tm), lambda j, i, k: (k, i)) if ta else pl.BlockSpec((tm, tk), lambda j, i, k: (i, k))
    b_once = dict(pipeline_mode=pl.Buffered(1)) if (n_j == 1 and n_k == 1) else {}
    if tb:
        in_specs = [a_spec, pl.BlockSpec((tn, tk), lambda j, i, k: (j, k), **b_once)]
    else:
        in_specs = [a_spec, pl.BlockSpec((tk, tn), lambda j, i, k: (k, j), **b_once)]
    if more is not None:
        k2 = more[0].shape[1]
        in_specs += [pl.BlockSpec((tm, k2), lambda j, i, k: (i, 0)), pl.BlockSpec((tn, k2), lambda j, i, k: (j, 0), **b_once)]
    for ex in extras:
        kind = ex[0]
        if kind == "tile":
            in_specs.append(pl.BlockSpec((tm, tn), lambda j, i, k: (i, j)))
        elif kind == "tilecol":
            assert n_j == 1
            in_specs.append(pl.BlockSpec((tm, tn), functools.partial(lambda c, j, i, k: (i, c), ex[2])))
        else:
            in_specs.append(pl.BlockSpec((1, tn), lambda j, i, k: (0, j)))
    out_specs, out_shape = [], []
    for kind, dt, *rest in outs:
        if kind == "tile":
            out_specs.append(pl.BlockSpec((tm, tn), lambda j, i, k: (i, j)))
            out_shape.append(jax.ShapeDtypeStruct((M, N), dt))
        elif kind == "colshard":
            out_specs.append(pl.BlockSpec((1, tm, tn), lambda j, i, k: (j, i, 0)))
            out_shape.append(jax.ShapeDtypeStruct((n_j, M, tn), dt))
        elif kind == "cols":
            assert n_j == 1
            out_specs.append(pl.BlockSpec((tm, rest[0]), lambda j, i, k: (i, 0)))
            out_shape.append(jax.ShapeDtypeStruct((M, rest[0]), dt))
        elif kind == "side":
            assert n_j == 1
            out_specs.append(pl.BlockSpec((tm, LANES), lambda j, i, k: (i, 0)))
            out_shape.append(jax.ShapeDtypeStruct((M, LANES), dt))
        else:
            out_specs.append(pl.BlockSpec((1, 1, tn), lambda j, i, k: (i, 0, j)))
            out_shape.append(jax.ShapeDtypeStruct((n_i, 1, N), dt))
    scratch = [pltpu.VMEM((tm, tn), F32)] if n_k > 1 else []
    return pl.pallas_call(
        body, name=name, grid=(n_j, n_i, n_k), in_specs=in_specs, out_specs=out_specs, out_shape=out_shape,
        scratch_shapes=scratch, compiler_params=_params(("parallel", "parallel", "arbitrary")),
    )(a, b, *(more or ()), *[ex[1] for ex in extras])


def _rmsnorm_cast(x, g, *, ts, name):
    S, D = x.shape

    def body(x_ref, g_ref, o_ref):
        y, _ = _rms_fwd(x_ref[...], g_ref[...])
        o_ref[...] = y.astype(BF16)

    return pl.pallas_call(
        body, name=name, grid=(S // ts,),
        in_specs=[pl.BlockSpec((ts, D), lambda i: (i, 0)), pl.BlockSpec((1, D), lambda i: (0, 0))],
        out_specs=pl.BlockSpec((ts, D), lambda i: (i, 0)), out_shape=jax.ShapeDtypeStruct((S, D), BF16),
        compiler_params=_params(("parallel",)),
    )(x, g)


LRU_NB = 4


def _lru_gates(xa, wa_ref, ba_ref, wx_ref, bx_ref, lam):
    xab = xa.astype(BF16)
    W = RNN_BLOCK_W
    rs, is_ = [], []
    for j in range(LRU_NB):
        xj = xab[:, j * W:(j + 1) * W]
        rs.append(_sigmoid(jnp.dot(xj, wa_ref[j], preferred_element_type=F32) + ba_ref[j]))
        is_.append(_sigmoid(jnp.dot(xj, wx_ref[j], preferred_element_type=F32) + bx_ref[j]))
    r = jnp.concatenate(rs, axis=1)
    i = jnp.concatenate(is_, axis=1)
    sp = _softplus(-lam)
    log_a = (-LRU_C * r) * sp
    a = jnp.exp(log_a)
    y = 2.0 * log_a
    one_m = jnp.where(y > -0.01, -y * (1.0 + 0.5 * y * (1.0 + y * (1.0 / 3.0))), 1.0 - a * a)
    return r, i, sp, a, jnp.sqrt(one_m)


def _rows_before(x, tail8, k):
    e16 = jnp.concatenate([tail8, x[0:8, :]], axis=0)
    return jnp.concatenate([pltpu.roll(e16, k, 0)[8:16, :], pltpu.roll(x, k, 0)[8:, :]], axis=0)


def _rows_after(x, head8, k):
    tt = x.shape[0]
    e16 = jnp.concatenate([x[tt - 8:tt, :], head8], axis=0)
    return jnp.concatenate([pltpu.roll(x, tt - k, 0)[:tt - 8, :], pltpu.roll(e16, 16 - k, 0)[0:8, :]], axis=0)


def _scan_down(a, b, h0, a_s, b_s, c_s):
    tt, C = a.shape
    G, nch = tt // 8, C // LANES
    rin = lax.broadcasted_iota(jnp.int32, (tt, C), 0) & 7

    def in_group(v, s):
        return pltpu.roll(v.reshape(G, 8, C), s, 1).reshape(tt, C)

    A, B = a, b
    for s in (1, 2, 4):
        B = A * jnp.where(rin >= s, in_group(B, s), 0.0) + B
        A = A * jnp.where(rin >= s, in_group(A, s), 1.0)
    for j in range(nch):
        a_s[j] = A[:, j * LANES:(j + 1) * LANES]
        b_s[j] = B[:, j * LANES:(j + 1) * LANES]
    At = jnp.concatenate([a_s.at[j][pl.ds(7, G, stride=8), :] for j in range(nch)], axis=1)
    Bt = jnp.concatenate([b_s.at[j][pl.ds(7, G, stride=8), :] for j in range(nch)], axis=1)
    rowg = lax.broadcasted_iota(jnp.int32, (G, C), 0)
    s = 1
    while s < G:
        Bt = At * _shift_down(Bt, s, 0.0, rowg) + Bt
        At = At * _shift_down(At, s, 1.0, rowg)
        s *= 2
    hg = At * h0 + Bt
    cin = _shift_down(hg, 1, h0, rowg)
    for j in range(nch):
        for r in range(8):
            c_s.at[j][pl.ds(r, G, stride=8), :] = cin[:, j * LANES:(j + 1) * LANES]
    return A * jnp.concatenate([c_s[j] for j in range(nch)], axis=1) + B, hg[G - 1:G, :]


def _scan_up(c, g_in, g_next, a_s, b_s, c_s):
    tt, C = c.shape
    G, nch = tt // 8, C // LANES
    rin = lax.broadcasted_iota(jnp.int32, (tt, C), 0) & 7

    def in_group(v, s):
        return pltpu.roll(v.reshape(G, 8, C), 8 - s, 1).reshape(tt, C)

    Cc, Gv = c, g_in
    for s in (1, 2, 4):
        Gv = Gv + Cc * jnp.where(rin < 8 - s, in_group(Gv, s), 0.0)
        Cc = Cc * jnp.where(rin < 8 - s, in_group(Cc, s), 1.0)
    for j in range(nch):
        a_s[j] = Cc[:, j * LANES:(j + 1) * LANES]
        b_s[j] = Gv[:, j * LANES:(j + 1) * LANES]
    Ct = jnp.concatenate([a_s.at[j][pl.ds(0, G, stride=8), :] for j in range(nch)], axis=1)
    Gt = jnp.concatenate([b_s.at[j][pl.ds(0, G, stride=8), :] for j in range(nch)], axis=1)
    rowg = lax.broadcasted_iota(jnp.int32, (G, C), 0)
    s = 1
    while s < G:
        Gt = Gt + Ct * _shift_up(Gt, s, 0.0, rowg, G)
        Ct = Ct * _shift_up(Ct, s, 1.0, rowg, G)
        s *= 2
    gg = Gt + Ct * g_next
    cin = _shift_up(gg, 1, g_next, rowg, G)
    for j in range(nch):
        for r in range(8):
            c_s.at[j][pl.ds(r, G, stride=8), :] = cin[:, j * LANES:(j + 1) * LANES]
    return Gv + Cc * jnp.concatenate([c_s[j] for j in range(nch)], axis=1), gg[0:1, :]


def _lru_fwd(z_rx, conv_w, conv_b, wa, ba, wx, bx, lam, *, tt):
    S = z_rx.shape[0]
    n_t = S // tt
    BW = RNN_BLOCK_W
    W = LRU_NB * BW

    def body(x_ref, cw_ref, cb_ref, wa_ref, ba_ref, wx_ref, bx_ref, lam_ref, h_ref, tail, hc, a_s, b_s, c_s):
        t = pl.program_id(1)

        @pl.when(t == 0)
        def _():
            tail[...] = jnp.zeros((8, W), F32)
            hc[...] = jnp.zeros((8, W), F32)

        x = x_ref[...]
        before = tail[...]
        cw = cw_ref[...]
        xa = (cb_ref[...] + cw[3:4] * x + cw[2:3] * _rows_before(x, before, 1) + cw[1:2] * _rows_before(x, before, 2)
              + cw[0:1] * _rows_before(x, before, 3))
        tail[...] = x[tt - 8:tt, :]
        _r, i, _sp, a, mult = _lru_gates(xa, wa_ref, ba_ref, wx_ref, bx_ref, lam_ref[...])
        h, h_last = _scan_down(a, mult * (i * xa), hc[0:1, :], a_s, b_s, c_s)
        h_ref[...] = h
        hc[...] = jnp.broadcast_to(h_last, (8, W))

    blk = lambda n, t: (t, n)
    vec = pl.BlockSpec((1, W), lambda n, t: (0, n))
    mat = pl.BlockSpec((LRU_NB, BW, BW), lambda n, t: (n, 0, 0))
    bias = pl.BlockSpec((LRU_NB, 1, BW), lambda n, t: (n, 0, 0))
    row8 = pltpu.VMEM((8, W), F32)
    wide = pltpu.VMEM((LRU_NB, tt, LANES), F32)
    return pl.pallas_call(
        body, name="lru_fwd", grid=(RNN_BLOCKS // LRU_NB, n_t),
        in_specs=[pl.BlockSpec((tt, W), blk), pl.BlockSpec((CONV_WIDTH, W), lambda n, t: (0, n)), vec, mat, bias, mat,
                  bias, vec],
        out_specs=pl.BlockSpec((tt, W), blk), out_shape=jax.ShapeDtypeStruct((S, D_MODEL), F32),
        scratch_shapes=[row8, row8, wide, wide, wide],
        compiler_params=_params(("parallel", "arbitrary")),
    )(z_rx, conv_w, conv_b, wa, ba, wx, bx, lam)


def _lru_bwd(z_rx, h, dh, dz, conv_w, conv_b, wa, wat, ba, wx, wxt, bx, lam, *, tt):
    S = z_rx.shape[0]
    n_t = S // tt
    BW = RNN_BLOCK_W
    W = LRU_NB * BW
    t8 = tt // 8

    def body(x_ref, xp_ref, h_ref, hp_ref, dh_ref, _dz_ref, cw_ref, cb_ref, wa_ref, wat_ref, ba_ref, wx_ref, wxt_ref,
             bx_ref, lam_ref, dx_ref, dwa_ref, dwx_ref, dba_ref, dbx_ref, dlam_ref, dcw_ref, dcb_ref, nxt, a_c, g_c, a_s,
             b_s, c_s):
        t = pl.program_id(1)
        tile = n_t - 1 - t

        @pl.when(t == 0)
        def _():
            a_c[...] = jnp.zeros((8, W), F32)
            g_c[...] = jnp.zeros((8, W), F32)
            nxt[...] = jnp.zeros((8, W), F32)
            dwa_ref[...] = jnp.zeros_like(dwa_ref)
            dwx_ref[...] = jnp.zeros_like(dwx_ref)
            dba_ref[...] = jnp.zeros_like(dba_ref)
            dbx_ref[...] = jnp.zeros_like(dbx_ref)
            dlam_ref[...] = jnp.zeros_like(dlam_ref)
            dcw_ref[...] = jnp.zeros_like(dcw_ref)
            dcb_ref[...] = jnp.zeros_like(dcb_ref)

        has_prev = (tile > 0).astype(F32)
        x = x_ref[...]
        before = xp_ref[...] * has_prev
        xm1, xm2, xm3 = _rows_before(x, before, 1), _rows_before(x, before, 2), _rows_before(x, before, 3)
        cw = cw_ref[...]
        xa = cb_ref[...] + cw[3:4] * x + cw[2:3] * xm1 + cw[1:2] * xm2 + cw[0:1] * xm3
        lam = lam_ref[...]
        r, i, sp, a, mult = _lru_gates(xa, wa_ref, ba_ref, wx_ref, bx_ref, lam)
        gated = i * xa
        h_prev = _rows_before(h_ref[...], hp_ref[...] * has_prev, 1)
        g, g_first = _scan_up(_rows_after(a, a_c[...], 1), dh_ref[...], g_c[0:1, :], a_s, b_s, c_s)
        a_c[...] = jnp.broadcast_to(a[0:1, :], (8, W))
        g_c[...] = jnp.broadcast_to(g_first, (8, W))
        dlog_a = g * h_prev * a - g * gated * (a * a) / mult
        dgated = g * mult
        di = dgated * xa
        dxa = dgated * i
        dr = dlog_a * (-LRU_C * sp)
        dlam_ref[...] += jnp.sum(dlog_a * (-LRU_C * r), axis=0, keepdims=True) * (-_sigmoid(-lam))
        dpr = dr * r * (1.0 - r)
        dpi = di * i * (1.0 - i)
        xab, dprb, dpib = xa.astype(BF16), dpr.astype(BF16), dpi.astype(BF16)
        tn_dims = (((0,), (0,)), ((), ()))
        back = []
        for j in range(LRU_NB):
            sl = slice(j * BW, (j + 1) * BW)
            dwa_ref[j] += lax.dot_general(xab[:, sl], dprb[:, sl], tn_dims, preferred_element_type=F32)
            dwx_ref[j] += lax.dot_general(xab[:, sl], dpib[:, sl], tn_dims, preferred_element_type=F32)
            dba_ref[j] += jnp.sum(dpr[:, sl], axis=0, keepdims=True)
            dbx_ref[j] += jnp.sum(dpi[:, sl], axis=0, keepdims=True)
            back.append(jnp.dot(dprb[:, sl], wat_ref[j], preferred_element_type=F32)
                        + jnp.dot(dpib[:, sl], wxt_ref[j], preferred_element_type=F32))
        dxa = dxa + jnp.concatenate(back, axis=1)
        after = nxt[...]
        dx = (cw[3:4] * dxa + cw[2:3] * _rows_after(dxa, after, 1) + cw[1:2] * _rows_after(dxa, after, 2)
              + cw[0:1] * _rows_after(dxa, after, 3))
        nxt[...] = dxa[0:8, :]
        dx_ref[...] = dx.astype(BF16)
        dcw_ref[3:4, :] += jnp.sum(dxa * x, axis=0, keepdims=True)
        dcw_ref[2:3, :] += jnp.sum(dxa * xm1, axis=0, keepdims=True)
        dcw_ref[1:2, :] += jnp.sum(dxa * xm2, axis=0, keepdims=True)
        dcw_ref[0:1, :] += jnp.sum(dxa * xm3, axis=0, keepdims=True)
        dcb_ref[...] += jnp.sum(dxa, axis=0, keepdims=True)

    blk = lambda n, t: (n_t - 1 - t, n)
    prev = lambda n, t: (jnp.maximum((n_t - 1 - t) * t8 - 1, 0), n)
    vec = pl.BlockSpec((1, W), lambda n, t: (0, n))
    mat = pl.BlockSpec((LRU_NB, BW, BW), lambda n, t: (n, 0, 0))
    bias = pl.BlockSpec((LRU_NB, 1, BW), lambda n, t: (n, 0, 0))
    cws = pl.BlockSpec((CONV_WIDTH, W), lambda n, t: (0, n))
    tile = pl.BlockSpec((tt, W), blk)
    prev8 = pl.BlockSpec((8, W), prev)
    row8 = pltpu.VMEM((8, W), F32)
    wide = pltpu.VMEM((LRU_NB, tt, LANES), F32)
    return pl.pallas_call(
        body, name="lru_bwd", grid=(RNN_BLOCKS // LRU_NB, n_t),
        in_specs=[tile, prev8, tile, prev8, tile, pl.BlockSpec(memory_space=pl.ANY), cws, vec, mat, mat, bias, mat, mat,
                  bias, vec],
        out_specs=[tile, mat, mat, bias, bias, vec, cws, vec], input_output_aliases={5: 0},
        out_shape=[jax.ShapeDtypeStruct(dz.shape, BF16),
                   jax.ShapeDtypeStruct((RNN_BLOCKS, BW, BW), F32), jax.ShapeDtypeStruct((RNN_BLOCKS, BW, BW), F32),
                   jax.ShapeDtypeStruct((RNN_BLOCKS, 1, BW), F32), jax.ShapeDtypeStruct((RNN_BLOCKS, 1, BW), F32),
                   jax.ShapeDtypeStruct((1, D_MODEL), F32),
                   jax.ShapeDtypeStruct((CONV_WIDTH, D_MODEL), F32), jax.ShapeDtypeStruct((1, D_MODEL), F32)],
        scratch_shapes=[row8, row8, row8, wide, wide, wide],
        compiler_params=_params(("parallel", "arbitrary")),
    )(z_rx, z_rx, h, h, dh, dz, conv_w, conv_b, wa, wat, ba, wx, wxt, bx, lam)


def _mla_proj(z_ckv, q_norm, kv_norm, w_uq, w_ukv, cos, sin, *, ts):
    S = z_ckv.shape[0]
    H = MLA_HEADS

    def body(c_ref, qn_ref, kn_ref, wq_ref, wkv_ref, cos_ref, sin_ref, q_ref, k_ref, v_ref):
        c = c_ref[...]
        cqn, _ = _rms_fwd(c[:, 0:Q_LORA], qn_ref[...])
        ckn, _ = _rms_fwd(c[:, Q_LORA:Q_LORA + KV_LORA], kn_ref[...])
        q = jnp.dot(cqn.astype(BF16), wq_ref[...], preferred_element_type=F32) * (ATTN_SCALE * LOG2E)
        kv = jnp.dot(ckn.astype(BF16), wkv_ref[...], preferred_element_type=F32)
        cos1, sin1 = cos_ref[...], sin_ref[...]
        cos8 = jnp.concatenate([cos1] * H, axis=1)
        sin8 = jnp.concatenate([sin1] * H, axis=1)
        qr = q[:, H * QK_NOPE:]
        lane8 = lax.broadcasted_iota(jnp.int32, qr.shape, 1)
        qr = qr * cos8 + _rot_half(qr, lane8) * sin8
        kr = c[:, Q_LORA + KV_LORA:]
        lane1 = lax.broadcasted_iota(jnp.int32, kr.shape, 1)
        kr = (kr * cos1 + _rot_half(kr, lane1) * sin1).astype(BF16)
        for h in range(H):
            q_ref[h, :, 0:QK_NOPE] = q[:, h * QK_NOPE:(h + 1) * QK_NOPE].astype(BF16)
            q_ref[h, :, QK_NOPE:] = qr[:, h * LANES:(h + 1) * LANES].astype(BF16)
            k_ref[h, :, 0:QK_NOPE] = kv[:, h * 2 * LANES:h * 2 * LANES + LANES].astype(BF16)
            k_ref[h, :, QK_NOPE:] = kr
            v_ref[h] = kv[:, h * 2 * LANES + LANES:(h + 1) * 2 * LANES].astype(BF16)

    full = lambda shape: pl.BlockSpec(shape, lambda i: (0,) * len(shape))
    return pl.pallas_call(
        body, name="mla_proj", grid=(S // ts,),
        in_specs=[pl.BlockSpec((ts, CKV_W), lambda i: (i, 0)), full((1, Q_LORA)), full((1, KV_LORA)),
                  full(w_uq.shape), full(w_ukv.shape), pl.BlockSpec((ts, LANES), lambda i: (i, 0)),
                  pl.BlockSpec((ts, LANES), lambda i: (i, 0))],
        out_specs=[pl.BlockSpec((H, ts, QK_PAD), lambda i: (0, i, 0)), pl.BlockSpec((H, ts, QK_PAD), lambda i: (0, i, 0)),
                   pl.BlockSpec((H, ts, V_HEAD), lambda i: (0, i, 0))],
        out_shape=[jax.ShapeDtypeStruct((H, S, QK_PAD), BF16), jax.ShapeDtypeStruct((H, S, QK_PAD), BF16),
                   jax.ShapeDtypeStruct((H, S, V_HEAD), BF16)],
        compiler_params=_params(("parallel",)),
    )(z_ckv, q_norm, kv_norm, w_uq, w_ukv, cos, sin)


def _mla_proj_bwd(z_ckv, dq, dk, dv, q_norm, kv_norm, w_uqt, w_ukvt, cos, sin, *, ts):
    S = z_ckv.shape[0]
    H = MLA_HEADS

    def body(c_ref, dq_ref, dk_ref, dv_ref, qn_ref, kn_ref, wqt_ref, wkvt_ref, cos_ref, sin_ref,
             dz_ref, dwq_ref, dwkv_ref, dqn_ref, dkn_ref):
        @pl.when(pl.program_id(0) == 0)
        def _():
            dwq_ref[...] = jnp.zeros_like(dwq_ref)
            dwkv_ref[...] = jnp.zeros_like(dwkv_ref)
            dqn_ref[...] = jnp.zeros_like(dqn_ref)
            dkn_ref[...] = jnp.zeros_like(dkn_ref)

        c = c_ref[...]
        cq, ck = c[:, 0:Q_LORA], c[:, Q_LORA:Q_LORA + KV_LORA]
        qn, kn = qn_ref[...], kn_ref[...]
        cqn, _ = _rms_fwd(cq, qn)
        ckn, _ = _rms_fwd(ck, kn)
        cos1, sin1 = cos_ref[...], sin_ref[...]
        lane1 = lax.broadcasted_iota(jnp.int32, cos1.shape, 1)

        def unrope(g):
            return g * cos1 - _rot_half(g * sin1, lane1)

        dq_all = jnp.concatenate([dq_ref[h, :, 0:QK_NOPE] for h in range(H)]
                                 + [unrope(dq_ref[h, :, QK_NOPE:]) for h in range(H)], axis=1)
        dq_all = (dq_all * ATTN_SCALE).astype(BF16)
        dkv_all = jnp.concatenate([p for h in range(H) for p in (dk_ref[h, :, 0:QK_NOPE], dv_ref[h])],
                                  axis=1).astype(BF16)
        dkr = dk_ref[0, :, QK_NOPE:].astype(F32)
        for h in range(1, H):
            dkr = dkr + dk_ref[h, :, QK_NOPE:].astype(F32)
        dkr = unrope(dkr)
        tn_dims = (((0,), (0,)), ((), ()))
        dwq_ref[...] += lax.dot_general(cqn.astype(BF16), dq_all, tn_dims, preferred_element_type=F32)
        dwkv_ref[...] += lax.dot_general(ckn.astype(BF16), dkv_all, tn_dims, preferred_element_type=F32)
        dcqn = jnp.dot(dq_all, wqt_ref[...], preferred_element_type=F32)
        dckn = jnp.dot(dkv_all, wkvt_ref[...], preferred_element_type=F32)
        dcq, dqn_rows = _rms_bwd(dcqn, cq, qn)
        dck, dkn_rows = _rms_bwd(dckn, ck, kn)
        dqn_ref[...] += jnp.sum(dqn_rows, axis=0, keepdims=True)
        dkn_ref[...] += jnp.sum(dkn_rows, axis=0, keepdims=True)
        dz_ref[:, 0:Q_LORA] = dcq.astype(BF16)
        dz_ref[:, Q_LORA:Q_LORA + KV_LORA] = dck.astype(BF16)
        dz_ref[:, Q_LORA + KV_LORA:] = dkr.astype(BF16)

    full = lambda shape: pl.BlockSpec(shape, lambda i: (0,) * len(shape))
    return pl.pallas_call(
        body, name="mla_proj_bwd", grid=(S // ts,),
        in_specs=[pl.BlockSpec((ts, CKV_W), lambda i: (i, 0)), pl.BlockSpec((H, ts, QK_PAD), lambda i: (0, i, 0)),
                  pl.BlockSpec((H, ts, QK_PAD), lambda i: (0, i, 0)), pl.BlockSpec((H, ts, V_HEAD), lambda i: (0, i, 0)),
                  full((1, Q_LORA)), full((1, KV_LORA)), full(w_uqt.shape), full(w_ukvt.shape),
                  pl.BlockSpec((ts, LANES), lambda i: (i, 0)), pl.BlockSpec((ts, LANES), lambda i: (i, 0))],
        out_specs=[pl.BlockSpec((ts, CKV_W), lambda i: (i, 0)), full((Q_LORA, w_uqt.shape[0])),
                   full((KV_LORA, w_ukvt.shape[0])), full((1, Q_LORA)), full((1, KV_LORA))],
        out_shape=[jax.ShapeDtypeStruct((S, CKV_W), BF16), jax.ShapeDtypeStruct((Q_LORA, w_uqt.shape[0]), F32),
                   jax.ShapeDtypeStruct((KV_LORA, w_ukvt.shape[0]), F32), jax.ShapeDtypeStruct((1, Q_LORA), F32),
                   jax.ShapeDtypeStruct((1, KV_LORA), F32)],
        compiler_params=_params(("arbitrary",)),
    )(z_ckv, dq, dk, dv, q_norm, kv_norm, w_uqt, w_ukvt, cos, sin)


NT_DIMS = (((1,), (1,)), ((), ()))
TN_DIMS = (((0,), (0,)), ((), ()))


def _attn_fwd(q, k, v, *, t, hb):
    H, S, _ = q.shape
    n = S // t
    pairs = [(i, j) for i in range(n) for j in range(i + 1)]
    qi = jnp.asarray(np.array([p[0] for p in pairs], np.int32))
    ki = jnp.asarray(np.array([p[1] for p in pairs], np.int32))

    def body(qi_ref, ki_ref, q_ref, k_ref, v_ref, o_ref, lse_ref, m_s, l_s, acc_s):
        p = pl.program_id(1)
        i, j = qi_ref[p], ki_ref[p]

        @pl.when(j == 0)
        def _():
            m_s[...] = jnp.full(m_s.shape, NEG, F32)
            l_s[...] = jnp.zeros(l_s.shape, F32)
            acc_s[...] = jnp.zeros(acc_s.shape, F32)

        def block(hh, r0, nr, nk, masked):
            rows = slice(r0, r0 + nr)
            s = lax.dot_general(q_ref[hh, rows, :], k_ref[hh, 0:nk, :], NT_DIMS, preferred_element_type=F32)
            if masked:
                row = lax.broadcasted_iota(jnp.int32, (nr, nk), 0) + r0
                col = lax.broadcasted_iota(jnp.int32, (nr, nk), 1)
                s = jnp.where(row >= col, s, NEG)
            chunks = nk // LANES
            mc = s[:, 0:LANES]
            for c in range(1, chunks):
                mc = jnp.maximum(mc, s[:, c * LANES:(c + 1) * LANES])
            m_prev = m_s[hh, rows, :]
            m_new = jnp.maximum(m_prev, jnp.max(mc, axis=1, keepdims=True))
            alpha = jnp.exp2(m_prev - m_new)
            pr = jnp.exp2(s - jnp.concatenate([m_new] * chunks, axis=1))
            ls = pr[:, 0:LANES]
            for c in range(1, chunks):
                ls = ls + pr[:, c * LANES:(c + 1) * LANES]
            l_s[hh, rows, :] = alpha * l_s[hh, rows, :] + ls
            acc_s[hh, rows, :] = alpha * acc_s[hh, rows, :] + jnp.dot(pr.astype(BF16), v_ref[hh, 0:nk, :],
                                                                      preferred_element_type=F32)
            m_s[hh, rows, :] = m_new

        def step(diagonal):
            for hh in range(hb):
                if diagonal:
                    block(hh, 0, t // 2, t // 2, True)
                    block(hh, t // 2, t // 2, t, True)
                else:
                    block(hh, 0, t, t, False)

        @pl.when(j < i)
        def _():
            step(False)

        @pl.when(j == i)
        def _():
            step(True)
            for hh in range(hb):
                l = jnp.sum(l_s[hh], axis=1, keepdims=True)
                o_ref[:, hh * V_HEAD:(hh + 1) * V_HEAD] = acc_s[hh] / l
                lse_ref[hh] = (m_s[hh] + jnp.log2(l)).T[0:1, :]

    grid_spec = pltpu.PrefetchScalarGridSpec(
        num_scalar_prefetch=2, grid=(H // hb, len(pairs)),
        in_specs=[pl.BlockSpec((hb, t, QK_PAD), lambda h, p, qi, ki: (h, qi[p], 0)),
                  pl.BlockSpec((hb, t, QK_PAD), lambda h, p, qi, ki: (h, ki[p], 0)),
                  pl.BlockSpec((hb, t, V_HEAD), lambda h, p, qi, ki: (h, ki[p], 0))],
        out_specs=[pl.BlockSpec((t, hb * V_HEAD), lambda h, p, qi, ki: (qi[p], h)),
                   pl.BlockSpec((hb, 1, t), lambda h, p, qi, ki: (h, 0, qi[p]))],
        scratch_shapes=[pltpu.VMEM((hb, t, LANES), F32), pltpu.VMEM((hb, t, LANES), F32),
                        pltpu.VMEM((hb, t, V_HEAD), F32)],
    )
    return pl.pallas_call(
        body, name="attn_fwd", grid_spec=grid_spec,
        out_shape=[jax.ShapeDtypeStruct((S, H * V_HEAD), F32), jax.ShapeDtypeStruct((H, 1, S), F32)],
        compiler_params=_params(("parallel", "arbitrary")),
    )(qi, ki, q, k, v)


def _attn_bwd(q, k, v, do, lse_row, delta_row, *, t):
    H, S, _ = q.shape
    n = S // t
    pairs = [(i, j) for j in range(n) for i in range(j, n)]
    qi = jnp.asarray(np.array([p[0] for p in pairs], np.int32))
    ki = jnp.asarray(np.array([p[1] for p in pairs], np.int32))

    def body(qi_ref, ki_ref, q_ref, k_ref, v_ref, do_ref, lse_ref, dl_ref, dq_ref, dk_ref, dv_ref, dk_s, dv_s, dq_s):
        p = pl.program_id(1)
        i, j = qi_ref[p], ki_ref[p]

        @pl.when(p == 0)
        def _():
            dq_s[...] = jnp.zeros_like(dq_s)

        def block(k0, nk, q0, nq, masked):
            qb, dob = q_ref[0, q0:q0 + nq, :], do_ref[q0:q0 + nq, :]
            kb, vb = k_ref[0, k0:k0 + nk, :], v_ref[0, k0:k0 + nk, :]
            st = lax.dot_general(kb, qb, NT_DIMS, preferred_element_type=F32)
            if masked:
                krow = lax.broadcasted_iota(jnp.int32, (nk, nq), 0) + k0
                qcol = lax.broadcasted_iota(jnp.int32, (nk, nq), 1) + q0
                st = jnp.where(krow <= qcol, st, NEG)
            pt = jnp.exp2(st - lse_ref[0][:, q0:q0 + nq])
            dvp = jnp.dot(pt.astype(BF16), dob, preferred_element_type=F32)
            dpt = lax.dot_general(vb, dob, NT_DIMS, preferred_element_type=F32)
            dst = (pt * (dpt - dl_ref[0][:, q0:q0 + nq])).astype(BF16)
            dkp = jnp.dot(dst, qb, preferred_element_type=F32)
            rows = pl.ds(pl.multiple_of(i * t + q0, LANES), nq)
            dq_s[rows, :] += lax.dot_general(dst, kb, TN_DIMS, preferred_element_type=F32)
            return dkp, dvp

        @pl.when(i == j)
        def _():
            half = t // 2
            dk_s[0:half, :], dv_s[0:half, :] = block(0, half, 0, t, True)
            dk_s[half:t, :], dv_s[half:t, :] = block(half, half, half, half, True)

        @pl.when(i != j)
        def _():
            dkp, dvp = block(0, t, 0, t, False)
            dk_s[...] += dkp
            dv_s[...] += dvp

        @pl.when(i == n - 1)
        def _():
            dk_ref[0] = (dk_s[...] * LN2).astype(BF16)
            dv_ref[0] = dv_s[...].astype(BF16)

        @pl.when(p == len(pairs) - 1)
        def _():
            dq_ref[0] = dq_s[...].astype(BF16)

    grid_spec = pltpu.PrefetchScalarGridSpec(
        num_scalar_prefetch=2, grid=(H, len(pairs)),
        in_specs=[pl.BlockSpec((1, t, QK_PAD), lambda h, p, qi, ki: (h, qi[p], 0)),
                  pl.BlockSpec((1, t, QK_PAD), lambda h, p, qi, ki: (h, ki[p], 0)),
                  pl.BlockSpec((1, t, V_HEAD), lambda h, p, qi, ki: (h, ki[p], 0)),
                  pl.BlockSpec((t, V_HEAD), lambda h, p, qi, ki: (qi[p], h)),
                  pl.BlockSpec((1, 1, t), lambda h, p, qi, ki: (h, 0, qi[p])),
                  pl.BlockSpec((1, 1, t), lambda h, p, qi, ki: (h, 0, qi[p]))],
        out_specs=[pl.BlockSpec((1, S, QK_PAD), lambda h, p, qi, ki: (h, 0, 0)),
                   pl.BlockSpec((1, t, QK_PAD), lambda h, p, qi, ki: (h, ki[p], 0)),
                   pl.BlockSpec((1, t, V_HEAD), lambda h, p, qi, ki: (h, ki[p], 0))],
        scratch_shapes=[pltpu.VMEM((t, QK_PAD), F32), pltpu.VMEM((t, V_HEAD), F32), pltpu.VMEM((S, QK_PAD), F32)],
    )
    return pl.pallas_call(
        body, name="attn_bwd", grid_spec=grid_spec,
        out_shape=[jax.ShapeDtypeStruct((H, S, QK_PAD), BF16), jax.ShapeDtypeStruct((H, S, QK_PAD), BF16),
                   jax.ShapeDtypeStruct((H, S, V_HEAD), BF16)],
        compiler_params=_params(("parallel", "arbitrary")),
    )(qi, ki, q, k, v, do, lse_row, delta_row)


def _merge_h1(h, z_gates, o, x, w_out, norm_mlp, *, ts):
    S = h.shape[0]
    D = D_MODEL

    def body(h_ref, rg_ref, ga_ref, gb_ref, o_ref, x_ref, w_ref, g_ref, m_ref, h1_ref, n2_ref):
        gl, _ = _gelu_and_grad(rg_ref[...].astype(F32))
        m = (_sigmoid(ga_ref[...].astype(F32)) * (h_ref[...] * gl)
             + _sigmoid(gb_ref[...].astype(F32)) * o_ref[...]).astype(BF16)
        m_ref[...] = m
        h1 = x_ref[...] + jnp.dot(m, w_ref[...], preferred_element_type=F32)
        h1_ref[...] = h1
        n2, _ = _rms_fwd(h1, g_ref[...])
        n2_ref[...] = n2.astype(BF16)

    col = lambda c: pl.BlockSpec((ts, D), lambda i: (i, c))
    fixed = lambda shape: pl.BlockSpec(shape, lambda i: (0, 0), pipeline_mode=pl.Buffered(1))
    return pl.pallas_call(
        body, name="merge_h1", grid=(S // ts,),
        in_specs=[col(0), col(0), col(1), col(2), col(0), col(0), fixed((D, D)), fixed((1, D))],
        out_specs=[col(0), col(0), col(0)],
        out_shape=[jax.ShapeDtypeStruct((S, D), BF16), jax.ShapeDtypeStruct((S, D), F32),
                   jax.ShapeDtypeStruct((S, D), BF16)],
        compiler_params=_params(("parallel",)),
    )(h, z_gates, z_gates, z_gates, o, x, w_out, norm_mlp)


def _my_place():
    return lax.axis_index("x"), lax.axis_index("y"), lax.axis_index("c")


def _all_gather(shards, *, name):
    n = len(shards)

    def body(*refs):
        x_refs, out_refs = refs[:n], refs[n:2 * n]
        send_sems, recv_sems, local_sems = refs[2 * n:]
        x, y, c = _my_place()
        me, sibling = (x, y, c), (x, y, 1 - c)
        chips = [(1 - x, y), (x, 1 - y), (1 - x, 1 - y)]

        def slot(a, px, py, pc):
            return out_refs[a].at[4 * px + 2 * py + pc]

        def copy(a, k, block, to, src=None):
            return pltpu.make_async_remote_copy(
                src_ref=slot(a, *block) if src is None else src, dst_ref=slot(a, *block),
                send_sem=send_sems.at[7 * a + k], recv_sem=recv_sems.at[7 * a + k], device_id=to, device_id_type=MESH)

        mine = [pltpu.make_async_copy(x_refs[a], slot(a, *me), local_sems.at[a]) for a in range(n)]
        for cp in mine:
            cp.start()
        first = []
        for a in range(n):
            first.append(copy(a, 0, me, sibling, src=x_refs[a]))
            first += [copy(a, 1 + j, me, (*chip, c), src=x_refs[a]) for j, chip in enumerate(chips)]
        for cp in first:
            cp.start()
        passed = []
        for a in range(n):
            for j, chip in enumerate(chips):
                copy(a, 1 + j, (*chip, c), me).wait_recv()
                fwd = copy(a, 4 + j, (*chip, c), sibling)
                fwd.start()
                passed.append(fwd)
        for a in range(n):
            copy(a, 0, sibling, me).wait_recv()
            for j, chip in enumerate(chips):
                copy(a, 4 + j, (*chip, 1 - c), me).wait_recv()
        for cp in first + passed:
            cp.wait_send()
        for cp in mine:
            cp.wait()

    hbm = pl.BlockSpec(memory_space=pl.ANY)
    return pl.pallas_call(
        body, name=name, out_shape=[jax.ShapeDtypeStruct((N_DEV, *s.shape), s.dtype) for s in shards],
        in_specs=[hbm] * n, out_specs=[hbm] * n,
        scratch_shapes=[pltpu.SemaphoreType.DMA((7 * n,)), pltpu.SemaphoreType.DMA((7 * n,)),
                        pltpu.SemaphoreType.DMA((n,))],
    )(*shards)


def _pushes(src_refs, land_refs, send_sems, recv_sems, slab_per_peer):
    x, y, c = _my_place()
    me = 4 * x + 2 * y + c
    copies = []
    for a in range(len(src_refs)):
        for k in range(1, N_DEV):
            px, py, pc = x ^ (k >> 2), y ^ ((k >> 1) & 1), c ^ (k & 1)
            src = src_refs[a].at[4 * px + 2 * py + pc] if slab_per_peer else src_refs[a]
            copies.append(pltpu.make_async_remote_copy(
                src_ref=src, dst_ref=land_refs[a].at[me], send_sem=send_sems.at[7 * a + k - 1],
                recv_sem=recv_sems.at[7 * a + k - 1], device_id=(px, py, pc), device_id_type=MESH))
    return copies


def _push_start(srcs, *, name, slab_per_peer):
    n = len(srcs)
    lands = [lax.empty((N_DEV, *(s.shape[1:] if slab_per_peer else s.shape)), s.dtype) for s in srcs]

    def body(*refs):
        src_refs, land_refs = refs[:n], refs[n:2 * n]
        send_sems, recv_sems, token = refs[2 * n], refs[2 * n + 1], refs[-1]
        for cp in _pushes(src_refs, land_refs, send_sems, recv_sems, slab_per_peer):
            cp.start()
        token[...] = jnp.zeros_like(token)

    hbm = pl.BlockSpec(memory_space=pltpu.HBM)
    sem = pl.BlockSpec(memory_space=pltpu.SEMAPHORE)
    out = pl.pallas_call(
        body, name=name,
        out_shape=(pltpu.SemaphoreType.DMA((7 * n,)), pltpu.SemaphoreType.DMA((7 * n,)),
                   *[pltpu.HBM(a.shape, a.dtype) for a in srcs + lands], jax.ShapeDtypeStruct((8, LANES), F32)),
        in_specs=[hbm] * (2 * n), out_specs=(sem, sem, *[hbm] * (2 * n), pl.BlockSpec(memory_space=pltpu.VMEM)),
        input_output_aliases={i: 2 + i for i in range(2 * n)},
        compiler_params=pltpu.CompilerParams(has_side_effects=pltpu.SideEffectType.DATAFLOW_SIDE_EFFECTING),
    )(*[pltpu.with_memory_space_constraint(a, pltpu.HBM) for a in srcs + lands])
    return out[0], out[1], list(out[2:2 + n]), list(out[2 + n:2 + 2 * n]), out[-1]


def _push_wait(send_sems, recv_sems, srcs, lands, after, *, name, slab_per_peer):
    n = len(srcs)

    def body(*refs):
        src_refs, land_refs = refs[:n], refs[n:2 * n]
        s_sems, r_sems = refs[2 * n], refs[2 * n + 1]
        for cp in _pushes(src_refs, land_refs, s_sems, r_sems, slab_per_peer):
            cp.wait_send()
            cp.wait_recv()

    hbm = pl.BlockSpec(memory_space=pltpu.HBM)
    sem = pl.BlockSpec(memory_space=pltpu.SEMAPHORE)
    out = pl.pallas_call(
        body, name=name, out_shape=tuple(pltpu.HBM(a.shape, a.dtype) for a in srcs + lands),
        in_specs=[hbm] * (2 * n) + [sem, sem, pl.BlockSpec(memory_space=pl.ANY)], out_specs=tuple([hbm] * (2 * n)),
        input_output_aliases={i: i for i in range(2 * n)},
        compiler_params=pltpu.CompilerParams(has_side_effects=pltpu.SideEffectType.DATAFLOW_SIDE_EFFECTING),
    )(*srcs, *lands, send_sems, recv_sems, after)
    return list(out[:n]), list(out[n:])


def _sum_parts(gp_ref, rows):
    g = gp_ref[0, 0:rows, :].astype(F32)
    for p in range(1, gp_ref.shape[0]):
        g = g + gp_ref[p, 0:rows, :].astype(F32)
    return g


def _adamw_update(w, m, v, g):
    m_new = ADAM_B1 * m + (1.0 - ADAM_B1) * g
    v_new = ADAM_B2 * v + (1.0 - ADAM_B2) * (g * g)
    m_hat = m_new / (1.0 - ADAM_B1 ** ADAM_STEP)
    v_hat = v_new / (1.0 - ADAM_B2 ** ADAM_STEP)
    return -ADAM_LR * (m_hat / (jnp.sqrt(v_hat) + ADAM_EPS) + ADAM_WD * w), m_new, v_new


def _adamw_many(ws, ms, vs, gparts, sums, *, name):
    n, k = len(ws), len(sums)

    def body(*refs):
        w_refs, m_refs, v_refs = refs[:n], refs[n:2 * n], refs[2 * n:3 * n]
        g_refs, s_refs, outs = refs[3 * n:4 * n], refs[4 * n:4 * n + k], refs[4 * n + k:]
        for a in range(n):
            g = _sum_parts(g_refs[a], w_refs[a].shape[0])
            d, m_new, v_new = _adamw_update(w_refs[a][...], m_refs[a][...], v_refs[a][...], g)
            for o_ref, val in zip(outs[4 * a:4 * a + 4], (g, d, m_new, v_new)):
                o_ref[...] = val
        for b in range(k):
            outs[4 * n + b][...] = _sum_parts(s_refs[b], s_refs[b].shape[1])

    out_shape = [jax.ShapeDtypeStruct(w.shape, F32) for w in ws for _ in range(4)]
    out_shape += [jax.ShapeDtypeStruct(s.shape[1:], F32) for s in sums]
    return pl.pallas_call(body, name=name, out_shape=out_shape, compiler_params=_params())(
        *ws, *ms, *vs, *gparts, *sums)


def _adamw(w, m, v, gparts, *, tr, name):
    R, C = w.shape
    n_parts = gparts.shape[0]

    def body(w_ref, m_ref, v_ref, gp_ref, g_ref, d_ref, nm_ref, nv_ref):
        g = _sum_parts(gp_ref, tr)
        d_ref[...], nm_ref[...], nv_ref[...] = _adamw_update(w_ref[...], m_ref[...], v_ref[...], g)
        g_ref[...] = g

    row = pl.BlockSpec((tr, C), lambda i: (i, 0))
    shp = jax.ShapeDtypeStruct((R, C), F32)
    return pl.pallas_call(
        body, name=name, grid=(R // tr,),
        in_specs=[row, row, row, pl.BlockSpec((n_parts, tr, C), lambda i: (0, i, 0))],
        out_specs=[row, row, row, row], out_shape=[shp, shp, shp, shp],
        compiler_params=_params(("parallel",)),
    )(w, m, v, gparts)


def _rope_tables(s):
    pos = jnp.arange(s, dtype=F32)
    inv_freq = 1.0 / (ROPE_THETA ** (jnp.arange(0, QK_ROPE, 2, dtype=F32) / QK_ROPE))
    ang = pos[:, None] * inv_freq[None, :]
    cos, sin = jnp.cos(ang), jnp.sin(ang)
    zero = jnp.zeros((s, LANES - QK_ROPE), F32)
    return jnp.concatenate([cos, cos, zero], -1), jnp.concatenate([sin, sin, zero], -1)


def _pick(n, want):
    t = min(n, want)
    assert n % t == 0
    return t


def _local_step(x, target, wts, small, hooks):
    S = x.shape[0]
    H = MLA_HEADS
    ts = _pick(S, 1024)
    tm = _pick(S, 512)
    tm_wide = _pick(S, 1024)
    tk_s = _pick(S, 4096)
    tt = _pick(S, 512)
    ta = _pick(S, 1024)
    ts_proj = _pick(S, 512)
    ts_merge = _pick(S, 512)
    row = lambda v: v.reshape(1, -1)
    w_in = wts["w_in"]
    w_main = jnp.concatenate([w_in[:, 0:2048], w_in[:, 2624:4672]], axis=1)
    w_ckv = jnp.concatenate([w_in[:, 2048:2624], jnp.zeros((D_MODEL, CKV_W - 576), BF16)], axis=1)
    w_uq3 = wts["w_uq"].reshape(Q_LORA, H, QK_NOPE + QK_ROPE)
    w_uq_p = jnp.concatenate(
        [w_uq3[:, :, :QK_NOPE].reshape(Q_LORA, H * QK_NOPE),
         jnp.pad(w_uq3[:, :, QK_NOPE:], ((0, 0), (0, 0), (0, LANES - QK_ROPE))).reshape(Q_LORA, H * LANES)], axis=1)
    w_ukv = wts["w_ukv"]
    cos, sin = _rope_tables(S)
    conv_w, conv_b = small["conv_w"], row(small["conv_b"])
    wa, wx = small["lru_wa"].astype(BF16), small["lru_wx"].astype(BF16)
    wat, wxt = jnp.swapaxes(wa, 1, 2), jnp.swapaxes(wx, 1, 2)
    ba, bx = small["lru_ba"].reshape(RNN_BLOCKS, 1, RNN_BLOCK_W), small["lru_bx"].reshape(RNN_BLOCKS, 1, RNN_BLOCK_W)
    lam = row(small["lru_lambda"])
    q_norm, kv_norm = row(small["q_norm"]), row(small["kv_norm"])
    norm_mix, norm_mlp, norm_final = row(small["norm_mix"]), row(small["norm_mlp"]), row(small["norm_final"])

    xn = _rmsnorm_cast(x, norm_mix, ts=ts, name="norm_mix")
    ident = lambda acc: (acc,)
    (z_rx,) = _mm(xn, w_main[:, :D_MODEL], name="z_rx", tm=tm_wide, tn=1024, tk=1024, outs=[("tile", F32)],
                  epilogue=ident)
    (z_gates,) = _mm(xn, w_main[:, D_MODEL:], name="z_gates", tm=tm_wide, tn=3 * D_MODEL, tk=1024, outs=[("tile", BF16)],
                     epilogue=ident)
    (z_ckv,) = _mm(xn, w_ckv, name="z_ckv", tm=tm, tn=CKV_W, tk=1024, outs=[("tile", F32)], epilogue=ident)
    h = _lru_fwd(z_rx, conv_w, conv_b, wa, ba, wx, bx, lam, tt=tt)
    q, k, v = _mla_proj(z_ckv, q_norm, kv_norm, w_uq_p, w_ukv, cos, sin, ts=ts_proj)
    o, lse = _attn_fwd(q, k, v, t=ta, hb=2)
    w_out, w_up, w_down = hooks["weights_later"](o)
    merged, h1, n2 = _merge_h1(h, z_gates, o, x, w_out, norm_mlp, ts=ts_merge)

    def ep_up(acc):
        r = jnp.maximum(acc, 0.0)
        return r * r, r

    act, relu = _mm(n2, w_up, name="up", tm=tm_wide, tn=2048, tk=1024, outs=[("tile", BF16), ("tile", BF16)],
                    epilogue=ep_up)

    def ep_loss(acc, h1v, tgt, g):
        h2 = acc + h1v
        y, _ = _rms_fwd(h2, g)
        err = y - tgt
        loss_rows = 0.5 * jnp.mean(err * err, axis=-1, keepdims=True)
        dy = err * (1.0 / D_MODEL)
        dh2, dg_rows = _rms_bwd(dy, h2, g)
        lsum = jnp.sum(loss_rows, axis=0, keepdims=True)
        return dh2, dh2, jnp.sum(dg_rows, axis=0, keepdims=True), jnp.broadcast_to(lsum, (1, D_MODEL))

    dh2, dh2b, dnf_p, loss_p = _mm(
        act, w_down, name="down_loss", tm=tm, tn=1024, tk=D_FF,
        outs=[("tile", F32), ("tile", BF16), ("rowpart", F32), ("rowpart", F32)], epilogue=ep_loss,
        extras=[("tile", h1), ("tile", target), ("row", norm_final)])
    loss_part = jnp.sum(loss_p[:, 0, 0])
    d_norm_final = jnp.sum(dnf_p, axis=(0, 1))

    def ep_du(acc, r):
        return (acc * (2.0 * r.astype(F32)),)

    (du,) = _mm(dh2b, w_down, name="d_act", tb=True, tm=tm_wide, tn=2048, tk=1024, outs=[("tile", BF16)], epilogue=ep_du,
                extras=[("tile", relu)])

    def ep_dh1(acc, h1v, dh2v, g):
        dv, dg_rows = _rms_bwd(acc, h1v, g)
        dh1 = dh2v + dv
        return dh1, dh1, jnp.sum(dg_rows, axis=0, keepdims=True)

    dh1, dh1b, dnm_p = _mm(du, w_up, name="d_n2", tb=True, tm=tm, tn=1024, tk=D_FF,
                           outs=[("tile", F32), ("tile", BF16), ("rowpart", F32)], epilogue=ep_dh1,
                           extras=[("tile", h1), ("tile", dh2), ("row", norm_mlp)])
    d_norm_mlp = jnp.sum(dnm_p, axis=(0, 1))
    tn_mm = functools.partial(_mm, ta=True, tk=tk_s, outs=[("tile", BF16)], epilogue=ident)
    (d_w_down,) = tn_mm(act, dh2b, name="dw_down", tm=1024, tn=1024)
    (p_w_up,) = _mm(n2, du, name="dw_up", ta=True, tk=tk_s, tm=1024, tn=D_FF // N_DEV, outs=[("colshard", BF16)],
                    epilogue=ident)
    (d_w_out,) = tn_mm(merged, dh1b, name="dw_out", tm=1024, tn=1024)
    early = [d_w_out.reshape(N_DEV, -1, D_MODEL), p_w_up, d_w_down.reshape(N_DEV, -1, D_MODEL)]
    w_out = w_out + hooks["send"]("early", early)[0, 0].astype(BF16)


    def ep_dmerge(dm, hv, rg, ga, gb, ov):
        rg, ga, gb = rg.astype(F32), ga.astype(F32), gb.astype(F32)
        gl, dgl = _gelu_and_grad(rg)
        sa, sb = _sigmoid(ga), _sigmoid(gb)
        ya = hv * gl
        dya = dm * sa
        do = dm * sb
        dga = dm * ya * sa * (1.0 - sa)
        dgb = dm * ov * sb * (1.0 - sb)
        dh = dya * gl
        drg = dya * hv * dgl
        dov = do * ov
        lane = lax.broadcasted_iota(jnp.int32, (dm.shape[0], LANES), 1)
        delta = jnp.zeros((dm.shape[0], LANES), F32)
        for hh in range(H):
            dsum = jnp.sum(dov[:, hh * V_HEAD:(hh + 1) * V_HEAD], axis=1, keepdims=True)
            delta = jnp.where(lane == hh, dsum, delta)
        return dh, jnp.concatenate([drg, dga, dgb], axis=1), do, delta

    dh_lru, dz_part, do, delta_w = _mm(
        dh1b, w_out, name="d_merge", tb=True, tm=ts_merge, tn=1024, tk=1024,
        outs=[("tile", F32), ("cols", BF16, 4 * D_MODEL, D_MODEL), ("tile", BF16), ("side", F32)],
        epilogue=ep_dmerge,
        extras=[("tile", h), ("tilecol", z_gates, 0), ("tilecol", z_gates, 1), ("tilecol", z_gates, 2), ("tile", o)])
    delta_row = delta_w[:, :H].T.reshape(H, 1, S)
    lse_row = lse

    dq, dk, dv = _attn_bwd(q, k, v, do, lse_row, delta_row, t=ta)
    dz_ckv, d_w_uq_p, d_w_ukv, d_q_norm, d_kv_norm = _mla_proj_bwd(
        z_ckv, dq, dk, dv, q_norm, kv_norm, w_uq_p.T, w_ukv.T, cos, sin, ts=ts_proj)
    d_w_uq = jnp.concatenate(
        [d_w_uq_p[:, :H * QK_NOPE].reshape(Q_LORA, H, QK_NOPE),
         d_w_uq_p[:, H * QK_NOPE:].reshape(Q_LORA, H, LANES)[:, :, :QK_ROPE]], axis=2).reshape(Q_LORA, -1)

    dz_main, d_wa, d_wx, d_ba, d_bx, d_lam, d_conv_w, d_conv_b = _lru_bwd(
        z_rx, h, dh_lru, dz_part, conv_w, conv_b, wa, wat, ba, wx, wxt, bx, lam, tt=tt)

    (d_w_main,) = tn_mm(xn, dz_main, name="dw_main", tm=1024, tn=1024)
    (d_w_ckv,) = tn_mm(xn, dz_ckv, name="dw_ckv", tm=1024, tn=CKV_W)
    d_w_in = jnp.concatenate([d_w_main[:, 0:2048], d_w_ckv[:, 0:576], d_w_main[:, 2048:4096]], axis=1)

    def col_parts(full):
        r = full.shape[0]
        return jnp.transpose(full.astype(BF16).reshape(r, N_DEV, -1), (1, 0, 2))

    late = [col_parts(d_w_in), col_parts(d_w_uq), col_parts(d_w_ukv)]
    norm_mix = norm_mix + hooks["send"]("late", late)[0, 0]

    def ep_dx(acc, xv, dh1v, g):
        dv, dg_rows = _rms_bwd(acc, xv, g)
        return dh1v + dv, jnp.sum(dg_rows, axis=0, keepdims=True)

    grad_x, dnx_p = _mm(dz_main, w_main, name="dx", tb=True, tm=tm, tn=1024, tk=4 * D_MODEL,
                        outs=[("tile", F32), ("rowpart", F32)], epilogue=ep_dx, more=(dz_ckv, w_ckv),
                        extras=[("tile", x), ("tile", dh1), ("row", norm_mix)])
    d_norm_mix = jnp.sum(dnx_p, axis=(0, 1))
    sm = {"norm_mix": d_norm_mix, "conv_w": d_conv_w, "conv_b": d_conv_b.reshape(-1), "lru_wa": d_wa,
          "lru_ba": d_ba.reshape(RNN_BLOCKS, RNN_BLOCK_W), "lru_wx": d_wx, "lru_bx": d_bx.reshape(RNN_BLOCKS, RNN_BLOCK_W),
          "lru_lambda": d_lam.reshape(-1), "q_norm": d_q_norm.reshape(-1), "kv_norm": d_kv_norm.reshape(-1),
          "norm_mlp": d_norm_mlp, "norm_final": d_norm_final}
    return loss_part, grad_x, sm


BIG = ("w_in", "w_uq", "w_ukv", "w_out", "w_up", "w_down")
SMALL = ("norm_mix", "conv_b", "lru_wa", "lru_ba", "lru_wx", "lru_bx", "lru_lambda", "q_norm", "kv_norm", "norm_mlp",
         "norm_final")
WEIGHTS = ("norm_mix", "w_in", "conv_w", "conv_b", "lru_wa", "lru_ba", "lru_wx", "lru_bx", "lru_lambda", "q_norm", "w_uq",
           "kv_norm", "w_ukv", "w_out", "norm_mlp", "w_up", "w_down", "norm_final")
ADAM_TILE_ROWS = {"w_in": 256, "w_uq": 128, "w_ukv": 128, "w_out": 64, "w_up": 256, "w_down": 128}
CONV_ROWS = N_DEV * 8


def _rows(a):
    return a.reshape(-1, LANES)


def _pad_rows(a, mult):
    r = a.shape[-2]
    pad = (-r) % mult
    if pad == 0:
        return a
    cfg = [(0, 0)] * (a.ndim - 2) + [(0, pad), (0, 0)]
    return jnp.pad(a, cfg)


def _cols_from_shards(g):
    return jnp.transpose(g, (1, 0, 2)).reshape(g.shape[1], -1)


def kernel(x, norm_mix, w_in, conv_w, conv_b, lru_wa, lru_ba, lru_wx, lru_bx, lru_lambda, q_norm, w_uq, kv_norm, w_ukv, w_out, norm_mlp, w_up, w_down, norm_final, loss_target, m_norm_mix, m_w_in, m_conv_w, m_conv_b, m_lru_wa, m_lru_ba, m_lru_wx, m_lru_bx, m_lru_lambda, m_q_norm, m_w_uq, m_kv_norm, m_w_ukv, m_w_out, m_norm_mlp, m_w_up, m_w_down, m_norm_final, v_norm_mix, v_w_in, v_conv_w, v_conv_b, v_lru_wa, v_lru_ba, v_lru_wx, v_lru_bx, v_lru_lambda, v_q_norm, v_w_uq, v_kv_norm, v_w_ukv, v_w_out, v_norm_mlp, v_w_up, v_w_down, v_norm_final):
    W = dict(norm_mix=norm_mix, w_in=w_in, conv_w=conv_w, conv_b=conv_b, lru_wa=lru_wa, lru_ba=lru_ba, lru_wx=lru_wx,
             lru_bx=lru_bx, lru_lambda=lru_lambda, q_norm=q_norm, w_uq=w_uq, kv_norm=kv_norm, w_ukv=w_ukv, w_out=w_out,
             norm_mlp=norm_mlp, w_up=w_up, w_down=w_down, norm_final=norm_final)
    M = dict(norm_mix=m_norm_mix, w_in=m_w_in, conv_w=m_conv_w, conv_b=m_conv_b, lru_wa=m_lru_wa, lru_ba=m_lru_ba,
             lru_wx=m_lru_wx, lru_bx=m_lru_bx, lru_lambda=m_lru_lambda, q_norm=m_q_norm, w_uq=m_w_uq, kv_norm=m_kv_norm,
             w_ukv=m_w_ukv, w_out=m_w_out, norm_mlp=m_norm_mlp, w_up=m_w_up, w_down=m_w_down, norm_final=m_norm_final)
    V = dict(norm_mix=v_norm_mix, w_in=v_w_in, conv_w=v_conv_w, conv_b=v_conv_b, lru_wa=v_lru_wa, lru_ba=v_lru_ba,
             lru_wx=v_lru_wx, lru_bx=v_lru_bx, lru_lambda=v_lru_lambda, q_norm=v_q_norm, w_uq=v_w_uq, kv_norm=v_kv_norm,
             w_ukv=v_w_ukv, w_out=v_w_out, norm_mlp=v_norm_mlp, w_up=v_w_up, w_down=v_w_down, norm_final=v_norm_final)
    me = 4 * lax.axis_index("x") + 2 * lax.axis_index("y") + lax.axis_index("c")

    first, later = ("w_in", "w_uq", "w_ukv"), ("w_out", "w_up", "w_down")
    got = _all_gather([W[n].astype(BF16) for n in first] + [_pad_rows(conv_w, 8)], name="gather_weights")
    wts = {"w_in": _cols_from_shards(got[0]), "w_uq": _cols_from_shards(got[1]), "w_ukv": _cols_from_shards(got[2])}
    w_send, w_recv, w_src, w_land, zeros = _push_start([W[n].astype(BF16) for n in later], name="gather_later_start",
                                                       slab_per_peer=False)
    small = {n: W[n] for n in SMALL}
    small["conv_w"] = _cols_from_shards(got[3][:, :CONV_WIDTH])
    small["norm_mix"] = norm_mix + zeros[0, 0]

    def with_own_slab(land, mine):
        return lax.dynamic_update_slice(land, mine, (me, 0, 0))

    def weights_later(after):
        srcs, lands = _push_wait(w_send, w_recv, w_src, w_land, after, name="gather_later_wait", slab_per_peer=False)
        w_out_g, w_up_g, w_down_g = [with_own_slab(l, s[None]) for l, s in zip(lands, srcs)]
        return w_out_g.reshape(-1, D_MODEL), _cols_from_shards(w_up_g), w_down_g.reshape(-1, D_MODEL)

    sent = {}
    G, Dl, NM, NV = {}, {}, {}, {}

    def finish(group, names, after):
        s_sems, r_sems, srcs, lands, _ = sent[group]
        srcs, lands = _push_wait(s_sems, r_sems, srcs, lands, after, name="exchange_" + group + "_wait",
                                 slab_per_peer=True)
        for n, src, land in zip(names, srcs, lands):
            parts = with_own_slab(land, lax.dynamic_slice(src, (me, 0, 0), (1, *src.shape[1:])))
            G[n], Dl[n], NM[n], NV[n] = _adamw(W[n], M[n], V[n], parts, tr=ADAM_TILE_ROWS[n], name="adamw_" + n)

    def send(group, parts):
        sent[group] = _push_start(parts, name="exchange_" + group + "_start", slab_per_peer=True)
        zeros = sent[group][4]
        if group == "late":
            finish("early", later, zeros)
            zeros = zeros + 0.0 * (Dl["w_out"][0:8, 0:LANES] + Dl["w_up"][0:8, 0:LANES] + Dl["w_down"][0:8, 0:LANES])
        return zeros

    loss_part, grad_x, g_small = _local_step(x[0], loss_target[0], wts, small,
                                              {"weights_later": weights_later, "send": send})
    finish("late", first, grad_x)

    conv_rows = _pad_rows(jnp.transpose(g_small["conv_w"].reshape(CONV_WIDTH, N_DEV, LANES), (1, 0, 2)), 8)
    loss_rows = jnp.zeros((8, LANES), F32).at[0, 0].set(loss_part)
    as_sent = lambda n: g_small[n].astype(BF16) if n in ("lru_wa", "lru_wx") else g_small[n]
    gathered = _all_gather([_pad_rows(_rows(as_sent(n)), 8) for n in SMALL]
                           + [conv_rows.reshape(CONV_ROWS, LANES), loss_rows], name="gather_small")
    k = len(SMALL)
    outs = _adamw_many([_rows(W[n]) for n in SMALL], [_rows(M[n]) for n in SMALL], [_rows(V[n]) for n in SMALL],
                       gathered[:k], gathered[k:], name="adamw_small")
    for j, n in enumerate(SMALL):
        for out, o in zip((G, Dl, NM, NV), outs[4 * j:4 * j + 4]):
            out[n] = o.reshape(W[n].shape)
    conv_sum, loss_sum = outs[4 * k:]
    loss = loss_sum[0, 0]

    g_conv = lax.dynamic_slice(conv_sum, (me * 8, 0), (8, LANES))
    conv_out = _adamw(_pad_rows(conv_w, 8), _pad_rows(m_conv_w, 8), _pad_rows(v_conv_w, 8), g_conv[None], tr=8,
                      name="adamw_conv_w")
    for out, pk in zip((G, Dl, NM, NV), conv_out):
        out["conv_w"] = pk[:CONV_WIDTH]
    return (loss, grad_x[None], *[G[n] for n in WEIGHTS], *[Dl[n] for n in WEIGHTS], *[NM[n] for n in WEIGHTS],
            *[NV[n] for n in WEIGHTS])
```

```python
import functools

import numpy as np
import jax
import jax.numpy as jnp
from jax import lax
from jax.experimental import pallas as pl
from jax.experimental.pallas import tpu as pltpu

F32 = jnp.float32
BF16 = jnp.bfloat16
MESH = pl.DeviceIdType.MESH

D_MODEL = 1024
N_DEV = 8
LANES = 128
RNN_BLOCKS = 8
RNN_BLOCK_W = 128
CONV_WIDTH = 4
LRU_C = 8.0
MLA_HEADS = 8
QK_NOPE = 128
QK_ROPE = 64
V_HEAD = 128
QK_PAD = 256
Q_LORA = 256
KV_LORA = 256
CKV_W = 640
ROPE_THETA = 10000.0
D_FF = 4096
EPS = 1e-6
ATTN_SCALE = (QK_NOPE + QK_ROPE) ** -0.5
LOG2E = 1.4426950408889634
LN2 = 0.6931471805599453
NEG = -1e30

ADAM_LR = 0.001
ADAM_B1 = 0.9
ADAM_B2 = 0.999
ADAM_EPS = 1e-08
ADAM_WD = 0.01
ADAM_STEP = 10

VMEM_LIMIT = 56 * 1024 * 1024


def _params(sem=None):
    return pltpu.CompilerParams(dimension_semantics=sem, vmem_limit_bytes=VMEM_LIMIT)


def _sigmoid(v):
    return 1.0 / (1.0 + jnp.exp(-v))


def _softplus(y):
    e = jnp.exp(-jnp.abs(y))
    u = 1.0 + e
    d = u - 1.0
    l1p = jnp.where(d == 0.0, e, jnp.log(u) * e / jnp.where(d == 0.0, 1.0, d))
    return jnp.maximum(y, 0.0) + l1p


_GELU_K = 0.7978845608028654
_GELU_C = 0.044715


def _gelu_and_grad(v):
    t = jnp.tanh(_GELU_K * (v + _GELU_C * v * v * v))
    g = 0.5 * v * (1.0 + t)
    dg = 0.5 * (1.0 + t) + 0.5 * v * (1.0 - t * t) * _GELU_K * (1.0 + 3.0 * _GELU_C * v * v)
    return g, dg


def _rms_fwd(v, g):
    rstd = lax.rsqrt(jnp.mean(v * v, axis=-1, keepdims=True) + EPS)
    return v * rstd * g, rstd


def _rms_bwd(dy, v, g):
    rstd = lax.rsqrt(jnp.mean(v * v, axis=-1, keepdims=True) + EPS)
    vh = v * rstd
    dvh = dy * g
    dv = rstd * (dvh - vh * jnp.mean(dvh * vh, axis=-1, keepdims=True))
    return dv, dy * vh


def _shift_down(v, s, fill, row):
    return jnp.where(row >= s, pltpu.roll(v, s, 0), fill)


def _shift_up(v, s, fill, row, n):
    return jnp.where(row < n - s, pltpu.roll(v, n - s, 0), fill)


def _rot_half(v, lane):
    n = v.shape[-1]
    l = lane & (LANES - 1)
    up = pltpu.roll(v, n - QK_ROPE // 2, 1)
    dn = pltpu.roll(v, QK_ROPE // 2, 1)
    return jnp.where(l < QK_ROPE // 2, -up, jnp.where(l < QK_ROPE, dn, 0.0))


def _mm(a, b, *, name, tm, tn, tk, outs, epilogue, extras=(), ta=False, tb=False, more=None):
    assert not (ta and tb)
    if ta:
        K, M = a.shape
    else:
        M, K = a.shape
    if tb:
        N, K2 = b.shape
    else:
        K2, N = b.shape
    assert K == K2 and M % tm == 0 and N % tn == 0 and K % tk == 0, (name, a.shape, b.shape)
    n_i, n_j, n_k = M // tm, N // tn, K // tk
    n_ex, n_out = len(extras), len(outs)
    n_more = 0 if more is None else 2
    assert more is None or (n_k == 1 and not ta)

    def body(*refs):
        a_ref, b_ref = refs[0], refs[1]
        ex_refs = refs[2 + n_more:2 + n_more + n_ex]
        out_refs = refs[2 + n_more + n_ex:2 + n_more + n_ex + n_out]
        if ta:
            part = lax.dot_general(a_ref[...], b_ref[...], (((0,), (0,)), ((), ())), preferred_element_type=F32)
        elif tb:
            part = lax.dot_general(a_ref[...], b_ref[...], (((1,), (1,)), ((), ())), preferred_element_type=F32)
        else:
            part = jnp.dot(a_ref[...], b_ref[...], preferred_element_type=F32)
        if more is not None:
            part = part + lax.dot_general(refs[2][...], refs[3][...], (((1,), (1,)), ((), ())),
                                          preferred_element_type=F32)

        def finish(acc):
            res = epilogue(acc, *[r[...] for r in ex_refs])
            for o_ref, r, spec in zip(out_refs, res, outs):
                if spec[0] == "cols":
                    o_ref[:, spec[3]:spec[3] + r.shape[1]] = r.astype(o_ref.dtype)
                else:
                    o_ref[...] = r.astype(o_ref.dtype).reshape(o_ref.shape)

        if n_k == 1:
            finish(part)
        else:
            acc_ref = refs[-1]
            k = pl.program_id(2)

            @pl.when(k == 0)
            def _():
                acc_ref[...] = part

            @pl.when(k > 0)
            def _():
                acc_ref[...] += part

            @pl.when(k == n_k - 1)
            def _():
                finish(acc_ref[...])

    a_spec = pl.BlockSpec((tk, tm), lambda j, i, k: (k, i)) if ta else pl.BlockSpec((tm, tk), lambda j, i, k: (i, k))
    b_once = dict(pipeline_mode=pl.Buffered(1)) if (n_j == 1 and n_k == 1) else {}
    if tb:
        in_specs = [a_spec, pl.BlockSpec((tn, tk), lambda j, i, k: (j, k), **b_once)]
    else:
        in_specs = [a_spec, pl.BlockSpec((tk, tn), lambda j, i, k: (k, j), **b_once)]
    if more is not None:
        k2 = more[0].shape[1]
        in_specs += [pl.BlockSpec((tm, k2), lambda j, i, k: (i, 0)), pl.BlockSpec((tn, k2), lambda j, i, k: (j, 0), **b_once)]
    for ex in extras:
        kind = ex[0]
        if kind == "tile":
            in_specs.append(pl.BlockSpec((tm, tn), lambda j, i, k: (i, j)))
        elif kind == "tilecol":
            assert n_j == 1
            in_specs.append(pl.BlockSpec((tm, tn), functools.partial(lambda c, j, i, k: (i, c), ex[2])))
        else:
            in_specs.append(pl.BlockSpec((1, tn), lambda j, i, k: (0, j)))
    out_specs, out_shape = [], []
    for kind, dt, *rest in outs:
        if kind == "tile":
            out_specs.append(pl.BlockSpec((tm, tn), lambda j, i, k: (i, j)))
            out_shape.append(jax.ShapeDtypeStruct((M, N), dt))
        elif kind == "colshard":
            out_specs.append(pl.BlockSpec((1, tm, tn), lambda j, i, k: (j, i, 0)))
            out_shape.append(jax.ShapeDtypeStruct((n_j, M, tn), dt))
        elif kind == "cols":
            assert n_j == 1
            out_specs.append(pl.BlockSpec((tm, rest[0]), lambda j, i, k: (i, 0)))
            out_shape.append(jax.ShapeDtypeStruct((M, rest[0]), dt))
        elif kind == "side":
            assert n_j == 1
            out_specs.append(pl.BlockSpec((tm, LANES), lambda j, i, k: (i, 0)))
            out_shape.append(jax.ShapeDtypeStruct((M, LANES), dt))
        else:
            out_specs.append(pl.BlockSpec((1, 1, tn), lambda j, i, k: (i, 0, j)))
            out_shape.append(jax.ShapeDtypeStruct((n_i, 1, N), dt))
    scratch = [pltpu.VMEM((tm, tn), F32)] if n_k > 1 else []
    return pl.pallas_call(
        body, name=name, grid=(n_j, n_i, n_k), in_specs=in_specs, out_specs=out_specs, out_shape=out_shape,
        scratch_shapes=scratch, compiler_params=_params(("parallel", "parallel", "arbitrary")),
    )(a, b, *(more or ()), *[ex[1] for ex in extras])


def _rmsnorm_cast(x, g, *, ts, name):
    S, D = x.shape

    def body(x_ref, g_ref, o_ref):
        y, _ = _rms_fwd(x_ref[...], g_ref[...])
        o_ref[...] = y.astype(BF16)

    return pl.pallas_call(
        body, name=name, grid=(S // ts,),
        in_specs=[pl.BlockSpec((ts, D), lambda i: (i, 0)), pl.BlockSpec((1, D), lambda i: (0, 0))],
        out_specs=pl.BlockSpec((ts, D), lambda i: (i, 0)), out_shape=jax.ShapeDtypeStruct((S, D), BF16),
        compiler_params=_params(("parallel",)),
    )(x, g)


LRU_NB = 4


def _lru_gates(xa, wa_ref, ba_ref, wx_ref, bx_ref, lam):
    xab = xa.astype(BF16)
    W = RNN_BLOCK_W
    rs, is_ = [], []
    for j in range(LRU_NB):
        xj = xab[:, j * W:(j + 1) * W]
        rs.append(_sigmoid(jnp.dot(xj, wa_ref[j], preferred_element_type=F32) + ba_ref[j]))
        is_.append(_sigmoid(jnp.dot(xj, wx_ref[j], preferred_element_type=F32) + bx_ref[j]))
    r = jnp.concatenate(rs, axis=1)
    i = jnp.concatenate(is_, axis=1)
    sp = _softplus(-lam)
    log_a = (-LRU_C * r) * sp
    a = jnp.exp(log_a)
    y = 2.0 * log_a
    one_m = jnp.where(y > -0.01, -y * (1.0 + 0.5 * y * (1.0 + y * (1.0 / 3.0))), 1.0 - a * a)
    return r, i, sp, a, jnp.sqrt(one_m)


def _rows_before(x, tail8, k):
    e16 = jnp.concatenate([tail8, x[0:8, :]], axis=0)
    return jnp.concatenate([pltpu.roll(e16, k, 0)[8:16, :], pltpu.roll(x, k, 0)[8:, :]], axis=0)


def _rows_after(x, head8, k):
    tt = x.shape[0]
    e16 = jnp.concatenate([x[tt - 8:tt, :], head8], axis=0)
    return jnp.concatenate([pltpu.roll(x, tt - k, 0)[:tt - 8, :], pltpu.roll(e16, 16 - k, 0)[0:8, :]], axis=0)


def _scan_down(a, b, h0, a_s, b_s, c_s):
    tt, C = a.shape
    G, nch = tt // 8, C // LANES
    rin = lax.broadcasted_iota(jnp.int32, (tt, C), 0) & 7

    def in_group(v, s):
        return pltpu.roll(v.reshape(G, 8, C), s, 1).reshape(tt, C)

    A, B = a, b
    for s in (1, 2, 4):
        B = A * jnp.where(rin >= s, in_group(B, s), 0.0) + B
        A = A * jnp.where(rin >= s, in_group(A, s), 1.0)
    for j in range(nch):
        a_s[j] = A[:, j * LANES:(j + 1) * LANES]
        b_s[j] = B[:, j * LANES:(j + 1) * LANES]
    At = jnp.concatenate([a_s.at[j][pl.ds(7, G, stride=8), :] for j in range(nch)], axis=1)
    Bt = jnp.concatenate([b_s.at[j][pl.ds(7, G, stride=8), :] for j in range(nch)], axis=1)
    rowg = lax.broadcasted_iota(jnp.int32, (G, C), 0)
    s = 1
    while s < G:
        Bt = At * _shift_down(Bt, s, 0.0, rowg) + Bt
        At = At * _shift_down(At, s, 1.0, rowg)
        s *= 2
    hg = At * h0 + Bt
    cin = _shift_down(hg, 1, h0, rowg)
    for j in range(nch):
        for r in range(8):
            c_s.at[j][pl.ds(r, G, stride=8), :] = cin[:, j * LANES:(j + 1) * LANES]
    return A * jnp.concatenate([c_s[j] for j in range(nch)], axis=1) + B, hg[G - 1:G, :]


def _scan_up(c, g_in, g_next, a_s, b_s, c_s):
    tt, C = c.shape
    G, nch = tt // 8, C // LANES
    rin = lax.broadcasted_iota(jnp.int32, (tt, C), 0) & 7

    def in_group(v, s):
        return pltpu.roll(v.reshape(G, 8, C), 8 - s, 1).reshape(tt, C)

    Cc, Gv = c, g_in
    for s in (1, 2, 4):
        Gv = Gv + Cc * jnp.where(rin < 8 - s, in_group(Gv, s), 0.0)
        Cc = Cc * jnp.where(rin < 8 - s, in_group(Cc, s), 1.0)
    for j in range(nch):
        a_s[j] = Cc[:, j * LANES:(j + 1) * LANES]
        b_s[j] = Gv[:, j * LANES:(j + 1) * LANES]
    Ct = jnp.concatenate([a_s.at[j][pl.ds(0, G, stride=8), :] for j in range(nch)], axis=1)
    Gt = jnp.concatenate([b_s.at[j][pl.ds(0, G, stride=8), :] for j in range(nch)], axis=1)
    rowg = lax.broadcasted_iota(jnp.int32, (G, C), 0)
    s = 1
    while s < G:
        Gt = Gt + Ct * _shift_up(Gt, s, 0.0, rowg, G)
        Ct = Ct * _shift_up(Ct, s, 1.0, rowg, G)
        s *= 2
    gg = Gt + Ct * g_next
    cin = _shift_up(gg, 1, g_next, rowg, G)
    for j in range(nch):
        for r in range(8):
            c_s.at[j][pl.ds(r, G, stride=8), :] = cin[:, j * LANES:(j + 1) * LANES]
    return Gv + Cc * jnp.concatenate([c_s[j] for j in range(nch)], axis=1), gg[0:1, :]


def _lru_fwd(z_rx, conv_w, conv_b, wa, ba, wx, bx, lam, *, tt):
    S = z_rx.shape[0]
    n_t = S // tt
    BW = RNN_BLOCK_W
    W = LRU_NB * BW

    def body(x_ref, cw_ref, cb_ref, wa_ref, ba_ref, wx_ref, bx_ref, lam_ref, h_ref, tail, hc, a_s, b_s, c_s):
        t = pl.program_id(1)

        @pl.when(t == 0)
        def _():
            tail[...] = jnp.zeros((8, W), F32)
            hc[...] = jnp.zeros((8, W), F32)

        x = x_ref[...]
        before = tail[...]
        cw = cw_ref[...]
        xa = (cb_ref[...] + cw[3:4] * x + cw[2:3] * _rows_before(x, before, 1) + cw[1:2] * _rows_before(x, before, 2)
              + cw[0:1] * _rows_before(x, before, 3))
        tail[...] = x[tt - 8:tt, :]
        _r, i, _sp, a, mult = _lru_gates(xa, wa_ref, ba_ref, wx_ref, bx_ref, lam_ref[...])
        h, h_last = _scan_down(a, mult * (i * xa), hc[0:1, :], a_s, b_s, c_s)
        h_ref[...] = h.astype(BF16)
        hc[...] = jnp.broadcast_to(h_last, (8, W))

    blk = lambda n, t: (t, n)
    vec = pl.BlockSpec((1, W), lambda n, t: (0, n))
    mat = pl.BlockSpec((LRU_NB, BW, BW), lambda n, t: (n, 0, 0))
    bias = pl.BlockSpec((LRU_NB, 1, BW), lambda n, t: (n, 0, 0))
    row8 = pltpu.VMEM((8, W), F32)
    wide = pltpu.VMEM((LRU_NB, tt, LANES), F32)
    return pl.pallas_call(
        body, name="lru_fwd", grid=(RNN_BLOCKS // LRU_NB, n_t),
        in_specs=[pl.BlockSpec((tt, W), blk), pl.BlockSpec((CONV_WIDTH, W), lambda n, t: (0, n)), vec, mat, bias, mat,
                  bias, vec],
        out_specs=pl.BlockSpec((tt, W), blk), out_shape=jax.ShapeDtypeStruct((S, D_MODEL), BF16),
        scratch_shapes=[row8, row8, wide, wide, wide],
        compiler_params=_params(("parallel", "arbitrary")),
    )(z_rx, conv_w, conv_b, wa, ba, wx, bx, lam)


def _lru_bwd(z_rx, h, dh, dz, conv_w, conv_b, wa, wat, ba, wx, wxt, bx, lam, *, tt):
    S = z_rx.shape[0]
    n_t = S // tt
    BW = RNN_BLOCK_W
    W = LRU_NB * BW
    t8 = tt // 8

    def body(x_ref, xp_ref, h_ref, hp_ref, dh_ref, _dz_ref, cw_ref, cb_ref, wa_ref, wat_ref, ba_ref, wx_ref, wxt_ref,
             bx_ref, lam_ref, dx_ref, dwa_ref, dwx_ref, dba_ref, dbx_ref, dlam_ref, dcw_ref, dcb_ref, nxt, a_c, g_c, a_s,
             b_s, c_s):
        t = pl.program_id(1)
        tile = n_t - 1 - t

        @pl.when(t == 0)
        def _():
            a_c[...] = jnp.zeros((8, W), F32)
            g_c[...] = jnp.zeros((8, W), F32)
            nxt[...] = jnp.zeros((8, W), F32)
            dwa_ref[...] = jnp.zeros_like(dwa_ref)
            dwx_ref[...] = jnp.zeros_like(dwx_ref)
            dba_ref[...] = jnp.zeros_like(dba_ref)
            dbx_ref[...] = jnp.zeros_like(dbx_ref)
            dlam_ref[...] = jnp.zeros_like(dlam_ref)
            dcw_ref[...] = jnp.zeros_like(dcw_ref)
            dcb_ref[...] = jnp.zeros_like(dcb_ref)

        has_prev = (tile > 0).astype(F32)
        x = x_ref[...]
        before = xp_ref[...] * has_prev
        xm1, xm2, xm3 = _rows_before(x, before, 1), _rows_before(x, before, 2), _rows_before(x, before, 3)
        cw = cw_ref[...]
        xa = cb_ref[...] + cw[3:4] * x + cw[2:3] * xm1 + cw[1:2] * xm2 + cw[0:1] * xm3
        lam = lam_ref[...]
        r, i, sp, a, mult = _lru_gates(xa, wa_ref, ba_ref, wx_ref, bx_ref, lam)
        gated = i * xa
        h_prev = _rows_before(h_ref[...].astype(F32), hp_ref[8:16, :].astype(F32) * has_prev, 1)
        g, g_first = _scan_up(_rows_after(a, a_c[...], 1), dh_ref[...], g_c[0:1, :], a_s, b_s, c_s)
        a_c[...] = jnp.broadcast_to(a[0:1, :], (8, W))
        g_c[...] = jnp.broadcast_to(g_first, (8, W))
        dlog_a = g * h_prev * a - g * gated * (a * a) / mult
        dgated = g * mult
        di = dgated * xa
        dxa = dgated * i
        dr = dlog_a * (-LRU_C * sp)
        dlam_ref[...] += jnp.sum(dlog_a * (-LRU_C * r), axis=0, keepdims=True) * (-_sigmoid(-lam))
        dpr = dr * r * (1.0 - r)
        dpi = di * i * (1.0 - i)
        xab, dprb, dpib = xa.astype(BF16), dpr.astype(BF16), dpi.astype(BF16)
        tn_dims = (((0,), (0,)), ((), ()))
        back = []
        for j in range(LRU_NB):
            sl = slice(j * BW, (j + 1) * BW)
            dwa_ref[j] += lax.dot_general(xab[:, sl], dprb[:, sl], tn_dims, preferred_element_type=F32)
            dwx_ref[j] += lax.dot_general(xab[:, sl], dpib[:, sl], tn_dims, preferred_element_type=F32)
            dba_ref[j] += jnp.sum(dpr[:, sl], axis=0, keepdims=True)
            dbx_ref[j] += jnp.sum(dpi[:, sl], axis=0, keepdims=True)
            back.append(jnp.dot(dprb[:, sl], wat_ref[j], preferred_element_type=F32)
                        + jnp.dot(dpib[:, sl], wxt_ref[j], preferred_element_type=F32))
        dxa = dxa + jnp.concatenate(back, axis=1)
        after = nxt[...]
        dx = (cw[3:4] * dxa + cw[2:3] * _rows_after(dxa, after, 1) + cw[1:2] * _rows_after(dxa, after, 2)
              + cw[0:1] * _rows_after(dxa, after, 3))
        nxt[...] = dxa[0:8, :]
        dx_ref[...] = dx.astype(BF16)
        dcw_ref[3:4, :] += jnp.sum(dxa * x, axis=0, keepdims=True)
        dcw_ref[2:3, :] += jnp.sum(dxa * xm1, axis=0, keepdims=True)
        dcw_ref[1:2, :] += jnp.sum(dxa * xm2, axis=0, keepdims=True)
        dcw_ref[0:1, :] += jnp.sum(dxa * xm3, axis=0, keepdims=True)
        dcb_ref[...] += jnp.sum(dxa, axis=0, keepdims=True)

    blk = lambda n, t: (n_t - 1 - t, n)
    prev = lambda n, t: (jnp.maximum((n_t - 1 - t) * t8 - 1, 0), n)
    vec = pl.BlockSpec((1, W), lambda n, t: (0, n))
    mat = pl.BlockSpec((LRU_NB, BW, BW), lambda n, t: (n, 0, 0))
    bias = pl.BlockSpec((LRU_NB, 1, BW), lambda n, t: (n, 0, 0))
    cws = pl.BlockSpec((CONV_WIDTH, W), lambda n, t: (0, n))
    tile = pl.BlockSpec((tt, W), blk)
    prev8 = pl.BlockSpec((8, W), prev)
    prev16 = pl.BlockSpec((16, W), lambda n, t: (jnp.maximum((n_t - 1 - t) * (tt // 16) - 1, 0), n))
    row8 = pltpu.VMEM((8, W), F32)
    wide = pltpu.VMEM((LRU_NB, tt, LANES), F32)
    return pl.pallas_call(
        body, name="lru_bwd", grid=(RNN_BLOCKS // LRU_NB, n_t),
        in_specs=[tile, prev8, tile, prev16, tile, pl.BlockSpec(memory_space=pl.ANY), cws, vec, mat, mat, bias, mat, mat,
                  bias, vec],
        out_specs=[tile, mat, mat, bias, bias, vec, cws, vec], input_output_aliases={5: 0},
        out_shape=[jax.ShapeDtypeStruct(dz.shape, BF16),
                   jax.ShapeDtypeStruct((RNN_BLOCKS, BW, BW), F32), jax.ShapeDtypeStruct((RNN_BLOCKS, BW, BW), F32),
                   jax.ShapeDtypeStruct((RNN_BLOCKS, 1, BW), F32), jax.ShapeDtypeStruct((RNN_BLOCKS, 1, BW), F32),
                   jax.ShapeDtypeStruct((1, D_MODEL), F32),
                   jax.ShapeDtypeStruct((CONV_WIDTH, D_MODEL), F32), jax.ShapeDtypeStruct((1, D_MODEL), F32)],
        scratch_shapes=[row8, row8, row8, wide, wide, wide],
        compiler_params=_params(("parallel", "arbitrary")),
    )(z_rx, z_rx, h, h, dh, dz, conv_w, conv_b, wa, wat, ba, wx, wxt, bx, lam)


def _mla_proj(z_ckv, q_norm, kv_norm, w_uq, w_ukv, cos, sin, *, ts):
    S = z_ckv.shape[0]
    H = MLA_HEADS

    def body(c_ref, qn_ref, kn_ref, wq_ref, wkv_ref, cos_ref, sin_ref, q_ref, k_ref, v_ref):
        c = c_ref[...]
        cqn, _ = _rms_fwd(c[:, 0:Q_LORA], qn_ref[...])
        ckn, _ = _rms_fwd(c[:, Q_LORA:Q_LORA + KV_LORA], kn_ref[...])
        q = jnp.dot(cqn.astype(BF16), wq_ref[...], preferred_element_type=F32) * (ATTN_SCALE * LOG2E)
        kv = jnp.dot(ckn.astype(BF16), wkv_ref[...], preferred_element_type=F32)
        cos1, sin1 = cos_ref[...], sin_ref[...]
        cos8 = jnp.concatenate([cos1] * H, axis=1)
        sin8 = jnp.concatenate([sin1] * H, axis=1)
        qr = q[:, H * QK_NOPE:]
        lane8 = lax.broadcasted_iota(jnp.int32, qr.shape, 1)
        qr = qr * cos8 + _rot_half(qr, lane8) * sin8
        kr = c[:, Q_LORA + KV_LORA:]
        lane1 = lax.broadcasted_iota(jnp.int32, kr.shape, 1)
        kr = (kr * cos1 + _rot_half(kr, lane1) * sin1).astype(BF16)
        for h in range(H):
            q_ref[h, :, 0:QK_NOPE] = q[:, h * QK_NOPE:(h + 1) * QK_NOPE].astype(BF16)
            q_ref[h, :, QK_NOPE:] = qr[:, h * LANES:(h + 1) * LANES].astype(BF16)
            k_ref[h, :, 0:QK_NOPE] = kv[:, h * 2 * LANES:h * 2 * LANES + LANES].astype(BF16)
            k_ref[h, :, QK_NOPE:] = kr
            v_ref[h] = kv[:, h * 2 * LANES + LANES:(h + 1) * 2 * LANES].astype(BF16)

    full = lambda shape: pl.BlockSpec(shape, lambda i: (0,) * len(shape))
    return pl.pallas_call(
        body, name="mla_proj", grid=(S // ts,),
        in_specs=[pl.BlockSpec((ts, CKV_W), lambda i: (i, 0)), full((1, Q_LORA)), full((1, KV_LORA)),
                  full(w_uq.shape), full(w_ukv.shape), pl.BlockSpec((ts, LANES), lambda i: (i, 0)),
                  pl.BlockSpec((ts, LANES), lambda i: (i, 0))],
        out_specs=[pl.BlockSpec((H, ts, QK_PAD), lambda i: (0, i, 0)), pl.BlockSpec((H, ts, QK_PAD), lambda i: (0, i, 0)),
                   pl.BlockSpec((H, ts, V_HEAD), lambda i: (0, i, 0))],
        out_shape=[jax.ShapeDtypeStruct((H, S, QK_PAD), BF16), jax.ShapeDtypeStruct((H, S, QK_PAD), BF16),
                   jax.ShapeDtypeStruct((H, S, V_HEAD), BF16)],
        compiler_params=_params(("parallel",)),
    )(z_ckv, q_norm, kv_norm, w_uq, w_ukv, cos, sin)


def _mla_proj_bwd(z_ckv, dq, dk, dv, q_norm, kv_norm, w_uqt, w_ukvt, cos, sin, *, ts):
    S = z_ckv.shape[0]
    H = MLA_HEADS

    def body(c_ref, dq_ref, dk_ref, dv_ref, qn_ref, kn_ref, wqt_ref, wkvt_ref, cos_ref, sin_ref,
             dz_ref, dwq_ref, dwkv_ref, dqn_ref, dkn_ref):
        @pl.when(pl.program_id(0) == 0)
        def _():
            dwq_ref[...] = jnp.zeros_like(dwq_ref)
            dwkv_ref[...] = jnp.zeros_like(dwkv_ref)
            dqn_ref[...] = jnp.zeros_like(dqn_ref)
            dkn_ref[...] = jnp.zeros_like(dkn_ref)

        c = c_ref[...]
        cq, ck = c[:, 0:Q_LORA], c[:, Q_LORA:Q_LORA + KV_LORA]
        qn, kn = qn_ref[...], kn_ref[...]
        cqn, _ = _rms_fwd(cq, qn)
        ckn, _ = _rms_fwd(ck, kn)
        cos1, sin1 = cos_ref[...], sin_ref[...]
        lane1 = lax.broadcasted_iota(jnp.int32, cos1.shape, 1)

        def unrope(g):
            return g * cos1 - _rot_half(g * sin1, lane1)

        dq_all = jnp.concatenate([dq_ref[h, :, 0:QK_NOPE] for h in range(H)]
                                 + [unrope(dq_ref[h, :, QK_NOPE:]) for h in range(H)], axis=1)
        dq_all = (dq_all * ATTN_SCALE).astype(BF16)
        dkv_all = jnp.concatenate([p for h in range(H) for p in (dk_ref[h, :, 0:QK_NOPE], dv_ref[h])],
                                  axis=1).astype(BF16)
        dkr = dk_ref[0, :, QK_NOPE:].astype(F32)
        for h in range(1, H):
            dkr = dkr + dk_ref[h, :, QK_NOPE:].astype(F32)
        dkr = unrope(dkr)
        tn_dims = (((0,), (0,)), ((), ()))
        dwq_ref[...] += lax.dot_general(cqn.astype(BF16), dq_all, tn_dims, preferred_element_type=F32)
        dwkv_ref[...] += lax.dot_general(ckn.astype(BF16), dkv_all, tn_dims, preferred_element_type=F32)
        dcqn = jnp.dot(dq_all, wqt_ref[...], preferred_element_type=F32)
        dckn = jnp.dot(dkv_all, wkvt_ref[...], preferred_element_type=F32)
        dcq, dqn_rows = _rms_bwd(dcqn, cq, qn)
        dck, dkn_rows = _rms_bwd(dckn, ck, kn)
        dqn_ref[...] += jnp.sum(dqn_rows, axis=0, keepdims=True)
        dkn_ref[...] += jnp.sum(dkn_rows, axis=0, keepdims=True)
        dz_ref[:, 0:Q_LORA] = dcq.astype(BF16)
        dz_ref[:, Q_LORA:Q_LORA + KV_LORA] = dck.astype(BF16)
        dz_ref[:, Q_LORA + KV_LORA:] = dkr.astype(BF16)

    full = lambda shape: pl.BlockSpec(shape, lambda i: (0,) * len(shape))
    return pl.pallas_call(
        body, name="mla_proj_bwd", grid=(S // ts,),
        in_specs=[pl.BlockSpec((ts, CKV_W), lambda i: (i, 0)), pl.BlockSpec((H, ts, QK_PAD), lambda i: (0, i, 0)),
                  pl.BlockSpec((H, ts, QK_PAD), lambda i: (0, i, 0)), pl.BlockSpec((H, ts, V_HEAD), lambda i: (0, i, 0)),
                  full((1, Q_LORA)), full((1, KV_LORA)), full(w_uqt.shape), full(w_ukvt.shape),
                  pl.BlockSpec((ts, LANES), lambda i: (i, 0)), pl.BlockSpec((ts, LANES), lambda i: (i, 0))],
        out_specs=[pl.BlockSpec((ts, CKV_W), lambda i: (i, 0)), full((Q_LORA, w_uqt.shape[0])),
                   full((KV_LORA, w_ukvt.shape[0])), full((1, Q_LORA)), full((1, KV_LORA))],
        out_shape=[jax.ShapeDtypeStruct((S, CKV_W), BF16), jax.ShapeDtypeStruct((Q_LORA, w_uqt.shape[0]), F32),
                   jax.ShapeDtypeStruct((KV_LORA, w_ukvt.shape[0]), F32), jax.ShapeDtypeStruct((1, Q_LORA), F32),
                   jax.ShapeDtypeStruct((1, KV_LORA), F32)],
        compiler_params=_params(("arbitrary",)),
    )(z_ckv, dq, dk, dv, q_norm, kv_norm, w_uqt, w_ukvt, cos, sin)


NT_DIMS = (((1,), (1,)), ((), ()))
TN_DIMS = (((0,), (0,)), ((), ()))


def _attn_fwd(q, k, v, *, t, hb):
    H, S, _ = q.shape
    n = S // t
    pairs = [(i, j) for i in range(n) for j in range(i + 1)]
    qi = jnp.asarray(np.array([p[0] for p in pairs], np.int32))
    ki = jnp.asarray(np.array([p[1] for p in pairs], np.int32))

    def body(qi_ref, ki_ref, q_ref, k_ref, v_ref, o_ref, lse_ref, m_s, l_s, acc_s):
        p = pl.program_id(1)
        i, j = qi_ref[p], ki_ref[p]

        @pl.when(j == 0)
        def _():
            m_s[...] = jnp.full(m_s.shape, NEG, F32)
            l_s[...] = jnp.zeros(l_s.shape, F32)
            acc_s[...] = jnp.zeros(acc_s.shape, F32)

        def block(hh, r0, nr, nk, masked):
            rows = slice(r0, r0 + nr)
            s = lax.dot_general(q_ref[hh, rows, :], k_ref[hh, 0:nk, :], NT_DIMS, preferred_element_type=F32)
            if masked:
                row = lax.broadcasted_iota(jnp.int32, (nr, nk), 0) + r0
                col = lax.broadcasted_iota(jnp.int32, (nr, nk), 1)
                s = jnp.where(row >= col, s, NEG)
            chunks = nk // LANES
            mc = s[:, 0:LANES]
            for c in range(1, chunks):
                mc = jnp.maximum(mc, s[:, c * LANES:(c + 1) * LANES])
            m_prev = m_s[hh, rows, :]
            m_new = jnp.maximum(m_prev, jnp.max(mc, axis=1, keepdims=True))
            alpha = jnp.exp2(m_prev - m_new)
            pr = jnp.exp2(s - jnp.concatenate([m_new] * chunks, axis=1))
            ls = pr[:, 0:LANES]
            for c in range(1, chunks):
                ls = ls + pr[:, c * LANES:(c + 1) * LANES]
            l_s[hh, rows, :] = alpha * l_s[hh, rows, :] + ls
            acc_s[hh, rows, :] = alpha * acc_s[hh, rows, :] + jnp.dot(pr.astype(BF16), v_ref[hh, 0:nk, :],
                                                                      preferred_element_type=F32)
            m_s[hh, rows, :] = m_new

        def step(diagonal):
            for hh in range(hb):
                if diagonal:
                    block(hh, 0, t // 2, t // 2, True)
                    block(hh, t // 2, t // 2, t, True)
                else:
                    block(hh, 0, t, t, False)

        @pl.when(j < i)
        def _():
            step(False)

        @pl.when(j == i)
        def _():
            step(True)
            for hh in range(hb):
                l = jnp.sum(l_s[hh], axis=1, keepdims=True)
                o_ref[:, hh * V_HEAD:(hh + 1) * V_HEAD] = acc_s[hh] / l
                lse_ref[hh] = (m_s[hh] + jnp.log2(l)).T[0:1, :]

    grid_spec = pltpu.PrefetchScalarGridSpec(
        num_scalar_prefetch=2, grid=(H // hb, len(pairs)),
        in_specs=[pl.BlockSpec((hb, t, QK_PAD), lambda h, p, qi, ki: (h, qi[p], 0)),
                  pl.BlockSpec((hb, t, QK_PAD), lambda h, p, qi, ki: (h, ki[p], 0)),
                  pl.BlockSpec((hb, t, V_HEAD), lambda h, p, qi, ki: (h, ki[p], 0))],
        out_specs=[pl.BlockSpec((t, hb * V_HEAD), lambda h, p, qi, ki: (qi[p], h)),
                   pl.BlockSpec((hb, 1, t), lambda h, p, qi, ki: (h, 0, qi[p]))],
        scratch_shapes=[pltpu.VMEM((hb, t, LANES), F32), pltpu.VMEM((hb, t, LANES), F32),
                        pltpu.VMEM((hb, t, V_HEAD), F32)],
    )
    return pl.pallas_call(
        body, name="attn_fwd", grid_spec=grid_spec,
        out_shape=[jax.ShapeDtypeStruct((S, H * V_HEAD), F32), jax.ShapeDtypeStruct((H, 1, S), F32)],
        compiler_params=_params(("parallel", "arbitrary")),
    )(qi, ki, q, k, v)


def _attn_bwd(q, k, v, do, lse_row, delta_row, *, t):
    H, S, _ = q.shape
    n = S // t
    pairs = [(i, j) for j in range(n) for i in range(j, n)]
    qi = jnp.asarray(np.array([p[0] for p in pairs], np.int32))
    ki = jnp.asarray(np.array([p[1] for p in pairs], np.int32))

    def body(qi_ref, ki_ref, q_ref, k_ref, v_ref, do_ref, lse_ref, dl_ref, dq_ref, dk_ref, dv_ref, dk_s, dv_s, dq_s):
        p = pl.program_id(1)
        i, j = qi_ref[p], ki_ref[p]

        @pl.when(p == 0)
        def _():
            dq_s[...] = jnp.zeros_like(dq_s)

        def block(k0, nk, q0, nq, masked):
            qb, dob = q_ref[0, q0:q0 + nq, :], do_ref[q0:q0 + nq, :]
            kb, vb = k_ref[0, k0:k0 + nk, :], v_ref[0, k0:k0 + nk, :]
            st = lax.dot_general(kb, qb, NT_DIMS, preferred_element_type=F32)
            if masked:
                krow = lax.broadcasted_iota(jnp.int32, (nk, nq), 0) + k0
                qcol = lax.broadcasted_iota(jnp.int32, (nk, nq), 1) + q0
                st = jnp.where(krow <= qcol, st, NEG)
            pt = jnp.exp2(st - lse_ref[0][:, q0:q0 + nq])
            dvp = jnp.dot(pt.astype(BF16), dob, preferred_element_type=F32)
            dpt = lax.dot_general(vb, dob, NT_DIMS, preferred_element_type=F32)
            dst = (pt * (dpt - dl_ref[0][:, q0:q0 + nq])).astype(BF16)
            dkp = jnp.dot(dst, qb, preferred_element_type=F32)
            rows = pl.ds(pl.multiple_of(i * t + q0, LANES), nq)
            dq_s[rows, :] += lax.dot_general(dst, kb, TN_DIMS, preferred_element_type=F32)
            return dkp, dvp

        @pl.when(i == j)
        def _():
            half = t // 2
            dk_s[0:half, :], dv_s[0:half, :] = block(0, half, 0, t, True)
            dk_s[half:t, :], dv_s[half:t, :] = block(half, half, half, half, True)

        @pl.when(i != j)
        def _():
            dkp, dvp = block(0, t, 0, t, False)
            dk_s[...] += dkp
            dv_s[...] += dvp

        @pl.when(i == n - 1)
        def _():
            dk_ref[0] = (dk_s[...] * LN2).astype(BF16)
            dv_ref[0] = dv_s[...].astype(BF16)

        @pl.when(p == len(pairs) - 1)
        def _():
            dq_ref[0] = dq_s[...].astype(BF16)

    grid_spec = pltpu.PrefetchScalarGridSpec(
        num_scalar_prefetch=2, grid=(H, len(pairs)),
        in_specs=[pl.BlockSpec((1, t, QK_PAD), lambda h, p, qi, ki: (h, qi[p], 0)),
                  pl.BlockSpec((1, t, QK_PAD), lambda h, p, qi, ki: (h, ki[p], 0)),
                  pl.BlockSpec((1, t, V_HEAD), lambda h, p, qi, ki: (h, ki[p], 0)),
                  pl.BlockSpec((t, V_HEAD), lambda h, p, qi, ki: (qi[p], h)),
                  pl.BlockSpec((1, 1, t), lambda h, p, qi, ki: (h, 0, qi[p])),
                  pl.BlockSpec((1, 1, t), lambda h, p, qi, ki: (h, 0, qi[p]))],
        out_specs=[pl.BlockSpec((1, S, QK_PAD), lambda h, p, qi, ki: (h, 0, 0)),
                   pl.BlockSpec((1, t, QK_PAD), lambda h, p, qi, ki: (h, ki[p], 0)),
                   pl.BlockSpec((1, t, V_HEAD), lambda h, p, qi, ki: (h, ki[p], 0))],
        scratch_shapes=[pltpu.VMEM((t, QK_PAD), F32), pltpu.VMEM((t, V_HEAD), F32), pltpu.VMEM((S, QK_PAD), F32)],
    )
    return pl.pallas_call(
        body, name="attn_bwd", grid_spec=grid_spec,
        out_shape=[jax.ShapeDtypeStruct((H, S, QK_PAD), BF16), jax.ShapeDtypeStruct((H, S, QK_PAD), BF16),
                   jax.ShapeDtypeStruct((H, S, V_HEAD), BF16)],
        compiler_params=_params(("parallel", "arbitrary")),
    )(qi, ki, q, k, v, do, lse_row, delta_row)


def _merge_h1(h, z_gates, o, x, w_out, norm_mlp, *, ts):
    S = h.shape[0]
    D = D_MODEL

    def body(h_ref, rg_ref, ga_ref, gb_ref, o_ref, x_ref, w_ref, g_ref, m_ref, h1_ref, n2_ref):
        gl, _ = _gelu_and_grad(rg_ref[...].astype(F32))
        m = (_sigmoid(ga_ref[...].astype(F32)) * (h_ref[...].astype(F32) * gl)
             + _sigmoid(gb_ref[...].astype(F32)) * o_ref[...]).astype(BF16)
        m_ref[...] = m
        h1 = x_ref[...] + jnp.dot(m, w_ref[...], preferred_element_type=F32)
        h1_ref[...] = h1
        n2, _ = _rms_fwd(h1, g_ref[...])
        n2_ref[...] = n2.astype(BF16)

    col = lambda c: pl.BlockSpec((ts, D), lambda i: (i, c))
    fixed = lambda shape: pl.BlockSpec(shape, lambda i: (0, 0), pipeline_mode=pl.Buffered(1))
    return pl.pallas_call(
        body, name="merge_h1", grid=(S // ts,),
        in_specs=[col(0), col(0), col(1), col(2), col(0), col(0), fixed((D, D)), fixed((1, D))],
        out_specs=[col(0), col(0), col(0)],
        out_shape=[jax.ShapeDtypeStruct((S, D), BF16), jax.ShapeDtypeStruct((S, D), F32),
                   jax.ShapeDtypeStruct((S, D), BF16)],
        compiler_params=_params(("parallel",)),
    )(h, z_gates, z_gates, z_gates, o, x, w_out, norm_mlp)


def _my_place():
    return lax.axis_index("x"), lax.axis_index("y"), lax.axis_index("c")


def _all_gather(shards, *, name):
    n = len(shards)

    def body(*refs):
        x_refs, out_refs = refs[:n], refs[n:2 * n]
        send_sems, recv_sems, local_sems = refs[2 * n:]
        x, y, c = _my_place()
        me, sibling = (x, y, c), (x, y, 1 - c)
        chips = [(1 - x, y), (x, 1 - y), (1 - x, 1 - y)]

        def slot(a, px, py, pc):
            return out_refs[a].at[4 * px + 2 * py + pc]

        def copy(a, k, block, to, src=None):
            return pltpu.make_async_remote_copy(
                src_ref=slot(a, *block) if src is None else src, dst_ref=slot(a, *block),
                send_sem=send_sems.at[7 * a + k], recv_sem=recv_sems.at[7 * a + k], device_id=to, device_id_type=MESH)

        mine = [pltpu.make_async_copy(x_refs[a], slot(a, *me), local_sems.at[a]) for a in range(n)]
        for cp in mine:
            cp.start()
        first = []
        for a in range(n):
            first.append(copy(a, 0, me, sibling, src=x_refs[a]))
            first += [copy(a, 1 + j, me, (*chip, c), src=x_refs[a]) for j, chip in enumerate(chips)]
        for cp in first:
            cp.start()
        passed = []
        for a in range(n):
            for j, chip in enumerate(chips):
                copy(a, 1 + j, (*chip, c), me).wait_recv()
                fwd = copy(a, 4 + j, (*chip, c), sibling)
                fwd.start()
                passed.append(fwd)
        for a in range(n):
            copy(a, 0, sibling, me).wait_recv()
            for j, chip in enumerate(chips):
                copy(a, 4 + j, (*chip, 1 - c), me).wait_recv()
        for cp in first + passed:
            cp.wait_send()
        for cp in mine:
            cp.wait()

    hbm = pl.BlockSpec(memory_space=pl.ANY)
    return pl.pallas_call(
        body, name=name, out_shape=[jax.ShapeDtypeStruct((N_DEV, *s.shape), s.dtype) for s in shards],
        in_specs=[hbm] * n, out_specs=[hbm] * n,
        scratch_shapes=[pltpu.SemaphoreType.DMA((7 * n,)), pltpu.SemaphoreType.DMA((7 * n,)),
                        pltpu.SemaphoreType.DMA((n,))],
    )(*shards)


def _pushes(src_refs, land_refs, send_sems, recv_sems, slab_per_peer):
    x, y, c = _my_place()
    me = 4 * x + 2 * y + c
    copies = []
    for a in range(len(src_refs)):
        for k in range(1, N_DEV):
            px, py, pc = x ^ (k >> 2), y ^ ((k >> 1) & 1), c ^ (k & 1)
            src = src_refs[a].at[4 * px + 2 * py + pc] if slab_per_peer else src_refs[a]
            copies.append(pltpu.make_async_remote_copy(
                src_ref=src, dst_ref=land_refs[a].at[me], send_sem=send_sems.at[7 * a + k - 1],
                recv_sem=recv_sems.at[7 * a + k - 1], device_id=(px, py, pc), device_id_type=MESH))
    return copies


def _push_start(srcs, *, name, slab_per_peer):
    n = len(srcs)
    lands = [lax.empty((N_DEV, *(s.shape[1:] if slab_per_peer else s.shape)), s.dtype) for s in srcs]

    def body(*refs):
        src_refs, land_refs = refs[:n], refs[n:2 * n]
        send_sems, recv_sems, token = refs[2 * n], refs[2 * n + 1], refs[-1]
        for cp in _pushes(src_refs, land_refs, send_sems, recv_sems, slab_per_peer):
            cp.start()
        token[...] = jnp.zeros_like(token)

    hbm = pl.BlockSpec(memory_space=pltpu.HBM)
    sem = pl.BlockSpec(memory_space=pltpu.SEMAPHORE)
    out = pl.pallas_call(
        body, name=name,
        out_shape=(pltpu.SemaphoreType.DMA((7 * n,)), pltpu.SemaphoreType.DMA((7 * n,)),
                   *[pltpu.HBM(a.shape, a.dtype) for a in srcs + lands], jax.ShapeDtypeStruct((8, LANES), F32)),
        in_specs=[hbm] * (2 * n), out_specs=(sem, sem, *[hbm] * (2 * n), pl.BlockSpec(memory_space=pltpu.VMEM)),
        input_output_aliases={i: 2 + i for i in range(2 * n)},
        compiler_params=pltpu.CompilerParams(has_side_effects=pltpu.SideEffectType.DATAFLOW_SIDE_EFFECTING),
    )(*[pltpu.with_memory_space_constraint(a, pltpu.HBM) for a in srcs + lands])
    return out[0], out[1], list(out[2:2 + n]), list(out[2 + n:2 + 2 * n]), out[-1]


def _push_wait(send_sems, recv_sems, srcs, lands, after, *, name, slab_per_peer):
    n = len(srcs)

    def body(*refs):
        src_refs, land_refs = refs[:n], refs[n:2 * n]
        s_sems, r_sems = refs[2 * n], refs[2 * n + 1]
        for cp in _pushes(src_refs, land_refs, s_sems, r_sems, slab_per_peer):
            cp.wait_send()
            cp.wait_recv()

    hbm = pl.BlockSpec(memory_space=pltpu.HBM)
    sem = pl.BlockSpec(memory_space=pltpu.SEMAPHORE)
    out = pl.pallas_call(
        body, name=name, out_shape=tuple(pltpu.HBM(a.shape, a.dtype) for a in srcs + lands),
        in_specs=[hbm] * (2 * n) + [sem, sem, pl.BlockSpec(memory_space=pl.ANY)], out_specs=tuple([hbm] * (2 * n)),
        input_output_aliases={i: i for i in range(2 * n)},
        compiler_params=pltpu.CompilerParams(has_side_effects=pltpu.SideEffectType.DATAFLOW_SIDE_EFFECTING),
    )(*srcs, *lands, send_sems, recv_sems, after)
    return list(out[:n]), list(out[n:])


def _sum_parts(gp_ref, rows):
    g = gp_ref[0, 0:rows, :].astype(F32)
    for p in range(1, gp_ref.shape[0]):
        g = g + gp_ref[p, 0:rows, :].astype(F32)
    return g


def _adamw_update(w, m, v, g):
    m_new = ADAM_B1 * m + (1.0 - ADAM_B1) * g
    v_new = ADAM_B2 * v + (1.0 - ADAM_B2) * (g * g)
    m_hat = m_new / (1.0 - ADAM_B1 ** ADAM_STEP)
    v_hat = v_new / (1.0 - ADAM_B2 ** ADAM_STEP)
    return -ADAM_LR * (m_hat / (jnp.sqrt(v_hat) + ADAM_EPS) + ADAM_WD * w), m_new, v_new


def _adamw_many(ws, ms, vs, gparts, sums, *, name):
    n, k = len(ws), len(sums)

    def body(*refs):
        w_refs, m_refs, v_refs = refs[:n], refs[n:2 * n], refs[2 * n:3 * n]
        g_refs, s_refs, outs = refs[3 * n:4 * n], refs[4 * n:4 * n + k], refs[4 * n + k:]
        for a in range(n):
            g = _sum_parts(g_refs[a], w_refs[a].shape[0])
            d, m_new, v_new = _adamw_update(w_refs[a][...], m_refs[a][...], v_refs[a][...], g)
            for o_ref, val in zip(outs[4 * a:4 * a + 4], (g, d, m_new, v_new)):
                o_ref[...] = val
        for b in range(k):
            outs[4 * n + b][...] = _sum_parts(s_refs[b], s_refs[b].shape[1])

    out_shape = [jax.ShapeDtypeStruct(w.shape, F32) for w in ws for _ in range(4)]
    out_shape += [jax.ShapeDtypeStruct(s.shape[1:], F32) for s in sums]
    return pl.pallas_call(body, name=name, out_shape=out_shape, compiler_params=_params())(
        *ws, *ms, *vs, *gparts, *sums)


def _adamw(w, m, v, gparts, *, tr, name):
    R, C = w.shape
    n_parts = gparts.shape[0]

    def body(w_ref, m_ref, v_ref, gp_ref, g_ref, d_ref, nm_ref, nv_ref):
        g = _sum_parts(gp_ref, tr)
        d_ref[...], nm_ref[...], nv_ref[...] = _adamw_update(w_ref[...], m_ref[...], v_ref[...], g)
        g_ref[...] = g

    row = pl.BlockSpec((tr, C), lambda i: (i, 0))
    shp = jax.ShapeDtypeStruct((R, C), F32)
    return pl.pallas_call(
        body, name=name, grid=(R // tr,),
        in_specs=[row, row, row, pl.BlockSpec((n_parts, tr, C), lambda i: (0, i, 0))],
        out_specs=[row, row, row, row], out_shape=[shp, shp, shp, shp],
        compiler_params=_params(("parallel",)),
    )(w, m, v, gparts)


def _rope_tables(s):
    pos = jnp.arange(s, dtype=F32)
    inv_freq = 1.0 / (ROPE_THETA ** (jnp.arange(0, QK_ROPE, 2, dtype=F32) / QK_ROPE))
    per_lane = jnp.concatenate([inv_freq, inv_freq, jnp.zeros((LANES - QK_ROPE,), F32)])
    ang = pos[:, None] * per_lane[None, :]
    live = jnp.arange(LANES) < QK_ROPE
    return jnp.where(live, jnp.cos(ang), 0.0), jnp.where(live, jnp.sin(ang), 0.0)


def _pick(n, want):
    t = min(n, want)
    assert n % t == 0
    return t


def _local_step(x, target, wts, small, hooks):
    S = x.shape[0]
    H = MLA_HEADS
    ts = _pick(S, 1024)
    tm = _pick(S, 512)
    tm_wide = _pick(S, 1024)
    tk_s = _pick(S, 4096)
    tt = _pick(S, 512)
    ta = _pick(S, 1024)
    ts_proj = _pick(S, 512)
    ts_merge = _pick(S, 512)
    row = lambda v: v.reshape(1, -1)
    w_in = wts["w_in"]
    w_main = jnp.concatenate([w_in[:, 0:2048], w_in[:, 2624:4672]], axis=1)
    w_ckv = jnp.concatenate([w_in[:, 2048:2624], jnp.zeros((D_MODEL, CKV_W - 576), BF16)], axis=1)
    w_uq3 = wts["w_uq"].reshape(Q_LORA, H, QK_NOPE + QK_ROPE)
    w_uq_p = jnp.concatenate(
        [w_uq3[:, :, :QK_NOPE].reshape(Q_LORA, H * QK_NOPE),
         jnp.pad(w_uq3[:, :, QK_NOPE:], ((0, 0), (0, 0), (0, LANES - QK_ROPE))).reshape(Q_LORA, H * LANES)], axis=1)
    w_ukv = wts["w_ukv"]
    cos, sin = _rope_tables(S)
    conv_w, conv_b = small["conv_w"], row(small["conv_b"])
    wa, wx = small["lru_wa"].astype(BF16), small["lru_wx"].astype(BF16)
    wat, wxt = jnp.swapaxes(wa, 1, 2), jnp.swapaxes(wx, 1, 2)
    ba, bx = small["lru_ba"].reshape(RNN_BLOCKS, 1, RNN_BLOCK_W), small["lru_bx"].reshape(RNN_BLOCKS, 1, RNN_BLOCK_W)
    lam = row(small["lru_lambda"])
    q_norm, kv_norm = row(small["q_norm"]), row(small["kv_norm"])
    norm_mix, norm_mlp, norm_final = row(small["norm_mix"]), row(small["norm_mlp"]), row(small["norm_final"])

    xn = _rmsnorm_cast(x, norm_mix, ts=ts, name="norm_mix")
    ident = lambda acc: (acc,)
    (z_rx,) = _mm(xn, w_main[:, :D_MODEL], name="z_rx", tm=tm_wide, tn=1024, tk=1024, outs=[("tile", F32)],
                  epilogue=ident)
    (z_gates,) = _mm(xn, w_main[:, D_MODEL:], name="z_gates", tm=tm_wide, tn=3 * D_MODEL, tk=1024, outs=[("tile", BF16)],
                     epilogue=ident)
    (z_ckv,) = _mm(xn, w_ckv, name="z_ckv", tm=tm, tn=CKV_W, tk=1024, outs=[("tile", F32)], epilogue=ident)
    h = _lru_fwd(z_rx, conv_w, conv_b, wa, ba, wx, bx, lam, tt=tt)
    q, k, v = _mla_proj(z_ckv, q_norm, kv_norm, w_uq_p, w_ukv, cos, sin, ts=ts_proj)
    o, lse = _attn_fwd(q, k, v, t=ta, hb=2)
    w_out, w_up, w_down = hooks["weights_later"](o)
    merged, h1, n2 = _merge_h1(h, z_gates, o, x, w_out, norm_mlp, ts=ts_merge)

    def ep_up(acc):
        r = jnp.maximum(acc, 0.0)
        return r * r, r

    act, relu = _mm(n2, w_up, name="up", tm=tm_wide, tn=2048, tk=1024, outs=[("tile", BF16), ("tile", BF16)],
                    epilogue=ep_up)

    def ep_loss(acc, h1v, tgt, g):
        h2 = acc + h1v
        y, _ = _rms_fwd(h2, g)
        err = y - tgt
        loss_rows = 0.5 * jnp.mean(err * err, axis=-1, keepdims=True)
        dy = err * (1.0 / D_MODEL)
        dh2, dg_rows = _rms_bwd(dy, h2, g)
        lsum = jnp.sum(loss_rows, axis=0, keepdims=True)
        return dh2, dh2, jnp.sum(dg_rows, axis=0, keepdims=True), jnp.broadcast_to(lsum, (1, D_MODEL))

    dh2, dh2b, dnf_p, loss_p = _mm(
        act, w_down, name="down_loss", tm=tm, tn=1024, tk=D_FF,
        outs=[("tile", F32), ("tile", BF16), ("rowpart", F32), ("rowpart", F32)], epilogue=ep_loss,
        extras=[("tile", h1), ("tile", target), ("row", norm_final)])
    loss_part = jnp.sum(loss_p[:, 0, 0])
    d_norm_final = jnp.sum(dnf_p, axis=(0, 1))

    def ep_du(acc, r):
        return (acc * (2.0 * r.astype(F32)),)

    (du,) = _mm(dh2b, w_down, name="d_act", tb=True, tm=tm_wide, tn=2048, tk=1024, outs=[("tile", BF16)], epilogue=ep_du,
                extras=[("tile", relu)])

    def ep_dh1(acc, h1v, dh2v, g):
        dv, dg_rows = _rms_bwd(acc, h1v, g)
        dh1 = dh2v + dv
        return dh1, dh1, jnp.sum(dg_rows, axis=0, keepdims=True)

    dh1, dh1b, dnm_p = _mm(du, w_up, name="d_n2", tb=True, tm=tm, tn=1024, tk=D_FF,
                           outs=[("tile", F32), ("tile", BF16), ("rowpart", F32)], epilogue=ep_dh1,
                           extras=[("tile", h1), ("tile", dh2), ("row", norm_mlp)])
    d_norm_mlp = jnp.sum(dnm_p, axis=(0, 1))
    tn_mm = functools.partial(_mm, ta=True, tk=tk_s, outs=[("tile", BF16)], epilogue=ident)
    (d_w_down,) = tn_mm(act, dh2b, name="dw_down", tm=1024, tn=1024)
    (p_w_up,) = _mm(n2, du, name="dw_up", ta=True, tk=tk_s, tm=1024, tn=D_FF // N_DEV, outs=[("colshard", BF16)],
                    epilogue=ident)
    (d_w_out,) = tn_mm(merged, dh1b, name="dw_out", tm=1024, tn=1024)
    early = [d_w_out.reshape(N_DEV, -1, D_MODEL), p_w_up, d_w_down.reshape(N_DEV, -1, D_MODEL)]
    w_out = w_out + hooks["send"]("early", early)[0, 0].astype(BF16)


    def ep_dmerge(dm, hv, rg, ga, gb, ov):
        hv, rg, ga, gb = hv.astype(F32), rg.astype(F32), ga.astype(F32), gb.astype(F32)
        gl, dgl = _gelu_and_grad(rg)
        sa, sb = _sigmoid(ga), _sigmoid(gb)
        ya = hv * gl
        dya = dm * sa
        do = dm * sb
        dga = dm * ya * sa * (1.0 - sa)
        dgb = dm * ov * sb * (1.0 - sb)
        dh = dya * gl
        drg = dya * hv * dgl
        dov = do * ov
        lane = lax.broadcasted_iota(jnp.int32, (dm.shape[0], LANES), 1)
        delta = jnp.zeros((dm.shape[0], LANES), F32)
        for hh in range(H):
            dsum = jnp.sum(dov[:, hh * V_HEAD:(hh + 1) * V_HEAD], axis=1, keepdims=True)
            delta = jnp.where(lane == hh, dsum, delta)
        return dh, jnp.concatenate([drg, dga, dgb], axis=1), do, delta

    dh_lru, dz_part, do, delta_w = _mm(
        dh1b, w_out, name="d_merge", tb=True, tm=ts_merge, tn=1024, tk=1024,
        outs=[("tile", F32), ("cols", BF16, 4 * D_MODEL, D_MODEL), ("tile", BF16), ("side", F32)],
        epilogue=ep_dmerge,
        extras=[("tile", h), ("tilecol", z_gates, 0), ("tilecol", z_gates, 1), ("tilecol", z_gates, 2), ("tile", o)])
    delta_row = delta_w[:, :H].T.reshape(H, 1, S)
    lse_row = lse

    dq, dk, dv = _attn_bwd(q, k, v, do, lse_row, delta_row, t=ta)
    dz_ckv, d_w_uq_p, d_w_ukv, d_q_norm, d_kv_norm = _mla_proj_bwd(
        z_ckv, dq, dk, dv, q_norm, kv_norm, w_uq_p.T, w_ukv.T, cos, sin, ts=ts_proj)
    d_w_uq = jnp.concatenate(
        [d_w_uq_p[:, :H * QK_NOPE].reshape(Q_LORA, H, QK_NOPE),
         d_w_uq_p[:, H * QK_NOPE:].reshape(Q_LORA, H, LANES)[:, :, :QK_ROPE]], axis=2).reshape(Q_LORA, -1)

    dz_main, d_wa, d_wx, d_ba, d_bx, d_lam, d_conv_w, d_conv_b = _lru_bwd(
        z_rx, h, dh_lru, dz_part, conv_w, conv_b, wa, wat, ba, wx, wxt, bx, lam, tt=tt)

    (d_w_main,) = tn_mm(xn, dz_main, name="dw_main", tm=1024, tn=1024)
    (d_w_ckv,) = tn_mm(xn, dz_ckv, name="dw_ckv", tm=1024, tn=CKV_W)
    d_w_in = jnp.concatenate([d_w_main[:, 0:2048], d_w_ckv[:, 0:576], d_w_main[:, 2048:4096]], axis=1)

    def col_parts(full):
        r = full.shape[0]
        return jnp.transpose(full.astype(BF16).reshape(r, N_DEV, -1), (1, 0, 2))

    late = [col_parts(d_w_in), col_parts(d_w_uq), col_parts(d_w_ukv)]
    norm_mix = norm_mix + hooks["send"]("late", late)[0, 0]

    def ep_dx(acc, xv, dh1v, g):
        dv, dg_rows = _rms_bwd(acc, xv, g)
        return dh1v + dv, jnp.sum(dg_rows, axis=0, keepdims=True)

    grad_x, dnx_p = _mm(dz_main, w_main, name="dx", tb=True, tm=tm, tn=1024, tk=4 * D_MODEL,
                        outs=[("tile", F32), ("rowpart", F32)], epilogue=ep_dx, more=(dz_ckv, w_ckv),
                        extras=[("tile", x), ("tile", dh1), ("row", norm_mix)])
    d_norm_mix = jnp.sum(dnx_p, axis=(0, 1))
    sm = {"norm_mix": d_norm_mix, "conv_w": d_conv_w, "conv_b": d_conv_b.reshape(-1), "lru_wa": d_wa,
          "lru_ba": d_ba.reshape(RNN_BLOCKS, RNN_BLOCK_W), "lru_wx": d_wx, "lru_bx": d_bx.reshape(RNN_BLOCKS, RNN_BLOCK_W),
          "lru_lambda": d_lam.reshape(-1), "q_norm": d_q_norm.reshape(-1), "kv_norm": d_kv_norm.reshape(-1),
          "norm_mlp": d_norm_mlp, "norm_final": d_norm_final}
    return loss_part, grad_x, sm


BIG = ("w_in", "w_uq", "w_ukv", "w_out", "w_up", "w_down")
SMALL = ("norm_mix", "conv_b", "lru_wa", "lru_ba", "lru_wx", "lru_bx", "lru_lambda", "q_norm", "kv_norm", "norm_mlp",
         "norm_final")
WEIGHTS = ("norm_mix", "w_in", "conv_w", "conv_b", "lru_wa", "lru_ba", "lru_wx", "lru_bx", "lru_lambda", "q_norm", "w_uq",
           "kv_norm", "w_ukv", "w_out", "norm_mlp", "w_up", "w_down", "norm_final")
ADAM_TILE_ROWS = {"w_in": 256, "w_uq": 128, "w_ukv": 128, "w_out": 64, "w_up": 256, "w_down": 128}
CONV_ROWS = N_DEV * 8


def _rows(a):
    return a.reshape(-1, LANES)


def _pad_rows(a, mult):
    r = a.shape[-2]
    pad = (-r) % mult
    if pad == 0:
        return a
    cfg = [(0, 0)] * (a.ndim - 2) + [(0, pad), (0, 0)]
    return jnp.pad(a, cfg)


def _cols_from_shards(g):
    return jnp.transpose(g, (1, 0, 2)).reshape(g.shape[1], -1)


def kernel(x, norm_mix, w_in, conv_w, conv_b, lru_wa, lru_ba, lru_wx, lru_bx, lru_lambda, q_norm, w_uq, kv_norm, w_ukv, w_out, norm_mlp, w_up, w_down, norm_final, loss_target, m_norm_mix, m_w_in, m_conv_w, m_conv_b, m_lru_wa, m_lru_ba, m_lru_wx, m_lru_bx, m_lru_lambda, m_q_norm, m_w_uq, m_kv_norm, m_w_ukv, m_w_out, m_norm_mlp, m_w_up, m_w_down, m_norm_final, v_norm_mix, v_w_in, v_conv_w, v_conv_b, v_lru_wa, v_lru_ba, v_lru_wx, v_lru_bx, v_lru_lambda, v_q_norm, v_w_uq, v_kv_norm, v_w_ukv, v_w_out, v_norm_mlp, v_w_up, v_w_down, v_norm_final):
    W = dict(norm_mix=norm_mix, w_in=w_in, conv_w=conv_w, conv_b=conv_b, lru_wa=lru_wa, lru_ba=lru_ba, lru_wx=lru_wx,
             lru_bx=lru_bx, lru_lambda=lru_lambda, q_norm=q_norm, w_uq=w_uq, kv_norm=kv_norm, w_ukv=w_ukv, w_out=w_out,
             norm_mlp=norm_mlp, w_up=w_up, w_down=w_down, norm_final=norm_final)
    M = dict(norm_mix=m_norm_mix, w_in=m_w_in, conv_w=m_conv_w, conv_b=m_conv_b, lru_wa=m_lru_wa, lru_ba=m_lru_ba,
             lru_wx=m_lru_wx, lru_bx=m_lru_bx, lru_lambda=m_lru_lambda, q_norm=m_q_norm, w_uq=m_w_uq, kv_norm=m_kv_norm,
             w_ukv=m_w_ukv, w_out=m_w_out, norm_mlp=m_norm_mlp, w_up=m_w_up, w_down=m_w_down, norm_final=m_norm_final)
    V = dict(norm_mix=v_norm_mix, w_in=v_w_in, conv_w=v_conv_w, conv_b=v_conv_b, lru_wa=v_lru_wa, lru_ba=v_lru_ba,
             lru_wx=v_lru_wx, lru_bx=v_lru_bx, lru_lambda=v_lru_lambda, q_norm=v_q_norm, w_uq=v_w_uq, kv_norm=v_kv_norm,
             w_ukv=v_w_ukv, w_out=v_w_out, norm_mlp=v_norm_mlp, w_up=v_w_up, w_down=v_w_down, norm_final=v_norm_final)
    me = 4 * lax.axis_index("x") + 2 * lax.axis_index("y") + lax.axis_index("c")

    first, later = ("w_in", "w_uq", "w_ukv"), ("w_out", "w_up", "w_down")
    got = _all_gather([W[n].astype(BF16) for n in first] + [_pad_rows(conv_w, 8)], name="gather_weights")
    wts = {"w_in": _cols_from_shards(got[0]), "w_uq": _cols_from_shards(got[1]), "w_ukv": _cols_from_shards(got[2])}
    w_send, w_recv, w_src, w_land, zeros = _push_start([W[n].astype(BF16) for n in later], name="gather_later_start",
                                                       slab_per_peer=False)
    small = {n: W[n] for n in SMALL}
    small["conv_w"] = _cols_from_shards(got[3][:, :CONV_WIDTH])
    small["norm_mix"] = norm_mix + zeros[0, 0]

    def with_own_slab(land, mine):
        return lax.dynamic_update_slice(land, mine, (me, 0, 0))

    def weights_later(after):
        srcs, lands = _push_wait(w_send, w_recv, w_src, w_land, after, name="gather_later_wait", slab_per_peer=False)
        w_out_g, w_up_g, w_down_g = [with_own_slab(l, s[None]) for l, s in zip(lands, srcs)]
        return w_out_g.reshape(-1, D_MODEL), _cols_from_shards(w_up_g), w_down_g.reshape(-1, D_MODEL)

    sent = {}
    G, Dl, NM, NV = {}, {}, {}, {}

    def finish(group, names, after):
        s_sems, r_sems, srcs, lands, _ = sent[group]
        srcs, lands = _push_wait(s_sems, r_sems, srcs, lands, after, name="exchange_" + group + "_wait",
                                 slab_per_peer=True)
        for n, src, land in zip(names, srcs, lands):
            parts = with_own_slab(land, lax.dynamic_slice(src, (me, 0, 0), (1, *src.shape[1:])))
            G[n], Dl[n], NM[n], NV[n] = _adamw(W[n], M[n], V[n], parts, tr=ADAM_TILE_ROWS[n], name="adamw_" + n)

    def send(group, parts):
        sent[group] = _push_start(parts, name="exchange_" + group + "_start", slab_per_peer=True)
        zeros = sent[group][4]
        if group == "late":
            finish("early", later, zeros)
            zeros = zeros + 0.0 * (Dl["w_out"][0:8, 0:LANES] + Dl["w_up"][0:8, 0:LANES] + Dl["w_down"][0:8, 0:LANES])
        return zeros

    loss_part, grad_x, g_small = _local_step(x[0], loss_target[0], wts, small,
                                              {"weights_later": weights_later, "send": send})
    finish("late", first, grad_x)

    conv_rows = _pad_rows(jnp.transpose(g_small["conv_w"].reshape(CONV_WIDTH, N_DEV, LANES), (1, 0, 2)), 8)
    loss_rows = jnp.zeros((8, LANES), F32).at[0, 0].set(loss_part)
    as_sent = lambda n: g_small[n].astype(BF16) if n in ("lru_wa", "lru_wx") else g_small[n]
    gathered = _all_gather([_pad_rows(_rows(as_sent(n)), 8) for n in SMALL]
                           + [conv_rows.reshape(CONV_ROWS, LANES), loss_rows], name="gather_small")
    k = len(SMALL)
    outs = _adamw_many([_rows(W[n]) for n in SMALL], [_rows(M[n]) for n in SMALL], [_rows(V[n]) for n in SMALL],
                       gathered[:k], gathered[k:], name="adamw_small")
    for j, n in enumerate(SMALL):
        for out, o in zip((G, Dl, NM, NV), outs[4 * j:4 * j + 4]):
            out[n] = o.reshape(W[n].shape)
    conv_sum, loss_sum = outs[4 * k:]
    loss = loss_sum[0, 0]

    g_conv = lax.dynamic_slice(conv_sum, (me * 8, 0), (8, LANES))
    conv_out = _adamw(_pad_rows(conv_w, 8), _pad_rows(m_conv_w, 8), _pad_rows(v_conv_w, 8), g_conv[None], tr=8,
                      name="adamw_conv_w")
    for out, pk in zip((G, Dl, NM, NV), conv_out):
        out["conv_w"] = pk[:CONV_WIDTH]
    return (loss, grad_x[None], *[G[n] for n in WEIGHTS], *[Dl[n] for n in WEIGHTS], *[NM[n] for n in WEIGHTS],
            *[NV[n] for n in WEIGHTS])
```

```python
import functools

import numpy as np
import jax
import jax.numpy as jnp
from jax import lax
from jax.experimental import pallas as pl
from jax.experimental.pallas import tpu as pltpu

F32 = jnp.float32
BF16 = jnp.bfloat16
MESH = pl.DeviceIdType.MESH

D_MODEL = 1024
N_DEV = 8
LANES = 128
RNN_BLOCKS = 8
RNN_BLOCK_W = 128
CONV_WIDTH = 4
LRU_C = 8.0
MLA_HEADS = 8
QK_NOPE = 128
QK_ROPE = 64
V_HEAD = 128
QK_PAD = 256
Q_LORA = 256
KV_LORA = 256
CKV_W = 640
ROPE_THETA = 10000.0
D_FF = 4096
EPS = 1e-6
ATTN_SCALE = (QK_NOPE + QK_ROPE) ** -0.5
LOG2E = 1.4426950408889634
LN2 = 0.6931471805599453
NEG = -1e30

ADAM_LR = 0.001
ADAM_B1 = 0.9
ADAM_B2 = 0.999
ADAM_EPS = 1e-08
ADAM_WD = 0.01
ADAM_STEP = 10

VMEM_LIMIT = 56 * 1024 * 1024


def _params(sem=None):
    return pltpu.CompilerParams(dimension_semantics=sem, vmem_limit_bytes=VMEM_LIMIT)


def _sigmoid(v):
    return 1.0 / (1.0 + jnp.exp(-v))


def _softplus(y):
    e = jnp.exp(-jnp.abs(y))
    u = 1.0 + e
    d = u - 1.0
    l1p = jnp.where(d == 0.0, e, jnp.log(u) * e / jnp.where(d == 0.0, 1.0, d))
    return jnp.maximum(y, 0.0) + l1p


_GELU_K = 0.7978845608028654
_GELU_C = 0.044715


def _gelu_and_grad(v):
    t = jnp.tanh(_GELU_K * (v + _GELU_C * v * v * v))
    g = 0.5 * v * (1.0 + t)
    dg = 0.5 * (1.0 + t) + 0.5 * v * (1.0 - t * t) * _GELU_K * (1.0 + 3.0 * _GELU_C * v * v)
    return g, dg


def _rms_fwd(v, g):
    rstd = lax.rsqrt(jnp.mean(v * v, axis=-1, keepdims=True) + EPS)
    return v * rstd * g, rstd


def _rms_bwd(dy, v, g):
    rstd = lax.rsqrt(jnp.mean(v * v, axis=-1, keepdims=True) + EPS)
    vh = v * rstd
    dvh = dy * g
    dv = rstd * (dvh - vh * jnp.mean(dvh * vh, axis=-1, keepdims=True))
    return dv, dy * vh


def _shift_down(v, s, fill, row):
    return jnp.where(row >= s, pltpu.roll(v, s, 0), fill)


def _shift_up(v, s, fill, row, n):
    return jnp.where(row < n - s, pltpu.roll(v, n - s, 0), fill)


def _rot_half(v, lane):
    n = v.shape[-1]
    l = lane & (LANES - 1)
    up = pltpu.roll(v, n - QK_ROPE // 2, 1)
    dn = pltpu.roll(v, QK_ROPE // 2, 1)
    return jnp.where(l < QK_ROPE // 2, -up, jnp.where(l < QK_ROPE, dn, 0.0))


def _mm(a, b, *, name, tm, tn, tk, outs, epilogue, extras=(), ta=False, tb=False, more=None, also=None):
    assert not (ta and tb)
    if ta:
        K, M = a.shape
    else:
        M, K = a.shape
    if tb:
        N, K2 = b.shape
    else:
        K2, N = b.shape
    assert K == K2 and M % tm == 0 and N % tn == 0 and K % tk == 0, (name, a.shape, b.shape)
    n_i, n_j, n_k = M // tm, N // tn, K // tk
    n_ex, n_out = len(extras), len(outs)
    n_more = 0 if more is None else 2
    n_also = 0 if also is None else 1
    assert more is None or (n_k == 1 and not ta)
    assert also is None or (n_k == 1 and n_j == 1 and not ta and more is None)

    def body(*refs):
        a_ref, b_ref = refs[0], refs[1]
        ex_refs = refs[2 + n_more:2 + n_more + n_ex]
        first_out = 2 + n_more + n_ex + n_also
        out_refs = refs[first_out:first_out + n_out]
        if ta:
            part = lax.dot_general(a_ref[...], b_ref[...], (((0,), (0,)), ((), ())), preferred_element_type=F32)
        elif tb:
            part = lax.dot_general(a_ref[...], b_ref[...], (((1,), (1,)), ((), ())), preferred_element_type=F32)
        else:
            part = jnp.dot(a_ref[...], b_ref[...], preferred_element_type=F32)
        if more is not None:
            part = part + lax.dot_general(refs[2][...], refs[3][...], (((1,), (1,)), ((), ())),
                                          preferred_element_type=F32)

        def finish(acc):
            res = epilogue(acc, *[r[...] for r in ex_refs])
            for o_ref, r, spec in zip(out_refs, res, outs):
                if spec[0] == "cols":
                    o_ref[:, spec[3]:spec[3] + r.shape[1]] = r.astype(o_ref.dtype)
                else:
                    o_ref[...] = r.astype(o_ref.dtype).reshape(o_ref.shape)

        if also is not None:
            refs[first_out + n_out][...] = jnp.dot(a_ref[...], refs[first_out - 1][...], preferred_element_type=F32)
        if n_k == 1:
            finish(part)
        else:
            acc_ref = refs[-1]
            k = pl.program_id(2)

            @pl.when(k == 0)
            def _():
                acc_ref[...] = part

            @pl.when(k > 0)
            def _():
                acc_ref[...] += part

            @pl.when(k == n_k - 1)
            def _():
                finish(acc_ref[...])

    a_spec = pl.BlockSpec((tk, tm), lambda j, i, k: (k, i)) if ta else pl.BlockSpec((tm, tk), lambda j, i, k: (i, k))
    b_once = dict(pipeline_mode=pl.Buffered(1)) if (n_j == 1 and n_k == 1) else {}
    if tb:
        in_specs = [a_spec, pl.BlockSpec((tn, tk), lambda j, i, k: (j, k), **b_once)]
    else:
        in_specs = [a_spec, pl.BlockSpec((tk, tn), lambda j, i, k: (k, j), **b_once)]
    if more is not None:
        k2 = more[0].shape[1]
        in_specs += [pl.BlockSpec((tm, k2), lambda j, i, k: (i, 0)), pl.BlockSpec((tn, k2), lambda j, i, k: (j, 0), **b_once)]
    for ex in extras:
        kind = ex[0]
        if kind == "tile":
            in_specs.append(pl.BlockSpec((tm, tn), lambda j, i, k: (i, j)))
        elif kind == "tilecol":
            assert n_j == 1
            in_specs.append(pl.BlockSpec((tm, tn), functools.partial(lambda c, j, i, k: (i, c), ex[2])))
        else:
            in_specs.append(pl.BlockSpec((1, tn), lambda j, i, k: (0, j)))
    out_specs, out_shape = [], []
    for kind, dt, *rest in outs:
        if kind == "tile":
            out_specs.append(pl.BlockSpec((tm, tn), lambda j, i, k: (i, j)))
            out_shape.append(jax.ShapeDtypeStruct((M, N), dt))
        elif kind == "colshard":
            out_specs.append(pl.BlockSpec((1, tm, tn), lambda j, i, k: (j, i, 0)))
            out_shape.append(jax.ShapeDtypeStruct((n_j, M, tn), dt))
        elif kind == "cols":
            assert n_j == 1
            out_specs.append(pl.BlockSpec((tm, rest[0]), lambda j, i, k: (i, 0)))
            out_shape.append(jax.ShapeDtypeStruct((M, rest[0]), dt))
        elif kind == "side":
            assert n_j == 1
            out_specs.append(pl.BlockSpec((tm, LANES), lambda j, i, k: (i, 0)))
            out_shape.append(jax.ShapeDtypeStruct((M, LANES), dt))
        else:
            out_specs.append(pl.BlockSpec((1, 1, tn), lambda j, i, k: (i, 0, j)))
            out_shape.append(jax.ShapeDtypeStruct((n_i, 1, N), dt))
    scratch = [pltpu.VMEM((tm, tn), F32)] if n_k > 1 else []
    if also is not None:
        in_specs.append(pl.BlockSpec(also.shape, lambda j, i, k: (0, 0), **b_once))
        out_specs.append(pl.BlockSpec((tm, also.shape[1]), lambda j, i, k: (i, 0)))
        out_shape.append(jax.ShapeDtypeStruct((M, also.shape[1]), F32))
    return pl.pallas_call(
        body, name=name, grid=(n_j, n_i, n_k), in_specs=in_specs, out_specs=out_specs, out_shape=out_shape,
        scratch_shapes=scratch, compiler_params=_params(("parallel", "parallel", "arbitrary")),
    )(a, b, *(more or ()), *[ex[1] for ex in extras], *([] if also is None else [also]))


def _rmsnorm_cast(x, g, *, ts, name):
    S, D = x.shape

    def body(x_ref, g_ref, o_ref):
        y, _ = _rms_fwd(x_ref[...], g_ref[...])
        o_ref[...] = y.astype(BF16)

    return pl.pallas_call(
        body, name=name, grid=(S // ts,),
        in_specs=[pl.BlockSpec((ts, D), lambda i: (i, 0)), pl.BlockSpec((1, D), lambda i: (0, 0))],
        out_specs=pl.BlockSpec((ts, D), lambda i: (i, 0)), out_shape=jax.ShapeDtypeStruct((S, D), BF16),
        compiler_params=_params(("parallel",)),
    )(x, g)


LRU_NB = 4


def _lru_gates(xa, wa_ref, ba_ref, wx_ref, bx_ref, lam):
    xab = xa.astype(BF16)
    W = RNN_BLOCK_W
    rs, is_ = [], []
    for j in range(LRU_NB):
        xj = xab[:, j * W:(j + 1) * W]
        rs.append(_sigmoid(jnp.dot(xj, wa_ref[j], preferred_element_type=F32) + ba_ref[j]))
        is_.append(_sigmoid(jnp.dot(xj, wx_ref[j], preferred_element_type=F32) + bx_ref[j]))
    r = jnp.concatenate(rs, axis=1)
    i = jnp.concatenate(is_, axis=1)
    sp = _softplus(-lam)
    log_a = (-LRU_C * r) * sp
    a = jnp.exp(log_a)
    y = 2.0 * log_a
    one_m = jnp.where(y > -0.01, -y * (1.0 + 0.5 * y * (1.0 + y * (1.0 / 3.0))), 1.0 - a * a)
    return r, i, sp, a, jnp.sqrt(one_m)


def _rows_before(x, tail8, k):
    e16 = jnp.concatenate([tail8, x[0:8, :]], axis=0)
    return jnp.concatenate([pltpu.roll(e16, k, 0)[8:16, :], pltpu.roll(x, k, 0)[8:, :]], axis=0)


def _rows_after(x, head8, k):
    tt = x.shape[0]
    e16 = jnp.concatenate([x[tt - 8:tt, :], head8], axis=0)
    return jnp.concatenate([pltpu.roll(x, tt - k, 0)[:tt - 8, :], pltpu.roll(e16, 16 - k, 0)[0:8, :]], axis=0)


def _scan_down(a, b, h0, a_s, b_s, c_s):
    tt, C = a.shape
    G, nch = tt // 8, C // LANES
    rin = lax.broadcasted_iota(jnp.int32, (tt, C), 0) & 7

    def in_group(v, s):
        return pltpu.roll(v.reshape(G, 8, C), s, 1).reshape(tt, C)

    A, B = a, b
    for s in (1, 2, 4):
        B = A * jnp.where(rin >= s, in_group(B, s), 0.0) + B
        A = A * jnp.where(rin >= s, in_group(A, s), 1.0)
    for j in range(nch):
        a_s[j] = A[:, j * LANES:(j + 1) * LANES]
        b_s[j] = B[:, j * LANES:(j + 1) * LANES]
    At = jnp.concatenate([a_s.at[j][pl.ds(7, G, stride=8), :] for j in range(nch)], axis=1)
    Bt = jnp.concatenate([b_s.at[j][pl.ds(7, G, stride=8), :] for j in range(nch)], axis=1)
    rowg = lax.broadcasted_iota(jnp.int32, (G, C), 0)
    s = 1
    while s < G:
        Bt = At * _shift_down(Bt, s, 0.0, rowg) + Bt
        At = At * _shift_down(At, s, 1.0, rowg)
        s *= 2
    hg = At * h0 + Bt
    cin = _shift_down(hg, 1, h0, rowg)
    for j in range(nch):
        for r in range(8):
            c_s.at[j][pl.ds(r, G, stride=8), :] = cin[:, j * LANES:(j + 1) * LANES]
    return A * jnp.concatenate([c_s[j] for j in range(nch)], axis=1) + B, hg[G - 1:G, :]


def _scan_up(c, g_in, g_next, a_s, b_s, c_s):
    tt, C = c.shape
    G, nch = tt // 8, C // LANES
    rin = lax.broadcasted_iota(jnp.int32, (tt, C), 0) & 7

    def in_group(v, s):
        return pltpu.roll(v.reshape(G, 8, C), 8 - s, 1).reshape(tt, C)

    Cc, Gv = c, g_in
    for s in (1, 2, 4):
        Gv = Gv + Cc * jnp.where(rin < 8 - s, in_group(Gv, s), 0.0)
        Cc = Cc * jnp.where(rin < 8 - s, in_group(Cc, s), 1.0)
    for j in range(nch):
        a_s[j] = Cc[:, j * LANES:(j + 1) * LANES]
        b_s[j] = Gv[:, j * LANES:(j + 1) * LANES]
    Ct = jnp.concatenate([a_s.at[j][pl.ds(0, G, stride=8), :] for j in range(nch)], axis=1)
    Gt = jnp.concatenate([b_s.at[j][pl.ds(0, G, stride=8), :] for j in range(nch)], axis=1)
    rowg = lax.broadcasted_iota(jnp.int32, (G, C), 0)
    s = 1
    while s < G:
        Gt = Gt + Ct * _shift_up(Gt, s, 0.0, rowg, G)
        Ct = Ct * _shift_up(Ct, s, 1.0, rowg, G)
        s *= 2
    gg = Gt + Ct * g_next
    cin = _shift_up(gg, 1, g_next, rowg, G)
    for j in range(nch):
        for r in range(8):
            c_s.at[j][pl.ds(r, G, stride=8), :] = cin[:, j * LANES:(j + 1) * LANES]
    return Gv + Cc * jnp.concatenate([c_s[j] for j in range(nch)], axis=1), gg[0:1, :]


def _lru_fwd(z_rx, conv_w, conv_b, wa, ba, wx, bx, lam, *, tt):
    S = z_rx.shape[0]
    n_t = S // tt
    BW = RNN_BLOCK_W
    W = LRU_NB * BW

    def body(x_ref, cw_ref, cb_ref, wa_ref, ba_ref, wx_ref, bx_ref, lam_ref, h_ref, tail, hc, a_s, b_s, c_s):
        t = pl.program_id(1)

        @pl.when(t == 0)
        def _():
            tail[...] = jnp.zeros((8, W), F32)
            hc[...] = jnp.zeros((8, W), F32)

        x = x_ref[...]
        before = tail[...]
        cw = cw_ref[...]
        xa = (cb_ref[...] + cw[3:4] * x + cw[2:3] * _rows_before(x, before, 1) + cw[1:2] * _rows_before(x, before, 2)
              + cw[0:1] * _rows_before(x, before, 3))
        tail[...] = x[tt - 8:tt, :]
        _r, i, _sp, a, mult = _lru_gates(xa, wa_ref, ba_ref, wx_ref, bx_ref, lam_ref[...])
        h, h_last = _scan_down(a, mult * (i * xa), hc[0:1, :], a_s, b_s, c_s)
        h_ref[...] = h.astype(BF16)
        hc[...] = jnp.broadcast_to(h_last, (8, W))

    blk = lambda n, t: (t, n)
    vec = pl.BlockSpec((1, W), lambda n, t: (0, n))
    mat = pl.BlockSpec((LRU_NB, BW, BW), lambda n, t: (n, 0, 0))
    bias = pl.BlockSpec((LRU_NB, 1, BW), lambda n, t: (n, 0, 0))
    row8 = pltpu.VMEM((8, W), F32)
    wide = pltpu.VMEM((LRU_NB, tt, LANES), F32)
    return pl.pallas_call(
        body, name="lru_fwd", grid=(RNN_BLOCKS // LRU_NB, n_t),
        in_specs=[pl.BlockSpec((tt, W), blk), pl.BlockSpec((CONV_WIDTH, W), lambda n, t: (0, n)), vec, mat, bias, mat,
                  bias, vec],
        out_specs=pl.BlockSpec((tt, W), blk), out_shape=jax.ShapeDtypeStruct((S, D_MODEL), BF16),
        scratch_shapes=[row8, row8, wide, wide, wide],
        compiler_params=_params(("parallel", "arbitrary")),
    )(z_rx, conv_w, conv_b, wa, ba, wx, bx, lam)


def _lru_bwd(z_rx, h, dh, dz, conv_w, conv_b, wa, wat, ba, wx, wxt, bx, lam, *, tt):
    S = z_rx.shape[0]
    n_t = S // tt
    BW = RNN_BLOCK_W
    W = LRU_NB * BW
    t8 = tt // 8

    def body(x_ref, xp_ref, h_ref, hp_ref, dh_ref, _dz_ref, cw_ref, cb_ref, wa_ref, wat_ref, ba_ref, wx_ref, wxt_ref,
             bx_ref, lam_ref, dx_ref, dwa_ref, dwx_ref, dba_ref, dbx_ref, dlam_ref, dcw_ref, dcb_ref, nxt, a_c, g_c, a_s,
             b_s, c_s):
        t = pl.program_id(1)
        tile = n_t - 1 - t

        @pl.when(t == 0)
        def _():
            a_c[...] = jnp.zeros((8, W), F32)
            g_c[...] = jnp.zeros((8, W), F32)
            nxt[...] = jnp.zeros((8, W), F32)
            dwa_ref[...] = jnp.zeros_like(dwa_ref)
            dwx_ref[...] = jnp.zeros_like(dwx_ref)
            dba_ref[...] = jnp.zeros_like(dba_ref)
            dbx_ref[...] = jnp.zeros_like(dbx_ref)
            dlam_ref[...] = jnp.zeros_like(dlam_ref)
            dcw_ref[...] = jnp.zeros_like(dcw_ref)
            dcb_ref[...] = jnp.zeros_like(dcb_ref)

        has_prev = (tile > 0).astype(F32)
        x = x_ref[...]
        before = xp_ref[...] * has_prev
        xm1, xm2, xm3 = _rows_before(x, before, 1), _rows_before(x, before, 2), _rows_before(x, before, 3)
        cw = cw_ref[...]
        xa = cb_ref[...] + cw[3:4] * x + cw[2:3] * xm1 + cw[1:2] * xm2 + cw[0:1] * xm3
        lam = lam_ref[...]
        r, i, sp, a, mult = _lru_gates(xa, wa_ref, ba_ref, wx_ref, bx_ref, lam)
        gated = i * xa
        h_prev = _rows_before(h_ref[...].astype(F32), hp_ref[8:16, :].astype(F32) * has_prev, 1)
        g, g_first = _scan_up(_rows_after(a, a_c[...], 1), dh_ref[...], g_c[0:1, :], a_s, b_s, c_s)
        a_c[...] = jnp.broadcast_to(a[0:1, :], (8, W))
        g_c[...] = jnp.broadcast_to(g_first, (8, W))
        dlog_a = g * h_prev * a - g * gated * (a * a) / mult
        dgated = g * mult
        di = dgated * xa
        dxa = dgated * i
        dr = dlog_a * (-LRU_C * sp)
        dlam_ref[...] += jnp.sum(dlog_a * (-LRU_C * r), axis=0, keepdims=True) * (-_sigmoid(-lam))
        dpr = dr * r * (1.0 - r)
        dpi = di * i * (1.0 - i)
        xab, dprb, dpib = xa.astype(BF16), dpr.astype(BF16), dpi.astype(BF16)
        tn_dims = (((0,), (0,)), ((), ()))
        back = []
        for j in range(LRU_NB):
            sl = slice(j * BW, (j + 1) * BW)
            dwa_ref[j] += lax.dot_general(xab[:, sl], dprb[:, sl], tn_dims, preferred_element_type=F32)
            dwx_ref[j] += lax.dot_general(xab[:, sl], dpib[:, sl], tn_dims, preferred_element_type=F32)
            dba_ref[j] += jnp.sum(dpr[:, sl], axis=0, keepdims=True)
            dbx_ref[j] += jnp.sum(dpi[:, sl], axis=0, keepdims=True)
            back.append(jnp.dot(dprb[:, sl], wat_ref[j], preferred_element_type=F32)
                        + jnp.dot(dpib[:, sl], wxt_ref[j], preferred_element_type=F32))
        dxa = dxa + jnp.concatenate(back, axis=1)
        after = nxt[...]
        dx = (cw[3:4] * dxa + cw[2:3] * _rows_after(dxa, after, 1) + cw[1:2] * _rows_after(dxa, after, 2)
              + cw[0:1] * _rows_after(dxa, after, 3))
        nxt[...] = dxa[0:8, :]
        dx_ref[...] = dx.astype(BF16)
        dcw_ref[3:4, :] += jnp.sum(dxa * x, axis=0, keepdims=True)
        dcw_ref[2:3, :] += jnp.sum(dxa * xm1, axis=0, keepdims=True)
        dcw_ref[1:2, :] += jnp.sum(dxa * xm2, axis=0, keepdims=True)
        dcw_ref[0:1, :] += jnp.sum(dxa * xm3, axis=0, keepdims=True)
        dcb_ref[...] += jnp.sum(dxa, axis=0, keepdims=True)

    blk = lambda n, t: (n_t - 1 - t, n)
    prev = lambda n, t: (jnp.maximum((n_t - 1 - t) * t8 - 1, 0), n)
    vec = pl.BlockSpec((1, W), lambda n, t: (0, n))
    mat = pl.BlockSpec((LRU_NB, BW, BW), lambda n, t: (n, 0, 0))
    bias = pl.BlockSpec((LRU_NB, 1, BW), lambda n, t: (n, 0, 0))
    cws = pl.BlockSpec((CONV_WIDTH, W), lambda n, t: (0, n))
    tile = pl.BlockSpec((tt, W), blk)
    prev8 = pl.BlockSpec((8, W), prev)
    prev16 = pl.BlockSpec((16, W), lambda n, t: (jnp.maximum((n_t - 1 - t) * (tt // 16) - 1, 0), n))
    row8 = pltpu.VMEM((8, W), F32)
    wide = pltpu.VMEM((LRU_NB, tt, LANES), F32)
    return pl.pallas_call(
        body, name="lru_bwd", grid=(RNN_BLOCKS // LRU_NB, n_t),
        in_specs=[tile, prev8, tile, prev16, tile, pl.BlockSpec(memory_space=pl.ANY), cws, vec, mat, mat, bias, mat, mat,
                  bias, vec],
        out_specs=[tile, mat, mat, bias, bias, vec, cws, vec], input_output_aliases={5: 0},
        out_shape=[jax.ShapeDtypeStruct(dz.shape, BF16),
                   jax.ShapeDtypeStruct((RNN_BLOCKS, BW, BW), F32), jax.ShapeDtypeStruct((RNN_BLOCKS, BW, BW), F32),
                   jax.ShapeDtypeStruct((RNN_BLOCKS, 1, BW), F32), jax.ShapeDtypeStruct((RNN_BLOCKS, 1, BW), F32),
                   jax.ShapeDtypeStruct((1, D_MODEL), F32),
                   jax.ShapeDtypeStruct((CONV_WIDTH, D_MODEL), F32), jax.ShapeDtypeStruct((1, D_MODEL), F32)],
        scratch_shapes=[row8, row8, row8, wide, wide, wide],
        compiler_params=_params(("parallel", "arbitrary")),
    )(z_rx, z_rx, h, h, dh, dz, conv_w, conv_b, wa, wat, ba, wx, wxt, bx, lam)


def _mla_proj(z_ckv, q_norm, kv_norm, w_uq, w_ukv, cos, sin, *, ts):
    S = z_ckv.shape[0]
    H = MLA_HEADS

    def body(c_ref, qn_ref, kn_ref, wq_ref, wkv_ref, cos_ref, sin_ref, q_ref, k_ref, v_ref):
        c = c_ref[...]
        cqn, _ = _rms_fwd(c[:, 0:Q_LORA], qn_ref[...])
        ckn, _ = _rms_fwd(c[:, Q_LORA:Q_LORA + KV_LORA], kn_ref[...])
        q = jnp.dot(cqn.astype(BF16), wq_ref[...], preferred_element_type=F32) * (ATTN_SCALE * LOG2E)
        kv = jnp.dot(ckn.astype(BF16), wkv_ref[...], preferred_element_type=F32)
        cos1, sin1 = cos_ref[...], sin_ref[...]
        cos8 = jnp.concatenate([cos1] * H, axis=1)
        sin8 = jnp.concatenate([sin1] * H, axis=1)
        qr = q[:, H * QK_NOPE:]
        lane8 = lax.broadcasted_iota(jnp.int32, qr.shape, 1)
        qr = qr * cos8 + _rot_half(qr, lane8) * sin8
        kr = c[:, Q_LORA + KV_LORA:]
        lane1 = lax.broadcasted_iota(jnp.int32, kr.shape, 1)
        kr = (kr * cos1 + _rot_half(kr, lane1) * sin1).astype(BF16)
        for h in range(H):
            q_ref[h, :, 0:QK_NOPE] = q[:, h * QK_NOPE:(h + 1) * QK_NOPE].astype(BF16)
            q_ref[h, :, QK_NOPE:] = qr[:, h * LANES:(h + 1) * LANES].astype(BF16)
            k_ref[h, :, 0:QK_NOPE] = kv[:, h * 2 * LANES:h * 2 * LANES + LANES].astype(BF16)
            k_ref[h, :, QK_NOPE:] = kr
            v_ref[h] = kv[:, h * 2 * LANES + LANES:(h + 1) * 2 * LANES].astype(BF16)

    full = lambda shape: pl.BlockSpec(shape, lambda i: (0,) * len(shape))
    return pl.pallas_call(
        body, name="mla_proj", grid=(S // ts,),
        in_specs=[pl.BlockSpec((ts, CKV_W), lambda i: (i, 0)), full((1, Q_LORA)), full((1, KV_LORA)),
                  full(w_uq.shape), full(w_ukv.shape), pl.BlockSpec((ts, LANES), lambda i: (i, 0)),
                  pl.BlockSpec((ts, LANES), lambda i: (i, 0))],
        out_specs=[pl.BlockSpec((H, ts, QK_PAD), lambda i: (0, i, 0)), pl.BlockSpec((H, ts, QK_PAD), lambda i: (0, i, 0)),
                   pl.BlockSpec((H, ts, V_HEAD), lambda i: (0, i, 0))],
        out_shape=[jax.ShapeDtypeStruct((H, S, QK_PAD), BF16), jax.ShapeDtypeStruct((H, S, QK_PAD), BF16),
                   jax.ShapeDtypeStruct((H, S, V_HEAD), BF16)],
        compiler_params=_params(("parallel",)),
    )(z_ckv, q_norm, kv_norm, w_uq, w_ukv, cos, sin)


def _mla_proj_bwd(z_ckv, dq, dk, dv, q_norm, kv_norm, w_uqt, w_ukvt, cos, sin, *, ts):
    S = z_ckv.shape[0]
    H = MLA_HEADS

    def body(c_ref, dq_ref, dk_ref, dv_ref, qn_ref, kn_ref, wqt_ref, wkvt_ref, cos_ref, sin_ref,
             dz_ref, dwq_ref, dwkv_ref, dqn_ref, dkn_ref):
        @pl.when(pl.program_id(0) == 0)
        def _():
            dwq_ref[...] = jnp.zeros_like(dwq_ref)
            dwkv_ref[...] = jnp.zeros_like(dwkv_ref)
            dqn_ref[...] = jnp.zeros_like(dqn_ref)
            dkn_ref[...] = jnp.zeros_like(dkn_ref)

        c = c_ref[...]
        cq, ck = c[:, 0:Q_LORA], c[:, Q_LORA:Q_LORA + KV_LORA]
        qn, kn = qn_ref[...], kn_ref[...]
        cqn, _ = _rms_fwd(cq, qn)
        ckn, _ = _rms_fwd(ck, kn)
        cos1, sin1 = cos_ref[...], sin_ref[...]
        lane1 = lax.broadcasted_iota(jnp.int32, cos1.shape, 1)

        def unrope(g):
            return g * cos1 - _rot_half(g * sin1, lane1)

        dq_all = jnp.concatenate([dq_ref[h, :, 0:QK_NOPE] for h in range(H)]
                                 + [unrope(dq_ref[h, :, QK_NOPE:]) for h in range(H)], axis=1)
        dq_all = (dq_all * ATTN_SCALE).astype(BF16)
        dkv_all = jnp.concatenate([p for h in range(H) for p in (dk_ref[h, :, 0:QK_NOPE], dv_ref[h])],
                                  axis=1).astype(BF16)
        dkr = dk_ref[0, :, QK_NOPE:].astype(F32)
        for h in range(1, H):
            dkr = dkr + dk_ref[h, :, QK_NOPE:].astype(F32)
        dkr = unrope(dkr)
        tn_dims = (((0,), (0,)), ((), ()))
        dwq_ref[...] += lax.dot_general(cqn.astype(BF16), dq_all, tn_dims, preferred_element_type=F32)
        dwkv_ref[...] += lax.dot_general(ckn.astype(BF16), dkv_all, tn_dims, preferred_element_type=F32)
        dcqn = jnp.dot(dq_all, wqt_ref[...], preferred_element_type=F32)
        dckn = jnp.dot(dkv_all, wkvt_ref[...], preferred_element_type=F32)
        dcq, dqn_rows = _rms_bwd(dcqn, cq, qn)
        dck, dkn_rows = _rms_bwd(dckn, ck, kn)
        dqn_ref[...] += jnp.sum(dqn_rows, axis=0, keepdims=True)
        dkn_ref[...] += jnp.sum(dkn_rows, axis=0, keepdims=True)
        dz_ref[:, 0:Q_LORA] = dcq.astype(BF16)
        dz_ref[:, Q_LORA:Q_LORA + KV_LORA] = dck.astype(BF16)
        dz_ref[:, Q_LORA + KV_LORA:] = dkr.astype(BF16)

    full = lambda shape: pl.BlockSpec(shape, lambda i: (0,) * len(shape))
    return pl.pallas_call(
        body, name="mla_proj_bwd", grid=(S // ts,),
        in_specs=[pl.BlockSpec((ts, CKV_W), lambda i: (i, 0)), pl.BlockSpec((H, ts, QK_PAD), lambda i: (0, i, 0)),
                  pl.BlockSpec((H, ts, QK_PAD), lambda i: (0, i, 0)), pl.BlockSpec((H, ts, V_HEAD), lambda i: (0, i, 0)),
                  full((1, Q_LORA)), full((1, KV_LORA)), full(w_uqt.shape), full(w_ukvt.shape),
                  pl.BlockSpec((ts, LANES), lambda i: (i, 0)), pl.BlockSpec((ts, LANES), lambda i: (i, 0))],
        out_specs=[pl.BlockSpec((ts, CKV_W), lambda i: (i, 0)), full((Q_LORA, w_uqt.shape[0])),
                   full((KV_LORA, w_ukvt.shape[0])), full((1, Q_LORA)), full((1, KV_LORA))],
        out_shape=[jax.ShapeDtypeStruct((S, CKV_W), BF16), jax.ShapeDtypeStruct((Q_LORA, w_uqt.shape[0]), F32),
                   jax.ShapeDtypeStruct((KV_LORA, w_ukvt.shape[0]), F32), jax.ShapeDtypeStruct((1, Q_LORA), F32),
                   jax.ShapeDtypeStruct((1, KV_LORA), F32)],
        compiler_params=_params(("arbitrary",)),
    )(z_ckv, dq, dk, dv, q_norm, kv_norm, w_uqt, w_ukvt, cos, sin)


NT_DIMS = (((1,), (1,)), ((), ()))
TN_DIMS = (((0,), (0,)), ((), ()))


def _attn_fwd(q, k, v, *, t, hb):
    H, S, _ = q.shape
    n = S // t
    pairs = [(i, j) for i in range(n) for j in range(i + 1)]
    qi = jnp.asarray(np.array([p[0] for p in pairs], np.int32))
    ki = jnp.asarray(np.array([p[1] for p in pairs], np.int32))

    def body(qi_ref, ki_ref, q_ref, k_ref, v_ref, o_ref, lse_ref, m_s, l_s, acc_s):
        p = pl.program_id(1)
        i, j = qi_ref[p], ki_ref[p]

        @pl.when(j == 0)
        def _():
            m_s[...] = jnp.full(m_s.shape, NEG, F32)
            l_s[...] = jnp.zeros(l_s.shape, F32)
            acc_s[...] = jnp.zeros(acc_s.shape, F32)

        def block(hh, r0, nr, nk, masked):
            rows = slice(r0, r0 + nr)
            s = lax.dot_general(q_ref[hh, rows, :], k_ref[hh, 0:nk, :], NT_DIMS, preferred_element_type=F32)
            if masked:
                row = lax.broadcasted_iota(jnp.int32, (nr, nk), 0) + r0
                col = lax.broadcasted_iota(jnp.int32, (nr, nk), 1)
                s = jnp.where(row >= col, s, NEG)
            chunks = nk // LANES
            mc = s[:, 0:LANES]
            for c in range(1, chunks):
                mc = jnp.maximum(mc, s[:, c * LANES:(c + 1) * LANES])
            m_prev = m_s[hh, rows, :]
            m_new = jnp.maximum(m_prev, jnp.max(mc, axis=1, keepdims=True))
            alpha = jnp.exp2(m_prev - m_new)
            pr = jnp.exp2(s - jnp.concatenate([m_new] * chunks, axis=1))
            ls = pr[:, 0:LANES]
            for c in range(1, chunks):
                ls = ls + pr[:, c * LANES:(c + 1) * LANES]
            l_s[hh, rows, :] = alpha * l_s[hh, rows, :] + ls
            acc_s[hh, rows, :] = alpha * acc_s[hh, rows, :] + jnp.dot(pr.astype(BF16), v_ref[hh, 0:nk, :],
                                                                      preferred_element_type=F32)
            m_s[hh, rows, :] = m_new

        def step(diagonal):
            for hh in range(hb):
                if diagonal:
                    block(hh, 0, t // 2, t // 2, True)
                    block(hh, t // 2, t // 2, t, True)
                else:
                    block(hh, 0, t, t, False)

        @pl.when(j < i)
        def _():
            step(False)

        @pl.when(j == i)
        def _():
            step(True)
            for hh in range(hb):
                l = jnp.sum(l_s[hh], axis=1, keepdims=True)
                o_ref[:, hh * V_HEAD:(hh + 1) * V_HEAD] = acc_s[hh] / l
                lse_ref[hh] = (m_s[hh] + jnp.log2(l)).T[0:1, :]

    grid_spec = pltpu.PrefetchScalarGridSpec(
        num_scalar_prefetch=2, grid=(H // hb, len(pairs)),
        in_specs=[pl.BlockSpec((hb, t, QK_PAD), lambda h, p, qi, ki: (h, qi[p], 0)),
                  pl.BlockSpec((hb, t, QK_PAD), lambda h, p, qi, ki: (h, ki[p], 0)),
                  pl.BlockSpec((hb, t, V_HEAD), lambda h, p, qi, ki: (h, ki[p], 0))],
        out_specs=[pl.BlockSpec((t, hb * V_HEAD), lambda h, p, qi, ki: (qi[p], h)),
                   pl.BlockSpec((hb, 1, t), lambda h, p, qi, ki: (h, 0, qi[p]))],
        scratch_shapes=[pltpu.VMEM((hb, t, LANES), F32), pltpu.VMEM((hb, t, LANES), F32),
                        pltpu.VMEM((hb, t, V_HEAD), F32)],
    )
    return pl.pallas_call(
        body, name="attn_fwd", grid_spec=grid_spec,
        out_shape=[jax.ShapeDtypeStruct((S, H * V_HEAD), F32), jax.ShapeDtypeStruct((H, 1, S), F32)],
        compiler_params=_params(("parallel", "arbitrary")),
    )(qi, ki, q, k, v)


def _attn_bwd(q, k, v, do, lse_row, delta_row, *, t):
    H, S, _ = q.shape
    n = S // t
    pairs = [(i, j) for j in range(n) for i in range(j, n)]
    qi = jnp.asarray(np.array([p[0] for p in pairs], np.int32))
    ki = jnp.asarray(np.array([p[1] for p in pairs], np.int32))

    def body(qi_ref, ki_ref, q_ref, k_ref, v_ref, do_ref, lse_ref, dl_ref, dq_ref, dk_ref, dv_ref, dk_s, dv_s, dq_s):
        p = pl.program_id(1)
        i, j = qi_ref[p], ki_ref[p]

        @pl.when(p == 0)
        def _():
            dq_s[...] = jnp.zeros_like(dq_s)

        def block(k0, nk, q0, nq, masked):
            qb, dob = q_ref[0, q0:q0 + nq, :], do_ref[q0:q0 + nq, :]
            kb, vb = k_ref[0, k0:k0 + nk, :], v_ref[0, k0:k0 + nk, :]
            st = lax.dot_general(kb, qb, NT_DIMS, preferred_element_type=F32)
            if masked:
                krow = lax.broadcasted_iota(jnp.int32, (nk, nq), 0) + k0
                qcol = lax.broadcasted_iota(jnp.int32, (nk, nq), 1) + q0
                st = jnp.where(krow <= qcol, st, NEG)
            pt = jnp.exp2(st - lse_ref[0][:, q0:q0 + nq])
            dvp = jnp.dot(pt.astype(BF16), dob, preferred_element_type=F32)
            dpt = lax.dot_general(vb, dob, NT_DIMS, preferred_element_type=F32)
            dst = (pt * (dpt - dl_ref[0][:, q0:q0 + nq])).astype(BF16)
            dkp = jnp.dot(dst, qb, preferred_element_type=F32)
            rows = pl.ds(pl.multiple_of(i * t + q0, LANES), nq)
            dq_s[rows, :] += lax.dot_general(dst, kb, TN_DIMS, preferred_element_type=F32)
            return dkp, dvp

        @pl.when(i == j)
        def _():
            half = t // 2
            dk_s[0:half, :], dv_s[0:half, :] = block(0, half, 0, t, True)
            dk_s[half:t, :], dv_s[half:t, :] = block(half, half, half, half, True)

        @pl.when(i != j)
        def _():
            dkp, dvp = block(0, t, 0, t, False)
            dk_s[...] += dkp
            dv_s[...] += dvp

        @pl.when(i == n - 1)
        def _():
            dk_ref[0] = (dk_s[...] * LN2).astype(BF16)
            dv_ref[0] = dv_s[...].astype(BF16)

        @pl.when(p == len(pairs) - 1)
        def _():
            dq_ref[0] = dq_s[...].astype(BF16)

    grid_spec = pltpu.PrefetchScalarGridSpec(
        num_scalar_prefetch=2, grid=(H, len(pairs)),
        in_specs=[pl.BlockSpec((1, t, QK_PAD), lambda h, p, qi, ki: (h, qi[p], 0)),
                  pl.BlockSpec((1, t, QK_PAD), lambda h, p, qi, ki: (h, ki[p], 0)),
                  pl.BlockSpec((1, t, V_HEAD), lambda h, p, qi, ki: (h, ki[p], 0)),
                  pl.BlockSpec((t, V_HEAD), lambda h, p, qi, ki: (qi[p], h)),
                  pl.BlockSpec((1, 1, t), lambda h, p, qi, ki: (h, 0, qi[p])),
                  pl.BlockSpec((1, 1, t), lambda h, p, qi, ki: (h, 0, qi[p]))],
        out_specs=[pl.BlockSpec((1, S, QK_PAD), lambda h, p, qi, ki: (h, 0, 0)),
                   pl.BlockSpec((1, t, QK_PAD), lambda h, p, qi, ki: (h, ki[p], 0)),
                   pl.BlockSpec((1, t, V_HEAD), lambda h, p, qi, ki: (h, ki[p], 0))],
        scratch_shapes=[pltpu.VMEM((t, QK_PAD), F32), pltpu.VMEM((t, V_HEAD), F32), pltpu.VMEM((S, QK_PAD), F32)],
    )
    return pl.pallas_call(
        body, name="attn_bwd", grid_spec=grid_spec,
        out_shape=[jax.ShapeDtypeStruct((H, S, QK_PAD), BF16), jax.ShapeDtypeStruct((H, S, QK_PAD), BF16),
                   jax.ShapeDtypeStruct((H, S, V_HEAD), BF16)],
        compiler_params=_params(("parallel", "arbitrary")),
    )(qi, ki, q, k, v, do, lse_row, delta_row)


def _merge_h1(h, z_gates, o, x, w_out, norm_mlp, *, ts):
    S = h.shape[0]
    D = D_MODEL

    def body(h_ref, rg_ref, ga_ref, gb_ref, o_ref, x_ref, w_ref, g_ref, m_ref, h1_ref, n2_ref):
        gl, _ = _gelu_and_grad(rg_ref[...].astype(F32))
        m = (_sigmoid(ga_ref[...].astype(F32)) * (h_ref[...].astype(F32) * gl)
             + _sigmoid(gb_ref[...].astype(F32)) * o_ref[...]).astype(BF16)
        m_ref[...] = m
        h1 = x_ref[...] + jnp.dot(m, w_ref[...], preferred_element_type=F32)
        h1_ref[...] = h1
        n2, _ = _rms_fwd(h1, g_ref[...])
        n2_ref[...] = n2.astype(BF16)

    col = lambda c: pl.BlockSpec((ts, D), lambda i: (i, c))
    fixed = lambda shape: pl.BlockSpec(shape, lambda i: (0, 0), pipeline_mode=pl.Buffered(1))
    return pl.pallas_call(
        body, name="merge_h1", grid=(S // ts,),
        in_specs=[col(0), col(0), col(1), col(2), col(0), col(0), fixed((D, D)), fixed((1, D))],
        out_specs=[col(0), col(0), col(0)],
        out_shape=[jax.ShapeDtypeStruct((S, D), BF16), jax.ShapeDtypeStruct((S, D), F32),
                   jax.ShapeDtypeStruct((S, D), BF16)],
        compiler_params=_params(("parallel",)),
    )(h, z_gates, z_gates, z_gates, o, x, w_out, norm_mlp)


def _my_place():
    return lax.axis_index("x"), lax.axis_index("y"), lax.axis_index("c")


def _all_gather(shards, *, name):
    n = len(shards)

    def body(*refs):
        x_refs, out_refs = refs[:n], refs[n:2 * n]
        send_sems, recv_sems, local_sems = refs[2 * n:]
        x, y, c = _my_place()
        me, sibling = (x, y, c), (x, y, 1 - c)
        chips = [(1 - x, y), (x, 1 - y), (1 - x, 1 - y)]

        def slot(a, px, py, pc):
            return out_refs[a].at[4 * px + 2 * py + pc]

        def copy(a, k, block, to, src=None):
            return pltpu.make_async_remote_copy(
                src_ref=slot(a, *block) if src is None else src, dst_ref=slot(a, *block),
                send_sem=send_sems.at[7 * a + k], recv_sem=recv_sems.at[7 * a + k], device_id=to, device_id_type=MESH)

        mine = [pltpu.make_async_copy(x_refs[a], slot(a, *me), local_sems.at[a]) for a in range(n)]
        for cp in mine:
            cp.start()
        first = []
        for a in range(n):
            first.append(copy(a, 0, me, sibling, src=x_refs[a]))
            first += [copy(a, 1 + j, me, (*chip, c), src=x_refs[a]) for j, chip in enumerate(chips)]
        for cp in first:
            cp.start()
        passed = []
        for a in range(n):
            for j, chip in enumerate(chips):
                copy(a, 1 + j, (*chip, c), me).wait_recv()
                fwd = copy(a, 4 + j, (*chip, c), sibling)
                fwd.start()
                passed.append(fwd)
        for a in range(n):
            copy(a, 0, sibling, me).wait_recv()
            for j, chip in enumerate(chips):
                copy(a, 4 + j, (*chip, 1 - c), me).wait_recv()
        for cp in first + passed:
            cp.wait_send()
        for cp in mine:
            cp.wait()

    hbm = pl.BlockSpec(memory_space=pl.ANY)
    return pl.pallas_call(
        body, name=name, out_shape=[jax.ShapeDtypeStruct((N_DEV, *s.shape), s.dtype) for s in shards],
        in_specs=[hbm] * n, out_specs=[hbm] * n,
        scratch_shapes=[pltpu.SemaphoreType.DMA((7 * n,)), pltpu.SemaphoreType.DMA((7 * n,)),
                        pltpu.SemaphoreType.DMA((n,))],
    )(*shards)


def _pushes(src_refs, land_refs, send_sems, recv_sems, slab_per_peer):
    x, y, c = _my_place()
    me = 4 * x + 2 * y + c
    copies = []
    for a in range(len(src_refs)):
        for k in range(1, N_DEV):
            px, py, pc = x ^ (k >> 2), y ^ ((k >> 1) & 1), c ^ (k & 1)
            src = src_refs[a].at[4 * px + 2 * py + pc] if slab_per_peer else src_refs[a]
            copies.append(pltpu.make_async_remote_copy(
                src_ref=src, dst_ref=land_refs[a].at[me], send_sem=send_sems.at[7 * a + k - 1],
                recv_sem=recv_sems.at[7 * a + k - 1], device_id=(px, py, pc), device_id_type=MESH))
    return copies


def _push_start(srcs, *, name, slab_per_peer):
    n = len(srcs)
    lands = [lax.empty((N_DEV, *(s.shape[1:] if slab_per_peer else s.shape)), s.dtype) for s in srcs]

    def body(*refs):
        src_refs, land_refs = refs[:n], refs[n:2 * n]
        send_sems, recv_sems, token = refs[2 * n], refs[2 * n + 1], refs[-1]
        for cp in _pushes(src_refs, land_refs, send_sems, recv_sems, slab_per_peer):
            cp.start()
        token[...] = jnp.zeros_like(token)

    hbm = pl.BlockSpec(memory_space=pltpu.HBM)
    sem = pl.BlockSpec(memory_space=pltpu.SEMAPHORE)
    out = pl.pallas_call(
        body, name=name,
        out_shape=(pltpu.SemaphoreType.DMA((7 * n,)), pltpu.SemaphoreType.DMA((7 * n,)),
                   *[pltpu.HBM(a.shape, a.dtype) for a in srcs + lands], jax.ShapeDtypeStruct((8, LANES), F32)),
        in_specs=[hbm] * (2 * n), out_specs=(sem, sem, *[hbm] * (2 * n), pl.BlockSpec(memory_space=pltpu.VMEM)),
        input_output_aliases={i: 2 + i for i in range(2 * n)},
        compiler_params=pltpu.CompilerParams(has_side_effects=pltpu.SideEffectType.DATAFLOW_SIDE_EFFECTING),
    )(*[pltpu.with_memory_space_constraint(a, pltpu.HBM) for a in srcs + lands])
    return out[0], out[1], list(out[2:2 + n]), list(out[2 + n:2 + 2 * n]), out[-1]


def _push_wait(send_sems, recv_sems, srcs, lands, after, *, name, slab_per_peer):
    n = len(srcs)

    def body(*refs):
        src_refs, land_refs = refs[:n], refs[n:2 * n]
        s_sems, r_sems = refs[2 * n], refs[2 * n + 1]
        for cp in _pushes(src_refs, land_refs, s_sems, r_sems, slab_per_peer):
            cp.wait_send()
            cp.wait_recv()

    hbm = pl.BlockSpec(memory_space=pltpu.HBM)
    sem = pl.BlockSpec(memory_space=pltpu.SEMAPHORE)
    out = pl.pallas_call(
        body, name=name, out_shape=tuple(pltpu.HBM(a.shape, a.dtype) for a in srcs + lands),
        in_specs=[hbm] * (2 * n) + [sem, sem, pl.BlockSpec(memory_space=pl.ANY)], out_specs=tuple([hbm] * (2 * n)),
        input_output_aliases={i: i for i in range(2 * n)},
        compiler_params=pltpu.CompilerParams(has_side_effects=pltpu.SideEffectType.DATAFLOW_SIDE_EFFECTING),
    )(*srcs, *lands, send_sems, recv_sems, after)
    return list(out[:n]), list(out[n:])


def _sum_parts(gp_ref, rows):
    g = gp_ref[0, 0:rows, :].astype(F32)
    for p in range(1, gp_ref.shape[0]):
        g = g + gp_ref[p, 0:rows, :].astype(F32)
    return g


def _adamw_update(w, m, v, g):
    m_new = ADAM_B1 * m + (1.0 - ADAM_B1) * g
    v_new = ADAM_B2 * v + (1.0 - ADAM_B2) * (g * g)
    m_hat = m_new / (1.0 - ADAM_B1 ** ADAM_STEP)
    v_hat = v_new / (1.0 - ADAM_B2 ** ADAM_STEP)
    return -ADAM_LR * (m_hat / (jnp.sqrt(v_hat) + ADAM_EPS) + ADAM_WD * w), m_new, v_new


def _adamw_many(ws, ms, vs, gparts, sums, *, name):
    n, k = len(ws), len(sums)

    def body(*refs):
        w_refs, m_refs, v_refs = refs[:n], refs[n:2 * n], refs[2 * n:3 * n]
        g_refs, s_refs, outs = refs[3 * n:4 * n], refs[4 * n:4 * n + k], refs[4 * n + k:]
        for a in range(n):
            g = _sum_parts(g_refs[a], w_refs[a].shape[0])
            d, m_new, v_new = _adamw_update(w_refs[a][...], m_refs[a][...], v_refs[a][...], g)
            for o_ref, val in zip(outs[4 * a:4 * a + 4], (g, d, m_new, v_new)):
                o_ref[...] = val
        for b in range(k):
            outs[4 * n + b][...] = _sum_parts(s_refs[b], s_refs[b].shape[1])

    out_shape = [jax.ShapeDtypeStruct(w.shape, F32) for w in ws for _ in range(4)]
    out_shape += [jax.ShapeDtypeStruct(s.shape[1:], F32) for s in sums]
    return pl.pallas_call(body, name=name, out_shape=out_shape, compiler_params=_params())(
        *ws, *ms, *vs, *gparts, *sums)


def _adamw(w, m, v, gparts, *, tr, name):
    R, C = w.shape
    n_parts = gparts.shape[0]

    def body(w_ref, m_ref, v_ref, gp_ref, g_ref, d_ref, nm_ref, nv_ref):
        g = _sum_parts(gp_ref, tr)
        d_ref[...], nm_ref[...], nv_ref[...] = _adamw_update(w_ref[...], m_ref[...], v_ref[...], g)
        g_ref[...] = g

    row = pl.BlockSpec((tr, C), lambda i: (i, 0))
    shp = jax.ShapeDtypeStruct((R, C), F32)
    return pl.pallas_call(
        body, name=name, grid=(R // tr,),
        in_specs=[row, row, row, pl.BlockSpec((n_parts, tr, C), lambda i: (0, i, 0))],
        out_specs=[row, row, row, row], out_shape=[shp, shp, shp, shp],
        compiler_params=_params(("parallel",)),
    )(w, m, v, gparts)


def _rope_tables(s):
    pos = jnp.arange(s, dtype=F32)
    inv_freq = 1.0 / (ROPE_THETA ** (jnp.arange(0, QK_ROPE, 2, dtype=F32) / QK_ROPE))
    per_lane = jnp.concatenate([inv_freq, inv_freq, jnp.zeros((LANES - QK_ROPE,), F32)])
    ang = pos[:, None] * per_lane[None, :]
    live = jnp.arange(LANES) < QK_ROPE
    return jnp.where(live, jnp.cos(ang), 0.0), jnp.where(live, jnp.sin(ang), 0.0)


def _pick(n, want):
    t = min(n, want)
    assert n % t == 0
    return t


def _local_step(x, target, wts, small, hooks):
    S = x.shape[0]
    H = MLA_HEADS
    ts = _pick(S, 1024)
    tm = _pick(S, 512)
    tm_wide = _pick(S, 1024)
    tk_s = _pick(S, 4096)
    tt = _pick(S, 512)
    ta = _pick(S, 1024)
    ts_proj = _pick(S, 512)
    ts_merge = _pick(S, 512)
    row = lambda v: v.reshape(1, -1)
    w_in = wts["w_in"]
    w_main = jnp.concatenate([w_in[:, 0:2048], w_in[:, 2624:4672]], axis=1)
    w_ckv = jnp.concatenate([w_in[:, 2048:2624], jnp.zeros((D_MODEL, CKV_W - 576), BF16)], axis=1)
    w_uq3 = wts["w_uq"].reshape(Q_LORA, H, QK_NOPE + QK_ROPE)
    w_uq_p = jnp.concatenate(
        [w_uq3[:, :, :QK_NOPE].reshape(Q_LORA, H * QK_NOPE),
         jnp.pad(w_uq3[:, :, QK_NOPE:], ((0, 0), (0, 0), (0, LANES - QK_ROPE))).reshape(Q_LORA, H * LANES)], axis=1)
    w_ukv = wts["w_ukv"]
    cos, sin = _rope_tables(S)
    conv_w, conv_b = small["conv_w"], row(small["conv_b"])
    wa, wx = small["lru_wa"].astype(BF16), small["lru_wx"].astype(BF16)
    wat, wxt = jnp.swapaxes(wa, 1, 2), jnp.swapaxes(wx, 1, 2)
    ba, bx = small["lru_ba"].reshape(RNN_BLOCKS, 1, RNN_BLOCK_W), small["lru_bx"].reshape(RNN_BLOCKS, 1, RNN_BLOCK_W)
    lam = row(small["lru_lambda"])
    q_norm, kv_norm = row(small["q_norm"]), row(small["kv_norm"])
    norm_mix, norm_mlp, norm_final = row(small["norm_mix"]), row(small["norm_mlp"]), row(small["norm_final"])

    xn = _rmsnorm_cast(x, norm_mix, ts=ts, name="norm_mix")
    ident = lambda acc: (acc,)
    z_rx, z_ckv = _mm(xn, w_main[:, :D_MODEL], name="z_rx_ckv", tm=tm_wide, tn=1024, tk=1024, outs=[("tile", F32)],
                      epilogue=ident, also=w_ckv)
    (z_gates,) = _mm(xn, w_main[:, D_MODEL:], name="z_gates", tm=tm_wide, tn=3 * D_MODEL, tk=1024, outs=[("tile", BF16)],
                     epilogue=ident)
    h = _lru_fwd(z_rx, conv_w, conv_b, wa, ba, wx, bx, lam, tt=tt)
    q, k, v = _mla_proj(z_ckv, q_norm, kv_norm, w_uq_p, w_ukv, cos, sin, ts=ts_proj)
    o, lse = _attn_fwd(q, k, v, t=ta, hb=2)
    w_out, w_up, w_down = hooks["weights_later"](o)
    merged, h1, n2 = _merge_h1(h, z_gates, o, x, w_out, norm_mlp, ts=ts_merge)

    def ep_up(acc):
        r = jnp.maximum(acc, 0.0)
        return r * r, r

    act, relu = _mm(n2, w_up, name="up", tm=tm_wide, tn=2048, tk=1024, outs=[("tile", BF16), ("tile", BF16)],
                    epilogue=ep_up)

    def ep_loss(acc, h1v, tgt, g):
        h2 = acc + h1v
        y, _ = _rms_fwd(h2, g)
        err = y - tgt
        loss_rows = 0.5 * jnp.mean(err * err, axis=-1, keepdims=True)
        dy = err * (1.0 / D_MODEL)
        dh2, dg_rows = _rms_bwd(dy, h2, g)
        lsum = jnp.sum(loss_rows, axis=0, keepdims=True)
        return dh2, dh2, jnp.sum(dg_rows, axis=0, keepdims=True), jnp.broadcast_to(lsum, (1, D_MODEL))

    dh2, dh2b, dnf_p, loss_p = _mm(
        act, w_down, name="down_loss", tm=tm, tn=1024, tk=D_FF,
        outs=[("tile", F32), ("tile", BF16), ("rowpart", F32), ("rowpart", F32)], epilogue=ep_loss,
        extras=[("tile", h1), ("tile", target), ("row", norm_final)])
    loss_part = jnp.sum(loss_p[:, 0, 0])
    d_norm_final = jnp.sum(dnf_p, axis=(0, 1))

    def ep_du(acc, r):
        return (acc * (2.0 * r.astype(F32)),)

    (du,) = _mm(dh2b, w_down, name="d_act", tb=True, tm=tm_wide, tn=2048, tk=1024, outs=[("tile", BF16)], epilogue=ep_du,
                extras=[("tile", relu)])

    def ep_dh1(acc, h1v, dh2v, g):
        dv, dg_rows = _rms_bwd(acc, h1v, g)
        dh1 = dh2v + dv
        return dh1, dh1, jnp.sum(dg_rows, axis=0, keepdims=True)

    dh1, dh1b, dnm_p = _mm(du, w_up, name="d_n2", tb=True, tm=tm, tn=1024, tk=D_FF,
                           outs=[("tile", F32), ("tile", BF16), ("rowpart", F32)], epilogue=ep_dh1,
                           extras=[("tile", h1), ("tile", dh2), ("row", norm_mlp)])
    d_norm_mlp = jnp.sum(dnm_p, axis=(0, 1))
    tn_mm = functools.partial(_mm, ta=True, tk=tk_s, outs=[("tile", BF16)], epilogue=ident)
    (d_w_down,) = tn_mm(act, dh2b, name="dw_down", tm=1024, tn=1024)
    (p_w_up,) = _mm(n2, du, name="dw_up", ta=True, tk=tk_s, tm=1024, tn=D_FF // N_DEV, outs=[("colshard", BF16)],
                    epilogue=ident)
    (d_w_out,) = tn_mm(merged, dh1b, name="dw_out", tm=1024, tn=1024)
    early = [d_w_out.reshape(N_DEV, -1, D_MODEL), p_w_up, d_w_down.reshape(N_DEV, -1, D_MODEL)]
    w_out = w_out + hooks["send"]("early", early)[0, 0].astype(BF16)


    def ep_dmerge(dm, hv, rg, ga, gb, ov):
        hv, rg, ga, gb = hv.astype(F32), rg.astype(F32), ga.astype(F32), gb.astype(F32)
        gl, dgl = _gelu_and_grad(rg)
        sa, sb = _sigmoid(ga), _sigmoid(gb)
        ya = hv * gl
        dya = dm * sa
        do = dm * sb
        dga = dm * ya * sa * (1.0 - sa)
        dgb = dm * ov * sb * (1.0 - sb)
        dh = dya * gl
        drg = dya * hv * dgl
        dov = do * ov
        lane = lax.broadcasted_iota(jnp.int32, (dm.shape[0], LANES), 1)
        delta = jnp.zeros((dm.shape[0], LANES), F32)
        for hh in range(H):
            dsum = jnp.sum(dov[:, hh * V_HEAD:(hh + 1) * V_HEAD], axis=1, keepdims=True)
            delta = jnp.where(lane == hh, dsum, delta)
        return dh, jnp.concatenate([drg, dga, dgb], axis=1), do, delta

    dh_lru, dz_part, do, delta_w = _mm(
        dh1b, w_out, name="d_merge", tb=True, tm=ts_merge, tn=1024, tk=1024,
        outs=[("tile", F32), ("cols", BF16, 4 * D_MODEL, D_MODEL), ("tile", BF16), ("side", F32)],
        epilogue=ep_dmerge,
        extras=[("tile", h), ("tilecol", z_gates, 0), ("tilecol", z_gates, 1), ("tilecol", z_gates, 2), ("tile", o)])
    delta_row = delta_w[:, :H].T.reshape(H, 1, S)
    lse_row = lse

    dq, dk, dv = _attn_bwd(q, k, v, do, lse_row, delta_row, t=ta)
    dz_ckv, d_w_uq_p, d_w_ukv, d_q_norm, d_kv_norm = _mla_proj_bwd(
        z_ckv, dq, dk, dv, q_norm, kv_norm, w_uq_p.T, w_ukv.T, cos, sin, ts=ts_proj)
    d_w_uq = jnp.concatenate(
        [d_w_uq_p[:, :H * QK_NOPE].reshape(Q_LORA, H, QK_NOPE),
         d_w_uq_p[:, H * QK_NOPE:].reshape(Q_LORA, H, LANES)[:, :, :QK_ROPE]], axis=2).reshape(Q_LORA, -1)

    dz_main, d_wa, d_wx, d_ba, d_bx, d_lam, d_conv_w, d_conv_b = _lru_bwd(
        z_rx, h, dh_lru, dz_part, conv_w, conv_b, wa, wat, ba, wx, wxt, bx, lam, tt=tt)

    (d_w_main,) = tn_mm(xn, dz_main, name="dw_main", tm=1024, tn=1024)
    (d_w_ckv,) = tn_mm(xn, dz_ckv, name="dw_ckv", tm=1024, tn=CKV_W)
    d_w_in = jnp.concatenate([d_w_main[:, 0:2048], d_w_ckv[:, 0:576], d_w_main[:, 2048:4096]], axis=1)

    def col_parts(full):
        r = full.shape[0]
        return jnp.transpose(full.astype(BF16).reshape(r, N_DEV, -1), (1, 0, 2))

    late = [col_parts(d_w_in), col_parts(d_w_uq), col_parts(d_w_ukv)]
    norm_mix = norm_mix + hooks["send"]("late", late)[0, 0]

    def ep_dx(acc, xv, dh1v, g):
        dv, dg_rows = _rms_bwd(acc, xv, g)
        return dh1v + dv, jnp.sum(dg_rows, axis=0, keepdims=True)

    grad_x, dnx_p = _mm(dz_main, w_main, name="dx", tb=True, tm=tm, tn=1024, tk=4 * D_MODEL,
                        outs=[("tile", F32), ("rowpart", F32)], epilogue=ep_dx, more=(dz_ckv, w_ckv),
                        extras=[("tile", x), ("tile", dh1), ("row", norm_mix)])
    d_norm_mix = jnp.sum(dnx_p, axis=(0, 1))
    sm = {"norm_mix": d_norm_mix, "conv_w": d_conv_w, "conv_b": d_conv_b.reshape(-1), "lru_wa": d_wa,
          "lru_ba": d_ba.reshape(RNN_BLOCKS, RNN_BLOCK_W), "lru_wx": d_wx, "lru_bx": d_bx.reshape(RNN_BLOCKS, RNN_BLOCK_W),
          "lru_lambda": d_lam.reshape(-1), "q_norm": d_q_norm.reshape(-1), "kv_norm": d_kv_norm.reshape(-1),
          "norm_mlp": d_norm_mlp, "norm_final": d_norm_final}
    return loss_part, grad_x, sm


SMALL = ("norm_mix", "conv_b", "lru_wa", "lru_ba", "lru_wx", "lru_bx", "lru_lambda", "q_norm", "kv_norm", "norm_mlp",
         "norm_final")
WEIGHTS = ("norm_mix", "w_in", "conv_w", "conv_b", "lru_wa", "lru_ba", "lru_wx", "lru_bx", "lru_lambda", "q_norm", "w_uq",
           "kv_norm", "w_ukv", "w_out", "norm_mlp", "w_up", "w_down", "norm_final")
ADAM_TILE_ROWS = {"w_in": 256, "w_uq": 128, "w_ukv": 128, "w_out": 64, "w_up": 256, "w_down": 128}
CONV_ROWS = N_DEV * 8


def _rows(a):
    return a.reshape(-1, LANES)


def _pad_rows(a, mult):
    r = a.shape[-2]
    pad = (-r) % mult
    if pad == 0:
        return a
    cfg = [(0, 0)] * (a.ndim - 2) + [(0, pad), (0, 0)]
    return jnp.pad(a, cfg)


def _cols_from_shards(g):
    return jnp.transpose(g, (1, 0, 2)).reshape(g.shape[1], -1)


def kernel(x, norm_mix, w_in, conv_w, conv_b, lru_wa, lru_ba, lru_wx, lru_bx, lru_lambda, q_norm, w_uq, kv_norm, w_ukv, w_out, norm_mlp, w_up, w_down, norm_final, loss_target, m_norm_mix, m_w_in, m_conv_w, m_conv_b, m_lru_wa, m_lru_ba, m_lru_wx, m_lru_bx, m_lru_lambda, m_q_norm, m_w_uq, m_kv_norm, m_w_ukv, m_w_out, m_norm_mlp, m_w_up, m_w_down, m_norm_final, v_norm_mix, v_w_in, v_conv_w, v_conv_b, v_lru_wa, v_lru_ba, v_lru_wx, v_lru_bx, v_lru_lambda, v_q_norm, v_w_uq, v_kv_norm, v_w_ukv, v_w_out, v_norm_mlp, v_w_up, v_w_down, v_norm_final):
    W = dict(norm_mix=norm_mix, w_in=w_in, conv_w=conv_w, conv_b=conv_b, lru_wa=lru_wa, lru_ba=lru_ba, lru_wx=lru_wx,
             lru_bx=lru_bx, lru_lambda=lru_lambda, q_norm=q_norm, w_uq=w_uq, kv_norm=kv_norm, w_ukv=w_ukv, w_out=w_out,
             norm_mlp=norm_mlp, w_up=w_up, w_down=w_down, norm_final=norm_final)
    M = dict(norm_mix=m_norm_mix, w_in=m_w_in, conv_w=m_conv_w, conv_b=m_conv_b, lru_wa=m_lru_wa, lru_ba=m_lru_ba,
             lru_wx=m_lru_wx, lru_bx=m_lru_bx, lru_lambda=m_lru_lambda, q_norm=m_q_norm, w_uq=m_w_uq, kv_norm=m_kv_norm,
             w_ukv=m_w_ukv, w_out=m_w_out, norm_mlp=m_norm_mlp, w_up=m_w_up, w_down=m_w_down, norm_final=m_norm_final)
    V = dict(norm_mix=v_norm_mix, w_in=v_w_in, conv_w=v_conv_w, conv_b=v_conv_b, lru_wa=v_lru_wa, lru_ba=v_lru_ba,
             lru_wx=v_lru_wx, lru_bx=v_lru_bx, lru_lambda=v_lru_lambda, q_norm=v_q_norm, w_uq=v_w_uq, kv_norm=v_kv_norm,
             w_ukv=v_w_ukv, w_out=v_w_out, norm_mlp=v_norm_mlp, w_up=v_w_up, w_down=v_w_down, norm_final=v_norm_final)
    me = 4 * lax.axis_index("x") + 2 * lax.axis_index("y") + lax.axis_index("c")

    first, later = ("w_in", "w_uq", "w_ukv"), ("w_out", "w_up", "w_down")
    got = _all_gather([W[n].astype(BF16) for n in first] + [_pad_rows(conv_w, 8)], name="gather_weights")
    wts = {"w_in": _cols_from_shards(got[0]), "w_uq": _cols_from_shards(got[1]), "w_ukv": _cols_from_shards(got[2])}
    w_send, w_recv, w_src, w_land, zeros = _push_start([W[n].astype(BF16) for n in later], name="gather_later_start",
                                                       slab_per_peer=False)
    small = {n: W[n] for n in SMALL}
    small["conv_w"] = _cols_from_shards(got[3][:, :CONV_WIDTH])
    small["norm_mix"] = norm_mix + zeros[0, 0]

    def with_own_slab(land, mine):
        return lax.dynamic_update_slice(land, mine, (me, 0, 0))

    def weights_later(after):
        srcs, lands = _push_wait(w_send, w_recv, w_src, w_land, after, name="gather_later_wait", slab_per_peer=False)
        w_out_g, w_up_g, w_down_g = [with_own_slab(l, s[None]) for l, s in zip(lands, srcs)]
        return w_out_g.reshape(-1, D_MODEL), _cols_from_shards(w_up_g), w_down_g.reshape(-1, D_MODEL)

    sent = {}
    G, Dl, NM, NV = {}, {}, {}, {}

    def finish(group, names, after):
        s_sems, r_sems, srcs, lands, _ = sent[group]
        srcs, lands = _push_wait(s_sems, r_sems, srcs, lands, after, name="exchange_" + group + "_wait",
                                 slab_per_peer=True)
        for n, src, land in zip(names, srcs, lands):
            parts = with_own_slab(land, lax.dynamic_slice(src, (me, 0, 0), (1, *src.shape[1:])))
            G[n], Dl[n], NM[n], NV[n] = _adamw(W[n], M[n], V[n], parts, tr=ADAM_TILE_ROWS[n], name="adamw_" + n)

    def send(group, parts):
        sent[group] = _push_start(parts, name="exchange_" + group + "_start", slab_per_peer=True)
        zeros = sent[group][4]
        if group == "late":
            finish("early", later, zeros)
            zeros = zeros + 0.0 * (Dl["w_out"][0:8, 0:LANES] + Dl["w_up"][0:8, 0:LANES] + Dl["w_down"][0:8, 0:LANES])
        return zeros

    loss_part, grad_x, g_small = _local_step(x[0], loss_target[0], wts, small,
                                              {"weights_later": weights_later, "send": send})
    finish("late", first, grad_x)

    conv_rows = _pad_rows(jnp.transpose(g_small["conv_w"].reshape(CONV_WIDTH, N_DEV, LANES), (1, 0, 2)), 8)
    loss_rows = jnp.zeros((8, LANES), F32).at[0, 0].set(loss_part)
    as_sent = lambda n: g_small[n].astype(BF16) if n in ("lru_wa", "lru_wx") else g_small[n]
    gathered = _all_gather([_pad_rows(_rows(as_sent(n)), 8) for n in SMALL]
                           + [conv_rows.reshape(CONV_ROWS, LANES), loss_rows], name="gather_small")
    k = len(SMALL)
    outs = _adamw_many([_rows(W[n]) for n in SMALL], [_rows(M[n]) for n in SMALL], [_rows(V[n]) for n in SMALL],
                       gathered[:k], gathered[k:], name="adamw_small")
    for j, n in enumerate(SMALL):
        for out, o in zip((G, Dl, NM, NV), outs[4 * j:4 * j + 4]):
            out[n] = o.reshape(W[n].shape)
    conv_sum, loss_sum = outs[4 * k:]
    loss = loss_sum[0, 0]

    g_conv = lax.dynamic_slice(conv_sum, (me * 8, 0), (8, LANES))
    conv_out = _adamw(_pad_rows(conv_w, 8), _pad_rows(m_conv_w, 8), _pad_rows(v_conv_w, 8), g_conv[None], tr=8,
                      name="adamw_conv_w")
    for out, pk in zip((G, Dl, NM, NV), conv_out):
        out["conv_w"] = pk[:CONV_WIDTH]
    return (loss, grad_x[None], *[G[n] for n in WEIGHTS], *[Dl[n] for n in WEIGHTS], *[NM[n] for n in WEIGHTS],
            *[NV[n] for n in WEIGHTS])
```

```python
import functools

import numpy as np
import jax
import jax.numpy as jnp
from jax import lax
from jax.experimental import pallas as pl
from jax.experimental.pallas import tpu as pltpu

F32 = jnp.float32
BF16 = jnp.bfloat16
MESH = pl.DeviceIdType.MESH

D_MODEL = 1024
N_DEV = 8
LANES = 128
RNN_BLOCKS = 8
RNN_BLOCK_W = 128
CONV_WIDTH = 4
LRU_C = 8.0
MLA_HEADS = 8
QK_NOPE = 128
QK_ROPE = 64
V_HEAD = 128
QK_PAD = 256
Q_LORA = 256
KV_LORA = 256
CKV_W = 640
ROPE_THETA = 10000.0
D_FF = 4096
EPS = 1e-6
ATTN_SCALE = (QK_NOPE + QK_ROPE) ** -0.5
LOG2E = 1.4426950408889634
LN2 = 0.6931471805599453
NEG = -1e30

ADAM_LR = 0.001
ADAM_B1 = 0.9
ADAM_B2 = 0.999
ADAM_EPS = 1e-08
ADAM_WD = 0.01
ADAM_STEP = 10

VMEM_LIMIT = 56 * 1024 * 1024


def _params(sem=None):
    return pltpu.CompilerParams(dimension_semantics=sem, vmem_limit_bytes=VMEM_LIMIT)


def _sigmoid(v):
    return 1.0 / (1.0 + jnp.exp(-v))


def _softplus(y):
    e = jnp.exp(-jnp.abs(y))
    u = 1.0 + e
    d = u - 1.0
    l1p = jnp.where(d == 0.0, e, jnp.log(u) * e / jnp.where(d == 0.0, 1.0, d))
    return jnp.maximum(y, 0.0) + l1p


_GELU_K = 0.7978845608028654
_GELU_C = 0.044715


def _gelu_and_grad(v):
    t = jnp.tanh(_GELU_K * (v + _GELU_C * v * v * v))
    g = 0.5 * v * (1.0 + t)
    dg = 0.5 * (1.0 + t) + 0.5 * v * (1.0 - t * t) * _GELU_K * (1.0 + 3.0 * _GELU_C * v * v)
    return g, dg


def _rms_fwd(v, g):
    rstd = lax.rsqrt(jnp.mean(v * v, axis=-1, keepdims=True) + EPS)
    return v * rstd * g, rstd


def _rms_bwd(dy, v, g):
    rstd = lax.rsqrt(jnp.mean(v * v, axis=-1, keepdims=True) + EPS)
    vh = v * rstd
    dvh = dy * g
    dv = rstd * (dvh - vh * jnp.mean(dvh * vh, axis=-1, keepdims=True))
    return dv, dy * vh


def _shift_down(v, s, fill, row):
    return jnp.where(row >= s, pltpu.roll(v, s, 0), fill)


def _shift_up(v, s, fill, row, n):
    return jnp.where(row < n - s, pltpu.roll(v, n - s, 0), fill)


def _rot_half(v, lane):
    n = v.shape[-1]
    l = lane & (LANES - 1)
    up = pltpu.roll(v, n - QK_ROPE // 2, 1)
    dn = pltpu.roll(v, QK_ROPE // 2, 1)
    return jnp.where(l < QK_ROPE // 2, -up, jnp.where(l < QK_ROPE, dn, 0.0))


def _mm(a, b, *, name, tm, tn, tk, outs, epilogue, extras=(), ta=False, tb=False, more=None, also=None):
    assert not (ta and tb)
    if ta:
        K, M = a.shape
    else:
        M, K = a.shape
    if tb:
        N, K2 = b.shape
    else:
        K2, N = b.shape
    assert K == K2 and M % tm == 0 and N % tn == 0 and K % tk == 0, (name, a.shape, b.shape)
    n_i, n_j, n_k = M // tm, N // tn, K // tk
    n_ex, n_out = len(extras), len(outs)
    n_more = 0 if more is None else 2
    n_also = 0 if also is None else 1
    assert more is None or (n_k == 1 and not ta)
    assert also is None or (n_k == 1 and n_j == 1 and not ta and more is None)

    def body(*refs):
        a_ref, b_ref = refs[0], refs[1]
        ex_refs = refs[2 + n_more:2 + n_more + n_ex]
        first_out = 2 + n_more + n_ex + n_also
        out_refs = refs[first_out:first_out + n_out]
        if ta:
            part = lax.dot_general(a_ref[...], b_ref[...], (((0,), (0,)), ((), ())), preferred_element_type=F32)
        elif tb:
            part = lax.dot_general(a_ref[...], b_ref[...], (((1,), (1,)), ((), ())), preferred_element_type=F32)
        else:
            part = jnp.dot(a_ref[...], b_ref[...], preferred_element_type=F32)
        if more is not None:
            part = part + lax.dot_general(refs[2][...], refs[3][...], (((1,), (1,)), ((), ())),
                                          preferred_element_type=F32)

        def finish(acc):
            res = epilogue(acc, *[r[...] for r in ex_refs])
            for o_ref, r, spec in zip(out_refs, res, outs):
                if spec[0] == "cols":
                    o_ref[:, spec[3]:spec[3] + r.shape[1]] = r.astype(o_ref.dtype)
                else:
                    o_ref[...] = r.astype(o_ref.dtype).reshape(o_ref.shape)

        if also is not None:
            refs[first_out + n_out][...] = jnp.dot(a_ref[...], refs[first_out - 1][...], preferred_element_type=F32)
        if n_k == 1:
            finish(part)
        else:
            acc_ref = refs[-1]
            k = pl.program_id(2)

            @pl.when(k == 0)
            def _():
                acc_ref[...] = part

            @pl.when(k > 0)
            def _():
                acc_ref[...] += part

            @pl.when(k == n_k - 1)
            def _():
                finish(acc_ref[...])

    a_spec = pl.BlockSpec((tk, tm), lambda j, i, k: (k, i)) if ta else pl.BlockSpec((tm, tk), lambda j, i, k: (i, k))
    b_once = dict(pipeline_mode=pl.Buffered(1)) if (n_j == 1 and n_k == 1) else {}
    if tb:
        in_specs = [a_spec, pl.BlockSpec((tn, tk), lambda j, i, k: (j, k), **b_once)]
    else:
        in_specs = [a_spec, pl.BlockSpec((tk, tn), lambda j, i, k: (k, j), **b_once)]
    if more is not None:
        k2 = more[0].shape[1]
        in_specs += [pl.BlockSpec((tm, k2), lambda j, i, k: (i, 0)), pl.BlockSpec((tn, k2), lambda j, i, k: (j, 0), **b_once)]
    for ex in extras:
        kind = ex[0]
        if kind == "tile":
            in_specs.append(pl.BlockSpec((tm, tn), lambda j, i, k: (i, j)))
        elif kind == "tilecol":
            assert n_j == 1
            in_specs.append(pl.BlockSpec((tm, tn), functools.partial(lambda c, j, i, k: (i, c), ex[2])))
        else:
            in_specs.append(pl.BlockSpec((1, tn), lambda j, i, k: (0, j)))
    out_specs, out_shape = [], []
    for kind, dt, *rest in outs:
        if kind == "tile":
            out_specs.append(pl.BlockSpec((tm, tn), lambda j, i, k: (i, j)))
            out_shape.append(jax.ShapeDtypeStruct((M, N), dt))
        elif kind == "colshard":
            out_specs.append(pl.BlockSpec((1, tm, tn), lambda j, i, k: (j, i, 0)))
            out_shape.append(jax.ShapeDtypeStruct((n_j, M, tn), dt))
        elif kind == "cols":
            assert n_j == 1
            out_specs.append(pl.BlockSpec((tm, rest[0]), lambda j, i, k: (i, 0)))
            out_shape.append(jax.ShapeDtypeStruct((M, rest[0]), dt))
        elif kind == "side":
            assert n_j == 1
            out_specs.append(pl.BlockSpec((tm, LANES), lambda j, i, k: (i, 0)))
            out_shape.append(jax.ShapeDtypeStruct((M, LANES), dt))
        else:
            out_specs.append(pl.BlockSpec((1, 1, tn), lambda j, i, k: (i, 0, j)))
            out_shape.append(jax.ShapeDtypeStruct((n_i, 1, N), dt))
    scratch = [pltpu.VMEM((tm, tn), F32)] if n_k > 1 else []
    if also is not None:
        in_specs.append(pl.BlockSpec(also.shape, lambda j, i, k: (0, 0), **b_once))
        out_specs.append(pl.BlockSpec((tm, also.shape[1]), lambda j, i, k: (i, 0)))
        out_shape.append(jax.ShapeDtypeStruct((M, also.shape[1]), F32))
    return pl.pallas_call(
        body, name=name, grid=(n_j, n_i, n_k), in_specs=in_specs, out_specs=out_specs, out_shape=out_shape,
        scratch_shapes=scratch, compiler_params=_params(("parallel", "parallel", "arbitrary")),
    )(a, b, *(more or ()), *[ex[1] for ex in extras], *([] if also is None else [also]))


def _rmsnorm_cast(x, g, *, ts, name):
    S, D = x.shape

    def body(x_ref, g_ref, o_ref):
        y, _ = _rms_fwd(x_ref[...], g_ref[...])
        o_ref[...] = y.astype(BF16)

    return pl.pallas_call(
        body, name=name, grid=(S // ts,),
        in_specs=[pl.BlockSpec((ts, D), lambda i: (i, 0)), pl.BlockSpec((1, D), lambda i: (0, 0))],
        out_specs=pl.BlockSpec((ts, D), lambda i: (i, 0)), out_shape=jax.ShapeDtypeStruct((S, D), BF16),
        compiler_params=_params(("parallel",)),
    )(x, g)


LRU_NB = 4


def _lru_gates(xa, wa_ref, ba_ref, wx_ref, bx_ref, lam):
    xab = xa.astype(BF16)
    W = RNN_BLOCK_W
    rs, is_ = [], []
    for j in range(LRU_NB):
        xj = xab[:, j * W:(j + 1) * W]
        rs.append(_sigmoid(jnp.dot(xj, wa_ref[j], preferred_element_type=F32) + ba_ref[j]))
        is_.append(_sigmoid(jnp.dot(xj, wx_ref[j], preferred_element_type=F32) + bx_ref[j]))
    r = jnp.concatenate(rs, axis=1)
    i = jnp.concatenate(is_, axis=1)
    sp = _softplus(-lam)
    log_a = (-LRU_C * r) * sp
    a = jnp.exp(log_a)
    y = 2.0 * log_a
    one_m = jnp.where(y > -0.01, -y * (1.0 + 0.5 * y * (1.0 + y * (1.0 / 3.0))), 1.0 - a * a)
    return r, i, sp, a, jnp.sqrt(one_m)


def _rows_before(x, tail8, k):
    e16 = jnp.concatenate([tail8, x[0:8, :]], axis=0)
    return jnp.concatenate([pltpu.roll(e16, k, 0)[8:16, :], pltpu.roll(x, k, 0)[8:, :]], axis=0)


def _rows_after(x, head8, k):
    tt = x.shape[0]
    e16 = jnp.concatenate([x[tt - 8:tt, :], head8], axis=0)
    return jnp.concatenate([pltpu.roll(x, tt - k, 0)[:tt - 8, :], pltpu.roll(e16, 16 - k, 0)[0:8, :]], axis=0)


def _scan_down(a, b, h0, a_s, b_s, c_s):
    tt, C = a.shape
    G, nch = tt // 8, C // LANES
    rin = lax.broadcasted_iota(jnp.int32, (tt, C), 0) & 7

    def in_group(v, s):
        return pltpu.roll(v.reshape(G, 8, C), s, 1).reshape(tt, C)

    A, B = a, b
    for s in (1, 2, 4):
        B = A * jnp.where(rin >= s, in_group(B, s), 0.0) + B
        A = A * jnp.where(rin >= s, in_group(A, s), 1.0)
    for j in range(nch):
        a_s[j] = A[:, j * LANES:(j + 1) * LANES]
        b_s[j] = B[:, j * LANES:(j + 1) * LANES]
    At = jnp.concatenate([a_s.at[j][pl.ds(7, G, stride=8), :] for j in range(nch)], axis=1)
    Bt = jnp.concatenate([b_s.at[j][pl.ds(7, G, stride=8), :] for j in range(nch)], axis=1)
    rowg = lax.broadcasted_iota(jnp.int32, (G, C), 0)
    s = 1
    while s < G:
        Bt = At * _shift_down(Bt, s, 0.0, rowg) + Bt
        At = At * _shift_down(At, s, 1.0, rowg)
        s *= 2
    hg = At * h0 + Bt
    cin = _shift_down(hg, 1, h0, rowg)
    for j in range(nch):
        for r in range(8):
            c_s.at[j][pl.ds(r, G, stride=8), :] = cin[:, j * LANES:(j + 1) * LANES]
    return A * jnp.concatenate([c_s[j] for j in range(nch)], axis=1) + B, hg[G - 1:G, :]


def _scan_up(c, g_in, g_next, a_s, b_s, c_s):
    tt, C = c.shape
    G, nch = tt // 8, C // LANES
    rin = lax.broadcasted_iota(jnp.int32, (tt, C), 0) & 7

    def in_group(v, s):
        return pltpu.roll(v.reshape(G, 8, C), 8 - s, 1).reshape(tt, C)

    Cc, Gv = c, g_in
    for s in (1, 2, 4):
        Gv = Gv + Cc * jnp.where(rin < 8 - s, in_group(Gv, s), 0.0)
        Cc = Cc * jnp.where(rin < 8 - s, in_group(Cc, s), 1.0)
    for j in range(nch):
        a_s[j] = Cc[:, j * LANES:(j + 1) * LANES]
        b_s[j] = Gv[:, j * LANES:(j + 1) * LANES]
    Ct = jnp.concatenate([a_s.at[j][pl.ds(0, G, stride=8), :] for j in range(nch)], axis=1)
    Gt = jnp.concatenate([b_s.at[j][pl.ds(0, G, stride=8), :] for j in range(nch)], axis=1)
    rowg = lax.broadcasted_iota(jnp.int32, (G, C), 0)
    s = 1
    while s < G:
        Gt = Gt + Ct * _shift_up(Gt, s, 0.0, rowg, G)
        Ct = Ct * _shift_up(Ct, s, 1.0, rowg, G)
        s *= 2
    gg = Gt + Ct * g_next
    cin = _shift_up(gg, 1, g_next, rowg, G)
    for j in range(nch):
        for r in range(8):
            c_s.at[j][pl.ds(r, G, stride=8), :] = cin[:, j * LANES:(j + 1) * LANES]
    return Gv + Cc * jnp.concatenate([c_s[j] for j in range(nch)], axis=1), gg[0:1, :]


def _lru_fwd(z_rx, conv_w, conv_b, wa, ba, wx, bx, lam, *, tt):
    S = z_rx.shape[0]
    n_t = S // tt
    BW = RNN_BLOCK_W
    W = LRU_NB * BW

    def body(x_ref, cw_ref, cb_ref, wa_ref, ba_ref, wx_ref, bx_ref, lam_ref, h_ref, tail, hc, a_s, b_s, c_s):
        t = pl.program_id(1)

        @pl.when(t == 0)
        def _():
            tail[...] = jnp.zeros((8, W), F32)
            hc[...] = jnp.zeros((8, W), F32)

        x = x_ref[...]
        before = tail[...]
        cw = cw_ref[...]
        xa = (cb_ref[...] + cw[3:4] * x + cw[2:3] * _rows_before(x, before, 1) + cw[1:2] * _rows_before(x, before, 2)
              + cw[0:1] * _rows_before(x, before, 3))
        tail[...] = x[tt - 8:tt, :]
        _r, i, _sp, a, mult = _lru_gates(xa, wa_ref, ba_ref, wx_ref, bx_ref, lam_ref[...])
        h, h_last = _scan_down(a, mult * (i * xa), hc[0:1, :], a_s, b_s, c_s)
        h_ref[...] = h.astype(BF16)
        hc[...] = jnp.broadcast_to(h_last, (8, W))

    blk = lambda n, t: (t, n)
    vec = pl.BlockSpec((1, W), lambda n, t: (0, n))
    mat = pl.BlockSpec((LRU_NB, BW, BW), lambda n, t: (n, 0, 0))
    bias = pl.BlockSpec((LRU_NB, 1, BW), lambda n, t: (n, 0, 0))
    row8 = pltpu.VMEM((8, W), F32)
    wide = pltpu.VMEM((LRU_NB, tt, LANES), F32)
    return pl.pallas_call(
        body, name="lru_fwd", grid=(RNN_BLOCKS // LRU_NB, n_t),
        in_specs=[pl.BlockSpec((tt, W), blk), pl.BlockSpec((CONV_WIDTH, W), lambda n, t: (0, n)), vec, mat, bias, mat,
                  bias, vec],
        out_specs=pl.BlockSpec((tt, W), blk), out_shape=jax.ShapeDtypeStruct((S, D_MODEL), BF16),
        scratch_shapes=[row8, row8, wide, wide, wide],
        compiler_params=_params(("parallel", "arbitrary")),
    )(z_rx, conv_w, conv_b, wa, ba, wx, bx, lam)


def _lru_bwd(z_rx, h, dh, dz, conv_w, conv_b, wa, wat, ba, wx, wxt, bx, lam, *, tt):
    S = z_rx.shape[0]
    n_t = S // tt
    BW = RNN_BLOCK_W
    W = LRU_NB * BW
    t8 = tt // 8

    def body(x_ref, xp_ref, h_ref, hp_ref, dh_ref, _dz_ref, cw_ref, cb_ref, wa_ref, wat_ref, ba_ref, wx_ref, wxt_ref,
             bx_ref, lam_ref, dx_ref, dwa_ref, dwx_ref, dba_ref, dbx_ref, dlam_ref, dcw_ref, dcb_ref, nxt, a_c, g_c, a_s,
             b_s, c_s):
        t = pl.program_id(1)
        tile = n_t - 1 - t

        @pl.when(t == 0)
        def _():
            a_c[...] = jnp.zeros((8, W), F32)
            g_c[...] = jnp.zeros((8, W), F32)
            nxt[...] = jnp.zeros((8, W), F32)
            dwa_ref[...] = jnp.zeros_like(dwa_ref)
            dwx_ref[...] = jnp.zeros_like(dwx_ref)
            dba_ref[...] = jnp.zeros_like(dba_ref)
            dbx_ref[...] = jnp.zeros_like(dbx_ref)
            dlam_ref[...] = jnp.zeros_like(dlam_ref)
            dcw_ref[...] = jnp.zeros_like(dcw_ref)
            dcb_ref[...] = jnp.zeros_like(dcb_ref)

        has_prev = (tile > 0).astype(F32)
        x = x_ref[...]
        before = xp_ref[...] * has_prev
        xm1, xm2, xm3 = _rows_before(x, before, 1), _rows_before(x, before, 2), _rows_before(x, before, 3)
        cw = cw_ref[...]
        xa = cb_ref[...] + cw[3:4] * x + cw[2:3] * xm1 + cw[1:2] * xm2 + cw[0:1] * xm3
        lam = lam_ref[...]
        r, i, sp, a, mult = _lru_gates(xa, wa_ref, ba_ref, wx_ref, bx_ref, lam)
        gated = i * xa
        h_prev = _rows_before(h_ref[...].astype(F32), hp_ref[8:16, :].astype(F32) * has_prev, 1)
        g, g_first = _scan_up(_rows_after(a, a_c[...], 1), dh_ref[...], g_c[0:1, :], a_s, b_s, c_s)
        a_c[...] = jnp.broadcast_to(a[0:1, :], (8, W))
        g_c[...] = jnp.broadcast_to(g_first, (8, W))
        dlog_a = g * h_prev * a - g * gated * (a * a) / mult
        dgated = g * mult
        di = dgated * xa
        dxa = dgated * i
        dr = dlog_a * (-LRU_C * sp)
        dlam_ref[...] += jnp.sum(dlog_a * (-LRU_C * r), axis=0, keepdims=True) * (-_sigmoid(-lam))
        dpr = dr * r * (1.0 - r)
        dpi = di * i * (1.0 - i)
        xab, dprb, dpib = xa.astype(BF16), dpr.astype(BF16), dpi.astype(BF16)
        tn_dims = (((0,), (0,)), ((), ()))
        back = []
        for j in range(LRU_NB):
            sl = slice(j * BW, (j + 1) * BW)
            dwa_ref[j] += lax.dot_general(xab[:, sl], dprb[:, sl], tn_dims, preferred_element_type=F32)
            dwx_ref[j] += lax.dot_general(xab[:, sl], dpib[:, sl], tn_dims, preferred_element_type=F32)
            dba_ref[j] += jnp.sum(dpr[:, sl], axis=0, keepdims=True)
            dbx_ref[j] += jnp.sum(dpi[:, sl], axis=0, keepdims=True)
            back.append(jnp.dot(dprb[:, sl], wat_ref[j], preferred_element_type=F32)
                        + jnp.dot(dpib[:, sl], wxt_ref[j], preferred_element_type=F32))
        dxa = dxa + jnp.concatenate(back, axis=1)
        after = nxt[...]
        dx = (cw[3:4] * dxa + cw[2:3] * _rows_after(dxa, after, 1) + cw[1:2] * _rows_after(dxa, after, 2)
              + cw[0:1] * _rows_after(dxa, after, 3))
        nxt[...] = dxa[0:8, :]
        dx_ref[...] = dx.astype(BF16)
        dcw_ref[3:4, :] += jnp.sum(dxa * x, axis=0, keepdims=True)
        dcw_ref[2:3, :] += jnp.sum(dxa * xm1, axis=0, keepdims=True)
        dcw_ref[1:2, :] += jnp.sum(dxa * xm2, axis=0, keepdims=True)
        dcw_ref[0:1, :] += jnp.sum(dxa * xm3, axis=0, keepdims=True)
        dcb_ref[...] += jnp.sum(dxa, axis=0, keepdims=True)

    blk = lambda n, t: (n_t - 1 - t, n)
    prev = lambda n, t: (jnp.maximum((n_t - 1 - t) * t8 - 1, 0), n)
    vec = pl.BlockSpec((1, W), lambda n, t: (0, n))
    mat = pl.BlockSpec((LRU_NB, BW, BW), lambda n, t: (n, 0, 0))
    bias = pl.BlockSpec((LRU_NB, 1, BW), lambda n, t: (n, 0, 0))
    cws = pl.BlockSpec((CONV_WIDTH, W), lambda n, t: (0, n))
    tile = pl.BlockSpec((tt, W), blk)
    prev8 = pl.BlockSpec((8, W), prev)
    prev16 = pl.BlockSpec((16, W), lambda n, t: (jnp.maximum((n_t - 1 - t) * (tt // 16) - 1, 0), n))
    row8 = pltpu.VMEM((8, W), F32)
    wide = pltpu.VMEM((LRU_NB, tt, LANES), F32)
    return pl.pallas_call(
        body, name="lru_bwd", grid=(RNN_BLOCKS // LRU_NB, n_t),
        in_specs=[tile, prev8, tile, prev16, tile, pl.BlockSpec(memory_space=pl.ANY), cws, vec, mat, mat, bias, mat, mat,
                  bias, vec],
        out_specs=[tile, mat, mat, bias, bias, vec, cws, vec], input_output_aliases={5: 0},
        out_shape=[jax.ShapeDtypeStruct(dz.shape, BF16),
                   jax.ShapeDtypeStruct((RNN_BLOCKS, BW, BW), F32), jax.ShapeDtypeStruct((RNN_BLOCKS, BW, BW), F32),
                   jax.ShapeDtypeStruct((RNN_BLOCKS, 1, BW), F32), jax.ShapeDtypeStruct((RNN_BLOCKS, 1, BW), F32),
                   jax.ShapeDtypeStruct((1, D_MODEL), F32),
                   jax.ShapeDtypeStruct((CONV_WIDTH, D_MODEL), F32), jax.ShapeDtypeStruct((1, D_MODEL), F32)],
        scratch_shapes=[row8, row8, row8, wide, wide, wide],
        compiler_params=_params(("parallel", "arbitrary")),
    )(z_rx, z_rx, h, h, dh, dz, conv_w, conv_b, wa, wat, ba, wx, wxt, bx, lam)


def _mla_proj(z_ckv, q_norm, kv_norm, w_uq, w_ukv, cos, sin, *, ts):
    S = z_ckv.shape[0]
    H = MLA_HEADS

    def body(c_ref, qn_ref, kn_ref, wq_ref, wkv_ref, cos_ref, sin_ref, q_ref, k_ref, v_ref):
        c = c_ref[...]
        cqn, _ = _rms_fwd(c[:, 0:Q_LORA], qn_ref[...])
        ckn, _ = _rms_fwd(c[:, Q_LORA:Q_LORA + KV_LORA], kn_ref[...])
        q = jnp.dot(cqn.astype(BF16), wq_ref[...], preferred_element_type=F32) * (ATTN_SCALE * LOG2E)
        kv = jnp.dot(ckn.astype(BF16), wkv_ref[...], preferred_element_type=F32)
        cos1, sin1 = cos_ref[...], sin_ref[...]
        cos8 = jnp.concatenate([cos1] * H, axis=1)
        sin8 = jnp.concatenate([sin1] * H, axis=1)
        qr = q[:, H * QK_NOPE:]
        lane8 = lax.broadcasted_iota(jnp.int32, qr.shape, 1)
        qr = qr * cos8 + _rot_half(qr, lane8) * sin8
        kr = c[:, Q_LORA + KV_LORA:]
        lane1 = lax.broadcasted_iota(jnp.int32, kr.shape, 1)
        kr = (kr * cos1 + _rot_half(kr, lane1) * sin1).astype(BF16)
        for h in range(H):
            q_ref[h, :, 0:QK_NOPE] = q[:, h * QK_NOPE:(h + 1) * QK_NOPE].astype(BF16)
            q_ref[h, :, QK_NOPE:] = qr[:, h * LANES:(h + 1) * LANES].astype(BF16)
            k_ref[h, :, 0:QK_NOPE] = kv[:, h * 2 * LANES:h * 2 * LANES + LANES].astype(BF16)
            k_ref[h, :, QK_NOPE:] = kr
            v_ref[h] = kv[:, h * 2 * LANES + LANES:(h + 1) * 2 * LANES].astype(BF16)

    full = lambda shape: pl.BlockSpec(shape, lambda i: (0,) * len(shape))
    return pl.pallas_call(
        body, name="mla_proj", grid=(S // ts,),
        in_specs=[pl.BlockSpec((ts, CKV_W), lambda i: (i, 0)), full((1, Q_LORA)), full((1, KV_LORA)),
                  full(w_uq.shape), full(w_ukv.shape), pl.BlockSpec((ts, LANES), lambda i: (i, 0)),
                  pl.BlockSpec((ts, LANES), lambda i: (i, 0))],
        out_specs=[pl.BlockSpec((H, ts, QK_PAD), lambda i: (0, i, 0)), pl.BlockSpec((H, ts, QK_PAD), lambda i: (0, i, 0)),
                   pl.BlockSpec((H, ts, V_HEAD), lambda i: (0, i, 0))],
        out_shape=[jax.ShapeDtypeStruct((H, S, QK_PAD), BF16), jax.ShapeDtypeStruct((H, S, QK_PAD), BF16),
                   jax.ShapeDtypeStruct((H, S, V_HEAD), BF16)],
        compiler_params=_params(("parallel",)),
    )(z_ckv, q_norm, kv_norm, w_uq, w_ukv, cos, sin)


def _mla_proj_bwd(z_ckv, dq, dk, dv, q_norm, kv_norm, w_uqt, w_ukvt, cos, sin, *, ts):
    S = z_ckv.shape[0]
    H = MLA_HEADS

    def body(c_ref, dq_ref, dk_ref, dv_ref, qn_ref, kn_ref, wqt_ref, wkvt_ref, cos_ref, sin_ref,
             dz_ref, dwq_ref, dwkv_ref, dqn_ref, dkn_ref):
        @pl.when(pl.program_id(0) == 0)
        def _():
            dwq_ref[...] = jnp.zeros_like(dwq_ref)
            dwkv_ref[...] = jnp.zeros_like(dwkv_ref)
            dqn_ref[...] = jnp.zeros_like(dqn_ref)
            dkn_ref[...] = jnp.zeros_like(dkn_ref)

        c = c_ref[...]
        cq, ck = c[:, 0:Q_LORA], c[:, Q_LORA:Q_LORA + KV_LORA]
        qn, kn = qn_ref[...], kn_ref[...]
        cqn, _ = _rms_fwd(cq, qn)
        ckn, _ = _rms_fwd(ck, kn)
        cos1, sin1 = cos_ref[...], sin_ref[...]
        lane1 = lax.broadcasted_iota(jnp.int32, cos1.shape, 1)

        def unrope(g):
            return g * cos1 - _rot_half(g * sin1, lane1)

        dq_all = jnp.concatenate([dq_ref[h, :, 0:QK_NOPE] for h in range(H)]
                                 + [unrope(dq_ref[h, :, QK_NOPE:]) for h in range(H)], axis=1)
        dq_all = (dq_all * ATTN_SCALE).astype(BF16)
        dkv_all = jnp.concatenate([p for h in range(H) for p in (dk_ref[h, :, 0:QK_NOPE], dv_ref[h])],
                                  axis=1).astype(BF16)
        dkr = dk_ref[0, :, QK_NOPE:].astype(F32)
        for h in range(1, H):
            dkr = dkr + dk_ref[h, :, QK_NOPE:].astype(F32)
        dkr = unrope(dkr)
        tn_dims = (((0,), (0,)), ((), ()))
        dwq_ref[...] += lax.dot_general(cqn.astype(BF16), dq_all, tn_dims, preferred_element_type=F32)
        dwkv_ref[...] += lax.dot_general(ckn.astype(BF16), dkv_all, tn_dims, preferred_element_type=F32)
        dcqn = jnp.dot(dq_all, wqt_ref[...], preferred_element_type=F32)
        dckn = jnp.dot(dkv_all, wkvt_ref[...], preferred_element_type=F32)
        dcq, dqn_rows = _rms_bwd(dcqn, cq, qn)
        dck, dkn_rows = _rms_bwd(dckn, ck, kn)
        dqn_ref[...] += jnp.sum(dqn_rows, axis=0, keepdims=True)
        dkn_ref[...] += jnp.sum(dkn_rows, axis=0, keepdims=True)
        dz_ref[:, 0:Q_LORA] = dcq.astype(BF16)
        dz_ref[:, Q_LORA:Q_LORA + KV_LORA] = dck.astype(BF16)
        dz_ref[:, Q_LORA + KV_LORA:] = dkr.astype(BF16)

    full = lambda shape: pl.BlockSpec(shape, lambda i: (0,) * len(shape))
    return pl.pallas_call(
        body, name="mla_proj_bwd", grid=(S // ts,),
        in_specs=[pl.BlockSpec((ts, CKV_W), lambda i: (i, 0)), pl.BlockSpec((H, ts, QK_PAD), lambda i: (0, i, 0)),
                  pl.BlockSpec((H, ts, QK_PAD), lambda i: (0, i, 0)), pl.BlockSpec((H, ts, V_HEAD), lambda i: (0, i, 0)),
                  full((1, Q_LORA)), full((1, KV_LORA)), full(w_uqt.shape), full(w_ukvt.shape),
                  pl.BlockSpec((ts, LANES), lambda i: (i, 0)), pl.BlockSpec((ts, LANES), lambda i: (i, 0))],
        out_specs=[pl.BlockSpec((ts, CKV_W), lambda i: (i, 0)), full((Q_LORA, w_uqt.shape[0])),
                   full((KV_LORA, w_ukvt.shape[0])), full((1, Q_LORA)), full((1, KV_LORA))],
        out_shape=[jax.ShapeDtypeStruct((S, CKV_W), BF16), jax.ShapeDtypeStruct((Q_LORA, w_uqt.shape[0]), F32),
                   jax.ShapeDtypeStruct((KV_LORA, w_ukvt.shape[0]), F32), jax.ShapeDtypeStruct((1, Q_LORA), F32),
                   jax.ShapeDtypeStruct((1, KV_LORA), F32)],
        compiler_params=_params(("arbitrary",)),
    )(z_ckv, dq, dk, dv, q_norm, kv_norm, w_uqt, w_ukvt, cos, sin)


NT_DIMS = (((1,), (1,)), ((), ()))
TN_DIMS = (((0,), (0,)), ((), ()))


def _attn_fwd(q, k, v, *, t, hb):
    H, S, _ = q.shape
    n = S // t
    pairs = [(i, j) for i in range(n) for j in range(i + 1)]
    qi = jnp.asarray(np.array([p[0] for p in pairs], np.int32))
    ki = jnp.asarray(np.array([p[1] for p in pairs], np.int32))

    def body(qi_ref, ki_ref, q_ref, k_ref, v_ref, o_ref, lse_ref, m_s, l_s, acc_s):
        p = pl.program_id(1)
        i, j = qi_ref[p], ki_ref[p]

        @pl.when(j == 0)
        def _():
            m_s[...] = jnp.full(m_s.shape, NEG, F32)
            l_s[...] = jnp.zeros(l_s.shape, F32)
            acc_s[...] = jnp.zeros(acc_s.shape, F32)

        def block(hh, r0, nr, nk, masked):
            rows = slice(r0, r0 + nr)
            s = lax.dot_general(q_ref[hh, rows, :], k_ref[hh, 0:nk, :], NT_DIMS, preferred_element_type=F32)
            if masked:
                row = lax.broadcasted_iota(jnp.int32, (nr, nk), 0) + r0
                col = lax.broadcasted_iota(jnp.int32, (nr, nk), 1)
                s = jnp.where(row >= col, s, NEG)
            chunks = nk // LANES
            mc = s[:, 0:LANES]
            for c in range(1, chunks):
                mc = jnp.maximum(mc, s[:, c * LANES:(c + 1) * LANES])
            m_prev = m_s[hh, rows, :]
            m_new = jnp.maximum(m_prev, jnp.max(mc, axis=1, keepdims=True))
            alpha = jnp.exp2(m_prev - m_new)
            pr = jnp.exp2(s - jnp.concatenate([m_new] * chunks, axis=1))
            ls = pr[:, 0:LANES]
            for c in range(1, chunks):
                ls = ls + pr[:, c * LANES:(c + 1) * LANES]
            l_s[hh, rows, :] = alpha * l_s[hh, rows, :] + ls
            acc_s[hh, rows, :] = alpha * acc_s[hh, rows, :] + jnp.dot(pr.astype(BF16), v_ref[hh, 0:nk, :],
                                                                      preferred_element_type=F32)
            m_s[hh, rows, :] = m_new

        def step(diagonal):
            for hh in range(hb):
                if diagonal:
                    block(hh, 0, t // 2, t // 2, True)
                    block(hh, t // 2, t // 2, t, True)
                else:
                    block(hh, 0, t, t, False)

        @pl.when(j < i)
        def _():
            step(False)

        @pl.when(j == i)
        def _():
            step(True)
            for hh in range(hb):
                l = jnp.sum(l_s[hh], axis=1, keepdims=True)
                o_ref[:, hh * V_HEAD:(hh + 1) * V_HEAD] = acc_s[hh] / l
                lse_ref[hh] = (m_s[hh] + jnp.log2(l)).T[0:1, :]

    grid_spec = pltpu.PrefetchScalarGridSpec(
        num_scalar_prefetch=2, grid=(H // hb, len(pairs)),
        in_specs=[pl.BlockSpec((hb, t, QK_PAD), lambda h, p, qi, ki: (h, qi[p], 0)),
                  pl.BlockSpec((hb, t, QK_PAD), lambda h, p, qi, ki: (h, ki[p], 0)),
                  pl.BlockSpec((hb, t, V_HEAD), lambda h, p, qi, ki: (h, ki[p], 0))],
        out_specs=[pl.BlockSpec((t, hb * V_HEAD), lambda h, p, qi, ki: (qi[p], h)),
                   pl.BlockSpec((hb, 1, t), lambda h, p, qi, ki: (h, 0, qi[p]))],
        scratch_shapes=[pltpu.VMEM((hb, t, LANES), F32), pltpu.VMEM((hb, t, LANES), F32),
                        pltpu.VMEM((hb, t, V_HEAD), F32)],
    )
    return pl.pallas_call(
        body, name="attn_fwd", grid_spec=grid_spec,
        out_shape=[jax.ShapeDtypeStruct((S, H * V_HEAD), F32), jax.ShapeDtypeStruct((H, 1, S), F32)],
        compiler_params=_params(("parallel", "arbitrary")),
    )(qi, ki, q, k, v)


def _attn_bwd(q, k, v, do, lse_row, delta_row, *, t):
    H, S, _ = q.shape
    n = S // t
    pairs = [(i, j) for j in range(n) for i in range(j, n)]
    qi = jnp.asarray(np.array([p[0] for p in pairs], np.int32))
    ki = jnp.asarray(np.array([p[1] for p in pairs], np.int32))

    def body(qi_ref, ki_ref, q_ref, k_ref, v_ref, do_ref, lse_ref, dl_ref, dq_ref, dk_ref, dv_ref, dk_s, dv_s, dq_s):
        p = pl.program_id(1)
        i, j = qi_ref[p], ki_ref[p]

        @pl.when(p == 0)
        def _():
            dq_s[...] = jnp.zeros_like(dq_s)

        def block(k0, nk, q0, nq, masked):
            qb, dob = q_ref[0, q0:q0 + nq, :], do_ref[q0:q0 + nq, :]
            kb, vb = k_ref[0, k0:k0 + nk, :], v_ref[0, k0:k0 + nk, :]
            st = lax.dot_general(kb, qb, NT_DIMS, preferred_element_type=F32)
            if masked:
                krow = lax.broadcasted_iota(jnp.int32, (nk, nq), 0) + k0
                qcol = lax.broadcasted_iota(jnp.int32, (nk, nq), 1) + q0
                st = jnp.where(krow <= qcol, st, NEG)
            pt = jnp.exp2(st - lse_ref[0][:, q0:q0 + nq])
            dvp = jnp.dot(pt.astype(BF16), dob, preferred_element_type=F32)
            dpt = lax.dot_general(vb, dob, NT_DIMS, preferred_element_type=F32)
            dst = (pt * (dpt - dl_ref[0][:, q0:q0 + nq])).astype(BF16)
            dkp = jnp.dot(dst, qb, preferred_element_type=F32)
            rows = pl.ds(pl.multiple_of(i * t + q0, LANES), nq)
            dq_s[rows, :] += lax.dot_general(dst, kb, TN_DIMS, preferred_element_type=F32)
            return dkp, dvp

        @pl.when(i == j)
        def _():
            half = t // 2
            dk_s[0:half, :], dv_s[0:half, :] = block(0, half, 0, t, True)
            dk_s[half:t, :], dv_s[half:t, :] = block(half, half, half, half, True)

        @pl.when(i != j)
        def _():
            dkp, dvp = block(0, t, 0, t, False)
            dk_s[...] += dkp
            dv_s[...] += dvp

        @pl.when(i == n - 1)
        def _():
            dk_ref[0] = (dk_s[...] * LN2).astype(BF16)
            dv_ref[0] = dv_s[...].astype(BF16)

        @pl.when(p == len(pairs) - 1)
        def _():
            dq_ref[0] = dq_s[...].astype(BF16)

    grid_spec = pltpu.PrefetchScalarGridSpec(
        num_scalar_prefetch=2, grid=(H, len(pairs)),
        in_specs=[pl.BlockSpec((1, t, QK_PAD), lambda h, p, qi, ki: (h, qi[p], 0)),
                  pl.BlockSpec((1, t, QK_PAD), lambda h, p, qi, ki: (h, ki[p], 0)),
                  pl.BlockSpec((1, t, V_HEAD), lambda h, p, qi, ki: (h, ki[p], 0)),
                  pl.BlockSpec((t, V_HEAD), lambda h, p, qi, ki: (qi[p], h)),
                  pl.BlockSpec((1, 1, t), lambda h, p, qi, ki: (h, 0, qi[p])),
                  pl.BlockSpec((1, 1, t), lambda h, p, qi, ki: (h, 0, qi[p]))],
        out_specs=[pl.BlockSpec((1, S, QK_PAD), lambda h, p, qi, ki: (h, 0, 0)),
                   pl.BlockSpec((1, t, QK_PAD), lambda h, p, qi, ki: (h, ki[p], 0)),
                   pl.BlockSpec((1, t, V_HEAD), lambda h, p, qi, ki: (h, ki[p], 0))],
        scratch_shapes=[pltpu.VMEM((t, QK_PAD), F32), pltpu.VMEM((t, V_HEAD), F32), pltpu.VMEM((S, QK_PAD), F32)],
    )
    return pl.pallas_call(
        body, name="attn_bwd", grid_spec=grid_spec,
        out_shape=[jax.ShapeDtypeStruct((H, S, QK_PAD), BF16), jax.ShapeDtypeStruct((H, S, QK_PAD), BF16),
                   jax.ShapeDtypeStruct((H, S, V_HEAD), BF16)],
        compiler_params=_params(("parallel", "arbitrary")),
    )(qi, ki, q, k, v, do, lse_row, delta_row)


def _merge_h1(h, z_gates, o, x, w_out, norm_mlp, *, ts):
    S = h.shape[0]
    D = D_MODEL

    def body(h_ref, rg_ref, ga_ref, gb_ref, o_ref, x_ref, w_ref, g_ref, m_ref, h1_ref, n2_ref):
        gl, _ = _gelu_and_grad(rg_ref[...].astype(F32))
        m = (_sigmoid(ga_ref[...].astype(F32)) * (h_ref[...].astype(F32) * gl)
             + _sigmoid(gb_ref[...].astype(F32)) * o_ref[...]).astype(BF16)
        m_ref[...] = m
        h1 = x_ref[...] + jnp.dot(m, w_ref[...], preferred_element_type=F32)
        h1_ref[...] = h1
        n2, _ = _rms_fwd(h1, g_ref[...])
        n2_ref[...] = n2.astype(BF16)

    col = lambda c: pl.BlockSpec((ts, D), lambda i: (i, c))
    fixed = lambda shape: pl.BlockSpec(shape, lambda i: (0, 0), pipeline_mode=pl.Buffered(1))
    return pl.pallas_call(
        body, name="merge_h1", grid=(S // ts,),
        in_specs=[col(0), col(0), col(1), col(2), col(0), col(0), fixed((D, D)), fixed((1, D))],
        out_specs=[col(0), col(0), col(0)],
        out_shape=[jax.ShapeDtypeStruct((S, D), BF16), jax.ShapeDtypeStruct((S, D), F32),
                   jax.ShapeDtypeStruct((S, D), BF16)],
        compiler_params=_params(("parallel",)),
    )(h, z_gates, z_gates, z_gates, o, x, w_out, norm_mlp)


def _my_place():
    return lax.axis_index("x"), lax.axis_index("y"), lax.axis_index("c")


def _all_gather(shards, *, name):
    n = len(shards)

    def body(*refs):
        x_refs, out_refs = refs[:n], refs[n:2 * n]
        send_sems, recv_sems, local_sems = refs[2 * n:]
        x, y, c = _my_place()
        me, sibling = (x, y, c), (x, y, 1 - c)
        chips = [(1 - x, y), (x, 1 - y), (1 - x, 1 - y)]

        def slot(a, px, py, pc):
            return out_refs[a].at[4 * px + 2 * py + pc]

        def copy(a, k, block, to, src=None):
            return pltpu.make_async_remote_copy(
                src_ref=slot(a, *block) if src is None else src, dst_ref=slot(a, *block),
                send_sem=send_sems.at[7 * a + k], recv_sem=recv_sems.at[7 * a + k], device_id=to, device_id_type=MESH)

        mine = [pltpu.make_async_copy(x_refs[a], slot(a, *me), local_sems.at[a]) for a in range(n)]
        for cp in mine:
            cp.start()
        first = []
        for a in range(n):
            first.append(copy(a, 0, me, sibling, src=x_refs[a]))
            first += [copy(a, 1 + j, me, (*chip, c), src=x_refs[a]) for j, chip in enumerate(chips)]
        for cp in first:
            cp.start()
        passed = []
        for a in range(n):
            for j, chip in enumerate(chips):
                copy(a, 1 + j, (*chip, c), me).wait_recv()
                fwd = copy(a, 4 + j, (*chip, c), sibling)
                fwd.start()
                passed.append(fwd)
        for a in range(n):
            copy(a, 0, sibling, me).wait_recv()
            for j, chip in enumerate(chips):
                copy(a, 4 + j, (*chip, 1 - c), me).wait_recv()
        for cp in first + passed:
            cp.wait_send()
        for cp in mine:
            cp.wait()

    hbm = pl.BlockSpec(memory_space=pl.ANY)
    return pl.pallas_call(
        body, name=name, out_shape=[jax.ShapeDtypeStruct((N_DEV, *s.shape), s.dtype) for s in shards],
        in_specs=[hbm] * n, out_specs=[hbm] * n,
        scratch_shapes=[pltpu.SemaphoreType.DMA((7 * n,)), pltpu.SemaphoreType.DMA((7 * n,)),
                        pltpu.SemaphoreType.DMA((n,))],
    )(*shards)


def _pushes(src_refs, land_refs, send_sems, recv_sems, slab_per_peer):
    x, y, c = _my_place()
    me = 4 * x + 2 * y + c
    copies = []
    for a in range(len(src_refs)):
        for k in range(1, N_DEV):
            px, py, pc = x ^ (k >> 2), y ^ ((k >> 1) & 1), c ^ (k & 1)
            src = src_refs[a].at[4 * px + 2 * py + pc] if slab_per_peer else src_refs[a]
            copies.append(pltpu.make_async_remote_copy(
                src_ref=src, dst_ref=land_refs[a].at[me], send_sem=send_sems.at[7 * a + k - 1],
                recv_sem=recv_sems.at[7 * a + k - 1], device_id=(px, py, pc), device_id_type=MESH))
    return copies


def _push_start(srcs, *, name, slab_per_peer):
    n = len(srcs)
    lands = [lax.empty((N_DEV, *(s.shape[1:] if slab_per_peer else s.shape)), s.dtype) for s in srcs]

    def body(*refs):
        src_refs, land_refs = refs[:n], refs[n:2 * n]
        send_sems, recv_sems, token = refs[2 * n], refs[2 * n + 1], refs[-1]
        for cp in _pushes(src_refs, land_refs, send_sems, recv_sems, slab_per_peer):
            cp.start()
        token[...] = jnp.zeros_like(token)

    hbm = pl.BlockSpec(memory_space=pltpu.HBM)
    sem = pl.BlockSpec(memory_space=pltpu.SEMAPHORE)
    out = pl.pallas_call(
        body, name=name,
        out_shape=(pltpu.SemaphoreType.DMA((7 * n,)), pltpu.SemaphoreType.DMA((7 * n,)),
                   *[pltpu.HBM(a.shape, a.dtype) for a in srcs + lands], jax.ShapeDtypeStruct((8, LANES), F32)),
        in_specs=[hbm] * (2 * n), out_specs=(sem, sem, *[hbm] * (2 * n), pl.BlockSpec(memory_space=pltpu.VMEM)),
        input_output_aliases={i: 2 + i for i in range(2 * n)},
        compiler_params=pltpu.CompilerParams(has_side_effects=pltpu.SideEffectType.DATAFLOW_SIDE_EFFECTING),
    )(*[pltpu.with_memory_space_constraint(a, pltpu.HBM) for a in srcs + lands])
    return out[0], out[1], list(out[2:2 + n]), list(out[2 + n:2 + 2 * n]), out[-1]


def _push_wait(send_sems, recv_sems, srcs, lands, after, *, name, slab_per_peer):
    n = len(srcs)

    def body(*refs):
        src_refs, land_refs = refs[:n], refs[n:2 * n]
        s_sems, r_sems = refs[2 * n], refs[2 * n + 1]
        for cp in _pushes(src_refs, land_refs, s_sems, r_sems, slab_per_peer):
            cp.wait_send()
            cp.wait_recv()

    hbm = pl.BlockSpec(memory_space=pltpu.HBM)
    sem = pl.BlockSpec(memory_space=pltpu.SEMAPHORE)
    out = pl.pallas_call(
        body, name=name, out_shape=tuple(pltpu.HBM(a.shape, a.dtype) for a in srcs + lands),
        in_specs=[hbm] * (2 * n) + [sem, sem, pl.BlockSpec(memory_space=pl.ANY)], out_specs=tuple([hbm] * (2 * n)),
        input_output_aliases={i: i for i in range(2 * n)},
        compiler_params=pltpu.CompilerParams(has_side_effects=pltpu.SideEffectType.DATAFLOW_SIDE_EFFECTING),
    )(*srcs, *lands, send_sems, recv_sems, after)
    return list(out[:n]), list(out[n:])


def _sum_parts(gp_ref, rows):
    g = gp_ref[0, 0:rows, :].astype(F32)
    for p in range(1, gp_ref.shape[0]):
        g = g + gp_ref[p, 0:rows, :].astype(F32)
    return g


def _adamw_update(w, m, v, g):
    m_new = ADAM_B1 * m + (1.0 - ADAM_B1) * g
    v_new = ADAM_B2 * v + (1.0 - ADAM_B2) * (g * g)
    m_hat = m_new / (1.0 - ADAM_B1 ** ADAM_STEP)
    v_hat = v_new / (1.0 - ADAM_B2 ** ADAM_STEP)
    return -ADAM_LR * (m_hat / (jnp.sqrt(v_hat) + ADAM_EPS) + ADAM_WD * w), m_new, v_new


def _adamw_many(ws, ms, vs, gparts, sums, *, name):
    n, k = len(ws), len(sums)

    def body(*refs):
        w_refs, m_refs, v_refs = refs[:n], refs[n:2 * n], refs[2 * n:3 * n]
        g_refs, s_refs, outs = refs[3 * n:4 * n], refs[4 * n:4 * n + k], refs[4 * n + k:]
        for a in range(n):
            g = _sum_parts(g_refs[a], w_refs[a].shape[0])
            d, m_new, v_new = _adamw_update(w_refs[a][...], m_refs[a][...], v_refs[a][...], g)
            for o_ref, val in zip(outs[4 * a:4 * a + 4], (g, d, m_new, v_new)):
                o_ref[...] = val
        for b in range(k):
            outs[4 * n + b][...] = _sum_parts(s_refs[b], s_refs[b].shape[1])

    out_shape = [jax.ShapeDtypeStruct(w.shape, F32) for w in ws for _ in range(4)]
    out_shape += [jax.ShapeDtypeStruct(s.shape[1:], F32) for s in sums]
    return pl.pallas_call(body, name=name, out_shape=out_shape, compiler_params=_params())(
        *ws, *ms, *vs, *gparts, *sums)


def _adamw(w, m, v, gparts, *, tr, name):
    R, C = w.shape
    n_parts = gparts.shape[0]

    def body(w_ref, m_ref, v_ref, gp_ref, g_ref, d_ref, nm_ref, nv_ref):
        g = _sum_parts(gp_ref, tr)
        d_ref[...], nm_ref[...], nv_ref[...] = _adamw_update(w_ref[...], m_ref[...], v_ref[...], g)
        g_ref[...] = g

    row = pl.BlockSpec((tr, C), lambda i: (i, 0))
    shp = jax.ShapeDtypeStruct((R, C), F32)
    return pl.pallas_call(
        body, name=name, grid=(R // tr,),
        in_specs=[row, row, row, pl.BlockSpec((n_parts, tr, C), lambda i: (0, i, 0))],
        out_specs=[row, row, row, row], out_shape=[shp, shp, shp, shp],
        compiler_params=_params(("parallel",)),
    )(w, m, v, gparts)


def _rope_tables(s):
    pos = jnp.arange(s, dtype=F32)
    inv_freq = 1.0 / (ROPE_THETA ** (jnp.arange(0, QK_ROPE, 2, dtype=F32) / QK_ROPE))
    per_lane = jnp.concatenate([inv_freq, inv_freq, jnp.zeros((LANES - QK_ROPE,), F32)])
    ang = pos[:, None] * per_lane[None, :]
    live = jnp.arange(LANES) < QK_ROPE
    return jnp.where(live, jnp.cos(ang), 0.0), jnp.where(live, jnp.sin(ang), 0.0)


def _pick(n, want):
    t = min(n, want)
    assert n % t == 0
    return t


def _local_step(x, target, wts, small, hooks):
    S = x.shape[0]
    H = MLA_HEADS
    ts = _pick(S, 1024)
    tm = _pick(S, 512)
    tm_wide = _pick(S, 1024)
    tk_s = _pick(S, 4096)
    tt = _pick(S, 512)
    ta = _pick(S, 1024)
    ts_proj = _pick(S, 512)
    ts_merge = _pick(S, 512)
    row = lambda v: v.reshape(1, -1)
    w_in = wts["w_in"]
    w_main = jnp.concatenate([w_in[:, 0:2048], w_in[:, 2624:4672]], axis=1)
    w_ckv = jnp.concatenate([w_in[:, 2048:2624], jnp.zeros((D_MODEL, CKV_W - 576), BF16)], axis=1)
    w_uq3 = wts["w_uq"].reshape(Q_LORA, H, QK_NOPE + QK_ROPE)
    w_uq_p = jnp.concatenate(
        [w_uq3[:, :, :QK_NOPE].reshape(Q_LORA, H * QK_NOPE),
         jnp.pad(w_uq3[:, :, QK_NOPE:], ((0, 0), (0, 0), (0, LANES - QK_ROPE))).reshape(Q_LORA, H * LANES)], axis=1)
    w_ukv = wts["w_ukv"]
    cos, sin = _rope_tables(S)
    conv_w, conv_b = small["conv_w"], row(small["conv_b"])
    wa, wx = small["lru_wa"].astype(BF16), small["lru_wx"].astype(BF16)
    wat, wxt = jnp.swapaxes(wa, 1, 2), jnp.swapaxes(wx, 1, 2)
    ba, bx = small["lru_ba"].reshape(RNN_BLOCKS, 1, RNN_BLOCK_W), small["lru_bx"].reshape(RNN_BLOCKS, 1, RNN_BLOCK_W)
    lam = row(small["lru_lambda"])
    q_norm, kv_norm = row(small["q_norm"]), row(small["kv_norm"])
    norm_mix, norm_mlp, norm_final = row(small["norm_mix"]), row(small["norm_mlp"]), row(small["norm_final"])

    xn = _rmsnorm_cast(x, norm_mix, ts=ts, name="norm_mix")
    ident = lambda acc: (acc,)
    z_rx, z_ckv = _mm(xn, w_main[:, :D_MODEL], name="z_rx_ckv", tm=tm_wide, tn=1024, tk=1024, outs=[("tile", F32)],
                      epilogue=ident, also=w_ckv)
    (z_gates,) = _mm(xn, w_main[:, D_MODEL:], name="z_gates", tm=tm_wide, tn=3 * D_MODEL, tk=1024, outs=[("tile", BF16)],
                     epilogue=ident)
    h = _lru_fwd(z_rx, conv_w, conv_b, wa, ba, wx, bx, lam, tt=tt)
    q, k, v = _mla_proj(z_ckv, q_norm, kv_norm, w_uq_p, w_ukv, cos, sin, ts=ts_proj)
    o, lse = _attn_fwd(q, k, v, t=ta, hb=4)
    w_out, w_up, w_down = hooks["weights_later"](o)
    merged, h1, n2 = _merge_h1(h, z_gates, o, x, w_out, norm_mlp, ts=ts_merge)

    def ep_up(acc):
        r = jnp.maximum(acc, 0.0)
        return r * r, r

    act, relu = _mm(n2, w_up, name="up", tm=tm_wide, tn=2048, tk=1024, outs=[("tile", BF16), ("tile", BF16)],
                    epilogue=ep_up)

    def ep_loss(acc, h1v, tgt, g):
        h2 = acc + h1v
        y, _ = _rms_fwd(h2, g)
        err = y - tgt
        loss_rows = 0.5 * jnp.mean(err * err, axis=-1, keepdims=True)
        dy = err * (1.0 / D_MODEL)
        dh2, dg_rows = _rms_bwd(dy, h2, g)
        lsum = jnp.sum(loss_rows, axis=0, keepdims=True)
        return dh2, dh2, jnp.sum(dg_rows, axis=0, keepdims=True), jnp.broadcast_to(lsum, (1, D_MODEL))

    dh2, dh2b, dnf_p, loss_p = _mm(
        act, w_down, name="down_loss", tm=tm, tn=1024, tk=D_FF,
        outs=[("tile", F32), ("tile", BF16), ("rowpart", F32), ("rowpart", F32)], epilogue=ep_loss,
        extras=[("tile", h1), ("tile", target), ("row", norm_final)])
    loss_part = jnp.sum(loss_p[:, 0, 0])
    d_norm_final = jnp.sum(dnf_p, axis=(0, 1))

    def ep_du(acc, r):
        return (acc * (2.0 * r.astype(F32)),)

    (du,) = _mm(dh2b, w_down, name="d_act", tb=True, tm=tm_wide, tn=2048, tk=1024, outs=[("tile", BF16)], epilogue=ep_du,
                extras=[("tile", relu)])

    def ep_dh1(acc, h1v, dh2v, g):
        dv, dg_rows = _rms_bwd(acc, h1v, g)
        dh1 = dh2v + dv
        return dh1, dh1, jnp.sum(dg_rows, axis=0, keepdims=True)

    dh1, dh1b, dnm_p = _mm(du, w_up, name="d_n2", tb=True, tm=tm, tn=1024, tk=D_FF,
                           outs=[("tile", F32), ("tile", BF16), ("rowpart", F32)], epilogue=ep_dh1,
                           extras=[("tile", h1), ("tile", dh2), ("row", norm_mlp)])
    d_norm_mlp = jnp.sum(dnm_p, axis=(0, 1))
    tn_mm = functools.partial(_mm, ta=True, tk=tk_s, outs=[("tile", BF16)], epilogue=ident)
    (d_w_down,) = tn_mm(act, dh2b, name="dw_down", tm=1024, tn=1024)
    (p_w_up,) = _mm(n2, du, name="dw_up", ta=True, tk=tk_s, tm=1024, tn=D_FF // N_DEV, outs=[("colshard", BF16)],
                    epilogue=ident)
    (d_w_out,) = tn_mm(merged, dh1b, name="dw_out", tm=1024, tn=1024)
    early = [d_w_out.reshape(N_DEV, -1, D_MODEL), p_w_up, d_w_down.reshape(N_DEV, -1, D_MODEL)]
    w_out = w_out + hooks["send"]("early", early)[0, 0].astype(BF16)


    def ep_dmerge(dm, hv, rg, ga, gb, ov):
        hv, rg, ga, gb = hv.astype(F32), rg.astype(F32), ga.astype(F32), gb.astype(F32)
        gl, dgl = _gelu_and_grad(rg)
        sa, sb = _sigmoid(ga), _sigmoid(gb)
        ya = hv * gl
        dya = dm * sa
        do = dm * sb
        dga = dm * ya * sa * (1.0 - sa)
        dgb = dm * ov * sb * (1.0 - sb)
        dh = dya * gl
        drg = dya * hv * dgl
        dov = do * ov
        lane = lax.broadcasted_iota(jnp.int32, (dm.shape[0], LANES), 1)
        delta = jnp.zeros((dm.shape[0], LANES), F32)
        for hh in range(H):
            dsum = jnp.sum(dov[:, hh * V_HEAD:(hh + 1) * V_HEAD], axis=1, keepdims=True)
            delta = jnp.where(lane == hh, dsum, delta)
        return dh, jnp.concatenate([drg, dga, dgb], axis=1), do, delta

    dh_lru, dz_part, do, delta_w = _mm(
        dh1b, w_out, name="d_merge", tb=True, tm=ts_merge, tn=1024, tk=1024,
        outs=[("tile", F32), ("cols", BF16, 4 * D_MODEL, D_MODEL), ("tile", BF16), ("side", F32)],
        epilogue=ep_dmerge,
        extras=[("tile", h), ("tilecol", z_gates, 0), ("tilecol", z_gates, 1), ("tilecol", z_gates, 2), ("tile", o)])
    delta_row = delta_w[:, :H].T.reshape(H, 1, S)
    lse_row = lse

    dq, dk, dv = _attn_bwd(q, k, v, do, lse_row, delta_row, t=ta)
    dz_ckv, d_w_uq_p, d_w_ukv, d_q_norm, d_kv_norm = _mla_proj_bwd(
        z_ckv, dq, dk, dv, q_norm, kv_norm, w_uq_p.T, w_ukv.T, cos, sin, ts=ts_proj)
    d_w_uq = jnp.concatenate(
        [d_w_uq_p[:, :H * QK_NOPE].reshape(Q_LORA, H, QK_NOPE),
         d_w_uq_p[:, H * QK_NOPE:].reshape(Q_LORA, H, LANES)[:, :, :QK_ROPE]], axis=2).reshape(Q_LORA, -1)

    dz_main, d_wa, d_wx, d_ba, d_bx, d_lam, d_conv_w, d_conv_b = _lru_bwd(
        z_rx, h, dh_lru, dz_part, conv_w, conv_b, wa, wat, ba, wx, wxt, bx, lam, tt=tt)

    (d_w_main,) = tn_mm(xn, dz_main, name="dw_main", tm=1024, tn=1024)
    (d_w_ckv,) = tn_mm(xn, dz_ckv, name="dw_ckv", tm=1024, tn=CKV_W)
    d_w_in = jnp.concatenate([d_w_main[:, 0:2048], d_w_ckv[:, 0:576], d_w_main[:, 2048:4096]], axis=1)

    def col_parts(full):
        r = full.shape[0]
        return jnp.transpose(full.astype(BF16).reshape(r, N_DEV, -1), (1, 0, 2))

    late = [col_parts(d_w_in), col_parts(d_w_uq), col_parts(d_w_ukv)]
    norm_mix = norm_mix + hooks["send"]("late", late)[0, 0]

    def ep_dx(acc, xv, dh1v, g):
        dv, dg_rows = _rms_bwd(acc, xv, g)
        return dh1v + dv, jnp.sum(dg_rows, axis=0, keepdims=True)

    grad_x, dnx_p = _mm(dz_main, w_main, name="dx", tb=True, tm=tm, tn=1024, tk=4 * D_MODEL,
                        outs=[("tile", F32), ("rowpart", F32)], epilogue=ep_dx, more=(dz_ckv, w_ckv),
                        extras=[("tile", x), ("tile", dh1), ("row", norm_mix)])
    d_norm_mix = jnp.sum(dnx_p, axis=(0, 1))
    sm = {"norm_mix": d_norm_mix, "conv_w": d_conv_w, "conv_b": d_conv_b.reshape(-1), "lru_wa": d_wa,
          "lru_ba": d_ba.reshape(RNN_BLOCKS, RNN_BLOCK_W), "lru_wx": d_wx, "lru_bx": d_bx.reshape(RNN_BLOCKS, RNN_BLOCK_W),
          "lru_lambda": d_lam.reshape(-1), "q_norm": d_q_norm.reshape(-1), "kv_norm": d_kv_norm.reshape(-1),
          "norm_mlp": d_norm_mlp, "norm_final": d_norm_final}
    return loss_part, grad_x, sm


SMALL = ("norm_mix", "conv_b", "lru_wa", "lru_ba", "lru_wx", "lru_bx", "lru_lambda", "q_norm", "kv_norm", "norm_mlp",
         "norm_final")
WEIGHTS = ("norm_mix", "w_in", "conv_w", "conv_b", "lru_wa", "lru_ba", "lru_wx", "lru_bx", "lru_lambda", "q_norm", "w_uq",
           "kv_norm", "w_ukv", "w_out", "norm_mlp", "w_up", "w_down", "norm_final")
ADAM_TILE_ROWS = {"w_in": 256, "w_uq": 128, "w_ukv": 128, "w_out": 64, "w_up": 256, "w_down": 128}
CONV_ROWS = N_DEV * 8


def _rows(a):
    return a.reshape(-1, LANES)


def _pad_rows(a, mult):
    r = a.shape[-2]
    pad = (-r) % mult
    if pad == 0:
        return a
    cfg = [(0, 0)] * (a.ndim - 2) + [(0, pad), (0, 0)]
    return jnp.pad(a, cfg)


def _cols_from_shards(g):
    return jnp.transpose(g, (1, 0, 2)).reshape(g.shape[1], -1)


def kernel(x, norm_mix, w_in, conv_w, conv_b, lru_wa, lru_ba, lru_wx, lru_bx, lru_lambda, q_norm, w_uq, kv_norm, w_ukv, w_out, norm_mlp, w_up, w_down, norm_final, loss_target, m_norm_mix, m_w_in, m_conv_w, m_conv_b, m_lru_wa, m_lru_ba, m_lru_wx, m_lru_bx, m_lru_lambda, m_q_norm, m_w_uq, m_kv_norm, m_w_ukv, m_w_out, m_norm_mlp, m_w_up, m_w_down, m_norm_final, v_norm_mix, v_w_in, v_conv_w, v_conv_b, v_lru_wa, v_lru_ba, v_lru_wx, v_lru_bx, v_lru_lambda, v_q_norm, v_w_uq, v_kv_norm, v_w_ukv, v_w_out, v_norm_mlp, v_w_up, v_w_down, v_norm_final):
    W = dict(norm_mix=norm_mix, w_in=w_in, conv_w=conv_w, conv_b=conv_b, lru_wa=lru_wa, lru_ba=lru_ba, lru_wx=lru_wx,
             lru_bx=lru_bx, lru_lambda=lru_lambda, q_norm=q_norm, w_uq=w_uq, kv_norm=kv_norm, w_ukv=w_ukv, w_out=w_out,
             norm_mlp=norm_mlp, w_up=w_up, w_down=w_down, norm_final=norm_final)
    M = dict(norm_mix=m_norm_mix, w_in=m_w_in, conv_w=m_conv_w, conv_b=m_conv_b, lru_wa=m_lru_wa, lru_ba=m_lru_ba,
             lru_wx=m_lru_wx, lru_bx=m_lru_bx, lru_lambda=m_lru_lambda, q_norm=m_q_norm, w_uq=m_w_uq, kv_norm=m_kv_norm,
             w_ukv=m_w_ukv, w_out=m_w_out, norm_mlp=m_norm_mlp, w_up=m_w_up, w_down=m_w_down, norm_final=m_norm_final)
    V = dict(norm_mix=v_norm_mix, w_in=v_w_in, conv_w=v_conv_w, conv_b=v_conv_b, lru_wa=v_lru_wa, lru_ba=v_lru_ba,
             lru_wx=v_lru_wx, lru_bx=v_lru_bx, lru_lambda=v_lru_lambda, q_norm=v_q_norm, w_uq=v_w_uq, kv_norm=v_kv_norm,
             w_ukv=v_w_ukv, w_out=v_w_out, norm_mlp=v_norm_mlp, w_up=v_w_up, w_down=v_w_down, norm_final=v_norm_final)
    me = 4 * lax.axis_index("x") + 2 * lax.axis_index("y") + lax.axis_index("c")

    first, later = ("w_in", "w_uq", "w_ukv"), ("w_out", "w_up", "w_down")
    got = _all_gather([W[n].astype(BF16) for n in first] + [_pad_rows(conv_w, 8)], name="gather_weights")
    wts = {"w_in": _cols_from_shards(got[0]), "w_uq": _cols_from_shards(got[1]), "w_ukv": _cols_from_shards(got[2])}
    w_send, w_recv, w_src, w_land, zeros = _push_start([W[n].astype(BF16) for n in later], name="gather_later_start",
                                                       slab_per_peer=False)
    small = {n: W[n] for n in SMALL}
    small["conv_w"] = _cols_from_shards(got[3][:, :CONV_WIDTH])
    small["norm_mix"] = norm_mix + zeros[0, 0]

    def with_own_slab(land, mine):
        return lax.dynamic_update_slice(land, mine, (me, 0, 0))

    def weights_later(after):
        srcs, lands = _push_wait(w_send, w_recv, w_src, w_land, after, name="gather_later_wait", slab_per_peer=False)
        w_out_g, w_up_g, w_down_g = [with_own_slab(l, s[None]) for l, s in zip(lands, srcs)]
        return w_out_g.reshape(-1, D_MODEL), _cols_from_shards(w_up_g), w_down_g.reshape(-1, D_MODEL)

    sent = {}
    G, Dl, NM, NV = {}, {}, {}, {}

    def finish(group, names, after):
        s_sems, r_sems, srcs, lands, _ = sent[group]
        srcs, lands = _push_wait(s_sems, r_sems, srcs, lands, after, name="exchange_" + group + "_wait",
                                 slab_per_peer=True)
        for n, src, land in zip(names, srcs, lands):
            parts = with_own_slab(land, lax.dynamic_slice(src, (me, 0, 0), (1, *src.shape[1:])))
            G[n], Dl[n], NM[n], NV[n] = _adamw(W[n], M[n], V[n], parts, tr=ADAM_TILE_ROWS[n], name="adamw_" + n)

    def send(group, parts):
        sent[group] = _push_start(parts, name="exchange_" + group + "_start", slab_per_peer=True)
        zeros = sent[group][4]
        if group == "late":
            finish("early", later, zeros)
            zeros = zeros + 0.0 * (Dl["w_out"][0:8, 0:LANES] + Dl["w_up"][0:8, 0:LANES] + Dl["w_down"][0:8, 0:LANES])
        return zeros

    loss_part, grad_x, g_small = _local_step(x[0], loss_target[0], wts, small,
                                              {"weights_later": weights_later, "send": send})
    finish("late", first, grad_x)

    conv_rows = _pad_rows(jnp.transpose(g_small["conv_w"].reshape(CONV_WIDTH, N_DEV, LANES), (1, 0, 2)), 8)
    loss_rows = jnp.zeros((8, LANES), F32).at[0, 0].set(loss_part)
    as_sent = lambda n: g_small[n].astype(BF16) if n in ("lru_wa", "lru_wx") else g_small[n]
    gathered = _all_gather([_pad_rows(_rows(as_sent(n)), 8) for n in SMALL]
                           + [conv_rows.reshape(CONV_ROWS, LANES), loss_rows], name="gather_small")
    k = len(SMALL)
    outs = _adamw_many([_rows(W[n]) for n in SMALL], [_rows(M[n]) for n in SMALL], [_rows(V[n]) for n in SMALL],
                       gathered[:k], gathered[k:], name="adamw_small")
    for j, n in enumerate(SMALL):
        for out, o in zip((G, Dl, NM, NV), outs[4 * j:4 * j + 4]):
            out[n] = o.reshape(W[n].shape)
    conv_sum, loss_sum = outs[4 * k:]
    loss = loss_sum[0, 0]

    g_conv = lax.dynamic_slice(conv_sum, (me * 8, 0), (8, LANES))
    conv_out = _adamw(_pad_rows(conv_w, 8), _pad_rows(m_conv_w, 8), _pad_rows(v_conv_w, 8), g_conv[None], tr=8,
                      name="adamw_conv_w")
    for out, pk in zip((G, Dl, NM, NV), conv_out):
        out["conv_w"] = pk[:CONV_WIDTH]
    return (loss, grad_x[None], *[G[n] for n in WEIGHTS], *[Dl[n] for n in WEIGHTS], *[NM[n] for n in WEIGHTS],
            *[NV[n] for n in WEIGHTS])
```

```python
import functools

import numpy as np
import jax
import jax.numpy as jnp
from jax import lax
from jax.experimental import pallas as pl
from jax.experimental.pallas import tpu as pltpu

F32 = jnp.float32
BF16 = jnp.bfloat16
MESH = pl.DeviceIdType.MESH

D_MODEL = 1024
N_DEV = 8
LANES = 128
RNN_BLOCKS = 8
RNN_BLOCK_W = 128
CONV_WIDTH = 4
LRU_C = 8.0
MLA_HEADS = 8
QK_NOPE = 128
QK_ROPE = 64
V_HEAD = 128
QK_PAD = 256
Q_LORA = 256
KV_LORA = 256
CKV_W = 640
ROPE_THETA = 10000.0
D_FF = 4096
EPS = 1e-6
ATTN_SCALE = (QK_NOPE + QK_ROPE) ** -0.5
LOG2E = 1.4426950408889634
LN2 = 0.6931471805599453
NEG = -1e30

ADAM_LR = 0.001
ADAM_B1 = 0.9
ADAM_B2 = 0.999
ADAM_EPS = 1e-08
ADAM_WD = 0.01
ADAM_STEP = 10

VMEM_LIMIT = 56 * 1024 * 1024


def _params(sem=None):
    return pltpu.CompilerParams(dimension_semantics=sem, vmem_limit_bytes=VMEM_LIMIT)


def _sigmoid(v):
    return 1.0 / (1.0 + jnp.exp(-v))


def _softplus(y):
    e = jnp.exp(-jnp.abs(y))
    u = 1.0 + e
    d = u - 1.0
    l1p = jnp.where(d == 0.0, e, jnp.log(u) * e / jnp.where(d == 0.0, 1.0, d))
    return jnp.maximum(y, 0.0) + l1p


_GELU_K = 0.7978845608028654
_GELU_C = 0.044715


def _gelu_and_grad(v):
    t = jnp.tanh(_GELU_K * (v + _GELU_C * v * v * v))
    g = 0.5 * v * (1.0 + t)
    dg = 0.5 * (1.0 + t) + 0.5 * v * (1.0 - t * t) * _GELU_K * (1.0 + 3.0 * _GELU_C * v * v)
    return g, dg


def _rms_fwd(v, g):
    rstd = lax.rsqrt(jnp.mean(v * v, axis=-1, keepdims=True) + EPS)
    return v * rstd * g, rstd


def _rms_bwd(dy, v, g):
    rstd = lax.rsqrt(jnp.mean(v * v, axis=-1, keepdims=True) + EPS)
    vh = v * rstd
    dvh = dy * g
    dv = rstd * (dvh - vh * jnp.mean(dvh * vh, axis=-1, keepdims=True))
    return dv, dy * vh


def _shift_down(v, s, fill, row):
    return jnp.where(row >= s, pltpu.roll(v, s, 0), fill)


def _shift_up(v, s, fill, row, n):
    return jnp.where(row < n - s, pltpu.roll(v, n - s, 0), fill)


def _rot_half(v, lane):
    n = v.shape[-1]
    l = lane & (LANES - 1)
    up = pltpu.roll(v, n - QK_ROPE // 2, 1)
    dn = pltpu.roll(v, QK_ROPE // 2, 1)
    return jnp.where(l < QK_ROPE // 2, -up, jnp.where(l < QK_ROPE, dn, 0.0))


def _mm(a, b, *, name, tm, tn, tk, outs, epilogue, extras=(), ta=False, tb=False, more=None, also=None):
    assert not (ta and tb)
    if ta:
        K, M = a.shape
    else:
        M, K = a.shape
    if tb:
        N, K2 = b.shape
    else:
        K2, N = b.shape
    assert K == K2 and M % tm == 0 and N % tn == 0 and K % tk == 0, (name, a.shape, b.shape)
    n_i, n_j, n_k = M // tm, N // tn, K // tk
    n_ex, n_out = len(extras), len(outs)
    n_more = 0 if more is None else 2
    n_also = 0 if also is None else 1
    assert more is None or (n_k == 1 and not ta)
    assert also is None or (n_k == 1 and n_j == 1 and not ta and more is None)

    def body(*refs):
        a_ref, b_ref = refs[0], refs[1]
        ex_refs = refs[2 + n_more:2 + n_more + n_ex]
        first_out = 2 + n_more + n_ex + n_also
        out_refs = refs[first_out:first_out + n_out]
        if ta:
            part = lax.dot_general(a_ref[...], b_ref[...], (((0,), (0,)), ((), ())), preferred_element_type=F32)
        elif tb:
            part = lax.dot_general(a_ref[...], b_ref[...], (((1,), (1,)), ((), ())), preferred_element_type=F32)
        else:
            part = jnp.dot(a_ref[...], b_ref[...], preferred_element_type=F32)
        if more is not None:
            part = part + lax.dot_general(refs[2][...], refs[3][...], (((1,), (1,)), ((), ())),
                                          preferred_element_type=F32)

        def finish(acc):
            res = epilogue(acc, *[r[...] for r in ex_refs])
            for o_ref, r, spec in zip(out_refs, res, outs):
                if spec[0] == "cols":
                    o_ref[:, spec[3]:spec[3] + r.shape[1]] = r.astype(o_ref.dtype)
                else:
                    o_ref[...] = r.astype(o_ref.dtype).reshape(o_ref.shape)

        if also is not None:
            refs[first_out + n_out][...] = jnp.dot(a_ref[...], refs[first_out - 1][...], preferred_element_type=F32)
        if n_k == 1:
            finish(part)
        else:
            acc_ref = refs[-1]
            k = pl.program_id(2)

            @pl.when(k == 0)
            def _():
                acc_ref[...] = part

            @pl.when(k > 0)
            def _():
                acc_ref[...] += part

            @pl.when(k == n_k - 1)
            def _():
                finish(acc_ref[...])

    a_spec = pl.BlockSpec((tk, tm), lambda j, i, k: (k, i)) if ta else pl.BlockSpec((tm, tk), lambda j, i, k: (i, k))
    b_once = dict(pipeline_mode=pl.Buffered(1)) if (n_j == 1 and n_k == 1) else {}
    if tb:
        in_specs = [a_spec, pl.BlockSpec((tn, tk), lambda j, i, k: (j, k), **b_once)]
    else:
        in_specs = [a_spec, pl.BlockSpec((tk, tn), lambda j, i, k: (k, j), **b_once)]
    if more is not None:
        k2 = more[0].shape[1]
        in_specs += [pl.BlockSpec((tm, k2), lambda j, i, k: (i, 0)), pl.BlockSpec((tn, k2), lambda j, i, k: (j, 0), **b_once)]
    for ex in extras:
        kind = ex[0]
        if kind == "tile":
            in_specs.append(pl.BlockSpec((tm, tn), lambda j, i, k: (i, j)))
        elif kind == "tilecol":
            assert n_j == 1
            in_specs.append(pl.BlockSpec((tm, tn), functools.partial(lambda c, j, i, k: (i, c), ex[2])))
        else:
            in_specs.append(pl.BlockSpec((1, tn), lambda j, i, k: (0, j)))
    out_specs, out_shape = [], []
    for kind, dt, *rest in outs:
        if kind == "tile":
            out_specs.append(pl.BlockSpec((tm, tn), lambda j, i, k: (i, j)))
            out_shape.append(jax.ShapeDtypeStruct((M, N), dt))
        elif kind == "colshard":
            out_specs.append(pl.BlockSpec((1, tm, tn), lambda j, i, k: (j, i, 0)))
            out_shape.append(jax.ShapeDtypeStruct((n_j, M, tn), dt))
        elif kind == "cols":
            assert n_j == 1
            out_specs.append(pl.BlockSpec((tm, rest[0]), lambda j, i, k: (i, 0)))
            out_shape.append(jax.ShapeDtypeStruct((M, rest[0]), dt))
        elif kind == "side":
            assert n_j == 1
            out_specs.append(pl.BlockSpec((tm, LANES), lambda j, i, k: (i, 0)))
            out_shape.append(jax.ShapeDtypeStruct((M, LANES), dt))
        else:
            out_specs.append(pl.BlockSpec((1, 1, tn), lambda j, i, k: (i, 0, j)))
            out_shape.append(jax.ShapeDtypeStruct((n_i, 1, N), dt))
    scratch = [pltpu.VMEM((tm, tn), F32)] if n_k > 1 else []
    if also is not None:
        in_specs.append(pl.BlockSpec(also.shape, lambda j, i, k: (0, 0), **b_once))
        out_specs.append(pl.BlockSpec((tm, also.shape[1]), lambda j, i, k: (i, 0)))
        out_shape.append(jax.ShapeDtypeStruct((M, also.shape[1]), F32))
    return pl.pallas_call(
        body, name=name, grid=(n_j, n_i, n_k), in_specs=in_specs, out_specs=out_specs, out_shape=out_shape,
        scratch_shapes=scratch, compiler_params=_params(("parallel", "parallel", "arbitrary")),
    )(a, b, *(more or ()), *[ex[1] for ex in extras], *([] if also is None else [also]))


def _rmsnorm_cast(x, g, *, ts, name):
    S, D = x.shape

    def body(x_ref, g_ref, o_ref):
        y, _ = _rms_fwd(x_ref[...], g_ref[...])
        o_ref[...] = y.astype(BF16)

    return pl.pallas_call(
        body, name=name, grid=(S // ts,),
        in_specs=[pl.BlockSpec((ts, D), lambda i: (i, 0)), pl.BlockSpec((1, D), lambda i: (0, 0))],
        out_specs=pl.BlockSpec((ts, D), lambda i: (i, 0)), out_shape=jax.ShapeDtypeStruct((S, D), BF16),
        compiler_params=_params(("parallel",)),
    )(x, g)


LRU_NB = 4


def _lru_gates(xa, wa_ref, ba_ref, wx_ref, bx_ref, lam):
    xab = xa.astype(BF16)
    W = RNN_BLOCK_W
    rs, is_ = [], []
    for j in range(LRU_NB):
        xj = xab[:, j * W:(j + 1) * W]
        rs.append(_sigmoid(jnp.dot(xj, wa_ref[j], preferred_element_type=F32) + ba_ref[j]))
        is_.append(_sigmoid(jnp.dot(xj, wx_ref[j], preferred_element_type=F32) + bx_ref[j]))
    r = jnp.concatenate(rs, axis=1)
    i = jnp.concatenate(is_, axis=1)
    sp = _softplus(-lam)
    log_a = (-LRU_C * r) * sp
    a = jnp.exp(log_a)
    y = 2.0 * log_a
    one_m = jnp.where(y > -0.01, -y * (1.0 + 0.5 * y * (1.0 + y * (1.0 / 3.0))), 1.0 - a * a)
    return r, i, sp, a, jnp.sqrt(one_m)


def _rows_before(x, tail8, k):
    e16 = jnp.concatenate([tail8, x[0:8, :]], axis=0)
    return jnp.concatenate([pltpu.roll(e16, k, 0)[8:16, :], pltpu.roll(x, k, 0)[8:, :]], axis=0)


def _rows_after(x, head8, k):
    tt = x.shape[0]
    e16 = jnp.concatenate([x[tt - 8:tt, :], head8], axis=0)
    return jnp.concatenate([pltpu.roll(x, tt - k, 0)[:tt - 8, :], pltpu.roll(e16, 16 - k, 0)[0:8, :]], axis=0)


def _scan_down(a, b, h0, a_s, b_s, c_s):
    tt, C = a.shape
    G, nch = tt // 8, C // LANES
    rin = lax.broadcasted_iota(jnp.int32, (tt, C), 0) & 7

    def in_group(v, s):
        return pltpu.roll(v.reshape(G, 8, C), s, 1).reshape(tt, C)

    A, B = a, b
    for s in (1, 2, 4):
        B = A * jnp.where(rin >= s, in_group(B, s), 0.0) + B
        A = A * jnp.where(rin >= s, in_group(A, s), 1.0)
    for j in range(nch):
        a_s[j] = A[:, j * LANES:(j + 1) * LANES]
        b_s[j] = B[:, j * LANES:(j + 1) * LANES]
    At = jnp.concatenate([a_s.at[j][pl.ds(7, G, stride=8), :] for j in range(nch)], axis=1)
    Bt = jnp.concatenate([b_s.at[j][pl.ds(7, G, stride=8), :] for j in range(nch)], axis=1)
    rowg = lax.broadcasted_iota(jnp.int32, (G, C), 0)
    s = 1
    while s < G:
        Bt = At * _shift_down(Bt, s, 0.0, rowg) + Bt
        At = At * _shift_down(At, s, 1.0, rowg)
        s *= 2
    hg = At * h0 + Bt
    cin = _shift_down(hg, 1, h0, rowg)
    for j in range(nch):
        for r in range(8):
            c_s.at[j][pl.ds(r, G, stride=8), :] = cin[:, j * LANES:(j + 1) * LANES]
    return A * jnp.concatenate([c_s[j] for j in range(nch)], axis=1) + B, hg[G - 1:G, :]


def _scan_up(c, g_in, g_next, a_s, b_s, c_s):
    tt, C = c.shape
    G, nch = tt // 8, C // LANES
    rin = lax.broadcasted_iota(jnp.int32, (tt, C), 0) & 7

    def in_group(v, s):
        return pltpu.roll(v.reshape(G, 8, C), 8 - s, 1).reshape(tt, C)

    Cc, Gv = c, g_in
    for s in (1, 2, 4):
        Gv = Gv + Cc * jnp.where(rin < 8 - s, in_group(Gv, s), 0.0)
        Cc = Cc * jnp.where(rin < 8 - s, in_group(Cc, s), 1.0)
    for j in range(nch):
        a_s[j] = Cc[:, j * LANES:(j + 1) * LANES]
        b_s[j] = Gv[:, j * LANES:(j + 1) * LANES]
    Ct = jnp.concatenate([a_s.at[j][pl.ds(0, G, stride=8), :] for j in range(nch)], axis=1)
    Gt = jnp.concatenate([b_s.at[j][pl.ds(0, G, stride=8), :] for j in range(nch)], axis=1)
    rowg = lax.broadcasted_iota(jnp.int32, (G, C), 0)
    s = 1
    while s < G:
        Gt = Gt + Ct * _shift_up(Gt, s, 0.0, rowg, G)
        Ct = Ct * _shift_up(Ct, s, 1.0, rowg, G)
        s *= 2
    gg = Gt + Ct * g_next
    cin = _shift_up(gg, 1, g_next, rowg, G)
    for j in range(nch):
        for r in range(8):
            c_s.at[j][pl.ds(r, G, stride=8), :] = cin[:, j * LANES:(j + 1) * LANES]
    return Gv + Cc * jnp.concatenate([c_s[j] for j in range(nch)], axis=1), gg[0:1, :]


def _lru_fwd(z_rx, conv_w, conv_b, wa, ba, wx, bx, lam, *, tt):
    S = z_rx.shape[0]
    n_t = S // tt
    BW = RNN_BLOCK_W
    W = LRU_NB * BW

    def body(x_ref, cw_ref, cb_ref, wa_ref, ba_ref, wx_ref, bx_ref, lam_ref, h_ref, tail, hc, a_s, b_s, c_s):
        t = pl.program_id(1)

        @pl.when(t == 0)
        def _():
            tail[...] = jnp.zeros((8, W), F32)
            hc[...] = jnp.zeros((8, W), F32)

        x = x_ref[...]
        before = tail[...]
        cw = cw_ref[...]
        xa = (cb_ref[...] + cw[3:4] * x + cw[2:3] * _rows_before(x, before, 1) + cw[1:2] * _rows_before(x, before, 2)
              + cw[0:1] * _rows_before(x, before, 3))
        tail[...] = x[tt - 8:tt, :]
        _r, i, _sp, a, mult = _lru_gates(xa, wa_ref, ba_ref, wx_ref, bx_ref, lam_ref[...])
        h, h_last = _scan_down(a, mult * (i * xa), hc[0:1, :], a_s, b_s, c_s)
        h_ref[...] = h.astype(BF16)
        hc[...] = jnp.broadcast_to(h_last, (8, W))

    blk = lambda n, t: (t, n)
    vec = pl.BlockSpec((1, W), lambda n, t: (0, n))
    mat = pl.BlockSpec((LRU_NB, BW, BW), lambda n, t: (n, 0, 0))
    bias = pl.BlockSpec((LRU_NB, 1, BW), lambda n, t: (n, 0, 0))
    row8 = pltpu.VMEM((8, W), F32)
    wide = pltpu.VMEM((LRU_NB, tt, LANES), F32)
    return pl.pallas_call(
        body, name="lru_fwd", grid=(RNN_BLOCKS // LRU_NB, n_t),
        in_specs=[pl.BlockSpec((tt, W), blk), pl.BlockSpec((CONV_WIDTH, W), lambda n, t: (0, n)), vec, mat, bias, mat,
                  bias, vec],
        out_specs=pl.BlockSpec((tt, W), blk), out_shape=jax.ShapeDtypeStruct((S, D_MODEL), BF16),
        scratch_shapes=[row8, row8, wide, wide, wide],
        compiler_params=_params(("parallel", "arbitrary")),
    )(z_rx, conv_w, conv_b, wa, ba, wx, bx, lam)


def _lru_bwd(z_rx, h, dh, dz, conv_w, conv_b, wa, wat, ba, wx, wxt, bx, lam, *, tt):
    S = z_rx.shape[0]
    n_t = S // tt
    BW = RNN_BLOCK_W
    W = LRU_NB * BW
    t8 = tt // 8

    def body(x_ref, xp_ref, h_ref, hp_ref, dh_ref, _dz_ref, cw_ref, cb_ref, wa_ref, wat_ref, ba_ref, wx_ref, wxt_ref,
             bx_ref, lam_ref, dx_ref, dwa_ref, dwx_ref, dba_ref, dbx_ref, dlam_ref, dcw_ref, dcb_ref, nxt, a_c, g_c, a_s,
             b_s, c_s):
        t = pl.program_id(1)
        tile = n_t - 1 - t

        @pl.when(t == 0)
        def _():
            a_c[...] = jnp.zeros((8, W), F32)
            g_c[...] = jnp.zeros((8, W), F32)
            nxt[...] = jnp.zeros((8, W), F32)
            dwa_ref[...] = jnp.zeros_like(dwa_ref)
            dwx_ref[...] = jnp.zeros_like(dwx_ref)
            dba_ref[...] = jnp.zeros_like(dba_ref)
            dbx_ref[...] = jnp.zeros_like(dbx_ref)
            dlam_ref[...] = jnp.zeros_like(dlam_ref)
            dcw_ref[...] = jnp.zeros_like(dcw_ref)
            dcb_ref[...] = jnp.zeros_like(dcb_ref)

        has_prev = (tile > 0).astype(F32)
        x = x_ref[...]
        before = xp_ref[...] * has_prev
        xm1, xm2, xm3 = _rows_before(x, before, 1), _rows_before(x, before, 2), _rows_before(x, before, 3)
        cw = cw_ref[...]
        xa = cb_ref[...] + cw[3:4] * x + cw[2:3] * xm1 + cw[1:2] * xm2 + cw[0:1] * xm3
        lam = lam_ref[...]
        r, i, sp, a, mult = _lru_gates(xa, wa_ref, ba_ref, wx_ref, bx_ref, lam)
        gated = i * xa
        h_prev = _rows_before(h_ref[...].astype(F32), hp_ref[8:16, :].astype(F32) * has_prev, 1)
        g, g_first = _scan_up(_rows_after(a, a_c[...], 1), dh_ref[...], g_c[0:1, :], a_s, b_s, c_s)
        a_c[...] = jnp.broadcast_to(a[0:1, :], (8, W))
        g_c[...] = jnp.broadcast_to(g_first, (8, W))
        dlog_a = g * h_prev * a - g * gated * (a * a) / mult
        dgated = g * mult
        di = dgated * xa
        dxa = dgated * i
        dr = dlog_a * (-LRU_C * sp)
        dlam_ref[...] += jnp.sum(dlog_a * (-LRU_C * r), axis=0, keepdims=True) * (-_sigmoid(-lam))
        dpr = dr * r * (1.0 - r)
        dpi = di * i * (1.0 - i)
        xab, dprb, dpib = xa.astype(BF16), dpr.astype(BF16), dpi.astype(BF16)
        tn_dims = (((0,), (0,)), ((), ()))
        back = []
        for j in range(LRU_NB):
            sl = slice(j * BW, (j + 1) * BW)
            dwa_ref[j] += lax.dot_general(xab[:, sl], dprb[:, sl], tn_dims, preferred_element_type=F32)
            dwx_ref[j] += lax.dot_general(xab[:, sl], dpib[:, sl], tn_dims, preferred_element_type=F32)
            dba_ref[j] += jnp.sum(dpr[:, sl], axis=0, keepdims=True)
            dbx_ref[j] += jnp.sum(dpi[:, sl], axis=0, keepdims=True)
            back.append(jnp.dot(dprb[:, sl], wat_ref[j], preferred_element_type=F32)
                        + jnp.dot(dpib[:, sl], wxt_ref[j], preferred_element_type=F32))
        dxa = dxa + jnp.concatenate(back, axis=1)
        after = nxt[...]
        dx = (cw[3:4] * dxa + cw[2:3] * _rows_after(dxa, after, 1) + cw[1:2] * _rows_after(dxa, after, 2)
              + cw[0:1] * _rows_after(dxa, after, 3))
        nxt[...] = dxa[0:8, :]
        dx_ref[...] = dx.astype(BF16)
        dcw_ref[3:4, :] += jnp.sum(dxa * x, axis=0, keepdims=True)
        dcw_ref[2:3, :] += jnp.sum(dxa * xm1, axis=0, keepdims=True)
        dcw_ref[1:2, :] += jnp.sum(dxa * xm2, axis=0, keepdims=True)
        dcw_ref[0:1, :] += jnp.sum(dxa * xm3, axis=0, keepdims=True)
        dcb_ref[...] += jnp.sum(dxa, axis=0, keepdims=True)

    blk = lambda n, t: (n_t - 1 - t, n)
    prev = lambda n, t: (jnp.maximum((n_t - 1 - t) * t8 - 1, 0), n)
    vec = pl.BlockSpec((1, W), lambda n, t: (0, n))
    mat = pl.BlockSpec((LRU_NB, BW, BW), lambda n, t: (n, 0, 0))
    bias = pl.BlockSpec((LRU_NB, 1, BW), lambda n, t: (n, 0, 0))
    cws = pl.BlockSpec((CONV_WIDTH, W), lambda n, t: (0, n))
    tile = pl.BlockSpec((tt, W), blk)
    prev8 = pl.BlockSpec((8, W), prev)
    prev16 = pl.BlockSpec((16, W), lambda n, t: (jnp.maximum((n_t - 1 - t) * (tt // 16) - 1, 0), n))
    row8 = pltpu.VMEM((8, W), F32)
    wide = pltpu.VMEM((LRU_NB, tt, LANES), F32)
    return pl.pallas_call(
        body, name="lru_bwd", grid=(RNN_BLOCKS // LRU_NB, n_t),
        in_specs=[tile, prev8, tile, prev16, tile, pl.BlockSpec(memory_space=pl.ANY), cws, vec, mat, mat, bias, mat, mat,
                  bias, vec],
        out_specs=[tile, mat, mat, bias, bias, vec, cws, vec], input_output_aliases={5: 0},
        out_shape=[jax.ShapeDtypeStruct(dz.shape, BF16),
                   jax.ShapeDtypeStruct((RNN_BLOCKS, BW, BW), F32), jax.ShapeDtypeStruct((RNN_BLOCKS, BW, BW), F32),
                   jax.ShapeDtypeStruct((RNN_BLOCKS, 1, BW), F32), jax.ShapeDtypeStruct((RNN_BLOCKS, 1, BW), F32),
                   jax.ShapeDtypeStruct((1, D_MODEL), F32),
                   jax.ShapeDtypeStruct((CONV_WIDTH, D_MODEL), F32), jax.ShapeDtypeStruct((1, D_MODEL), F32)],
        scratch_shapes=[row8, row8, row8, wide, wide, wide],
        compiler_params=_params(("parallel", "arbitrary")),
    )(z_rx, z_rx, h, h, dh, dz, conv_w, conv_b, wa, wat, ba, wx, wxt, bx, lam)


def _mla_proj(z_ckv, q_norm, kv_norm, w_uq, w_ukv, cos, sin, *, ts):
    S = z_ckv.shape[0]
    H = MLA_HEADS

    def body(c_ref, qn_ref, kn_ref, wq_ref, wkv_ref, cos_ref, sin_ref, q_ref, k_ref, v_ref):
        c = c_ref[...]
        cqn, _ = _rms_fwd(c[:, 0:Q_LORA], qn_ref[...])
        ckn, _ = _rms_fwd(c[:, Q_LORA:Q_LORA + KV_LORA], kn_ref[...])
        q = jnp.dot(cqn.astype(BF16), wq_ref[...], preferred_element_type=F32) * (ATTN_SCALE * LOG2E)
        kv = jnp.dot(ckn.astype(BF16), wkv_ref[...], preferred_element_type=F32)
        cos1, sin1 = cos_ref[...], sin_ref[...]
        cos8 = jnp.concatenate([cos1] * H, axis=1)
        sin8 = jnp.concatenate([sin1] * H, axis=1)
        qr = q[:, H * QK_NOPE:]
        lane8 = lax.broadcasted_iota(jnp.int32, qr.shape, 1)
        qr = qr * cos8 + _rot_half(qr, lane8) * sin8
        kr = c[:, Q_LORA + KV_LORA:]
        lane1 = lax.broadcasted_iota(jnp.int32, kr.shape, 1)
        kr = (kr * cos1 + _rot_half(kr, lane1) * sin1).astype(BF16)
        for h in range(H):
            q_ref[h, :, 0:QK_NOPE] = q[:, h * QK_NOPE:(h + 1) * QK_NOPE].astype(BF16)
            q_ref[h, :, QK_NOPE:] = qr[:, h * LANES:(h + 1) * LANES].astype(BF16)
            k_ref[h, :, 0:QK_NOPE] = kv[:, h * 2 * LANES:h * 2 * LANES + LANES].astype(BF16)
            k_ref[h, :, QK_NOPE:] = kr
            v_ref[h] = kv[:, h * 2 * LANES + LANES:(h + 1) * 2 * LANES].astype(BF16)

    full = lambda shape: pl.BlockSpec(shape, lambda i: (0,) * len(shape))
    return pl.pallas_call(
        body, name="mla_proj", grid=(S // ts,),
        in_specs=[pl.BlockSpec((ts, CKV_W), lambda i: (i, 0)), full((1, Q_LORA)), full((1, KV_LORA)),
                  full(w_uq.shape), full(w_ukv.shape), pl.BlockSpec((ts, LANES), lambda i: (i, 0)),
                  pl.BlockSpec((ts, LANES), lambda i: (i, 0))],
        out_specs=[pl.BlockSpec((H, ts, QK_PAD), lambda i: (0, i, 0)), pl.BlockSpec((H, ts, QK_PAD), lambda i: (0, i, 0)),
                   pl.BlockSpec((H, ts, V_HEAD), lambda i: (0, i, 0))],
        out_shape=[jax.ShapeDtypeStruct((H, S, QK_PAD), BF16), jax.ShapeDtypeStruct((H, S, QK_PAD), BF16),
                   jax.ShapeDtypeStruct((H, S, V_HEAD), BF16)],
        compiler_params=_params(("parallel",)),
    )(z_ckv, q_norm, kv_norm, w_uq, w_ukv, cos, sin)


def _mla_proj_bwd(z_ckv, dq, dk, dv, q_norm, kv_norm, w_uqt, w_ukvt, cos, sin, *, ts):
    S = z_ckv.shape[0]
    H = MLA_HEADS

    def body(c_ref, dq_ref, dk_ref, dv_ref, qn_ref, kn_ref, wqt_ref, wkvt_ref, cos_ref, sin_ref,
             dz_ref, dwq_ref, dwkv_ref, dqn_ref, dkn_ref):
        @pl.when(pl.program_id(0) == 0)
        def _():
            dwq_ref[...] = jnp.zeros_like(dwq_ref)
            dwkv_ref[...] = jnp.zeros_like(dwkv_ref)
            dqn_ref[...] = jnp.zeros_like(dqn_ref)
            dkn_ref[...] = jnp.zeros_like(dkn_ref)

        c = c_ref[...]
        cq, ck = c[:, 0:Q_LORA], c[:, Q_LORA:Q_LORA + KV_LORA]
        qn, kn = qn_ref[...], kn_ref[...]
        cqn, _ = _rms_fwd(cq, qn)
        ckn, _ = _rms_fwd(ck, kn)
        cos1, sin1 = cos_ref[...], sin_ref[...]
        lane1 = lax.broadcasted_iota(jnp.int32, cos1.shape, 1)

        def unrope(g):
            return g * cos1 - _rot_half(g * sin1, lane1)

        dq_all = jnp.concatenate([dq_ref[h, :, 0:QK_NOPE] for h in range(H)]
                                 + [unrope(dq_ref[h, :, QK_NOPE:]) for h in range(H)], axis=1)
        dq_all = (dq_all * ATTN_SCALE).astype(BF16)
        dkv_all = jnp.concatenate([p for h in range(H) for p in (dk_ref[h, :, 0:QK_NOPE], dv_ref[h])],
                                  axis=1).astype(BF16)
        dkr = dk_ref[0, :, QK_NOPE:].astype(F32)
        for h in range(1, H):
            dkr = dkr + dk_ref[h, :, QK_NOPE:].astype(F32)
        dkr = unrope(dkr)
        tn_dims = (((0,), (0,)), ((), ()))
        dwq_ref[...] += lax.dot_general(cqn.astype(BF16), dq_all, tn_dims, preferred_element_type=F32)
        dwkv_ref[...] += lax.dot_general(ckn.astype(BF16), dkv_all, tn_dims, preferred_element_type=F32)
        dcqn = jnp.dot(dq_all, wqt_ref[...], preferred_element_type=F32)
        dckn = jnp.dot(dkv_all, wkvt_ref[...], preferred_element_type=F32)
        dcq, dqn_rows = _rms_bwd(dcqn, cq, qn)
        dck, dkn_rows = _rms_bwd(dckn, ck, kn)
        dqn_ref[...] += jnp.sum(dqn_rows, axis=0, keepdims=True)
        dkn_ref[...] += jnp.sum(dkn_rows, axis=0, keepdims=True)
        dz_ref[:, 0:Q_LORA] = dcq.astype(BF16)
        dz_ref[:, Q_LORA:Q_LORA + KV_LORA] = dck.astype(BF16)
        dz_ref[:, Q_LORA + KV_LORA:] = dkr.astype(BF16)

    full = lambda shape: pl.BlockSpec(shape, lambda i: (0,) * len(shape))
    return pl.pallas_call(
        body, name="mla_proj_bwd", grid=(S // ts,),
        in_specs=[pl.BlockSpec((ts, CKV_W), lambda i: (i, 0)), pl.BlockSpec((H, ts, QK_PAD), lambda i: (0, i, 0)),
                  pl.BlockSpec((H, ts, QK_PAD), lambda i: (0, i, 0)), pl.BlockSpec((H, ts, V_HEAD), lambda i: (0, i, 0)),
                  full((1, Q_LORA)), full((1, KV_LORA)), full(w_uqt.shape), full(w_ukvt.shape),
                  pl.BlockSpec((ts, LANES), lambda i: (i, 0)), pl.BlockSpec((ts, LANES), lambda i: (i, 0))],
        out_specs=[pl.BlockSpec((ts, CKV_W), lambda i: (i, 0)), full((Q_LORA, w_uqt.shape[0])),
                   full((KV_LORA, w_ukvt.shape[0])), full((1, Q_LORA)), full((1, KV_LORA))],
        out_shape=[jax.ShapeDtypeStruct((S, CKV_W), BF16), jax.ShapeDtypeStruct((Q_LORA, w_uqt.shape[0]), F32),
                   jax.ShapeDtypeStruct((KV_LORA, w_ukvt.shape[0]), F32), jax.ShapeDtypeStruct((1, Q_LORA), F32),
                   jax.ShapeDtypeStruct((1, KV_LORA), F32)],
        compiler_params=_params(("arbitrary",)),
    )(z_ckv, dq, dk, dv, q_norm, kv_norm, w_uqt, w_ukvt, cos, sin)


NT_DIMS = (((1,), (1,)), ((), ()))
TN_DIMS = (((0,), (0,)), ((), ()))


def _attn_fwd(q, k, v, *, t, hb):
    H, S, _ = q.shape
    n = S // t
    pairs = [(i, j) for i in range(n) for j in range(i + 1)]
    qi = jnp.asarray(np.array([p[0] for p in pairs], np.int32))
    ki = jnp.asarray(np.array([p[1] for p in pairs], np.int32))

    def body(qi_ref, ki_ref, q_ref, k_ref, v_ref, o_ref, lse_ref, m_s, l_s, acc_s):
        p = pl.program_id(1)
        i, j = qi_ref[p], ki_ref[p]

        @pl.when(j == 0)
        def _():
            m_s[...] = jnp.full(m_s.shape, NEG, F32)
            l_s[...] = jnp.zeros(l_s.shape, F32)
            acc_s[...] = jnp.zeros(acc_s.shape, F32)

        def block(hh, r0, nr, nk, masked):
            rows = slice(r0, r0 + nr)
            s = lax.dot_general(q_ref[hh, rows, :], k_ref[hh, 0:nk, :], NT_DIMS, preferred_element_type=F32)
            if masked:
                row = lax.broadcasted_iota(jnp.int32, (nr, nk), 0) + r0
                col = lax.broadcasted_iota(jnp.int32, (nr, nk), 1)
                s = jnp.where(row >= col, s, NEG)
            chunks = nk // LANES
            mc = s[:, 0:LANES]
            for c in range(1, chunks):
                mc = jnp.maximum(mc, s[:, c * LANES:(c + 1) * LANES])
            m_prev = m_s[hh, rows, :]
            m_new = jnp.maximum(m_prev, jnp.max(mc, axis=1, keepdims=True))
            alpha = jnp.exp2(m_prev - m_new)
            pr = jnp.exp2(s - jnp.concatenate([m_new] * chunks, axis=1))
            ls = pr[:, 0:LANES]
            for c in range(1, chunks):
                ls = ls + pr[:, c * LANES:(c + 1) * LANES]
            l_s[hh, rows, :] = alpha * l_s[hh, rows, :] + ls
            acc_s[hh, rows, :] = alpha * acc_s[hh, rows, :] + jnp.dot(pr.astype(BF16), v_ref[hh, 0:nk, :],
                                                                      preferred_element_type=F32)
            m_s[hh, rows, :] = m_new

        def step(diagonal):
            for hh in range(hb):
                if diagonal:
                    block(hh, 0, t // 2, t // 2, True)
                    block(hh, t // 2, t // 2, t, True)
                else:
                    block(hh, 0, t, t, False)

        @pl.when(j < i)
        def _():
            step(False)

        @pl.when(j == i)
        def _():
            step(True)
            for hh in range(hb):
                l = jnp.sum(l_s[hh], axis=1, keepdims=True)
                o_ref[:, hh * V_HEAD:(hh + 1) * V_HEAD] = acc_s[hh] / l
                lse_ref[hh] = (m_s[hh] + jnp.log2(l)).T[0:1, :]

    grid_spec = pltpu.PrefetchScalarGridSpec(
        num_scalar_prefetch=2, grid=(H // hb, len(pairs)),
        in_specs=[pl.BlockSpec((hb, t, QK_PAD), lambda h, p, qi, ki: (h, qi[p], 0)),
                  pl.BlockSpec((hb, t, QK_PAD), lambda h, p, qi, ki: (h, ki[p], 0)),
                  pl.BlockSpec((hb, t, V_HEAD), lambda h, p, qi, ki: (h, ki[p], 0))],
        out_specs=[pl.BlockSpec((t, hb * V_HEAD), lambda h, p, qi, ki: (qi[p], h)),
                   pl.BlockSpec((hb, 1, t), lambda h, p, qi, ki: (h, 0, qi[p]))],
        scratch_shapes=[pltpu.VMEM((hb, t, LANES), F32), pltpu.VMEM((hb, t, LANES), F32),
                        pltpu.VMEM((hb, t, V_HEAD), F32)],
    )
    return pl.pallas_call(
        body, name="attn_fwd", grid_spec=grid_spec,
        out_shape=[jax.ShapeDtypeStruct((S, H * V_HEAD), F32), jax.ShapeDtypeStruct((H, 1, S), F32)],
        compiler_params=_params(("parallel", "arbitrary")),
    )(qi, ki, q, k, v)


def _attn_bwd(q, k, v, do, lse_row, delta_row, *, t):
    H, S, _ = q.shape
    n = S // t
    pairs = [(i, j) for j in range(n) for i in range(j, n)]
    qi = jnp.asarray(np.array([p[0] for p in pairs], np.int32))
    ki = jnp.asarray(np.array([p[1] for p in pairs], np.int32))

    def body(qi_ref, ki_ref, q_ref, k_ref, v_ref, do_ref, lse_ref, dl_ref, dq_ref, dk_ref, dv_ref, dk_s, dv_s, dq_s):
        p = pl.program_id(1)
        i, j = qi_ref[p], ki_ref[p]

        @pl.when(p == 0)
        def _():
            dq_s[...] = jnp.zeros_like(dq_s)

        def block(k0, nk, q0, nq, masked):
            qb, dob = q_ref[0, q0:q0 + nq, :], do_ref[q0:q0 + nq, :]
            kb, vb = k_ref[0, k0:k0 + nk, :], v_ref[0, k0:k0 + nk, :]
            st = lax.dot_general(kb, qb, NT_DIMS, preferred_element_type=F32)
            if masked:
                krow = lax.broadcasted_iota(jnp.int32, (nk, nq), 0) + k0
                qcol = lax.broadcasted_iota(jnp.int32, (nk, nq), 1) + q0
                st = jnp.where(krow <= qcol, st, NEG)
            pt = jnp.exp2(st - lse_ref[0][:, q0:q0 + nq])
            dvp = jnp.dot(pt.astype(BF16), dob, preferred_element_type=F32)
            dpt = lax.dot_general(vb, dob, NT_DIMS, preferred_element_type=F32)
            dst = (pt * (dpt - dl_ref[0][:, q0:q0 + nq])).astype(BF16)
            dkp = jnp.dot(dst, qb, preferred_element_type=F32)
            rows = pl.ds(pl.multiple_of(i * t + q0, LANES), nq)
            dq_s[rows, :] += lax.dot_general(dst, kb, TN_DIMS, preferred_element_type=F32)
            return dkp, dvp

        @pl.when(i == j)
        def _():
            half = t // 2
            dk_s[0:half, :], dv_s[0:half, :] = block(0, half, 0, t, True)
            dk_s[half:t, :], dv_s[half:t, :] = block(half, half, half, half, True)

        @pl.when(i != j)
        def _():
            dkp, dvp = block(0, t, 0, t, False)
            dk_s[...] += dkp
            dv_s[...] += dvp

        @pl.when(i == n - 1)
        def _():
            dk_ref[0] = (dk_s[...] * LN2).astype(BF16)
            dv_ref[0] = dv_s[...].astype(BF16)

        @pl.when(p == len(pairs) - 1)
        def _():
            dq_ref[0] = dq_s[...].astype(BF16)

    grid_spec = pltpu.PrefetchScalarGridSpec(
        num_scalar_prefetch=2, grid=(H, len(pairs)),
        in_specs=[pl.BlockSpec((1, t, QK_PAD), lambda h, p, qi, ki: (h, qi[p], 0)),
                  pl.BlockSpec((1, t, QK_PAD), lambda h, p, qi, ki: (h, ki[p], 0)),
                  pl.BlockSpec((1, t, V_HEAD), lambda h, p, qi, ki: (h, ki[p], 0)),
                  pl.BlockSpec((t, V_HEAD), lambda h, p, qi, ki: (qi[p], h)),
                  pl.BlockSpec((1, 1, t), lambda h, p, qi, ki: (h, 0, qi[p])),
                  pl.BlockSpec((1, 1, t), lambda h, p, qi, ki: (h, 0, qi[p]))],
        out_specs=[pl.BlockSpec((1, S, QK_PAD), lambda h, p, qi, ki: (h, 0, 0)),
                   pl.BlockSpec((1, t, QK_PAD), lambda h, p, qi, ki: (h, ki[p], 0)),
                   pl.BlockSpec((1, t, V_HEAD), lambda h, p, qi, ki: (h, ki[p], 0))],
        scratch_shapes=[pltpu.VMEM((t, QK_PAD), F32), pltpu.VMEM((t, V_HEAD), F32), pltpu.VMEM((S, QK_PAD), F32)],
    )
    return pl.pallas_call(
        body, name="attn_bwd", grid_spec=grid_spec,
        out_shape=[jax.ShapeDtypeStruct((H, S, QK_PAD), BF16), jax.ShapeDtypeStruct((H, S, QK_PAD), BF16),
                   jax.ShapeDtypeStruct((H, S, V_HEAD), BF16)],
        compiler_params=_params(("parallel", "arbitrary")),
    )(qi, ki, q, k, v, do, lse_row, delta_row)


def _merge_h1(h, z_gates, o, x, w_out, norm_mlp, *, ts):
    S = h.shape[0]
    D = D_MODEL

    def body(h_ref, rg_ref, ga_ref, gb_ref, o_ref, x_ref, w_ref, g_ref, m_ref, h1_ref, n2_ref):
        gl, _ = _gelu_and_grad(rg_ref[...].astype(F32))
        m = (_sigmoid(ga_ref[...].astype(F32)) * (h_ref[...].astype(F32) * gl)
             + _sigmoid(gb_ref[...].astype(F32)) * o_ref[...]).astype(BF16)
        m_ref[...] = m
        h1 = x_ref[...] + jnp.dot(m, w_ref[...], preferred_element_type=F32)
        h1_ref[...] = h1
        n2, _ = _rms_fwd(h1, g_ref[...])
        n2_ref[...] = n2.astype(BF16)

    col = lambda c: pl.BlockSpec((ts, D), lambda i: (i, c))
    fixed = lambda shape: pl.BlockSpec(shape, lambda i: (0, 0), pipeline_mode=pl.Buffered(1))
    return pl.pallas_call(
        body, name="merge_h1", grid=(S // ts,),
        in_specs=[col(0), col(0), col(1), col(2), col(0), col(0), fixed((D, D)), fixed((1, D))],
        out_specs=[col(0), col(0), col(0)],
        out_shape=[jax.ShapeDtypeStruct((S, D), BF16), jax.ShapeDtypeStruct((S, D), F32),
                   jax.ShapeDtypeStruct((S, D), BF16)],
        compiler_params=_params(("parallel",)),
    )(h, z_gates, z_gates, z_gates, o, x, w_out, norm_mlp)


def _my_place():
    return lax.axis_index("x"), lax.axis_index("y"), lax.axis_index("c")


def _all_gather(shards, *, name):
    n = len(shards)

    def body(*refs):
        x_refs, out_refs = refs[:n], refs[n:2 * n]
        send_sems, recv_sems, local_sems = refs[2 * n:]
        x, y, c = _my_place()
        me, sibling = (x, y, c), (x, y, 1 - c)
        chips = [(1 - x, y), (x, 1 - y), (1 - x, 1 - y)]

        def slot(a, px, py, pc):
            return out_refs[a].at[4 * px + 2 * py + pc]

        def copy(a, k, block, to, src=None):
            return pltpu.make_async_remote_copy(
                src_ref=slot(a, *block) if src is None else src, dst_ref=slot(a, *block),
                send_sem=send_sems.at[7 * a + k], recv_sem=recv_sems.at[7 * a + k], device_id=to, device_id_type=MESH)

        mine = [pltpu.make_async_copy(x_refs[a], slot(a, *me), local_sems.at[a]) for a in range(n)]
        for cp in mine:
            cp.start()
        first = []
        for a in range(n):
            first.append(copy(a, 0, me, sibling, src=x_refs[a]))
            first += [copy(a, 1 + j, me, (*chip, c), src=x_refs[a]) for j, chip in enumerate(chips)]
        for cp in first:
            cp.start()
        passed = []
        for a in range(n):
            for j, chip in enumerate(chips):
                copy(a, 1 + j, (*chip, c), me).wait_recv()
                fwd = copy(a, 4 + j, (*chip, c), sibling)
                fwd.start()
                passed.append(fwd)
        for a in range(n):
            copy(a, 0, sibling, me).wait_recv()
            for j, chip in enumerate(chips):
                copy(a, 4 + j, (*chip, 1 - c), me).wait_recv()
        for cp in first + passed:
            cp.wait_send()
        for cp in mine:
            cp.wait()

    hbm = pl.BlockSpec(memory_space=pl.ANY)
    return pl.pallas_call(
        body, name=name, out_shape=[jax.ShapeDtypeStruct((N_DEV, *s.shape), s.dtype) for s in shards],
        in_specs=[hbm] * n, out_specs=[hbm] * n,
        scratch_shapes=[pltpu.SemaphoreType.DMA((7 * n,)), pltpu.SemaphoreType.DMA((7 * n,)),
                        pltpu.SemaphoreType.DMA((n,))],
    )(*shards)


def _pushes(src_refs, land_refs, send_sems, recv_sems, slab_per_peer):
    x, y, c = _my_place()
    me = 4 * x + 2 * y + c
    copies = []
    for a in range(len(src_refs)):
        for k in range(1, N_DEV):
            px, py, pc = x ^ (k >> 2), y ^ ((k >> 1) & 1), c ^ (k & 1)
            src = src_refs[a].at[4 * px + 2 * py + pc] if slab_per_peer else src_refs[a]
            copies.append(pltpu.make_async_remote_copy(
                src_ref=src, dst_ref=land_refs[a].at[me], send_sem=send_sems.at[7 * a + k - 1],
                recv_sem=recv_sems.at[7 * a + k - 1], device_id=(px, py, pc), device_id_type=MESH))
    return copies


def _push_start(srcs, *, name, slab_per_peer):
    n = len(srcs)
    lands = [lax.empty((N_DEV, *(s.shape[1:] if slab_per_peer else s.shape)), s.dtype) for s in srcs]

    def body(*refs):
        src_refs, land_refs = refs[:n], refs[n:2 * n]
        send_sems, recv_sems, token = refs[2 * n], refs[2 * n + 1], refs[-1]
        for cp in _pushes(src_refs, land_refs, send_sems, recv_sems, slab_per_peer):
            cp.start()
        token[...] = jnp.zeros_like(token)

    hbm = pl.BlockSpec(memory_space=pltpu.HBM)
    sem = pl.BlockSpec(memory_space=pltpu.SEMAPHORE)
    out = pl.pallas_call(
        body, name=name,
        out_shape=(pltpu.SemaphoreType.DMA((7 * n,)), pltpu.SemaphoreType.DMA((7 * n,)),
                   *[pltpu.HBM(a.shape, a.dtype) for a in srcs + lands], jax.ShapeDtypeStruct((8, LANES), F32)),
        in_specs=[hbm] * (2 * n), out_specs=(sem, sem, *[hbm] * (2 * n), pl.BlockSpec(memory_space=pltpu.VMEM)),
        input_output_aliases={i: 2 + i for i in range(2 * n)},
        compiler_params=pltpu.CompilerParams(has_side_effects=pltpu.SideEffectType.DATAFLOW_SIDE_EFFECTING),
    )(*[pltpu.with_memory_space_constraint(a, pltpu.HBM) for a in srcs + lands])
    return out[0], out[1], list(out[2:2 + n]), list(out[2 + n:2 + 2 * n]), out[-1]


def _push_wait(send_sems, recv_sems, srcs, lands, after, *, name, slab_per_peer):
    n = len(srcs)

    def body(*refs):
        src_refs, land_refs = refs[:n], refs[n:2 * n]
        s_sems, r_sems = refs[2 * n], refs[2 * n + 1]
        for cp in _pushes(src_refs, land_refs, s_sems, r_sems, slab_per_peer):
            cp.wait_send()
            cp.wait_recv()

    hbm = pl.BlockSpec(memory_space=pltpu.HBM)
    sem = pl.BlockSpec(memory_space=pltpu.SEMAPHORE)
    out = pl.pallas_call(
        body, name=name, out_shape=tuple(pltpu.HBM(a.shape, a.dtype) for a in srcs + lands),
        in_specs=[hbm] * (2 * n) + [sem, sem, pl.BlockSpec(memory_space=pl.ANY)], out_specs=tuple([hbm] * (2 * n)),
        input_output_aliases={i: i for i in range(2 * n)},
        compiler_params=pltpu.CompilerParams(has_side_effects=pltpu.SideEffectType.DATAFLOW_SIDE_EFFECTING),
    )(*srcs, *lands, send_sems, recv_sems, after)
    return list(out[:n]), list(out[n:])


def _sum_parts(gp_ref, rows):
    g = gp_ref[0, 0:rows, :].astype(F32)
    for p in range(1, gp_ref.shape[0]):
        g = g + gp_ref[p, 0:rows, :].astype(F32)
    return g


def _adamw_update(w, m, v, g):
    m_new = ADAM_B1 * m + (1.0 - ADAM_B1) * g
    v_new = ADAM_B2 * v + (1.0 - ADAM_B2) * (g * g)
    m_hat = m_new / (1.0 - ADAM_B1 ** ADAM_STEP)
    v_hat = v_new / (1.0 - ADAM_B2 ** ADAM_STEP)
    return -ADAM_LR * (m_hat / (jnp.sqrt(v_hat) + ADAM_EPS) + ADAM_WD * w), m_new, v_new


def _adamw_many(ws, ms, vs, gparts, sums, *, name):
    n, k = len(ws), len(sums)

    def body(*refs):
        w_refs, m_refs, v_refs = refs[:n], refs[n:2 * n], refs[2 * n:3 * n]
        g_refs, s_refs, outs = refs[3 * n:4 * n], refs[4 * n:4 * n + k], refs[4 * n + k:]
        for a in range(n):
            g = _sum_parts(g_refs[a], w_refs[a].shape[0])
            d, m_new, v_new = _adamw_update(w_refs[a][...], m_refs[a][...], v_refs[a][...], g)
            for o_ref, val in zip(outs[4 * a:4 * a + 4], (g, d, m_new, v_new)):
                o_ref[...] = val
        for b in range(k):
            outs[4 * n + b][...] = _sum_parts(s_refs[b], s_refs[b].shape[1])

    out_shape = [jax.ShapeDtypeStruct(w.shape, F32) for w in ws for _ in range(4)]
    out_shape += [jax.ShapeDtypeStruct(s.shape[1:], F32) for s in sums]
    return pl.pallas_call(body, name=name, out_shape=out_shape, compiler_params=_params())(
        *ws, *ms, *vs, *gparts, *sums)


def _adamw(w, m, v, gparts, *, tr, name):
    R, C = w.shape
    n_parts = gparts.shape[0]

    def body(w_ref, m_ref, v_ref, gp_ref, g_ref, d_ref, nm_ref, nv_ref):
        g = _sum_parts(gp_ref, tr)
        d_ref[...], nm_ref[...], nv_ref[...] = _adamw_update(w_ref[...], m_ref[...], v_ref[...], g)
        g_ref[...] = g

    row = pl.BlockSpec((tr, C), lambda i: (i, 0))
    shp = jax.ShapeDtypeStruct((R, C), F32)
    return pl.pallas_call(
        body, name=name, grid=(R // tr,),
        in_specs=[row, row, row, pl.BlockSpec((n_parts, tr, C), lambda i: (0, i, 0))],
        out_specs=[row, row, row, row], out_shape=[shp, shp, shp, shp],
        compiler_params=_params(("parallel",)),
    )(w, m, v, gparts)


def _rope_tables(s):
    pos = jnp.arange(s, dtype=F32)
    inv_freq = 1.0 / (ROPE_THETA ** (jnp.arange(0, QK_ROPE, 2, dtype=F32) / QK_ROPE))
    per_lane = jnp.concatenate([inv_freq, inv_freq, jnp.zeros((LANES - QK_ROPE,), F32)])
    ang = pos[:, None] * per_lane[None, :]
    live = jnp.arange(LANES) < QK_ROPE
    return jnp.where(live, jnp.cos(ang), 0.0), jnp.where(live, jnp.sin(ang), 0.0)


def _pick(n, want):
    t = min(n, want)
    assert n % t == 0
    return t


def _local_step(x, target, wts, small, hooks):
    S = x.shape[0]
    H = MLA_HEADS
    ts = _pick(S, 1024)
    tm = _pick(S, 512)
    tm_wide = _pick(S, 1024)
    tk_s = _pick(S, 4096)
    tt = _pick(S, 512)
    ta = _pick(S, 1024)
    ts_proj = _pick(S, 512)
    ts_merge = _pick(S, 512)
    row = lambda v: v.reshape(1, -1)
    w_in = wts["w_in"]
    w_main = jnp.concatenate([w_in[:, 0:2048], w_in[:, 2624:4672]], axis=1)
    w_ckv = jnp.concatenate([w_in[:, 2048:2624], jnp.zeros((D_MODEL, CKV_W - 576), BF16)], axis=1)
    w_uq3 = wts["w_uq"].reshape(Q_LORA, H, QK_NOPE + QK_ROPE)
    w_uq_p = jnp.concatenate(
        [w_uq3[:, :, :QK_NOPE].reshape(Q_LORA, H * QK_NOPE),
         jnp.pad(w_uq3[:, :, QK_NOPE:], ((0, 0), (0, 0), (0, LANES - QK_ROPE))).reshape(Q_LORA, H * LANES)], axis=1)
    w_ukv = wts["w_ukv"]
    cos, sin = _rope_tables(S)
    conv_w, conv_b = small["conv_w"], row(small["conv_b"])
    wa, wx = small["lru_wa"].astype(BF16), small["lru_wx"].astype(BF16)
    wat, wxt = jnp.swapaxes(wa, 1, 2), jnp.swapaxes(wx, 1, 2)
    ba, bx = small["lru_ba"].reshape(RNN_BLOCKS, 1, RNN_BLOCK_W), small["lru_bx"].reshape(RNN_BLOCKS, 1, RNN_BLOCK_W)
    lam = row(small["lru_lambda"])
    q_norm, kv_norm = row(small["q_norm"]), row(small["kv_norm"])
    norm_mix, norm_mlp, norm_final = row(small["norm_mix"]), row(small["norm_mlp"]), row(small["norm_final"])

    xn = _rmsnorm_cast(x, norm_mix, ts=ts, name="norm_mix")
    ident = lambda acc: (acc,)
    z_rx, z_ckv = _mm(xn, w_main[:, :D_MODEL], name="z_rx_ckv", tm=tm_wide, tn=1024, tk=1024, outs=[("tile", F32)],
                      epilogue=ident, also=w_ckv)
    (z_gates,) = _mm(xn, w_main[:, D_MODEL:], name="z_gates", tm=tm_wide, tn=3 * D_MODEL, tk=1024, outs=[("tile", BF16)],
                     epilogue=ident)
    h = _lru_fwd(z_rx, conv_w, conv_b, wa, ba, wx, bx, lam, tt=tt)
    q, k, v = _mla_proj(z_ckv, q_norm, kv_norm, w_uq_p, w_ukv, cos, sin, ts=ts_proj)
    o, lse = _attn_fwd(q, k, v, t=ta, hb=MLA_HEADS)
    w_out, w_up, w_down = hooks["weights_later"](o)
    merged, h1, n2 = _merge_h1(h, z_gates, o, x, w_out, norm_mlp, ts=ts_merge)

    def ep_up(acc):
        r = jnp.maximum(acc, 0.0)
        return r * r, r

    act, relu = _mm(n2, w_up, name="up", tm=tm_wide, tn=2048, tk=1024, outs=[("tile", BF16), ("tile", BF16)],
                    epilogue=ep_up)

    def ep_loss(acc, h1v, tgt, g):
        h2 = acc + h1v
        y, _ = _rms_fwd(h2, g)
        err = y - tgt
        loss_rows = 0.5 * jnp.mean(err * err, axis=-1, keepdims=True)
        dy = err * (1.0 / D_MODEL)
        dh2, dg_rows = _rms_bwd(dy, h2, g)
        lsum = jnp.sum(loss_rows, axis=0, keepdims=True)
        return dh2, dh2, jnp.sum(dg_rows, axis=0, keepdims=True), jnp.broadcast_to(lsum, (1, D_MODEL))

    dh2, dh2b, dnf_p, loss_p = _mm(
        act, w_down, name="down_loss", tm=tm, tn=1024, tk=D_FF,
        outs=[("tile", F32), ("tile", BF16), ("rowpart", F32), ("rowpart", F32)], epilogue=ep_loss,
        extras=[("tile", h1), ("tile", target), ("row", norm_final)])
    loss_part = jnp.sum(loss_p[:, 0, 0])
    d_norm_final = jnp.sum(dnf_p, axis=(0, 1))

    def ep_du(acc, r):
        return (acc * (2.0 * r.astype(F32)),)

    (du,) = _mm(dh2b, w_down, name="d_act", tb=True, tm=tm_wide, tn=2048, tk=1024, outs=[("tile", BF16)], epilogue=ep_du,
                extras=[("tile", relu)])

    def ep_dh1(acc, h1v, dh2v, g):
        dv, dg_rows = _rms_bwd(acc, h1v, g)
        dh1 = dh2v + dv
        return dh1, dh1, jnp.sum(dg_rows, axis=0, keepdims=True)

    dh1, dh1b, dnm_p = _mm(du, w_up, name="d_n2", tb=True, tm=tm, tn=1024, tk=D_FF,
                           outs=[("tile", F32), ("tile", BF16), ("rowpart", F32)], epilogue=ep_dh1,
                           extras=[("tile", h1), ("tile", dh2), ("row", norm_mlp)])
    d_norm_mlp = jnp.sum(dnm_p, axis=(0, 1))
    tn_mm = functools.partial(_mm, ta=True, tk=tk_s, outs=[("tile", BF16)], epilogue=ident)
    (d_w_down,) = tn_mm(act, dh2b, name="dw_down", tm=1024, tn=1024)
    (p_w_up,) = _mm(n2, du, name="dw_up", ta=True, tk=tk_s, tm=1024, tn=D_FF // N_DEV, outs=[("colshard", BF16)],
                    epilogue=ident)
    (d_w_out,) = tn_mm(merged, dh1b, name="dw_out", tm=1024, tn=1024)
    early = [d_w_out.reshape(N_DEV, -1, D_MODEL), p_w_up, d_w_down.reshape(N_DEV, -1, D_MODEL)]
    w_out = w_out + hooks["send"]("early", early)[0, 0].astype(BF16)


    def ep_dmerge(dm, hv, rg, ga, gb, ov):
        hv, rg, ga, gb = hv.astype(F32), rg.astype(F32), ga.astype(F32), gb.astype(F32)
        gl, dgl = _gelu_and_grad(rg)
        sa, sb = _sigmoid(ga), _sigmoid(gb)
        ya = hv * gl
        dya = dm * sa
        do = dm * sb
        dga = dm * ya * sa * (1.0 - sa)
        dgb = dm * ov * sb * (1.0 - sb)
        dh = dya * gl
        drg = dya * hv * dgl
        dov = do * ov
        lane = lax.broadcasted_iota(jnp.int32, (dm.shape[0], LANES), 1)
        delta = jnp.zeros((dm.shape[0], LANES), F32)
        for hh in range(H):
            dsum = jnp.sum(dov[:, hh * V_HEAD:(hh + 1) * V_HEAD], axis=1, keepdims=True)
            delta = jnp.where(lane == hh, dsum, delta)
        return dh, jnp.concatenate([drg, dga, dgb], axis=1), do, delta

    dh_lru, dz_part, do, delta_w = _mm(
        dh1b, w_out, name="d_merge", tb=True, tm=ts_merge, tn=1024, tk=1024,
        outs=[("tile", F32), ("cols", BF16, 4 * D_MODEL, D_MODEL), ("tile", BF16), ("side", F32)],
        epilogue=ep_dmerge,
        extras=[("tile", h), ("tilecol", z_gates, 0), ("tilecol", z_gates, 1), ("tilecol", z_gates, 2), ("tile", o)])
    delta_row = delta_w[:, :H].T.reshape(H, 1, S)
    lse_row = lse

    dq, dk, dv = _attn_bwd(q, k, v, do, lse_row, delta_row, t=ta)
    dz_ckv, d_w_uq_p, d_w_ukv, d_q_norm, d_kv_norm = _mla_proj_bwd(
        z_ckv, dq, dk, dv, q_norm, kv_norm, w_uq_p.T, w_ukv.T, cos, sin, ts=ts_proj)
    d_w_uq = jnp.concatenate(
        [d_w_uq_p[:, :H * QK_NOPE].reshape(Q_LORA, H, QK_NOPE),
         d_w_uq_p[:, H * QK_NOPE:].reshape(Q_LORA, H, LANES)[:, :, :QK_ROPE]], axis=2).reshape(Q_LORA, -1)

    dz_main, d_wa, d_wx, d_ba, d_bx, d_lam, d_conv_w, d_conv_b = _lru_bwd(
        z_rx, h, dh_lru, dz_part, conv_w, conv_b, wa, wat, ba, wx, wxt, bx, lam, tt=tt)

    (d_w_main,) = tn_mm(xn, dz_main, name="dw_main", tm=1024, tn=1024)
    (d_w_ckv,) = tn_mm(xn, dz_ckv, name="dw_ckv", tm=1024, tn=CKV_W)
    d_w_in = jnp.concatenate([d_w_main[:, 0:2048], d_w_ckv[:, 0:576], d_w_main[:, 2048:4096]], axis=1)

    def col_parts(full):
        r = full.shape[0]
        return jnp.transpose(full.astype(BF16).reshape(r, N_DEV, -1), (1, 0, 2))

    late = [col_parts(d_w_in), col_parts(d_w_uq), col_parts(d_w_ukv)]
    norm_mix = norm_mix + hooks["send"]("late", late)[0, 0]

    def ep_dx(acc, xv, dh1v, g):
        dv, dg_rows = _rms_bwd(acc, xv, g)
        return dh1v + dv, jnp.sum(dg_rows, axis=0, keepdims=True)

    grad_x, dnx_p = _mm(dz_main, w_main, name="dx", tb=True, tm=tm, tn=1024, tk=4 * D_MODEL,
                        outs=[("tile", F32), ("rowpart", F32)], epilogue=ep_dx, more=(dz_ckv, w_ckv),
                        extras=[("tile", x), ("tile", dh1), ("row", norm_mix)])
    d_norm_mix = jnp.sum(dnx_p, axis=(0, 1))
    sm = {"norm_mix": d_norm_mix, "conv_w": d_conv_w, "conv_b": d_conv_b.reshape(-1), "lru_wa": d_wa,
          "lru_ba": d_ba.reshape(RNN_BLOCKS, RNN_BLOCK_W), "lru_wx": d_wx, "lru_bx": d_bx.reshape(RNN_BLOCKS, RNN_BLOCK_W),
          "lru_lambda": d_lam.reshape(-1), "q_norm": d_q_norm.reshape(-1), "kv_norm": d_kv_norm.reshape(-1),
          "norm_mlp": d_norm_mlp, "norm_final": d_norm_final}
    return loss_part, grad_x, sm


SMALL = ("norm_mix", "conv_b", "lru_wa", "lru_ba", "lru_wx", "lru_bx", "lru_lambda", "q_norm", "kv_norm", "norm_mlp",
         "norm_final")
WEIGHTS = ("norm_mix", "w_in", "conv_w", "conv_b", "lru_wa", "lru_ba", "lru_wx", "lru_bx", "lru_lambda", "q_norm", "w_uq",
           "kv_norm", "w_ukv", "w_out", "norm_mlp", "w_up", "w_down", "norm_final")
ADAM_TILE_ROWS = {"w_in": 256, "w_uq": 128, "w_ukv": 128, "w_out": 64, "w_up": 256, "w_down": 128}
CONV_ROWS = N_DEV * 8


def _rows(a):
    return a.reshape(-1, LANES)


def _pad_rows(a, mult):
    r = a.shape[-2]
    pad = (-r) % mult
    if pad == 0:
        return a
    cfg = [(0, 0)] * (a.ndim - 2) + [(0, pad), (0, 0)]
    return jnp.pad(a, cfg)


def _cols_from_shards(g):
    return jnp.transpose(g, (1, 0, 2)).reshape(g.shape[1], -1)


def kernel(x, norm_mix, w_in, conv_w, conv_b, lru_wa, lru_ba, lru_wx, lru_bx, lru_lambda, q_norm, w_uq, kv_norm, w_ukv, w_out, norm_mlp, w_up, w_down, norm_final, loss_target, m_norm_mix, m_w_in, m_conv_w, m_conv_b, m_lru_wa, m_lru_ba, m_lru_wx, m_lru_bx, m_lru_lambda, m_q_norm, m_w_uq, m_kv_norm, m_w_ukv, m_w_out, m_norm_mlp, m_w_up, m_w_down, m_norm_final, v_norm_mix, v_w_in, v_conv_w, v_conv_b, v_lru_wa, v_lru_ba, v_lru_wx, v_lru_bx, v_lru_lambda, v_q_norm, v_w_uq, v_kv_norm, v_w_ukv, v_w_out, v_norm_mlp, v_w_up, v_w_down, v_norm_final):
    W = dict(norm_mix=norm_mix, w_in=w_in, conv_w=conv_w, conv_b=conv_b, lru_wa=lru_wa, lru_ba=lru_ba, lru_wx=lru_wx,
             lru_bx=lru_bx, lru_lambda=lru_lambda, q_norm=q_norm, w_uq=w_uq, kv_norm=kv_norm, w_ukv=w_ukv, w_out=w_out,
             norm_mlp=norm_mlp, w_up=w_up, w_down=w_down, norm_final=norm_final)
    M = dict(norm_mix=m_norm_mix, w_in=m_w_in, conv_w=m_conv_w, conv_b=m_conv_b, lru_wa=m_lru_wa, lru_ba=m_lru_ba,
             lru_wx=m_lru_wx, lru_bx=m_lru_bx, lru_lambda=m_lru_lambda, q_norm=m_q_norm, w_uq=m_w_uq, kv_norm=m_kv_norm,
             w_ukv=m_w_ukv, w_out=m_w_out, norm_mlp=m_norm_mlp, w_up=m_w_up, w_down=m_w_down, norm_final=m_norm_final)
    V = dict(norm_mix=v_norm_mix, w_in=v_w_in, conv_w=v_conv_w, conv_b=v_conv_b, lru_wa=v_lru_wa, lru_ba=v_lru_ba,
             lru_wx=v_lru_wx, lru_bx=v_lru_bx, lru_lambda=v_lru_lambda, q_norm=v_q_norm, w_uq=v_w_uq, kv_norm=v_kv_norm,
             w_ukv=v_w_ukv, w_out=v_w_out, norm_mlp=v_norm_mlp, w_up=v_w_up, w_down=v_w_down, norm_final=v_norm_final)
    me = 4 * lax.axis_index("x") + 2 * lax.axis_index("y") + lax.axis_index("c")

    first, later = ("w_in", "w_uq", "w_ukv"), ("w_out", "w_up", "w_down")
    got = _all_gather([W[n].astype(BF16) for n in first] + [_pad_rows(conv_w, 8)], name="gather_weights")
    wts = {"w_in": _cols_from_shards(got[0]), "w_uq": _cols_from_shards(got[1]), "w_ukv": _cols_from_shards(got[2])}
    w_send, w_recv, w_src, w_land, zeros = _push_start([W[n].astype(BF16) for n in later], name="gather_later_start",
                                                       slab_per_peer=False)
    small = {n: W[n] for n in SMALL}
    small["conv_w"] = _cols_from_shards(got[3][:, :CONV_WIDTH])
    small["norm_mix"] = norm_mix + zeros[0, 0]

    def with_own_slab(land, mine):
        return lax.dynamic_update_slice(land, mine, (me, 0, 0))

    def weights_later(after):
        srcs, lands = _push_wait(w_send, w_recv, w_src, w_land, after, name="gather_later_wait", slab_per_peer=False)
        w_out_g, w_up_g, w_down_g = [with_own_slab(l, s[None]) for l, s in zip(lands, srcs)]
        return w_out_g.reshape(-1, D_MODEL), _cols_from_shards(w_up_g), w_down_g.reshape(-1, D_MODEL)

    sent = {}
    G, Dl, NM, NV = {}, {}, {}, {}

    def finish(group, names, after):
        s_sems, r_sems, srcs, lands, _ = sent[group]
        srcs, lands = _push_wait(s_sems, r_sems, srcs, lands, after, name="exchange_" + group + "_wait",
                                 slab_per_peer=True)
        for n, src, land in zip(names, srcs, lands):
            parts = with_own_slab(land, lax.dynamic_slice(src, (me, 0, 0), (1, *src.shape[1:])))
            G[n], Dl[n], NM[n], NV[n] = _adamw(W[n], M[n], V[n], parts, tr=ADAM_TILE_ROWS[n], name="adamw_" + n)

    def send(group, parts):
        sent[group] = _push_start(parts, name="exchange_" + group + "_start", slab_per_peer=True)
        zeros = sent[group][4]
        if group == "late":
            finish("early", later, zeros)
            zeros = zeros + 0.0 * (Dl["w_out"][0:8, 0:LANES] + Dl["w_up"][0:8, 0:LANES] + Dl["w_down"][0:8, 0:LANES])
        return zeros

    loss_part, grad_x, g_small = _local_step(x[0], loss_target[0], wts, small,
                                              {"weights_later": weights_later, "send": send})
    finish("late", first, grad_x)

    conv_rows = _pad_rows(jnp.transpose(g_small["conv_w"].reshape(CONV_WIDTH, N_DEV, LANES), (1, 0, 2)), 8)
    loss_rows = jnp.zeros((8, LANES), F32).at[0, 0].set(loss_part)
    as_sent = lambda n: g_small[n].astype(BF16) if n in ("lru_wa", "lru_wx") else g_small[n]
    gathered = _all_gather([_pad_rows(_rows(as_sent(n)), 8) for n in SMALL]
                           + [conv_rows.reshape(CONV_ROWS, LANES), loss_rows], name="gather_small")
    k = len(SMALL)
    outs = _adamw_many([_rows(W[n]) for n in SMALL], [_rows(M[n]) for n in SMALL], [_rows(V[n]) for n in SMALL],
                       gathered[:k], gathered[k:], name="adamw_small")
    for j, n in enumerate(SMALL):
        for out, o in zip((G, Dl, NM, NV), outs[4 * j:4 * j + 4]):
            out[n] = o.reshape(W[n].shape)
    conv_sum, loss_sum = outs[4 * k:]
    loss = loss_sum[0, 0]

    g_conv = lax.dynamic_slice(conv_sum, (me * 8, 0), (8, LANES))
    conv_out = _adamw(_pad_rows(conv_w, 8), _pad_rows(m_conv_w, 8), _pad_rows(v_conv_w, 8), g_conv[None], tr=8,
                      name="adamw_conv_w")
    for out, pk in zip((G, Dl, NM, NV), conv_out):
        out["conv_w"] = pk[:CONV_WIDTH]
    return (loss, grad_x[None], *[G[n] for n in WEIGHTS], *[Dl[n] for n in WEIGHTS], *[NM[n] for n in WEIGHTS],
            *[NV[n] for n in WEIGHTS])
```

```python
import functools

import numpy as np
import jax
import jax.numpy as jnp
from jax import lax
from jax.experimental import pallas as pl
from jax.experimental.pallas import tpu as pltpu

F32 = jnp.float32
BF16 = jnp.bfloat16
MESH = pl.DeviceIdType.MESH

D_MODEL = 1024
N_DEV = 8
LANES = 128
RNN_BLOCKS = 8
RNN_BLOCK_W = 128
CONV_WIDTH = 4
LRU_C = 8.0
MLA_HEADS = 8
QK_NOPE = 128
QK_ROPE = 64
V_HEAD = 128
QK_PAD = 256
Q_LORA = 256
KV_LORA = 256
CKV_W = 640
ROPE_THETA = 10000.0
D_FF = 4096
EPS = 1e-6
ATTN_SCALE = (QK_NOPE + QK_ROPE) ** -0.5
LOG2E = 1.4426950408889634
LN2 = 0.6931471805599453
NEG = -1e30

ADAM_LR = 0.001
ADAM_B1 = 0.9
ADAM_B2 = 0.999
ADAM_EPS = 1e-08
ADAM_WD = 0.01
ADAM_STEP = 10

VMEM_LIMIT = 56 * 1024 * 1024


def _params(sem=None):
    return pltpu.CompilerParams(dimension_semantics=sem, vmem_limit_bytes=VMEM_LIMIT)


def _sigmoid(v):
    return 1.0 / (1.0 + jnp.exp(-v))


def _softplus(y):
    e = jnp.exp(-jnp.abs(y))
    u = 1.0 + e
    d = u - 1.0
    l1p = jnp.where(d == 0.0, e, jnp.log(u) * e / jnp.where(d == 0.0, 1.0, d))
    return jnp.maximum(y, 0.0) + l1p


_GELU_K = 0.7978845608028654
_GELU_C = 0.044715


def _gelu_and_grad(v):
    t = jnp.tanh(_GELU_K * (v + _GELU_C * v * v * v))
    g = 0.5 * v * (1.0 + t)
    dg = 0.5 * (1.0 + t) + 0.5 * v * (1.0 - t * t) * _GELU_K * (1.0 + 3.0 * _GELU_C * v * v)
    return g, dg


def _rms_fwd(v, g):
    rstd = lax.rsqrt(jnp.mean(v * v, axis=-1, keepdims=True) + EPS)
    return v * rstd * g, rstd


def _rms_bwd(dy, v, g):
    rstd = lax.rsqrt(jnp.mean(v * v, axis=-1, keepdims=True) + EPS)
    vh = v * rstd
    dvh = dy * g
    dv = rstd * (dvh - vh * jnp.mean(dvh * vh, axis=-1, keepdims=True))
    return dv, dy * vh


def _shift_down(v, s, fill, row):
    return jnp.where(row >= s, pltpu.roll(v, s, 0), fill)


def _shift_up(v, s, fill, row, n):
    return jnp.where(row < n - s, pltpu.roll(v, n - s, 0), fill)


def _rot_half(v, lane):
    n = v.shape[-1]
    l = lane & (LANES - 1)
    up = pltpu.roll(v, n - QK_ROPE // 2, 1)
    dn = pltpu.roll(v, QK_ROPE // 2, 1)
    return jnp.where(l < QK_ROPE // 2, -up, jnp.where(l < QK_ROPE, dn, 0.0))


def _mm(a, b, *, name, tm, tn, tk, outs, epilogue, extras=(), ta=False, tb=False, more=None, also=None):
    assert not (ta and tb)
    if ta:
        K, M = a.shape
    else:
        M, K = a.shape
    if tb:
        N, K2 = b.shape
    else:
        K2, N = b.shape
    assert K == K2 and M % tm == 0 and N % tn == 0 and K % tk == 0, (name, a.shape, b.shape)
    n_i, n_j, n_k = M // tm, N // tn, K // tk
    n_ex, n_out = len(extras), len(outs)
    n_more = 0 if more is None else 2
    n_also = 0 if also is None else 1
    assert more is None or (n_k == 1 and not ta)
    assert also is None or (n_k == 1 and n_j == 1 and not ta and more is None)

    def body(*refs):
        a_ref, b_ref = refs[0], refs[1]
        ex_refs = refs[2 + n_more:2 + n_more + n_ex]
        first_out = 2 + n_more + n_ex + n_also
        out_refs = refs[first_out:first_out + n_out]
        if ta:
            part = lax.dot_general(a_ref[...], b_ref[...], (((0,), (0,)), ((), ())), preferred_element_type=F32)
        elif tb:
            part = lax.dot_general(a_ref[...], b_ref[...], (((1,), (1,)), ((), ())), preferred_element_type=F32)
        else:
            part = jnp.dot(a_ref[...], b_ref[...], preferred_element_type=F32)
        if more is not None:
            part = part + lax.dot_general(refs[2][...], refs[3][...], (((1,), (1,)), ((), ())),
                                          preferred_element_type=F32)

        def finish(acc):
            res = epilogue(acc, *[r[...] for r in ex_refs])
            for o_ref, r, spec in zip(out_refs, res, outs):
                if spec[0] == "cols":
                    o_ref[:, spec[3]:spec[3] + r.shape[1]] = r.astype(o_ref.dtype)
                else:
                    o_ref[...] = r.astype(o_ref.dtype).reshape(o_ref.shape)

        if also is not None:
            refs[first_out + n_out][...] = jnp.dot(a_ref[...], refs[first_out - 1][...], preferred_element_type=F32)
        if n_k == 1:
            finish(part)
        else:
            acc_ref = refs[-1]
            k = pl.program_id(2)

            @pl.when(k == 0)
            def _():
                acc_ref[...] = part

            @pl.when(k > 0)
            def _():
                acc_ref[...] += part

            @pl.when(k == n_k - 1)
            def _():
                finish(acc_ref[...])

    a_spec = pl.BlockSpec((tk, tm), lambda j, i, k: (k, i)) if ta else pl.BlockSpec((tm, tk), lambda j, i, k: (i, k))
    b_once = dict(pipeline_mode=pl.Buffered(1)) if (n_j == 1 and n_k == 1) else {}
    if tb:
        in_specs = [a_spec, pl.BlockSpec((tn, tk), lambda j, i, k: (j, k), **b_once)]
    else:
        in_specs = [a_spec, pl.BlockSpec((tk, tn), lambda j, i, k: (k, j), **b_once)]
    if more is not None:
        k2 = more[0].shape[1]
        in_specs += [pl.BlockSpec((tm, k2), lambda j, i, k: (i, 0)), pl.BlockSpec((tn, k2), lambda j, i, k: (j, 0), **b_once)]
    for ex in extras:
        kind = ex[0]
        if kind == "tile":
            in_specs.append(pl.BlockSpec((tm, tn), lambda j, i, k: (i, j)))
        elif kind == "tilecol":
            assert n_j == 1
            in_specs.append(pl.BlockSpec((tm, tn), functools.partial(lambda c, j, i, k: (i, c), ex[2])))
        else:
            in_specs.append(pl.BlockSpec((1, tn), lambda j, i, k: (0, j)))
    out_specs, out_shape = [], []
    for kind, dt, *rest in outs:
        if kind == "tile":
            out_specs.append(pl.BlockSpec((tm, tn), lambda j, i, k: (i, j)))
            out_shape.append(jax.ShapeDtypeStruct((M, N), dt))
        elif kind == "colshard":
            out_specs.append(pl.BlockSpec((1, tm, tn), lambda j, i, k: (j, i, 0)))
            out_shape.append(jax.ShapeDtypeStruct((n_j, M, tn), dt))
        elif kind == "cols":
            assert n_j == 1
            out_specs.append(pl.BlockSpec((tm, rest[0]), lambda j, i, k: (i, 0)))
            out_shape.append(jax.ShapeDtypeStruct((M, rest[0]), dt))
        elif kind == "side":
            assert n_j == 1
            out_specs.append(pl.BlockSpec((tm, LANES), lambda j, i, k: (i, 0)))
            out_shape.append(jax.ShapeDtypeStruct((M, LANES), dt))
        else:
            out_specs.append(pl.BlockSpec((1, 1, tn), lambda j, i, k: (i, 0, j)))
            out_shape.append(jax.ShapeDtypeStruct((n_i, 1, N), dt))
    scratch = [pltpu.VMEM((tm, tn), F32)] if n_k > 1 else []
    if also is not None:
        in_specs.append(pl.BlockSpec(also.shape, lambda j, i, k: (0, 0), **b_once))
        out_specs.append(pl.BlockSpec((tm, also.shape[1]), lambda j, i, k: (i, 0)))
        out_shape.append(jax.ShapeDtypeStruct((M, also.shape[1]), F32))
    return pl.pallas_call(
        body, name=name, grid=(n_j, n_i, n_k), in_specs=in_specs, out_specs=out_specs, out_shape=out_shape,
        scratch_shapes=scratch, compiler_params=_params(("parallel", "parallel", "arbitrary")),
    )(a, b, *(more or ()), *[ex[1] for ex in extras], *([] if also is None else [also]))


def _rmsnorm_cast(x, g, *, ts, name):
    S, D = x.shape

    def body(x_ref, g_ref, o_ref):
        y, _ = _rms_fwd(x_ref[...], g_ref[...])
        o_ref[...] = y.astype(BF16)

    return pl.pallas_call(
        body, name=name, grid=(S // ts,),
        in_specs=[pl.BlockSpec((ts, D), lambda i: (i, 0)), pl.BlockSpec((1, D), lambda i: (0, 0))],
        out_specs=pl.BlockSpec((ts, D), lambda i: (i, 0)), out_shape=jax.ShapeDtypeStruct((S, D), BF16),
        compiler_params=_params(("parallel",)),
    )(x, g)


LRU_NB = 4


def _lru_gates(xa, wa_ref, ba_ref, wx_ref, bx_ref, lam):
    xab = xa.astype(BF16)
    W = RNN_BLOCK_W
    rs, is_ = [], []
    for j in range(LRU_NB):
        xj = xab[:, j * W:(j + 1) * W]
        rs.append(_sigmoid(jnp.dot(xj, wa_ref[j], preferred_element_type=F32) + ba_ref[j]))
        is_.append(_sigmoid(jnp.dot(xj, wx_ref[j], preferred_element_type=F32) + bx_ref[j]))
    r = jnp.concatenate(rs, axis=1)
    i = jnp.concatenate(is_, axis=1)
    sp = _softplus(-lam)
    log_a = (-LRU_C * r) * sp
    a = jnp.exp(log_a)
    y = 2.0 * log_a
    one_m = jnp.where(y > -0.01, -y * (1.0 + 0.5 * y * (1.0 + y * (1.0 / 3.0))), 1.0 - a * a)
    return r, i, sp, a, jnp.sqrt(one_m)


def _rows_before(x, tail8, k):
    e16 = jnp.concatenate([tail8, x[0:8, :]], axis=0)
    return jnp.concatenate([pltpu.roll(e16, k, 0)[8:16, :], pltpu.roll(x, k, 0)[8:, :]], axis=0)


def _rows_after(x, head8, k):
    tt = x.shape[0]
    e16 = jnp.concatenate([x[tt - 8:tt, :], head8], axis=0)
    return jnp.concatenate([pltpu.roll(x, tt - k, 0)[:tt - 8, :], pltpu.roll(e16, 16 - k, 0)[0:8, :]], axis=0)


def _scan_down(a, b, h0, a_s, b_s, c_s):
    tt, C = a.shape
    G, nch = tt // 8, C // LANES
    rin = lax.broadcasted_iota(jnp.int32, (tt, C), 0) & 7

    def in_group(v, s):
        return pltpu.roll(v.reshape(G, 8, C), s, 1).reshape(tt, C)

    A, B = a, b
    for s in (1, 2, 4):
        B = A * jnp.where(rin >= s, in_group(B, s), 0.0) + B
        A = A * jnp.where(rin >= s, in_group(A, s), 1.0)
    for j in range(nch):
        a_s[j] = A[:, j * LANES:(j + 1) * LANES]
        b_s[j] = B[:, j * LANES:(j + 1) * LANES]
    At = jnp.concatenate([a_s.at[j][pl.ds(7, G, stride=8), :] for j in range(nch)], axis=1)
    Bt = jnp.concatenate([b_s.at[j][pl.ds(7, G, stride=8), :] for j in range(nch)], axis=1)
    rowg = lax.broadcasted_iota(jnp.int32, (G, C), 0)
    s = 1
    while s < G:
        Bt = At * _shift_down(Bt, s, 0.0, rowg) + Bt
        At = At * _shift_down(At, s, 1.0, rowg)
        s *= 2
    hg = At * h0 + Bt
    cin = _shift_down(hg, 1, h0, rowg)
    for j in range(nch):
        for r in range(8):
            c_s.at[j][pl.ds(r, G, stride=8), :] = cin[:, j * LANES:(j + 1) * LANES]
    return A * jnp.concatenate([c_s[j] for j in range(nch)], axis=1) + B, hg[G - 1:G, :]


def _scan_up(c, g_in, g_next, a_s, b_s, c_s):
    tt, C = c.shape
    G, nch = tt // 8, C // LANES
    rin = lax.broadcasted_iota(jnp.int32, (tt, C), 0) & 7

    def in_group(v, s):
        return pltpu.roll(v.reshape(G, 8, C), 8 - s, 1).reshape(tt, C)

    Cc, Gv = c, g_in
    for s in (1, 2, 4):
        Gv = Gv + Cc * jnp.where(rin < 8 - s, in_group(Gv, s), 0.0)
        Cc = Cc * jnp.where(rin < 8 - s, in_group(Cc, s), 1.0)
    for j in range(nch):
        a_s[j] = Cc[:, j * LANES:(j + 1) * LANES]
        b_s[j] = Gv[:, j * LANES:(j + 1) * LANES]
    Ct = jnp.concatenate([a_s.at[j][pl.ds(0, G, stride=8), :] for j in range(nch)], axis=1)
    Gt = jnp.concatenate([b_s.at[j][pl.ds(0, G, stride=8), :] for j in range(nch)], axis=1)
    rowg = lax.broadcasted_iota(jnp.int32, (G, C), 0)
    s = 1
    while s < G:
        Gt = Gt + Ct * _shift_up(Gt, s, 0.0, rowg, G)
        Ct = Ct * _shift_up(Ct, s, 1.0, rowg, G)
        s *= 2
    gg = Gt + Ct * g_next
    cin = _shift_up(gg, 1, g_next, rowg, G)
    for j in range(nch):
        for r in range(8):
            c_s.at[j][pl.ds(r, G, stride=8), :] = cin[:, j * LANES:(j + 1) * LANES]
    return Gv + Cc * jnp.concatenate([c_s[j] for j in range(nch)], axis=1), gg[0:1, :]


def _lru_fwd(z_rx, conv_w, conv_b, wa, ba, wx, bx, lam, *, tt):
    S = z_rx.shape[0]
    n_t = S // tt
    BW = RNN_BLOCK_W
    W = LRU_NB * BW

    def body(x_ref, cw_ref, cb_ref, wa_ref, ba_ref, wx_ref, bx_ref, lam_ref, h_ref, tail, hc, a_s, b_s, c_s):
        t = pl.program_id(1)

        @pl.when(t == 0)
        def _():
            tail[...] = jnp.zeros((8, W), F32)
            hc[...] = jnp.zeros((8, W), F32)

        x = x_ref[...]
        before = tail[...]
        cw = cw_ref[...]
        xa = (cb_ref[...] + cw[3:4] * x + cw[2:3] * _rows_before(x, before, 1) + cw[1:2] * _rows_before(x, before, 2)
              + cw[0:1] * _rows_before(x, before, 3))
        tail[...] = x[tt - 8:tt, :]
        _r, i, _sp, a, mult = _lru_gates(xa, wa_ref, ba_ref, wx_ref, bx_ref, lam_ref[...])
        h, h_last = _scan_down(a, mult * (i * xa), hc[0:1, :], a_s, b_s, c_s)
        h_ref[...] = h.astype(BF16)
        hc[...] = jnp.broadcast_to(h_last, (8, W))

    blk = lambda n, t: (t, n)
    vec = pl.BlockSpec((1, W), lambda n, t: (0, n))
    mat = pl.BlockSpec((LRU_NB, BW, BW), lambda n, t: (n, 0, 0))
    bias = pl.BlockSpec((LRU_NB, 1, BW), lambda n, t: (n, 0, 0))
    row8 = pltpu.VMEM((8, W), F32)
    wide = pltpu.VMEM((LRU_NB, tt, LANES), F32)
    return pl.pallas_call(
        body, name="lru_fwd", grid=(RNN_BLOCKS // LRU_NB, n_t),
        in_specs=[pl.BlockSpec((tt, W), blk), pl.BlockSpec((CONV_WIDTH, W), lambda n, t: (0, n)), vec, mat, bias, mat,
                  bias, vec],
        out_specs=pl.BlockSpec((tt, W), blk), out_shape=jax.ShapeDtypeStruct((S, D_MODEL), BF16),
        scratch_shapes=[row8, row8, wide, wide, wide],
        compiler_params=_params(("parallel", "arbitrary")),
    )(z_rx, conv_w, conv_b, wa, ba, wx, bx, lam)


def _lru_bwd(z_rx, h, dh, dz, conv_w, conv_b, wa, wat, ba, wx, wxt, bx, lam, *, tt):
    S = z_rx.shape[0]
    n_t = S // tt
    BW = RNN_BLOCK_W
    W = LRU_NB * BW
    t8 = tt // 8

    def body(x_ref, xp_ref, h_ref, hp_ref, dh_ref, _dz_ref, cw_ref, cb_ref, wa_ref, wat_ref, ba_ref, wx_ref, wxt_ref,
             bx_ref, lam_ref, dx_ref, dwa_ref, dwx_ref, dba_ref, dbx_ref, dlam_ref, dcw_ref, dcb_ref, nxt, a_c, g_c, a_s,
             b_s, c_s):
        t = pl.program_id(1)
        tile = n_t - 1 - t

        @pl.when(t == 0)
        def _():
            a_c[...] = jnp.zeros((8, W), F32)
            g_c[...] = jnp.zeros((8, W), F32)
            nxt[...] = jnp.zeros((8, W), F32)
            dwa_ref[...] = jnp.zeros_like(dwa_ref)
            dwx_ref[...] = jnp.zeros_like(dwx_ref)
            dba_ref[...] = jnp.zeros_like(dba_ref)
            dbx_ref[...] = jnp.zeros_like(dbx_ref)
            dlam_ref[...] = jnp.zeros_like(dlam_ref)
            dcw_ref[...] = jnp.zeros_like(dcw_ref)
            dcb_ref[...] = jnp.zeros_like(dcb_ref)

        has_prev = (tile > 0).astype(F32)
        x = x_ref[...]
        before = xp_ref[...] * has_prev
        xm1, xm2, xm3 = _rows_before(x, before, 1), _rows_before(x, before, 2), _rows_before(x, before, 3)
        cw = cw_ref[...]
        xa = cb_ref[...] + cw[3:4] * x + cw[2:3] * xm1 + cw[1:2] * xm2 + cw[0:1] * xm3
        lam = lam_ref[...]
        r, i, sp, a, mult = _lru_gates(xa, wa_ref, ba_ref, wx_ref, bx_ref, lam)
        gated = i * xa
        h_prev = _rows_before(h_ref[...].astype(F32), hp_ref[8:16, :].astype(F32) * has_prev, 1)
        g, g_first = _scan_up(_rows_after(a, a_c[...], 1), dh_ref[...], g_c[0:1, :], a_s, b_s, c_s)
        a_c[...] = jnp.broadcast_to(a[0:1, :], (8, W))
        g_c[...] = jnp.broadcast_to(g_first, (8, W))
        dlog_a = g * h_prev * a - g * gated * (a * a) / mult
        dgated = g * mult
        di = dgated * xa
        dxa = dgated * i
        dr = dlog_a * (-LRU_C * sp)
        dlam_ref[...] += jnp.sum(dlog_a * (-LRU_C * r), axis=0, keepdims=True) * (-_sigmoid(-lam))
        dpr = dr * r * (1.0 - r)
        dpi = di * i * (1.0 - i)
        xab, dprb, dpib = xa.astype(BF16), dpr.astype(BF16), dpi.astype(BF16)
        tn_dims = (((0,), (0,)), ((), ()))
        back = []
        for j in range(LRU_NB):
            sl = slice(j * BW, (j + 1) * BW)
            dwa_ref[j] += lax.dot_general(xab[:, sl], dprb[:, sl], tn_dims, preferred_element_type=F32)
            dwx_ref[j] += lax.dot_general(xab[:, sl], dpib[:, sl], tn_dims, preferred_element_type=F32)
            dba_ref[j] += jnp.sum(dpr[:, sl], axis=0, keepdims=True)
            dbx_ref[j] += jnp.sum(dpi[:, sl], axis=0, keepdims=True)
            back.append(jnp.dot(dprb[:, sl], wat_ref[j], preferred_element_type=F32)
                        + jnp.dot(dpib[:, sl], wxt_ref[j], preferred_element_type=F32))
        dxa = dxa + jnp.concatenate(back, axis=1)
        after = nxt[...]
        dx = (cw[3:4] * dxa + cw[2:3] * _rows_after(dxa, after, 1) + cw[1:2] * _rows_after(dxa, after, 2)
              + cw[0:1] * _rows_after(dxa, after, 3))
        nxt[...] = dxa[0:8, :]
        dx_ref[...] = dx.astype(BF16)
        dcw_ref[3:4, :] += jnp.sum(dxa * x, axis=0, keepdims=True)
        dcw_ref[2:3, :] += jnp.sum(dxa * xm1, axis=0, keepdims=True)
        dcw_ref[1:2, :] += jnp.sum(dxa * xm2, axis=0, keepdims=True)
        dcw_ref[0:1, :] += jnp.sum(dxa * xm3, axis=0, keepdims=True)
        dcb_ref[...] += jnp.sum(dxa, axis=0, keepdims=True)

    blk = lambda n, t: (n_t - 1 - t, n)
    prev = lambda n, t: (jnp.maximum((n_t - 1 - t) * t8 - 1, 0), n)
    vec = pl.BlockSpec((1, W), lambda n, t: (0, n))
    mat = pl.BlockSpec((LRU_NB, BW, BW), lambda n, t: (n, 0, 0))
    bias = pl.BlockSpec((LRU_NB, 1, BW), lambda n, t: (n, 0, 0))
    cws = pl.BlockSpec((CONV_WIDTH, W), lambda n, t: (0, n))
    tile = pl.BlockSpec((tt, W), blk)
    prev8 = pl.BlockSpec((8, W), prev)
    prev16 = pl.BlockSpec((16, W), lambda n, t: (jnp.maximum((n_t - 1 - t) * (tt // 16) - 1, 0), n))
    row8 = pltpu.VMEM((8, W), F32)
    wide = pltpu.VMEM((LRU_NB, tt, LANES), F32)
    return pl.pallas_call(
        body, name="lru_bwd", grid=(RNN_BLOCKS // LRU_NB, n_t),
        in_specs=[tile, prev8, tile, prev16, tile, pl.BlockSpec(memory_space=pl.ANY), cws, vec, mat, mat, bias, mat, mat,
                  bias, vec],
        out_specs=[tile, mat, mat, bias, bias, vec, cws, vec], input_output_aliases={5: 0},
        out_shape=[jax.ShapeDtypeStruct(dz.shape, BF16),
                   jax.ShapeDtypeStruct((RNN_BLOCKS, BW, BW), F32), jax.ShapeDtypeStruct((RNN_BLOCKS, BW, BW), F32),
                   jax.ShapeDtypeStruct((RNN_BLOCKS, 1, BW), F32), jax.ShapeDtypeStruct((RNN_BLOCKS, 1, BW), F32),
                   jax.ShapeDtypeStruct((1, D_MODEL), F32),
                   jax.ShapeDtypeStruct((CONV_WIDTH, D_MODEL), F32), jax.ShapeDtypeStruct((1, D_MODEL), F32)],
        scratch_shapes=[row8, row8, row8, wide, wide, wide],
        compiler_params=_params(("parallel", "arbitrary")),
    )(z_rx, z_rx, h, h, dh, dz, conv_w, conv_b, wa, wat, ba, wx, wxt, bx, lam)


def _mla_proj(z_ckv, q_norm, kv_norm, w_uq, w_ukv, cos, sin, *, ts):
    S = z_ckv.shape[0]
    H = MLA_HEADS

    def body(c_ref, qn_ref, kn_ref, wq_ref, wkv_ref, cos_ref, sin_ref, q_ref, k_ref, v_ref):
        c = c_ref[...]
        cqn, _ = _rms_fwd(c[:, 0:Q_LORA], qn_ref[...])
        ckn, _ = _rms_fwd(c[:, Q_LORA:Q_LORA + KV_LORA], kn_ref[...])
        q = jnp.dot(cqn.astype(BF16), wq_ref[...], preferred_element_type=F32) * (ATTN_SCALE * LOG2E)
        kv = jnp.dot(ckn.astype(BF16), wkv_ref[...], preferred_element_type=F32)
        cos1, sin1 = cos_ref[...], sin_ref[...]
        cos8 = jnp.concatenate([cos1] * H, axis=1)
        sin8 = jnp.concatenate([sin1] * H, axis=1)
        qr = q[:, H * QK_NOPE:]
        lane8 = lax.broadcasted_iota(jnp.int32, qr.shape, 1)
        qr = qr * cos8 + _rot_half(qr, lane8) * sin8
        kr = c[:, Q_LORA + KV_LORA:]
        lane1 = lax.broadcasted_iota(jnp.int32, kr.shape, 1)
        kr = (kr * cos1 + _rot_half(kr, lane1) * sin1).astype(BF16)
        for h in range(H):
            q_ref[h, :, 0:QK_NOPE] = q[:, h * QK_NOPE:(h + 1) * QK_NOPE].astype(BF16)
            q_ref[h, :, QK_NOPE:] = qr[:, h * LANES:(h + 1) * LANES].astype(BF16)
            k_ref[h, :, 0:QK_NOPE] = kv[:, h * 2 * LANES:h * 2 * LANES + LANES].astype(BF16)
            k_ref[h, :, QK_NOPE:] = kr
            v_ref[h] = kv[:, h * 2 * LANES + LANES:(h + 1) * 2 * LANES].astype(BF16)

    full = lambda shape: pl.BlockSpec(shape, lambda i: (0,) * len(shape))
    return pl.pallas_call(
        body, name="mla_proj", grid=(S // ts,),
        in_specs=[pl.BlockSpec((ts, CKV_W), lambda i: (i, 0)), full((1, Q_LORA)), full((1, KV_LORA)),
                  full(w_uq.shape), full(w_ukv.shape), pl.BlockSpec((ts, LANES), lambda i: (i, 0)),
                  pl.BlockSpec((ts, LANES), lambda i: (i, 0))],
        out_specs=[pl.BlockSpec((H, ts, QK_PAD), lambda i: (0, i, 0)), pl.BlockSpec((H, ts, QK_PAD), lambda i: (0, i, 0)),
                   pl.BlockSpec((H, ts, V_HEAD), lambda i: (0, i, 0))],
        out_shape=[jax.ShapeDtypeStruct((H, S, QK_PAD), BF16), jax.ShapeDtypeStruct((H, S, QK_PAD), BF16),
                   jax.ShapeDtypeStruct((H, S, V_HEAD), BF16)],
        compiler_params=_params(("parallel",)),
    )(z_ckv, q_norm, kv_norm, w_uq, w_ukv, cos, sin)


def _mla_proj_bwd(z_ckv, dq, dk, dv, q_norm, kv_norm, w_uqt, w_ukvt, cos, sin, *, ts):
    S = z_ckv.shape[0]
    H = MLA_HEADS

    def body(c_ref, dq_ref, dk_ref, dv_ref, qn_ref, kn_ref, wqt_ref, wkvt_ref, cos_ref, sin_ref,
             dz_ref, dwq_ref, dwkv_ref, dqn_ref, dkn_ref):
        @pl.when(pl.program_id(0) == 0)
        def _():
            dwq_ref[...] = jnp.zeros_like(dwq_ref)
            dwkv_ref[...] = jnp.zeros_like(dwkv_ref)
            dqn_ref[...] = jnp.zeros_like(dqn_ref)
            dkn_ref[...] = jnp.zeros_like(dkn_ref)

        c = c_ref[...]
        cq, ck = c[:, 0:Q_LORA], c[:, Q_LORA:Q_LORA + KV_LORA]
        qn, kn = qn_ref[...], kn_ref[...]
        cqn, _ = _rms_fwd(cq, qn)
        ckn, _ = _rms_fwd(ck, kn)
        cos1, sin1 = cos_ref[...], sin_ref[...]
        lane1 = lax.broadcasted_iota(jnp.int32, cos1.shape, 1)

        def unrope(g):
            return g * cos1 - _rot_half(g * sin1, lane1)

        dq_all = jnp.concatenate([dq_ref[h, :, 0:QK_NOPE] for h in range(H)]
                                 + [unrope(dq_ref[h, :, QK_NOPE:]) for h in range(H)], axis=1)
        dq_all = (dq_all * ATTN_SCALE).astype(BF16)
        dkv_all = jnp.concatenate([p for h in range(H) for p in (dk_ref[h, :, 0:QK_NOPE], dv_ref[h])],
                                  axis=1).astype(BF16)
        dkr = dk_ref[0, :, QK_NOPE:].astype(F32)
        for h in range(1, H):
            dkr = dkr + dk_ref[h, :, QK_NOPE:].astype(F32)
        dkr = unrope(dkr)
        tn_dims = (((0,), (0,)), ((), ()))
        dwq_ref[...] += lax.dot_general(cqn.astype(BF16), dq_all, tn_dims, preferred_element_type=F32)
        dwkv_ref[...] += lax.dot_general(ckn.astype(BF16), dkv_all, tn_dims, preferred_element_type=F32)
        dcqn = jnp.dot(dq_all, wqt_ref[...], preferred_element_type=F32)
        dckn = jnp.dot(dkv_all, wkvt_ref[...], preferred_element_type=F32)
        dcq, dqn_rows = _rms_bwd(dcqn, cq, qn)
        dck, dkn_rows = _rms_bwd(dckn, ck, kn)
        dqn_ref[...] += jnp.sum(dqn_rows, axis=0, keepdims=True)
        dkn_ref[...] += jnp.sum(dkn_rows, axis=0, keepdims=True)
        dz_ref[:, 0:Q_LORA] = dcq.astype(BF16)
        dz_ref[:, Q_LORA:Q_LORA + KV_LORA] = dck.astype(BF16)
        dz_ref[:, Q_LORA + KV_LORA:] = dkr.astype(BF16)

    full = lambda shape: pl.BlockSpec(shape, lambda i: (0,) * len(shape))
    return pl.pallas_call(
        body, name="mla_proj_bwd", grid=(S // ts,),
        in_specs=[pl.BlockSpec((ts, CKV_W), lambda i: (i, 0)), pl.BlockSpec((H, ts, QK_PAD), lambda i: (0, i, 0)),
                  pl.BlockSpec((H, ts, QK_PAD), lambda i: (0, i, 0)), pl.BlockSpec((H, ts, V_HEAD), lambda i: (0, i, 0)),
                  full((1, Q_LORA)), full((1, KV_LORA)), full(w_uqt.shape), full(w_ukvt.shape),
                  pl.BlockSpec((ts, LANES), lambda i: (i, 0)), pl.BlockSpec((ts, LANES), lambda i: (i, 0))],
        out_specs=[pl.BlockSpec((ts, CKV_W), lambda i: (i, 0)), full((Q_LORA, w_uqt.shape[0])),
                   full((KV_LORA, w_ukvt.shape[0])), full((1, Q_LORA)), full((1, KV_LORA))],
        out_shape=[jax.ShapeDtypeStruct((S, CKV_W), BF16), jax.ShapeDtypeStruct((Q_LORA, w_uqt.shape[0]), F32),
                   jax.ShapeDtypeStruct((KV_LORA, w_ukvt.shape[0]), F32), jax.ShapeDtypeStruct((1, Q_LORA), F32),
                   jax.ShapeDtypeStruct((1, KV_LORA), F32)],
        compiler_params=_params(("arbitrary",)),
    )(z_ckv, dq, dk, dv, q_norm, kv_norm, w_uqt, w_ukvt, cos, sin)


NT_DIMS = (((1,), (1,)), ((), ()))
TN_DIMS = (((0,), (0,)), ((), ()))


def _attn_fwd(q, k, v, *, t, hb):
    H, S, _ = q.shape
    n = S // t
    pairs = [(i, j) for i in range(n) for j in range(i + 1)]
    qi = jnp.asarray(np.array([p[0] for p in pairs], np.int32))
    ki = jnp.asarray(np.array([p[1] for p in pairs], np.int32))

    def body(qi_ref, ki_ref, q_ref, k_ref, v_ref, o_ref, lse_ref, m_s, l_s, acc_s):
        p = pl.program_id(1)
        i, j = qi_ref[p], ki_ref[p]

        @pl.when(j == 0)
        def _():
            m_s[...] = jnp.full(m_s.shape, NEG, F32)
            l_s[...] = jnp.zeros(l_s.shape, F32)
            acc_s[...] = jnp.zeros(acc_s.shape, F32)

        def block(hh, r0, nr, nk, masked):
            rows = slice(r0, r0 + nr)
            s = lax.dot_general(q_ref[hh, rows, :], k_ref[hh, 0:nk, :], NT_DIMS, preferred_element_type=F32)
            if masked:
                row = lax.broadcasted_iota(jnp.int32, (nr, nk), 0) + r0
                col = lax.broadcasted_iota(jnp.int32, (nr, nk), 1)
                s = jnp.where(row >= col, s, NEG)
            chunks = nk // LANES
            mc = s[:, 0:LANES]
            for c in range(1, chunks):
                mc = jnp.maximum(mc, s[:, c * LANES:(c + 1) * LANES])
            m_prev = m_s[hh, rows, :]
            m_new = jnp.maximum(m_prev, jnp.max(mc, axis=1, keepdims=True))
            alpha = jnp.exp2(m_prev - m_new)
            pr = jnp.exp2(s - jnp.concatenate([m_new] * chunks, axis=1))
            ls = pr[:, 0:LANES]
            for c in range(1, chunks):
                ls = ls + pr[:, c * LANES:(c + 1) * LANES]
            l_s[hh, rows, :] = alpha * l_s[hh, rows, :] + ls
            acc_s[hh, rows, :] = alpha * acc_s[hh, rows, :] + jnp.dot(pr.astype(BF16), v_ref[hh, 0:nk, :],
                                                                      preferred_element_type=F32)
            m_s[hh, rows, :] = m_new

        def step(diagonal):
            for hh in range(hb):
                if diagonal:
                    block(hh, 0, t // 2, t // 2, True)
                    block(hh, t // 2, t // 2, t, True)
                else:
                    block(hh, 0, t, t, False)

        @pl.when(j < i)
        def _():
            step(False)

        @pl.when(j == i)
        def _():
            step(True)
            for hh in range(hb):
                l = jnp.sum(l_s[hh], axis=1, keepdims=True)
                o_ref[:, hh * V_HEAD:(hh + 1) * V_HEAD] = acc_s[hh] / l
                lse_ref[hh] = (m_s[hh] + jnp.log2(l)).T[0:1, :]

    grid_spec = pltpu.PrefetchScalarGridSpec(
        num_scalar_prefetch=2, grid=(H // hb, len(pairs)),
        in_specs=[pl.BlockSpec((hb, t, QK_PAD), lambda h, p, qi, ki: (h, qi[p], 0)),
                  pl.BlockSpec((hb, t, QK_PAD), lambda h, p, qi, ki: (h, ki[p], 0)),
                  pl.BlockSpec((hb, t, V_HEAD), lambda h, p, qi, ki: (h, ki[p], 0))],
        out_specs=[pl.BlockSpec((t, hb * V_HEAD), lambda h, p, qi, ki: (qi[p], h)),
                   pl.BlockSpec((hb, 1, t), lambda h, p, qi, ki: (h, 0, qi[p]))],
        scratch_shapes=[pltpu.VMEM((hb, t, LANES), F32), pltpu.VMEM((hb, t, LANES), F32),
                        pltpu.VMEM((hb, t, V_HEAD), F32)],
    )
    return pl.pallas_call(
        body, name="attn_fwd", grid_spec=grid_spec,
        out_shape=[jax.ShapeDtypeStruct((S, H * V_HEAD), F32), jax.ShapeDtypeStruct((H, 1, S), F32)],
        compiler_params=_params(("parallel", "arbitrary")),
    )(qi, ki, q, k, v)


def _attn_bwd(q, k, v, do, lse_row, delta_row, *, t):
    H, S, _ = q.shape
    n = S // t
    pairs = [(i, j) for j in range(n) for i in range(j, n)]
    qi = jnp.asarray(np.array([p[0] for p in pairs], np.int32))
    ki = jnp.asarray(np.array([p[1] for p in pairs], np.int32))

    def body(qi_ref, ki_ref, q_ref, k_ref, v_ref, do_ref, lse_ref, dl_ref, dq_ref, dk_ref, dv_ref, dk_s, dv_s, dq_s):
        p = pl.program_id(1)
        i, j = qi_ref[p], ki_ref[p]

        @pl.when(p == 0)
        def _():
            dq_s[...] = jnp.zeros_like(dq_s)

        def block(k0, nk, q0, nq, masked):
            qb, dob = q_ref[0, q0:q0 + nq, :], do_ref[q0:q0 + nq, :]
            kb, vb = k_ref[0, k0:k0 + nk, :], v_ref[0, k0:k0 + nk, :]
            st = lax.dot_general(kb, qb, NT_DIMS, preferred_element_type=F32)
            if masked:
                krow = lax.broadcasted_iota(jnp.int32, (nk, nq), 0) + k0
                qcol = lax.broadcasted_iota(jnp.int32, (nk, nq), 1) + q0
                st = jnp.where(krow <= qcol, st, NEG)
            pt = jnp.exp2(st - lse_ref[0][:, q0:q0 + nq])
            dvp = jnp.dot(pt.astype(BF16), dob, preferred_element_type=F32)
            dpt = lax.dot_general(vb, dob, NT_DIMS, preferred_element_type=F32)
            dst = (pt * (dpt - dl_ref[0][:, q0:q0 + nq])).astype(BF16)
            dkp = jnp.dot(dst, qb, preferred_element_type=F32)
            rows = pl.ds(pl.multiple_of(i * t + q0, LANES), nq)
            dq_s[rows, :] += lax.dot_general(dst, kb, TN_DIMS, preferred_element_type=F32)
            return dkp, dvp

        @pl.when(i == j)
        def _():
            half = t // 2
            dk_s[0:half, :], dv_s[0:half, :] = block(0, half, 0, t, True)
            dk_s[half:t, :], dv_s[half:t, :] = block(half, half, half, half, True)

        @pl.when(i != j)
        def _():
            dkp, dvp = block(0, t, 0, t, False)
            dk_s[...] += dkp
            dv_s[...] += dvp

        @pl.when(i == n - 1)
        def _():
            dk_ref[0] = (dk_s[...] * LN2).astype(BF16)
            dv_ref[0] = dv_s[...].astype(BF16)

        @pl.when(p == len(pairs) - 1)
        def _():
            dq_ref[0] = dq_s[...].astype(BF16)

    grid_spec = pltpu.PrefetchScalarGridSpec(
        num_scalar_prefetch=2, grid=(H, len(pairs)),
        in_specs=[pl.BlockSpec((1, t, QK_PAD), lambda h, p, qi, ki: (h, qi[p], 0)),
                  pl.BlockSpec((1, t, QK_PAD), lambda h, p, qi, ki: (h, ki[p], 0)),
                  pl.BlockSpec((1, t, V_HEAD), lambda h, p, qi, ki: (h, ki[p], 0)),
                  pl.BlockSpec((t, V_HEAD), lambda h, p, qi, ki: (qi[p], h)),
                  pl.BlockSpec((1, 1, t), lambda h, p, qi, ki: (h, 0, qi[p])),
                  pl.BlockSpec((1, 1, t), lambda h, p, qi, ki: (h, 0, qi[p]))],
        out_specs=[pl.BlockSpec((1, S, QK_PAD), lambda h, p, qi, ki: (h, 0, 0)),
                   pl.BlockSpec((1, t, QK_PAD), lambda h, p, qi, ki: (h, ki[p], 0)),
                   pl.BlockSpec((1, t, V_HEAD), lambda h, p, qi, ki: (h, ki[p], 0))],
        scratch_shapes=[pltpu.VMEM((t, QK_PAD), F32), pltpu.VMEM((t, V_HEAD), F32), pltpu.VMEM((S, QK_PAD), F32)],
    )
    return pl.pallas_call(
        body, name="attn_bwd", grid_spec=grid_spec,
        out_shape=[jax.ShapeDtypeStruct((H, S, QK_PAD), BF16), jax.ShapeDtypeStruct((H, S, QK_PAD), BF16),
                   jax.ShapeDtypeStruct((H, S, V_HEAD), BF16)],
        compiler_params=_params(("parallel", "arbitrary")),
    )(qi, ki, q, k, v, do, lse_row, delta_row)


def _merge_h1(h, z_gates, o, x, w_out, norm_mlp, *, ts):
    S = h.shape[0]
    D = D_MODEL
    n = S // ts
    NBUF = 3

    def body(h_any, g_any, o_any, x_any, w_ref, nrm_ref, m_ref, h1_ref, n2_ref, hbuf, gbuf, obuf, xbuf, sems):
        i = pl.program_id(0)

        def copies(tile, slot):
            rows = pl.ds(pl.multiple_of(tile * ts, ts), ts)
            return [pltpu.make_async_copy(src.at[rows, :], buf.at[slot], sems.at[s, slot])
                    for s, (src, buf) in enumerate(((h_any, hbuf), (g_any, gbuf), (o_any, obuf), (x_any, xbuf)))]

        @pl.when(i == 0)
        def _():
            for t0 in range(min(NBUF - 1, n)):
                for cp in copies(t0, t0):
                    cp.start()

        @pl.when(i + NBUF - 1 < n)
        def _():
            for cp in copies(i + NBUF - 1, (i + NBUF - 1) % NBUF):
                cp.start()

        slot = i % NBUF
        for cp in copies(i, slot):
            cp.wait()
        gates = gbuf[slot].astype(F32)
        gl, _ = _gelu_and_grad(gates[:, 0:D])
        m = (_sigmoid(gates[:, D:2 * D]) * (hbuf[slot].astype(F32) * gl)
             + _sigmoid(gates[:, 2 * D:3 * D]) * obuf[slot]).astype(BF16)
        m_ref[...] = m
        h1 = xbuf[slot] + jnp.dot(m, w_ref[...], preferred_element_type=F32)
        h1_ref[...] = h1
        n2, _ = _rms_fwd(h1, nrm_ref[...])
        n2_ref[...] = n2.astype(BF16)

    col = pl.BlockSpec((ts, D), lambda i: (i, 0))
    hbm = pl.BlockSpec(memory_space=pl.ANY)
    fixed = lambda shape: pl.BlockSpec(shape, lambda i: (0, 0), pipeline_mode=pl.Buffered(1))
    return pl.pallas_call(
        body, name="merge_h1", grid=(n,),
        in_specs=[hbm, hbm, hbm, hbm, fixed((D, D)), fixed((1, D))],
        out_specs=[col, col, col],
        out_shape=[jax.ShapeDtypeStruct((S, D), BF16), jax.ShapeDtypeStruct((S, D), F32),
                   jax.ShapeDtypeStruct((S, D), BF16)],
        scratch_shapes=[pltpu.VMEM((NBUF, ts, D), BF16), pltpu.VMEM((NBUF, ts, 3 * D), BF16),
                        pltpu.VMEM((NBUF, ts, D), F32), pltpu.VMEM((NBUF, ts, D), F32),
                        pltpu.SemaphoreType.DMA((4, NBUF))],
        compiler_params=_params(("arbitrary",)),
    )(h, z_gates, o, x, w_out, norm_mlp)


def _my_place():
    return lax.axis_index("x"), lax.axis_index("y"), lax.axis_index("c")


def _all_gather(shards, *, name):
    n = len(shards)

    def body(*refs):
        x_refs, out_refs = refs[:n], refs[n:2 * n]
        send_sems, recv_sems, local_sems = refs[2 * n:]
        x, y, c = _my_place()
        me, sibling = (x, y, c), (x, y, 1 - c)
        chips = [(1 - x, y), (x, 1 - y), (1 - x, 1 - y)]

        def slot(a, px, py, pc):
            return out_refs[a].at[4 * px + 2 * py + pc]

        def copy(a, k, block, to, src=None):
            return pltpu.make_async_remote_copy(
                src_ref=slot(a, *block) if src is None else src, dst_ref=slot(a, *block),
                send_sem=send_sems.at[7 * a + k], recv_sem=recv_sems.at[7 * a + k], device_id=to, device_id_type=MESH)

        mine = [pltpu.make_async_copy(x_refs[a], slot(a, *me), local_sems.at[a]) for a in range(n)]
        for cp in mine:
            cp.start()
        first = []
        for a in range(n):
            first.append(copy(a, 0, me, sibling, src=x_refs[a]))
            first += [copy(a, 1 + j, me, (*chip, c), src=x_refs[a]) for j, chip in enumerate(chips)]
        for cp in first:
            cp.start()
        passed = []
        for a in range(n):
            for j, chip in enumerate(chips):
                copy(a, 1 + j, (*chip, c), me).wait_recv()
                fwd = copy(a, 4 + j, (*chip, c), sibling)
                fwd.start()
                passed.append(fwd)
        for a in range(n):
            copy(a, 0, sibling, me).wait_recv()
            for j, chip in enumerate(chips):
                copy(a, 4 + j, (*chip, 1 - c), me).wait_recv()
        for cp in first + passed:
            cp.wait_send()
        for cp in mine:
            cp.wait()

    hbm = pl.BlockSpec(memory_space=pl.ANY)
    return pl.pallas_call(
        body, name=name, out_shape=[jax.ShapeDtypeStruct((N_DEV, *s.shape), s.dtype) for s in shards],
        in_specs=[hbm] * n, out_specs=[hbm] * n,
        scratch_shapes=[pltpu.SemaphoreType.DMA((7 * n,)), pltpu.SemaphoreType.DMA((7 * n,)),
                        pltpu.SemaphoreType.DMA((n,))],
    )(*shards)


def _pushes(src_refs, land_refs, send_sems, recv_sems, slab_per_peer):
    x, y, c = _my_place()
    me = 4 * x + 2 * y + c
    copies = []
    for a in range(len(src_refs)):
        for k in range(1, N_DEV):
            px, py, pc = x ^ (k >> 2), y ^ ((k >> 1) & 1), c ^ (k & 1)
            src = src_refs[a].at[4 * px + 2 * py + pc] if slab_per_peer else src_refs[a]
            copies.append(pltpu.make_async_remote_copy(
                src_ref=src, dst_ref=land_refs[a].at[me], send_sem=send_sems.at[7 * a + k - 1],
                recv_sem=recv_sems.at[7 * a + k - 1], device_id=(px, py, pc), device_id_type=MESH))
    return copies


def _push_start(srcs, *, name, slab_per_peer):
    n = len(srcs)
    lands = [lax.empty((N_DEV, *(s.shape[1:] if slab_per_peer else s.shape)), s.dtype) for s in srcs]

    def body(*refs):
        src_refs, land_refs = refs[:n], refs[n:2 * n]
        send_sems, recv_sems, token = refs[2 * n], refs[2 * n + 1], refs[-1]
        for cp in _pushes(src_refs, land_refs, send_sems, recv_sems, slab_per_peer):
            cp.start()
        token[...] = jnp.zeros_like(token)

    hbm = pl.BlockSpec(memory_space=pltpu.HBM)
    sem = pl.BlockSpec(memory_space=pltpu.SEMAPHORE)
    out = pl.pallas_call(
        body, name=name,
        out_shape=(pltpu.SemaphoreType.DMA((7 * n,)), pltpu.SemaphoreType.DMA((7 * n,)),
                   *[pltpu.HBM(a.shape, a.dtype) for a in srcs + lands], jax.ShapeDtypeStruct((8, LANES), F32)),
        in_specs=[hbm] * (2 * n), out_specs=(sem, sem, *[hbm] * (2 * n), pl.BlockSpec(memory_space=pltpu.VMEM)),
        input_output_aliases={i: 2 + i for i in range(2 * n)},
        compiler_params=pltpu.CompilerParams(has_side_effects=pltpu.SideEffectType.DATAFLOW_SIDE_EFFECTING),
    )(*[pltpu.with_memory_space_constraint(a, pltpu.HBM) for a in srcs + lands])
    return out[0], out[1], list(out[2:2 + n]), list(out[2 + n:2 + 2 * n]), out[-1]


def _push_wait(send_sems, recv_sems, srcs, lands, after, *, name, slab_per_peer):
    n = len(srcs)

    def body(*refs):
        src_refs, land_refs = refs[:n], refs[n:2 * n]
        s_sems, r_sems = refs[2 * n], refs[2 * n + 1]
        for cp in _pushes(src_refs, land_refs, s_sems, r_sems, slab_per_peer):
            cp.wait_send()
            cp.wait_recv()

    hbm = pl.BlockSpec(memory_space=pltpu.HBM)
    sem = pl.BlockSpec(memory_space=pltpu.SEMAPHORE)
    out = pl.pallas_call(
        body, name=name, out_shape=tuple(pltpu.HBM(a.shape, a.dtype) for a in srcs + lands),
        in_specs=[hbm] * (2 * n) + [sem, sem, pl.BlockSpec(memory_space=pl.ANY)], out_specs=tuple([hbm] * (2 * n)),
        input_output_aliases={i: i for i in range(2 * n)},
        compiler_params=pltpu.CompilerParams(has_side_effects=pltpu.SideEffectType.DATAFLOW_SIDE_EFFECTING),
    )(*srcs, *lands, send_sems, recv_sems, after)
    return list(out[:n]), list(out[n:])


def _sum_parts(gp_ref, rows):
    g = gp_ref[0, 0:rows, :].astype(F32)
    for p in range(1, gp_ref.shape[0]):
        g = g + gp_ref[p, 0:rows, :].astype(F32)
    return g


def _adamw_update(w, m, v, g):
    m_new = ADAM_B1 * m + (1.0 - ADAM_B1) * g
    v_new = ADAM_B2 * v + (1.0 - ADAM_B2) * (g * g)
    m_hat = m_new / (1.0 - ADAM_B1 ** ADAM_STEP)
    v_hat = v_new / (1.0 - ADAM_B2 ** ADAM_STEP)
    return -ADAM_LR * (m_hat / (jnp.sqrt(v_hat) + ADAM_EPS) + ADAM_WD * w), m_new, v_new


def _adamw_many(ws, ms, vs, gparts, sums, *, name):
    n, k = len(ws), len(sums)

    def body(*refs):
        w_refs, m_refs, v_refs = refs[:n], refs[n:2 * n], refs[2 * n:3 * n]
        g_refs, s_refs, outs = refs[3 * n:4 * n], refs[4 * n:4 * n + k], refs[4 * n + k:]
        for a in range(n):
            g = _sum_parts(g_refs[a], w_refs[a].shape[0])
            d, m_new, v_new = _adamw_update(w_refs[a][...], m_refs[a][...], v_refs[a][...], g)
            for o_ref, val in zip(outs[4 * a:4 * a + 4], (g, d, m_new, v_new)):
                o_ref[...] = val
        for b in range(k):
            outs[4 * n + b][...] = _sum_parts(s_refs[b], s_refs[b].shape[1])

    out_shape = [jax.ShapeDtypeStruct(w.shape, F32) for w in ws for _ in range(4)]
    out_shape += [jax.ShapeDtypeStruct(s.shape[1:], F32) for s in sums]
    return pl.pallas_call(body, name=name, out_shape=out_shape, compiler_params=_params())(
        *ws, *ms, *vs, *gparts, *sums)


def _adamw(w, m, v, gparts, *, tr, name):
    R, C = w.shape
    n_parts = gparts.shape[0]

    def body(w_ref, m_ref, v_ref, gp_ref, g_ref, d_ref, nm_ref, nv_ref):
        g = _sum_parts(gp_ref, tr)
        d_ref[...], nm_ref[...], nv_ref[...] = _adamw_update(w_ref[...], m_ref[...], v_ref[...], g)
        g_ref[...] = g

    row = pl.BlockSpec((tr, C), lambda i: (i, 0))
    shp = jax.ShapeDtypeStruct((R, C), F32)
    return pl.pallas_call(
        body, name=name, grid=(R // tr,),
        in_specs=[row, row, row, pl.BlockSpec((n_parts, tr, C), lambda i: (0, i, 0))],
        out_specs=[row, row, row, row], out_shape=[shp, shp, shp, shp],
        compiler_params=_params(("parallel",)),
    )(w, m, v, gparts)


def _rope_tables(s):
    pos = jnp.arange(s, dtype=F32)
    inv_freq = 1.0 / (ROPE_THETA ** (jnp.arange(0, QK_ROPE, 2, dtype=F32) / QK_ROPE))
    per_lane = jnp.concatenate([inv_freq, inv_freq, jnp.zeros((LANES - QK_ROPE,), F32)])
    ang = pos[:, None] * per_lane[None, :]
    live = jnp.arange(LANES) < QK_ROPE
    return jnp.where(live, jnp.cos(ang), 0.0), jnp.where(live, jnp.sin(ang), 0.0)


def _pick(n, want):
    t = min(n, want)
    assert n % t == 0
    return t


def _local_step(x, target, wts, small, hooks):
    S = x.shape[0]
    H = MLA_HEADS
    ts = _pick(S, 1024)
    tm = _pick(S, 512)
    tm_wide = _pick(S, 1024)
    tk_s = _pick(S, 4096)
    tt = _pick(S, 512)
    ta = _pick(S, 1024)
    ts_proj = _pick(S, 512)
    ts_merge = _pick(S, 512)
    row = lambda v: v.reshape(1, -1)
    w_in = wts["w_in"]
    w_main = jnp.concatenate([w_in[:, 0:2048], w_in[:, 2624:4672]], axis=1)
    w_ckv = jnp.concatenate([w_in[:, 2048:2624], jnp.zeros((D_MODEL, CKV_W - 576), BF16)], axis=1)
    w_uq3 = wts["w_uq"].reshape(Q_LORA, H, QK_NOPE + QK_ROPE)
    w_uq_p = jnp.concatenate(
        [w_uq3[:, :, :QK_NOPE].reshape(Q_LORA, H * QK_NOPE),
         jnp.pad(w_uq3[:, :, QK_NOPE:], ((0, 0), (0, 0), (0, LANES - QK_ROPE))).reshape(Q_LORA, H * LANES)], axis=1)
    w_ukv = wts["w_ukv"]
    cos, sin = _rope_tables(S)
    conv_w, conv_b = small["conv_w"], row(small["conv_b"])
    wa, wx = small["lru_wa"].astype(BF16), small["lru_wx"].astype(BF16)
    wat, wxt = jnp.swapaxes(wa, 1, 2), jnp.swapaxes(wx, 1, 2)
    ba, bx = small["lru_ba"].reshape(RNN_BLOCKS, 1, RNN_BLOCK_W), small["lru_bx"].reshape(RNN_BLOCKS, 1, RNN_BLOCK_W)
    lam = row(small["lru_lambda"])
    q_norm, kv_norm = row(small["q_norm"]), row(small["kv_norm"])
    norm_mix, norm_mlp, norm_final = row(small["norm_mix"]), row(small["norm_mlp"]), row(small["norm_final"])

    xn = _rmsnorm_cast(x, norm_mix, ts=ts, name="norm_mix")
    ident = lambda acc: (acc,)
    z_rx, z_ckv = _mm(xn, w_main[:, :D_MODEL], name="z_rx_ckv", tm=tm_wide, tn=1024, tk=1024, outs=[("tile", F32)],
                      epilogue=ident, also=w_ckv)
    (z_gates,) = _mm(xn, w_main[:, D_MODEL:], name="z_gates", tm=tm_wide, tn=3 * D_MODEL, tk=1024, outs=[("tile", BF16)],
                     epilogue=ident)
    h = _lru_fwd(z_rx, conv_w, conv_b, wa, ba, wx, bx, lam, tt=tt)
    q, k, v = _mla_proj(z_ckv, q_norm, kv_norm, w_uq_p, w_ukv, cos, sin, ts=ts_proj)
    o, lse = _attn_fwd(q, k, v, t=ta, hb=4)
    w_out, w_up, w_down = hooks["weights_later"](o)
    merged, h1, n2 = _merge_h1(h, z_gates, o, x, w_out, norm_mlp, ts=ts_merge)

    def ep_up(acc):
        r = jnp.maximum(acc, 0.0)
        return r * r, r

    act, relu = _mm(n2, w_up, name="up", tm=tm_wide, tn=2048, tk=1024, outs=[("tile", BF16), ("tile", BF16)],
                    epilogue=ep_up)

    def ep_loss(acc, h1v, tgt, g):
        h2 = acc + h1v
        y, _ = _rms_fwd(h2, g)
        err = y - tgt
        loss_rows = 0.5 * jnp.mean(err * err, axis=-1, keepdims=True)
        dy = err * (1.0 / D_MODEL)
        dh2, dg_rows = _rms_bwd(dy, h2, g)
        lsum = jnp.sum(loss_rows, axis=0, keepdims=True)
        return dh2, dh2, jnp.sum(dg_rows, axis=0, keepdims=True), jnp.broadcast_to(lsum, (1, D_MODEL))

    dh2, dh2b, dnf_p, loss_p = _mm(
        act, w_down, name="down_loss", tm=tm, tn=1024, tk=D_FF,
        outs=[("tile", F32), ("tile", BF16), ("rowpart", F32), ("rowpart", F32)], epilogue=ep_loss,
        extras=[("tile", h1), ("tile", target), ("row", norm_final)])
    loss_part = jnp.sum(loss_p[:, 0, 0])
    d_norm_final = jnp.sum(dnf_p, axis=(0, 1))

    def ep_du(acc, r):
        return (acc * (2.0 * r.astype(F32)),)

    (du,) = _mm(dh2b, w_down, name="d_act", tb=True, tm=tm_wide, tn=2048, tk=1024, outs=[("tile", BF16)], epilogue=ep_du,
                extras=[("tile", relu)])

    def ep_dh1(acc, h1v, dh2v, g):
        dv, dg_rows = _rms_bwd(acc, h1v, g)
        dh1 = dh2v + dv
        return dh1, dh1, jnp.sum(dg_rows, axis=0, keepdims=True)

    dh1, dh1b, dnm_p = _mm(du, w_up, name="d_n2", tb=True, tm=tm, tn=1024, tk=D_FF,
                           outs=[("tile", F32), ("tile", BF16), ("rowpart", F32)], epilogue=ep_dh1,
                           extras=[("tile", h1), ("tile", dh2), ("row", norm_mlp)])
    d_norm_mlp = jnp.sum(dnm_p, axis=(0, 1))
    tn_mm = functools.partial(_mm, ta=True, tk=tk_s, outs=[("tile", BF16)], epilogue=ident)
    (d_w_down,) = tn_mm(act, dh2b, name="dw_down", tm=1024, tn=1024)
    (p_w_up,) = _mm(n2, du, name="dw_up", ta=True, tk=tk_s, tm=1024, tn=D_FF // N_DEV, outs=[("colshard", BF16)],
                    epilogue=ident)
    (d_w_out,) = tn_mm(merged, dh1b, name="dw_out", tm=1024, tn=1024)
    early = [d_w_out.reshape(N_DEV, -1, D_MODEL), p_w_up, d_w_down.reshape(N_DEV, -1, D_MODEL)]
    w_out = w_out + hooks["send"]("early", early)[0, 0].astype(BF16)


    def ep_dmerge(dm, hv, rg, ga, gb, ov):
        hv, rg, ga, gb = hv.astype(F32), rg.astype(F32), ga.astype(F32), gb.astype(F32)
        gl, dgl = _gelu_and_grad(rg)
        sa, sb = _sigmoid(ga), _sigmoid(gb)
        ya = hv * gl
        dya = dm * sa
        do = dm * sb
        dga = dm * ya * sa * (1.0 - sa)
        dgb = dm * ov * sb * (1.0 - sb)
        dh = dya * gl
        drg = dya * hv * dgl
        dov = do * ov
        lane = lax.broadcasted_iota(jnp.int32, (dm.shape[0], LANES), 1)
        delta = jnp.zeros((dm.shape[0], LANES), F32)
        for hh in range(H):
            dsum = jnp.sum(dov[:, hh * V_HEAD:(hh + 1) * V_HEAD], axis=1, keepdims=True)
            delta = jnp.where(lane == hh, dsum, delta)
        return dh, jnp.concatenate([drg, dga, dgb], axis=1), do, delta

    dh_lru, dz_part, do, delta_w = _mm(
        dh1b, w_out, name="d_merge", tb=True, tm=ts_merge, tn=1024, tk=1024,
        outs=[("tile", F32), ("cols", BF16, 4 * D_MODEL, D_MODEL), ("tile", BF16), ("side", F32)],
        epilogue=ep_dmerge,
        extras=[("tile", h), ("tilecol", z_gates, 0), ("tilecol", z_gates, 1), ("tilecol", z_gates, 2), ("tile", o)])
    delta_row = delta_w[:, :H].T.reshape(H, 1, S)
    lse_row = lse

    dq, dk, dv = _attn_bwd(q, k, v, do, lse_row, delta_row, t=ta)
    dz_ckv, d_w_uq_p, d_w_ukv, d_q_norm, d_kv_norm = _mla_proj_bwd(
        z_ckv, dq, dk, dv, q_norm, kv_norm, w_uq_p.T, w_ukv.T, cos, sin, ts=ts_proj)
    d_w_uq = jnp.concatenate(
        [d_w_uq_p[:, :H * QK_NOPE].reshape(Q_LORA, H, QK_NOPE),
         d_w_uq_p[:, H * QK_NOPE:].reshape(Q_LORA, H, LANES)[:, :, :QK_ROPE]], axis=2).reshape(Q_LORA, -1)

    dz_main, d_wa, d_wx, d_ba, d_bx, d_lam, d_conv_w, d_conv_b = _lru_bwd(
        z_rx, h, dh_lru, dz_part, conv_w, conv_b, wa, wat, ba, wx, wxt, bx, lam, tt=tt)

    (d_w_main,) = tn_mm(xn, dz_main, name="dw_main", tm=1024, tn=1024)
    (d_w_ckv,) = tn_mm(xn, dz_ckv, name="dw_ckv", tm=1024, tn=CKV_W)
    d_w_in = jnp.concatenate([d_w_main[:, 0:2048], d_w_ckv[:, 0:576], d_w_main[:, 2048:4096]], axis=1)

    def col_parts(full):
        r = full.shape[0]
        return jnp.transpose(full.astype(BF16).reshape(r, N_DEV, -1), (1, 0, 2))

    late = [col_parts(d_w_in), col_parts(d_w_uq), col_parts(d_w_ukv)]
    norm_mix = norm_mix + hooks["send"]("late", late)[0, 0]

    def ep_dx(acc, xv, dh1v, g):
        dv, dg_rows = _rms_bwd(acc, xv, g)
        return dh1v + dv, jnp.sum(dg_rows, axis=0, keepdims=True)

    grad_x, dnx_p = _mm(dz_main, w_main, name="dx", tb=True, tm=tm, tn=1024, tk=4 * D_MODEL,
                        outs=[("tile", F32), ("rowpart", F32)], epilogue=ep_dx, more=(dz_ckv, w_ckv),
                        extras=[("tile", x), ("tile", dh1), ("row", norm_mix)])
    d_norm_mix = jnp.sum(dnx_p, axis=(0, 1))
    sm = {"norm_mix": d_norm_mix, "conv_w": d_conv_w, "conv_b": d_conv_b.reshape(-1), "lru_wa": d_wa,
          "lru_ba": d_ba.reshape(RNN_BLOCKS, RNN_BLOCK_W), "lru_wx": d_wx, "lru_bx": d_bx.reshape(RNN_BLOCKS, RNN_BLOCK_W),
          "lru_lambda": d_lam.reshape(-1), "q_norm": d_q_norm.reshape(-1), "kv_norm": d_kv_norm.reshape(-1),
          "norm_mlp": d_norm_mlp, "norm_final": d_norm_final}
    return loss_part, grad_x, sm


SMALL = ("norm_mix", "conv_b", "lru_wa", "lru_ba", "lru_wx", "lru_bx", "lru_lambda", "q_norm", "kv_norm", "norm_mlp",
         "norm_final")
WEIGHTS = ("norm_mix", "w_in", "conv_w", "conv_b", "lru_wa", "lru_ba", "lru_wx", "lru_bx", "lru_lambda", "q_norm", "w_uq",
           "kv_norm", "w_ukv", "w_out", "norm_mlp", "w_up", "w_down", "norm_final")
ADAM_TILE_ROWS = {"w_in": 256, "w_uq": 128, "w_ukv": 128, "w_out": 64, "w_up": 256, "w_down": 128}
CONV_ROWS = N_DEV * 8


def _rows(a):
    return a.reshape(-1, LANES)


def _pad_rows(a, mult):
    r = a.shape[-2]
    pad = (-r) % mult
    if pad == 0:
        return a
    cfg = [(0, 0)] * (a.ndim - 2) + [(0, pad), (0, 0)]
    return jnp.pad(a, cfg)


def _cols_from_shards(g):
    return jnp.transpose(g, (1, 0, 2)).reshape(g.shape[1], -1)


def kernel(x, norm_mix, w_in, conv_w, conv_b, lru_wa, lru_ba, lru_wx, lru_bx, lru_lambda, q_norm, w_uq, kv_norm, w_ukv, w_out, norm_mlp, w_up, w_down, norm_final, loss_target, m_norm_mix, m_w_in, m_conv_w, m_conv_b, m_lru_wa, m_lru_ba, m_lru_wx, m_lru_bx, m_lru_lambda, m_q_norm, m_w_uq, m_kv_norm, m_w_ukv, m_w_out, m_norm_mlp, m_w_up, m_w_down, m_norm_final, v_norm_mix, v_w_in, v_conv_w, v_conv_b, v_lru_wa, v_lru_ba, v_lru_wx, v_lru_bx, v_lru_lambda, v_q_norm, v_w_uq, v_kv_norm, v_w_ukv, v_w_out, v_norm_mlp, v_w_up, v_w_down, v_norm_final):
    W = dict(norm_mix=norm_mix, w_in=w_in, conv_w=conv_w, conv_b=conv_b, lru_wa=lru_wa, lru_ba=lru_ba, lru_wx=lru_wx,
             lru_bx=lru_bx, lru_lambda=lru_lambda, q_norm=q_norm, w_uq=w_uq, kv_norm=kv_norm, w_ukv=w_ukv, w_out=w_out,
             norm_mlp=norm_mlp, w_up=w_up, w_down=w_down, norm_final=norm_final)
    M = dict(norm_mix=m_norm_mix, w_in=m_w_in, conv_w=m_conv_w, conv_b=m_conv_b, lru_wa=m_lru_wa, lru_ba=m_lru_ba,
             lru_wx=m_lru_wx, lru_bx=m_lru_bx, lru_lambda=m_lru_lambda, q_norm=m_q_norm, w_uq=m_w_uq, kv_norm=m_kv_norm,
             w_ukv=m_w_ukv, w_out=m_w_out, norm_mlp=m_norm_mlp, w_up=m_w_up, w_down=m_w_down, norm_final=m_norm_final)
    V = dict(norm_mix=v_norm_mix, w_in=v_w_in, conv_w=v_conv_w, conv_b=v_conv_b, lru_wa=v_lru_wa, lru_ba=v_lru_ba,
             lru_wx=v_lru_wx, lru_bx=v_lru_bx, lru_lambda=v_lru_lambda, q_norm=v_q_norm, w_uq=v_w_uq, kv_norm=v_kv_norm,
             w_ukv=v_w_ukv, w_out=v_w_out, norm_mlp=v_norm_mlp, w_up=v_w_up, w_down=v_w_down, norm_final=v_norm_final)
    me = 4 * lax.axis_index("x") + 2 * lax.axis_index("y") + lax.axis_index("c")

    first, later = ("w_in", "w_uq", "w_ukv"), ("w_out", "w_up", "w_down")
    got = _all_gather([W[n].astype(BF16) for n in first] + [_pad_rows(conv_w, 8)], name="gather_weights")
    wts = {"w_in": _cols_from_shards(got[0]), "w_uq": _cols_from_shards(got[1]), "w_ukv": _cols_from_shards(got[2])}
    w_send, w_recv, w_src, w_land, zeros = _push_start([W[n].astype(BF16) for n in later], name="gather_later_start",
                                                       slab_per_peer=False)
    small = {n: W[n] for n in SMALL}
    small["conv_w"] = _cols_from_shards(got[3][:, :CONV_WIDTH])
    small["norm_mix"] = norm_mix + zeros[0, 0]

    def with_own_slab(land, mine):
        return lax.dynamic_update_slice(land, mine, (me, 0, 0))

    def weights_later(after):
        srcs, lands = _push_wait(w_send, w_recv, w_src, w_land, after, name="gather_later_wait", slab_per_peer=False)
        w_out_g, w_up_g, w_down_g = [with_own_slab(l, s[None]) for l, s in zip(lands, srcs)]
        return w_out_g.reshape(-1, D_MODEL), _cols_from_shards(w_up_g), w_down_g.reshape(-1, D_MODEL)

    sent = {}
    G, Dl, NM, NV = {}, {}, {}, {}

    def finish(group, names, after):
        s_sems, r_sems, srcs, lands, _ = sent[group]
        srcs, lands = _push_wait(s_sems, r_sems, srcs, lands, after, name="exchange_" + group + "_wait",
                                 slab_per_peer=True)
        for n, src, land in zip(names, srcs, lands):
            parts = with_own_slab(land, lax.dynamic_slice(src, (me, 0, 0), (1, *src.shape[1:])))
            G[n], Dl[n], NM[n], NV[n] = _adamw(W[n], M[n], V[n], parts, tr=ADAM_TILE_ROWS[n], name="adamw_" + n)

    def send(group, parts):
        sent[group] = _push_start(parts, name="exchange_" + group + "_start", slab_per_peer=True)
        zeros = sent[group][4]
        if group == "late":
            finish("early", later, zeros)
            zeros = zeros + 0.0 * (Dl["w_out"][0:8, 0:LANES] + Dl["w_up"][0:8, 0:LANES] + Dl["w_down"][0:8, 0:LANES])
        return zeros

    loss_part, grad_x, g_small = _local_step(x[0], loss_target[0], wts, small,
                                              {"weights_later": weights_later, "send": send})
    finish("late", first, grad_x)

    conv_rows = _pad_rows(jnp.transpose(g_small["conv_w"].reshape(CONV_WIDTH, N_DEV, LANES), (1, 0, 2)), 8)
    loss_rows = jnp.zeros((8, LANES), F32).at[0, 0].set(loss_part)
    as_sent = lambda n: g_small[n].astype(BF16) if n in ("lru_wa", "lru_wx") else g_small[n]
    gathered = _all_gather([_pad_rows(_rows(as_sent(n)), 8) for n in SMALL]
                           + [conv_rows.reshape(CONV_ROWS, LANES), loss_rows], name="gather_small")
    k = len(SMALL)
    outs = _adamw_many([_rows(W[n]) for n in SMALL], [_rows(M[n]) for n in SMALL], [_rows(V[n]) for n in SMALL],
                       gathered[:k], gathered[k:], name="adamw_small")
    for j, n in enumerate(SMALL):
        for out, o in zip((G, Dl, NM, NV), outs[4 * j:4 * j + 4]):
            out[n] = o.reshape(W[n].shape)
    conv_sum, loss_sum = outs[4 * k:]
    loss = loss_sum[0, 0]

    g_conv = lax.dynamic_slice(conv_sum, (me * 8, 0), (8, LANES))
    conv_out = _adamw(_pad_rows(conv_w, 8), _pad_rows(m_conv_w, 8), _pad_rows(v_conv_w, 8), g_conv[None], tr=8,
                      name="adamw_conv_w")
    for out, pk in zip((G, Dl, NM, NV), conv_out):
        out["conv_w"] = pk[:CONV_WIDTH]
    return (loss, grad_x[None], *[G[n] for n in WEIGHTS], *[Dl[n] for n in WEIGHTS], *[NM[n] for n in WEIGHTS],
            *[NV[n] for n in WEIGHTS])
```

```python
import functools

import numpy as np
import jax
import jax.numpy as jnp
from jax import lax
from jax.experimental import pallas as pl
from jax.experimental.pallas import tpu as pltpu

F32 = jnp.float32
BF16 = jnp.bfloat16
MESH = pl.DeviceIdType.MESH

D_MODEL = 1024
N_DEV = 8
LANES = 128
RNN_BLOCKS = 8
RNN_BLOCK_W = 128
CONV_WIDTH = 4
LRU_C = 8.0
MLA_HEADS = 8
QK_NOPE = 128
QK_ROPE = 64
V_HEAD = 128
QK_PAD = 256
Q_LORA = 256
KV_LORA = 256
CKV_W = 640
ROPE_THETA = 10000.0
D_FF = 4096
EPS = 1e-6
ATTN_SCALE = (QK_NOPE + QK_ROPE) ** -0.5
LOG2E = 1.4426950408889634
LN2 = 0.6931471805599453
NEG = -1e30

ADAM_LR = 0.001
ADAM_B1 = 0.9
ADAM_B2 = 0.999
ADAM_EPS = 1e-08
ADAM_WD = 0.01
ADAM_STEP = 10

VMEM_LIMIT = 56 * 1024 * 1024


def _params(sem=None):
    return pltpu.CompilerParams(dimension_semantics=sem, vmem_limit_bytes=VMEM_LIMIT)


def _sigmoid(v):
    return 1.0 / (1.0 + jnp.exp(-v))


def _softplus(y):
    e = jnp.exp(-jnp.abs(y))
    u = 1.0 + e
    d = u - 1.0
    l1p = jnp.where(d == 0.0, e, jnp.log(u) * e / jnp.where(d == 0.0, 1.0, d))
    return jnp.maximum(y, 0.0) + l1p


_GELU_K = 0.7978845608028654
_GELU_C = 0.044715


def _gelu_and_grad(v):
    t = jnp.tanh(_GELU_K * (v + _GELU_C * v * v * v))
    g = 0.5 * v * (1.0 + t)
    dg = 0.5 * (1.0 + t) + 0.5 * v * (1.0 - t * t) * _GELU_K * (1.0 + 3.0 * _GELU_C * v * v)
    return g, dg


def _rms_fwd(v, g):
    rstd = lax.rsqrt(jnp.mean(v * v, axis=-1, keepdims=True) + EPS)
    return v * rstd * g, rstd


def _rms_bwd(dy, v, g):
    rstd = lax.rsqrt(jnp.mean(v * v, axis=-1, keepdims=True) + EPS)
    vh = v * rstd
    dvh = dy * g
    dv = rstd * (dvh - vh * jnp.mean(dvh * vh, axis=-1, keepdims=True))
    return dv, dy * vh


def _shift_down(v, s, fill, row):
    return jnp.where(row >= s, pltpu.roll(v, s, 0), fill)


def _shift_up(v, s, fill, row, n):
    return jnp.where(row < n - s, pltpu.roll(v, n - s, 0), fill)


def _rot_half(v, lane):
    n = v.shape[-1]
    l = lane & (LANES - 1)
    up = pltpu.roll(v, n - QK_ROPE // 2, 1)
    dn = pltpu.roll(v, QK_ROPE // 2, 1)
    return jnp.where(l < QK_ROPE // 2, -up, jnp.where(l < QK_ROPE, dn, 0.0))


def _mm(a, b, *, name, tm, tn, tk, outs, epilogue, extras=(), ta=False, tb=False, more=None, also=None):
    assert not (ta and tb)
    if ta:
        K, M = a.shape
    else:
        M, K = a.shape
    if tb:
        N, K2 = b.shape
    else:
        K2, N = b.shape
    assert K == K2 and M % tm == 0 and N % tn == 0 and K % tk == 0, (name, a.shape, b.shape)
    n_i, n_j, n_k = M // tm, N // tn, K // tk
    n_ex, n_out = len(extras), len(outs)
    n_more = 0 if more is None else 2
    n_also = 0 if also is None else 1
    assert more is None or (n_k == 1 and not ta)
    assert also is None or (n_k == 1 and n_j == 1 and not ta and more is None)

    def body(*refs):
        a_ref, b_ref = refs[0], refs[1]
        ex_refs = refs[2 + n_more:2 + n_more + n_ex]
        first_out = 2 + n_more + n_ex + n_also
        out_refs = refs[first_out:first_out + n_out]
        if ta:
            part = lax.dot_general(a_ref[...], b_ref[...], (((0,), (0,)), ((), ())), preferred_element_type=F32)
        elif tb:
            part = lax.dot_general(a_ref[...], b_ref[...], (((1,), (1,)), ((), ())), preferred_element_type=F32)
        else:
            part = jnp.dot(a_ref[...], b_ref[...], preferred_element_type=F32)
        if more is not None:
            part = part + lax.dot_general(refs[2][...], refs[3][...], (((1,), (1,)), ((), ())),
                                          preferred_element_type=F32)

        def finish(acc):
            res = epilogue(acc, *[r[...] for r in ex_refs])
            for o_ref, r, spec in zip(out_refs, res, outs):
                if spec[0] == "cols":
                    o_ref[:, spec[3]:spec[3] + r.shape[1]] = r.astype(o_ref.dtype)
                else:
                    o_ref[...] = r.astype(o_ref.dtype).reshape(o_ref.shape)

        if also is not None:
            refs[first_out + n_out][...] = jnp.dot(a_ref[...], refs[first_out - 1][...], preferred_element_type=F32)
        if n_k == 1:
            finish(part)
        else:
            acc_ref = refs[-1]
            k = pl.program_id(2)

            @pl.when(k == 0)
            def _():
                acc_ref[...] = part

            @pl.when(k > 0)
            def _():
                acc_ref[...] += part

            @pl.when(k == n_k - 1)
            def _():
                finish(acc_ref[...])

    a_spec = pl.BlockSpec((tk, tm), lambda j, i, k: (k, i)) if ta else pl.BlockSpec((tm, tk), lambda j, i, k: (i, k))
    b_once = dict(pipeline_mode=pl.Buffered(1)) if (n_j == 1 and n_k == 1) else {}
    if tb:
        in_specs = [a_spec, pl.BlockSpec((tn, tk), lambda j, i, k: (j, k), **b_once)]
    else:
        in_specs = [a_spec, pl.BlockSpec((tk, tn), lambda j, i, k: (k, j), **b_once)]
    if more is not None:
        k2 = more[0].shape[1]
        in_specs += [pl.BlockSpec((tm, k2), lambda j, i, k: (i, 0)), pl.BlockSpec((tn, k2), lambda j, i, k: (j, 0), **b_once)]
    for ex in extras:
        kind = ex[0]
        if kind == "tile":
            in_specs.append(pl.BlockSpec((tm, tn), lambda j, i, k: (i, j)))
        elif kind == "tilecol":
            assert n_j == 1
            in_specs.append(pl.BlockSpec((tm, tn), functools.partial(lambda c, j, i, k: (i, c), ex[2])))
        else:
            in_specs.append(pl.BlockSpec((1, tn), lambda j, i, k: (0, j)))
    out_specs, out_shape = [], []
    for kind, dt, *rest in outs:
        if kind == "tile":
            out_specs.append(pl.BlockSpec((tm, tn), lambda j, i, k: (i, j)))
            out_shape.append(jax.ShapeDtypeStruct((M, N), dt))
        elif kind == "colshard":
            out_specs.append(pl.BlockSpec((1, tm, tn), lambda j, i, k: (j, i, 0)))
            out_shape.append(jax.ShapeDtypeStruct((n_j, M, tn), dt))
        elif kind == "cols":
            assert n_j == 1
            out_specs.append(pl.BlockSpec((tm, rest[0]), lambda j, i, k: (i, 0)))
            out_shape.append(jax.ShapeDtypeStruct((M, rest[0]), dt))
        elif kind == "side":
            assert n_j == 1
            out_specs.append(pl.BlockSpec((tm, LANES), lambda j, i, k: (i, 0)))
            out_shape.append(jax.ShapeDtypeStruct((M, LANES), dt))
        else:
            out_specs.append(pl.BlockSpec((1, 1, tn), lambda j, i, k: (i, 0, j)))
            out_shape.append(jax.ShapeDtypeStruct((n_i, 1, N), dt))
    scratch = [pltpu.VMEM((tm, tn), F32)] if n_k > 1 else []
    if also is not None:
        in_specs.append(pl.BlockSpec(also.shape, lambda j, i, k: (0, 0), **b_once))
        out_specs.append(pl.BlockSpec((tm, also.shape[1]), lambda j, i, k: (i, 0)))
        out_shape.append(jax.ShapeDtypeStruct((M, also.shape[1]), F32))
    return pl.pallas_call(
        body, name=name, grid=(n_j, n_i, n_k), in_specs=in_specs, out_specs=out_specs, out_shape=out_shape,
        scratch_shapes=scratch, compiler_params=_params(("parallel", "parallel", "arbitrary")),
    )(a, b, *(more or ()), *[ex[1] for ex in extras], *([] if also is None else [also]))


def _rmsnorm_cast(x, g, *, ts, name):
    S, D = x.shape

    def body(x_ref, g_ref, o_ref):
        y, _ = _rms_fwd(x_ref[...], g_ref[...])
        o_ref[...] = y.astype(BF16)

    return pl.pallas_call(
        body, name=name, grid=(S // ts,),
        in_specs=[pl.BlockSpec((ts, D), lambda i: (i, 0)), pl.BlockSpec((1, D), lambda i: (0, 0))],
        out_specs=pl.BlockSpec((ts, D), lambda i: (i, 0)), out_shape=jax.ShapeDtypeStruct((S, D), BF16),
        compiler_params=_params(("parallel",)),
    )(x, g)


LRU_NB = 4


def _lru_gates(xa, wa_ref, ba_ref, wx_ref, bx_ref, lam):
    xab = xa.astype(BF16)
    W = RNN_BLOCK_W
    rs, is_ = [], []
    for j in range(LRU_NB):
        xj = xab[:, j * W:(j + 1) * W]
        rs.append(_sigmoid(jnp.dot(xj, wa_ref[j], preferred_element_type=F32) + ba_ref[j]))
        is_.append(_sigmoid(jnp.dot(xj, wx_ref[j], preferred_element_type=F32) + bx_ref[j]))
    r = jnp.concatenate(rs, axis=1)
    i = jnp.concatenate(is_, axis=1)
    sp = _softplus(-lam)
    log_a = (-LRU_C * r) * sp
    a = jnp.exp(log_a)
    y = 2.0 * log_a
    one_m = jnp.where(y > -0.01, -y * (1.0 + 0.5 * y * (1.0 + y * (1.0 / 3.0))), 1.0 - a * a)
    return r, i, sp, a, jnp.sqrt(one_m)


def _rows_before(x, tail8, k):
    e16 = jnp.concatenate([tail8, x[0:8, :]], axis=0)
    return jnp.concatenate([pltpu.roll(e16, k, 0)[8:16, :], pltpu.roll(x, k, 0)[8:, :]], axis=0)


def _rows_after(x, head8, k):
    tt = x.shape[0]
    e16 = jnp.concatenate([x[tt - 8:tt, :], head8], axis=0)
    return jnp.concatenate([pltpu.roll(x, tt - k, 0)[:tt - 8, :], pltpu.roll(e16, 16 - k, 0)[0:8, :]], axis=0)


def _scan_down(a, b, h0, a_s, b_s, c_s):
    tt, C = a.shape
    G, nch = tt // 8, C // LANES
    rin = lax.broadcasted_iota(jnp.int32, (tt, C), 0) & 7

    def in_group(v, s):
        return pltpu.roll(v.reshape(G, 8, C), s, 1).reshape(tt, C)

    A, B = a, b
    for s in (1, 2, 4):
        B = A * jnp.where(rin >= s, in_group(B, s), 0.0) + B
        A = A * jnp.where(rin >= s, in_group(A, s), 1.0)
    for j in range(nch):
        a_s[j] = A[:, j * LANES:(j + 1) * LANES]
        b_s[j] = B[:, j * LANES:(j + 1) * LANES]
    At = jnp.concatenate([a_s.at[j][pl.ds(7, G, stride=8), :] for j in range(nch)], axis=1)
    Bt = jnp.concatenate([b_s.at[j][pl.ds(7, G, stride=8), :] for j in range(nch)], axis=1)
    rowg = lax.broadcasted_iota(jnp.int32, (G, C), 0)
    s = 1
    while s < G:
        Bt = At * _shift_down(Bt, s, 0.0, rowg) + Bt
        At = At * _shift_down(At, s, 1.0, rowg)
        s *= 2
    hg = At * h0 + Bt
    cin = _shift_down(hg, 1, h0, rowg)
    for j in range(nch):
        for r in range(8):
            c_s.at[j][pl.ds(r, G, stride=8), :] = cin[:, j * LANES:(j + 1) * LANES]
    return A * jnp.concatenate([c_s[j] for j in range(nch)], axis=1) + B, hg[G - 1:G, :]


def _scan_up(c, g_in, g_next, a_s, b_s, c_s):
    tt, C = c.shape
    G, nch = tt // 8, C // LANES
    rin = lax.broadcasted_iota(jnp.int32, (tt, C), 0) & 7

    def in_group(v, s):
        return pltpu.roll(v.reshape(G, 8, C), 8 - s, 1).reshape(tt, C)

    Cc, Gv = c, g_in
    for s in (1, 2, 4):
        Gv = Gv + Cc * jnp.where(rin < 8 - s, in_group(Gv, s), 0.0)
        Cc = Cc * jnp.where(rin < 8 - s, in_group(Cc, s), 1.0)
    for j in range(nch):
        a_s[j] = Cc[:, j * LANES:(j + 1) * LANES]
        b_s[j] = Gv[:, j * LANES:(j + 1) * LANES]
    Ct = jnp.concatenate([a_s.at[j][pl.ds(0, G, stride=8), :] for j in range(nch)], axis=1)
    Gt = jnp.concatenate([b_s.at[j][pl.ds(0, G, stride=8), :] for j in range(nch)], axis=1)
    rowg = lax.broadcasted_iota(jnp.int32, (G, C), 0)
    s = 1
    while s < G:
        Gt = Gt + Ct * _shift_up(Gt, s, 0.0, rowg, G)
        Ct = Ct * _shift_up(Ct, s, 1.0, rowg, G)
        s *= 2
    gg = Gt + Ct * g_next
    cin = _shift_up(gg, 1, g_next, rowg, G)
    for j in range(nch):
        for r in range(8):
            c_s.at[j][pl.ds(r, G, stride=8), :] = cin[:, j * LANES:(j + 1) * LANES]
    return Gv + Cc * jnp.concatenate([c_s[j] for j in range(nch)], axis=1), gg[0:1, :]


def _lru_fwd(z_rx, conv_w, conv_b, wa, ba, wx, bx, lam, *, tt):
    S = z_rx.shape[0]
    n_t = S // tt
    BW = RNN_BLOCK_W
    W = LRU_NB * BW

    def body(x_ref, cw_ref, cb_ref, wa_ref, ba_ref, wx_ref, bx_ref, lam_ref, h_ref, tail, hc, a_s, b_s, c_s):
        t = pl.program_id(1)

        @pl.when(t == 0)
        def _():
            tail[...] = jnp.zeros((8, W), F32)
            hc[...] = jnp.zeros((8, W), F32)

        x = x_ref[...]
        before = tail[...]
        cw = cw_ref[...]
        xa = (cb_ref[...] + cw[3:4] * x + cw[2:3] * _rows_before(x, before, 1) + cw[1:2] * _rows_before(x, before, 2)
              + cw[0:1] * _rows_before(x, before, 3))
        tail[...] = x[tt - 8:tt, :]
        _r, i, _sp, a, mult = _lru_gates(xa, wa_ref, ba_ref, wx_ref, bx_ref, lam_ref[...])
        h, h_last = _scan_down(a, mult * (i * xa), hc[0:1, :], a_s, b_s, c_s)
        h_ref[...] = h.astype(BF16)
        hc[...] = jnp.broadcast_to(h_last, (8, W))

    blk = lambda n, t: (t, n)
    vec = pl.BlockSpec((1, W), lambda n, t: (0, n))
    mat = pl.BlockSpec((LRU_NB, BW, BW), lambda n, t: (n, 0, 0))
    bias = pl.BlockSpec((LRU_NB, 1, BW), lambda n, t: (n, 0, 0))
    row8 = pltpu.VMEM((8, W), F32)
    wide = pltpu.VMEM((LRU_NB, tt, LANES), F32)
    return pl.pallas_call(
        body, name="lru_fwd", grid=(RNN_BLOCKS // LRU_NB, n_t),
        in_specs=[pl.BlockSpec((tt, W), blk), pl.BlockSpec((CONV_WIDTH, W), lambda n, t: (0, n)), vec, mat, bias, mat,
                  bias, vec],
        out_specs=pl.BlockSpec((tt, W), blk), out_shape=jax.ShapeDtypeStruct((S, D_MODEL), BF16),
        scratch_shapes=[row8, row8, wide, wide, wide],
        compiler_params=_params(("parallel", "arbitrary")),
    )(z_rx, conv_w, conv_b, wa, ba, wx, bx, lam)


def _lru_bwd(z_rx, h, dh, dz, conv_w, conv_b, wa, wat, ba, wx, wxt, bx, lam, *, tt):
    S = z_rx.shape[0]
    n_t = S // tt
    BW = RNN_BLOCK_W
    W = LRU_NB * BW
    t8 = tt // 8

    def body(x_ref, xp_ref, h_ref, hp_ref, dh_ref, _dz_ref, cw_ref, cb_ref, wa_ref, wat_ref, ba_ref, wx_ref, wxt_ref,
             bx_ref, lam_ref, dx_ref, dwa_ref, dwx_ref, dba_ref, dbx_ref, dlam_ref, dcw_ref, dcb_ref, nxt, a_c, g_c, a_s,
             b_s, c_s):
        t = pl.program_id(1)
        tile = n_t - 1 - t

        @pl.when(t == 0)
        def _():
            a_c[...] = jnp.zeros((8, W), F32)
            g_c[...] = jnp.zeros((8, W), F32)
            nxt[...] = jnp.zeros((8, W), F32)
            dwa_ref[...] = jnp.zeros_like(dwa_ref)
            dwx_ref[...] = jnp.zeros_like(dwx_ref)
            dba_ref[...] = jnp.zeros_like(dba_ref)
            dbx_ref[...] = jnp.zeros_like(dbx_ref)
            dlam_ref[...] = jnp.zeros_like(dlam_ref)
            dcw_ref[...] = jnp.zeros_like(dcw_ref)
            dcb_ref[...] = jnp.zeros_like(dcb_ref)

        has_prev = (tile > 0).astype(F32)
        x = x_ref[...]
        before = xp_ref[...] * has_prev
        xm1, xm2, xm3 = _rows_before(x, before, 1), _rows_before(x, before, 2), _rows_before(x, before, 3)
        cw = cw_ref[...]
        xa = cb_ref[...] + cw[3:4] * x + cw[2:3] * xm1 + cw[1:2] * xm2 + cw[0:1] * xm3
        lam = lam_ref[...]
        r, i, sp, a, mult = _lru_gates(xa, wa_ref, ba_ref, wx_ref, bx_ref, lam)
        gated = i * xa
        h_prev = _rows_before(h_ref[...].astype(F32), hp_ref[8:16, :].astype(F32) * has_prev, 1)
        g, g_first = _scan_up(_rows_after(a, a_c[...], 1), dh_ref[...], g_c[0:1, :], a_s, b_s, c_s)
        a_c[...] = jnp.broadcast_to(a[0:1, :], (8, W))
        g_c[...] = jnp.broadcast_to(g_first, (8, W))
        dlog_a = g * h_prev * a - g * gated * (a * a) / mult
        dgated = g * mult
        di = dgated * xa
        dxa = dgated * i
        dr = dlog_a * (-LRU_C * sp)
        dlam_ref[...] += jnp.sum(dlog_a * (-LRU_C * r), axis=0, keepdims=True) * (-_sigmoid(-lam))
        dpr = dr * r * (1.0 - r)
        dpi = di * i * (1.0 - i)
        xab, dprb, dpib = xa.astype(BF16), dpr.astype(BF16), dpi.astype(BF16)
        tn_dims = (((0,), (0,)), ((), ()))
        back = []
        for j in range(LRU_NB):
            sl = slice(j * BW, (j + 1) * BW)
            dwa_ref[j] += lax.dot_general(xab[:, sl], dprb[:, sl], tn_dims, preferred_element_type=F32)
            dwx_ref[j] += lax.dot_general(xab[:, sl], dpib[:, sl], tn_dims, preferred_element_type=F32)
            dba_ref[j] += jnp.sum(dpr[:, sl], axis=0, keepdims=True)
            dbx_ref[j] += jnp.sum(dpi[:, sl], axis=0, keepdims=True)
            back.append(jnp.dot(dprb[:, sl], wat_ref[j], preferred_element_type=F32)
                        + jnp.dot(dpib[:, sl], wxt_ref[j], preferred_element_type=F32))
        dxa = dxa + jnp.concatenate(back, axis=1)
        after = nxt[...]
        dx = (cw[3:4] * dxa + cw[2:3] * _rows_after(dxa, after, 1) + cw[1:2] * _rows_after(dxa, after, 2)
              + cw[0:1] * _rows_after(dxa, after, 3))
        nxt[...] = dxa[0:8, :]
        dx_ref[...] = dx.astype(BF16)
        dcw_ref[3:4, :] += jnp.sum(dxa * x, axis=0, keepdims=True)
        dcw_ref[2:3, :] += jnp.sum(dxa * xm1, axis=0, keepdims=True)
        dcw_ref[1:2, :] += jnp.sum(dxa * xm2, axis=0, keepdims=True)
        dcw_ref[0:1, :] += jnp.sum(dxa * xm3, axis=0, keepdims=True)
        dcb_ref[...] += jnp.sum(dxa, axis=0, keepdims=True)

    blk = lambda n, t: (n_t - 1 - t, n)
    prev = lambda n, t: (jnp.maximum((n_t - 1 - t) * t8 - 1, 0), n)
    vec = pl.BlockSpec((1, W), lambda n, t: (0, n))
    mat = pl.BlockSpec((LRU_NB, BW, BW), lambda n, t: (n, 0, 0))
    bias = pl.BlockSpec((LRU_NB, 1, BW), lambda n, t: (n, 0, 0))
    cws = pl.BlockSpec((CONV_WIDTH, W), lambda n, t: (0, n))
    tile = pl.BlockSpec((tt, W), blk)
    prev8 = pl.BlockSpec((8, W), prev)
    prev16 = pl.BlockSpec((16, W), lambda n, t: (jnp.maximum((n_t - 1 - t) * (tt // 16) - 1, 0), n))
    row8 = pltpu.VMEM((8, W), F32)
    wide = pltpu.VMEM((LRU_NB, tt, LANES), F32)
    return pl.pallas_call(
        body, name="lru_bwd", grid=(RNN_BLOCKS // LRU_NB, n_t),
        in_specs=[tile, prev8, tile, prev16, tile, pl.BlockSpec(memory_space=pl.ANY), cws, vec, mat, mat, bias, mat, mat,
                  bias, vec],
        out_specs=[tile, mat, mat, bias, bias, vec, cws, vec], input_output_aliases={5: 0},
        out_shape=[jax.ShapeDtypeStruct(dz.shape, BF16),
                   jax.ShapeDtypeStruct((RNN_BLOCKS, BW, BW), F32), jax.ShapeDtypeStruct((RNN_BLOCKS, BW, BW), F32),
                   jax.ShapeDtypeStruct((RNN_BLOCKS, 1, BW), F32), jax.ShapeDtypeStruct((RNN_BLOCKS, 1, BW), F32),
                   jax.ShapeDtypeStruct((1, D_MODEL), F32),
                   jax.ShapeDtypeStruct((CONV_WIDTH, D_MODEL), F32), jax.ShapeDtypeStruct((1, D_MODEL), F32)],
        scratch_shapes=[row8, row8, row8, wide, wide, wide],
        compiler_params=_params(("parallel", "arbitrary")),
    )(z_rx, z_rx, h, h, dh, dz, conv_w, conv_b, wa, wat, ba, wx, wxt, bx, lam)


def _mla_proj(z_ckv, q_norm, kv_norm, w_uq, w_ukv, cos, sin, *, ts):
    S = z_ckv.shape[0]
    H = MLA_HEADS

    def body(c_ref, qn_ref, kn_ref, wq_ref, wkv_ref, cos_ref, sin_ref, q_ref, k_ref, v_ref):
        c = c_ref[...]
        cqn, _ = _rms_fwd(c[:, 0:Q_LORA], qn_ref[...])
        ckn, _ = _rms_fwd(c[:, Q_LORA:Q_LORA + KV_LORA], kn_ref[...])
        q = jnp.dot(cqn.astype(BF16), wq_ref[...], preferred_element_type=F32) * (ATTN_SCALE * LOG2E)
        kv = jnp.dot(ckn.astype(BF16), wkv_ref[...], preferred_element_type=F32)
        cos1, sin1 = cos_ref[...], sin_ref[...]
        cos8 = jnp.concatenate([cos1] * H, axis=1)
        sin8 = jnp.concatenate([sin1] * H, axis=1)
        qr = q[:, H * QK_NOPE:]
        lane8 = lax.broadcasted_iota(jnp.int32, qr.shape, 1)
        qr = qr * cos8 + _rot_half(qr, lane8) * sin8
        kr = c[:, Q_LORA + KV_LORA:]
        lane1 = lax.broadcasted_iota(jnp.int32, kr.shape, 1)
        kr = (kr * cos1 + _rot_half(kr, lane1) * sin1).astype(BF16)
        for h in range(H):
            q_ref[h, :, 0:QK_NOPE] = q[:, h * QK_NOPE:(h + 1) * QK_NOPE].astype(BF16)
            q_ref[h, :, QK_NOPE:] = qr[:, h * LANES:(h + 1) * LANES].astype(BF16)
            k_ref[h, :, 0:QK_NOPE] = kv[:, h * 2 * LANES:h * 2 * LANES + LANES].astype(BF16)
            k_ref[h, :, QK_NOPE:] = kr
            v_ref[h] = kv[:, h * 2 * LANES + LANES:(h + 1) * 2 * LANES].astype(BF16)

    full = lambda shape: pl.BlockSpec(shape, lambda i: (0,) * len(shape))
    return pl.pallas_call(
        body, name="mla_proj", grid=(S // ts,),
        in_specs=[pl.BlockSpec((ts, CKV_W), lambda i: (i, 0)), full((1, Q_LORA)), full((1, KV_LORA)),
                  full(w_uq.shape), full(w_ukv.shape), pl.BlockSpec((ts, LANES), lambda i: (i, 0)),
                  pl.BlockSpec((ts, LANES), lambda i: (i, 0))],
        out_specs=[pl.BlockSpec((H, ts, QK_PAD), lambda i: (0, i, 0)), pl.BlockSpec((H, ts, QK_PAD), lambda i: (0, i, 0)),
                   pl.BlockSpec((H, ts, V_HEAD), lambda i: (0, i, 0))],
        out_shape=[jax.ShapeDtypeStruct((H, S, QK_PAD), BF16), jax.ShapeDtypeStruct((H, S, QK_PAD), BF16),
                   jax.ShapeDtypeStruct((H, S, V_HEAD), BF16)],
        compiler_params=_params(("parallel",)),
    )(z_ckv, q_norm, kv_norm, w_uq, w_ukv, cos, sin)


def _mla_proj_bwd(z_ckv, dq, dk, dv, q_norm, kv_norm, w_uqt, w_ukvt, cos, sin, *, ts):
    S = z_ckv.shape[0]
    H = MLA_HEADS

    n = S // ts
    NBUF = 3

    def body(c_ref, dq_any, dk_any, dv_any, qn_ref, kn_ref, wqt_ref, wkvt_ref, cos_ref, sin_ref,
             dz_ref, dwq_ref, dwkv_ref, dqn_ref, dkn_ref, dq_buf, dk_buf, dv_buf, sems):
        i = pl.program_id(0)

        def copies(tile, slot):
            rows = pl.ds(pl.multiple_of(tile * ts, ts), ts)
            return [pltpu.make_async_copy(src.at[:, rows, :], buf.at[slot], sems.at[k, slot])
                    for k, (src, buf) in enumerate(((dq_any, dq_buf), (dk_any, dk_buf), (dv_any, dv_buf)))]

        @pl.when(i == 0)
        def _():
            for t0 in range(min(NBUF - 1, n)):
                for cp in copies(t0, t0):
                    cp.start()

        @pl.when(i + NBUF - 1 < n)
        def _():
            for cp in copies(i + NBUF - 1, (i + NBUF - 1) % NBUF):
                cp.start()

        slot = i % NBUF
        for cp in copies(i, slot):
            cp.wait()
        dq_ref, dk_ref, dv_ref = dq_buf.at[slot], dk_buf.at[slot], dv_buf.at[slot]

        @pl.when(i == 0)
        def _():
            dwq_ref[...] = jnp.zeros_like(dwq_ref)
            dwkv_ref[...] = jnp.zeros_like(dwkv_ref)
            dqn_ref[...] = jnp.zeros_like(dqn_ref)
            dkn_ref[...] = jnp.zeros_like(dkn_ref)

        c = c_ref[...]
        cq, ck = c[:, 0:Q_LORA], c[:, Q_LORA:Q_LORA + KV_LORA]
        qn, kn = qn_ref[...], kn_ref[...]
        cqn, _ = _rms_fwd(cq, qn)
        ckn, _ = _rms_fwd(ck, kn)
        cos1, sin1 = cos_ref[...], sin_ref[...]
        lane1 = lax.broadcasted_iota(jnp.int32, cos1.shape, 1)

        def unrope(g):
            return g * cos1 - _rot_half(g * sin1, lane1)

        dq_all = jnp.concatenate([dq_ref[h, :, 0:QK_NOPE] for h in range(H)]
                                 + [unrope(dq_ref[h, :, QK_NOPE:]) for h in range(H)], axis=1)
        dq_all = (dq_all * ATTN_SCALE).astype(BF16)
        dkv_all = jnp.concatenate([p for h in range(H) for p in (dk_ref[h, :, 0:QK_NOPE], dv_ref[h])],
                                  axis=1).astype(BF16)
        dkr = dk_ref[0, :, QK_NOPE:].astype(F32)
        for h in range(1, H):
            dkr = dkr + dk_ref[h, :, QK_NOPE:].astype(F32)
        dkr = unrope(dkr)
        tn_dims = (((0,), (0,)), ((), ()))
        dwq_ref[...] += lax.dot_general(cqn.astype(BF16), dq_all, tn_dims, preferred_element_type=F32)
        dwkv_ref[...] += lax.dot_general(ckn.astype(BF16), dkv_all, tn_dims, preferred_element_type=F32)
        dcqn = jnp.dot(dq_all, wqt_ref[...], preferred_element_type=F32)
        dckn = jnp.dot(dkv_all, wkvt_ref[...], preferred_element_type=F32)
        dcq, dqn_rows = _rms_bwd(dcqn, cq, qn)
        dck, dkn_rows = _rms_bwd(dckn, ck, kn)
        dqn_ref[...] += jnp.sum(dqn_rows, axis=0, keepdims=True)
        dkn_ref[...] += jnp.sum(dkn_rows, axis=0, keepdims=True)
        dz_ref[:, 0:Q_LORA] = dcq.astype(BF16)
        dz_ref[:, Q_LORA:Q_LORA + KV_LORA] = dck.astype(BF16)
        dz_ref[:, Q_LORA + KV_LORA:] = dkr.astype(BF16)

    full = lambda shape: pl.BlockSpec(shape, lambda i: (0,) * len(shape))
    return pl.pallas_call(
        body, name="mla_proj_bwd", grid=(S // ts,),
        in_specs=[pl.BlockSpec((ts, CKV_W), lambda i: (i, 0)), pl.BlockSpec(memory_space=pl.ANY),
                  pl.BlockSpec(memory_space=pl.ANY), pl.BlockSpec(memory_space=pl.ANY),
                  full((1, Q_LORA)), full((1, KV_LORA)), full(w_uqt.shape), full(w_ukvt.shape),
                  pl.BlockSpec((ts, LANES), lambda i: (i, 0)), pl.BlockSpec((ts, LANES), lambda i: (i, 0))],
        out_specs=[pl.BlockSpec((ts, CKV_W), lambda i: (i, 0)), full((Q_LORA, w_uqt.shape[0])),
                   full((KV_LORA, w_ukvt.shape[0])), full((1, Q_LORA)), full((1, KV_LORA))],
        out_shape=[jax.ShapeDtypeStruct((S, CKV_W), BF16), jax.ShapeDtypeStruct((Q_LORA, w_uqt.shape[0]), F32),
                   jax.ShapeDtypeStruct((KV_LORA, w_ukvt.shape[0]), F32), jax.ShapeDtypeStruct((1, Q_LORA), F32),
                   jax.ShapeDtypeStruct((1, KV_LORA), F32)],
        scratch_shapes=[pltpu.VMEM((NBUF, H, ts, QK_PAD), BF16), pltpu.VMEM((NBUF, H, ts, QK_PAD), BF16),
                        pltpu.VMEM((NBUF, H, ts, V_HEAD), BF16), pltpu.SemaphoreType.DMA((3, NBUF))],
        compiler_params=_params(("arbitrary",)),
    )(z_ckv, dq, dk, dv, q_norm, kv_norm, w_uqt, w_ukvt, cos, sin)


NT_DIMS = (((1,), (1,)), ((), ()))
TN_DIMS = (((0,), (0,)), ((), ()))


def _attn_fwd(q, k, v, *, t, hb):
    H, S, _ = q.shape
    n = S // t
    pairs = [(i, j) for i in range(n) for j in range(i + 1)]
    qi = jnp.asarray(np.array([p[0] for p in pairs], np.int32))
    ki = jnp.asarray(np.array([p[1] for p in pairs], np.int32))

    def body(qi_ref, ki_ref, q_ref, k_ref, v_ref, o_ref, lse_ref, m_s, l_s, acc_s):
        p = pl.program_id(1)
        i, j = qi_ref[p], ki_ref[p]

        @pl.when(j == 0)
        def _():
            m_s[...] = jnp.full(m_s.shape, NEG, F32)
            l_s[...] = jnp.zeros(l_s.shape, F32)
            acc_s[...] = jnp.zeros(acc_s.shape, F32)

        def block(hh, r0, nr, nk, masked):
            rows = slice(r0, r0 + nr)
            s = lax.dot_general(q_ref[hh, rows, :], k_ref[hh, 0:nk, :], NT_DIMS, preferred_element_type=F32)
            if masked:
                row = lax.broadcasted_iota(jnp.int32, (nr, nk), 0) + r0
                col = lax.broadcasted_iota(jnp.int32, (nr, nk), 1)
                s = jnp.where(row >= col, s, NEG)
            chunks = nk // LANES
            mc = s[:, 0:LANES]
            for c in range(1, chunks):
                mc = jnp.maximum(mc, s[:, c * LANES:(c + 1) * LANES])
            m_prev = m_s[hh, rows, :]
            m_new = jnp.maximum(m_prev, jnp.max(mc, axis=1, keepdims=True))
            alpha = jnp.exp2(m_prev - m_new)
            pr = jnp.exp2(s - jnp.concatenate([m_new] * chunks, axis=1))
            ls = pr[:, 0:LANES]
            for c in range(1, chunks):
                ls = ls + pr[:, c * LANES:(c + 1) * LANES]
            l_s[hh, rows, :] = alpha * l_s[hh, rows, :] + ls
            acc_s[hh, rows, :] = alpha * acc_s[hh, rows, :] + jnp.dot(pr.astype(BF16), v_ref[hh, 0:nk, :],
                                                                      preferred_element_type=F32)
            m_s[hh, rows, :] = m_new

        def step(diagonal):
            for hh in range(hb):
                if diagonal:
                    block(hh, 0, t // 2, t // 2, True)
                    block(hh, t // 2, t // 2, t, True)
                else:
                    block(hh, 0, t, t, False)

        @pl.when(j < i)
        def _():
            step(False)

        @pl.when(j == i)
        def _():
            step(True)
            for hh in range(hb):
                l = jnp.sum(l_s[hh], axis=1, keepdims=True)
                o_ref[:, hh * V_HEAD:(hh + 1) * V_HEAD] = acc_s[hh] / l
                lse_ref[hh] = (m_s[hh] + jnp.log2(l)).T[0:1, :]

    grid_spec = pltpu.PrefetchScalarGridSpec(
        num_scalar_prefetch=2, grid=(H // hb, len(pairs)),
        in_specs=[pl.BlockSpec((hb, t, QK_PAD), lambda h, p, qi, ki: (h, qi[p], 0)),
                  pl.BlockSpec((hb, t, QK_PAD), lambda h, p, qi, ki: (h, ki[p], 0)),
                  pl.BlockSpec((hb, t, V_HEAD), lambda h, p, qi, ki: (h, ki[p], 0))],
        out_specs=[pl.BlockSpec((t, hb * V_HEAD), lambda h, p, qi, ki: (qi[p], h)),
                   pl.BlockSpec((hb, 1, t), lambda h, p, qi, ki: (h, 0, qi[p]))],
        scratch_shapes=[pltpu.VMEM((hb, t, LANES), F32), pltpu.VMEM((hb, t, LANES), F32),
                        pltpu.VMEM((hb, t, V_HEAD), F32)],
    )
    return pl.pallas_call(
        body, name="attn_fwd", grid_spec=grid_spec,
        out_shape=[jax.ShapeDtypeStruct((S, H * V_HEAD), F32), jax.ShapeDtypeStruct((H, 1, S), F32)],
        compiler_params=_params(("parallel", "arbitrary")),
    )(qi, ki, q, k, v)


def _attn_bwd(q, k, v, do, lse_row, delta_row, *, t):
    H, S, _ = q.shape
    n = S // t
    pairs = [(i, j) for j in range(n) for i in range(j, n)]
    qi = jnp.asarray(np.array([p[0] for p in pairs], np.int32))
    ki = jnp.asarray(np.array([p[1] for p in pairs], np.int32))

    def body(qi_ref, ki_ref, q_ref, k_ref, v_ref, do_ref, lse_ref, dl_ref, dq_ref, dk_ref, dv_ref, dk_s, dv_s, dq_s):
        p = pl.program_id(1)
        i, j = qi_ref[p], ki_ref[p]

        @pl.when(p == 0)
        def _():
            dq_s[...] = jnp.zeros_like(dq_s)

        def block(k0, nk, q0, nq, masked):
            qb, dob = q_ref[0, q0:q0 + nq, :], do_ref[q0:q0 + nq, :]
            kb, vb = k_ref[0, k0:k0 + nk, :], v_ref[0, k0:k0 + nk, :]
            st = lax.dot_general(kb, qb, NT_DIMS, preferred_element_type=F32)
            if masked:
                krow = lax.broadcasted_iota(jnp.int32, (nk, nq), 0) + k0
                qcol = lax.broadcasted_iota(jnp.int32, (nk, nq), 1) + q0
                st = jnp.where(krow <= qcol, st, NEG)
            pt = jnp.exp2(st - lse_ref[0][:, q0:q0 + nq])
            dvp = jnp.dot(pt.astype(BF16), dob, preferred_element_type=F32)
            dpt = lax.dot_general(vb, dob, NT_DIMS, preferred_element_type=F32)
            dst = (pt * (dpt - dl_ref[0][:, q0:q0 + nq])).astype(BF16)
            dkp = jnp.dot(dst, qb, preferred_element_type=F32)
            rows = pl.ds(pl.multiple_of(i * t + q0, LANES), nq)
            dq_s[rows, :] += lax.dot_general(dst, kb, TN_DIMS, preferred_element_type=F32)
            return dkp, dvp

        @pl.when(i == j)
        def _():
            half = t // 2
            dk_s[0:half, :], dv_s[0:half, :] = block(0, half, 0, t, True)
            dk_s[half:t, :], dv_s[half:t, :] = block(half, half, half, half, True)

        @pl.when(i != j)
        def _():
            dkp, dvp = block(0, t, 0, t, False)
            dk_s[...] += dkp
            dv_s[...] += dvp

        @pl.when(i == n - 1)
        def _():
            dk_ref[0] = (dk_s[...] * LN2).astype(BF16)
            dv_ref[0] = dv_s[...].astype(BF16)

        @pl.when(p == len(pairs) - 1)
        def _():
            dq_ref[0] = dq_s[...].astype(BF16)

    grid_spec = pltpu.PrefetchScalarGridSpec(
        num_scalar_prefetch=2, grid=(H, len(pairs)),
        in_specs=[pl.BlockSpec((1, t, QK_PAD), lambda h, p, qi, ki: (h, qi[p], 0)),
                  pl.BlockSpec((1, t, QK_PAD), lambda h, p, qi, ki: (h, ki[p], 0)),
                  pl.BlockSpec((1, t, V_HEAD), lambda h, p, qi, ki: (h, ki[p], 0)),
                  pl.BlockSpec((t, V_HEAD), lambda h, p, qi, ki: (qi[p], h)),
                  pl.BlockSpec((1, 1, t), lambda h, p, qi, ki: (h, 0, qi[p])),
                  pl.BlockSpec((1, 1, t), lambda h, p, qi, ki: (h, 0, qi[p]))],
        out_specs=[pl.BlockSpec((1, S, QK_PAD), lambda h, p, qi, ki: (h, 0, 0)),
                   pl.BlockSpec((1, t, QK_PAD), lambda h, p, qi, ki: (h, ki[p], 0)),
                   pl.BlockSpec((1, t, V_HEAD), lambda h, p, qi, ki: (h, ki[p], 0))],
        scratch_shapes=[pltpu.VMEM((t, QK_PAD), F32), pltpu.VMEM((t, V_HEAD), F32), pltpu.VMEM((S, QK_PAD), F32)],
    )
    return pl.pallas_call(
        body, name="attn_bwd", grid_spec=grid_spec,
        out_shape=[jax.ShapeDtypeStruct((H, S, QK_PAD), BF16), jax.ShapeDtypeStruct((H, S, QK_PAD), BF16),
                   jax.ShapeDtypeStruct((H, S, V_HEAD), BF16)],
        compiler_params=_params(("parallel", "arbitrary")),
    )(qi, ki, q, k, v, do, lse_row, delta_row)


def _merge_h1(h, z_gates, o, x, w_out, norm_mlp, *, ts):
    S = h.shape[0]
    D = D_MODEL
    n = S // ts
    NBUF = 3

    def body(h_any, g_any, o_any, x_any, w_ref, nrm_ref, m_ref, h1_ref, n2_ref, hbuf, gbuf, obuf, xbuf, sems):
        i = pl.program_id(0)

        def copies(tile, slot):
            rows = pl.ds(pl.multiple_of(tile * ts, ts), ts)
            return [pltpu.make_async_copy(src.at[rows, :], buf.at[slot], sems.at[s, slot])
                    for s, (src, buf) in enumerate(((h_any, hbuf), (g_any, gbuf), (o_any, obuf), (x_any, xbuf)))]

        @pl.when(i == 0)
        def _():
            for t0 in range(min(NBUF - 1, n)):
                for cp in copies(t0, t0):
                    cp.start()

        @pl.when(i + NBUF - 1 < n)
        def _():
            for cp in copies(i + NBUF - 1, (i + NBUF - 1) % NBUF):
                cp.start()

        slot = i % NBUF
        for cp in copies(i, slot):
            cp.wait()
        gates = gbuf[slot].astype(F32)
        gl, _ = _gelu_and_grad(gates[:, 0:D])
        m = (_sigmoid(gates[:, D:2 * D]) * (hbuf[slot].astype(F32) * gl)
             + _sigmoid(gates[:, 2 * D:3 * D]) * obuf[slot]).astype(BF16)
        m_ref[...] = m
        h1 = xbuf[slot] + jnp.dot(m, w_ref[...], preferred_element_type=F32)
        h1_ref[...] = h1
        n2, _ = _rms_fwd(h1, nrm_ref[...])
        n2_ref[...] = n2.astype(BF16)

    col = pl.BlockSpec((ts, D), lambda i: (i, 0))
    hbm = pl.BlockSpec(memory_space=pl.ANY)
    fixed = lambda shape: pl.BlockSpec(shape, lambda i: (0, 0), pipeline_mode=pl.Buffered(1))
    return pl.pallas_call(
        body, name="merge_h1", grid=(n,),
        in_specs=[hbm, hbm, hbm, hbm, fixed((D, D)), fixed((1, D))],
        out_specs=[col, col, col],
        out_shape=[jax.ShapeDtypeStruct((S, D), BF16), jax.ShapeDtypeStruct((S, D), F32),
                   jax.ShapeDtypeStruct((S, D), BF16)],
        scratch_shapes=[pltpu.VMEM((NBUF, ts, D), BF16), pltpu.VMEM((NBUF, ts, 3 * D), BF16),
                        pltpu.VMEM((NBUF, ts, D), F32), pltpu.VMEM((NBUF, ts, D), F32),
                        pltpu.SemaphoreType.DMA((4, NBUF))],
        compiler_params=_params(("arbitrary",)),
    )(h, z_gates, o, x, w_out, norm_mlp)


def _my_place():
    return lax.axis_index("x"), lax.axis_index("y"), lax.axis_index("c")


def _all_gather(shards, *, name):
    n = len(shards)

    def body(*refs):
        x_refs, out_refs = refs[:n], refs[n:2 * n]
        send_sems, recv_sems, local_sems = refs[2 * n:]
        x, y, c = _my_place()
        me, sibling = (x, y, c), (x, y, 1 - c)
        chips = [(1 - x, y), (x, 1 - y), (1 - x, 1 - y)]

        def slot(a, px, py, pc):
            return out_refs[a].at[4 * px + 2 * py + pc]

        def copy(a, k, block, to, src=None):
            return pltpu.make_async_remote_copy(
                src_ref=slot(a, *block) if src is None else src, dst_ref=slot(a, *block),
                send_sem=send_sems.at[7 * a + k], recv_sem=recv_sems.at[7 * a + k], device_id=to, device_id_type=MESH)

        mine = [pltpu.make_async_copy(x_refs[a], slot(a, *me), local_sems.at[a]) for a in range(n)]
        for cp in mine:
            cp.start()
        first = []
        for a in range(n):
            first.append(copy(a, 0, me, sibling, src=x_refs[a]))
            first += [copy(a, 1 + j, me, (*chip, c), src=x_refs[a]) for j, chip in enumerate(chips)]
        for cp in first:
            cp.start()
        passed = []
        for a in range(n):
            for j, chip in enumerate(chips):
                copy(a, 1 + j, (*chip, c), me).wait_recv()
                fwd = copy(a, 4 + j, (*chip, c), sibling)
                fwd.start()
                passed.append(fwd)
        for a in range(n):
            copy(a, 0, sibling, me).wait_recv()
            for j, chip in enumerate(chips):
                copy(a, 4 + j, (*chip, 1 - c), me).wait_recv()
        for cp in first + passed:
            cp.wait_send()
        for cp in mine:
            cp.wait()

    hbm = pl.BlockSpec(memory_space=pl.ANY)
    return pl.pallas_call(
        body, name=name, out_shape=[jax.ShapeDtypeStruct((N_DEV, *s.shape), s.dtype) for s in shards],
        in_specs=[hbm] * n, out_specs=[hbm] * n,
        scratch_shapes=[pltpu.SemaphoreType.DMA((7 * n,)), pltpu.SemaphoreType.DMA((7 * n,)),
                        pltpu.SemaphoreType.DMA((n,))],
    )(*shards)


def _pushes(src_refs, land_refs, send_sems, recv_sems, slab_per_peer):
    x, y, c = _my_place()
    me = 4 * x + 2 * y + c
    copies = []
    for a in range(len(src_refs)):
        for k in range(1, N_DEV):
            px, py, pc = x ^ (k >> 2), y ^ ((k >> 1) & 1), c ^ (k & 1)
            src = src_refs[a].at[4 * px + 2 * py + pc] if slab_per_peer else src_refs[a]
            copies.append(pltpu.make_async_remote_copy(
                src_ref=src, dst_ref=land_refs[a].at[me], send_sem=send_sems.at[7 * a + k - 1],
                recv_sem=recv_sems.at[7 * a + k - 1], device_id=(px, py, pc), device_id_type=MESH))
    return copies


def _push_start(srcs, *, name, slab_per_peer):
    n = len(srcs)
    lands = [lax.empty((N_DEV, *(s.shape[1:] if slab_per_peer else s.shape)), s.dtype) for s in srcs]

    def body(*refs):
        src_refs, land_refs = refs[:n], refs[n:2 * n]
        send_sems, recv_sems, token = refs[2 * n], refs[2 * n + 1], refs[-1]
        for cp in _pushes(src_refs, land_refs, send_sems, recv_sems, slab_per_peer):
            cp.start()
        token[...] = jnp.zeros_like(token)

    hbm = pl.BlockSpec(memory_space=pltpu.HBM)
    sem = pl.BlockSpec(memory_space=pltpu.SEMAPHORE)
    out = pl.pallas_call(
        body, name=name,
        out_shape=(pltpu.SemaphoreType.DMA((7 * n,)), pltpu.SemaphoreType.DMA((7 * n,)),
                   *[pltpu.HBM(a.shape, a.dtype) for a in srcs + lands], jax.ShapeDtypeStruct((8, LANES), F32)),
        in_specs=[hbm] * (2 * n), out_specs=(sem, sem, *[hbm] * (2 * n), pl.BlockSpec(memory_space=pltpu.VMEM)),
        input_output_aliases={i: 2 + i for i in range(2 * n)},
        compiler_params=pltpu.CompilerParams(has_side_effects=pltpu.SideEffectType.DATAFLOW_SIDE_EFFECTING),
    )(*[pltpu.with_memory_space_constraint(a, pltpu.HBM) for a in srcs + lands])
    return out[0], out[1], list(out[2:2 + n]), list(out[2 + n:2 + 2 * n]), out[-1]


def _push_wait(send_sems, recv_sems, srcs, lands, after, *, name, slab_per_peer):
    n = len(srcs)

    def body(*refs):
        src_refs, land_refs = refs[:n], refs[n:2 * n]
        s_sems, r_sems = refs[2 * n], refs[2 * n + 1]
        for cp in _pushes(src_refs, land_refs, s_sems, r_sems, slab_per_peer):
            cp.wait_send()
            cp.wait_recv()

    hbm = pl.BlockSpec(memory_space=pltpu.HBM)
    sem = pl.BlockSpec(memory_space=pltpu.SEMAPHORE)
    out = pl.pallas_call(
        body, name=name, out_shape=tuple(pltpu.HBM(a.shape, a.dtype) for a in srcs + lands),
        in_specs=[hbm] * (2 * n) + [sem, sem, pl.BlockSpec(memory_space=pl.ANY)], out_specs=tuple([hbm] * (2 * n)),
        input_output_aliases={i: i for i in range(2 * n)},
        compiler_params=pltpu.CompilerParams(has_side_effects=pltpu.SideEffectType.DATAFLOW_SIDE_EFFECTING),
    )(*srcs, *lands, send_sems, recv_sems, after)
    return list(out[:n]), list(out[n:])


def _sum_parts(gp_ref, rows):
    g = gp_ref[0, 0:rows, :].astype(F32)
    for p in range(1, gp_ref.shape[0]):
        g = g + gp_ref[p, 0:rows, :].astype(F32)
    return g


def _adamw_update(w, m, v, g):
    m_new = ADAM_B1 * m + (1.0 - ADAM_B1) * g
    v_new = ADAM_B2 * v + (1.0 - ADAM_B2) * (g * g)
    m_hat = m_new / (1.0 - ADAM_B1 ** ADAM_STEP)
    v_hat = v_new / (1.0 - ADAM_B2 ** ADAM_STEP)
    return -ADAM_LR * (m_hat / (jnp.sqrt(v_hat) + ADAM_EPS) + ADAM_WD * w), m_new, v_new


def _adamw_many(ws, ms, vs, gparts, sums, *, name):
    n, k = len(ws), len(sums)

    def body(*refs):
        w_refs, m_refs, v_refs = refs[:n], refs[n:2 * n], refs[2 * n:3 * n]
        g_refs, s_refs, outs = refs[3 * n:4 * n], refs[4 * n:4 * n + k], refs[4 * n + k:]
        for a in range(n):
            g = _sum_parts(g_refs[a], w_refs[a].shape[0])
            d, m_new, v_new = _adamw_update(w_refs[a][...], m_refs[a][...], v_refs[a][...], g)
            for o_ref, val in zip(outs[4 * a:4 * a + 4], (g, d, m_new, v_new)):
                o_ref[...] = val
        for b in range(k):
            outs[4 * n + b][...] = _sum_parts(s_refs[b], s_refs[b].shape[1])

    out_shape = [jax.ShapeDtypeStruct(w.shape, F32) for w in ws for _ in range(4)]
    out_shape += [jax.ShapeDtypeStruct(s.shape[1:], F32) for s in sums]
    return pl.pallas_call(body, name=name, out_shape=out_shape, compiler_params=_params())(
        *ws, *ms, *vs, *gparts, *sums)


def _adamw(w, m, v, gparts, *, tr, name):
    R, C = w.shape
    n_parts = gparts.shape[0]

    def body(w_ref, m_ref, v_ref, gp_ref, g_ref, d_ref, nm_ref, nv_ref):
        g = _sum_parts(gp_ref, tr)
        d_ref[...], nm_ref[...], nv_ref[...] = _adamw_update(w_ref[...], m_ref[...], v_ref[...], g)
        g_ref[...] = g

    row = pl.BlockSpec((tr, C), lambda i: (i, 0))
    shp = jax.ShapeDtypeStruct((R, C), F32)
    return pl.pallas_call(
        body, name=name, grid=(R // tr,),
        in_specs=[row, row, row, pl.BlockSpec((n_parts, tr, C), lambda i: (0, i, 0))],
        out_specs=[row, row, row, row], out_shape=[shp, shp, shp, shp],
        compiler_params=_params(("parallel",)),
    )(w, m, v, gparts)


def _rope_tables(s):
    pos = jnp.arange(s, dtype=F32)
    inv_freq = 1.0 / (ROPE_THETA ** (jnp.arange(0, QK_ROPE, 2, dtype=F32) / QK_ROPE))
    per_lane = jnp.concatenate([inv_freq, inv_freq, jnp.zeros((LANES - QK_ROPE,), F32)])
    ang = pos[:, None] * per_lane[None, :]
    live = jnp.arange(LANES) < QK_ROPE
    return jnp.where(live, jnp.cos(ang), 0.0), jnp.where(live, jnp.sin(ang), 0.0)


def _pick(n, want):
    t = min(n, want)
    assert n % t == 0
    return t


def _local_step(x, target, wts, small, hooks):
    S = x.shape[0]
    H = MLA_HEADS
    ts = _pick(S, 1024)
    tm = _pick(S, 512)
    tm_wide = _pick(S, 1024)
    tk_s = _pick(S, 4096)
    tt = _pick(S, 512)
    ta = _pick(S, 1024)
    ts_proj = _pick(S, 512)
    ts_merge = _pick(S, 512)
    row = lambda v: v.reshape(1, -1)
    w_in = wts["w_in"]
    w_main = jnp.concatenate([w_in[:, 0:2048], w_in[:, 2624:4672]], axis=1)
    w_ckv = jnp.concatenate([w_in[:, 2048:2624], jnp.zeros((D_MODEL, CKV_W - 576), BF16)], axis=1)
    w_uq3 = wts["w_uq"].reshape(Q_LORA, H, QK_NOPE + QK_ROPE)
    w_uq_p = jnp.concatenate(
        [w_uq3[:, :, :QK_NOPE].reshape(Q_LORA, H * QK_NOPE),
         jnp.pad(w_uq3[:, :, QK_NOPE:], ((0, 0), (0, 0), (0, LANES - QK_ROPE))).reshape(Q_LORA, H * LANES)], axis=1)
    w_ukv = wts["w_ukv"]
    cos, sin = _rope_tables(S)
    conv_w, conv_b = small["conv_w"], row(small["conv_b"])
    wa, wx = small["lru_wa"].astype(BF16), small["lru_wx"].astype(BF16)
    wat, wxt = jnp.swapaxes(wa, 1, 2), jnp.swapaxes(wx, 1, 2)
    ba, bx = small["lru_ba"].reshape(RNN_BLOCKS, 1, RNN_BLOCK_W), small["lru_bx"].reshape(RNN_BLOCKS, 1, RNN_BLOCK_W)
    lam = row(small["lru_lambda"])
    q_norm, kv_norm = row(small["q_norm"]), row(small["kv_norm"])
    norm_mix, norm_mlp, norm_final = row(small["norm_mix"]), row(small["norm_mlp"]), row(small["norm_final"])

    xn = _rmsnorm_cast(x, norm_mix, ts=ts, name="norm_mix")
    ident = lambda acc: (acc,)
    z_rx, z_ckv = _mm(xn, w_main[:, :D_MODEL], name="z_rx_ckv", tm=tm_wide, tn=1024, tk=1024, outs=[("tile", F32)],
                      epilogue=ident, also=w_ckv)
    (z_gates,) = _mm(xn, w_main[:, D_MODEL:], name="z_gates", tm=tm_wide, tn=3 * D_MODEL, tk=1024, outs=[("tile", BF16)],
                     epilogue=ident)
    h = _lru_fwd(z_rx, conv_w, conv_b, wa, ba, wx, bx, lam, tt=tt)
    q, k, v = _mla_proj(z_ckv, q_norm, kv_norm, w_uq_p, w_ukv, cos, sin, ts=ts_proj)
    o, lse = _attn_fwd(q, k, v, t=ta, hb=4)
    w_out, w_up, w_down = hooks["weights_later"](o)
    merged, h1, n2 = _merge_h1(h, z_gates, o, x, w_out, norm_mlp, ts=ts_merge)

    def ep_up(acc):
        r = jnp.maximum(acc, 0.0)
        return r * r, r

    act, relu = _mm(n2, w_up, name="up", tm=tm_wide, tn=2048, tk=1024, outs=[("tile", BF16), ("tile", BF16)],
                    epilogue=ep_up)

    def ep_loss(acc, h1v, tgt, g):
        h2 = acc + h1v
        y, _ = _rms_fwd(h2, g)
        err = y - tgt
        loss_rows = 0.5 * jnp.mean(err * err, axis=-1, keepdims=True)
        dy = err * (1.0 / D_MODEL)
        dh2, dg_rows = _rms_bwd(dy, h2, g)
        lsum = jnp.sum(loss_rows, axis=0, keepdims=True)
        return dh2, dh2, jnp.sum(dg_rows, axis=0, keepdims=True), jnp.broadcast_to(lsum, (1, D_MODEL))

    dh2, dh2b, dnf_p, loss_p = _mm(
        act, w_down, name="down_loss", tm=tm, tn=1024, tk=D_FF,
        outs=[("tile", F32), ("tile", BF16), ("rowpart", F32), ("rowpart", F32)], epilogue=ep_loss,
        extras=[("tile", h1), ("tile", target), ("row", norm_final)])
    loss_part = jnp.sum(loss_p[:, 0, 0])
    d_norm_final = jnp.sum(dnf_p, axis=(0, 1))

    def ep_du(acc, r):
        return (acc * (2.0 * r.astype(F32)),)

    (du,) = _mm(dh2b, w_down, name="d_act", tb=True, tm=tm_wide, tn=2048, tk=1024, outs=[("tile", BF16)], epilogue=ep_du,
                extras=[("tile", relu)])

    def ep_dh1(acc, h1v, dh2v, g):
        dv, dg_rows = _rms_bwd(acc, h1v, g)
        dh1 = dh2v + dv
        return dh1, dh1, jnp.sum(dg_rows, axis=0, keepdims=True)

    dh1, dh1b, dnm_p = _mm(du, w_up, name="d_n2", tb=True, tm=tm, tn=1024, tk=D_FF,
                           outs=[("tile", F32), ("tile", BF16), ("rowpart", F32)], epilogue=ep_dh1,
                           extras=[("tile", h1), ("tile", dh2), ("row", norm_mlp)])
    d_norm_mlp = jnp.sum(dnm_p, axis=(0, 1))
    tn_mm = functools.partial(_mm, ta=True, tk=tk_s, outs=[("tile", BF16)], epilogue=ident)
    (d_w_down,) = tn_mm(act, dh2b, name="dw_down", tm=1024, tn=1024)
    (p_w_up,) = _mm(n2, du, name="dw_up", ta=True, tk=tk_s, tm=1024, tn=D_FF // N_DEV, outs=[("colshard", BF16)],
                    epilogue=ident)
    (d_w_out,) = tn_mm(merged, dh1b, name="dw_out", tm=1024, tn=1024)
    early = [d_w_out.reshape(N_DEV, -1, D_MODEL), p_w_up, d_w_down.reshape(N_DEV, -1, D_MODEL)]
    w_out = w_out + hooks["send"]("early", early)[0, 0].astype(BF16)


    def ep_dmerge(dm, hv, rg, ga, gb, ov):
        hv, rg, ga, gb = hv.astype(F32), rg.astype(F32), ga.astype(F32), gb.astype(F32)
        gl, dgl = _gelu_and_grad(rg)
        sa, sb = _sigmoid(ga), _sigmoid(gb)
        ya = hv * gl
        dya = dm * sa
        do = dm * sb
        dga = dm * ya * sa * (1.0 - sa)
        dgb = dm * ov * sb * (1.0 - sb)
        dh = dya * gl
        drg = dya * hv * dgl
        dov = do * ov
        lane = lax.broadcasted_iota(jnp.int32, (dm.shape[0], LANES), 1)
        delta = jnp.zeros((dm.shape[0], LANES), F32)
        for hh in range(H):
            dsum = jnp.sum(dov[:, hh * V_HEAD:(hh + 1) * V_HEAD], axis=1, keepdims=True)
            delta = jnp.where(lane == hh, dsum, delta)
        return dh, jnp.concatenate([drg, dga, dgb], axis=1), do, delta

    dh_lru, dz_part, do, delta_w = _mm(
        dh1b, w_out, name="d_merge", tb=True, tm=ts_merge, tn=1024, tk=1024,
        outs=[("tile", F32), ("cols", BF16, 4 * D_MODEL, D_MODEL), ("tile", BF16), ("side", F32)],
        epilogue=ep_dmerge,
        extras=[("tile", h), ("tilecol", z_gates, 0), ("tilecol", z_gates, 1), ("tilecol", z_gates, 2), ("tile", o)])
    delta_row = delta_w[:, :H].T.reshape(H, 1, S)
    lse_row = lse

    dq, dk, dv = _attn_bwd(q, k, v, do, lse_row, delta_row, t=ta)
    dz_ckv, d_w_uq_p, d_w_ukv, d_q_norm, d_kv_norm = _mla_proj_bwd(
        z_ckv, dq, dk, dv, q_norm, kv_norm, w_uq_p.T, w_ukv.T, cos, sin, ts=ts_proj)
    d_w_uq = jnp.concatenate(
        [d_w_uq_p[:, :H * QK_NOPE].reshape(Q_LORA, H, QK_NOPE),
         d_w_uq_p[:, H * QK_NOPE:].reshape(Q_LORA, H, LANES)[:, :, :QK_ROPE]], axis=2).reshape(Q_LORA, -1)

    dz_main, d_wa, d_wx, d_ba, d_bx, d_lam, d_conv_w, d_conv_b = _lru_bwd(
        z_rx, h, dh_lru, dz_part, conv_w, conv_b, wa, wat, ba, wx, wxt, bx, lam, tt=tt)

    (d_w_main,) = tn_mm(xn, dz_main, name="dw_main", tm=1024, tn=1024)
    (d_w_ckv,) = tn_mm(xn, dz_ckv, name="dw_ckv", tm=1024, tn=CKV_W)
    d_w_in = jnp.concatenate([d_w_main[:, 0:2048], d_w_ckv[:, 0:576], d_w_main[:, 2048:4096]], axis=1)

    def col_parts(full):
        r = full.shape[0]
        return jnp.transpose(full.astype(BF16).reshape(r, N_DEV, -1), (1, 0, 2))

    late = [col_parts(d_w_in), col_parts(d_w_uq), col_parts(d_w_ukv)]
    norm_mix = norm_mix + hooks["send"]("late", late)[0, 0]

    def ep_dx(acc, xv, dh1v, g):
        dv, dg_rows = _rms_bwd(acc, xv, g)
        return dh1v + dv, jnp.sum(dg_rows, axis=0, keepdims=True)

    grad_x, dnx_p = _mm(dz_main, w_main, name="dx", tb=True, tm=tm, tn=1024, tk=4 * D_MODEL,
                        outs=[("tile", F32), ("rowpart", F32)], epilogue=ep_dx, more=(dz_ckv, w_ckv),
                        extras=[("tile", x), ("tile", dh1), ("row", norm_mix)])
    d_norm_mix = jnp.sum(dnx_p, axis=(0, 1))
    sm = {"norm_mix": d_norm_mix, "conv_w": d_conv_w, "conv_b": d_conv_b.reshape(-1), "lru_wa": d_wa,
          "lru_ba": d_ba.reshape(RNN_BLOCKS, RNN_BLOCK_W), "lru_wx": d_wx, "lru_bx": d_bx.reshape(RNN_BLOCKS, RNN_BLOCK_W),
          "lru_lambda": d_lam.reshape(-1), "q_norm": d_q_norm.reshape(-1), "kv_norm": d_kv_norm.reshape(-1),
          "norm_mlp": d_norm_mlp, "norm_final": d_norm_final}
    return loss_part, grad_x, sm


SMALL = ("norm_mix", "conv_b", "lru_wa", "lru_ba", "lru_wx", "lru_bx", "lru_lambda", "q_norm", "kv_norm", "norm_mlp",
         "norm_final")
WEIGHTS = ("norm_mix", "w_in", "conv_w", "conv_b", "lru_wa", "lru_ba", "lru_wx", "lru_bx", "lru_lambda", "q_norm", "w_uq",
           "kv_norm", "w_ukv", "w_out", "norm_mlp", "w_up", "w_down", "norm_final")
ADAM_TILE_ROWS = {"w_in": 256, "w_uq": 128, "w_ukv": 128, "w_out": 64, "w_up": 256, "w_down": 128}
CONV_ROWS = N_DEV * 8


def _rows(a):
    return a.reshape(-1, LANES)


def _pad_rows(a, mult):
    r = a.shape[-2]
    pad = (-r) % mult
    if pad == 0:
        return a
    cfg = [(0, 0)] * (a.ndim - 2) + [(0, pad), (0, 0)]
    return jnp.pad(a, cfg)


def _cols_from_shards(g):
    return jnp.transpose(g, (1, 0, 2)).reshape(g.shape[1], -1)


def kernel(x, norm_mix, w_in, conv_w, conv_b, lru_wa, lru_ba, lru_wx, lru_bx, lru_lambda, q_norm, w_uq, kv_norm, w_ukv, w_out, norm_mlp, w_up, w_down, norm_final, loss_target, m_norm_mix, m_w_in, m_conv_w, m_conv_b, m_lru_wa, m_lru_ba, m_lru_wx, m_lru_bx, m_lru_lambda, m_q_norm, m_w_uq, m_kv_norm, m_w_ukv, m_w_out, m_norm_mlp, m_w_up, m_w_down, m_norm_final, v_norm_mix, v_w_in, v_conv_w, v_conv_b, v_lru_wa, v_lru_ba, v_lru_wx, v_lru_bx, v_lru_lambda, v_q_norm, v_w_uq, v_kv_norm, v_w_ukv, v_w_out, v_norm_mlp, v_w_up, v_w_down, v_norm_final):
    W = dict(norm_mix=norm_mix, w_in=w_in, conv_w=conv_w, conv_b=conv_b, lru_wa=lru_wa, lru_ba=lru_ba, lru_wx=lru_wx,
             lru_bx=lru_bx, lru_lambda=lru_lambda, q_norm=q_norm, w_uq=w_uq, kv_norm=kv_norm, w_ukv=w_ukv, w_out=w_out,
             norm_mlp=norm_mlp, w_up=w_up, w_down=w_down, norm_final=norm_final)
    M = dict(norm_mix=m_norm_mix, w_in=m_w_in, conv_w=m_conv_w, conv_b=m_conv_b, lru_wa=m_lru_wa, lru_ba=m_lru_ba,
             lru_wx=m_lru_wx, lru_bx=m_lru_bx, lru_lambda=m_lru_lambda, q_norm=m_q_norm, w_uq=m_w_uq, kv_norm=m_kv_norm,
             w_ukv=m_w_ukv, w_out=m_w_out, norm_mlp=m_norm_mlp, w_up=m_w_up, w_down=m_w_down, norm_final=m_norm_final)
    V = dict(norm_mix=v_norm_mix, w_in=v_w_in, conv_w=v_conv_w, conv_b=v_conv_b, lru_wa=v_lru_wa, lru_ba=v_lru_ba,
             lru_wx=v_lru_wx, lru_bx=v_lru_bx, lru_lambda=v_lru_lambda, q_norm=v_q_norm, w_uq=v_w_uq, kv_norm=v_kv_norm,
             w_ukv=v_w_ukv, w_out=v_w_out, norm_mlp=v_norm_mlp, w_up=v_w_up, w_down=v_w_down, norm_final=v_norm_final)
    me = 4 * lax.axis_index("x") + 2 * lax.axis_index("y") + lax.axis_index("c")

    first, later = ("w_in", "w_uq", "w_ukv"), ("w_out", "w_up", "w_down")
    got = _all_gather([W[n].astype(BF16) for n in first] + [_pad_rows(conv_w, 8)], name="gather_weights")
    wts = {"w_in": _cols_from_shards(got[0]), "w_uq": _cols_from_shards(got[1]), "w_ukv": _cols_from_shards(got[2])}
    w_send, w_recv, w_src, w_land, zeros = _push_start([W[n].astype(BF16) for n in later], name="gather_later_start",
                                                       slab_per_peer=False)
    small = {n: W[n] for n in SMALL}
    small["conv_w"] = _cols_from_shards(got[3][:, :CONV_WIDTH])
    small["norm_mix"] = norm_mix + zeros[0, 0]

    def with_own_slab(land, mine):
        return lax.dynamic_update_slice(land, mine, (me, 0, 0))

    def weights_later(after):
        srcs, lands = _push_wait(w_send, w_recv, w_src, w_land, after, name="gather_later_wait", slab_per_peer=False)
        w_out_g, w_up_g, w_down_g = [with_own_slab(l, s[None]) for l, s in zip(lands, srcs)]
        return w_out_g.reshape(-1, D_MODEL), _cols_from_shards(w_up_g), w_down_g.reshape(-1, D_MODEL)

    sent = {}
    G, Dl, NM, NV = {}, {}, {}, {}

    def finish(group, names, after):
        s_sems, r_sems, srcs, lands, _ = sent[group]
        srcs, lands = _push_wait(s_sems, r_sems, srcs, lands, after, name="exchange_" + group + "_wait",
                                 slab_per_peer=True)
        for n, src, land in zip(names, srcs, lands):
            parts = with_own_slab(land, lax.dynamic_slice(src, (me, 0, 0), (1, *src.shape[1:])))
            G[n], Dl[n], NM[n], NV[n] = _adamw(W[n], M[n], V[n], parts, tr=ADAM_TILE_ROWS[n], name="adamw_" + n)

    def send(group, parts):
        sent[group] = _push_start(parts, name="exchange_" + group + "_start", slab_per_peer=True)
        zeros = sent[group][4]
        if group == "late":
            finish("early", later, zeros)
            zeros = zeros + 0.0 * (Dl["w_out"][0:8, 0:LANES] + Dl["w_up"][0:8, 0:LANES] + Dl["w_down"][0:8, 0:LANES])
        return zeros

    loss_part, grad_x, g_small = _local_step(x[0], loss_target[0], wts, small,
                                              {"weights_later": weights_later, "send": send})
    finish("late", first, grad_x)

    conv_rows = _pad_rows(jnp.transpose(g_small["conv_w"].reshape(CONV_WIDTH, N_DEV, LANES), (1, 0, 2)), 8)
    loss_rows = jnp.zeros((8, LANES), F32).at[0, 0].set(loss_part)
    as_sent = lambda n: g_small[n].astype(BF16) if n in ("lru_wa", "lru_wx") else g_small[n]
    gathered = _all_gather([_pad_rows(_rows(as_sent(n)), 8) for n in SMALL]
                           + [conv_rows.reshape(CONV_ROWS, LANES), loss_rows], name="gather_small")
    k = len(SMALL)
    outs = _adamw_many([_rows(W[n]) for n in SMALL], [_rows(M[n]) for n in SMALL], [_rows(V[n]) for n in SMALL],
                       gathered[:k], gathered[k:], name="adamw_small")
    for j, n in enumerate(SMALL):
        for out, o in zip((G, Dl, NM, NV), outs[4 * j:4 * j + 4]):
            out[n] = o.reshape(W[n].shape)
    conv_sum, loss_sum = outs[4 * k:]
    loss = loss_sum[0, 0]

    g_conv = lax.dynamic_slice(conv_sum, (me * 8, 0), (8, LANES))
    conv_out = _adamw(_pad_rows(conv_w, 8), _pad_rows(m_conv_w, 8), _pad_rows(v_conv_w, 8), g_conv[None], tr=8,
                      name="adamw_conv_w")
    for out, pk in zip((G, Dl, NM, NV), conv_out):
        out["conv_w"] = pk[:CONV_WIDTH]
    return (loss, grad_x[None], *[G[n] for n in WEIGHTS], *[Dl[n] for n in WEIGHTS], *[NM[n] for n in WEIGHTS],
            *[NV[n] for n in WEIGHTS])
```
